```python
import math
import jax
import jax.numpy as jnp
from jax import lax
import numpy as np

D_MODEL = 1024
BATCH = 8
SEQ = 8192
DEPTH = 2

GRID_W = 64
HEAD_DIM = 64
N_BRANCH = 4
BRANCH_W = 256
Q_BLOCK = 128
ROPE_THETA = 10000.0
EPS = 1e-6
NEG_INF = -1e30

MLA_HEADS = 4
MLA_Q_LORA = 256
MLA_KV_LORA = 128
MLA_NOPE = 64
MLA_ROPE = 32
MLA_V = 64

GQA_HEADS = 4
GQA_KV_HEADS = 2

DIL_PATTERNS = ((128, 1), (512, 4), (2048, 16))
DIL_GROUPS = 3
DIL_HEADS = 4

WIN_HEADS = 4
WIN_KV_HEADS = 2
WIN_HALF = 128
WIN_BLOCK = 128

T5_BUCKETS = 32
T5_MAX_DIST = 1024
T5_HEADS = DIL_GROUPS * DIL_HEADS + WIN_HEADS

IN_WIDTHS = (
    MLA_Q_LORA, MLA_KV_LORA, MLA_ROPE,
    GQA_HEADS * HEAD_DIM, GQA_KV_HEADS * HEAD_DIM, GQA_KV_HEADS * HEAD_DIM,
    DIL_GROUPS * DIL_HEADS * HEAD_DIM, DIL_GROUPS * DIL_HEADS * HEAD_DIM, DIL_GROUPS * DIL_HEADS * HEAD_DIM,
    WIN_HEADS * HEAD_DIM, WIN_KV_HEADS * HEAD_DIM, WIN_KV_HEADS * HEAD_DIM,
    N_BRANCH * BRANCH_W,
    N_BRANCH * D_MODEL,
)
IN_WIDTH = sum(IN_WIDTHS)

kernel_name = "hybrid_gated_multi_mixer_encoder"


def rms_norm(x, g):
    xf = x.astype(jnp.float32)
    y = xf * lax.rsqrt(jnp.mean(xf * xf, axis=-1, keepdims=True) + EPS)
    return (y * g.astype(jnp.float32)).astype(x.dtype)


def rope_angles(pos, dim):
    inv = ROPE_THETA ** (-jnp.arange(0, dim, 2, dtype=jnp.float32) / dim)
    return pos.astype(jnp.float32)[:, None] * inv[None, :]


def apply_rope(x, ang):
    half = x.shape[-1] // 2
    xf = x.astype(jnp.float32)
    cos = jnp.cos(ang)[None, :, None, :]
    sin = jnp.sin(ang)[None, :, None, :]
    x1, x2 = xf[..., :half], xf[..., half:]
    return jnp.concatenate([x1 * cos - x2 * sin, x1 * sin + x2 * cos], axis=-1).astype(x.dtype)


def axial_rope(x, ang_row, ang_col):
    half = x.shape[-1] // 2
    return jnp.concatenate([apply_rope(x[..., :half], ang_row), apply_rope(x[..., half:], ang_col)], axis=-1)


def t5_bucket(rel):
    nb = T5_BUCKETS // 2
    max_exact = nb // 2
    n = jnp.abs(rel)
    nf = jnp.maximum(n, 1).astype(jnp.float32)
    large = max_exact + (jnp.log(nf / max_exact) / math.log(T5_MAX_DIST / max_exact)
                         * (nb - max_exact)).astype(jnp.int32)
    large = jnp.minimum(large, nb - 1)
    return jnp.where(rel > 0, nb, 0) + jnp.where(n < max_exact, n, large)


def band_t5_bias(table, block, stride, head_lo, hk, g):
    offs = jnp.arange(3 * block)[None, :] - block - jnp.arange(block)[:, None]
    bias = table[t5_bucket(offs * stride)][..., head_lo:head_lo + hk * g]
    return jnp.transpose(bias, (2, 0, 1)).reshape(hk, g, block, 3 * block).astype(jnp.float32)


def dense_attention(q, k, v):
    b, s, hk, g, d = q.shape
    scale = d ** -0.5
    nblk = s // Q_BLOCK
    qb = jnp.moveaxis(q.reshape(b, nblk, Q_BLOCK, hk, g, d), 1, 0)

    def attend(q_blk):
        logits = jnp.einsum("bqkgd,bskd->bkgqs", q_blk, k).astype(jnp.float32) * scale
        p = jax.nn.softmax(logits, axis=-1).astype(v.dtype)
        return jnp.einsum("bkgqs,bskd->bqkgd", p, v)

    out = lax.map(attend, qb)
    return jnp.moveaxis(out, 0, 1).reshape(b, s, hk, g, v.shape[-1])


def banded_attention(q, k, v, half_window, block, bias, sink):
    b, L, hk, g, d = q.shape
    dv = v.shape[-1]
    scale = d ** -0.5
    nblk = -(-L // block)
    lp = nblk * block
    pad = lp - L
    qp = jnp.pad(q, ((0, 0), (0, pad), (0, 0), (0, 0), (0, 0)))
    kp = jnp.pad(k, ((0, 0), (block, block + pad), (0, 0), (0, 0)))
    vp = jnp.pad(v, ((0, 0), (block, block + pad), (0, 0), (0, 0)))
    valid = jnp.pad(jnp.ones((L,), dtype=bool), (block, block + pad)).reshape(nblk + 2, block)
    kb = kp.reshape(b, nblk + 2, block, hk, d)
    vb = vp.reshape(b, nblk + 2, block, hk, dv)
    kband = jnp.concatenate([kb[:, :-2], kb[:, 1:-1], kb[:, 2:]], axis=2)
    vband = jnp.concatenate([vb[:, :-2], vb[:, 1:-1], vb[:, 2:]], axis=2)
    kvalid = jnp.concatenate([valid[:-2], valid[1:-1], valid[2:]], axis=1)
    offs = jnp.arange(3 * block)[None, :] - block - jnp.arange(block)[:, None]
    mask = (jnp.abs(offs) <= half_window)[None, :, :] & kvalid[:, None, :]
    qb = qp.reshape(b, nblk, block, hk, g, d)
    logits = jnp.einsum("bnqkgd,bnskd->bnkgqs", qb, kband).astype(jnp.float32) * scale + bias
    logits = jnp.where(mask[None, :, None, None], logits, NEG_INF)
    m = jnp.max(logits, axis=-1, keepdims=True)
    if sink is not None:
        sk = sink.astype(jnp.float32)[None, None, :, :, None, None]
        m = jnp.maximum(m, sk)
        e = jnp.exp(logits - m)
        ssum = jnp.sum(e, axis=-1, keepdims=True) + jnp.exp(sk - m)
    else:
        e = jnp.exp(logits - m)
        ssum = jnp.sum(e, axis=-1, keepdims=True)
    p = (e / ssum).astype(v.dtype)
    out = jnp.einsum("bnkgqs,bnskd->bnqkgd", p, vband).reshape(b, lp, hk, g, dv)[:, :L]
    lse = (m + jnp.log(ssum))[..., 0]
    lse = jnp.transpose(lse, (0, 1, 4, 2, 3)).reshape(b, lp, hk, g)[:, :L]
    return out, lse


def dilated_group(q, k, v, dilation, half_steps, bias):
    b, s, h, d = q.shape
    L = s // dilation

    def to_sub(t):
        return t.reshape(b, L, dilation, h, d).transpose(0, 2, 1, 3, 4).reshape(b * dilation, L, h, d)

    out, lse = banded_attention(to_sub(q)[:, :, :, None, :], to_sub(k), to_sub(v),
                                half_steps, half_steps, bias, None)
    out = out[:, :, :, 0].reshape(b, dilation, L, h, d).transpose(0, 2, 1, 3, 4).reshape(b, s, h, d)
    lse = lse[..., 0].reshape(b, dilation, L, h).transpose(0, 2, 1, 3).reshape(b, s, h)
    return out, lse


def _fwd_setup_inputs(seed: int = 0) -> dict:
    key = jax.random.key(seed)
    ks = jax.random.split(key, 16)
    f32 = jnp.float32

    def nrm(k, shape, scale):
        return jax.random.normal(k, shape, f32) * scale

    def gain(k, shape):
        return 1.0 + 0.05 * jax.random.normal(k, shape, f32)

    return {
        "x": nrm(ks[0], (BATCH, SEQ, D_MODEL), 1.0),
        "norm_g": gain(ks[1], (DEPTH, D_MODEL)),
        "w_in": nrm(ks[2], (DEPTH, D_MODEL, IN_WIDTH), D_MODEL ** -0.5),
        "mla_q_norm_g": gain(ks[3], (DEPTH, MLA_Q_LORA)),
        "mla_kv_norm_g": gain(ks[4], (DEPTH, MLA_KV_LORA)),
        "w_mla_q_up": nrm(ks[5], (DEPTH, MLA_Q_LORA, MLA_HEADS * (MLA_NOPE + MLA_ROPE)), MLA_Q_LORA ** -0.5),
        "w_mla_kv_up": nrm(ks[6], (DEPTH, MLA_KV_LORA, MLA_HEADS * (MLA_NOPE + MLA_V)), MLA_KV_LORA ** -0.5),
        "gqa_q_norm_g": gain(ks[7], (DEPTH, HEAD_DIM)),
        "gqa_k_norm_g": gain(ks[8], (DEPTH, HEAD_DIM)),
        "win_sink": nrm(ks[9], (DEPTH, WIN_HEADS), 0.5),
        "t5_table": nrm(ks[10], (T5_BUCKETS, T5_HEADS), 0.2),
        "w_branch": nrm(ks[11], (DEPTH, N_BRANCH, BRANCH_W, D_MODEL), BRANCH_W ** -0.5),
        "w_out": nrm(ks[12], (DEPTH, D_MODEL, D_MODEL), D_MODEL ** -0.5),
        "final_norm_g": gain(ks[13], (D_MODEL,)),
    }


def _fwd_reference(x, norm_g, w_in, mla_q_norm_g, mla_kv_norm_g, w_mla_q_up, w_mla_kv_up,
              gqa_q_norm_g, gqa_k_norm_g, win_sink, t5_table, w_branch, w_out, final_norm_g):
    b, s, _ = x.shape
    rows = s // GRID_W
    pos = jnp.arange(s, dtype=jnp.int32)
    row_idx = jnp.repeat(jnp.arange(rows, dtype=jnp.int32), GRID_W)
    col_idx = jnp.tile(jnp.arange(GRID_W, dtype=jnp.int32), rows)
    ang_1d = rope_angles(pos, MLA_ROPE)
    ang_row = rope_angles(row_idx, HEAD_DIM // 2)
    ang_col = rope_angles(col_idx, HEAD_DIM // 2)

    dil_bias = [band_t5_bias(t5_table, w // (2 * r), r, gi * DIL_HEADS, DIL_HEADS, 1)
                for gi, (w, r) in enumerate(DIL_PATTERNS)]
    g_win = WIN_HEADS // WIN_KV_HEADS
    win_bias = band_t5_bias(t5_table, WIN_BLOCK, 1, DIL_GROUPS * DIL_HEADS, WIN_KV_HEADS, g_win)
    split_at = np.cumsum(IN_WIDTHS)[:-1].tolist()
    g_gqa = GQA_HEADS // GQA_KV_HEADS

    for l in range(DEPTH):
        xn = rms_norm(x, norm_g[l])
        proj = xn @ w_in[l]
        (a_q, a_kv, a_kr, b_q, b_k, b_v, c_q, c_k, c_v,
         d_q, d_k, d_v, gate_path, merge_logits) = jnp.split(proj, split_at, axis=-1)

        cq = rms_norm(a_q, mla_q_norm_g[l])
        qa = (cq @ w_mla_q_up[l]).reshape(b, s, MLA_HEADS, MLA_NOPE + MLA_ROPE)
        q_a = jnp.concatenate([qa[..., :MLA_NOPE], apply_rope(qa[..., MLA_NOPE:], ang_1d)], axis=-1)
        ckv = rms_norm(a_kv, mla_kv_norm_g[l])
        kv = (ckv @ w_mla_kv_up[l]).reshape(b, s, MLA_HEADS, MLA_NOPE + MLA_V)
        k_rope = apply_rope(a_kr[:, :, None, :], ang_1d)
        k_a = jnp.concatenate([kv[..., :MLA_NOPE],
                               jnp.broadcast_to(k_rope, (b, s, MLA_HEADS, MLA_ROPE))], axis=-1)
        y_a = dense_attention(q_a[:, :, :, None, :], k_a, kv[..., MLA_NOPE:]).reshape(b, s, MLA_HEADS * MLA_V)

        q_b = axial_rope(rms_norm(b_q.reshape(b, s, GQA_HEADS, HEAD_DIM), gqa_q_norm_g[l]), ang_row, ang_col)
        k_b = axial_rope(rms_norm(b_k.reshape(b, s, GQA_KV_HEADS, HEAD_DIM), gqa_k_norm_g[l]), ang_row, ang_col)
        y_b = dense_attention(q_b.reshape(b, s, GQA_KV_HEADS, g_gqa, HEAD_DIM), k_b,
                              b_v.reshape(b, s, GQA_KV_HEADS, HEAD_DIM)).reshape(b, s, GQA_HEADS * HEAD_DIM)

        cq5 = c_q.reshape(b, s, DIL_GROUPS, DIL_HEADS, HEAD_DIM)
        ck5 = c_k.reshape(b, s, DIL_GROUPS, DIL_HEADS, HEAD_DIM)
        cv5 = c_v.reshape(b, s, DIL_GROUPS, DIL_HEADS, HEAD_DIM)
        outs, lses = [], []
        for gi, (w, r) in enumerate(DIL_PATTERNS):
            o, lse = dilated_group(cq5[:, :, gi], ck5[:, :, gi], cv5[:, :, gi], r, w // (2 * r), dil_bias[gi])
            outs.append(o)
            lses.append(lse)
        alpha = jax.nn.softmax(jnp.stack(lses, axis=0), axis=0)
        y_c = jnp.sum(alpha[..., None] * jnp.stack(outs, axis=0).astype(jnp.float32), axis=0)
        y_c = y_c.astype(x.dtype).reshape(b, s, DIL_HEADS * HEAD_DIM)

        y_d, _ = banded_attention(d_q.reshape(b, s, WIN_KV_HEADS, g_win, HEAD_DIM),
                                  d_k.reshape(b, s, WIN_KV_HEADS, HEAD_DIM),
                                  d_v.reshape(b, s, WIN_KV_HEADS, HEAD_DIM),
                                  WIN_HALF, WIN_BLOCK, win_bias,
                                  win_sink[l].reshape(WIN_KV_HEADS, g_win))
        y_d = y_d.reshape(b, s, WIN_HEADS * HEAD_DIM)

        y = jnp.concatenate([y_a, y_b, y_c, y_d], axis=-1) * jax.nn.silu(gate_path)
        branch = jnp.einsum("bsnc,ncd->bsnd", y.reshape(b, s, N_BRANCH, BRANCH_W), w_branch[l])
        gates = jax.nn.sigmoid(merge_logits.reshape(b, s, N_BRANCH, D_MODEL))
        x = x + jnp.sum(gates * branch, axis=2) @ w_out[l]

    return rms_norm(x, final_norm_g)


import jax as _jax
import jax.numpy as _jnp

TWIN_FORMAT = 'train_step'
FWD_PARAMS = ['x', 'norm_g', 'w_in', 'mla_q_norm_g', 'mla_kv_norm_g', 'w_mla_q_up', 'w_mla_kv_up', 'gqa_q_norm_g', 'gqa_k_norm_g', 'win_sink', 't5_table', 'w_branch', 'w_out', 'final_norm_g']
TWIN_WEIGHTS = ['norm_g', 'w_in', 'mla_q_norm_g', 'mla_kv_norm_g', 'w_mla_q_up', 'w_mla_kv_up', 'gqa_q_norm_g', 'gqa_k_norm_g', 'win_sink', 't5_table', 'w_branch', 'w_out', 'final_norm_g']
TWIN_DIFF_INPUT = 'x'
TWIN_INPUTS = ['x', 'norm_g', 'w_in', 'mla_q_norm_g', 'mla_kv_norm_g', 'w_mla_q_up', 'w_mla_kv_up', 'gqa_q_norm_g', 'gqa_k_norm_g', 'win_sink', 't5_table', 'w_branch', 'w_out', 'final_norm_g', 'loss_target', 'm_norm_g', 'm_w_in', 'm_mla_q_norm_g', 'm_mla_kv_norm_g', 'm_w_mla_q_up', 'm_w_mla_kv_up', 'm_gqa_q_norm_g', 'm_gqa_k_norm_g', 'm_win_sink', 'm_t5_table', 'm_w_branch', 'm_w_out', 'm_final_norm_g', 'v_norm_g', 'v_w_in', 'v_mla_q_norm_g', 'v_mla_kv_norm_g', 'v_w_mla_q_up', 'v_w_mla_kv_up', 'v_gqa_q_norm_g', 'v_gqa_k_norm_g', 'v_win_sink', 'v_t5_table', 'v_w_branch', 'v_w_out', 'v_final_norm_g']
TWIN_OUTPUTS = ['loss', 'grad_x', 'grad_norm_g', 'grad_w_in', 'grad_mla_q_norm_g', 'grad_mla_kv_norm_g', 'grad_w_mla_q_up', 'grad_w_mla_kv_up', 'grad_gqa_q_norm_g', 'grad_gqa_k_norm_g', 'grad_win_sink', 'grad_t5_table', 'grad_w_branch', 'grad_w_out', 'grad_final_norm_g', 'delta_norm_g', 'delta_w_in', 'delta_mla_q_norm_g', 'delta_mla_kv_norm_g', 'delta_w_mla_q_up', 'delta_w_mla_kv_up', 'delta_gqa_q_norm_g', 'delta_gqa_k_norm_g', 'delta_win_sink', 'delta_t5_table', 'delta_w_branch', 'delta_w_out', 'delta_final_norm_g', 'new_m_norm_g', 'new_m_w_in', 'new_m_mla_q_norm_g', 'new_m_mla_kv_norm_g', 'new_m_w_mla_q_up', 'new_m_w_mla_kv_up', 'new_m_gqa_q_norm_g', 'new_m_gqa_k_norm_g', 'new_m_win_sink', 'new_m_t5_table', 'new_m_w_branch', 'new_m_w_out', 'new_m_final_norm_g', 'new_v_norm_g', 'new_v_w_in', 'new_v_mla_q_norm_g', 'new_v_mla_kv_norm_g', 'new_v_w_mla_q_up', 'new_v_w_mla_kv_up', 'new_v_gqa_q_norm_g', 'new_v_gqa_k_norm_g', 'new_v_win_sink', 'new_v_t5_table', 'new_v_w_branch', 'new_v_w_out', 'new_v_final_norm_g']
TWIN_LEAF_KINDS = {'loss': 'loss', 'grad_x': 'grad_x', 'grad_norm_g': 'grad_w', 'grad_w_in': 'grad_w', 'grad_mla_q_norm_g': 'grad_w', 'grad_mla_kv_norm_g': 'grad_w', 'grad_w_mla_q_up': 'grad_w', 'grad_w_mla_kv_up': 'grad_w', 'grad_gqa_q_norm_g': 'grad_w', 'grad_gqa_k_norm_g': 'grad_w', 'grad_win_sink': 'grad_w', 'grad_t5_table': 'grad_w', 'grad_w_branch': 'grad_w', 'grad_w_out': 'grad_w', 'grad_final_norm_g': 'grad_w', 'delta_norm_g': 'delta_w', 'delta_w_in': 'delta_w', 'delta_mla_q_norm_g': 'delta_w', 'delta_mla_kv_norm_g': 'delta_w', 'delta_w_mla_q_up': 'delta_w', 'delta_w_mla_kv_up': 'delta_w', 'delta_gqa_q_norm_g': 'delta_w', 'delta_gqa_k_norm_g': 'delta_w', 'delta_win_sink': 'delta_w', 'delta_t5_table': 'delta_w', 'delta_w_branch': 'delta_w', 'delta_w_out': 'delta_w', 'delta_final_norm_g': 'delta_w', 'new_m_norm_g': 'new_m', 'new_m_w_in': 'new_m', 'new_m_mla_q_norm_g': 'new_m', 'new_m_mla_kv_norm_g': 'new_m', 'new_m_w_mla_q_up': 'new_m', 'new_m_w_mla_kv_up': 'new_m', 'new_m_gqa_q_norm_g': 'new_m', 'new_m_gqa_k_norm_g': 'new_m', 'new_m_win_sink': 'new_m', 'new_m_t5_table': 'new_m', 'new_m_w_branch': 'new_m', 'new_m_w_out': 'new_m', 'new_m_final_norm_g': 'new_m', 'new_v_norm_g': 'new_v', 'new_v_w_in': 'new_v', 'new_v_mla_q_norm_g': 'new_v', 'new_v_mla_kv_norm_g': 'new_v', 'new_v_w_mla_q_up': 'new_v', 'new_v_w_mla_kv_up': 'new_v', 'new_v_gqa_q_norm_g': 'new_v', 'new_v_gqa_k_norm_g': 'new_v', 'new_v_win_sink': 'new_v', 'new_v_t5_table': 'new_v', 'new_v_w_branch': 'new_v', 'new_v_w_out': 'new_v', 'new_v_final_norm_g': 'new_v'}


def _forward(args):
    return _fwd_reference(*[args[k] for k in FWD_PARAMS])


def _output_shape():
    def fwd():
        inp = _fwd_setup_inputs(0)
        return _fwd_reference(*[inp[k] for k in FWD_PARAMS])
    out = _jax.eval_shape(fwd)
    return out.shape, out.dtype

N_MICROBATCH = 1
ADAM_LR = 0.001
ADAM_B1 = 0.9
ADAM_B2 = 0.999
ADAM_EPS = 1e-08
ADAM_WD = 0.01
ADAM_STEP = 10
PER_EXAMPLE_BATCH_AXIS = {'x': 0, 'loss_target': 0}
SHARED_INPUTS = []
_WEIGHT_DTYPES = {'norm_g': _jnp.float32, 'w_in': _jnp.float32, 'mla_q_norm_g': _jnp.float32, 'mla_kv_norm_g': _jnp.float32, 'w_mla_q_up': _jnp.float32, 'w_mla_kv_up': _jnp.float32, 'gqa_q_norm_g': _jnp.float32, 'gqa_k_norm_g': _jnp.float32, 'win_sink': _jnp.float32, 't5_table': _jnp.float32, 'w_branch': _jnp.float32, 'w_out': _jnp.float32, 'final_norm_g': _jnp.float32}
MOMENT_SCALE = {'norm_g': 4.035542e-02, 'w_in': 1.323780e-02, 'mla_q_norm_g': 2.104568e-02, 'mla_kv_norm_g': 4.740817e-02, 'w_mla_q_up': 1.650102e-02, 'w_mla_kv_up': 1.935979e-02, 'gqa_q_norm_g': 2.832710e-02, 'gqa_k_norm_g': 2.728231e-02, 'win_sink': 4.421336e-04, 't5_table': 2.495820e-02, 'w_branch': 9.829153e-03, 'w_out': 1.961450e-02, 'final_norm_g': 6.409900e+01}


def _to_microbatches(a, axis):
    t = _jnp.moveaxis(a, axis, 0)
    t = t.reshape((N_MICROBATCH, t.shape[0] // N_MICROBATCH) + t.shape[1:])
    return _jnp.moveaxis(t, 1, axis + 1)


def setup_inputs(seed: int = 0) -> dict:
    inp = _fwd_setup_inputs(seed)
    key = _jax.random.fold_in(_jax.random.key(seed), 7919)
    shape, _ = _output_shape()
    out = dict(inp)
    out["loss_target"] = _jax.random.normal(_jax.random.fold_in(key, 0), shape, _jnp.float32)
    for i, name in enumerate(TWIN_WEIGHTS):
        w = inp[name].astype(_jnp.float32)
        if MOMENT_SCALE is None:
            s = _jnp.sqrt(_jnp.mean(_jnp.square(w)) + 1e-30)
        else:
            s = MOMENT_SCALE[name]
        km, kv = _jax.random.split(_jax.random.fold_in(key, i + 1))
        out[name] = w
        out["m_" + name] = s * _jax.random.normal(km, w.shape, _jnp.float32)
        out["v_" + name] = (s * s) * _jax.random.uniform(kv, w.shape, _jnp.float32, 0.5, 1.5)
    if N_MICROBATCH > 1:
        for name, axis in PER_EXAMPLE_BATCH_AXIS.items():
            out[name] = _to_microbatches(out[name], axis)
    return {'x': out['x'], 'norm_g': out['norm_g'], 'w_in': out['w_in'], 'mla_q_norm_g': out['mla_q_norm_g'], 'mla_kv_norm_g': out['mla_kv_norm_g'], 'w_mla_q_up': out['w_mla_q_up'], 'w_mla_kv_up': out['w_mla_kv_up'], 'gqa_q_norm_g': out['gqa_q_norm_g'], 'gqa_k_norm_g': out['gqa_k_norm_g'], 'win_sink': out['win_sink'], 't5_table': out['t5_table'], 'w_branch': out['w_branch'], 'w_out': out['w_out'], 'final_norm_g': out['final_norm_g'], 'loss_target': out['loss_target'], 'm_norm_g': out['m_norm_g'], 'm_w_in': out['m_w_in'], 'm_mla_q_norm_g': out['m_mla_q_norm_g'], 'm_mla_kv_norm_g': out['m_mla_kv_norm_g'], 'm_w_mla_q_up': out['m_w_mla_q_up'], 'm_w_mla_kv_up': out['m_w_mla_kv_up'], 'm_gqa_q_norm_g': out['m_gqa_q_norm_g'], 'm_gqa_k_norm_g': out['m_gqa_k_norm_g'], 'm_win_sink': out['m_win_sink'], 'm_t5_table': out['m_t5_table'], 'm_w_branch': out['m_w_branch'], 'm_w_out': out['m_w_out'], 'm_final_norm_g': out['m_final_norm_g'], 'v_norm_g': out['v_norm_g'], 'v_w_in': out['v_w_in'], 'v_mla_q_norm_g': out['v_mla_q_norm_g'], 'v_mla_kv_norm_g': out['v_mla_kv_norm_g'], 'v_w_mla_q_up': out['v_w_mla_q_up'], 'v_w_mla_kv_up': out['v_w_mla_kv_up'], 'v_gqa_q_norm_g': out['v_gqa_q_norm_g'], 'v_gqa_k_norm_g': out['v_gqa_k_norm_g'], 'v_win_sink': out['v_win_sink'], 'v_t5_table': out['v_t5_table'], 'v_w_branch': out['v_w_branch'], 'v_w_out': out['v_w_out'], 'v_final_norm_g': out['v_final_norm_g']}


def _loss(weights, diff, rest, loss_target):
    with _jax.named_scope("forward"):
        args = {**rest, TWIN_DIFF_INPUT: diff, **{k: w.astype(_WEIGHT_DTYPES[k]) for k, w in weights.items()}}
        y = _forward(args)
    with _jax.named_scope("loss_head"):
        err = _jnp.square(y.astype(_jnp.float32) - loss_target)
        return 0.5 * _jnp.sum(_jnp.mean(err, axis=-1)) if err.ndim else 0.5 * err


def _adamw(w, g, m, v):
    m = ADAM_B1 * m + (1.0 - ADAM_B1) * g
    v = ADAM_B2 * v + (1.0 - ADAM_B2) * _jnp.square(g)
    m_hat = m / (1.0 - ADAM_B1 ** ADAM_STEP)
    v_hat = v / (1.0 - ADAM_B2 ** ADAM_STEP)
    delta = -ADAM_LR * (m_hat / (_jnp.sqrt(v_hat) + ADAM_EPS) + ADAM_WD * w)
    return delta, m, v


def reference(x, norm_g, w_in, mla_q_norm_g, mla_kv_norm_g, w_mla_q_up, w_mla_kv_up, gqa_q_norm_g, gqa_k_norm_g, win_sink, t5_table, w_branch, w_out, final_norm_g, loss_target, m_norm_g, m_w_in, m_mla_q_norm_g, m_mla_kv_norm_g, m_w_mla_q_up, m_w_mla_kv_up, m_gqa_q_norm_g, m_gqa_k_norm_g, m_win_sink, m_t5_table, m_w_branch, m_w_out, m_final_norm_g, v_norm_g, v_w_in, v_mla_q_norm_g, v_mla_kv_norm_g, v_w_mla_q_up, v_w_mla_kv_up, v_gqa_q_norm_g, v_gqa_k_norm_g, v_win_sink, v_t5_table, v_w_branch, v_w_out, v_final_norm_g):
    given = dict(x=x, norm_g=norm_g, w_in=w_in, mla_q_norm_g=mla_q_norm_g, mla_kv_norm_g=mla_kv_norm_g, w_mla_q_up=w_mla_q_up, w_mla_kv_up=w_mla_kv_up, gqa_q_norm_g=gqa_q_norm_g, gqa_k_norm_g=gqa_k_norm_g, win_sink=win_sink, t5_table=t5_table, w_branch=w_branch, w_out=w_out, final_norm_g=final_norm_g, loss_target=loss_target, m_norm_g=m_norm_g, m_w_in=m_w_in, m_mla_q_norm_g=m_mla_q_norm_g, m_mla_kv_norm_g=m_mla_kv_norm_g, m_w_mla_q_up=m_w_mla_q_up, m_w_mla_kv_up=m_w_mla_kv_up, m_gqa_q_norm_g=m_gqa_q_norm_g, m_gqa_k_norm_g=m_gqa_k_norm_g, m_win_sink=m_win_sink, m_t5_table=m_t5_table, m_w_branch=m_w_branch, m_w_out=m_w_out, m_final_norm_g=m_final_norm_g, v_norm_g=v_norm_g, v_w_in=v_w_in, v_mla_q_norm_g=v_mla_q_norm_g, v_mla_kv_norm_g=v_mla_kv_norm_g, v_w_mla_q_up=v_w_mla_q_up, v_w_mla_kv_up=v_w_mla_kv_up, v_gqa_q_norm_g=v_gqa_q_norm_g, v_gqa_k_norm_g=v_gqa_k_norm_g, v_win_sink=v_win_sink, v_t5_table=v_t5_table, v_w_branch=v_w_branch, v_w_out=v_w_out, v_final_norm_g=v_final_norm_g)
    weights = {n: given[n] for n in TWIN_WEIGHTS}
    shared = {n: given[n] for n in SHARED_INPUTS}
    per_example = {n: given[n] for n in ['x']}
    grad_fn = _jax.value_and_grad(_loss, argnums=(0, 1))

    def one_microbatch(ex, loss_target):
        ex = dict(ex)
        diff = ex.pop(TWIN_DIFF_INPUT)
        return grad_fn(weights, diff, {**shared, **ex}, loss_target)

    if N_MICROBATCH == 1:
        loss, (grad_w, grad_x) = one_microbatch(per_example, given["loss_target"])
    else:
        def body(carry, xs):
            loss_sum, grad_sum = carry
            l_k, (gw_k, gx_k) = one_microbatch(xs[0], xs[1])
            with _jax.named_scope("update"):
                return (loss_sum + l_k, _jax.tree.map(_jnp.add, grad_sum, gw_k)), gx_k

        init = (_jnp.zeros((), _jnp.float32), _jax.tree.map(_jnp.zeros_like, weights))
        (loss, grad_w), grad_x = _jax.lax.scan(body, init, (per_example, given["loss_target"]))
    with _jax.named_scope("update"):
        delta_w, new_m, new_v = {}, {}, {}
        for n in TWIN_WEIGHTS:
            delta_w[n], new_m[n], new_v[n] = _adamw(weights[n], grad_w[n], given["m_" + n], given["v_" + n])
    return (loss, grad_x, *[grad_w[n] for n in TWIN_WEIGHTS], *[delta_w[n] for n in TWIN_WEIGHTS],
            *[new_m[n] for n in TWIN_WEIGHTS], *[new_v[n] for n in TWIN_WEIGHTS])
```

```python
import functools
import math

import jax
import jax.numpy as jnp
import numpy as np
from jax import lax
from jax.experimental import pallas as pl
from jax.experimental.pallas import tpu as pltpu

F32 = jnp.float32
BF16 = jnp.bfloat16
N_DEV = 8
LANES = 128
V7X_VMEM_LIMIT = 56 * 1024 * 1024

EPS = 1e-6
NEG_INF = -1e30
ROPE_THETA = 10000.0
GRID_W = 64
HEAD_DIM = 64
N_BRANCH = 4
BRANCH_W = 256
MLA_HEADS, MLA_Q_LORA, MLA_KV_LORA, MLA_NOPE, MLA_ROPE, MLA_V = 4, 256, 128, 64, 32, 64
MLA_QK = MLA_NOPE + MLA_ROPE
GQA_HEADS, GQA_KV_HEADS = 4, 2
DIL_PATTERNS = ((128, 1), (512, 4), (2048, 16))
DIL_HEADS = 4
WIN_HEADS, WIN_KV_HEADS, WIN_HALF = 4, 2, 128
T5_BUCKETS, T5_MAX_DIST = 32, 1024
BAND_BLOCK = 128
ADAM_LR, ADAM_B1, ADAM_B2, ADAM_EPS, ADAM_WD, ADAM_STEP = 0.001, 0.9, 0.999, 1e-08, 0.01, 10


def _params(*sem):
    return pltpu.CompilerParams(dimension_semantics=sem, vmem_limit_bytes=V7X_VMEM_LIMIT)


def _pick(n, cands):
    for c in cands:
        if n % c == 0:
            return c
    return n


def _dot(a, b, ca, cb):
    return lax.dot_general(a.astype(BF16), b.astype(BF16), (((ca,), (cb,)), ((), ())), preferred_element_type=F32)


@jax.custom_vjp
def _bdot(a, b):
    return _dot(a, b, 1, 0)


def _bdot_fwd(a, b):
    return _dot(a, b, 1, 0), (a, b)


def _bdot_bwd(res, g):
    a, b = res
    return _dot(g, b, 1, 1), _dot(a, g, 0, 0)


_bdot.defvjp(_bdot_fwd, _bdot_bwd)


def _hdot(a, c):
    return lax.dot_general(a, c, (((1,), (0,)), ((), ())), precision=lax.Precision.HIGHEST, preferred_element_type=F32)


@functools.partial(jax.custom_vjp, nondiff_argnums=(1,))
def _lane_roll(x, shift):
    return pltpu.roll(x, shift, 1)


def _lane_roll_fwd(x, shift):
    return pltpu.roll(x, shift, 1), None


def _lane_roll_bwd(shift, _, g):
    return (pltpu.roll(g, g.shape[1] - shift, 1),)


_lane_roll.defvjp(_lane_roll_fwd, _lane_roll_bwd)


def _rope(x, cos_t, sin_t, half):
    w = x.shape[1]
    lane = lax.broadcasted_iota(jnp.int32, (1, w), 1)
    first = (lane % (2 * half)) < half
    partner = jnp.where(first, _lane_roll(x, w - half), _lane_roll(x, half))
    return x * cos_t + partner * sin_t


def _rms(x, g):
    return x * lax.rsqrt(jnp.mean(x * x, axis=-1, keepdims=True) + EPS) * g


def _row_spec(tr, w):
    return pl.BlockSpec((tr, w), lambda i: (i, 0))


def _full_spec(shape):
    nd = len(shape)
    return pl.BlockSpec(tuple(shape), lambda i: (0,) * nd)


def _rowop_fwd_call(name, f, tr, out_widths, rows, aux, params, consts):
    nr, na, npar, nc = len(rows), len(aux), len(params), len(consts)
    n_rows = rows[0].shape[0]

    def body(*refs):
        vals = [x[...] for x in refs[:nr + na + npar + nc]]
        r, a = vals[:nr], vals[nr:nr + na]
        p, c = vals[nr + na:nr + na + npar], vals[nr + na + npar:]
        for o_ref, o in zip(refs[nr + na + npar + nc:], f(r, a, p, c)):
            o_ref[...] = o

    return pl.pallas_call(
        body,
        grid=(n_rows // tr,),
        in_specs=[_row_spec(tr, t.shape[1]) for t in rows + aux] + [_full_spec(t.shape) for t in params + consts],
        out_specs=[_row_spec(tr, w) for w in out_widths],
        out_shape=[jax.ShapeDtypeStruct((n_rows, w), F32) for w in out_widths],
        name=name + "_fwd",
        compiler_params=_params("parallel"),
    )(*rows, *aux, *params, *consts)


def _rowop_bwd_call(name, f, tr, rows, aux, params, consts, douts):
    nr, na, npar, nc, nd = len(rows), len(aux), len(params), len(consts), len(douts)
    n_rows = rows[0].shape[0]
    n_in = nr + na + npar + nc

    def body(*refs):
        vals = [x[...] for x in refs[:n_in + nd]]
        r, a = vals[:nr], vals[nr:nr + na]
        p, c = vals[nr + na:nr + na + npar], vals[nr + na + npar:n_in]
        d = vals[n_in:]
        out_refs = refs[n_in + nd:]
        _, vjp = jax.vjp(lambda r_, p_: tuple(f(r_, a, p_, c)), r, p)
        dr, dp = vjp(tuple(d))
        for o_ref, o in zip(out_refs[:nr], dr):
            o_ref[...] = o

        @pl.when(pl.program_id(0) == 0)
        def _():
            for o_ref in out_refs[nr:]:
                o_ref[...] = jnp.zeros_like(o_ref)

        for o_ref, o in zip(out_refs[nr:], dp):
            o_ref[...] += o

    outs = pl.pallas_call(
        body,
        grid=(n_rows // tr,),
        in_specs=[_row_spec(tr, t.shape[1]) for t in rows + aux] + [_full_spec(t.shape) for t in params + consts]
        + [_row_spec(tr, t.shape[1]) for t in douts],
        out_specs=[_row_spec(tr, t.shape[1]) for t in rows] + [_full_spec(t.shape) for t in params],
        out_shape=[jax.ShapeDtypeStruct(t.shape, F32) for t in rows + params],
        name=name + "_bwd",
        compiler_params=_params("arbitrary"),
    )(*rows, *aux, *params, *consts, *douts)
    return list(outs[:nr]), list(outs[nr:])


def _rowop(name, f, tr, out_widths, rows, params, consts=(), aux=()):
    rows, aux, params, consts = list(rows), list(aux), list(params), list(consts)
    tr = _pick(rows[0].shape[0], (tr, 256, 128, 64, 32, 16, 8))

    @jax.custom_vjp
    def op(rows, params, aux, consts):
        return tuple(_rowop_fwd_call(name, f, tr, out_widths, rows, aux, params, consts))

    def op_fwd(rows, params, aux, consts):
        return op(rows, params, aux, consts), (rows, params, aux, consts)

    def op_bwd(res, douts):
        rows, params, aux, consts = res
        dr, dp = _rowop_bwd_call(name, f, tr, rows, aux, params, consts, list(douts))
        return dr, dp, [jnp.zeros_like(t) for t in aux], [jnp.zeros_like(t) for t in consts]

    op.defvjp(op_fwd, op_bwd)
    return op(rows, params, aux, consts)


def _mm(a, b, mode, name):
    if mode == "nn":
        (m, k), n = a.shape, b.shape[1]
    elif mode == "nt":
        (m, k), n = a.shape, b.shape[0]
    else:
        (k, m), n = a.shape, b.shape[1]
    tm = _pick(m, (512, 256, 128))
    tn = _pick(n, (1024, 896, 640, 512, 384, 256, 128))
    tk = _pick(k, (1024, 896, 640, 512, 384, 256, 128))
    if mode == "tn":
        tm, tk = _pick(m, (1024, 512, 256, 128)), _pick(k, (512, 256, 128))
    nk = k // tk

    def body(a_ref, b_ref, o_ref, acc_ref):
        kk = pl.program_id(2)
        if mode == "nn":
            part = _dot(a_ref[...], b_ref[...], 1, 0)
        elif mode == "nt":
            part = _dot(a_ref[...], b_ref[...], 1, 1)
        else:
            part = _dot(a_ref[...], b_ref[...], 0, 0)
        if nk == 1:
            o_ref[...] = part
        else:
            @pl.when(kk == 0)
            def _():
                acc_ref[...] = part

            @pl.when(kk > 0)
            def _():
                acc_ref[...] += part

            @pl.when(kk == nk - 1)
            def _():
                o_ref[...] = acc_ref[...]

    if mode == "nn":
        a_spec = pl.BlockSpec((tm, tk), lambda i, j, kk: (i, kk))
        b_spec = pl.BlockSpec((tk, tn), lambda i, j, kk: (kk, j))
    elif mode == "nt":
        a_spec = pl.BlockSpec((tm, tk), lambda i, j, kk: (i, kk))
        b_spec = pl.BlockSpec((tn, tk), lambda i, j, kk: (j, kk))
    else:
        a_spec = pl.BlockSpec((tk, tm), lambda i, j, kk: (kk, i))
        b_spec = pl.BlockSpec((tk, tn), lambda i, j, kk: (kk, j))
    return pl.pallas_call(
        body,
        grid=(m // tm, n // tn, nk),
        in_specs=[a_spec, b_spec],
        out_specs=pl.BlockSpec((tm, tn), lambda i, j, kk: (i, j)),
        out_shape=jax.ShapeDtypeStruct((m, n), F32),
        scratch_shapes=[pltpu.VMEM((tm, tn), F32)],
        name=name,
        compiler_params=_params("parallel", "parallel", "arbitrary"),
    )(a, b)


def _matmul(a, b, name):
    @jax.custom_vjp
    def op(a, b):
        return _mm(a, b, "nn", name + "_nn")

    def op_fwd(a, b):
        return op(a, b), (a, b)

    def op_bwd(res, g):
        a, b = res
        return _mm(g, b, "nt", name + "_nt"), _mm(a, g, "tn", name + "_tn")

    op.defvjp(op_fwd, op_bwd)
    return op(a, b)


def _dense_fwd_call(q, k, v, scale, name):
    n, sq, d = q.shape
    sk, dv = k.shape[1], v.shape[2]
    tq, tk = _pick(sq, (512, 256, 128)), _pick(sk, (1024, 512, 256, 128))
    nkb = sk // tk

    def body(q_ref, k_ref, v_ref, o_ref, lse_ref, m_s, l_s, acc_s):
        j = pl.program_id(2)

        @pl.when(j == 0)
        def _():
            m_s[...] = jnp.full_like(m_s, NEG_INF)
            l_s[...] = jnp.zeros_like(l_s)
            acc_s[...] = jnp.zeros_like(acc_s)

        s = _dot(q_ref[0], k_ref[0], 1, 1) * scale
        m_new = jnp.maximum(m_s[...], jnp.max(s, axis=1, keepdims=True))
        alpha = jnp.exp(m_s[...] - m_new)
        p = jnp.exp(s - m_new)
        l_s[...] = alpha * l_s[...] + jnp.sum(p, axis=1, keepdims=True)
        acc_s[...] = alpha * acc_s[...] + _dot(p, v_ref[0], 1, 0)
        m_s[...] = m_new

        @pl.when(j == nkb - 1)
        def _():
            o_ref[0] = acc_s[...] / l_s[...]
            lse_ref[0] = m_s[...] + jnp.log(l_s[...])

    return pl.pallas_call(
        body,
        grid=(n, sq // tq, nkb),
        in_specs=[
            pl.BlockSpec((1, tq, d), lambda h, i, j: (h, i, 0)),
            pl.BlockSpec((1, tk, d), lambda h, i, j: (h, j, 0)),
            pl.BlockSpec((1, tk, dv), lambda h, i, j: (h, j, 0)),
        ],
        out_specs=[
            pl.BlockSpec((1, tq, dv), lambda h, i, j: (h, i, 0)),
            pl.BlockSpec((1, tq, 1), lambda h, i, j: (h, i, 0)),
        ],
        out_shape=[jax.ShapeDtypeStruct((n, sq, dv), F32), jax.ShapeDtypeStruct((n, sq, 1), F32)],
        scratch_shapes=[pltpu.VMEM((tq, 1), F32), pltpu.VMEM((tq, 1), F32), pltpu.VMEM((tq, dv), F32)],
        name=name + "_fwd",
        compiler_params=_params("parallel", "parallel", "arbitrary"),
    )(q, k, v)


def _dense_bwd_call(q, k, v, o, lse, do, scale, name):
    n, sq, d = q.shape
    sk, dv = k.shape[1], v.shape[2]
    tq, tk = _pick(sq, (512, 256, 128)), _pick(sk, (512, 256, 128))

    def body(q_ref, k_ref, v_ref, o_ref, lse_ref, do_ref, dq_ref, dk_ref, dv_ref):
        j, i = pl.program_id(1), pl.program_id(2)
        qt, kt, vt, dot_ = q_ref[0], k_ref[0], v_ref[0], do_ref[0]
        s = _dot(qt, kt, 1, 1) * scale
        p = jnp.exp(s - lse_ref[0])
        dp = _dot(dot_, vt, 1, 1)
        delta = jnp.sum(dot_ * o_ref[0], axis=1, keepdims=True)
        ds = p * (dp - delta) * scale
        dv_part = _dot(p, dot_, 0, 0)
        dk_part = _dot(ds, qt, 0, 0)
        dq_part = _dot(ds, kt, 1, 0)
        rows = pl.ds(pl.multiple_of(i * tq, tq), tq)

        @pl.when(i == 0)
        def _():
            dk_ref[0] = dk_part
            dv_ref[0] = dv_part

        @pl.when(i > 0)
        def _():
            dk_ref[0] += dk_part
            dv_ref[0] += dv_part

        @pl.when(j == 0)
        def _():
            dq_ref[0, rows, :] = dq_part

        @pl.when(j > 0)
        def _():
            dq_ref[0, rows, :] += dq_part

    return pl.pallas_call(
        body,
        grid=(n, sk // tk, sq // tq),
        in_specs=[
            pl.BlockSpec((1, tq, d), lambda h, j, i: (h, i, 0)),
            pl.BlockSpec((1, tk, d), lambda h, j, i: (h, j, 0)),
            pl.BlockSpec((1, tk, dv), lambda h, j, i: (h, j, 0)),
            pl.BlockSpec((1, tq, dv), lambda h, j, i: (h, i, 0)),
            pl.BlockSpec((1, tq, 1), lambda h, j, i: (h, i, 0)),
            pl.BlockSpec((1, tq, dv), lambda h, j, i: (h, i, 0)),
        ],
        out_specs=[
            pl.BlockSpec((1, sq, d), lambda h, j, i: (h, 0, 0)),
            pl.BlockSpec((1, tk, d), lambda h, j, i: (h, j, 0)),
            pl.BlockSpec((1, tk, dv), lambda h, j, i: (h, j, 0)),
        ],
        out_shape=[
            jax.ShapeDtypeStruct((n, sq, d), F32),
            jax.ShapeDtypeStruct((n, sk, d), F32),
            jax.ShapeDtypeStruct((n, sk, dv), F32),
        ],
        name=name + "_bwd",
        compiler_params=_params("arbitrary", "arbitrary", "arbitrary"),
    )(q, k, v, o, lse, do)


def _dense_attention(q, k, v, scale, name):
    @jax.custom_vjp
    def op(q, k, v):
        return _dense_fwd_call(q, k, v, scale, name)[0]

    def op_fwd(q, k, v):
        o, lse = _dense_fwd_call(q, k, v, scale, name)
        return o, (q, k, v, o, lse)

    def op_bwd(res, do):
        q, k, v, o, lse = res
        return tuple(_dense_bwd_call(q, k, v, o, lse, do, scale, name))

    op.defvjp(op_fwd, op_bwd)
    return op(q, k, v)


def _band_mask(block_index, seq_len, half_window):
    b = BAND_BLOCK
    r = lax.broadcasted_iota(jnp.int32, (b, 3 * b), 0)
    c = lax.broadcasted_iota(jnp.int32, (b, 3 * b), 1)
    pos = (block_index - 1) * b + c
    return (jnp.abs(c - b - r) <= half_window) & (pos >= 0) & (pos < seq_len)


def _band_specs(d, group, tb, nblk):
    b = BAND_BLOCK
    q_spec = pl.BlockSpec((1, tb * b, d), lambda n, i: (n, i, 0))
    prev = pl.BlockSpec((1, b, d), lambda n, i: (n // group, jnp.maximum(i * tb - 1, 0), 0))
    cur = pl.BlockSpec((1, tb * b, d), lambda n, i: (n // group, i, 0))
    nxt = pl.BlockSpec((1, b, d), lambda n, i: (n // group, jnp.minimum((i + 1) * tb, nblk - 1), 0))
    return q_spec, prev, cur, nxt


def _band_fwd_call(q, k, v, bias, sink, half_window, group, scale, name):
    nq, seq, d = q.shape
    hb = bias.shape[0]
    b = BAND_BLOCK
    nblk = seq // b
    tb = _pick(nblk, (4, 2, 1))

    def body(q_ref, kp_ref, kc_ref, kn_ref, vp_ref, vc_ref, vn_ref, bias_ref, sink_ref, o_ref, lse_ref):
        n, i = pl.program_id(0), pl.program_id(1)
        head = n % hb
        kcat = jnp.concatenate([kp_ref[0], kc_ref[0], kn_ref[0]], axis=0)
        vcat = jnp.concatenate([vp_ref[0], vc_ref[0], vn_ref[0]], axis=0)
        bias_h = bias_ref[head]
        sk = sink_ref[head]
        for t in range(tb):
            rows = slice(t * b, (t + 1) * b)
            band = slice(t * b, (t + 3) * b)
            logits = _dot(q_ref[0, rows, :], kcat[band], 1, 1) * scale + bias_h
            logits = jnp.where(_band_mask(i * tb + t, seq, half_window), logits, NEG_INF)
            m = jnp.maximum(jnp.max(logits, axis=1, keepdims=True), sk)
            e = jnp.exp(logits - m)
            ssum = jnp.sum(e, axis=1, keepdims=True) + jnp.exp(sk - m)
            o_ref[0, rows, :] = _dot(e / ssum, vcat[band], 1, 0)
            lse_ref[0, rows, :] = m + jnp.log(ssum)

    q_spec, prev, cur, nxt = _band_specs(d, group, tb, nblk)
    return pl.pallas_call(
        body,
        grid=(nq, nblk // tb),
        in_specs=[q_spec, prev, cur, nxt, prev, cur, nxt,
                  pl.BlockSpec(bias.shape, lambda n, i: (0, 0, 0)), pl.BlockSpec(sink.shape, lambda n, i: (0, 0, 0))],
        out_specs=[q_spec, pl.BlockSpec((1, tb * b, 1), lambda n, i: (n, i, 0))],
        out_shape=[jax.ShapeDtypeStruct((nq, seq, d), F32), jax.ShapeDtypeStruct((nq, seq, 1), F32)],
        name=name + "_fwd",
        compiler_params=_params("parallel", "parallel"),
    )(q, k, k, k, v, v, v, bias, sink)


def _band_bwd_call(q, k, v, bias, sink, o, lse, do, dlse, half_window, group, scale, name):
    nq, seq, d = q.shape
    nkv = nq // group
    hb = bias.shape[0]
    b = BAND_BLOCK
    nblk = seq // b
    tb = _pick(nblk, (4, 2, 1))
    nsteps = nblk // tb

    def body(q_ref, kp_ref, kc_ref, kn_ref, vp_ref, vc_ref, vn_ref, bias_ref, sink_ref, o_ref, lse_ref, do_ref,
             dlse_ref, dq_ref, dk_ref, dv_ref, dbias_ref, dsink_ref, dk_s, dv_s):
        n, i = pl.program_id(0), pl.program_id(1)
        head = n % hb

        @pl.when((n == 0) & (i == 0))
        def _():
            dbias_ref[...] = jnp.zeros_like(dbias_ref)
            dsink_ref[...] = jnp.zeros_like(dsink_ref)

        @pl.when((n % group == 0) & (i == 0))
        def _():
            dk_s[...] = jnp.zeros_like(dk_s)
            dv_s[...] = jnp.zeros_like(dv_s)

        kcat = jnp.concatenate([kp_ref[0], kc_ref[0], kn_ref[0]], axis=0)
        vcat = jnp.concatenate([vp_ref[0], vc_ref[0], vn_ref[0]], axis=0)
        bias_h = bias_ref[head]
        sk = sink_ref[head]
        for t in range(tb):
            rows = slice(t * b, (t + 1) * b)
            band = slice(t * b, (t + 3) * b)
            qt, dot_, lse_t = q_ref[0, rows, :], do_ref[0, rows, :], lse_ref[0, rows, :]
            logits = _dot(qt, kcat[band], 1, 1) * scale + bias_h
            logits = jnp.where(_band_mask(i * tb + t, seq, half_window), logits, NEG_INF)
            p = jnp.exp(logits - lse_t)
            dp = _dot(dot_, vcat[band], 1, 1)
            shift = dlse_ref[0, rows, :] - jnp.sum(dot_ * o_ref[0, rows, :], axis=1, keepdims=True)
            dlogits = p * (dp + shift)
            dbias_ref[head] += dlogits
            dsink_ref[head] += jnp.sum(jnp.exp(sk - lse_t) * shift, axis=0, keepdims=True)
            ds = dlogits * scale
            dq_ref[0, rows, :] = _dot(ds, kcat[band], 1, 0)
            win = pl.ds(pl.multiple_of((i * tb + t) * b, b), 3 * b)
            dk_s[win, :] += _dot(ds, qt, 0, 0)
            dv_s[win, :] += _dot(p, dot_, 0, 0)

        @pl.when((n % group == group - 1) & (i == nsteps - 1))
        def _():
            dk_ref[0] = dk_s[b:b + seq, :]
            dv_ref[0] = dv_s[b:b + seq, :]

    q_spec, prev, cur, nxt = _band_specs(d, group, tb, nblk)
    row1 = pl.BlockSpec((1, tb * b, 1), lambda n, i: (n, i, 0))
    kv_out = pl.BlockSpec((1, seq, d), lambda n, i: (n // group, 0, 0))
    return pl.pallas_call(
        body,
        grid=(nq, nsteps),
        in_specs=[q_spec, prev, cur, nxt, prev, cur, nxt,
                  pl.BlockSpec(bias.shape, lambda n, i: (0, 0, 0)), pl.BlockSpec(sink.shape, lambda n, i: (0, 0, 0)),
                  q_spec, row1, q_spec, row1],
        out_specs=[q_spec, kv_out, kv_out,
                   pl.BlockSpec(bias.shape, lambda n, i: (0, 0, 0)), pl.BlockSpec(sink.shape, lambda n, i: (0, 0, 0))],
        out_shape=[jax.ShapeDtypeStruct((nq, seq, d), F32), jax.ShapeDtypeStruct((nkv, seq, d), F32),
                   jax.ShapeDtypeStruct((nkv, seq, d), F32), jax.ShapeDtypeStruct(bias.shape, F32),
                   jax.ShapeDtypeStruct(sink.shape, F32)],
        scratch_shapes=[pltpu.VMEM((seq + 2 * b, d), F32), pltpu.VMEM((seq + 2 * b, d), F32)],
        name=name + "_bwd",
        compiler_params=_params("arbitrary", "arbitrary"),
    )(q, k, k, k, v, v, v, bias, sink, o, lse, do, dlse)


def _banded_attention(q, k, v, bias, sink, half_window, group, scale, name):
    @jax.custom_vjp
    def op(q, k, v, bias, sink):
        return tuple(_band_fwd_call(q, k, v, bias, sink, half_window, group, scale, name))

    def op_fwd(q, k, v, bias, sink):
        o, lse = _band_fwd_call(q, k, v, bias, sink, half_window, group, scale, name)
        return (o, lse), (q, k, v, bias, sink, o, lse)

    def op_bwd(res, cts):
        q, k, v, bias, sink, o, lse = res
        do, dlse = cts
        return tuple(_band_bwd_call(q, k, v, bias, sink, o, lse, do, dlse, half_window, group, scale, name))

    op.defvjp(op_fwd, op_bwd)
    return op(q, k, v, bias, sink)


def _loss_call(x, target, g):
    s, d = x.shape
    tr = _pick(s, (256, 128, 64, 32, 16, 8))

    def tile_loss(xt, gt, tt):
        err = jnp.square(_rms(xt, gt) - tt)
        return 0.5 * jnp.sum(jnp.mean(err, axis=-1, keepdims=True), axis=0, keepdims=True)

    def body(x_ref, t_ref, g_ref, loss_ref, dx_ref, dg_ref):
        tt = t_ref[...]
        val, vjp = jax.vjp(lambda xt, gt: tile_loss(xt, gt, tt), x_ref[...], g_ref[...])
        dx, dg = vjp(jnp.ones_like(val))
        dx_ref[...] = dx

        @pl.when(pl.program_id(0) == 0)
        def _():
            loss_ref[...] = jnp.zeros_like(loss_ref)
            dg_ref[...] = jnp.zeros_like(dg_ref)

        loss_ref[...] += val
        dg_ref[...] += dg

    return pl.pallas_call(
        body,
        grid=(s // tr,),
        in_specs=[_row_spec(tr, d), _row_spec(tr, d), _full_spec((1, d))],
        out_specs=[_full_spec((1, 1)), _row_spec(tr, d), _full_spec((1, d))],
        out_shape=[jax.ShapeDtypeStruct((1, 1), F32), jax.ShapeDtypeStruct((s, d), F32),
                   jax.ShapeDtypeStruct((1, d), F32)],
        name="final_norm_loss",
        compiler_params=_params("arbitrary"),
    )(x, target, g)


@jax.custom_vjp
def _loss_op(x, target, g):
    return _loss_call(x, target, g)[0][0, 0]


def _loss_op_fwd(x, target, g):
    loss, dx, dg = _loss_call(x, target, g)
    return loss[0, 0], (dx, dg, target)


def _loss_op_bwd(res, ct):
    dx, dg, target = res
    return ct * dx, jnp.zeros_like(target), ct * dg


_loss_op.defvjp(_loss_op_fwd, _loss_op_bwd)


def _norm_tile(r, a, p, c):
    return (_rms(r[0], p[0]),)


def _prep_mla_tile(r, a, p, c):
    a_q, a_kv, a_kr = r
    cos_t, sin_t = a
    g_q, g_kv, w_q, w_k, w_v = p
    (place_kr,) = c
    q = _rope(_bdot(_rms(a_q, g_q), w_q), cos_t, sin_t, MLA_ROPE // 2)
    ckv = _rms(a_kv, g_kv)
    k = _rope(_bdot(ckv, w_k) + _hdot(a_kr, place_kr), cos_t, sin_t, MLA_ROPE // 2)
    return q, k, _bdot(ckv, w_v)


def _head_rms(x, g, head_mean):
    return x * lax.rsqrt(_hdot(x * x, head_mean) + EPS) * g


def _prep_gqa_tile(r, a, p, c):
    b_q, b_k = r
    cos_t, sin_t = a
    g_q, g_k = p
    mean_q, mean_k = c
    wk = b_k.shape[1]
    q = _rope(_head_rms(b_q, g_q, mean_q), cos_t, sin_t, HEAD_DIM // 4)
    k = _rope(_head_rms(b_k, g_k, mean_k), cos_t[:, :wk], sin_t[:, :wk], HEAD_DIM // 4)
    return q, k


def _combine_tile(r, a, p, c):
    o0, o1, o2, l0, l1, l2 = r
    m = jnp.maximum(jnp.maximum(l0, l1), l2)
    e0, e1, e2 = jnp.exp(l0 - m), jnp.exp(l1 - m), jnp.exp(l2 - m)
    return ((e0 * o0 + e1 * o1 + e2 * o2) / (e0 + e1 + e2),)


def _merge_tile(r, a, p, c):
    y, gate_path, merge_logits = r
    (w_branch,) = p
    d = w_branch.shape[2]
    u = y * (gate_path * jax.nn.sigmoid(gate_path))
    out = None
    for nb in range(N_BRANCH):
        term = jax.nn.sigmoid(merge_logits[:, nb * d:(nb + 1) * d]) * _bdot(
            u[:, nb * BRANCH_W:(nb + 1) * BRANCH_W], w_branch[nb])
        out = term if out is None else out + term
    return (out,)


def _rope_angles(pos, dim):
    inv = ROPE_THETA ** (-jnp.arange(0, dim, 2, dtype=F32) / dim)
    return pos.astype(F32)[:, None] * inv[None, :]


def _rope_tables(s):
    pos = jnp.arange(s, dtype=jnp.int32)
    rows = s // GRID_W
    row_idx = jnp.repeat(jnp.arange(rows, dtype=jnp.int32), GRID_W)
    col_idx = jnp.tile(jnp.arange(GRID_W, dtype=jnp.int32), rows)
    a1 = _rope_angles(pos, MLA_ROPE)
    ar = _rope_angles(row_idx, HEAD_DIM // 2)
    ac = _rope_angles(col_idx, HEAD_DIM // 2)
    ones, zeros = jnp.ones((s, MLA_NOPE), F32), jnp.zeros((s, MLA_NOPE), F32)
    mla_cos = jnp.tile(jnp.concatenate([ones, jnp.cos(a1), jnp.cos(a1)], axis=1), (1, MLA_HEADS))
    mla_sin = jnp.tile(jnp.concatenate([zeros, -jnp.sin(a1), jnp.sin(a1)], axis=1), (1, MLA_HEADS))
    gqa_cos = jnp.tile(jnp.concatenate([jnp.cos(ar), jnp.cos(ar), jnp.cos(ac), jnp.cos(ac)], axis=1), (1, GQA_HEADS))
    gqa_sin = jnp.tile(jnp.concatenate([-jnp.sin(ar), jnp.sin(ar), -jnp.sin(ac), jnp.sin(ac)], axis=1), (1, GQA_HEADS))
    return mla_cos, mla_sin, gqa_cos, gqa_sin


def _t5_bucket(rel):
    nb = T5_BUCKETS // 2
    max_exact = nb // 2
    n = jnp.abs(rel)
    nf = jnp.maximum(n, 1).astype(F32)
    large = max_exact + (jnp.log(nf / max_exact) / math.log(T5_MAX_DIST / max_exact) * (nb - max_exact)).astype(jnp.int32)
    large = jnp.minimum(large, nb - 1)
    return jnp.where(rel > 0, nb, 0) + jnp.where(n < max_exact, n, large)


def _band_bias(table, stride, head_lo, heads):
    b = BAND_BLOCK
    offs = jnp.arange(3 * b)[None, :] - b - jnp.arange(b)[:, None]
    bias = table[_t5_bucket(offs * stride)][..., head_lo:head_lo + heads]
    return jnp.transpose(bias, (2, 0, 1)).astype(F32)


def _heads(t, h):
    s, w = t.shape
    return t.reshape(s, h, w // h).transpose(1, 0, 2)


def _unheads(t):
    h, s, d = t.shape
    return t.transpose(1, 0, 2).reshape(s, h * d)


def _layer(x, w, l, tabs, consts, biases):
    s, d_model = x.shape
    mla_cos, mla_sin, gqa_cos, gqa_sin = tabs
    place_kr, mean_q, mean_k = consts
    dil_bias, win_bias = biases

    xn = _rowop("norm", _norm_tile, 256, (d_model,), [x], [w["norm_g"][l][None, :]])[0]
    w_in = w["w_in"][l]
    lo_kr = MLA_Q_LORA + MLA_KV_LORA
    w_in_p = jnp.concatenate(
        [w_in[:, :lo_kr], w_in[:, lo_kr + MLA_ROPE:], w_in[:, lo_kr:lo_kr + MLA_ROPE],
         jnp.zeros((d_model, LANES - MLA_ROPE), F32)], axis=1)
    proj = _matmul(xn, w_in_p, "proj")
    widths = (MLA_Q_LORA, MLA_KV_LORA, 256, 128, 128, 768, 768, 768, 256, 128, 128,
              N_BRANCH * BRANCH_W, N_BRANCH * d_model, MLA_ROPE)
    pieces, at = [], 0
    for wd in widths:
        pieces.append(proj[:, at:at + wd])
        at += wd
    a_q, a_kv, b_q, b_k, b_v, c_q, c_k, c_v, d_q, d_k, d_v, gate_path, merge_logits, a_kr = pieces

    w_kv = w["w_mla_kv_up"][l].reshape(MLA_KV_LORA, MLA_HEADS, MLA_NOPE + MLA_V)
    w_k = jnp.concatenate([w_kv[:, :, :MLA_NOPE], jnp.zeros((MLA_KV_LORA, MLA_HEADS, MLA_ROPE), F32)], axis=2)
    w_k = w_k.reshape(MLA_KV_LORA, MLA_HEADS * MLA_QK)
    w_v = w_kv[:, :, MLA_NOPE:].reshape(MLA_KV_LORA, MLA_HEADS * MLA_V)
    q_a, k_a, v_a = _rowop(
        "prep_mla", _prep_mla_tile, 256, (MLA_HEADS * MLA_QK, MLA_HEADS * MLA_QK, MLA_HEADS * MLA_V),
        [a_q, a_kv, a_kr],
        [w["mla_q_norm_g"][l][None, :], w["mla_kv_norm_g"][l][None, :], w["w_mla_q_up"][l], w_k, w_v],
        [place_kr], [mla_cos, mla_sin])
    y_a = _unheads(_dense_attention(_heads(q_a, MLA_HEADS), _heads(k_a, MLA_HEADS), _heads(v_a, MLA_HEADS),
                                    MLA_QK ** -0.5, "mla"))

    q_b, k_b = _rowop(
        "prep_gqa", _prep_gqa_tile, 256, (GQA_HEADS * HEAD_DIM, GQA_KV_HEADS * HEAD_DIM),
        [b_q, b_k],
        [jnp.tile(w["gqa_q_norm_g"][l], GQA_HEADS)[None, :], jnp.tile(w["gqa_k_norm_g"][l], GQA_KV_HEADS)[None, :]],
        [mean_q, mean_k], [gqa_cos, gqa_sin])
    grp = GQA_HEADS // GQA_KV_HEADS
    o_b = _dense_attention(_heads(q_b, GQA_HEADS).reshape(GQA_KV_HEADS, grp * s, HEAD_DIM),
                           _heads(k_b, GQA_KV_HEADS), _heads(b_v, GQA_KV_HEADS), HEAD_DIM ** -0.5, "gqa")
    y_b = _unheads(o_b.reshape(GQA_HEADS, s, HEAD_DIM))

    no_sink = jnp.full((DIL_HEADS, 1, 1), NEG_INF, F32)
    outs, lses = [], []
    for gi, (window, dil) in enumerate(DIL_PATTERNS):
        sub = s // dil

        def to_sub(t):
            t = t[:, gi * DIL_HEADS * HEAD_DIM:(gi + 1) * DIL_HEADS * HEAD_DIM]
            return t.reshape(sub, dil, DIL_HEADS, HEAD_DIM).transpose(1, 2, 0, 3).reshape(dil * DIL_HEADS, sub, HEAD_DIM)

        o, lse = _banded_attention(to_sub(c_q), to_sub(c_k), to_sub(c_v), dil_bias[gi], no_sink,
                                   window // (2 * dil), 1, HEAD_DIM ** -0.5, "dil%d" % gi)

        def from_sub(t):
            dd = t.shape[2]
            return t.reshape(dil, DIL_HEADS, sub, dd).transpose(1, 2, 0, 3).reshape(DIL_HEADS * s, dd)

        outs.append(from_sub(o))
        lses.append(from_sub(lse))
    y_c = _rowop("dil_merge", _combine_tile, 512, (HEAD_DIM,), outs + lses, [])[0]
    y_c = _unheads(y_c.reshape(DIL_HEADS, s, HEAD_DIM))

    grp = WIN_HEADS // WIN_KV_HEADS
    o_d, _ = _banded_attention(_heads(d_q, WIN_HEADS), _heads(d_k, WIN_KV_HEADS), _heads(d_v, WIN_KV_HEADS),
                               win_bias, w["win_sink"][l].reshape(WIN_HEADS, 1, 1), WIN_HALF, grp,
                               HEAD_DIM ** -0.5, "win")
    y_d = _unheads(o_d)

    y = jnp.concatenate([y_a, y_b, y_c, y_d], axis=1)
    mix = _rowop("merge", _merge_tile, 128, (d_model,), [y, gate_path, merge_logits], [w["w_branch"][l]])[0]
    return x + _matmul(mix, w["w_out"][l], "out_proj")


def _local_loss(w, x, target):
    s = x.shape[0]
    tabs = _rope_tables(s)
    place = np.zeros((MLA_ROPE, MLA_HEADS * MLA_QK), np.float32)
    for h in range(MLA_HEADS):
        for i in range(MLA_ROPE):
            place[i, h * MLA_QK + MLA_NOPE + i] = 1.0

    def head_mean(nh):
        m = np.kron(np.eye(nh, dtype=np.float32), np.full((HEAD_DIM, HEAD_DIM), 1.0 / HEAD_DIM, np.float32))
        return jnp.asarray(m)

    consts = (jnp.asarray(place), head_mean(GQA_HEADS), head_mean(GQA_KV_HEADS))
    table = w["t5_table"]
    dil_bias = [_band_bias(table, dil, gi * DIL_HEADS, DIL_HEADS) for gi, (_, dil) in enumerate(DIL_PATTERNS)]
    win_bias = _band_bias(table, 1, len(DIL_PATTERNS) * DIL_HEADS, WIN_HEADS)
    depth = w["norm_g"].shape[0]
    for l in range(depth):
        x = _layer(x, w, l, tabs, consts, (dil_bias, win_bias))
    return _loss_op(x, target, w["final_norm_g"][None, :])


_ANY = pl.BlockSpec(memory_space=pl.ANY)
_MESH = pl.DeviceIdType.MESH


def _all_gather(block, name):
    def body(x_ref, out_ref, send_sems, recv_sems, local_sem):
        x, y, c = lax.axis_index("x"), lax.axis_index("y"), lax.axis_index("c")
        me, sibling = (x, y, c), (x, y, 1 - c)
        chips = [(1 - x, y), (x, 1 - y), (1 - x, 1 - y)]

        def slot(px, py, pc):
            return out_ref.at[4 * px + 2 * py + pc]

        def copy(k, blk, to, src=None):
            return pltpu.make_async_remote_copy(
                src_ref=slot(*blk) if src is None else src, dst_ref=slot(*blk),
                send_sem=send_sems.at[k], recv_sem=recv_sems.at[k], device_id=to, device_id_type=_MESH)

        mine = pltpu.make_async_copy(x_ref, slot(*me), local_sem)
        mine.start()
        first = [copy(0, me, sibling, src=x_ref)]
        first += [copy(1 + j, me, (*chip, c), src=x_ref) for j, chip in enumerate(chips)]
        for cp in first:
            cp.start()
        passed = [copy(4 + j, (*chip, c), sibling) for j, chip in enumerate(chips)]
        for j, chip in enumerate(chips):
            copy(1 + j, (*chip, c), me).wait_recv()
            passed[j].start()
        copy(0, sibling, me).wait_recv()
        for j, chip in enumerate(chips):
            copy(4 + j, (*chip, 1 - c), me).wait_recv()
        for cp in first + passed:
            cp.wait_send()
        mine.wait()

    return pl.pallas_call(
        body,
        out_shape=jax.ShapeDtypeStruct((N_DEV,) + block.shape, block.dtype),
        in_specs=[_ANY],
        out_specs=_ANY,
        scratch_shapes=[pltpu.SemaphoreType.DMA((7,)), pltpu.SemaphoreType.DMA((7,)), pltpu.SemaphoreType.DMA],
        name=name,
    )(block)


def _all_to_all(blocks, name):
    def body(x_ref, out_ref, send_sems, recv_sems, local_sem):
        x, y, c = lax.axis_index("x"), lax.axis_index("y"), lax.axis_index("c")
        me = 4 * x + 2 * y + c
        mine = pltpu.make_async_copy(x_ref.at[me], out_ref.at[me], local_sem)
        mine.start()
        copies, landed = [], []
        for k in range(1, N_DEV):
            px = 1 - x if k & 4 else x
            py = 1 - y if k & 2 else y
            pc = 1 - c if k & 1 else c
            peer = 4 * px + 2 * py + pc
            copies.append(pltpu.make_async_remote_copy(
                src_ref=x_ref.at[peer], dst_ref=out_ref.at[me], send_sem=send_sems.at[k - 1],
                recv_sem=recv_sems.at[k - 1], device_id=(px, py, pc), device_id_type=_MESH))
            landed.append(pltpu.make_async_remote_copy(
                src_ref=x_ref.at[peer], dst_ref=out_ref.at[peer], send_sem=send_sems.at[k - 1],
                recv_sem=recv_sems.at[k - 1], device_id=(px, py, pc), device_id_type=_MESH))
        for cp in copies:
            cp.start()
        for cp in landed:
            cp.wait_recv()
        for cp in copies:
            cp.wait_send()
        mine.wait()

    return pl.pallas_call(
        body,
        out_shape=jax.ShapeDtypeStruct(blocks.shape, blocks.dtype),
        in_specs=[_ANY],
        out_specs=_ANY,
        scratch_shapes=[pltpu.SemaphoreType.DMA((7,)), pltpu.SemaphoreType.DMA((7,)), pltpu.SemaphoreType.DMA],
        name=name,
    )(blocks)


def _sum_slots(parts, name):
    _, rows, w = parts.shape
    tr = _pick(rows, (1024, 512, 256, 128, 64, 32, 16, 8))

    def body(p_ref, o_ref):
        acc = p_ref[0].astype(F32)
        for j in range(1, N_DEV):
            acc = acc + p_ref[j].astype(F32)
        o_ref[...] = acc

    return pl.pallas_call(
        body,
        grid=(rows // tr,),
        in_specs=[pl.BlockSpec((N_DEV, tr, w), lambda i: (0, i, 0))],
        out_specs=pl.BlockSpec((tr, w), lambda i: (i, 0)),
        out_shape=jax.ShapeDtypeStruct((rows, w), F32),
        name=name,
        compiler_params=_params("parallel"),
    )(parts)


def _adamw(w, g, m, v, name):
    rows, width = w.shape
    tr = _pick(rows, (1024, 512, 256, 128, 64, 32, 16, 8))

    def body(w_ref, g_ref, m_ref, v_ref, d_ref, nm_ref, nv_ref):
        g_ = g_ref[...]
        m_ = ADAM_B1 * m_ref[...] + (1.0 - ADAM_B1) * g_
        v_ = ADAM_B2 * v_ref[...] + (1.0 - ADAM_B2) * jnp.square(g_)
        m_hat = m_ / (1.0 - ADAM_B1 ** ADAM_STEP)
        v_hat = v_ / (1.0 - ADAM_B2 ** ADAM_STEP)
        d_ref[...] = -ADAM_LR * (m_hat / (jnp.sqrt(v_hat) + ADAM_EPS) + ADAM_WD * w_ref[...])
        nm_ref[...] = m_
        nv_ref[...] = v_

    spec = pl.BlockSpec((tr, width), lambda i: (i, 0))
    return pl.pallas_call(
        body,
        grid=(rows // tr,),
        in_specs=[spec] * 4,
        out_specs=[spec] * 3,
        out_shape=[jax.ShapeDtypeStruct((rows, width), F32)] * 3,
        name=name,
        compiler_params=_params("parallel"),
    )(w, g, m, v)


_SHARDED = (("w_in", 2), ("w_mla_q_up", 2), ("w_mla_kv_up", 2), ("w_branch", 3), ("w_out", 1))
_REPLICATED = ("norm_g", "mla_q_norm_g", "mla_kv_norm_g", "gqa_q_norm_g", "gqa_k_norm_g", "win_sink", "t5_table",
               "final_norm_g")


def _pack(arrays, row_multiple):
    flat = jnp.concatenate([a.reshape(-1) for a in arrays])
    rows = -(-flat.shape[0] // (LANES * row_multiple)) * row_multiple
    return jnp.pad(flat, (0, rows * LANES - flat.shape[0])).reshape(rows, LANES)


def _unpack(packed, shapes):
    flat, out, at = packed.reshape(-1), [], 0
    for shp in shapes:
        n = int(np.prod(shp))
        out.append(flat[at:at + n].reshape(shp))
        at += n
    return out


def _split_shards(full, axis):
    shp = full.shape
    t = full.reshape(shp[:axis] + (N_DEV, shp[axis] // N_DEV) + shp[axis + 1:])
    return jnp.moveaxis(t, axis, 0)


def _join_shards(stacked, axis):
    t = jnp.moveaxis(stacked, 0, axis)
    shp = t.shape
    return t.reshape(shp[:axis] + (shp[axis] * shp[axis + 1],) + shp[axis + 2:])


def kernel(x, norm_g, w_in, mla_q_norm_g, mla_kv_norm_g, w_mla_q_up, w_mla_kv_up, gqa_q_norm_g, gqa_k_norm_g, win_sink, t5_table, w_branch, w_out, final_norm_g, loss_target, m_norm_g, m_w_in, m_mla_q_norm_g, m_mla_kv_norm_g, m_w_mla_q_up, m_w_mla_kv_up, m_gqa_q_norm_g, m_gqa_k_norm_g, m_win_sink, m_t5_table, m_w_branch, m_w_out, m_final_norm_g, v_norm_g, v_w_in, v_mla_q_norm_g, v_mla_kv_norm_g, v_w_mla_q_up, v_w_mla_kv_up, v_gqa_q_norm_g, v_gqa_k_norm_g, v_win_sink, v_t5_table, v_w_branch, v_w_out, v_final_norm_g):
    given = dict(locals())
    names = ("norm_g", "w_in", "mla_q_norm_g", "mla_kv_norm_g", "w_mla_q_up", "w_mla_kv_up", "gqa_q_norm_g",
             "gqa_k_norm_g", "win_sink", "t5_table", "w_branch", "w_out", "final_norm_g")
    shard_names = [n for n, _ in _SHARDED]
    shard_shapes = [given[n].shape for n in shard_names]

    mine = _pack([given[n] for n in shard_names], 16)
    gathered = _all_gather(mine.astype(BF16), "gather_weights").astype(F32)
    per_dev = [_unpack(gathered[j], shard_shapes) for j in range(N_DEV)]
    weights = {n: given[n] for n in _REPLICATED}
    for wi, (n, axis) in enumerate(_SHARDED):
        weights[n] = _join_shards(jnp.stack([per_dev[j][wi] for j in range(N_DEV)]), axis)

    loss, (gw, gx) = jax.value_and_grad(_local_loss, argnums=(0, 1))(weights, x[0], loss_target[0])
    loss = lax.psum(loss, ("x", "y", "c"))

    blocks = [_split_shards(gw[n], axis) for n, axis in _SHARDED]
    send = jnp.stack([_pack([b[j] for b in blocks], 16) for j in range(N_DEV)])
    g_shard = _unpack(_sum_slots(_all_to_all(send.astype(BF16), "scatter_grads"), "sum_grads"), shard_shapes)
    rep_shapes = [given[n].shape for n in _REPLICATED]
    g_rep = _unpack(_sum_slots(_all_gather(_pack([gw[n] for n in _REPLICATED], 8), "gather_small_grads"),
                               "sum_small_grads"), rep_shapes)
    grads = dict(zip(shard_names, g_shard))
    grads.update(zip(_REPLICATED, g_rep))

    def update(group, shapes, row_multiple, name):
        outs = _adamw(*[_pack([src[n] for n in group], row_multiple) for src in (
            given, grads, {n: given["m_" + n] for n in group}, {n: given["v_" + n] for n in group})], name)
        return [dict(zip(group, _unpack(o, shapes))) for o in outs]

    big = update(shard_names, shard_shapes, 16, "adamw_shards")
    small = update(list(_REPLICATED), rep_shapes, 8, "adamw_replicated")
    delta, new_m, new_v = [{**b, **s_} for b, s_ in zip(big, small)]
    return (loss, gx[None], *[grads[n] for n in names], *[delta[n] for n in names],
            *[new_m[n] for n in names], *[new_v[n] for n in names])
```

```python
import functools
import math

import jax
import jax.numpy as jnp
import numpy as np
from jax import lax
from jax.experimental import pallas as pl
from jax.experimental.pallas import tpu as pltpu

F32 = jnp.float32
BF16 = jnp.bfloat16
N_DEV = 8
LANES = 128
V7X_VMEM_LIMIT = 56 * 1024 * 1024

EPS = 1e-6
NEG_INF = -1e30
LOG2E = 1.4426950408889634
ROPE_THETA = 10000.0
GRID_W = 64
HEAD_DIM = 64
N_BRANCH = 4
BRANCH_W = 256
MLA_HEADS, MLA_Q_LORA, MLA_KV_LORA, MLA_NOPE, MLA_ROPE, MLA_V = 4, 256, 128, 64, 32, 64
MLA_QK = MLA_NOPE + MLA_ROPE
GQA_HEADS, GQA_KV_HEADS = 4, 2
DIL_PATTERNS = ((128, 1), (512, 4), (2048, 16))
DIL_HEADS = 4
WIN_HEADS, WIN_KV_HEADS, WIN_HALF = 4, 2, 128
T5_BUCKETS, T5_MAX_DIST = 32, 1024
BAND_BLOCK = 128
ADAM_LR, ADAM_B1, ADAM_B2, ADAM_EPS, ADAM_WD, ADAM_STEP = 0.001, 0.9, 0.999, 1e-08, 0.01, 10


def _params(*sem):
    return pltpu.CompilerParams(dimension_semantics=sem, vmem_limit_bytes=V7X_VMEM_LIMIT)


def _pick(n, cands):
    for c in cands:
        if n % c == 0:
            return c
    return n


def _dot(a, b, ca, cb):
    return lax.dot_general(a.astype(BF16), b.astype(BF16), (((ca,), (cb,)), ((), ())), preferred_element_type=F32)


@jax.custom_vjp
def _bdot(a, b):
    return _dot(a, b, 1, 0)


def _bdot_fwd(a, b):
    return _dot(a, b, 1, 0), (a, b)


def _bdot_bwd(res, g):
    a, b = res
    return _dot(g, b, 1, 1), _dot(a, g, 0, 0)


_bdot.defvjp(_bdot_fwd, _bdot_bwd)


def _hdot(a, c):
    return lax.dot_general(a, c, (((1,), (0,)), ((), ())), precision=lax.Precision.HIGHEST, preferred_element_type=F32)


@functools.partial(jax.custom_vjp, nondiff_argnums=(1,))
def _lane_roll(x, shift):
    return pltpu.roll(x, shift, 1)


def _lane_roll_fwd(x, shift):
    return pltpu.roll(x, shift, 1), None


def _lane_roll_bwd(shift, _, g):
    return (pltpu.roll(g, g.shape[1] - shift, 1),)


_lane_roll.defvjp(_lane_roll_fwd, _lane_roll_bwd)


def _rope(x, cos_t, sin_t, half):
    w = x.shape[1]
    lane = lax.broadcasted_iota(jnp.int32, (1, w), 1)
    first = (lane % (2 * half)) < half
    partner = jnp.where(first, _lane_roll(x, w - half), _lane_roll(x, half))
    return x * cos_t + partner * sin_t


def _rms(x, g):
    return x * lax.rsqrt(jnp.mean(x * x, axis=-1, keepdims=True) + EPS) * g


def _row_spec(tr, w):
    return pl.BlockSpec((tr, w), lambda i: (i, 0))


def _full_spec(shape):
    nd = len(shape)
    return pl.BlockSpec(tuple(shape), lambda i: (0,) * nd)


def _rowop_fwd_call(name, f, tr, out_widths, rows, aux, params, consts):
    nr, na, npar, nc = len(rows), len(aux), len(params), len(consts)
    n_rows = rows[0].shape[0]

    def body(*refs):
        vals = [x[...] for x in refs[:nr + na + npar + nc]]
        r, a = vals[:nr], vals[nr:nr + na]
        p, c = vals[nr + na:nr + na + npar], vals[nr + na + npar:]
        for o_ref, o in zip(refs[nr + na + npar + nc:], f(r, a, p, c)):
            o_ref[...] = o

    return pl.pallas_call(
        body,
        grid=(n_rows // tr,),
        in_specs=[_row_spec(tr, t.shape[1]) for t in rows + aux] + [_full_spec(t.shape) for t in params + consts],
        out_specs=[_row_spec(tr, w) for w in out_widths],
        out_shape=[jax.ShapeDtypeStruct((n_rows, w), F32) for w in out_widths],
        name=name + "_fwd",
        compiler_params=_params("parallel"),
    )(*rows, *aux, *params, *consts)


def _rowop_bwd_call(name, f, tr, rows, aux, params, consts, douts):
    nr, na, npar, nc, nd = len(rows), len(aux), len(params), len(consts), len(douts)
    n_rows = rows[0].shape[0]
    n_in = nr + na + npar + nc

    def body(*refs):
        vals = [x[...] for x in refs[:n_in + nd]]
        r, a = vals[:nr], vals[nr:nr + na]
        p, c = vals[nr + na:nr + na + npar], vals[nr + na + npar:n_in]
        d = vals[n_in:]
        out_refs = refs[n_in + nd:]
        _, vjp = jax.vjp(lambda r_, p_: tuple(f(r_, a, p_, c)), r, p)
        dr, dp = vjp(tuple(d))
        for o_ref, o in zip(out_refs[:nr], dr):
            o_ref[...] = o

        @pl.when(pl.program_id(0) == 0)
        def _():
            for o_ref in out_refs[nr:]:
                o_ref[...] = jnp.zeros_like(o_ref)

        for o_ref, o in zip(out_refs[nr:], dp):
            o_ref[...] += o

    outs = pl.pallas_call(
        body,
        grid=(n_rows // tr,),
        in_specs=[_row_spec(tr, t.shape[1]) for t in rows + aux] + [_full_spec(t.shape) for t in params + consts]
        + [_row_spec(tr, t.shape[1]) for t in douts],
        out_specs=[_row_spec(tr, t.shape[1]) for t in rows] + [_full_spec(t.shape) for t in params],
        out_shape=[jax.ShapeDtypeStruct(t.shape, F32) for t in rows + params],
        name=name + "_bwd",
        compiler_params=_params("arbitrary"),
    )(*rows, *aux, *params, *consts, *douts)
    return list(outs[:nr]), list(outs[nr:])


def _rowop(name, f, tr, out_widths, rows, params, consts=(), aux=()):
    rows, aux, params, consts = list(rows), list(aux), list(params), list(consts)
    tr = _pick(rows[0].shape[0], (tr, 256, 128, 64, 32, 16, 8))

    @jax.custom_vjp
    def op(rows, params, aux, consts):
        return tuple(_rowop_fwd_call(name, f, tr, out_widths, rows, aux, params, consts))

    def op_fwd(rows, params, aux, consts):
        return op(rows, params, aux, consts), (rows, params, aux, consts)

    def op_bwd(res, douts):
        rows, params, aux, consts = res
        dr, dp = _rowop_bwd_call(name, f, tr, rows, aux, params, consts, list(douts))
        return dr, dp, [jnp.zeros_like(t) for t in aux], [jnp.zeros_like(t) for t in consts]

    op.defvjp(op_fwd, op_bwd)
    return op(rows, params, aux, consts)


def _mm(a, b, mode, name):
    if mode == "nn":
        (m, k), n = a.shape, b.shape[1]
    elif mode == "nt":
        (m, k), n = a.shape, b.shape[0]
    else:
        (k, m), n = a.shape, b.shape[1]
    tm = _pick(m, (512, 256, 128))
    tn = _pick(n, (1024, 896, 640, 512, 384, 256, 128))
    tk = _pick(k, (1024, 896, 640, 512, 384, 256, 128))
    if mode == "tn":
        tm, tk = _pick(m, (1024, 512, 256, 128)), _pick(k, (512, 256, 128))
    nk = k // tk

    def body(a_ref, b_ref, o_ref, acc_ref):
        kk = pl.program_id(2)
        if mode == "nn":
            part = _dot(a_ref[...], b_ref[...], 1, 0)
        elif mode == "nt":
            part = _dot(a_ref[...], b_ref[...], 1, 1)
        else:
            part = _dot(a_ref[...], b_ref[...], 0, 0)
        if nk == 1:
            o_ref[...] = part
        else:
            @pl.when(kk == 0)
            def _():
                acc_ref[...] = part

            @pl.when(kk > 0)
            def _():
                acc_ref[...] += part

            @pl.when(kk == nk - 1)
            def _():
                o_ref[...] = acc_ref[...]

    if mode == "nn":
        a_spec = pl.BlockSpec((tm, tk), lambda i, j, kk: (i, kk))
        b_spec = pl.BlockSpec((tk, tn), lambda i, j, kk: (kk, j))
    elif mode == "nt":
        a_spec = pl.BlockSpec((tm, tk), lambda i, j, kk: (i, kk))
        b_spec = pl.BlockSpec((tn, tk), lambda i, j, kk: (j, kk))
    else:
        a_spec = pl.BlockSpec((tk, tm), lambda i, j, kk: (kk, i))
        b_spec = pl.BlockSpec((tk, tn), lambda i, j, kk: (kk, j))
    return pl.pallas_call(
        body,
        grid=(m // tm, n // tn, nk),
        in_specs=[a_spec, b_spec],
        out_specs=pl.BlockSpec((tm, tn), lambda i, j, kk: (i, j)),
        out_shape=jax.ShapeDtypeStruct((m, n), F32),
        scratch_shapes=[pltpu.VMEM((tm, tn), F32)],
        name=name,
        compiler_params=_params("parallel", "parallel", "arbitrary"),
    )(a, b)


def _matmul(a, b, name):
    @jax.custom_vjp
    def op(a, b):
        return _mm(a, b, "nn", name + "_nn")

    def op_fwd(a, b):
        return op(a, b), (a, b)

    def op_bwd(res, g):
        a, b = res
        return _mm(g, b, "nt", name + "_nt"), _mm(a, g, "tn", name + "_tn")

    op.defvjp(op_fwd, op_bwd)
    return op(a, b)


def _dense_fwd_call(q, k, v, scale, name):
    n, sq, d = q.shape
    sk, dv = k.shape[1], v.shape[2]
    tq = _pick(sq, (256, 128))
    c = scale * LOG2E

    def body(q_ref, k_ref, v_ref, o_ref, lse_ref, k_s, vext_s):
        @pl.when(pl.program_id(1) == 0)
        def _():
            k_s[...] = k_ref[0].astype(BF16)
            vext_s[...] = jnp.ones_like(vext_s)
            vext_s[:, :dv] = v_ref[0].astype(BF16)

        s = _dot(q_ref[0], k_s[...], 1, 1)
        m = jnp.max(s, axis=1, keepdims=True)
        p = jnp.exp2(s * c - m * c)
        acc = _dot(p, vext_s[...], 1, 0)
        l = acc[:, dv:dv + 1]
        o_ref[0] = acc[:, :dv] / l
        lse_ref[0] = m * scale + jnp.log(l)

    return pl.pallas_call(
        body,
        grid=(n, sq // tq),
        in_specs=[
            pl.BlockSpec((1, tq, d), lambda h, i: (h, i, 0)),
            pl.BlockSpec((1, sk, d), lambda h, i: (h, 0, 0)),
            pl.BlockSpec((1, sk, dv), lambda h, i: (h, 0, 0)),
        ],
        out_specs=[
            pl.BlockSpec((1, tq, dv), lambda h, i: (h, i, 0)),
            pl.BlockSpec((1, tq, 1), lambda h, i: (h, i, 0)),
        ],
        out_shape=[jax.ShapeDtypeStruct((n, sq, dv), F32), jax.ShapeDtypeStruct((n, sq, 1), F32)],
        scratch_shapes=[pltpu.VMEM((sk, d), BF16), pltpu.VMEM((sk, 2 * dv), BF16)],
        name=name + "_fwd",
        compiler_params=_params("arbitrary", "arbitrary"),
    )(q, k, v)


def _dense_bwd_call(q, k, v, o, lse, do, scale, name):
    n, sq, d = q.shape
    sk, dv = k.shape[1], v.shape[2]
    tq, tk = _pick(sq, (512, 256, 128)), _pick(sk, (2048, 1024, 512, 256, 128))
    c = scale * LOG2E

    def body(q_ref, k_ref, v_ref, o_ref, lse_ref, do_ref, dq_ref, dk_ref, dv_ref):
        j, i = pl.program_id(1), pl.program_id(2)
        qb, kb, vb = q_ref[0].astype(BF16), k_ref[0].astype(BF16), v_ref[0].astype(BF16)
        do_f = do_ref[0]
        dob = do_f.astype(BF16)
        p = jnp.exp2(_dot(qb, kb, 1, 1) * c - lse_ref[0] * LOG2E)
        delta = jnp.sum(do_f * o_ref[0], axis=1, keepdims=True)
        ds = (p * (_dot(dob, vb, 1, 1) - delta)).astype(BF16)
        dv_part = _dot(p, dob, 0, 0)
        dk_part = _dot(ds, qb, 0, 0) * scale
        dq_part = _dot(ds, kb, 1, 0) * scale
        rows = pl.ds(pl.multiple_of(i * tq, tq), tq)

        @pl.when(i == 0)
        def _():
            dk_ref[0] = dk_part
            dv_ref[0] = dv_part

        @pl.when(i > 0)
        def _():
            dk_ref[0] += dk_part
            dv_ref[0] += dv_part

        @pl.when(j == 0)
        def _():
            dq_ref[0, rows, :] = dq_part

        @pl.when(j > 0)
        def _():
            dq_ref[0, rows, :] += dq_part

    return pl.pallas_call(
        body,
        grid=(n, sk // tk, sq // tq),
        in_specs=[
            pl.BlockSpec((1, tq, d), lambda h, j, i: (h, i, 0)),
            pl.BlockSpec((1, tk, d), lambda h, j, i: (h, j, 0)),
            pl.BlockSpec((1, tk, dv), lambda h, j, i: (h, j, 0)),
            pl.BlockSpec((1, tq, dv), lambda h, j, i: (h, i, 0)),
            pl.BlockSpec((1, tq, 1), lambda h, j, i: (h, i, 0)),
            pl.BlockSpec((1, tq, dv), lambda h, j, i: (h, i, 0)),
        ],
        out_specs=[
            pl.BlockSpec((1, sq, d), lambda h, j, i: (h, 0, 0)),
            pl.BlockSpec((1, tk, d), lambda h, j, i: (h, j, 0)),
            pl.BlockSpec((1, tk, dv), lambda h, j, i: (h, j, 0)),
        ],
        out_shape=[
            jax.ShapeDtypeStruct((n, sq, d), F32),
            jax.ShapeDtypeStruct((n, sk, d), F32),
            jax.ShapeDtypeStruct((n, sk, dv), F32),
        ],
        name=name + "_bwd",
        compiler_params=_params("arbitrary", "arbitrary", "arbitrary"),
    )(q, k, v, o, lse, do)


def _dense_attention(q, k, v, scale, name):
    @jax.custom_vjp
    def op(q, k, v):
        return _dense_fwd_call(q, k, v, scale, name)[0]

    def op_fwd(q, k, v):
        o, lse = _dense_fwd_call(q, k, v, scale, name)
        return o, (q, k, v, o, lse)

    def op_bwd(res, do):
        q, k, v, o, lse = res
        return tuple(_dense_bwd_call(q, k, v, o, lse, do, scale, name))

    op.defvjp(op_fwd, op_bwd)
    return op(q, k, v)


def _band_mask(block_index, seq_len, half_window):
    b = BAND_BLOCK
    r = lax.broadcasted_iota(jnp.int32, (b, 3 * b), 0)
    c = lax.broadcasted_iota(jnp.int32, (b, 3 * b), 1)
    pos = (block_index - 1) * b + c
    return (jnp.abs(c - b - r) <= half_window) & (pos >= 0) & (pos < seq_len)


def _band_specs(d, group, tb, nblk):
    b = BAND_BLOCK
    q_spec = pl.BlockSpec((1, tb * b, d), lambda n, i: (n, i, 0))
    prev = pl.BlockSpec((1, b, d), lambda n, i: (n // group, jnp.maximum(i * tb - 1, 0), 0))
    cur = pl.BlockSpec((1, tb * b, d), lambda n, i: (n // group, i, 0))
    nxt = pl.BlockSpec((1, b, d), lambda n, i: (n // group, jnp.minimum((i + 1) * tb, nblk - 1), 0))
    return q_spec, prev, cur, nxt


def _band_fwd_call(q, k, v, bias, sink, half_window, group, scale, name):
    nq, seq, d = q.shape
    hb = bias.shape[0]
    b = BAND_BLOCK
    nblk = seq // b
    tb = _pick(nblk, (4, 2, 1))

    def body(q_ref, kp_ref, kc_ref, kn_ref, vp_ref, vc_ref, vn_ref, bias_ref, sink_ref, o_ref, lse_ref):
        n, i = pl.program_id(0), pl.program_id(1)
        head = n % hb
        kcat = jnp.concatenate([kp_ref[0], kc_ref[0], kn_ref[0]], axis=0)
        vcat = jnp.concatenate([vp_ref[0], vc_ref[0], vn_ref[0]], axis=0)
        bias_h = bias_ref[head]
        sk = sink_ref[head]
        for t in range(tb):
            rows = slice(t * b, (t + 1) * b)
            band = slice(t * b, (t + 3) * b)
            logits = _dot(q_ref[0, rows, :], kcat[band], 1, 1) * scale + bias_h
            logits = jnp.where(_band_mask(i * tb + t, seq, half_window), logits, NEG_INF)
            m = jnp.maximum(jnp.max(logits, axis=1, keepdims=True), sk)
            e = jnp.exp(logits - m)
            ssum = jnp.sum(e, axis=1, keepdims=True) + jnp.exp(sk - m)
            o_ref[0, rows, :] = _dot(e / ssum, vcat[band], 1, 0)
            lse_ref[0, rows, :] = m + jnp.log(ssum)

    q_spec, prev, cur, nxt = _band_specs(d, group, tb, nblk)
    return pl.pallas_call(
        body,
        grid=(nq, nblk // tb),
        in_specs=[q_spec, prev, cur, nxt, prev, cur, nxt,
                  pl.BlockSpec(bias.shape, lambda n, i: (0, 0, 0)), pl.BlockSpec(sink.shape, lambda n, i: (0, 0, 0))],
        out_specs=[q_spec, pl.BlockSpec((1, tb * b, 1), lambda n, i: (n, i, 0))],
        out_shape=[jax.ShapeDtypeStruct((nq, seq, d), F32), jax.ShapeDtypeStruct((nq, seq, 1), F32)],
        name=name + "_fwd",
        compiler_params=_params("parallel", "parallel"),
    )(q, k, k, k, v, v, v, bias, sink)


def _band_bwd_call(q, k, v, bias, sink, o, lse, do, dlse, half_window, group, scale, name):
    nq, seq, d = q.shape
    nkv = nq // group
    hb = bias.shape[0]
    b = BAND_BLOCK
    nblk = seq // b
    tb = _pick(nblk, (4, 2, 1))
    nsteps = nblk // tb

    def body(q_ref, kp_ref, kc_ref, kn_ref, vp_ref, vc_ref, vn_ref, bias_ref, sink_ref, o_ref, lse_ref, do_ref,
             dlse_ref, dq_ref, dk_ref, dv_ref, dbias_ref, dsink_ref, dk_s, dv_s):
        n, i = pl.program_id(0), pl.program_id(1)
        head = n % hb

        @pl.when((n == 0) & (i == 0))
        def _():
            dbias_ref[...] = jnp.zeros_like(dbias_ref)
            dsink_ref[...] = jnp.zeros_like(dsink_ref)

        @pl.when((n % group == 0) & (i == 0))
        def _():
            dk_s[...] = jnp.zeros_like(dk_s)
            dv_s[...] = jnp.zeros_like(dv_s)

        kcat = jnp.concatenate([kp_ref[0], kc_ref[0], kn_ref[0]], axis=0)
        vcat = jnp.concatenate([vp_ref[0], vc_ref[0], vn_ref[0]], axis=0)
        bias_h = bias_ref[head]
        sk = sink_ref[head]
        for t in range(tb):
            rows = slice(t * b, (t + 1) * b)
            band = slice(t * b, (t + 3) * b)
            qt, dot_, lse_t = q_ref[0, rows, :], do_ref[0, rows, :], lse_ref[0, rows, :]
            logits = _dot(qt, kcat[band], 1, 1) * scale + bias_h
            logits = jnp.where(_band_mask(i * tb + t, seq, half_window), logits, NEG_INF)
            p = jnp.exp(logits - lse_t)
            dp = _dot(dot_, vcat[band], 1, 1)
            shift = dlse_ref[0, rows, :] - jnp.sum(dot_ * o_ref[0, rows, :], axis=1, keepdims=True)
            dlogits = p * (dp + shift)
            dbias_ref[head] += dlogits
            dsink_ref[head] += jnp.sum(jnp.exp(sk - lse_t) * shift, axis=0, keepdims=True)
            ds = dlogits * scale
            dq_ref[0, rows, :] = _dot(ds, kcat[band], 1, 0)
            win = pl.ds(pl.multiple_of((i * tb + t) * b, b), 3 * b)
            dk_s[win, :] += _dot(ds, qt, 0, 0)
            dv_s[win, :] += _dot(p, dot_, 0, 0)

        @pl.when((n % group == group - 1) & (i == nsteps - 1))
        def _():
            dk_ref[0] = dk_s[b:b + seq, :]
            dv_ref[0] = dv_s[b:b + seq, :]

    q_spec, prev, cur, nxt = _band_specs(d, group, tb, nblk)
    row1 = pl.BlockSpec((1, tb * b, 1), lambda n, i: (n, i, 0))
    kv_out = pl.BlockSpec((1, seq, d), lambda n, i: (n // group, 0, 0))
    return pl.pallas_call(
        body,
        grid=(nq, nsteps),
        in_specs=[q_spec, prev, cur, nxt, prev, cur, nxt,
                  pl.BlockSpec(bias.shape, lambda n, i: (0, 0, 0)), pl.BlockSpec(sink.shape, lambda n, i: (0, 0, 0)),
                  q_spec, row1, q_spec, row1],
        out_specs=[q_spec, kv_out, kv_out,
                   pl.BlockSpec(bias.shape, lambda n, i: (0, 0, 0)), pl.BlockSpec(sink.shape, lambda n, i: (0, 0, 0))],
        out_shape=[jax.ShapeDtypeStruct((nq, seq, d), F32), jax.ShapeDtypeStruct((nkv, seq, d), F32),
                   jax.ShapeDtypeStruct((nkv, seq, d), F32), jax.ShapeDtypeStruct(bias.shape, F32),
                   jax.ShapeDtypeStruct(sink.shape, F32)],
        scratch_shapes=[pltpu.VMEM((seq + 2 * b, d), F32), pltpu.VMEM((seq + 2 * b, d), F32)],
        name=name + "_bwd",
        compiler_params=_params("arbitrary", "arbitrary"),
    )(q, k, k, k, v, v, v, bias, sink, o, lse, do, dlse)


def _banded_attention(q, k, v, bias, sink, half_window, group, scale, name):
    @jax.custom_vjp
    def op(q, k, v, bias, sink):
        return tuple(_band_fwd_call(q, k, v, bias, sink, half_window, group, scale, name))

    def op_fwd(q, k, v, bias, sink):
        o, lse = _band_fwd_call(q, k, v, bias, sink, half_window, group, scale, name)
        return (o, lse), (q, k, v, bias, sink, o, lse)

    def op_bwd(res, cts):
        q, k, v, bias, sink, o, lse = res
        do, dlse = cts
        return tuple(_band_bwd_call(q, k, v, bias, sink, o, lse, do, dlse, half_window, group, scale, name))

    op.defvjp(op_fwd, op_bwd)
    return op(q, k, v, bias, sink)


def _loss_call(x, target, g):
    s, d = x.shape
    tr = _pick(s, (256, 128, 64, 32, 16, 8))

    def tile_loss(xt, gt, tt):
        err = jnp.square(_rms(xt, gt) - tt)
        return 0.5 * jnp.sum(jnp.mean(err, axis=-1, keepdims=True), axis=0, keepdims=True)

    def body(x_ref, t_ref, g_ref, loss_ref, dx_ref, dg_ref):
        tt = t_ref[...]
        val, vjp = jax.vjp(lambda xt, gt: tile_loss(xt, gt, tt), x_ref[...], g_ref[...])
        dx, dg = vjp(jnp.ones_like(val))
        dx_ref[...] = dx

        @pl.when(pl.program_id(0) == 0)
        def _():
            loss_ref[...] = jnp.zeros_like(loss_ref)
            dg_ref[...] = jnp.zeros_like(dg_ref)

        loss_ref[...] += val
        dg_ref[...] += dg

    return pl.pallas_call(
        body,
        grid=(s // tr,),
        in_specs=[_row_spec(tr, d), _row_spec(tr, d), _full_spec((1, d))],
        out_specs=[_full_spec((1, 1)), _row_spec(tr, d), _full_spec((1, d))],
        out_shape=[jax.ShapeDtypeStruct((1, 1), F32), jax.ShapeDtypeStruct((s, d), F32),
                   jax.ShapeDtypeStruct((1, d), F32)],
        name="final_norm_loss",
        compiler_params=_params("arbitrary"),
    )(x, target, g)


@jax.custom_vjp
def _loss_op(x, target, g):
    return _loss_call(x, target, g)[0][0, 0]


def _loss_op_fwd(x, target, g):
    loss, dx, dg = _loss_call(x, target, g)
    return loss[0, 0], (dx, dg, target)


def _loss_op_bwd(res, ct):
    dx, dg, target = res
    return ct * dx, jnp.zeros_like(target), ct * dg


_loss_op.defvjp(_loss_op_fwd, _loss_op_bwd)


def _norm_tile(r, a, p, c):
    return (_rms(r[0], p[0]),)


def _prep_mla_tile(r, a, p, c):
    a_q, a_kv, a_kr = r
    cos_t, sin_t = a
    g_q, g_kv, w_q, w_k, w_v = p
    (place_kr,) = c
    q = _rope(_bdot(_rms(a_q, g_q), w_q), cos_t, sin_t, MLA_ROPE // 2)
    ckv = _rms(a_kv, g_kv)
    k = _rope(_bdot(ckv, w_k) + _hdot(a_kr, place_kr), cos_t, sin_t, MLA_ROPE // 2)
    return q, k, _bdot(ckv, w_v)


def _head_rms(x, g, head_mean):
    return x * lax.rsqrt(_hdot(x * x, head_mean) + EPS) * g


def _prep_gqa_tile(r, a, p, c):
    b_q, b_k = r
    cos_t, sin_t = a
    g_q, g_k = p
    mean_q, mean_k = c
    wk = b_k.shape[1]
    q = _rope(_head_rms(b_q, g_q, mean_q), cos_t, sin_t, HEAD_DIM // 4)
    k = _rope(_head_rms(b_k, g_k, mean_k), cos_t[:, :wk], sin_t[:, :wk], HEAD_DIM // 4)
    return q, k


def _combine_tile(r, a, p, c):
    o0, o1, o2, l0, l1, l2 = r
    m = jnp.maximum(jnp.maximum(l0, l1), l2)
    e0, e1, e2 = jnp.exp(l0 - m), jnp.exp(l1 - m), jnp.exp(l2 - m)
    return ((e0 * o0 + e1 * o1 + e2 * o2) / (e0 + e1 + e2),)


def _merge_tile(r, a, p, c):
    y, gate_path, merge_logits = r
    (w_branch,) = p
    d = w_branch.shape[2]
    u = y * (gate_path * jax.nn.sigmoid(gate_path))
    out = None
    for nb in range(N_BRANCH):
        term = jax.nn.sigmoid(merge_logits[:, nb * d:(nb + 1) * d]) * _bdot(
            u[:, nb * BRANCH_W:(nb + 1) * BRANCH_W], w_branch[nb])
        out = term if out is None else out + term
    return (out,)


def _rope_angles(pos, dim):
    inv = ROPE_THETA ** (-jnp.arange(0, dim, 2, dtype=F32) / dim)
    return pos.astype(F32)[:, None] * inv[None, :]


def _rope_tables(s):
    pos = jnp.arange(s, dtype=jnp.int32)
    rows = s // GRID_W
    row_idx = jnp.repeat(jnp.arange(rows, dtype=jnp.int32), GRID_W)
    col_idx = jnp.tile(jnp.arange(GRID_W, dtype=jnp.int32), rows)
    a1 = _rope_angles(pos, MLA_ROPE)
    ar = _rope_angles(row_idx, HEAD_DIM // 2)
    ac = _rope_angles(col_idx, HEAD_DIM // 2)
    ones, zeros = jnp.ones((s, MLA_NOPE), F32), jnp.zeros((s, MLA_NOPE), F32)
    mla_cos = jnp.tile(jnp.concatenate([ones, jnp.cos(a1), jnp.cos(a1)], axis=1), (1, MLA_HEADS))
    mla_sin = jnp.tile(jnp.concatenate([zeros, -jnp.sin(a1), jnp.sin(a1)], axis=1), (1, MLA_HEADS))
    gqa_cos = jnp.tile(jnp.concatenate([jnp.cos(ar), jnp.cos(ar), jnp.cos(ac), jnp.cos(ac)], axis=1), (1, GQA_HEADS))
    gqa_sin = jnp.tile(jnp.concatenate([-jnp.sin(ar), jnp.sin(ar), -jnp.sin(ac), jnp.sin(ac)], axis=1), (1, GQA_HEADS))
    return mla_cos, mla_sin, gqa_cos, gqa_sin


def _t5_bucket(rel):
    nb = T5_BUCKETS // 2
    max_exact = nb // 2
    n = jnp.abs(rel)
    nf = jnp.maximum(n, 1).astype(F32)
    large = max_exact + (jnp.log(nf / max_exact) / math.log(T5_MAX_DIST / max_exact) * (nb - max_exact)).astype(jnp.int32)
    large = jnp.minimum(large, nb - 1)
    return jnp.where(rel > 0, nb, 0) + jnp.where(n < max_exact, n, large)


def _band_bias(table, stride, head_lo, heads):
    b = BAND_BLOCK
    offs = jnp.arange(3 * b)[None, :] - b - jnp.arange(b)[:, None]
    one_hot = (_t5_bucket(offs * stride)[..., None] == jnp.arange(T5_BUCKETS)).astype(F32)
    bias = jnp.dot(one_hot.reshape(b * 3 * b, T5_BUCKETS), table[:, head_lo:head_lo + heads],
                   precision=lax.Precision.HIGHEST)
    return bias.T.reshape(heads, b, 3 * b)


def _heads(t, h):
    s, w = t.shape
    return t.reshape(s, h, w // h).transpose(1, 0, 2)


def _unheads(t):
    h, s, d = t.shape
    return t.transpose(1, 0, 2).reshape(s, h * d)


def _layer(x, w, l, tabs, consts, biases):
    s, d_model = x.shape
    mla_cos, mla_sin, gqa_cos, gqa_sin = tabs
    place_kr, mean_q, mean_k = consts
    dil_bias, win_bias = biases

    xn = _rowop("norm", _norm_tile, 256, (d_model,), [x], [w["norm_g"][l][None, :]])[0]
    w_in = w["w_in"][l]
    lo_kr = MLA_Q_LORA + MLA_KV_LORA
    w_in_p = jnp.concatenate(
        [w_in[:, :lo_kr], w_in[:, lo_kr + MLA_ROPE:], w_in[:, lo_kr:lo_kr + MLA_ROPE],
         jnp.zeros((d_model, LANES - MLA_ROPE), F32)], axis=1)
    proj = _matmul(xn, w_in_p, "proj")
    widths = (MLA_Q_LORA, MLA_KV_LORA, 256, 128, 128, 768, 768, 768, 256, 128, 128,
              N_BRANCH * BRANCH_W, N_BRANCH * d_model, MLA_ROPE)
    pieces, at = [], 0
    for wd in widths:
        pieces.append(proj[:, at:at + wd])
        at += wd
    a_q, a_kv, b_q, b_k, b_v, c_q, c_k, c_v, d_q, d_k, d_v, gate_path, merge_logits, a_kr = pieces

    w_kv = w["w_mla_kv_up"][l].reshape(MLA_KV_LORA, MLA_HEADS, MLA_NOPE + MLA_V)
    w_k = jnp.concatenate([w_kv[:, :, :MLA_NOPE], jnp.zeros((MLA_KV_LORA, MLA_HEADS, MLA_ROPE), F32)], axis=2)
    w_k = w_k.reshape(MLA_KV_LORA, MLA_HEADS * MLA_QK)
    w_v = w_kv[:, :, MLA_NOPE:].reshape(MLA_KV_LORA, MLA_HEADS * MLA_V)
    q_a, k_a, v_a = _rowop(
        "prep_mla", _prep_mla_tile, 256, (MLA_HEADS * MLA_QK, MLA_HEADS * MLA_QK, MLA_HEADS * MLA_V),
        [a_q, a_kv, a_kr],
        [w["mla_q_norm_g"][l][None, :], w["mla_kv_norm_g"][l][None, :], w["w_mla_q_up"][l], w_k, w_v],
        [place_kr], [mla_cos, mla_sin])
    y_a = _unheads(_dense_attention(_heads(q_a, MLA_HEADS), _heads(k_a, MLA_HEADS), _heads(v_a, MLA_HEADS),
                                    MLA_QK ** -0.5, "mla"))

    q_b, k_b = _rowop(
        "prep_gqa", _prep_gqa_tile, 256, (GQA_HEADS * HEAD_DIM, GQA_KV_HEADS * HEAD_DIM),
        [b_q, b_k],
        [jnp.tile(w["gqa_q_norm_g"][l], GQA_HEADS)[None, :], jnp.tile(w["gqa_k_norm_g"][l], GQA_KV_HEADS)[None, :]],
        [mean_q, mean_k], [gqa_cos, gqa_sin])
    grp = GQA_HEADS // GQA_KV_HEADS
    o_b = _dense_attention(_heads(q_b, GQA_HEADS).reshape(GQA_KV_HEADS, grp * s, HEAD_DIM),
                           _heads(k_b, GQA_KV_HEADS), _heads(b_v, GQA_KV_HEADS), HEAD_DIM ** -0.5, "gqa")
    y_b = _unheads(o_b.reshape(GQA_HEADS, s, HEAD_DIM))

    no_sink = jnp.full((DIL_HEADS, 1, 1), NEG_INF, F32)
    outs, lses = [], []
    for gi, (window, dil) in enumerate(DIL_PATTERNS):
        sub = s // dil

        def to_sub(t):
            t = t[:, gi * DIL_HEADS * HEAD_DIM:(gi + 1) * DIL_HEADS * HEAD_DIM]
            return t.reshape(sub, dil, DIL_HEADS, HEAD_DIM).transpose(1, 2, 0, 3).reshape(dil * DIL_HEADS, sub, HEAD_DIM)

        o, lse = _banded_attention(to_sub(c_q), to_sub(c_k), to_sub(c_v), dil_bias[gi], no_sink,
                                   window // (2 * dil), 1, HEAD_DIM ** -0.5, "dil%d" % gi)

        def from_sub(t):
            dd = t.shape[2]
            return t.reshape(dil, DIL_HEADS, sub, dd).transpose(1, 2, 0, 3).reshape(DIL_HEADS * s, dd)

        outs.append(from_sub(o))
        lses.append(from_sub(lse))
    y_c = _rowop("dil_merge", _combine_tile, 512, (HEAD_DIM,), outs + lses, [])[0]
    y_c = _unheads(y_c.reshape(DIL_HEADS, s, HEAD_DIM))

    grp = WIN_HEADS // WIN_KV_HEADS
    o_d, _ = _banded_attention(_heads(d_q, WIN_HEADS), _heads(d_k, WIN_KV_HEADS), _heads(d_v, WIN_KV_HEADS),
                               win_bias, w["win_sink"][l].reshape(WIN_HEADS, 1, 1), WIN_HALF, grp,
                               HEAD_DIM ** -0.5, "win")
    y_d = _unheads(o_d)

    y = jnp.concatenate([y_a, y_b, y_c, y_d], axis=1)
    mix = _rowop("merge", _merge_tile, 128, (d_model,), [y, gate_path, merge_logits], [w["w_branch"][l]])[0]
    return x + _matmul(mix, w["w_out"][l], "out_proj")


def _local_loss(w, x, target):
    s = x.shape[0]
    tabs = _rope_tables(s)
    place = np.zeros((MLA_ROPE, MLA_HEADS * MLA_QK), np.float32)
    for h in range(MLA_HEADS):
        for i in range(MLA_ROPE):
            place[i, h * MLA_QK + MLA_NOPE + i] = 1.0

    def head_mean(nh):
        m = np.kron(np.eye(nh, dtype=np.float32), np.full((HEAD_DIM, HEAD_DIM), 1.0 / HEAD_DIM, np.float32))
        return jnp.asarray(m)

    consts = (jnp.asarray(place), head_mean(GQA_HEADS), head_mean(GQA_KV_HEADS))
    table = w["t5_table"]
    dil_bias = [_band_bias(table, dil, gi * DIL_HEADS, DIL_HEADS) for gi, (_, dil) in enumerate(DIL_PATTERNS)]
    win_bias = _band_bias(table, 1, len(DIL_PATTERNS) * DIL_HEADS, WIN_HEADS)
    depth = w["norm_g"].shape[0]
    for l in range(depth):
        x = _layer(x, w, l, tabs, consts, (dil_bias, win_bias))
    return _loss_op(x, target, w["final_norm_g"][None, :])


_ANY = pl.BlockSpec(memory_space=pl.ANY)
_MESH = pl.DeviceIdType.MESH


def _all_gather(block, name):
    def body(x_ref, out_ref, send_sems, recv_sems, local_sem):
        x, y, c = lax.axis_index("x"), lax.axis_index("y"), lax.axis_index("c")
        me, sibling = (x, y, c), (x, y, 1 - c)
        chips = [(1 - x, y), (x, 1 - y), (1 - x, 1 - y)]

        def slot(px, py, pc):
            return out_ref.at[4 * px + 2 * py + pc]

        def copy(k, blk, to, src=None):
            return pltpu.make_async_remote_copy(
                src_ref=slot(*blk) if src is None else src, dst_ref=slot(*blk),
                send_sem=send_sems.at[k], recv_sem=recv_sems.at[k], device_id=to, device_id_type=_MESH)

        mine = pltpu.make_async_copy(x_ref, slot(*me), local_sem)
        mine.start()
        first = [copy(0, me, sibling, src=x_ref)]
        first += [copy(1 + j, me, (*chip, c), src=x_ref) for j, chip in enumerate(chips)]
        for cp in first:
            cp.start()
        passed = [copy(4 + j, (*chip, c), sibling) for j, chip in enumerate(chips)]
        for j, chip in enumerate(chips):
            copy(1 + j, (*chip, c), me).wait_recv()
            passed[j].start()
        copy(0, sibling, me).wait_recv()
        for j, chip in enumerate(chips):
            copy(4 + j, (*chip, 1 - c), me).wait_recv()
        for cp in first + passed:
            cp.wait_send()
        mine.wait()

    return pl.pallas_call(
        body,
        out_shape=jax.ShapeDtypeStruct((N_DEV,) + block.shape, block.dtype),
        in_specs=[_ANY],
        out_specs=_ANY,
        scratch_shapes=[pltpu.SemaphoreType.DMA((7,)), pltpu.SemaphoreType.DMA((7,)), pltpu.SemaphoreType.DMA],
        name=name,
    )(block)


def _all_to_all(blocks, name):
    def body(x_ref, out_ref, send_sems, recv_sems, local_sem):
        x, y, c = lax.axis_index("x"), lax.axis_index("y"), lax.axis_index("c")
        me = 4 * x + 2 * y + c
        mine = pltpu.make_async_copy(x_ref.at[me], out_ref.at[me], local_sem)
        mine.start()
        copies, landed = [], []
        for k in range(1, N_DEV):
            px = 1 - x if k & 4 else x
            py = 1 - y if k & 2 else y
            pc = 1 - c if k & 1 else c
            peer = 4 * px + 2 * py + pc
            copies.append(pltpu.make_async_remote_copy(
                src_ref=x_ref.at[peer], dst_ref=out_ref.at[me], send_sem=send_sems.at[k - 1],
                recv_sem=recv_sems.at[k - 1], device_id=(px, py, pc), device_id_type=_MESH))
            landed.append(pltpu.make_async_remote_copy(
                src_ref=x_ref.at[peer], dst_ref=out_ref.at[peer], send_sem=send_sems.at[k - 1],
                recv_sem=recv_sems.at[k - 1], device_id=(px, py, pc), device_id_type=_MESH))
        for cp in copies:
            cp.start()
        for cp in landed:
            cp.wait_recv()
        for cp in copies:
            cp.wait_send()
        mine.wait()

    return pl.pallas_call(
        body,
        out_shape=jax.ShapeDtypeStruct(blocks.shape, blocks.dtype),
        in_specs=[_ANY],
        out_specs=_ANY,
        scratch_shapes=[pltpu.SemaphoreType.DMA((7,)), pltpu.SemaphoreType.DMA((7,)), pltpu.SemaphoreType.DMA],
        name=name,
    )(blocks)


def _sum_slots(parts, name):
    _, rows, w = parts.shape
    tr = _pick(rows, (1024, 512, 256, 128, 64, 32, 16, 8))

    def body(p_ref, o_ref):
        acc = p_ref[0].astype(F32)
        for j in range(1, N_DEV):
            acc = acc + p_ref[j].astype(F32)
        o_ref[...] = acc

    return pl.pallas_call(
        body,
        grid=(rows // tr,),
        in_specs=[pl.BlockSpec((N_DEV, tr, w), lambda i: (0, i, 0))],
        out_specs=pl.BlockSpec((tr, w), lambda i: (i, 0)),
        out_shape=jax.ShapeDtypeStruct((rows, w), F32),
        name=name,
        compiler_params=_params("parallel"),
    )(parts)


def _adamw(w, g, m, v, name):
    rows, width = w.shape
    tr = _pick(rows, (1024, 512, 256, 128, 64, 32, 16, 8))

    def body(w_ref, g_ref, m_ref, v_ref, d_ref, nm_ref, nv_ref):
        g_ = g_ref[...]
        m_ = ADAM_B1 * m_ref[...] + (1.0 - ADAM_B1) * g_
        v_ = ADAM_B2 * v_ref[...] + (1.0 - ADAM_B2) * jnp.square(g_)
        m_hat = m_ / (1.0 - ADAM_B1 ** ADAM_STEP)
        v_hat = v_ / (1.0 - ADAM_B2 ** ADAM_STEP)
        d_ref[...] = -ADAM_LR * (m_hat / (jnp.sqrt(v_hat) + ADAM_EPS) + ADAM_WD * w_ref[...])
        nm_ref[...] = m_
        nv_ref[...] = v_

    spec = pl.BlockSpec((tr, width), lambda i: (i, 0))
    return pl.pallas_call(
        body,
        grid=(rows // tr,),
        in_specs=[spec] * 4,
        out_specs=[spec] * 3,
        out_shape=[jax.ShapeDtypeStruct((rows, width), F32)] * 3,
        name=name,
        compiler_params=_params("parallel"),
    )(w, g, m, v)


_SHARDED = (("w_in", 2), ("w_mla_q_up", 2), ("w_mla_kv_up", 2), ("w_branch", 3), ("w_out", 1))
_REPLICATED = ("norm_g", "mla_q_norm_g", "mla_kv_norm_g", "gqa_q_norm_g", "gqa_k_norm_g", "win_sink", "t5_table",
               "final_norm_g")


def _pack(arrays, row_multiple):
    flat = jnp.concatenate([a.reshape(-1) for a in arrays])
    rows = -(-flat.shape[0] // (LANES * row_multiple)) * row_multiple
    return jnp.pad(flat, (0, rows * LANES - flat.shape[0])).reshape(rows, LANES)


def _unpack(packed, shapes):
    flat, out, at = packed.reshape(-1), [], 0
    for shp in shapes:
        n = int(np.prod(shp))
        out.append(flat[at:at + n].reshape(shp))
        at += n
    return out


def _split_shards(full, axis):
    shp = full.shape
    t = full.reshape(shp[:axis] + (N_DEV, shp[axis] // N_DEV) + shp[axis + 1:])
    return jnp.moveaxis(t, axis, 0)


def _join_shards(stacked, axis):
    t = jnp.moveaxis(stacked, 0, axis)
    shp = t.shape
    return t.reshape(shp[:axis] + (shp[axis] * shp[axis + 1],) + shp[axis + 2:])


def kernel(x, norm_g, w_in, mla_q_norm_g, mla_kv_norm_g, w_mla_q_up, w_mla_kv_up, gqa_q_norm_g, gqa_k_norm_g, win_sink, t5_table, w_branch, w_out, final_norm_g, loss_target, m_norm_g, m_w_in, m_mla_q_norm_g, m_mla_kv_norm_g, m_w_mla_q_up, m_w_mla_kv_up, m_gqa_q_norm_g, m_gqa_k_norm_g, m_win_sink, m_t5_table, m_w_branch, m_w_out, m_final_norm_g, v_norm_g, v_w_in, v_mla_q_norm_g, v_mla_kv_norm_g, v_w_mla_q_up, v_w_mla_kv_up, v_gqa_q_norm_g, v_gqa_k_norm_g, v_win_sink, v_t5_table, v_w_branch, v_w_out, v_final_norm_g):
    given = dict(locals())
    names = ("norm_g", "w_in", "mla_q_norm_g", "mla_kv_norm_g", "w_mla_q_up", "w_mla_kv_up", "gqa_q_norm_g",
             "gqa_k_norm_g", "win_sink", "t5_table", "w_branch", "w_out", "final_norm_g")
    shard_names = [n for n, _ in _SHARDED]
    shard_shapes = [given[n].shape for n in shard_names]

    mine = _pack([given[n] for n in shard_names], 16)
    gathered = _all_gather(mine.astype(BF16), "gather_weights").astype(F32)
    per_dev = [_unpack(gathered[j], shard_shapes) for j in range(N_DEV)]
    weights = {n: given[n] for n in _REPLICATED}
    for wi, (n, axis) in enumerate(_SHARDED):
        weights[n] = _join_shards(jnp.stack([per_dev[j][wi] for j in range(N_DEV)]), axis)

    loss, (gw, gx) = jax.value_and_grad(_local_loss, argnums=(0, 1))(weights, x[0], loss_target[0])
    loss = lax.psum(loss, ("x", "y", "c"))

    blocks = [_split_shards(gw[n], axis) for n, axis in _SHARDED]
    send = jnp.stack([_pack([b[j] for b in blocks], 16) for j in range(N_DEV)])
    g_shard = _unpack(_sum_slots(_all_to_all(send.astype(BF16), "scatter_grads"), "sum_grads"), shard_shapes)
    rep_shapes = [given[n].shape for n in _REPLICATED]
    g_rep = _unpack(_sum_slots(_all_gather(_pack([gw[n] for n in _REPLICATED], 8), "gather_small_grads"),
                               "sum_small_grads"), rep_shapes)
    grads = dict(zip(shard_names, g_shard))
    grads.update(zip(_REPLICATED, g_rep))

    def update(group, shapes, row_multiple, name):
        outs = _adamw(*[_pack([src[n] for n in group], row_multiple) for src in (
            given, grads, {n: given["m_" + n] for n in group}, {n: given["v_" + n] for n in group})], name)
        return [dict(zip(group, _unpack(o, shapes))) for o in outs]

    big = update(shard_names, shard_shapes, 16, "adamw_shards")
    small = update(list(_REPLICATED), rep_shapes, 8, "adamw_replicated")
    delta, new_m, new_v = [{**b, **s_} for b, s_ in zip(big, small)]
    return (loss, gx[None], *[grads[n] for n in names], *[delta[n] for n in names],
            *[new_m[n] for n in names], *[new_v[n] for n in names])
```

```python
import functools
import math

import jax
import jax.numpy as jnp
import numpy as np
from jax import lax
from jax.experimental import pallas as pl
from jax.experimental.pallas import tpu as pltpu

F32 = jnp.float32
BF16 = jnp.bfloat16
N_DEV = 8
LANES = 128
HALF = LANES // 2
V7X_VMEM_LIMIT = 56 * 1024 * 1024

EPS = 1e-6
NEG_INF = -1e30
LOG2E = 1.4426950408889634
ROPE_THETA = 10000.0
GRID_W = 64
HEAD_DIM = 64
N_BRANCH = 4
BRANCH_W = 256
MLA_HEADS, MLA_Q_LORA, MLA_KV_LORA, MLA_NOPE, MLA_ROPE, MLA_V = 4, 256, 128, 64, 32, 64
MLA_QK = MLA_NOPE + MLA_ROPE
GQA_HEADS, GQA_KV_HEADS = 4, 2
DIL_PATTERNS = ((128, 1), (512, 4), (2048, 16))
DIL_HEADS = 4
WIN_HEADS, WIN_KV_HEADS, WIN_HALF = 4, 2, 128
T5_BUCKETS, T5_MAX_DIST = 32, 1024
BAND_BLOCK = 128
ADAM_LR, ADAM_B1, ADAM_B2, ADAM_EPS, ADAM_WD, ADAM_STEP = 0.001, 0.9, 0.999, 1e-08, 0.01, 10

D_MODEL = 1024
GM_W, SMALL_W, BAND_W = 5120, 512, 768
MLA_BLK, GQA_BLK, WIN_BLK, DIL_BLK = 10, 11, 8, 9
P_TOT = 9216
QW = 256


def _params(*sem):
    return pltpu.CompilerParams(dimension_semantics=sem, vmem_limit_bytes=V7X_VMEM_LIMIT)


def _pick(n, cands):
    for c in cands:
        if n % c == 0:
            return c
    return n


def _dot(a, b, ca, cb):
    return lax.dot_general(a.astype(BF16), b.astype(BF16), (((ca,), (cb,)), ((), ())), preferred_element_type=F32)


def _bmm(a, b, ca, cb):
    return lax.dot_general(a, b, (((ca,), (cb,)), ((0,), (0,))), preferred_element_type=F32)


@jax.custom_vjp
def _bdot(a, b):
    return _dot(a, b, 1, 0)


def _bdot_fwd(a, b):
    return _dot(a, b, 1, 0), (a, b)


def _bdot_bwd(res, g):
    a, b = res
    return _dot(g, b, 1, 1), _dot(a, g, 0, 0)


_bdot.defvjp(_bdot_fwd, _bdot_bwd)


def _hdot(a, c):
    return lax.dot_general(a, c, (((1,), (0,)), ((), ())), precision=lax.Precision.HIGHEST, preferred_element_type=F32)


@functools.partial(jax.custom_vjp, nondiff_argnums=(1,))
def _lane_roll(x, shift):
    return pltpu.roll(x, shift, 1)


def _lane_roll_fwd(x, shift):
    return pltpu.roll(x, shift, 1), None


def _lane_roll_bwd(shift, _, g):
    return (pltpu.roll(g, g.shape[1] - shift, 1),)


_lane_roll.defvjp(_lane_roll_fwd, _lane_roll_bwd)


@functools.partial(jax.custom_vjp, nondiff_argnums=(1,))
def _lane_ranges(x, cut):
    bounds, _ = cut
    return tuple(x[:, lo:hi] for lo, hi in zip(bounds[:-1], bounds[1:]))


def _lane_ranges_fwd(x, cut):
    return _lane_ranges(x, cut), None


def _lane_ranges_bwd(cut, _, cts):
    bounds, width = cut
    parts = list(cts)
    if bounds[-1] < width:
        parts.append(jnp.zeros((cts[0].shape[0], width - bounds[-1]), cts[0].dtype))
    return (jnp.concatenate(parts, axis=1),)


_lane_ranges.defvjp(_lane_ranges_fwd, _lane_ranges_bwd)


def _lanes(x, bounds):
    return _lane_ranges(x, (tuple(bounds), x.shape[1]))


@jax.custom_vjp
def _unstack(x):
    return tuple(x[i] for i in range(x.shape[0]))


def _unstack_fwd(x):
    return _unstack(x), None


def _unstack_bwd(_, cts):
    return (jnp.stack(cts, axis=0),)


_unstack.defvjp(_unstack_fwd, _unstack_bwd)


def _split_heads(x, h):
    d = x.shape[1] // h
    return jnp.stack(_lanes(x, tuple(range(0, x.shape[1] + 1, d))), axis=0)


def _join_heads(x):
    return jnp.concatenate(_unstack(x), axis=1)


def _rope(x, cos_t, sin_t, half):
    w = x.shape[1]
    lane = lax.broadcasted_iota(jnp.int32, (1, w), 1)
    first = (lane % (2 * half)) < half
    partner = jnp.where(first, _lane_roll(x, w - half), _lane_roll(x, half))
    return x * cos_t + partner * sin_t


def _rms(x, g):
    return x * lax.rsqrt(jnp.mean(x * x, axis=-1, keepdims=True) + EPS) * g


def _rows(tr, w, col=0):
    return pl.BlockSpec((tr, w), lambda i: (i, col))


def _head_rows(h, tr, d):
    return pl.BlockSpec((h, tr, d), lambda i: (0, i, 0))


def _whole(shape):
    nd = len(shape)
    return pl.BlockSpec(tuple(shape), lambda i: (0,) * nd)


def _fwd_call(name, fn, steps, rows, params, aux, outs):
    nr, npar, na = len(rows), len(params), len(aux)

    def body(*refs):
        vals = [x[...] for x in refs[:nr + npar + na]]
        res = fn(vals[:nr], vals[nr:nr + npar], vals[nr + npar:])
        for o_ref, o in zip(refs[nr + npar + na:], res):
            o_ref[...] = o

    return pl.pallas_call(
        body,
        grid=(steps,),
        in_specs=[s for _, s in rows] + [_whole(p.shape) for p in params] + [s for _, s in aux],
        out_specs=[s for _, s in outs],
        out_shape=[jax.ShapeDtypeStruct(shp, F32) for shp, _ in outs],
        name=name + "_fwd",
        compiler_params=_params("parallel"),
    )(*[a for a, _ in rows], *params, *[a for a, _ in aux])


def _vjp_call(name, fn, steps, rows, params, aux, cts, row_grads, into=None):
    nr, npar, na, nc = len(rows), len(params), len(aux), len(cts)
    n_in = nr + npar + na + nc
    lead = 0 if into is None else 1

    def body(*refs):
        refs = refs[lead:]
        vals = [x[...] for x in refs[:n_in]]
        r, p, a, d = vals[:nr], vals[nr:nr + npar], vals[nr + npar:nr + npar + na], vals[nr + npar + na:]
        out_refs = refs[n_in:]
        _, vjp = jax.vjp(lambda r_, p_: tuple(fn(r_, p_, a)), r, p)
        dr, dp = vjp(tuple(d))
        for o_ref, o in zip(out_refs[:nr], dr):
            o_ref[...] = o

        @pl.when(pl.program_id(0) == 0)
        def _():
            for o_ref in out_refs[nr:]:
                o_ref[...] = jnp.zeros_like(o_ref)

        for o_ref, o in zip(out_refs[nr:], dp):
            o_ref[...] += o

    outs = pl.pallas_call(
        body,
        grid=(steps,),
        in_specs=([] if into is None else [pl.BlockSpec(memory_space=pl.ANY)])
        + [s for _, s in rows] + [_whole(p.shape) for p in params] + [s for _, s in aux] + [s for _, s in cts],
        out_specs=[s for _, s in row_grads] + [_whole(p.shape) for p in params],
        out_shape=[jax.ShapeDtypeStruct(shp, F32) for shp, _ in row_grads]
        + [jax.ShapeDtypeStruct(p.shape, F32) for p in params],
        input_output_aliases={} if into is None else {0: 0},
        name=name + "_bwd",
        compiler_params=_params("arbitrary"),
    )(*([] if into is None else [into]), *[a for a, _ in rows], *params, *[a for a, _ in aux], *[a for a, _ in cts])
    return list(outs[:nr]), list(outs[nr:])


def _norm(x, g):
    s, d = x.shape
    tr = _pick(s, (256, 128, 64, 32, 16, 8))

    def tile(r, p, a):
        return (_rms(r[0], p[0]),)

    @jax.custom_vjp
    def op(x, g):
        return _fwd_call("norm", tile, s // tr, [(x, _rows(tr, d))], [g], [], [((s, d), _rows(tr, d))])[0]

    def op_fwd(x, g):
        return op(x, g), (x, g)

    def op_bwd(res, dy):
        x, g = res
        (dx,), (dg,) = _vjp_call("norm", tile, s // tr, [(x, _rows(tr, d))], [g], [], [(dy, _rows(tr, d))],
                                 [((s, d), _rows(tr, d))])
        return dx, dg

    op.defvjp(op_fwd, op_bwd)
    return op(x, g)


def _mm(a, b, mode, name):
    if mode == "nn":
        (m, k), n = a.shape, b.shape[1]
    elif mode == "nt":
        (m, k), n = a.shape, b.shape[0]
    else:
        (k, m), n = a.shape, b.shape[1]
    tm = _pick(m, (512, 256, 128))
    tn = _pick(n, (1024, 768, 512, 384, 256, 128))
    tk = _pick(k, (1024, 768, 512, 384, 256, 128))
    if mode == "tn":
        tm, tk = _pick(m, (1024, 512, 256, 128)), _pick(k, (512, 256, 128))
    nk = k // tk

    def body(a_ref, b_ref, o_ref, acc_ref):
        kk = pl.program_id(2)
        if mode == "nn":
            part = _dot(a_ref[...], b_ref[...], 1, 0)
        elif mode == "nt":
            part = _dot(a_ref[...], b_ref[...], 1, 1)
        else:
            part = _dot(a_ref[...], b_ref[...], 0, 0)
        if nk == 1:
            o_ref[...] = part
        else:
            @pl.when(kk == 0)
            def _():
                acc_ref[...] = part

            @pl.when(kk > 0)
            def _():
                acc_ref[...] += part

            @pl.when(kk == nk - 1)
            def _():
                o_ref[...] = acc_ref[...]

    if mode == "nn":
        a_spec = pl.BlockSpec((tm, tk), lambda i, j, kk: (i, kk))
        b_spec = pl.BlockSpec((tk, tn), lambda i, j, kk: (kk, j))
    elif mode == "nt":
        a_spec = pl.BlockSpec((tm, tk), lambda i, j, kk: (i, kk))
        b_spec = pl.BlockSpec((tn, tk), lambda i, j, kk: (j, kk))
    else:
        a_spec = pl.BlockSpec((tk, tm), lambda i, j, kk: (kk, i))
        b_spec = pl.BlockSpec((tk, tn), lambda i, j, kk: (kk, j))
    return pl.pallas_call(
        body,
        grid=(m // tm, n // tn, nk),
        in_specs=[a_spec, b_spec],
        out_specs=pl.BlockSpec((tm, tn), lambda i, j, kk: (i, j)),
        out_shape=jax.ShapeDtypeStruct((m, n), F32),
        scratch_shapes=[pltpu.VMEM((tm, tn), F32)],
        name=name,
        compiler_params=_params("parallel", "parallel", "arbitrary"),
    )(a, b)


def _matmul(a, b, name):
    @jax.custom_vjp
    def op(a, b):
        return _mm(a, b, "nn", name + "_nn")

    def op_fwd(a, b):
        return op(a, b), (a, b)

    def op_bwd(res, g):
        a, b = res
        return _mm(g, b, "nt", name + "_nt"), _mm(a, g, "tn", name + "_tn")

    op.defvjp(op_fwd, op_bwd)
    return op(a, b)


def _dense_fwd_call(q, k, v, scale, name):
    n, sq, d = q.shape
    sk, dv = k.shape[1], v.shape[2]
    tq = _pick(sq, (256, 128))
    c = scale * LOG2E

    def body(q_ref, k_ref, v_ref, o_ref, lse_ref, k_s, vext_s):
        @pl.when(pl.program_id(1) == 0)
        def _():
            k_s[...] = k_ref[0].astype(BF16)
            vext_s[...] = jnp.ones_like(vext_s)
            vext_s[:, :dv] = v_ref[0].astype(BF16)

        s = _dot(q_ref[0], k_s[...], 1, 1)
        m = jnp.max(s, axis=1, keepdims=True)
        p = jnp.exp2(s * c - m * c)
        acc = _dot(p, vext_s[...], 1, 0)
        l = acc[:, dv:dv + 1]
        o_ref[0] = acc[:, :dv] / l
        lse_ref[0] = m * scale + jnp.log(l)

    return pl.pallas_call(
        body,
        grid=(n, sq // tq),
        in_specs=[
            pl.BlockSpec((1, tq, d), lambda h, i: (h, i, 0)),
            pl.BlockSpec((1, sk, d), lambda h, i: (h, 0, 0)),
            pl.BlockSpec((1, sk, dv), lambda h, i: (h, 0, 0)),
        ],
        out_specs=[
            pl.BlockSpec((1, tq, dv), lambda h, i: (h, i, 0)),
            pl.BlockSpec((1, tq, 1), lambda h, i: (h, i, 0)),
        ],
        out_shape=[jax.ShapeDtypeStruct((n, sq, dv), F32), jax.ShapeDtypeStruct((n, sq, 1), F32)],
        scratch_shapes=[pltpu.VMEM((sk, d), BF16), pltpu.VMEM((sk, 2 * dv), BF16)],
        name=name + "_fwd",
        compiler_params=_params("arbitrary", "arbitrary"),
    )(q, k, v)


def _dense_bwd_call(q, k, v, o, lse, do, scale, name):
    n, sq, d = q.shape
    sk, dv = k.shape[1], v.shape[2]
    tq, tk = _pick(sq, (512, 256, 128)), _pick(sk, (2048, 1024, 512, 256, 128))
    c = scale * LOG2E

    def body(q_ref, k_ref, v_ref, o_ref, lse_ref, do_ref, dq_ref, dk_ref, dv_ref):
        j, i = pl.program_id(1), pl.program_id(2)
        qb, kb, vb = q_ref[0].astype(BF16), k_ref[0].astype(BF16), v_ref[0].astype(BF16)
        do_f = do_ref[0]
        dob = do_f.astype(BF16)
        p = jnp.exp2(_dot(qb, kb, 1, 1) * c - lse_ref[0] * LOG2E)
        delta = jnp.sum(do_f * o_ref[0], axis=1, keepdims=True)
        ds = (p * (_dot(dob, vb, 1, 1) - delta)).astype(BF16)
        dv_part = _dot(p, dob, 0, 0)
        dk_part = _dot(ds, qb, 0, 0) * scale
        dq_part = _dot(ds, kb, 1, 0) * scale
        rows = pl.ds(pl.multiple_of(i * tq, tq), tq)

        @pl.when(i == 0)
        def _():
            dk_ref[0] = dk_part
            dv_ref[0] = dv_part

        @pl.when(i > 0)
        def _():
            dk_ref[0] += dk_part
            dv_ref[0] += dv_part

        @pl.when(j == 0)
        def _():
            dq_ref[0, rows, :] = dq_part

        @pl.when(j > 0)
        def _():
            dq_ref[0, rows, :] += dq_part

    return pl.pallas_call(
        body,
        grid=(n, sk // tk, sq // tq),
        in_specs=[
            pl.BlockSpec((1, tq, d), lambda h, j, i: (h, i, 0)),
            pl.BlockSpec((1, tk, d), lambda h, j, i: (h, j, 0)),
            pl.BlockSpec((1, tk, dv), lambda h, j, i: (h, j, 0)),
            pl.BlockSpec((1, tq, dv), lambda h, j, i: (h, i, 0)),
            pl.BlockSpec((1, tq, 1), lambda h, j, i: (h, i, 0)),
            pl.BlockSpec((1, tq, dv), lambda h, j, i: (h, i, 0)),
        ],
        out_specs=[
            pl.BlockSpec((1, sq, d), lambda h, j, i: (h, 0, 0)),
            pl.BlockSpec((1, tk, d), lambda h, j, i: (h, j, 0)),
            pl.BlockSpec((1, tk, dv), lambda h, j, i: (h, j, 0)),
        ],
        out_shape=[
            jax.ShapeDtypeStruct((n, sq, d), F32),
            jax.ShapeDtypeStruct((n, sk, d), F32),
            jax.ShapeDtypeStruct((n, sk, dv), F32),
        ],
        name=name + "_bwd",
        compiler_params=_params("arbitrary", "arbitrary", "arbitrary"),
    )(q, k, v, o, lse, do)


def _head_geometry(h, group):
    pair, a = divmod(h, 2)
    kv_pair, b = divmod(h // group, 2)
    return pair, a, kv_pair, b


def _lane_half():
    return lax.broadcasted_iota(jnp.int32, (1, LANES), 1) // HALF


def _align(x, a, b):
    if a != b:
        x = pltpu.roll(x, HALF, 1)
    return jnp.where(_lane_half() == b, x, 0.0)


def _unalign(x, a, b):
    x = jnp.where(_lane_half() == b, x, 0.0)
    return pltpu.roll(x, HALF, 1) if a != b else x


def _bands(w, pw, nw, lo, kvw, nb):
    b = BAND_BLOCK
    cat = jnp.concatenate([pw[:, lo:lo + kvw], w[:, lo:lo + kvw], nw[:, lo:lo + kvw]], axis=0).astype(BF16)
    out = []
    for g in range(kvw // LANES):
        c3 = cat[:, g * LANES:(g + 1) * LANES].reshape(nb + 2, b, LANES)
        out.append(jnp.concatenate([c3[0:nb], c3[1:nb + 1], c3[2:nb + 2]], axis=1))
    return out


def _edge_mask(first_block, nb, n_blocks):
    b = BAND_BLOCK
    blk = first_block + lax.broadcasted_iota(jnp.int32, (nb, 1, 3 * b), 0)
    col = lax.broadcasted_iota(jnp.int32, (nb, 1, 3 * b), 2)
    outside = ((col < b) & (blk == 0)) | ((col >= 2 * b) & (blk == n_blocks - 1))
    return jnp.where(outside, NEG_INF, 0.0)


def _band_geometry(proj, dil):
    seq = proj.shape[0] // dil
    tl = _pick(seq, (1024, 512, 256, 128))
    return seq, tl, tl // BAND_BLOCK, seq // tl, P_TOT // BAND_W


def _band_in_specs(tl, nb, n_chunks, n_blocks, cols, col, last_step_idle):
    def chunk(i):
        return jnp.minimum(i, n_chunks - 1) if last_step_idle else i

    main = pl.BlockSpec((tl, BAND_W), lambda j, i: (chunk(i), j * cols + col))
    prev = pl.BlockSpec((BAND_BLOCK, BAND_W), lambda j, i: (jnp.maximum(chunk(i) * nb - 1, 0), j * cols + col))
    nxt = pl.BlockSpec((BAND_BLOCK, BAND_W),
                       lambda j, i: (jnp.minimum((chunk(i) + 1) * nb, n_blocks - 1), j * cols + col))
    rows = pl.BlockSpec((tl, QW), lambda j, i: (chunk(i), j))
    return main, prev, nxt, rows


def _band_fwd_call(proj, col, bias, sink, dil, group, kvw, scale, name):
    s_tok = proj.shape[0]
    seq, tl, nb, n_chunks, cols = _band_geometry(proj, dil)
    n_blocks = seq // BAND_BLOCK
    heads = bias.shape[0]

    def body(w_ref, pw_ref, nw_ref, bias_ref, sink_ref, o_ref, lse_ref):
        i = pl.program_id(1)
        w, pw, nw = w_ref[...], pw_ref[...], nw_ref[...]
        kb = _bands(w, pw, nw, QW, kvw, nb)
        vb = _bands(w, pw, nw, QW + kvw, kvw, nb)
        edge = _edge_mask(i * nb, nb, n_blocks)
        o_acc = [jnp.zeros((tl, LANES), F32) for _ in range(heads // 2)]
        lse_acc = [jnp.zeros((tl, LANES), F32) for _ in range(heads // 2)]
        for h in range(heads):
            pair, a, kvp, b = _head_geometry(h, group)
            q_al = _align(w[:, pair * LANES:(pair + 1) * LANES], a, b).astype(BF16).reshape(nb, BAND_BLOCK, LANES)
            logits = _bmm(q_al, kb[kvp], 2, 2) * scale + bias_ref[h][None] + edge
            sk = sink_ref[h].reshape(1, 1, 1)
            m = jnp.maximum(jnp.max(logits, axis=2, keepdims=True), sk)
            e = jnp.exp(logits - m)
            ssum = jnp.sum(e, axis=2, keepdims=True) + jnp.exp(sk - m)
            out = _bmm(e.astype(BF16), vb[kvp], 2, 1) / ssum
            o_acc[pair] = o_acc[pair] + _unalign(out.reshape(tl, LANES), a, b)
            lse = (m + jnp.log(ssum)).reshape(tl, 1)
            lse_acc[pair] = lse_acc[pair] + jnp.where(_lane_half() == a, lse, 0.0)
        o_ref[...] = jnp.concatenate(o_acc, axis=1)
        lse_ref[...] = jnp.concatenate(lse_acc, axis=1)

    main, prev, nxt, rows = _band_in_specs(tl, nb, n_chunks, n_blocks, cols, col, False)
    o, lse = pl.pallas_call(
        body,
        grid=(dil, n_chunks),
        in_specs=[main, prev, nxt, pl.BlockSpec(bias.shape, lambda j, i: (0, 0, 0)),
                  pl.BlockSpec(sink.shape, lambda j, i: (0, 0, 0))],
        out_specs=[rows, rows],
        out_shape=[jax.ShapeDtypeStruct((seq, dil * QW), F32)] * 2,
        name=name + "_fwd",
        compiler_params=_params("parallel", "parallel"),
    )(proj.reshape(seq, dil * P_TOT), proj.reshape(seq, dil * P_TOT), proj.reshape(seq, dil * P_TOT), bias, sink)
    return o.reshape(s_tok, QW), lse.reshape(s_tok, QW)


def _band_bwd_call(proj, o, do, lse, dlse, bias, sink, dproj, col, dil, group, kvw, scale, name):
    s_tok = proj.shape[0]
    seq, tl, nb, n_chunks, cols = _band_geometry(proj, dil)
    n_blocks = seq // BAND_BLOCK
    heads = bias.shape[0]
    b_ = BAND_BLOCK
    have_dlse = dlse is not None

    def body(*refs):
        (_, w_ref, pw_ref, nw_ref, o_ref, do_ref, lse_ref), refs = refs[:7], refs[7:]
        if have_dlse:
            dlse_ref, refs = refs[0], refs[1:]
        bias_ref, sink_ref, dwin_ref, dbias_ref, dsink_ref, dq_s, dk_s, dv_s = refs
        j, i = pl.program_id(0), pl.program_id(1)

        @pl.when((j == 0) & (i == 0))
        def _():
            dbias_ref[...] = jnp.zeros_like(dbias_ref)
            dsink_ref[...] = jnp.zeros_like(dsink_ref)

        @pl.when(i == 0)
        def _():
            dk_s[...] = jnp.zeros_like(dk_s)
            dv_s[...] = jnp.zeros_like(dv_s)

        @pl.when(i < n_chunks)
        def _():
            w, pw, nw = w_ref[...], pw_ref[...], nw_ref[...]
            kb = _bands(w, pw, nw, QW, kvw, nb)
            vb = _bands(w, pw, nw, QW + kvw, kvw, nb)
            edge = _edge_mask(i * nb, nb, n_blocks)
            dq_acc = [jnp.zeros((tl, LANES), F32) for _ in range(heads // 2)]
            for h in range(heads):
                pair, a, kvp, b = _head_geometry(h, group)
                lanes = slice(pair * LANES, (pair + 1) * LANES)
                mine = _lane_half() == a
                q_al = _align(w[:, lanes], a, b).astype(BF16).reshape(nb, b_, LANES)
                do_al = _align(do_ref[:, lanes], a, b).astype(BF16).reshape(nb, b_, LANES)
                lse_h = jnp.max(jnp.where(mine, lse_ref[:, lanes], NEG_INF), axis=1, keepdims=True)
                shift = -jnp.sum(jnp.where(mine, do_ref[:, lanes] * o_ref[:, lanes], 0.0), axis=1, keepdims=True)
                if have_dlse:
                    shift = shift + jnp.sum(jnp.where(mine, dlse_ref[:, lanes], 0.0), axis=1, keepdims=True)
                logits = _bmm(q_al, kb[kvp], 2, 2) * scale + bias_ref[h][None] + edge
                p = jnp.exp(logits - lse_h.reshape(nb, b_, 1))
                dlogits = p * (_bmm(do_al, vb[kvp], 2, 2) + shift.reshape(nb, b_, 1))
                dbias_ref[h] += jnp.sum(dlogits, axis=0)
                dsink_ref[h] += jnp.sum(jnp.exp(sink_ref[h] - lse_h) * shift, axis=0, keepdims=True)
                ds = (dlogits * scale).astype(BF16)
                dq_acc[pair] = dq_acc[pair] + _unalign(_bmm(ds, kb[kvp], 2, 1).reshape(tl, LANES), a, b)
                dk_band = _bmm(ds, q_al, 1, 1)
                dv_band = _bmm(p.astype(BF16), do_al, 1, 1)
                kv_lanes = slice(kvp * LANES, (kvp + 1) * LANES)
                for t in range(3):
                    at = pl.ds(pl.multiple_of(i * tl + t * b_, b_), tl)
                    dk_s[at, kv_lanes] += dk_band[:, t * b_:(t + 1) * b_, :].reshape(tl, LANES)
                    dv_s[at, kv_lanes] += dv_band[:, t * b_:(t + 1) * b_, :].reshape(tl, LANES)
            dq_s[lax.rem(i, 2)] = jnp.concatenate(dq_acc, axis=1)

        @pl.when(i >= 1)
        def _():
            at = pl.ds(pl.multiple_of((i - 1) * tl + b_, b_), tl)
            parts = [dq_s[lax.rem(i + 1, 2)], dk_s[at, :], dv_s[at, :]]
            if QW + 2 * kvw < BAND_W:
                parts.append(jnp.zeros((tl, BAND_W - QW - 2 * kvw), F32))
            dwin_ref[...] = jnp.concatenate(parts, axis=1)

    main, prev, nxt, rows = _band_in_specs(tl, nb, n_chunks, n_blocks, cols, col, True)
    view = proj.reshape(seq, dil * P_TOT)
    row_args = [a.reshape(seq, dil * QW) for a in ([o, do, lse] + ([dlse] if have_dlse else []))]
    small = [pl.BlockSpec(bias.shape, lambda j, i: (0, 0, 0)), pl.BlockSpec(sink.shape, lambda j, i: (0, 0, 0))]
    dview, dbias, dsink = pl.pallas_call(
        body,
        grid=(dil, n_chunks + 1),
        in_specs=[pl.BlockSpec(memory_space=pl.ANY), main, prev, nxt] + [rows] * len(row_args) + small,
        out_specs=[pl.BlockSpec((tl, BAND_W), lambda j, i: (jnp.maximum(i - 1, 0), j * cols + col))] + small,
        out_shape=[jax.ShapeDtypeStruct((seq, dil * P_TOT), F32), jax.ShapeDtypeStruct(bias.shape, F32),
                   jax.ShapeDtypeStruct(sink.shape, F32)],
        scratch_shapes=[pltpu.VMEM((2, tl, QW), F32), pltpu.VMEM((seq + 2 * b_, kvw), F32),
                        pltpu.VMEM((seq + 2 * b_, kvw), F32)],
        input_output_aliases={0: 0},
        name=name + "_bwd",
        compiler_params=_params("arbitrary", "arbitrary"),
    )(dproj.reshape(seq, dil * P_TOT), view, view, view, *row_args, bias, sink)
    return dview.reshape(s_tok, P_TOT), dbias, dsink


def _loss_call(x, target, g):
    s, d = x.shape
    tr = _pick(s, (256, 128, 64, 32, 16, 8))

    def tile_loss(xt, gt, tt):
        err = jnp.square(_rms(xt, gt) - tt)
        return 0.5 * jnp.sum(jnp.mean(err, axis=-1, keepdims=True), axis=0, keepdims=True)

    def body(x_ref, t_ref, g_ref, loss_ref, dx_ref, dg_ref):
        tt = t_ref[...]
        val, vjp = jax.vjp(lambda xt, gt: tile_loss(xt, gt, tt), x_ref[...], g_ref[...])
        dx, dg = vjp(jnp.ones_like(val))
        dx_ref[...] = dx

        @pl.when(pl.program_id(0) == 0)
        def _():
            loss_ref[...] = jnp.zeros_like(loss_ref)
            dg_ref[...] = jnp.zeros_like(dg_ref)

        loss_ref[...] += val
        dg_ref[...] += dg

    return pl.pallas_call(
        body,
        grid=(s // tr,),
        in_specs=[_rows(tr, d), _rows(tr, d), _whole((1, d))],
        out_specs=[_whole((1, 1)), _rows(tr, d), _whole((1, d))],
        out_shape=[jax.ShapeDtypeStruct((1, 1), F32), jax.ShapeDtypeStruct((s, d), F32),
                   jax.ShapeDtypeStruct((1, d), F32)],
        name="final_norm_loss",
        compiler_params=_params("arbitrary"),
    )(x, target, g)


@jax.custom_vjp
def _loss_op(x, target, g):
    return _loss_call(x, target, g)[0][0, 0]


def _loss_op_fwd(x, target, g):
    loss, dx, dg = _loss_call(x, target, g)
    return loss[0, 0], (dx, dg, target)


def _loss_op_bwd(res, ct):
    dx, dg, target = res
    return ct * dx, jnp.zeros_like(target), ct * dg


_loss_op.defvjp(_loss_op_fwd, _loss_op_bwd)


def _mla_tile(r, p, a):
    g_q, g_kv, w_q, w_k, w_v = p
    cos_t, sin_t, place_kr = a
    a_q, a_kv, a_kr = _lanes(r[0], (0, MLA_Q_LORA, MLA_Q_LORA + MLA_KV_LORA, MLA_Q_LORA + MLA_KV_LORA + MLA_ROPE))
    q = _rope(_bdot(_rms(a_q, g_q), w_q), cos_t, sin_t, MLA_ROPE // 2)
    ckv = _rms(a_kv, g_kv)
    k = _rope(_bdot(ckv, w_k) + _hdot(a_kr, place_kr), cos_t, sin_t, MLA_ROPE // 2)
    return _split_heads(q, MLA_HEADS), _split_heads(k, MLA_HEADS), _split_heads(_bdot(ckv, w_v), MLA_HEADS)


def _head_rms(x, g, head_mean):
    return x * lax.rsqrt(_hdot(x * x, head_mean) + EPS) * g


def _gqa_tile(r, p, a):
    g_q, g_k = p
    cos_t, sin_t, mean_q, mean_k = a
    wq, wk = GQA_HEADS * HEAD_DIM, GQA_KV_HEADS * HEAD_DIM
    b_q, b_k, b_v = _lanes(r[0], (0, wq, wq + wk, wq + 2 * wk))
    q = _rope(_head_rms(b_q, g_q, mean_q), cos_t, sin_t, HEAD_DIM // 4)
    k = _rope(_head_rms(b_k, g_k, mean_k), cos_t[:, :wk], sin_t[:, :wk], HEAD_DIM // 4)
    return _split_heads(q, GQA_HEADS), _split_heads(k, GQA_KV_HEADS), _split_heads(b_v, GQA_KV_HEADS)


def _merge_tile(r, p, a):
    gm, o_a, o_b, oc0, oc1, oc2, l0, l1, l2, o_d = r
    (w_branch,) = p
    d = w_branch.shape[2]
    gate_path, merge_logits = _lanes(gm, (0, N_BRANCH * BRANCH_W, N_BRANCH * BRANCH_W + N_BRANCH * d))
    m = jnp.maximum(jnp.maximum(l0, l1), l2)
    e0, e1, e2 = jnp.exp(l0 - m), jnp.exp(l1 - m), jnp.exp(l2 - m)
    y_c = (e0 * oc0 + e1 * oc1 + e2 * oc2) / (e0 + e1 + e2)
    y = jnp.concatenate([_join_heads(o_a), _join_heads(o_b), y_c, o_d], axis=1)
    u = y * (gate_path * jax.nn.sigmoid(gate_path))
    gates = _lanes(merge_logits, tuple(range(0, N_BRANCH * d + 1, d)))
    us = _lanes(u, tuple(range(0, N_BRANCH * BRANCH_W + 1, BRANCH_W)))
    branch_w = _unstack(w_branch)
    out = None
    for nb in range(N_BRANCH):
        term = jax.nn.sigmoid(gates[nb]) * _bdot(us[nb], branch_w[nb])
        out = term if out is None else out + term
    return (out,)


def _mixer_calls(proj, prm, aux):
    s = proj.shape[0]
    tr, tm = _pick(s, (256, 128)), _pick(s, (128,))
    mla_cos, mla_sin, gqa_cos, gqa_sin, place_kr, mean_q, mean_k = aux
    wq = MLA_HEADS * MLA_QK
    mla = dict(
        steps=s // tr, rows=[(proj, _rows(tr, SMALL_W, MLA_BLK))],
        params=[prm["g_q"], prm["g_kv"], prm["w_q"], prm["w_k"], prm["w_v"]],
        aux=[(mla_cos, _rows(tr, wq)), (mla_sin, _rows(tr, wq)), (place_kr, _whole(place_kr.shape))],
        outs=[((MLA_HEADS, s, MLA_QK), _head_rows(MLA_HEADS, tr, MLA_QK))] * 2
        + [((MLA_HEADS, s, MLA_V), _head_rows(MLA_HEADS, tr, MLA_V))],
        window=((s, P_TOT), _rows(tr, SMALL_W, MLA_BLK)))
    wg = GQA_HEADS * HEAD_DIM
    gqa = dict(
        steps=s // tr, rows=[(proj, _rows(tr, SMALL_W, GQA_BLK))], params=[prm["gq"], prm["gk"]],
        aux=[(gqa_cos, _rows(tr, wg)), (gqa_sin, _rows(tr, wg)), (mean_q, _whole(mean_q.shape)),
             (mean_k, _whole(mean_k.shape))],
        outs=[((GQA_HEADS, s, HEAD_DIM), _head_rows(GQA_HEADS, tr, HEAD_DIM))]
        + [((GQA_KV_HEADS, s, HEAD_DIM), _head_rows(GQA_KV_HEADS, tr, HEAD_DIM))] * 2,
        window=((s, P_TOT), _rows(tr, SMALL_W, GQA_BLK)))
    merge = dict(steps=s // tm, tm=tm, window=((s, P_TOT), _rows(tm, GM_W, 0)))
    return mla, gqa, merge


def _merge_rows(proj, o_a, o_b, ocs, lses, o_d, tm):
    h4 = _head_rows(4, tm, HEAD_DIM)
    return ([(proj, _rows(tm, GM_W, 0)), (o_a, h4), (o_b, h4)] + [(t, _rows(tm, QW)) for t in ocs + lses]
            + [(o_d, _rows(tm, QW))])


def _mixer_fwd(proj, prm, aux):
    s = proj.shape[0]
    mla, gqa, merge = _mixer_calls(proj, prm, aux)
    q_a, k_a, v_a = _fwd_call("prep_mla", _mla_tile, mla["steps"], mla["rows"], mla["params"], mla["aux"], mla["outs"])
    o_a, lse_a = _dense_fwd_call(q_a, k_a, v_a, MLA_QK ** -0.5, "mla")
    q_b, k_b, v_b = _fwd_call("prep_gqa", _gqa_tile, gqa["steps"], gqa["rows"], gqa["params"], gqa["aux"], gqa["outs"])
    grp = GQA_HEADS // GQA_KV_HEADS
    o_b, lse_b = _dense_fwd_call(q_b.reshape(GQA_KV_HEADS, grp * s, HEAD_DIM), k_b, v_b, HEAD_DIM ** -0.5, "gqa")
    scale = HEAD_DIM ** -0.5
    ocs, lses = [], []
    for gi, (_, dil) in enumerate(DIL_PATTERNS):
        o, lse = _band_fwd_call(proj, DIL_BLK + gi, prm["bias_dil"][gi], prm["no_sink"], dil, 1, QW, scale, "dil%d" % gi)
        ocs.append(o)
        lses.append(lse)
    o_d, lse_d = _band_fwd_call(proj, WIN_BLK, prm["bias_win"], prm["sink"], 1, WIN_HEADS // WIN_KV_HEADS,
                                WIN_KV_HEADS * HEAD_DIM, scale, "win")
    rows = _merge_rows(proj, o_a, o_b.reshape(GQA_HEADS, s, HEAD_DIM), ocs, lses, o_d, merge["tm"])
    mix = _fwd_call("merge", _merge_tile, merge["steps"], rows, [prm["w_branch"]], [],
                    [((s, prm["w_branch"].shape[2]), _rows(merge["tm"], prm["w_branch"].shape[2]))])[0]
    return mix, (q_a, k_a, v_a, o_a, lse_a, q_b, k_b, v_b, o_b, lse_b, ocs, lses, o_d, lse_d)


def _mixer_bwd(proj, prm, aux, saved, dmix):
    s = proj.shape[0]
    q_a, k_a, v_a, o_a, lse_a, q_b, k_b, v_b, o_b, lse_b, ocs, lses, o_d, lse_d = saved
    mla, gqa, merge = _mixer_calls(proj, prm, aux)
    tm, d_model = merge["tm"], prm["w_branch"].shape[2]
    grp = GQA_HEADS // GQA_KV_HEADS
    scale = HEAD_DIM ** -0.5

    rows = _merge_rows(proj, o_a, o_b.reshape(GQA_HEADS, s, HEAD_DIM), ocs, lses, o_d, tm)
    grads, (dw_branch,) = _vjp_call(
        "merge", _merge_tile, merge["steps"], rows, [prm["w_branch"]], [], [(dmix, _rows(tm, d_model))],
        [merge["window"]] + [(a.shape, spec) for a, spec in rows[1:]])
    dproj, do_a, do_b, docs, dlses, do_d = grads[0], grads[1], grads[2], grads[3:6], grads[6:9], grads[9]

    dq_a, dk_a, dv_a = _dense_bwd_call(q_a, k_a, v_a, o_a, lse_a, do_a, MLA_QK ** -0.5, "mla")
    (dproj,), dmla = _vjp_call("prep_mla", _mla_tile, mla["steps"], mla["rows"], mla["params"], mla["aux"],
                               [(t, spec) for t, (_, spec) in zip((dq_a, dk_a, dv_a), mla["outs"])],
                               [mla["window"]], into=dproj)
    dq_b, dk_b, dv_b = _dense_bwd_call(q_b.reshape(GQA_KV_HEADS, grp * s, HEAD_DIM), k_b, v_b, o_b, lse_b,
                                       do_b.reshape(GQA_KV_HEADS, grp * s, HEAD_DIM), scale, "gqa")
    (dproj,), dgqa = _vjp_call("prep_gqa", _gqa_tile, gqa["steps"], gqa["rows"], gqa["params"], gqa["aux"],
                               [(t, spec) for t, (_, spec) in zip((dq_b.reshape(GQA_HEADS, s, HEAD_DIM), dk_b, dv_b),
                                                                  gqa["outs"])],
                               [gqa["window"]], into=dproj)
    dproj, dbias_win, dsink = _band_bwd_call(proj, o_d, do_d, lse_d, None, prm["bias_win"], prm["sink"], dproj,
                                             WIN_BLK, 1, WIN_HEADS // WIN_KV_HEADS, WIN_KV_HEADS * HEAD_DIM, scale, "win")
    dbias_dil = []
    for gi, (_, dil) in enumerate(DIL_PATTERNS):
        dproj, dbias, _ = _band_bwd_call(proj, ocs[gi], docs[gi], lses[gi], dlses[gi], prm["bias_dil"][gi],
                                         prm["no_sink"], dproj, DIL_BLK + gi, dil, 1, QW, scale, "dil%d" % gi)
        dbias_dil.append(dbias)
    dprm = dict(g_q=dmla[0], g_kv=dmla[1], w_q=dmla[2], w_k=dmla[3], w_v=dmla[4], gq=dgqa[0], gk=dgqa[1],
                bias_dil=dbias_dil, bias_win=dbias_win, sink=dsink, no_sink=jnp.zeros_like(prm["no_sink"]),
                w_branch=dw_branch)
    return dproj, dprm


@jax.custom_vjp
def _mixer(proj, prm, aux):
    return _mixer_fwd(proj, prm, aux)[0]


def _mixer_vjp_fwd(proj, prm, aux):
    mix, saved = _mixer_fwd(proj, prm, aux)
    return mix, (proj, prm, aux, saved)


def _mixer_vjp_bwd(res, dmix):
    proj, prm, aux, saved = res
    dproj, dprm = _mixer_bwd(proj, prm, aux, saved, dmix)
    return dproj, dprm, tuple(jnp.zeros_like(t) for t in aux)


_mixer.defvjp(_mixer_vjp_fwd, _mixer_vjp_bwd)


def _rope_angles(pos, dim):
    inv = ROPE_THETA ** (-jnp.arange(0, dim, 2, dtype=F32) / dim)
    return pos.astype(F32)[:, None] * inv[None, :]


def _rope_tables(s):
    pos = jnp.arange(s, dtype=jnp.int32)
    rows = s // GRID_W
    row_idx = jnp.repeat(jnp.arange(rows, dtype=jnp.int32), GRID_W)
    col_idx = jnp.tile(jnp.arange(GRID_W, dtype=jnp.int32), rows)
    a1 = _rope_angles(pos, MLA_ROPE)
    ar = _rope_angles(row_idx, HEAD_DIM // 2)
    ac = _rope_angles(col_idx, HEAD_DIM // 2)
    ones, zeros = jnp.ones((s, MLA_NOPE), F32), jnp.zeros((s, MLA_NOPE), F32)
    mla_cos = jnp.tile(jnp.concatenate([ones, jnp.cos(a1), jnp.cos(a1)], axis=1), (1, MLA_HEADS))
    mla_sin = jnp.tile(jnp.concatenate([zeros, -jnp.sin(a1), jnp.sin(a1)], axis=1), (1, MLA_HEADS))
    gqa_cos = jnp.tile(jnp.concatenate([jnp.cos(ar), jnp.cos(ar), jnp.cos(ac), jnp.cos(ac)], axis=1), (1, GQA_HEADS))
    gqa_sin = jnp.tile(jnp.concatenate([-jnp.sin(ar), jnp.sin(ar), -jnp.sin(ac), jnp.sin(ac)], axis=1), (1, GQA_HEADS))
    return mla_cos, mla_sin, gqa_cos, gqa_sin


def _t5_bucket(rel):
    nb = T5_BUCKETS // 2
    max_exact = nb // 2
    n = jnp.abs(rel)
    nf = jnp.maximum(n, 1).astype(F32)
    large = max_exact + (jnp.log(nf / max_exact) / math.log(T5_MAX_DIST / max_exact) * (nb - max_exact)).astype(jnp.int32)
    large = jnp.minimum(large, nb - 1)
    return jnp.where(rel > 0, nb, 0) + jnp.where(n < max_exact, n, large)


def _band_bias(table, stride, head_lo, heads, half_window):
    b = BAND_BLOCK
    offs = jnp.arange(3 * b)[None, :] - b - jnp.arange(b)[:, None]
    one_hot = (_t5_bucket(offs * stride)[..., None] == jnp.arange(T5_BUCKETS)).astype(F32)
    bias = jnp.dot(one_hot.reshape(b * 3 * b, T5_BUCKETS), table[:, head_lo:head_lo + heads],
                   precision=lax.Precision.HIGHEST)
    bias = bias.T.reshape(heads, b, 3 * b)
    return jnp.where((jnp.abs(offs) <= half_window)[None], bias, NEG_INF)


def _w_in_layout(w_in):
    d = w_in.shape[0]
    at, pieces = 0, {}
    for name, width in (("a_q", 256), ("a_kv", 128), ("a_kr", 32), ("b_q", 256), ("b_k", 128), ("b_v", 128),
                        ("c_q", 768), ("c_k", 768), ("c_v", 768), ("d_q", 256), ("d_k", 128), ("d_v", 128),
                        ("gate", N_BRANCH * BRANCH_W), ("merge", N_BRANCH * d)):
        pieces[name] = w_in[:, at:at + width]
        at += width

    def zeros(n):
        return jnp.zeros((d, n), F32)

    front = [pieces["gate"], pieces["merge"]]
    mla = [pieces["a_q"], pieces["a_kv"], pieces["a_kr"], zeros(SMALL_W - 416)]
    gqa = [pieces["b_q"], pieces["b_k"], pieces["b_v"]]
    win = [pieces["d_q"], pieces["d_k"], pieces["d_v"], zeros(BAND_W - 512)]
    dil = [pieces[n][:, g * QW:(g + 1) * QW] for g in range(len(DIL_PATTERNS)) for n in ("c_q", "c_k", "c_v")]
    out = jnp.concatenate(front + mla + gqa + win + dil, axis=1)
    assert GM_W == MLA_BLK * SMALL_W and GM_W + 2 * SMALL_W == WIN_BLK * BAND_W and out.shape[1] == P_TOT
    return out


def _layer(x, w, l, aux, biases):
    proj = _matmul(_norm(x, w["norm_g"][l][None, :]), _w_in_layout(w["w_in"][l]), "proj")
    w_kv = w["w_mla_kv_up"][l].reshape(MLA_KV_LORA, MLA_HEADS, MLA_NOPE + MLA_V)
    w_k = jnp.concatenate([w_kv[:, :, :MLA_NOPE], jnp.zeros((MLA_KV_LORA, MLA_HEADS, MLA_ROPE), F32)], axis=2)
    dil_bias, win_bias = biases
    prm = dict(
        g_q=w["mla_q_norm_g"][l][None, :], g_kv=w["mla_kv_norm_g"][l][None, :], w_q=w["w_mla_q_up"][l],
        w_k=w_k.reshape(MLA_KV_LORA, MLA_HEADS * MLA_QK),
        w_v=w_kv[:, :, MLA_NOPE:].reshape(MLA_KV_LORA, MLA_HEADS * MLA_V),
        gq=jnp.tile(w["gqa_q_norm_g"][l], GQA_HEADS)[None, :], gk=jnp.tile(w["gqa_k_norm_g"][l], GQA_KV_HEADS)[None, :],
        bias_dil=list(dil_bias), bias_win=win_bias, sink=w["win_sink"][l].reshape(WIN_HEADS, 1, 1),
        no_sink=jnp.full((DIL_HEADS, 1, 1), NEG_INF, F32), w_branch=w["w_branch"][l])
    return x + _matmul(_mixer(proj, prm, aux), w["w_out"][l], "out_proj")


def _local_loss(w, x, target):
    s, d_model = x.shape
    assert d_model == D_MODEL, "the projection's window layout is laid out for d_model 1024"
    place = np.zeros((MLA_ROPE, MLA_HEADS * MLA_QK), np.float32)
    for h in range(MLA_HEADS):
        for i in range(MLA_ROPE):
            place[i, h * MLA_QK + MLA_NOPE + i] = 1.0

    def head_mean(nh):
        m = np.kron(np.eye(nh, dtype=np.float32), np.full((HEAD_DIM, HEAD_DIM), 1.0 / HEAD_DIM, np.float32))
        return jnp.asarray(m)

    aux = _rope_tables(s) + (jnp.asarray(place), head_mean(GQA_HEADS), head_mean(GQA_KV_HEADS))
    table = w["t5_table"]
    dil_bias = [_band_bias(table, dil, gi * DIL_HEADS, DIL_HEADS, window // (2 * dil))
                for gi, (window, dil) in enumerate(DIL_PATTERNS)]
    win_bias = _band_bias(table, 1, len(DIL_PATTERNS) * DIL_HEADS, WIN_HEADS, WIN_HALF)
    for l in range(w["norm_g"].shape[0]):
        x = _layer(x, w, l, aux, (dil_bias, win_bias))
    return _loss_op(x, target, w["final_norm_g"][None, :])


_ANY = pl.BlockSpec(memory_space=pl.ANY)
_MESH = pl.DeviceIdType.MESH


def _all_gather(block, name):
    def body(x_ref, out_ref, send_sems, recv_sems, local_sem):
        x, y, c = lax.axis_index("x"), lax.axis_index("y"), lax.axis_index("c")
        me, sibling = (x, y, c), (x, y, 1 - c)
        chips = [(1 - x, y), (x, 1 - y), (1 - x, 1 - y)]

        def slot(px, py, pc):
            return out_ref.at[4 * px + 2 * py + pc]

        def copy(k, blk, to, src=None):
            return pltpu.make_async_remote_copy(
                src_ref=slot(*blk) if src is None else src, dst_ref=slot(*blk),
                send_sem=send_sems.at[k], recv_sem=recv_sems.at[k], device_id=to, device_id_type=_MESH)

        mine = pltpu.make_async_copy(x_ref, slot(*me), local_sem)
        mine.start()
        first = [copy(0, me, sibling, src=x_ref)]
        first += [copy(1 + j, me, (*chip, c), src=x_ref) for j, chip in enumerate(chips)]
        for cp in first:
            cp.start()
        passed = [copy(4 + j, (*chip, c), sibling) for j, chip in enumerate(chips)]
        for j, chip in enumerate(chips):
            copy(1 + j, (*chip, c), me).wait_recv()
            passed[j].start()
        copy(0, sibling, me).wait_recv()
        for j, chip in enumerate(chips):
            copy(4 + j, (*chip, 1 - c), me).wait_recv()
        for cp in first + passed:
            cp.wait_send()
        mine.wait()

    return pl.pallas_call(
        body,
        out_shape=jax.ShapeDtypeStruct((N_DEV,) + block.shape, block.dtype),
        in_specs=[_ANY],
        out_specs=_ANY,
        scratch_shapes=[pltpu.SemaphoreType.DMA((7,)), pltpu.SemaphoreType.DMA((7,)), pltpu.SemaphoreType.DMA],
        name=name,
    )(block)


def _all_to_all(blocks, name):
    def body(x_ref, out_ref, send_sems, recv_sems, local_sem):
        x, y, c = lax.axis_index("x"), lax.axis_index("y"), lax.axis_index("c")
        me = 4 * x + 2 * y + c
        mine = pltpu.make_async_copy(x_ref.at[me], out_ref.at[me], local_sem)
        mine.start()
        copies, landed = [], []
        for k in range(1, N_DEV):
            px = 1 - x if k & 4 else x
            py = 1 - y if k & 2 else y
            pc = 1 - c if k & 1 else c
            peer = 4 * px + 2 * py + pc
            copies.append(pltpu.make_async_remote_copy(
                src_ref=x_ref.at[peer], dst_ref=out_ref.at[me], send_sem=send_sems.at[k - 1],
                recv_sem=recv_sems.at[k - 1], device_id=(px, py, pc), device_id_type=_MESH))
            landed.append(pltpu.make_async_remote_copy(
                src_ref=x_ref.at[peer], dst_ref=out_ref.at[peer], send_sem=send_sems.at[k - 1],
                recv_sem=recv_sems.at[k - 1], device_id=(px, py, pc), device_id_type=_MESH))
        for cp in copies:
            cp.start()
        for cp in landed:
            cp.wait_recv()
        for cp in copies:
            cp.wait_send()
        mine.wait()

    return pl.pallas_call(
        body,
        out_shape=jax.ShapeDtypeStruct(blocks.shape, blocks.dtype),
        in_specs=[_ANY],
        out_specs=_ANY,
        scratch_shapes=[pltpu.SemaphoreType.DMA((7,)), pltpu.SemaphoreType.DMA((7,)), pltpu.SemaphoreType.DMA],
        name=name,
    )(blocks)


def _sum_slots(parts, name):
    _, rows, w = parts.shape
    tr = _pick(rows, (1024, 512, 256, 128, 64, 32, 16, 8))

    def body(p_ref, o_ref):
        acc = p_ref[0].astype(F32)
        for j in range(1, N_DEV):
            acc = acc + p_ref[j].astype(F32)
        o_ref[...] = acc

    return pl.pallas_call(
        body,
        grid=(rows // tr,),
        in_specs=[pl.BlockSpec((N_DEV, tr, w), lambda i: (0, i, 0))],
        out_specs=pl.BlockSpec((tr, w), lambda i: (i, 0)),
        out_shape=jax.ShapeDtypeStruct((rows, w), F32),
        name=name,
        compiler_params=_params("parallel"),
    )(parts)


def _adamw(w, g, m, v, name):
    rows, width = w.shape
    tr = _pick(rows, (1024, 512, 256, 128, 64, 32, 16, 8))

    def body(w_ref, g_ref, m_ref, v_ref, d_ref, nm_ref, nv_ref):
        g_ = g_ref[...]
        m_ = ADAM_B1 * m_ref[...] + (1.0 - ADAM_B1) * g_
        v_ = ADAM_B2 * v_ref[...] + (1.0 - ADAM_B2) * jnp.square(g_)
        m_hat = m_ / (1.0 - ADAM_B1 ** ADAM_STEP)
        v_hat = v_ / (1.0 - ADAM_B2 ** ADAM_STEP)
        d_ref[...] = -ADAM_LR * (m_hat / (jnp.sqrt(v_hat) + ADAM_EPS) + ADAM_WD * w_ref[...])
        nm_ref[...] = m_
        nv_ref[...] = v_

    spec = pl.BlockSpec((tr, width), lambda i: (i, 0))
    return pl.pallas_call(
        body,
        grid=(rows // tr,),
        in_specs=[spec] * 4,
        out_specs=[spec] * 3,
        out_shape=[jax.ShapeDtypeStruct((rows, width), F32)] * 3,
        name=name,
        compiler_params=_params("parallel"),
    )(w, g, m, v)


_SHARDED = (("w_in", 2), ("w_mla_q_up", 2), ("w_mla_kv_up", 2), ("w_branch", 3), ("w_out", 1))
_REPLICATED = ("norm_g", "mla_q_norm_g", "mla_kv_norm_g", "gqa_q_norm_g", "gqa_k_norm_g", "win_sink", "t5_table",
               "final_norm_g")


def _pack(arrays, row_multiple):
    flat = jnp.concatenate([a.reshape(-1) for a in arrays])
    rows = -(-flat.shape[0] // (LANES * row_multiple)) * row_multiple
    return jnp.pad(flat, (0, rows * LANES - flat.shape[0])).reshape(rows, LANES)


def _unpack(packed, shapes):
    flat, out, at = packed.reshape(-1), [], 0
    for shp in shapes:
        n = int(np.prod(shp))
        out.append(flat[at:at + n].reshape(shp))
        at += n
    return out


def _split_shards(full, axis):
    shp = full.shape
    t = full.reshape(shp[:axis] + (N_DEV, shp[axis] // N_DEV) + shp[axis + 1:])
    return jnp.moveaxis(t, axis, 0)


def _join_shards(stacked, axis):
    t = jnp.moveaxis(stacked, 0, axis)
    shp = t.shape
    return t.reshape(shp[:axis] + (shp[axis] * shp[axis + 1],) + shp[axis + 2:])


def kernel(x, norm_g, w_in, mla_q_norm_g, mla_kv_norm_g, w_mla_q_up, w_mla_kv_up, gqa_q_norm_g, gqa_k_norm_g, win_sink, t5_table, w_branch, w_out, final_norm_g, loss_target, m_norm_g, m_w_in, m_mla_q_norm_g, m_mla_kv_norm_g, m_w_mla_q_up, m_w_mla_kv_up, m_gqa_q_norm_g, m_gqa_k_norm_g, m_win_sink, m_t5_table, m_w_branch, m_w_out, m_final_norm_g, v_norm_g, v_w_in, v_mla_q_norm_g, v_mla_kv_norm_g, v_w_mla_q_up, v_w_mla_kv_up, v_gqa_q_norm_g, v_gqa_k_norm_g, v_win_sink, v_t5_table, v_w_branch, v_w_out, v_final_norm_g):
    given = dict(locals())
    names = ("norm_g", "w_in", "mla_q_norm_g", "mla_kv_norm_g", "w_mla_q_up", "w_mla_kv_up", "gqa_q_norm_g",
             "gqa_k_norm_g", "win_sink", "t5_table", "w_branch", "w_out", "final_norm_g")
    shard_names = [n for n, _ in _SHARDED]
    shard_shapes = [given[n].shape for n in shard_names]

    mine = _pack([given[n] for n in shard_names], 16)
    gathered = _all_gather(mine.astype(BF16), "gather_weights").astype(F32)
    per_dev = [_unpack(gathered[j], shard_shapes) for j in range(N_DEV)]
    weights = {n: given[n] for n in _REPLICATED}
    for wi, (n, axis) in enumerate(_SHARDED):
        weights[n] = _join_shards(jnp.stack([per_dev[j][wi] for j in range(N_DEV)]), axis)

    loss, (gw, gx) = jax.value_and_grad(_local_loss, argnums=(0, 1))(weights, x[0], loss_target[0])
    loss = lax.psum(loss, ("x", "y", "c"))

    blocks = [_split_shards(gw[n], axis) for n, axis in _SHARDED]
    send = jnp.stack([_pack([b[j] for b in blocks], 16) for j in range(N_DEV)])
    g_shard = _unpack(_sum_slots(_all_to_all(send.astype(BF16), "scatter_grads"), "sum_grads"), shard_shapes)
    rep_shapes = [given[n].shape for n in _REPLICATED]
    g_rep = _unpack(_sum_slots(_all_gather(_pack([gw[n] for n in _REPLICATED], 8), "gather_small_grads"),
                               "sum_small_grads"), rep_shapes)
    grads = dict(zip(shard_names, g_shard))
    grads.update(zip(_REPLICATED, g_rep))

    def update(group, shapes, row_multiple, name):
        outs = _adamw(*[_pack([src[n] for n in group], row_multiple) for src in (
            given, grads, {n: given["m_" + n] for n in group}, {n: given["v_" + n] for n in group})], name)
        return [dict(zip(group, _unpack(o, shapes))) for o in outs]

    big = update(shard_names, shard_shapes, 16, "adamw_shards")
    small = update(list(_REPLICATED), rep_shapes, 8, "adamw_replicated")
    delta, new_m, new_v = [{**b, **s_} for b, s_ in zip(big, small)]
    return (loss, gx[None], *[grads[n] for n in names], *[delta[n] for n in names],
            *[new_m[n] for n in names], *[new_v[n] for n in names])
```

```python
import functools
import math

import jax
import jax.numpy as jnp
import numpy as np
from jax import lax
from jax.experimental import pallas as pl
from jax.experimental.pallas import tpu as pltpu

F32 = jnp.float32
BF16 = jnp.bfloat16
N_DEV = 8
LANES = 128
HALF = LANES // 2
V7X_VMEM_LIMIT = 56 * 1024 * 1024

EPS = 1e-6
NEG_INF = -1e30
LOG2E = 1.4426950408889634
ROPE_THETA = 10000.0
GRID_W = 64
HEAD_DIM = 64
N_BRANCH = 4
BRANCH_W = 256
MLA_HEADS, MLA_Q_LORA, MLA_KV_LORA, MLA_NOPE, MLA_ROPE, MLA_V = 4, 256, 128, 64, 32, 64
MLA_QK = MLA_NOPE + MLA_ROPE
GQA_HEADS, GQA_KV_HEADS = 4, 2
DIL_PATTERNS = ((128, 1), (512, 4), (2048, 16))
DIL_HEADS = 4
WIN_HEADS, WIN_KV_HEADS, WIN_HALF = 4, 2, 128
T5_BUCKETS, T5_MAX_DIST = 32, 1024
BAND_BLOCK = 128
ADAM_LR, ADAM_B1, ADAM_B2, ADAM_EPS, ADAM_WD, ADAM_STEP = 0.001, 0.9, 0.999, 1e-08, 0.01, 10

D_MODEL = 1024
GM_W, SMALL_W, BAND_W = 5120, 512, 768
MLA_BLK, GQA_BLK, WIN_BLK, DIL_BLK = 10, 11, 8, 9
P_TOT = 7680
QW = 256


def _params(*sem):
    return pltpu.CompilerParams(dimension_semantics=sem, vmem_limit_bytes=V7X_VMEM_LIMIT)


def _pick(n, cands):
    for c in cands:
        if n % c == 0:
            return c
    return n


def _dot(a, b, ca, cb):
    return lax.dot_general(a.astype(BF16), b.astype(BF16), (((ca,), (cb,)), ((), ())), preferred_element_type=F32)


def _bmm(a, b, ca, cb):
    return lax.dot_general(a, b, (((ca,), (cb,)), ((0,), (0,))), preferred_element_type=F32)


@jax.custom_vjp
def _bdot(a, b):
    return _dot(a, b, 1, 0)


def _bdot_fwd(a, b):
    return _dot(a, b, 1, 0), (a, b)


def _bdot_bwd(res, g):
    a, b = res
    return _dot(g, b, 1, 1), _dot(a, g, 0, 0)


_bdot.defvjp(_bdot_fwd, _bdot_bwd)


def _hdot(a, c):
    return lax.dot_general(a, c, (((1,), (0,)), ((), ())), precision=lax.Precision.HIGHEST, preferred_element_type=F32)


@functools.partial(jax.custom_vjp, nondiff_argnums=(1,))
def _lane_roll(x, shift):
    return pltpu.roll(x, shift, 1)


def _lane_roll_fwd(x, shift):
    return pltpu.roll(x, shift, 1), None


def _lane_roll_bwd(shift, _, g):
    return (pltpu.roll(g, g.shape[1] - shift, 1),)


_lane_roll.defvjp(_lane_roll_fwd, _lane_roll_bwd)


@functools.partial(jax.custom_vjp, nondiff_argnums=(1,))
def _lane_ranges(x, cut):
    bounds, _ = cut
    return tuple(x[:, lo:hi] for lo, hi in zip(bounds[:-1], bounds[1:]))


def _lane_ranges_fwd(x, cut):
    return _lane_ranges(x, cut), None


def _lane_ranges_bwd(cut, _, cts):
    bounds, width = cut
    parts = list(cts)
    if bounds[-1] < width:
        parts.append(jnp.zeros((cts[0].shape[0], width - bounds[-1]), cts[0].dtype))
    return (jnp.concatenate(parts, axis=1),)


_lane_ranges.defvjp(_lane_ranges_fwd, _lane_ranges_bwd)


def _lanes(x, bounds):
    return _lane_ranges(x, (tuple(bounds), x.shape[1]))


@jax.custom_vjp
def _unstack(x):
    return tuple(x[i] for i in range(x.shape[0]))


def _unstack_fwd(x):
    return _unstack(x), None


def _unstack_bwd(_, cts):
    return (jnp.stack(cts, axis=0),)


_unstack.defvjp(_unstack_fwd, _unstack_bwd)


@functools.partial(jax.custom_vjp, nondiff_argnums=(1,))
def _split_heads(x, h):
    d = x.shape[1] // h
    return jnp.stack([x[:, i * d:(i + 1) * d] for i in range(h)], axis=0)


def _split_heads_fwd(x, h):
    return _split_heads(x, h), None


def _split_heads_bwd(h, _, ct):
    return (jnp.concatenate([ct[i] for i in range(h)], axis=1),)


_split_heads.defvjp(_split_heads_fwd, _split_heads_bwd)


def _join_heads(x):
    return jnp.concatenate(_unstack(x), axis=1)


def _rope(x, cos_t, sin_t, half):
    w = x.shape[1]
    lane = lax.broadcasted_iota(jnp.int32, (1, w), 1)
    first = (lane % (2 * half)) < half
    partner = jnp.where(first, _lane_roll(x, w - half), _lane_roll(x, half))
    return x * cos_t + partner * sin_t


def _rms(x, g):
    return x * lax.rsqrt(jnp.mean(x * x, axis=-1, keepdims=True) + EPS) * g


def _rows(tr, w, col=0):
    return pl.BlockSpec((tr, w), lambda i: (i, col))


def _head_rows(h, tr, d):
    return pl.BlockSpec((h, tr, d), lambda i: (0, i, 0))


def _whole(shape):
    nd = len(shape)
    return pl.BlockSpec(tuple(shape), lambda i: (0,) * nd)


def _fwd_call(name, fn, steps, rows, params, aux, outs):
    nr, npar, na = len(rows), len(params), len(aux)

    def body(*refs):
        vals = [x[...] for x in refs[:nr + npar + na]]
        res = fn(vals[:nr], vals[nr:nr + npar], vals[nr + npar:])
        for o_ref, o in zip(refs[nr + npar + na:], res):
            o_ref[...] = o

    return pl.pallas_call(
        body,
        grid=(steps,),
        in_specs=[s for _, s in rows] + [_whole(p.shape) for p in params] + [s for _, s in aux],
        out_specs=[s for _, s in outs],
        out_shape=[jax.ShapeDtypeStruct(shp, F32) for shp, _ in outs],
        name=name + "_fwd",
        compiler_params=_params("parallel"),
    )(*[a for a, _ in rows], *params, *[a for a, _ in aux])


def _vjp_call(name, fn, steps, rows, params, aux, cts, row_grads, into=None):
    nr, npar, na, nc = len(rows), len(params), len(aux), len(cts)
    n_in = nr + npar + na + nc
    lead = 0 if into is None else 1

    def body(*refs):
        refs = refs[lead:]
        vals = [x[...] for x in refs[:n_in]]
        r, p, a, d = vals[:nr], vals[nr:nr + npar], vals[nr + npar:nr + npar + na], vals[nr + npar + na:]
        out_refs = refs[n_in:]
        _, vjp = jax.vjp(lambda r_, p_: tuple(fn(r_, p_, a)), r, p)
        dr, dp = vjp(tuple(d))
        for o_ref, o in zip(out_refs[:nr], dr):
            o_ref[...] = o

        @pl.when(pl.program_id(0) == 0)
        def _():
            for o_ref in out_refs[nr:]:
                o_ref[...] = jnp.zeros_like(o_ref)

        for o_ref, o in zip(out_refs[nr:], dp):
            o_ref[...] += o

    outs = pl.pallas_call(
        body,
        grid=(steps,),
        in_specs=([] if into is None else [pl.BlockSpec(memory_space=pl.ANY)])
        + [s for _, s in rows] + [_whole(p.shape) for p in params] + [s for _, s in aux] + [s for _, s in cts],
        out_specs=[s for _, s in row_grads] + [_whole(p.shape) for p in params],
        out_shape=[jax.ShapeDtypeStruct(shp, F32) for shp, _ in row_grads]
        + [jax.ShapeDtypeStruct(p.shape, F32) for p in params],
        input_output_aliases={} if into is None else {0: 0},
        name=name + "_bwd",
        compiler_params=_params("arbitrary"),
    )(*([] if into is None else [into]), *[a for a, _ in rows], *params, *[a for a, _ in aux], *[a for a, _ in cts])
    return list(outs[:nr]), list(outs[nr:])


def _norm(x, g):
    s, d = x.shape
    tr = _pick(s, (256, 128, 64, 32, 16, 8))

    def tile(r, p, a):
        return (_rms(r[0], p[0]),)

    @jax.custom_vjp
    def op(x, g):
        return _fwd_call("norm", tile, s // tr, [(x, _rows(tr, d))], [g], [], [((s, d), _rows(tr, d))])[0]

    def op_fwd(x, g):
        return op(x, g), (x, g)

    def op_bwd(res, dy):
        x, g = res
        (dx,), (dg,) = _vjp_call("norm", tile, s // tr, [(x, _rows(tr, d))], [g], [], [(dy, _rows(tr, d))],
                                 [((s, d), _rows(tr, d))])
        return dx, dg

    op.defvjp(op_fwd, op_bwd)
    return op(x, g)


def _mm(a, b, mode, name):
    if mode == "nn":
        (m, k), n = a.shape, b.shape[1]
    elif mode == "nt":
        (m, k), n = a.shape, b.shape[0]
    else:
        (k, m), n = a.shape, b.shape[1]
    tm = _pick(m, (512, 256, 128))
    tn = _pick(n, (1024, 768, 512, 384, 256, 128))
    tk = _pick(k, (1024, 768, 512, 384, 256, 128))
    if mode == "tn":
        tm, tk = _pick(m, (1024, 512, 256, 128)), _pick(k, (512, 256, 128))
    nk = k // tk

    def body(a_ref, b_ref, o_ref, acc_ref):
        kk = pl.program_id(2)
        if mode == "nn":
            part = _dot(a_ref[...], b_ref[...], 1, 0)
        elif mode == "nt":
            part = _dot(a_ref[...], b_ref[...], 1, 1)
        else:
            part = _dot(a_ref[...], b_ref[...], 0, 0)
        if nk == 1:
            o_ref[...] = part
        else:
            @pl.when(kk == 0)
            def _():
                acc_ref[...] = part

            @pl.when(kk > 0)
            def _():
                acc_ref[...] += part

            @pl.when(kk == nk - 1)
            def _():
                o_ref[...] = acc_ref[...]

    if mode == "nn":
        a_spec = pl.BlockSpec((tm, tk), lambda i, j, kk: (i, kk))
        b_spec = pl.BlockSpec((tk, tn), lambda i, j, kk: (kk, j))
    elif mode == "nt":
        a_spec = pl.BlockSpec((tm, tk), lambda i, j, kk: (i, kk))
        b_spec = pl.BlockSpec((tn, tk), lambda i, j, kk: (j, kk))
    else:
        a_spec = pl.BlockSpec((tk, tm), lambda i, j, kk: (kk, i))
        b_spec = pl.BlockSpec((tk, tn), lambda i, j, kk: (kk, j))
    return pl.pallas_call(
        body,
        grid=(m // tm, n // tn, nk),
        in_specs=[a_spec, b_spec],
        out_specs=pl.BlockSpec((tm, tn), lambda i, j, kk: (i, j)),
        out_shape=jax.ShapeDtypeStruct((m, n), F32),
        scratch_shapes=[pltpu.VMEM((tm, tn), F32)],
        name=name,
        compiler_params=_params("parallel", "parallel", "arbitrary"),
    )(a, b)


def _matmul(a, b, name):
    @jax.custom_vjp
    def op(a, b):
        return _mm(a, b, "nn", name + "_nn")

    def op_fwd(a, b):
        return op(a, b), (a, b)

    def op_bwd(res, g):
        a, b = res
        return _mm(g, b, "nt", name + "_nt"), _mm(a, g, "tn", name + "_tn")

    op.defvjp(op_fwd, op_bwd)
    return op(a, b)


def _dense_fwd_call(q, k, v, scale, name):
    n, sq, d = q.shape
    sk, dv = k.shape[1], v.shape[2]
    tq = _pick(sq, (256, 128))
    c = scale * LOG2E

    def body(q_ref, k_ref, v_ref, o_ref, lse_ref, k_s, vext_s):
        @pl.when(pl.program_id(1) == 0)
        def _():
            k_s[...] = k_ref[0].astype(BF16)
            vext_s[...] = jnp.ones_like(vext_s)
            vext_s[:, :dv] = v_ref[0].astype(BF16)

        s = _dot(q_ref[0], k_s[...], 1, 1)
        m = jnp.max(s, axis=1, keepdims=True)
        p = jnp.exp2(s * c - m * c)
        acc = _dot(p, vext_s[...], 1, 0)
        l = acc[:, dv:dv + 1]
        o_ref[0] = acc[:, :dv] / l
        lse_ref[0] = m * scale + jnp.log(l)

    return pl.pallas_call(
        body,
        grid=(n, sq // tq),
        in_specs=[
            pl.BlockSpec((1, tq, d), lambda h, i: (h, i, 0)),
            pl.BlockSpec((1, sk, d), lambda h, i: (h, 0, 0)),
            pl.BlockSpec((1, sk, dv), lambda h, i: (h, 0, 0)),
        ],
        out_specs=[
            pl.BlockSpec((1, tq, dv), lambda h, i: (h, i, 0)),
            pl.BlockSpec((1, tq, 1), lambda h, i: (h, i, 0)),
        ],
        out_shape=[jax.ShapeDtypeStruct((n, sq, dv), F32), jax.ShapeDtypeStruct((n, sq, 1), F32)],
        scratch_shapes=[pltpu.VMEM((sk, d), BF16), pltpu.VMEM((sk, 2 * dv), BF16)],
        name=name + "_fwd",
        compiler_params=_params("arbitrary", "arbitrary"),
    )(q, k, v)


def _dense_bwd_call(q, k, v, o, lse, do, scale, name):
    n, sq, d = q.shape
    sk, dv = k.shape[1], v.shape[2]
    tq, tk = _pick(sq, (512, 256, 128)), _pick(sk, (2048, 1024, 512, 256, 128))
    c = scale * LOG2E

    def body(q_ref, k_ref, v_ref, o_ref, lse_ref, do_ref, dq_ref, dk_ref, dv_ref):
        j, i = pl.program_id(1), pl.program_id(2)
        qb, kb, vb = q_ref[0].astype(BF16), k_ref[0].astype(BF16), v_ref[0].astype(BF16)
        do_f = do_ref[0]
        dob = do_f.astype(BF16)
        p = jnp.exp2(_dot(qb, kb, 1, 1) * c - lse_ref[0] * LOG2E)
        delta = jnp.sum(do_f * o_ref[0], axis=1, keepdims=True)
        ds = (p * (_dot(dob, vb, 1, 1) - delta)).astype(BF16)
        dv_part = _dot(p, dob, 0, 0)
        dk_part = _dot(ds, qb, 0, 0) * scale
        dq_part = _dot(ds, kb, 1, 0) * scale
        rows = pl.ds(pl.multiple_of(i * tq, tq), tq)

        @pl.when(i == 0)
        def _():
            dk_ref[0] = dk_part
            dv_ref[0] = dv_part

        @pl.when(i > 0)
        def _():
            dk_ref[0] += dk_part
            dv_ref[0] += dv_part

        @pl.when(j == 0)
        def _():
            dq_ref[0, rows, :] = dq_part

        @pl.when(j > 0)
        def _():
            dq_ref[0, rows, :] += dq_part

    return pl.pallas_call(
        body,
        grid=(n, sk // tk, sq // tq),
        in_specs=[
            pl.BlockSpec((1, tq, d), lambda h, j, i: (h, i, 0)),
            pl.BlockSpec((1, tk, d), lambda h, j, i: (h, j, 0)),
            pl.BlockSpec((1, tk, dv), lambda h, j, i: (h, j, 0)),
            pl.BlockSpec((1, tq, dv), lambda h, j, i: (h, i, 0)),
            pl.BlockSpec((1, tq, 1), lambda h, j, i: (h, i, 0)),
            pl.BlockSpec((1, tq, dv), lambda h, j, i: (h, i, 0)),
        ],
        out_specs=[
            pl.BlockSpec((1, sq, d), lambda h, j, i: (h, 0, 0)),
            pl.BlockSpec((1, tk, d), lambda h, j, i: (h, j, 0)),
            pl.BlockSpec((1, tk, dv), lambda h, j, i: (h, j, 0)),
        ],
        out_shape=[
            jax.ShapeDtypeStruct((n, sq, d), F32),
            jax.ShapeDtypeStruct((n, sk, d), F32),
            jax.ShapeDtypeStruct((n, sk, dv), F32),
        ],
        name=name + "_bwd",
        compiler_params=_params("arbitrary", "arbitrary", "arbitrary"),
    )(q, k, v, o, lse, do)


def _head_geometry(h, group):
    pair, a = divmod(h, 2)
    kv_pair, b = divmod(h // group, 2)
    return pair, a, kv_pair, b


def _lane_half():
    return lax.broadcasted_iota(jnp.int32, (1, LANES), 1) // HALF


def _align(x, a, b):
    if a != b:
        x = pltpu.roll(x, HALF, 1)
    return jnp.where(_lane_half() == b, x, 0.0)


def _unalign(x, a, b):
    x = jnp.where(_lane_half() == b, x, 0.0)
    return pltpu.roll(x, HALF, 1) if a != b else x


def _bands(w, pw, nw, lo, kvw, nb):
    b = BAND_BLOCK
    cat = jnp.concatenate([pw[:, lo:lo + kvw], w[:, lo:lo + kvw], nw[:, lo:lo + kvw]], axis=0).astype(BF16)
    out = []
    for g in range(kvw // LANES):
        c3 = cat[:, g * LANES:(g + 1) * LANES].reshape(nb + 2, b, LANES)
        out.append(jnp.concatenate([c3[0:nb], c3[1:nb + 1], c3[2:nb + 2]], axis=1))
    return out


def _edge_mask(first_block, nb, n_blocks):
    b = BAND_BLOCK
    blk = first_block + lax.broadcasted_iota(jnp.int32, (nb, 1, 3 * b), 0)
    col = lax.broadcasted_iota(jnp.int32, (nb, 1, 3 * b), 2)
    outside = ((col < b) & (blk == 0)) | ((col >= 2 * b) & (blk == n_blocks - 1))
    return jnp.where(outside, NEG_INF, 0.0)


def _band_geometry(proj, dil):
    seq = proj.shape[0] // dil
    tl = _pick(seq, (1024, 512, 256, 128))
    return seq, tl, tl // BAND_BLOCK, seq // tl


def _band_in_specs(tl, nb, n_chunks, n_blocks, col, last_step_idle):
    def chunk(i):
        return jnp.minimum(i, n_chunks - 1) if last_step_idle else i

    main = pl.BlockSpec((tl, BAND_W), lambda j, i: (j * n_chunks + chunk(i), col))
    prev = pl.BlockSpec((BAND_BLOCK, BAND_W),
                        lambda j, i: (j * n_blocks + jnp.maximum(chunk(i) * nb - 1, 0), col))
    nxt = pl.BlockSpec((BAND_BLOCK, BAND_W),
                       lambda j, i: (j * n_blocks + jnp.minimum((chunk(i) + 1) * nb, n_blocks - 1), col))
    rows = pl.BlockSpec((tl, QW), lambda j, i: (j * n_chunks + chunk(i), 0))
    return main, prev, nxt, rows


def _band_fwd_call(proj, col, bias, sink, dil, group, kvw, scale, name):
    s_tok = proj.shape[0]
    seq, tl, nb, n_chunks = _band_geometry(proj, dil)
    n_blocks = seq // BAND_BLOCK
    heads = bias.shape[0]

    def body(w_ref, pw_ref, nw_ref, bias_ref, sink_ref, o_ref, lse_ref):
        i = pl.program_id(1)
        w, pw, nw = w_ref[...], pw_ref[...], nw_ref[...]
        kb = _bands(w, pw, nw, QW, kvw, nb)
        vb = _bands(w, pw, nw, QW + kvw, kvw, nb)
        edge = _edge_mask(i * nb, nb, n_blocks)
        o_acc = [jnp.zeros((tl, LANES), F32) for _ in range(heads // 2)]
        lse_acc = [jnp.zeros((tl, LANES), F32) for _ in range(heads // 2)]
        for h in range(heads):
            pair, a, kvp, b = _head_geometry(h, group)
            q_al = _align(w[:, pair * LANES:(pair + 1) * LANES], a, b).astype(BF16).reshape(nb, BAND_BLOCK, LANES)
            logits = _bmm(q_al, kb[kvp], 2, 2) * scale + bias_ref[h][None] + edge
            sk = sink_ref[h].reshape(1, 1, 1)
            m = jnp.maximum(jnp.max(logits, axis=2, keepdims=True), sk)
            e = jnp.exp(logits - m)
            ssum = jnp.sum(e, axis=2, keepdims=True) + jnp.exp(sk - m)
            out = _bmm(e.astype(BF16), vb[kvp], 2, 1) / ssum
            o_acc[pair] = o_acc[pair] + _unalign(out.reshape(tl, LANES), a, b)
            lse = (m + jnp.log(ssum)).reshape(tl, 1)
            lse_acc[pair] = lse_acc[pair] + jnp.where(_lane_half() == a, lse, 0.0)
        o_ref[...] = jnp.concatenate(o_acc, axis=1)
        lse_ref[...] = jnp.concatenate(lse_acc, axis=1)

    main, prev, nxt, rows = _band_in_specs(tl, nb, n_chunks, n_blocks, col, False)
    return pl.pallas_call(
        body,
        grid=(dil, n_chunks),
        in_specs=[main, prev, nxt, pl.BlockSpec(bias.shape, lambda j, i: (0, 0, 0)),
                  pl.BlockSpec(sink.shape, lambda j, i: (0, 0, 0))],
        out_specs=[rows, rows],
        out_shape=[jax.ShapeDtypeStruct((s_tok, QW), F32)] * 2,
        name=name + "_fwd",
        compiler_params=_params("parallel", "parallel"),
    )(proj, proj, proj, bias, sink)


def _band_bwd_call(proj, o, do, lse, dlse, bias, sink, dproj, col, dil, group, kvw, scale, name):
    seq, tl, nb, n_chunks = _band_geometry(proj, dil)
    lead = 0 if dproj is None else 1
    n_blocks = seq // BAND_BLOCK
    heads = bias.shape[0]
    b_ = BAND_BLOCK
    have_dlse = dlse is not None

    def body(*refs):
        (w_ref, pw_ref, nw_ref, o_ref, do_ref, lse_ref), refs = refs[lead:lead + 6], refs[lead + 6:]
        if have_dlse:
            dlse_ref, refs = refs[0], refs[1:]
        bias_ref, sink_ref, dwin_ref, dbias_ref, dsink_ref, dq_s, dk_s, dv_s = refs
        j, i = pl.program_id(0), pl.program_id(1)

        @pl.when((j == 0) & (i == 0))
        def _():
            dbias_ref[...] = jnp.zeros_like(dbias_ref)
            dsink_ref[...] = jnp.zeros_like(dsink_ref)

        @pl.when(i == 0)
        def _():
            dk_s[...] = jnp.zeros_like(dk_s)
            dv_s[...] = jnp.zeros_like(dv_s)

        @pl.when(i < n_chunks)
        def _():
            w, pw, nw = w_ref[...], pw_ref[...], nw_ref[...]
            kb = _bands(w, pw, nw, QW, kvw, nb)
            vb = _bands(w, pw, nw, QW + kvw, kvw, nb)
            edge = _edge_mask(i * nb, nb, n_blocks)
            dq_acc = [jnp.zeros((tl, LANES), F32) for _ in range(heads // 2)]
            for h in range(heads):
                pair, a, kvp, b = _head_geometry(h, group)
                lanes = slice(pair * LANES, (pair + 1) * LANES)
                mine = _lane_half() == a
                q_al = _align(w[:, lanes], a, b).astype(BF16).reshape(nb, b_, LANES)
                do_al = _align(do_ref[:, lanes], a, b).astype(BF16).reshape(nb, b_, LANES)
                lse_h = jnp.max(jnp.where(mine, lse_ref[:, lanes], NEG_INF), axis=1, keepdims=True)
                shift = -jnp.sum(jnp.where(mine, do_ref[:, lanes] * o_ref[:, lanes], 0.0), axis=1, keepdims=True)
                if have_dlse:
                    shift = shift + jnp.sum(jnp.where(mine, dlse_ref[:, lanes], 0.0), axis=1, keepdims=True)
                logits = _bmm(q_al, kb[kvp], 2, 2) * scale + bias_ref[h][None] + edge
                p = jnp.exp(logits - lse_h.reshape(nb, b_, 1))
                dlogits = p * (_bmm(do_al, vb[kvp], 2, 2) + shift.reshape(nb, b_, 1))
                dbias_ref[h] += jnp.sum(dlogits, axis=0)
                dsink_ref[h] += jnp.sum(jnp.exp(sink_ref[h] - lse_h) * shift, axis=0, keepdims=True)
                ds = (dlogits * scale).astype(BF16)
                dq_acc[pair] = dq_acc[pair] + _unalign(_bmm(ds, kb[kvp], 2, 1).reshape(tl, LANES), a, b)
                dk_band = _bmm(ds, q_al, 1, 1)
                dv_band = _bmm(p.astype(BF16), do_al, 1, 1)
                kv_lanes = slice(kvp * LANES, (kvp + 1) * LANES)
                for t in range(3):
                    at = pl.ds(pl.multiple_of(i * tl + t * b_, b_), tl)
                    dk_s[at, kv_lanes] += dk_band[:, t * b_:(t + 1) * b_, :].reshape(tl, LANES)
                    dv_s[at, kv_lanes] += dv_band[:, t * b_:(t + 1) * b_, :].reshape(tl, LANES)
            dq_s[lax.rem(i, 2)] = jnp.concatenate(dq_acc, axis=1)

        @pl.when(i >= 1)
        def _():
            at = pl.ds(pl.multiple_of((i - 1) * tl + b_, b_), tl)
            parts = [dq_s[lax.rem(i + 1, 2)], dk_s[at, :], dv_s[at, :]]
            if QW + 2 * kvw < BAND_W:
                parts.append(jnp.zeros((tl, BAND_W - QW - 2 * kvw), F32))
            dwin_ref[...] = jnp.concatenate(parts, axis=1)

    main, prev, nxt, rows = _band_in_specs(tl, nb, n_chunks, n_blocks, col, True)
    row_args = [o, do, lse] + ([dlse] if have_dlse else [])
    small = [pl.BlockSpec(bias.shape, lambda j, i: (0, 0, 0)), pl.BlockSpec(sink.shape, lambda j, i: (0, 0, 0))]
    return pl.pallas_call(
        body,
        grid=(dil, n_chunks + 1),
        in_specs=[pl.BlockSpec(memory_space=pl.ANY)] * lead + [main, prev, nxt] + [rows] * len(row_args) + small,
        out_specs=[pl.BlockSpec((tl, BAND_W), lambda j, i: (j * n_chunks + jnp.maximum(i - 1, 0), col))] + small,
        out_shape=[jax.ShapeDtypeStruct(proj.shape, F32), jax.ShapeDtypeStruct(bias.shape, F32),
                   jax.ShapeDtypeStruct(sink.shape, F32)],
        scratch_shapes=[pltpu.VMEM((2, tl, QW), F32), pltpu.VMEM((seq + 2 * b_, kvw), F32),
                        pltpu.VMEM((seq + 2 * b_, kvw), F32)],
        input_output_aliases={0: 0} if lead else {},
        name=name + "_bwd",
        compiler_params=_params("arbitrary", "arbitrary"),
    )(*([dproj] if lead else []), proj, proj, proj, *row_args, bias, sink)


def _loss_call(x, target, g):
    s, d = x.shape
    tr = _pick(s, (256, 128, 64, 32, 16, 8))

    def tile_loss(xt, gt, tt):
        err = jnp.square(_rms(xt, gt) - tt)
        return 0.5 * jnp.sum(jnp.mean(err, axis=-1, keepdims=True), axis=0, keepdims=True)

    def body(x_ref, t_ref, g_ref, loss_ref, dx_ref, dg_ref):
        tt = t_ref[...]
        val, vjp = jax.vjp(lambda xt, gt: tile_loss(xt, gt, tt), x_ref[...], g_ref[...])
        dx, dg = vjp(jnp.ones_like(val))
        dx_ref[...] = dx

        @pl.when(pl.program_id(0) == 0)
        def _():
            loss_ref[...] = jnp.zeros_like(loss_ref)
            dg_ref[...] = jnp.zeros_like(dg_ref)

        loss_ref[...] += val
        dg_ref[...] += dg

    return pl.pallas_call(
        body,
        grid=(s // tr,),
        in_specs=[_rows(tr, d), _rows(tr, d), _whole((1, d))],
        out_specs=[_whole((1, 1)), _rows(tr, d), _whole((1, d))],
        out_shape=[jax.ShapeDtypeStruct((1, 1), F32), jax.ShapeDtypeStruct((s, d), F32),
                   jax.ShapeDtypeStruct((1, d), F32)],
        name="final_norm_loss",
        compiler_params=_params("arbitrary"),
    )(x, target, g)


@jax.custom_vjp
def _loss_op(x, target, g):
    return _loss_call(x, target, g)[0][0, 0]


def _loss_op_fwd(x, target, g):
    loss, dx, dg = _loss_call(x, target, g)
    return loss[0, 0], (dx, dg, target)


def _loss_op_bwd(res, ct):
    dx, dg, target = res
    return ct * dx, jnp.zeros_like(target), ct * dg


_loss_op.defvjp(_loss_op_fwd, _loss_op_bwd)


def _mla_tile(r, p, a):
    g_q, g_kv, w_q, w_k, w_v = p
    cos_t, sin_t, place_kr = a
    a_q, a_kv, a_kr = _lanes(r[0], (0, MLA_Q_LORA, MLA_Q_LORA + MLA_KV_LORA, MLA_Q_LORA + MLA_KV_LORA + MLA_ROPE))
    q = _rope(_bdot(_rms(a_q, g_q), w_q), cos_t, sin_t, MLA_ROPE // 2)
    ckv = _rms(a_kv, g_kv)
    k = _rope(_bdot(ckv, w_k) + _hdot(a_kr, place_kr), cos_t, sin_t, MLA_ROPE // 2)
    return _split_heads(q, MLA_HEADS), _split_heads(k, MLA_HEADS), _split_heads(_bdot(ckv, w_v), MLA_HEADS)


def _head_rms(x, g, head_mean):
    return x * lax.rsqrt(_hdot(x * x, head_mean) + EPS) * g


def _gqa_tile(r, p, a):
    g_q, g_k = p
    cos_t, sin_t, mean_q, mean_k = a
    wq, wk = GQA_HEADS * HEAD_DIM, GQA_KV_HEADS * HEAD_DIM
    b_q, b_k, b_v = _lanes(r[0], (0, wq, wq + wk, wq + 2 * wk))
    q = _rope(_head_rms(b_q, g_q, mean_q), cos_t, sin_t, HEAD_DIM // 4)
    k = _rope(_head_rms(b_k, g_k, mean_k), cos_t[:, :wk], sin_t[:, :wk], HEAD_DIM // 4)
    return _split_heads(q, GQA_HEADS), _split_heads(k, GQA_KV_HEADS), _split_heads(b_v, GQA_KV_HEADS)


def _merge_tile(r, p, a):
    gm, o_a, o_b, oc0, oc1, oc2, l0, l1, l2, o_d = r
    (w_branch,) = p
    d = w_branch.shape[2]
    gate_path, merge_logits = _lanes(gm, (0, N_BRANCH * BRANCH_W, N_BRANCH * BRANCH_W + N_BRANCH * d))
    m = jnp.maximum(jnp.maximum(l0, l1), l2)
    e0, e1, e2 = jnp.exp(l0 - m), jnp.exp(l1 - m), jnp.exp(l2 - m)
    y_c = (e0 * oc0 + e1 * oc1 + e2 * oc2) / (e0 + e1 + e2)
    y = jnp.concatenate([_join_heads(o_a), _join_heads(o_b), y_c, o_d], axis=1)
    u = y * (gate_path * jax.nn.sigmoid(gate_path))
    gates = _lanes(merge_logits, tuple(range(0, N_BRANCH * d + 1, d)))
    us = _lanes(u, tuple(range(0, N_BRANCH * BRANCH_W + 1, BRANCH_W)))
    branch_w = _unstack(w_branch)
    out = None
    for nb in range(N_BRANCH):
        term = jax.nn.sigmoid(gates[nb]) * _bdot(us[nb], branch_w[nb])
        out = term if out is None else out + term
    return (out,)


def _mixer_calls(proj, prm, aux):
    s = proj.shape[0]
    tr, tm = _pick(s, (256, 128)), _pick(s, (128,))
    mla_cos, mla_sin, gqa_cos, gqa_sin, place_kr, mean_q, mean_k = aux
    wq = MLA_HEADS * MLA_QK
    mla = dict(
        steps=s // tr, rows=[(proj, _rows(tr, SMALL_W, MLA_BLK))],
        params=[prm["g_q"], prm["g_kv"], prm["w_q"], prm["w_k"], prm["w_v"]],
        aux=[(mla_cos, _rows(tr, wq)), (mla_sin, _rows(tr, wq)), (place_kr, _whole(place_kr.shape))],
        outs=[((MLA_HEADS, s, MLA_QK), _head_rows(MLA_HEADS, tr, MLA_QK))] * 2
        + [((MLA_HEADS, s, MLA_V), _head_rows(MLA_HEADS, tr, MLA_V))],
        window=((s, P_TOT), _rows(tr, SMALL_W, MLA_BLK)))
    wg = GQA_HEADS * HEAD_DIM
    gqa = dict(
        steps=s // tr, rows=[(proj, _rows(tr, SMALL_W, GQA_BLK))], params=[prm["gq"], prm["gk"]],
        aux=[(gqa_cos, _rows(tr, wg)), (gqa_sin, _rows(tr, wg)), (mean_q, _whole(mean_q.shape)),
             (mean_k, _whole(mean_k.shape))],
        outs=[((GQA_HEADS, s, HEAD_DIM), _head_rows(GQA_HEADS, tr, HEAD_DIM))]
        + [((GQA_KV_HEADS, s, HEAD_DIM), _head_rows(GQA_KV_HEADS, tr, HEAD_DIM))] * 2,
        window=((s, P_TOT), _rows(tr, SMALL_W, GQA_BLK)))
    merge = dict(steps=s // tm, tm=tm, window=((s, P_TOT), _rows(tm, GM_W, 0)))
    return mla, gqa, merge


def _merge_rows(proj, o_a, o_b, ocs, lses, o_d, tm):
    h4 = _head_rows(4, tm, HEAD_DIM)
    return ([(proj, _rows(tm, GM_W, 0)), (o_a, h4), (o_b, h4)] + [(t, _rows(tm, QW)) for t in ocs + lses]
            + [(o_d, _rows(tm, QW))])


def _to_residues(t, dil):
    s, w = t.shape
    return t if dil == 1 else t.reshape(s // dil, dil, w).transpose(1, 0, 2).reshape(s, w)


def _from_residues(t, dil):
    s, w = t.shape
    return t if dil == 1 else t.reshape(dil, s // dil, w).transpose(1, 0, 2).reshape(s, w)


def _mixer_fwd(projs, prm, aux):
    proj = projs[0]
    s = proj.shape[0]
    mla, gqa, merge = _mixer_calls(proj, prm, aux)
    q_a, k_a, v_a = _fwd_call("prep_mla", _mla_tile, mla["steps"], mla["rows"], mla["params"], mla["aux"], mla["outs"])
    o_a, lse_a = _dense_fwd_call(q_a, k_a, v_a, MLA_QK ** -0.5, "mla")
    q_b, k_b, v_b = _fwd_call("prep_gqa", _gqa_tile, gqa["steps"], gqa["rows"], gqa["params"], gqa["aux"], gqa["outs"])
    grp = GQA_HEADS // GQA_KV_HEADS
    o_b, lse_b = _dense_fwd_call(q_b.reshape(GQA_KV_HEADS, grp * s, HEAD_DIM), k_b, v_b, HEAD_DIM ** -0.5, "gqa")
    scale = HEAD_DIM ** -0.5
    ocs, lses = [], []
    for gi, (_, dil) in enumerate(DIL_PATTERNS):
        o, lse = _band_fwd_call(projs[gi], DIL_BLK if gi == 0 else 0, prm["bias_dil"][gi], prm["no_sink"], dil, 1,
                                QW, scale, "dil%d" % gi)
        ocs.append(o)
        lses.append(lse)
    o_d, lse_d = _band_fwd_call(proj, WIN_BLK, prm["bias_win"], prm["sink"], 1, WIN_HEADS // WIN_KV_HEADS,
                                WIN_KV_HEADS * HEAD_DIM, scale, "win")
    dils = [dil for _, dil in DIL_PATTERNS]
    rows = _merge_rows(proj, o_a, o_b.reshape(GQA_HEADS, s, HEAD_DIM), [_from_residues(t, r) for t, r in zip(ocs, dils)],
                       [_from_residues(t, r) for t, r in zip(lses, dils)], o_d, merge["tm"])
    mix = _fwd_call("merge", _merge_tile, merge["steps"], rows, [prm["w_branch"]], [],
                    [((s, prm["w_branch"].shape[2]), _rows(merge["tm"], prm["w_branch"].shape[2]))])[0]
    return mix, (q_a, k_a, v_a, o_a, lse_a, q_b, k_b, v_b, o_b, lse_b, ocs, lses, o_d, lse_d)


def _mixer_bwd(projs, prm, aux, saved, dmix):
    proj = projs[0]
    s = proj.shape[0]
    q_a, k_a, v_a, o_a, lse_a, q_b, k_b, v_b, o_b, lse_b, ocs, lses, o_d, lse_d = saved
    dils = [dil for _, dil in DIL_PATTERNS]
    mla, gqa, merge = _mixer_calls(proj, prm, aux)
    tm, d_model = merge["tm"], prm["w_branch"].shape[2]
    grp = GQA_HEADS // GQA_KV_HEADS
    scale = HEAD_DIM ** -0.5

    rows = _merge_rows(proj, o_a, o_b.reshape(GQA_HEADS, s, HEAD_DIM), [_from_residues(t, r) for t, r in zip(ocs, dils)],
                       [_from_residues(t, r) for t, r in zip(lses, dils)], o_d, tm)
    grads, (dw_branch,) = _vjp_call(
        "merge", _merge_tile, merge["steps"], rows, [prm["w_branch"]], [], [(dmix, _rows(tm, d_model))],
        [merge["window"]] + [(a.shape, spec) for a, spec in rows[1:]])
    dproj, do_a, do_b, docs, dlses, do_d = grads[0], grads[1], grads[2], grads[3:6], grads[6:9], grads[9]

    dq_a, dk_a, dv_a = _dense_bwd_call(q_a, k_a, v_a, o_a, lse_a, do_a, MLA_QK ** -0.5, "mla")
    (dproj,), dmla = _vjp_call("prep_mla", _mla_tile, mla["steps"], mla["rows"], mla["params"], mla["aux"],
                               [(t, spec) for t, (_, spec) in zip((dq_a, dk_a, dv_a), mla["outs"])],
                               [mla["window"]], into=dproj)
    dq_b, dk_b, dv_b = _dense_bwd_call(q_b.reshape(GQA_KV_HEADS, grp * s, HEAD_DIM), k_b, v_b, o_b, lse_b,
                                       do_b.reshape(GQA_KV_HEADS, grp * s, HEAD_DIM), scale, "gqa")
    (dproj,), dgqa = _vjp_call("prep_gqa", _gqa_tile, gqa["steps"], gqa["rows"], gqa["params"], gqa["aux"],
                               [(t, spec) for t, (_, spec) in zip((dq_b.reshape(GQA_HEADS, s, HEAD_DIM), dk_b, dv_b),
                                                                  gqa["outs"])],
                               [gqa["window"]], into=dproj)
    dproj, dbias_win, dsink = _band_bwd_call(proj, o_d, do_d, lse_d, None, prm["bias_win"], prm["sink"], dproj,
                                             WIN_BLK, 1, WIN_HEADS // WIN_KV_HEADS, WIN_KV_HEADS * HEAD_DIM, scale, "win")
    dbias_dil, dprojs = [], []
    for gi, dil in enumerate(dils):
        dside, dbias, _ = _band_bwd_call(
            projs[gi], ocs[gi], _to_residues(docs[gi], dil), lses[gi], _to_residues(dlses[gi], dil),
            prm["bias_dil"][gi], prm["no_sink"], dproj if gi == 0 else None, DIL_BLK if gi == 0 else 0, dil, 1, QW,
            scale, "dil%d" % gi)
        if gi == 0:
            dproj = dside
        else:
            dprojs.append(dside)
        dbias_dil.append(dbias)
    dprm = dict(g_q=dmla[0], g_kv=dmla[1], w_q=dmla[2], w_k=dmla[3], w_v=dmla[4], gq=dgqa[0], gk=dgqa[1],
                bias_dil=dbias_dil, bias_win=dbias_win, sink=dsink, no_sink=jnp.zeros_like(prm["no_sink"]),
                w_branch=dw_branch)
    return [dproj] + dprojs, dprm


@jax.custom_vjp
def _mixer(projs, prm, aux):
    return _mixer_fwd(projs, prm, aux)[0]


def _mixer_vjp_fwd(projs, prm, aux):
    mix, saved = _mixer_fwd(projs, prm, aux)
    return mix, (projs, prm, aux, saved)


def _mixer_vjp_bwd(res, dmix):
    projs, prm, aux, saved = res
    dprojs, dprm = _mixer_bwd(projs, prm, aux, saved, dmix)
    return dprojs, dprm, tuple(jnp.zeros_like(t) for t in aux)


_mixer.defvjp(_mixer_vjp_fwd, _mixer_vjp_bwd)


def _rope_angles(pos, dim):
    inv = ROPE_THETA ** (-jnp.arange(0, dim, 2, dtype=F32) / dim)
    return pos.astype(F32)[:, None] * inv[None, :]


def _rope_tables(s):
    pos = jnp.arange(s, dtype=jnp.int32)
    rows = s // GRID_W
    row_idx = jnp.repeat(jnp.arange(rows, dtype=jnp.int32), GRID_W)
    col_idx = jnp.tile(jnp.arange(GRID_W, dtype=jnp.int32), rows)
    a1 = _rope_angles(pos, MLA_ROPE)
    ar = _rope_angles(row_idx, HEAD_DIM // 2)
    ac = _rope_angles(col_idx, HEAD_DIM // 2)
    ones, zeros = jnp.ones((s, MLA_NOPE), F32), jnp.zeros((s, MLA_NOPE), F32)
    mla_cos = jnp.tile(jnp.concatenate([ones, jnp.cos(a1), jnp.cos(a1)], axis=1), (1, MLA_HEADS))
    mla_sin = jnp.tile(jnp.concatenate([zeros, -jnp.sin(a1), jnp.sin(a1)], axis=1), (1, MLA_HEADS))
    gqa_cos = jnp.tile(jnp.concatenate([jnp.cos(ar), jnp.cos(ar), jnp.cos(ac), jnp.cos(ac)], axis=1), (1, GQA_HEADS))
    gqa_sin = jnp.tile(jnp.concatenate([-jnp.sin(ar), jnp.sin(ar), -jnp.sin(ac), jnp.sin(ac)], axis=1), (1, GQA_HEADS))
    return mla_cos, mla_sin, gqa_cos, gqa_sin


def _t5_bucket(rel):
    nb = T5_BUCKETS // 2
    max_exact = nb // 2
    n = jnp.abs(rel)
    nf = jnp.maximum(n, 1).astype(F32)
    large = max_exact + (jnp.log(nf / max_exact) / math.log(T5_MAX_DIST / max_exact) * (nb - max_exact)).astype(jnp.int32)
    large = jnp.minimum(large, nb - 1)
    return jnp.where(rel > 0, nb, 0) + jnp.where(n < max_exact, n, large)


def _band_bias(table, stride, head_lo, heads, half_window):
    b = BAND_BLOCK
    offs = jnp.arange(3 * b)[None, :] - b - jnp.arange(b)[:, None]
    one_hot = (_t5_bucket(offs * stride)[..., None] == jnp.arange(T5_BUCKETS)).astype(F32)
    bias = jnp.dot(one_hot.reshape(b * 3 * b, T5_BUCKETS), table[:, head_lo:head_lo + heads],
                   precision=lax.Precision.HIGHEST)
    bias = bias.T.reshape(heads, b, 3 * b)
    return jnp.where((jnp.abs(offs) <= half_window)[None], bias, NEG_INF)


def _w_in_layout(w_in):
    d = w_in.shape[0]
    at, pieces = 0, {}
    for name, width in (("a_q", 256), ("a_kv", 128), ("a_kr", 32), ("b_q", 256), ("b_k", 128), ("b_v", 128),
                        ("c_q", 768), ("c_k", 768), ("c_v", 768), ("d_q", 256), ("d_k", 128), ("d_v", 128),
                        ("gate", N_BRANCH * BRANCH_W), ("merge", N_BRANCH * d)):
        pieces[name] = w_in[:, at:at + width]
        at += width

    def zeros(n):
        return jnp.zeros((d, n), F32)

    front = [pieces["gate"], pieces["merge"]]
    mla = [pieces["a_q"], pieces["a_kv"], pieces["a_kr"], zeros(SMALL_W - 416)]
    gqa = [pieces["b_q"], pieces["b_k"], pieces["b_v"]]
    win = [pieces["d_q"], pieces["d_k"], pieces["d_v"], zeros(BAND_W - 512)]
    dil = [jnp.concatenate([pieces[n][:, g * QW:(g + 1) * QW] for n in ("c_q", "c_k", "c_v")], axis=1)
           for g in range(len(DIL_PATTERNS))]
    main = jnp.concatenate(front + mla + gqa + win + dil[:1], axis=1)
    assert GM_W == MLA_BLK * SMALL_W and GM_W + 2 * SMALL_W == WIN_BLK * BAND_W and main.shape[1] == P_TOT
    return [main] + dil[1:]


def _layer(x, w, l, aux, biases):
    xn = _norm(x, w["norm_g"][l][None, :])
    w_main, w_dil1, w_dil2 = _w_in_layout(w["w_in"][l])
    projs = [_matmul(xn, w_main, "proj"),
             _matmul(_to_residues(xn, DIL_PATTERNS[1][1]), w_dil1, "proj_dil1"),
             _matmul(_to_residues(xn, DIL_PATTERNS[2][1]), w_dil2, "proj_dil2")]
    w_kv = w["w_mla_kv_up"][l].reshape(MLA_KV_LORA, MLA_HEADS, MLA_NOPE + MLA_V)
    w_k = jnp.concatenate([w_kv[:, :, :MLA_NOPE], jnp.zeros((MLA_KV_LORA, MLA_HEADS, MLA_ROPE), F32)], axis=2)
    dil_bias, win_bias = biases
    prm = dict(
        g_q=w["mla_q_norm_g"][l][None, :], g_kv=w["mla_kv_norm_g"][l][None, :], w_q=w["w_mla_q_up"][l],
        w_k=w_k.reshape(MLA_KV_LORA, MLA_HEADS * MLA_QK),
        w_v=w_kv[:, :, MLA_NOPE:].reshape(MLA_KV_LORA, MLA_HEADS * MLA_V),
        gq=jnp.tile(w["gqa_q_norm_g"][l], GQA_HEADS)[None, :], gk=jnp.tile(w["gqa_k_norm_g"][l], GQA_KV_HEADS)[None, :],
        bias_dil=list(dil_bias), bias_win=win_bias, sink=w["win_sink"][l].reshape(WIN_HEADS, 1, 1),
        no_sink=jnp.full((DIL_HEADS, 1, 1), NEG_INF, F32), w_branch=w["w_branch"][l])
    return x + _matmul(_mixer(projs, prm, aux), w["w_out"][l], "out_proj")


def _local_loss(w, x, target):
    s, d_model = x.shape
    assert d_model == D_MODEL, "the projection's window layout is laid out for d_model 1024"
    place = np.zeros((MLA_ROPE, MLA_HEADS * MLA_QK), np.float32)
    for h in range(MLA_HEADS):
        for i in range(MLA_ROPE):
            place[i, h * MLA_QK + MLA_NOPE + i] = 1.0

    def head_mean(nh):
        m = np.kron(np.eye(nh, dtype=np.float32), np.full((HEAD_DIM, HEAD_DIM), 1.0 / HEAD_DIM, np.float32))
        return jnp.asarray(m)

    aux = _rope_tables(s) + (jnp.asarray(place), head_mean(GQA_HEADS), head_mean(GQA_KV_HEADS))
    table = w["t5_table"]
    dil_bias = [_band_bias(table, dil, gi * DIL_HEADS, DIL_HEADS, window // (2 * dil))
                for gi, (window, dil) in enumerate(DIL_PATTERNS)]
    win_bias = _band_bias(table, 1, len(DIL_PATTERNS) * DIL_HEADS, WIN_HEADS, WIN_HALF)
    for l in range(w["norm_g"].shape[0]):
        x = _layer(x, w, l, aux, (dil_bias, win_bias))
    return _loss_op(x, target, w["final_norm_g"][None, :])


_ANY = pl.BlockSpec(memory_space=pl.ANY)
_MESH = pl.DeviceIdType.MESH


def _all_gather(block, name):
    def body(x_ref, out_ref, send_sems, recv_sems, local_sem):
        x, y, c = lax.axis_index("x"), lax.axis_index("y"), lax.axis_index("c")
        me, sibling = (x, y, c), (x, y, 1 - c)
        chips = [(1 - x, y), (x, 1 - y), (1 - x, 1 - y)]

        def slot(px, py, pc):
            return out_ref.at[4 * px + 2 * py + pc]

        def copy(k, blk, to, src=None):
            return pltpu.make_async_remote_copy(
                src_ref=slot(*blk) if src is None else src, dst_ref=slot(*blk),
                send_sem=send_sems.at[k], recv_sem=recv_sems.at[k], device_id=to, device_id_type=_MESH)

        mine = pltpu.make_async_copy(x_ref, slot(*me), local_sem)
        mine.start()
        first = [copy(0, me, sibling, src=x_ref)]
        first += [copy(1 + j, me, (*chip, c), src=x_ref) for j, chip in enumerate(chips)]
        for cp in first:
            cp.start()
        passed = [copy(4 + j, (*chip, c), sibling) for j, chip in enumerate(chips)]
        for j, chip in enumerate(chips):
            copy(1 + j, (*chip, c), me).wait_recv()
            passed[j].start()
        copy(0, sibling, me).wait_recv()
        for j, chip in enumerate(chips):
            copy(4 + j, (*chip, 1 - c), me).wait_recv()
        for cp in first + passed:
            cp.wait_send()
        mine.wait()

    return pl.pallas_call(
        body,
        out_shape=jax.ShapeDtypeStruct((N_DEV,) + block.shape, block.dtype),
        in_specs=[_ANY],
        out_specs=_ANY,
        scratch_shapes=[pltpu.SemaphoreType.DMA((7,)), pltpu.SemaphoreType.DMA((7,)), pltpu.SemaphoreType.DMA],
        name=name,
    )(block)


def _all_to_all(blocks, name):
    def body(x_ref, out_ref, send_sems, recv_sems, local_sem):
        x, y, c = lax.axis_index("x"), lax.axis_index("y"), lax.axis_index("c")
        me = 4 * x + 2 * y + c
        mine = pltpu.make_async_copy(x_ref.at[me], out_ref.at[me], local_sem)
        mine.start()
        copies, landed = [], []
        for k in range(1, N_DEV):
            px = 1 - x if k & 4 else x
            py = 1 - y if k & 2 else y
            pc = 1 - c if k & 1 else c
            peer = 4 * px + 2 * py + pc
            copies.append(pltpu.make_async_remote_copy(
                src_ref=x_ref.at[peer], dst_ref=out_ref.at[me], send_sem=send_sems.at[k - 1],
                recv_sem=recv_sems.at[k - 1], device_id=(px, py, pc), device_id_type=_MESH))
            landed.append(pltpu.make_async_remote_copy(
                src_ref=x_ref.at[peer], dst_ref=out_ref.at[peer], send_sem=send_sems.at[k - 1],
                recv_sem=recv_sems.at[k - 1], device_id=(px, py, pc), device_id_type=_MESH))
        for cp in copies:
            cp.start()
        for cp in landed:
            cp.wait_recv()
        for cp in copies:
            cp.wait_send()
        mine.wait()

    return pl.pallas_call(
        body,
        out_shape=jax.ShapeDtypeStruct(blocks.shape, blocks.dtype),
        in_specs=[_ANY],
        out_specs=_ANY,
        scratch_shapes=[pltpu.SemaphoreType.DMA((7,)), pltpu.SemaphoreType.DMA((7,)), pltpu.SemaphoreType.DMA],
        name=name,
    )(blocks)


def _sum_slots(parts, name):
    _, rows, w = parts.shape
    tr = _pick(rows, (1024, 512, 256, 128, 64, 32, 16, 8))

    def body(p_ref, o_ref):
        acc = p_ref[0].astype(F32)
        for j in range(1, N_DEV):
            acc = acc + p_ref[j].astype(F32)
        o_ref[...] = acc

    return pl.pallas_call(
        body,
        grid=(rows // tr,),
        in_specs=[pl.BlockSpec((N_DEV, tr, w), lambda i: (0, i, 0))],
        out_specs=pl.BlockSpec((tr, w), lambda i: (i, 0)),
        out_shape=jax.ShapeDtypeStruct((rows, w), F32),
        name=name,
        compiler_params=_params("parallel"),
    )(parts)


def _adamw(w, g, m, v, name):
    rows, width = w.shape
    tr = _pick(rows, (1024, 512, 256, 128, 64, 32, 16, 8))

    def body(w_ref, g_ref, m_ref, v_ref, d_ref, nm_ref, nv_ref):
        g_ = g_ref[...]
        m_ = ADAM_B1 * m_ref[...] + (1.0 - ADAM_B1) * g_
        v_ = ADAM_B2 * v_ref[...] + (1.0 - ADAM_B2) * jnp.square(g_)
        m_hat = m_ / (1.0 - ADAM_B1 ** ADAM_STEP)
        v_hat = v_ / (1.0 - ADAM_B2 ** ADAM_STEP)
        d_ref[...] = -ADAM_LR * (m_hat / (jnp.sqrt(v_hat) + ADAM_EPS) + ADAM_WD * w_ref[...])
        nm_ref[...] = m_
        nv_ref[...] = v_

    spec = pl.BlockSpec((tr, width), lambda i: (i, 0))
    return pl.pallas_call(
        body,
        grid=(rows // tr,),
        in_specs=[spec] * 4,
        out_specs=[spec] * 3,
        out_shape=[jax.ShapeDtypeStruct((rows, width), F32)] * 3,
        name=name,
        compiler_params=_params("parallel"),
    )(w, g, m, v)


_SHARDED = (("w_in", 2), ("w_mla_q_up", 2), ("w_mla_kv_up", 2), ("w_branch", 3), ("w_out", 1))
_REPLICATED = ("norm_g", "mla_q_norm_g", "mla_kv_norm_g", "gqa_q_norm_g", "gqa_k_norm_g", "win_sink", "t5_table",
               "final_norm_g")


def _pack(arrays, row_multiple):
    flat = jnp.concatenate([a.reshape(-1) for a in arrays])
    rows = -(-flat.shape[0] // (LANES * row_multiple)) * row_multiple
    return jnp.pad(flat, (0, rows * LANES - flat.shape[0])).reshape(rows, LANES)


def _unpack(packed, shapes):
    flat, out, at = packed.reshape(-1), [], 0
    for shp in shapes:
        n = int(np.prod(shp))
        out.append(flat[at:at + n].reshape(shp))
        at += n
    return out


def _split_shards(full, axis):
    shp = full.shape
    t = full.reshape(shp[:axis] + (N_DEV, shp[axis] // N_DEV) + shp[axis + 1:])
    return jnp.moveaxis(t, axis, 0)


def _join_shards(stacked, axis):
    t = jnp.moveaxis(stacked, 0, axis)
    shp = t.shape
    return t.reshape(shp[:axis] + (shp[axis] * shp[axis + 1],) + shp[axis + 2:])


def kernel(x, norm_g, w_in, mla_q_norm_g, mla_kv_norm_g, w_mla_q_up, w_mla_kv_up, gqa_q_norm_g, gqa_k_norm_g, win_sink, t5_table, w_branch, w_out, final_norm_g, loss_target, m_norm_g, m_w_in, m_mla_q_norm_g, m_mla_kv_norm_g, m_w_mla_q_up, m_w_mla_kv_up, m_gqa_q_norm_g, m_gqa_k_norm_g, m_win_sink, m_t5_table, m_w_branch, m_w_out, m_final_norm_g, v_norm_g, v_w_in, v_mla_q_norm_g, v_mla_kv_norm_g, v_w_mla_q_up, v_w_mla_kv_up, v_gqa_q_norm_g, v_gqa_k_norm_g, v_win_sink, v_t5_table, v_w_branch, v_w_out, v_final_norm_g):
    given = dict(locals())
    names = ("norm_g", "w_in", "mla_q_norm_g", "mla_kv_norm_g", "w_mla_q_up", "w_mla_kv_up", "gqa_q_norm_g",
             "gqa_k_norm_g", "win_sink", "t5_table", "w_branch", "w_out", "final_norm_g")
    shard_names = [n for n, _ in _SHARDED]
    shard_shapes = [given[n].shape for n in shard_names]

    mine = _pack([given[n] for n in shard_names], 16)
    gathered = _all_gather(mine.astype(BF16), "gather_weights").astype(F32)
    per_dev = [_unpack(gathered[j], shard_shapes) for j in range(N_DEV)]
    weights = {n: given[n] for n in _REPLICATED}
    for wi, (n, axis) in enumerate(_SHARDED):
        weights[n] = _join_shards(jnp.stack([per_dev[j][wi] for j in range(N_DEV)]), axis)

    loss, (gw, gx) = jax.value_and_grad(_local_loss, argnums=(0, 1))(weights, x[0], loss_target[0])
    loss = lax.psum(loss, ("x", "y", "c"))

    blocks = [_split_shards(gw[n], axis) for n, axis in _SHARDED]
    send = jnp.stack([_pack([b[j] for b in blocks], 16) for j in range(N_DEV)])
    g_shard = _unpack(_sum_slots(_all_to_all(send.astype(BF16), "scatter_grads"), "sum_grads"), shard_shapes)
    rep_shapes = [given[n].shape for n in _REPLICATED]
    g_rep = _unpack(_sum_slots(_all_gather(_pack([gw[n] for n in _REPLICATED], 8), "gather_small_grads"),
                               "sum_small_grads"), rep_shapes)
    grads = dict(zip(shard_names, g_shard))
    grads.update(zip(_REPLICATED, g_rep))

    def update(group, shapes, row_multiple, name):
        outs = _adamw(*[_pack([src[n] for n in group], row_multiple) for src in (
            given, grads, {n: given["m_" + n] for n in group}, {n: given["v_" + n] for n in group})], name)
        return [dict(zip(group, _unpack(o, shapes))) for o in outs]

    big = update(shard_names, shard_shapes, 16, "adamw_shards")
    small = update(list(_REPLICATED), rep_shapes, 8, "adamw_replicated")
    delta, new_m, new_v = [{**b, **s_} for b, s_ in zip(big, small)]
    return (loss, gx[None], *[grads[n] for n in names], *[delta[n] for n in names],
            *[new_m[n] for n in names], *[new_v[n] for n in names])
```

```python
import functools
import math

import jax
import jax.numpy as jnp
import numpy as np
from jax import lax
from jax.experimental import pallas as pl
from jax.experimental.pallas import tpu as pltpu

F32 = jnp.float32
BF16 = jnp.bfloat16
N_DEV = 8
LANES = 128
HALF = LANES // 2
V7X_VMEM_LIMIT = 56 * 1024 * 1024

EPS = 1e-6
NEG_INF = -1e30
LOG2E = 1.4426950408889634
ROPE_THETA = 10000.0
GRID_W = 64
HEAD_DIM = 64
N_BRANCH = 4
BRANCH_W = 256
MLA_HEADS, MLA_Q_LORA, MLA_KV_LORA, MLA_NOPE, MLA_ROPE, MLA_V = 4, 256, 128, 64, 32, 64
MLA_QK = MLA_NOPE + MLA_ROPE
GQA_HEADS, GQA_KV_HEADS = 4, 2
DIL_PATTERNS = ((128, 1), (512, 4), (2048, 16))
DIL_HEADS = 4
WIN_HEADS, WIN_KV_HEADS, WIN_HALF = 4, 2, 128
T5_BUCKETS, T5_MAX_DIST = 32, 1024
BAND_BLOCK = 128
ADAM_LR, ADAM_B1, ADAM_B2, ADAM_EPS, ADAM_WD, ADAM_STEP = 0.001, 0.9, 0.999, 1e-08, 0.01, 10

D_MODEL = 1024
GM_W, SMALL_W, BAND_W = 5120, 512, 768
MLA_BLK, GQA_BLK, WIN_BLK, DIL_BLK = 10, 11, 8, 9
P_TOT = 7680
QW = 256


def _params(*sem):
    return pltpu.CompilerParams(dimension_semantics=sem, vmem_limit_bytes=V7X_VMEM_LIMIT)


def _pick(n, cands):
    for c in cands:
        if n % c == 0:
            return c
    return n


def _dot(a, b, ca, cb):
    return lax.dot_general(a.astype(BF16), b.astype(BF16), (((ca,), (cb,)), ((), ())), preferred_element_type=F32)


def _bmm(a, b, ca, cb):
    return lax.dot_general(a, b, (((ca,), (cb,)), ((0,), (0,))), preferred_element_type=F32)


@jax.custom_vjp
def _bdot(a, b):
    return _dot(a, b, 1, 0)


def _bdot_fwd(a, b):
    return _dot(a, b, 1, 0), (a, b)


def _bdot_bwd(res, g):
    a, b = res
    return _dot(g, b, 1, 1), _dot(a, g, 0, 0)


_bdot.defvjp(_bdot_fwd, _bdot_bwd)


def _hdot(a, c):
    return lax.dot_general(a, c, (((1,), (0,)), ((), ())), precision=lax.Precision.HIGHEST, preferred_element_type=F32)


@functools.partial(jax.custom_vjp, nondiff_argnums=(1,))
def _lane_roll(x, shift):
    return pltpu.roll(x, shift, 1)


def _lane_roll_fwd(x, shift):
    return pltpu.roll(x, shift, 1), None


def _lane_roll_bwd(shift, _, g):
    return (pltpu.roll(g, g.shape[1] - shift, 1),)


_lane_roll.defvjp(_lane_roll_fwd, _lane_roll_bwd)


@functools.partial(jax.custom_vjp, nondiff_argnums=(1,))
def _lane_ranges(x, cut):
    bounds, _ = cut
    return tuple(x[:, lo:hi] for lo, hi in zip(bounds[:-1], bounds[1:]))


def _lane_ranges_fwd(x, cut):
    return _lane_ranges(x, cut), None


def _lane_ranges_bwd(cut, _, cts):
    bounds, width = cut
    parts = list(cts)
    if bounds[-1] < width:
        parts.append(jnp.zeros((cts[0].shape[0], width - bounds[-1]), cts[0].dtype))
    return (jnp.concatenate(parts, axis=1),)


_lane_ranges.defvjp(_lane_ranges_fwd, _lane_ranges_bwd)


def _lanes(x, bounds):
    return _lane_ranges(x, (tuple(bounds), x.shape[1]))


@jax.custom_vjp
def _unstack(x):
    return tuple(x[i] for i in range(x.shape[0]))


def _unstack_fwd(x):
    return _unstack(x), None


def _unstack_bwd(_, cts):
    return (jnp.stack(cts, axis=0),)


_unstack.defvjp(_unstack_fwd, _unstack_bwd)


@functools.partial(jax.custom_vjp, nondiff_argnums=(1,))
def _split_heads(x, h):
    d = x.shape[1] // h
    return jnp.stack([x[:, i * d:(i + 1) * d] for i in range(h)], axis=0)


def _split_heads_fwd(x, h):
    return _split_heads(x, h), None


def _split_heads_bwd(h, _, ct):
    return (jnp.concatenate([ct[i] for i in range(h)], axis=1),)


_split_heads.defvjp(_split_heads_fwd, _split_heads_bwd)


def _join_heads(x):
    return jnp.concatenate(_unstack(x), axis=1)


def _rope(x, cos_t, sin_t, half):
    w = x.shape[1]
    lane = lax.broadcasted_iota(jnp.int32, (1, w), 1)
    first = (lane % (2 * half)) < half
    partner = jnp.where(first, _lane_roll(x, w - half), _lane_roll(x, half))
    return x * cos_t + partner * sin_t


def _rms(x, g):
    return x * lax.rsqrt(jnp.mean(x * x, axis=-1, keepdims=True) + EPS) * g


def _rows(tr, w, col=0):
    return pl.BlockSpec((tr, w), lambda i: (i, col))


def _head_rows(h, tr, d):
    return pl.BlockSpec((h, tr, d), lambda i: (0, i, 0))


def _whole(shape):
    nd = len(shape)
    return pl.BlockSpec(tuple(shape), lambda i: (0,) * nd)


def _fwd_call(name, fn, steps, rows, params, aux, outs):
    nr, npar, na = len(rows), len(params), len(aux)

    def body(*refs):
        vals = [x[...].astype(F32) for x in refs[:nr + npar + na]]
        res = fn(vals[:nr], vals[nr:nr + npar], vals[nr + npar:])
        for o_ref, o in zip(refs[nr + npar + na:], res):
            o_ref[...] = o.astype(o_ref.dtype)

    return pl.pallas_call(
        body,
        grid=(steps,),
        in_specs=[s for _, s in rows] + [_whole(p.shape) for p in params] + [s for _, s in aux],
        out_specs=[e[1] for e in outs],
        out_shape=[jax.ShapeDtypeStruct(e[0], e[2] if len(e) > 2 else F32) for e in outs],
        name=name + "_fwd",
        compiler_params=_params("parallel"),
    )(*[a for a, _ in rows], *params, *[a for a, _ in aux])


def _vjp_call(name, fn, steps, rows, params, aux, cts, row_grads, into=None):
    nr, npar, na, nc = len(rows), len(params), len(aux), len(cts)
    n_in = nr + npar + na + nc
    lead = 0 if into is None else 1

    def body(*refs):
        refs = refs[lead:]
        vals = [x[...].astype(F32) for x in refs[:n_in]]
        r, p, a, d = vals[:nr], vals[nr:nr + npar], vals[nr + npar:nr + npar + na], vals[nr + npar + na:]
        out_refs = refs[n_in:]
        _, vjp = jax.vjp(lambda r_, p_: tuple(fn(r_, p_, a)), r, p)
        dr, dp = vjp(tuple(d))
        for o_ref, o in zip(out_refs[:nr], dr):
            o_ref[...] = o.astype(o_ref.dtype)

        @pl.when(pl.program_id(0) == 0)
        def _():
            for o_ref in out_refs[nr:]:
                o_ref[...] = jnp.zeros_like(o_ref)

        for o_ref, o in zip(out_refs[nr:], dp):
            o_ref[...] += o

    outs = pl.pallas_call(
        body,
        grid=(steps,),
        in_specs=([] if into is None else [pl.BlockSpec(memory_space=pl.ANY)])
        + [s for _, s in rows] + [_whole(p.shape) for p in params] + [s for _, s in aux] + [s for _, s in cts],
        out_specs=[e[1] for e in row_grads] + [_whole(p.shape) for p in params],
        out_shape=[jax.ShapeDtypeStruct(e[0], e[2] if len(e) > 2 else F32) for e in row_grads]
        + [jax.ShapeDtypeStruct(p.shape, F32) for p in params],
        input_output_aliases={} if into is None else {0: 0},
        name=name + "_bwd",
        compiler_params=_params("arbitrary"),
    )(*([] if into is None else [into]), *[a for a, _ in rows], *params, *[a for a, _ in aux], *[a for a, _ in cts])
    return list(outs[:nr]), list(outs[nr:])


def _norm_tile(r, p, a):
    return (_rms(r[0], p[0]),)


def _mm(a, b, mode, name, out_dtype=F32):
    if mode == "nn":
        (m, k), n = a.shape, b.shape[1]
    elif mode == "nt":
        (m, k), n = a.shape, b.shape[0]
    else:
        (k, m), n = a.shape, b.shape[1]
    tm = _pick(m, (1024, 512, 256, 128))
    tn = _pick(n, (1024, 768, 512, 384, 256, 128))
    tk = _pick(k, (1024, 768, 512, 384, 256, 128))
    if mode == "tn":
        tk = _pick(k, (512, 256, 128))
    nk = k // tk

    def body(a_ref, b_ref, o_ref, acc_ref):
        kk = pl.program_id(2)
        if mode == "nn":
            part = _dot(a_ref[...], b_ref[...], 1, 0)
        elif mode == "nt":
            part = _dot(a_ref[...], b_ref[...], 1, 1)
        else:
            part = _dot(a_ref[...], b_ref[...], 0, 0)
        if nk == 1:
            o_ref[...] = part.astype(o_ref.dtype)
        else:
            @pl.when(kk == 0)
            def _():
                acc_ref[...] = part

            @pl.when(kk > 0)
            def _():
                acc_ref[...] += part

            @pl.when(kk == nk - 1)
            def _():
                o_ref[...] = acc_ref[...].astype(o_ref.dtype)

    if mode == "nn":
        a_spec = pl.BlockSpec((tm, tk), lambda i, j, kk: (i, kk))
        b_spec = pl.BlockSpec((tk, tn), lambda i, j, kk: (kk, j))
    elif mode == "nt":
        a_spec = pl.BlockSpec((tm, tk), lambda i, j, kk: (i, kk))
        b_spec = pl.BlockSpec((tn, tk), lambda i, j, kk: (j, kk))
    else:
        a_spec = pl.BlockSpec((tk, tm), lambda i, j, kk: (kk, i))
        b_spec = pl.BlockSpec((tk, tn), lambda i, j, kk: (kk, j))
    return pl.pallas_call(
        body,
        grid=(m // tm, n // tn, nk),
        in_specs=[a_spec, b_spec],
        out_specs=pl.BlockSpec((tm, tn), lambda i, j, kk: (i, j)),
        out_shape=jax.ShapeDtypeStruct((m, n), out_dtype),
        scratch_shapes=[pltpu.VMEM((tm, tn), F32)],
        name=name,
        compiler_params=_params("parallel", "parallel", "arbitrary"),
    )(a, b)


def _dense_fwd_call(q, k, v, scale, name):
    n, sq, d = q.shape
    sk, dv = k.shape[1], v.shape[2]
    tq = _pick(sq, (256, 128))
    c = scale * LOG2E

    def body(q_ref, k_ref, v_ref, o_ref, lse_ref, k_s, vext_s):
        @pl.when(pl.program_id(1) == 0)
        def _():
            k_s[...] = k_ref[0].astype(BF16)
            vext_s[...] = jnp.ones_like(vext_s)
            vext_s[:, :dv] = v_ref[0].astype(BF16)

        s = _dot(q_ref[0], k_s[...], 1, 1)
        m = jnp.max(s, axis=1, keepdims=True)
        p = jnp.exp2(s * c - m * c)
        acc = _dot(p, vext_s[...], 1, 0)
        l = acc[:, dv:dv + 1]
        o_ref[0] = acc[:, :dv] / l
        lse_ref[0] = m * scale + jnp.log(l)

    return pl.pallas_call(
        body,
        grid=(n, sq // tq),
        in_specs=[
            pl.BlockSpec((1, tq, d), lambda h, i: (h, i, 0)),
            pl.BlockSpec((1, sk, d), lambda h, i: (h, 0, 0)),
            pl.BlockSpec((1, sk, dv), lambda h, i: (h, 0, 0)),
        ],
        out_specs=[
            pl.BlockSpec((1, tq, dv), lambda h, i: (h, i, 0)),
            pl.BlockSpec((1, tq, 1), lambda h, i: (h, i, 0)),
        ],
        out_shape=[jax.ShapeDtypeStruct((n, sq, dv), F32), jax.ShapeDtypeStruct((n, sq, 1), F32)],
        scratch_shapes=[pltpu.VMEM((sk, d), BF16), pltpu.VMEM((sk, 2 * dv), BF16)],
        name=name + "_fwd",
        compiler_params=_params("arbitrary", "arbitrary"),
    )(q, k, v)


def _dense_bwd_call(q, k, v, o, lse, do, scale, name):
    n, sq, d = q.shape
    sk, dv = k.shape[1], v.shape[2]
    tq, tk = _pick(sq, (512, 256, 128)), _pick(sk, (2048, 1024, 512, 256, 128))
    c = scale * LOG2E

    def body(q_ref, k_ref, v_ref, o_ref, lse_ref, do_ref, dq_ref, dk_ref, dv_ref):
        j, i = pl.program_id(1), pl.program_id(2)
        qb, kb, vb = q_ref[0].astype(BF16), k_ref[0].astype(BF16), v_ref[0].astype(BF16)
        do_f = do_ref[0]
        dob = do_f.astype(BF16)
        p = jnp.exp2(_dot(qb, kb, 1, 1) * c - lse_ref[0] * LOG2E)
        delta = jnp.sum(do_f * o_ref[0], axis=1, keepdims=True)
        ds = (p * (_dot(dob, vb, 1, 1) - delta)).astype(BF16)
        dv_part = _dot(p, dob, 0, 0)
        dk_part = _dot(ds, qb, 0, 0) * scale
        dq_part = _dot(ds, kb, 1, 0) * scale
        rows = pl.ds(pl.multiple_of(i * tq, tq), tq)

        @pl.when(i == 0)
        def _():
            dk_ref[0] = dk_part
            dv_ref[0] = dv_part

        @pl.when(i > 0)
        def _():
            dk_ref[0] += dk_part
            dv_ref[0] += dv_part

        @pl.when(j == 0)
        def _():
            dq_ref[0, rows, :] = dq_part

        @pl.when(j > 0)
        def _():
            dq_ref[0, rows, :] += dq_part

    return pl.pallas_call(
        body,
        grid=(n, sk // tk, sq // tq),
        in_specs=[
            pl.BlockSpec((1, tq, d), lambda h, j, i: (h, i, 0)),
            pl.BlockSpec((1, tk, d), lambda h, j, i: (h, j, 0)),
            pl.BlockSpec((1, tk, dv), lambda h, j, i: (h, j, 0)),
            pl.BlockSpec((1, tq, dv), lambda h, j, i: (h, i, 0)),
            pl.BlockSpec((1, tq, 1), lambda h, j, i: (h, i, 0)),
            pl.BlockSpec((1, tq, dv), lambda h, j, i: (h, i, 0)),
        ],
        out_specs=[
            pl.BlockSpec((1, sq, d), lambda h, j, i: (h, 0, 0)),
            pl.BlockSpec((1, tk, d), lambda h, j, i: (h, j, 0)),
            pl.BlockSpec((1, tk, dv), lambda h, j, i: (h, j, 0)),
        ],
        out_shape=[
            jax.ShapeDtypeStruct((n, sq, d), F32),
            jax.ShapeDtypeStruct((n, sk, d), F32),
            jax.ShapeDtypeStruct((n, sk, dv), F32),
        ],
        name=name + "_bwd",
        compiler_params=_params("arbitrary", "arbitrary", "arbitrary"),
    )(q, k, v, o, lse, do)


def _head_geometry(h, group):
    pair, a = divmod(h, 2)
    kv_pair, b = divmod(h // group, 2)
    return pair, a, kv_pair, b


def _lane_half():
    return lax.broadcasted_iota(jnp.int32, (1, LANES), 1) // HALF


def _align(x, a, b):
    if a != b:
        x = pltpu.roll(x, HALF, 1)
    return jnp.where(_lane_half() == b, x, 0.0)


def _unalign(x, a, b):
    x = jnp.where(_lane_half() == b, x, 0.0)
    return pltpu.roll(x, HALF, 1) if a != b else x


def _bands(w, pw, nw, lo, kvw, nb):
    b = BAND_BLOCK
    cat = jnp.concatenate([pw[:, lo:lo + kvw], w[:, lo:lo + kvw], nw[:, lo:lo + kvw]], axis=0).astype(BF16)
    out = []
    for g in range(kvw // LANES):
        c3 = cat[:, g * LANES:(g + 1) * LANES].reshape(nb + 2, b, LANES)
        out.append(jnp.concatenate([c3[0:nb], c3[1:nb + 1], c3[2:nb + 2]], axis=1))
    return out


def _edge_mask(first_block, nb, n_blocks):
    b = BAND_BLOCK
    blk = first_block + lax.broadcasted_iota(jnp.int32, (nb, 1, 3 * b), 0)
    col = lax.broadcasted_iota(jnp.int32, (nb, 1, 3 * b), 2)
    outside = ((col < b) & (blk == 0)) | ((col >= 2 * b) & (blk == n_blocks - 1))
    return jnp.where(outside, NEG_INF, 0.0)


def _band_geometry(proj, dil):
    seq = proj.shape[0] // dil
    tl = _pick(seq, (1024, 512, 256, 128))
    return seq, tl, tl // BAND_BLOCK, seq // tl


def _band_in_specs(tl, nb, n_chunks, n_blocks, col, last_step_idle):
    def chunk(i):
        return jnp.minimum(i, n_chunks - 1) if last_step_idle else i

    main = pl.BlockSpec((tl, BAND_W), lambda j, i: (j * n_chunks + chunk(i), col))
    prev = pl.BlockSpec((BAND_BLOCK, BAND_W),
                        lambda j, i: (j * n_blocks + jnp.maximum(chunk(i) * nb - 1, 0), col))
    nxt = pl.BlockSpec((BAND_BLOCK, BAND_W),
                       lambda j, i: (j * n_blocks + jnp.minimum((chunk(i) + 1) * nb, n_blocks - 1), col))
    rows = pl.BlockSpec((tl, QW), lambda j, i: (j * n_chunks + chunk(i), 0))
    return main, prev, nxt, rows


def _band_fwd_call(proj, col, bias, sink, dil, group, kvw, scale, name):
    s_tok = proj.shape[0]
    seq, tl, nb, n_chunks = _band_geometry(proj, dil)
    n_blocks = seq // BAND_BLOCK
    heads = bias.shape[0]

    def body(w_ref, pw_ref, nw_ref, bias_ref, sink_ref, o_ref, lse_ref):
        i = pl.program_id(1)
        w, pw, nw = w_ref[...], pw_ref[...], nw_ref[...]
        kb = _bands(w, pw, nw, QW, kvw, nb)
        vb = _bands(w, pw, nw, QW + kvw, kvw, nb)
        edge = _edge_mask(i * nb, nb, n_blocks)
        o_acc = [jnp.zeros((tl, LANES), F32) for _ in range(heads // 2)]
        lse_acc = [jnp.zeros((tl, LANES), F32) for _ in range(heads // 2)]
        for h in range(heads):
            pair, a, kvp, b = _head_geometry(h, group)
            q_al = _align(w[:, pair * LANES:(pair + 1) * LANES], a, b).astype(BF16).reshape(nb, BAND_BLOCK, LANES)
            logits = _bmm(q_al, kb[kvp], 2, 2) * scale + bias_ref[h][None] + edge
            sk = sink_ref[h].reshape(1, 1, 1)
            m = jnp.maximum(jnp.max(logits, axis=2, keepdims=True), sk)
            e = jnp.exp(logits - m)
            ssum = jnp.sum(e, axis=2, keepdims=True) + jnp.exp(sk - m)
            out = _bmm(e.astype(BF16), vb[kvp], 2, 1) / ssum
            o_acc[pair] = o_acc[pair] + _unalign(out.reshape(tl, LANES), a, b)
            lse = (m + jnp.log(ssum)).reshape(tl, 1)
            lse_acc[pair] = lse_acc[pair] + jnp.where(_lane_half() == a, lse, 0.0)
        o_ref[...] = jnp.concatenate(o_acc, axis=1)
        lse_ref[...] = jnp.concatenate(lse_acc, axis=1)

    main, prev, nxt, rows = _band_in_specs(tl, nb, n_chunks, n_blocks, col, False)
    return pl.pallas_call(
        body,
        grid=(dil, n_chunks),
        in_specs=[main, prev, nxt, pl.BlockSpec(bias.shape, lambda j, i: (0, 0, 0)),
                  pl.BlockSpec(sink.shape, lambda j, i: (0, 0, 0))],
        out_specs=[rows, rows],
        out_shape=[jax.ShapeDtypeStruct((s_tok, QW), F32)] * 2,
        name=name + "_fwd",
        compiler_params=_params("parallel", "parallel"),
    )(proj, proj, proj, bias, sink)


def _band_bwd_call(proj, o, do, lse, dlse, bias, sink, dproj, col, dil, group, kvw, scale, name):
    seq, tl, nb, n_chunks = _band_geometry(proj, dil)
    lead = 0 if dproj is None else 1
    n_blocks = seq // BAND_BLOCK
    heads = bias.shape[0]
    b_ = BAND_BLOCK
    have_dlse = dlse is not None

    def body(*refs):
        (w_ref, pw_ref, nw_ref, o_ref, do_ref, lse_ref), refs = refs[lead:lead + 6], refs[lead + 6:]
        if have_dlse:
            dlse_ref, refs = refs[0], refs[1:]
        bias_ref, sink_ref, dwin_ref, dbias_ref, dsink_ref, dq_s, dk_s, dv_s = refs
        j, i = pl.program_id(0), pl.program_id(1)

        @pl.when((j == 0) & (i == 0))
        def _():
            dbias_ref[...] = jnp.zeros_like(dbias_ref)
            dsink_ref[...] = jnp.zeros_like(dsink_ref)

        @pl.when(i == 0)
        def _():
            dk_s[...] = jnp.zeros_like(dk_s)
            dv_s[...] = jnp.zeros_like(dv_s)

        @pl.when(i < n_chunks)
        def _():
            w, pw, nw = w_ref[...], pw_ref[...], nw_ref[...]
            kb = _bands(w, pw, nw, QW, kvw, nb)
            vb = _bands(w, pw, nw, QW + kvw, kvw, nb)
            edge = _edge_mask(i * nb, nb, n_blocks)
            dq_acc = [jnp.zeros((tl, LANES), F32) for _ in range(heads // 2)]
            for h in range(heads):
                pair, a, kvp, b = _head_geometry(h, group)
                lanes = slice(pair * LANES, (pair + 1) * LANES)
                mine = _lane_half() == a
                q_al = _align(w[:, lanes], a, b).astype(BF16).reshape(nb, b_, LANES)
                do_al = _align(do_ref[:, lanes], a, b).astype(BF16).reshape(nb, b_, LANES)
                lse_h = jnp.max(jnp.where(mine, lse_ref[:, lanes], NEG_INF), axis=1, keepdims=True)
                shift = -jnp.sum(jnp.where(mine, do_ref[:, lanes] * o_ref[:, lanes], 0.0), axis=1, keepdims=True)
                if have_dlse:
                    shift = shift + jnp.sum(jnp.where(mine, dlse_ref[:, lanes], 0.0), axis=1, keepdims=True)
                logits = _bmm(q_al, kb[kvp], 2, 2) * scale + bias_ref[h][None] + edge
                p = jnp.exp(logits - lse_h.reshape(nb, b_, 1))
                dlogits = p * (_bmm(do_al, vb[kvp], 2, 2) + shift.reshape(nb, b_, 1))
                dbias_ref[h] += jnp.sum(dlogits, axis=0)
                dsink_ref[h] += jnp.sum(jnp.exp(sink_ref[h] - lse_h) * shift, axis=0, keepdims=True)
                ds = (dlogits * scale).astype(BF16)
                dq_acc[pair] = dq_acc[pair] + _unalign(_bmm(ds, kb[kvp], 2, 1).reshape(tl, LANES), a, b)
                dk_band = _bmm(ds, q_al, 1, 1)
                dv_band = _bmm(p.astype(BF16), do_al, 1, 1)
                kv_lanes = slice(kvp * LANES, (kvp + 1) * LANES)
                for t in range(3):
                    at = pl.ds(pl.multiple_of(i * tl + t * b_, b_), tl)
                    dk_s[at, kv_lanes] += dk_band[:, t * b_:(t + 1) * b_, :].reshape(tl, LANES)
                    dv_s[at, kv_lanes] += dv_band[:, t * b_:(t + 1) * b_, :].reshape(tl, LANES)
            dq_s[lax.rem(i, 2)] = jnp.concatenate(dq_acc, axis=1)

        @pl.when(i >= 1)
        def _():
            at = pl.ds(pl.multiple_of((i - 1) * tl + b_, b_), tl)
            parts = [dq_s[lax.rem(i + 1, 2)], dk_s[at, :], dv_s[at, :]]
            if QW + 2 * kvw < BAND_W:
                parts.append(jnp.zeros((tl, BAND_W - QW - 2 * kvw), F32))
            dwin_ref[...] = jnp.concatenate(parts, axis=1).astype(dwin_ref.dtype)

    main, prev, nxt, rows = _band_in_specs(tl, nb, n_chunks, n_blocks, col, True)
    row_args = [o, do, lse] + ([dlse] if have_dlse else [])
    small = [pl.BlockSpec(bias.shape, lambda j, i: (0, 0, 0)), pl.BlockSpec(sink.shape, lambda j, i: (0, 0, 0))]
    return pl.pallas_call(
        body,
        grid=(dil, n_chunks + 1),
        in_specs=[pl.BlockSpec(memory_space=pl.ANY)] * lead + [main, prev, nxt] + [rows] * len(row_args) + small,
        out_specs=[pl.BlockSpec((tl, BAND_W), lambda j, i: (j * n_chunks + jnp.maximum(i - 1, 0), col))] + small,
        out_shape=[jax.ShapeDtypeStruct(proj.shape, BF16), jax.ShapeDtypeStruct(bias.shape, F32),
                   jax.ShapeDtypeStruct(sink.shape, F32)],
        scratch_shapes=[pltpu.VMEM((2, tl, QW), F32), pltpu.VMEM((seq + 2 * b_, kvw), F32),
                        pltpu.VMEM((seq + 2 * b_, kvw), F32)],
        input_output_aliases={0: 0} if lead else {},
        name=name + "_bwd",
        compiler_params=_params("arbitrary", "arbitrary"),
    )(*([dproj] if lead else []), proj, proj, proj, *row_args, bias, sink)


def _loss_call(x, target, g):
    s, d = x.shape
    tr = _pick(s, (256, 128, 64, 32, 16, 8))

    def tile_loss(xt, gt, tt):
        err = jnp.square(_rms(xt, gt) - tt)
        return 0.5 * jnp.sum(jnp.mean(err, axis=-1, keepdims=True), axis=0, keepdims=True)

    def body(x_ref, t_ref, g_ref, loss_ref, dx_ref, dg_ref):
        tt = t_ref[...]
        val, vjp = jax.vjp(lambda xt, gt: tile_loss(xt, gt, tt), x_ref[...], g_ref[...])
        dx, dg = vjp(jnp.ones_like(val))
        dx_ref[...] = dx

        @pl.when(pl.program_id(0) == 0)
        def _():
            loss_ref[...] = jnp.zeros_like(loss_ref)
            dg_ref[...] = jnp.zeros_like(dg_ref)

        loss_ref[...] += val
        dg_ref[...] += dg

    return pl.pallas_call(
        body,
        grid=(s // tr,),
        in_specs=[_rows(tr, d), _rows(tr, d), _whole((1, d))],
        out_specs=[_whole((1, 1)), _rows(tr, d), _whole((1, d))],
        out_shape=[jax.ShapeDtypeStruct((1, 1), F32), jax.ShapeDtypeStruct((s, d), F32),
                   jax.ShapeDtypeStruct((1, d), F32)],
        name="final_norm_loss",
        compiler_params=_params("arbitrary"),
    )(x, target, g)


@jax.custom_vjp
def _loss_op(x, target, g):
    return _loss_call(x, target, g)[0][0, 0]


def _loss_op_fwd(x, target, g):
    loss, dx, dg = _loss_call(x, target, g)
    return loss[0, 0], (dx, dg, target)


def _loss_op_bwd(res, ct):
    dx, dg, target = res
    return ct * dx, jnp.zeros_like(target), ct * dg


_loss_op.defvjp(_loss_op_fwd, _loss_op_bwd)


def _mla_tile(r, p, a):
    g_q, g_kv, w_q, w_k, w_v = p
    cos_t, sin_t, place_kr = a
    a_q, a_kv, a_kr = _lanes(r[0], (0, MLA_Q_LORA, MLA_Q_LORA + MLA_KV_LORA, MLA_Q_LORA + MLA_KV_LORA + MLA_ROPE))
    q = _rope(_bdot(_rms(a_q, g_q), w_q), cos_t, sin_t, MLA_ROPE // 2)
    ckv = _rms(a_kv, g_kv)
    k = _rope(_bdot(ckv, w_k) + _hdot(a_kr, place_kr), cos_t, sin_t, MLA_ROPE // 2)
    return _split_heads(q, MLA_HEADS), _split_heads(k, MLA_HEADS), _split_heads(_bdot(ckv, w_v), MLA_HEADS)


def _head_rms(x, g, head_mean):
    return x * lax.rsqrt(_hdot(x * x, head_mean) + EPS) * g


def _gqa_tile(r, p, a):
    g_q, g_k = p
    cos_t, sin_t, mean_q, mean_k = a
    wq, wk = GQA_HEADS * HEAD_DIM, GQA_KV_HEADS * HEAD_DIM
    b_q, b_k, b_v = _lanes(r[0], (0, wq, wq + wk, wq + 2 * wk))
    q = _rope(_head_rms(b_q, g_q, mean_q), cos_t, sin_t, HEAD_DIM // 4)
    k = _rope(_head_rms(b_k, g_k, mean_k), cos_t[:, :wk], sin_t[:, :wk], HEAD_DIM // 4)
    return _split_heads(q, GQA_HEADS), _split_heads(k, GQA_KV_HEADS), _split_heads(b_v, GQA_KV_HEADS)


def _merge_tile(r, p, a):
    gm, o_a, o_b, oc0, oc1, oc2, l0, l1, l2, o_d = r
    (w_branch,) = p
    d = w_branch.shape[2]
    gate_path, merge_logits = _lanes(gm, (0, N_BRANCH * BRANCH_W, N_BRANCH * BRANCH_W + N_BRANCH * d))
    m = jnp.maximum(jnp.maximum(l0, l1), l2)
    e0, e1, e2 = jnp.exp(l0 - m), jnp.exp(l1 - m), jnp.exp(l2 - m)
    y_c = (e0 * oc0 + e1 * oc1 + e2 * oc2) / (e0 + e1 + e2)
    y = jnp.concatenate([_join_heads(o_a), _join_heads(o_b), y_c, o_d], axis=1)
    u = y * (gate_path * jax.nn.sigmoid(gate_path))
    gates = _lanes(merge_logits, tuple(range(0, N_BRANCH * d + 1, d)))
    us = _lanes(u, tuple(range(0, N_BRANCH * BRANCH_W + 1, BRANCH_W)))
    branch_w = _unstack(w_branch)
    out = None
    for nb in range(N_BRANCH):
        term = jax.nn.sigmoid(gates[nb]) * _bdot(us[nb], branch_w[nb])
        out = term if out is None else out + term
    return (out,)


def _mixer_calls(proj, prm, aux):
    s = proj.shape[0]
    tr, tm = _pick(s, (256, 128)), _pick(s, (128,))
    mla_cos, mla_sin, gqa_cos, gqa_sin, place_kr, mean_q, mean_k = aux
    wq = MLA_HEADS * MLA_QK
    mla = dict(
        steps=s // tr, rows=[(proj, _rows(tr, SMALL_W, MLA_BLK))],
        params=[prm["g_q"], prm["g_kv"], prm["w_q"], prm["w_k"], prm["w_v"]],
        aux=[(mla_cos, _rows(tr, wq)), (mla_sin, _rows(tr, wq)), (place_kr, _whole(place_kr.shape))],
        outs=[((MLA_HEADS, s, MLA_QK), _head_rows(MLA_HEADS, tr, MLA_QK))] * 2
        + [((MLA_HEADS, s, MLA_V), _head_rows(MLA_HEADS, tr, MLA_V))],
        window=((s, P_TOT), _rows(tr, SMALL_W, MLA_BLK), BF16))
    wg = GQA_HEADS * HEAD_DIM
    gqa = dict(
        steps=s // tr, rows=[(proj, _rows(tr, SMALL_W, GQA_BLK))], params=[prm["gq"], prm["gk"]],
        aux=[(gqa_cos, _rows(tr, wg)), (gqa_sin, _rows(tr, wg)), (mean_q, _whole(mean_q.shape)),
             (mean_k, _whole(mean_k.shape))],
        outs=[((GQA_HEADS, s, HEAD_DIM), _head_rows(GQA_HEADS, tr, HEAD_DIM))]
        + [((GQA_KV_HEADS, s, HEAD_DIM), _head_rows(GQA_KV_HEADS, tr, HEAD_DIM))] * 2,
        window=((s, P_TOT), _rows(tr, SMALL_W, GQA_BLK), BF16))
    merge = dict(steps=s // tm, tm=tm, window=((s, P_TOT), _rows(tm, GM_W, 0), BF16))
    return mla, gqa, merge


def _merge_rows(proj, o_a, o_b, ocs, lses, o_d, tm):
    h4 = _head_rows(4, tm, HEAD_DIM)
    return ([(proj, _rows(tm, GM_W, 0)), (o_a, h4), (o_b, h4)] + [(t, _rows(tm, QW)) for t in ocs + lses]
            + [(o_d, _rows(tm, QW))])


def _to_residues(t, dil):
    s, w = t.shape
    return t if dil == 1 else t.reshape(s // dil, dil, w).transpose(1, 0, 2).reshape(s, w)


def _from_residues(t, dil):
    s, w = t.shape
    return t if dil == 1 else t.reshape(dil, s // dil, w).transpose(1, 0, 2).reshape(s, w)


def _mixer_fwd(projs, prm, aux):
    proj = projs[0]
    s = proj.shape[0]
    mla, gqa, merge = _mixer_calls(proj, prm, aux)
    q_a, k_a, v_a = _fwd_call("prep_mla", _mla_tile, mla["steps"], mla["rows"], mla["params"], mla["aux"], mla["outs"])
    o_a, lse_a = _dense_fwd_call(q_a, k_a, v_a, MLA_QK ** -0.5, "mla")
    q_b, k_b, v_b = _fwd_call("prep_gqa", _gqa_tile, gqa["steps"], gqa["rows"], gqa["params"], gqa["aux"], gqa["outs"])
    grp = GQA_HEADS // GQA_KV_HEADS
    o_b, lse_b = _dense_fwd_call(q_b.reshape(GQA_KV_HEADS, grp * s, HEAD_DIM), k_b, v_b, HEAD_DIM ** -0.5, "gqa")
    scale = HEAD_DIM ** -0.5
    ocs, lses = [], []
    for gi, (_, dil) in enumerate(DIL_PATTERNS):
        o, lse = _band_fwd_call(projs[gi], DIL_BLK if gi == 0 else 0, prm["bias_dil"][gi], prm["no_sink"], dil, 1,
                                QW, scale, "dil%d" % gi)
        ocs.append(o)
        lses.append(lse)
    o_d, lse_d = _band_fwd_call(proj, WIN_BLK, prm["bias_win"], prm["sink"], 1, WIN_HEADS // WIN_KV_HEADS,
                                WIN_KV_HEADS * HEAD_DIM, scale, "win")
    dils = [dil for _, dil in DIL_PATTERNS]
    rows = _merge_rows(proj, o_a, o_b.reshape(GQA_HEADS, s, HEAD_DIM), [_from_residues(t, r) for t, r in zip(ocs, dils)],
                       [_from_residues(t, r) for t, r in zip(lses, dils)], o_d, merge["tm"])
    mix = _fwd_call("merge", _merge_tile, merge["steps"], rows, [prm["w_branch"]], [],
                    [((s, prm["w_branch"].shape[2]), _rows(merge["tm"], prm["w_branch"].shape[2]), BF16)])[0]
    return mix, (q_a, k_a, v_a, o_a, lse_a, q_b, k_b, v_b, o_b, lse_b, ocs, lses, o_d, lse_d)


def _mixer_bwd(projs, prm, aux, saved, dmix):
    proj = projs[0]
    s = proj.shape[0]
    q_a, k_a, v_a, o_a, lse_a, q_b, k_b, v_b, o_b, lse_b, ocs, lses, o_d, lse_d = saved
    dils = [dil for _, dil in DIL_PATTERNS]
    mla, gqa, merge = _mixer_calls(proj, prm, aux)
    tm, d_model = merge["tm"], prm["w_branch"].shape[2]
    grp = GQA_HEADS // GQA_KV_HEADS
    scale = HEAD_DIM ** -0.5

    rows = _merge_rows(proj, o_a, o_b.reshape(GQA_HEADS, s, HEAD_DIM), [_from_residues(t, r) for t, r in zip(ocs, dils)],
                       [_from_residues(t, r) for t, r in zip(lses, dils)], o_d, tm)
    grads, (dw_branch,) = _vjp_call(
        "merge", _merge_tile, merge["steps"], rows, [prm["w_branch"]], [], [(dmix, _rows(tm, d_model))],
        [merge["window"]] + [(a.shape, spec) for a, spec in rows[1:]])
    dproj, do_a, do_b, docs, dlses, do_d = grads[0], grads[1], grads[2], grads[3:6], grads[6:9], grads[9]

    dq_a, dk_a, dv_a = _dense_bwd_call(q_a, k_a, v_a, o_a, lse_a, do_a, MLA_QK ** -0.5, "mla")
    (dproj,), dmla = _vjp_call("prep_mla", _mla_tile, mla["steps"], mla["rows"], mla["params"], mla["aux"],
                               [(t, spec) for t, (_, spec) in zip((dq_a, dk_a, dv_a), mla["outs"])],
                               [mla["window"]], into=dproj)
    dq_b, dk_b, dv_b = _dense_bwd_call(q_b.reshape(GQA_KV_HEADS, grp * s, HEAD_DIM), k_b, v_b, o_b, lse_b,
                                       do_b.reshape(GQA_KV_HEADS, grp * s, HEAD_DIM), scale, "gqa")
    (dproj,), dgqa = _vjp_call("prep_gqa", _gqa_tile, gqa["steps"], gqa["rows"], gqa["params"], gqa["aux"],
                               [(t, spec) for t, (_, spec) in zip((dq_b.reshape(GQA_HEADS, s, HEAD_DIM), dk_b, dv_b),
                                                                  gqa["outs"])],
                               [gqa["window"]], into=dproj)
    dproj, dbias_win, dsink = _band_bwd_call(proj, o_d, do_d, lse_d, None, prm["bias_win"], prm["sink"], dproj,
                                             WIN_BLK, 1, WIN_HEADS // WIN_KV_HEADS, WIN_KV_HEADS * HEAD_DIM, scale, "win")
    dbias_dil, dprojs = [], []
    for gi, dil in enumerate(dils):
        dside, dbias, _ = _band_bwd_call(
            projs[gi], ocs[gi], _to_residues(docs[gi], dil), lses[gi], _to_residues(dlses[gi], dil),
            prm["bias_dil"][gi], prm["no_sink"], dproj if gi == 0 else None, DIL_BLK if gi == 0 else 0, dil, 1, QW,
            scale, "dil%d" % gi)
        if gi == 0:
            dproj = dside
        else:
            dprojs.append(dside)
        dbias_dil.append(dbias)
    dprm = dict(g_q=dmla[0], g_kv=dmla[1], w_q=dmla[2], w_k=dmla[3], w_v=dmla[4], gq=dgqa[0], gk=dgqa[1],
                bias_dil=dbias_dil, bias_win=dbias_win, sink=dsink, no_sink=jnp.zeros_like(prm["no_sink"]),
                w_branch=dw_branch)
    return [dproj] + dprojs, {k: jax.tree.map(lambda g, p: g.astype(p.dtype), v, prm[k]) for k, v in dprm.items()}


def _layer_projs(xn, w):
    dils = [dil for _, dil in DIL_PATTERNS]
    xns = [xn] + [_to_residues(xn, r) for r in dils[1:]]
    return xns, [_mm(a, b, "nn", "proj%d_nn" % i) for i, (a, b) in enumerate(zip(xns, w["w_in"]))]


def _layer_fwd(x, w, aux):
    s, d = x.shape
    tr = _pick(s, (256, 128, 64, 32, 16, 8))
    xn = _fwd_call("norm", _norm_tile, s // tr, [(x, _rows(tr, d))], [w["norm_g"]], [], [((s, d), _rows(tr, d), BF16)])[0]
    xns, projs = _layer_projs(xn, w)
    mix, saved = _mixer_fwd(projs, w["mixer"], aux)
    return _mm(mix, w["w_out"], "nn", "out_proj_nn"), (x, w, aux, xns, projs, mix, saved)


@jax.custom_vjp
def _layer_core(x, w, aux):
    return _layer_fwd(x, w, aux)[0]


def _layer_core_bwd(res, dout):
    x, w, aux, xns, projs, mix, saved = res
    s, d = x.shape
    tr = _pick(s, (256, 128, 64, 32, 16, 8))
    dils = [dil for _, dil in DIL_PATTERNS]
    dmix = _mm(dout, w["w_out"], "nt", "out_proj_nt")
    dw_out = _mm(mix, dout, "tn", "out_proj_tn", w["w_out"].dtype)
    dprojs, dmixer = _mixer_bwd(projs, w["mixer"], aux, saved, dmix)
    dxn = None
    for i, (dp, wi, r) in enumerate(zip(dprojs, w["w_in"], dils)):
        part = _from_residues(_mm(dp, wi, "nt", "proj%d_nt" % i), r)
        dxn = part if dxn is None else dxn + part
    dw_in = [_mm(a, dp, "tn", "proj%d_tn" % i, wi.dtype) for i, (a, dp, wi) in enumerate(zip(xns, dprojs, w["w_in"]))]
    (dx,), (dg,) = _vjp_call("norm", _norm_tile, s // tr, [(x, _rows(tr, d))], [w["norm_g"]], [],
                             [(dxn, _rows(tr, d))], [((s, d), _rows(tr, d))])
    dw = dict(norm_g=dg, w_in=dw_in, mixer=dmixer, w_out=dw_out)
    return dx, dw, tuple(jnp.zeros_like(t) for t in aux)


_layer_core.defvjp(lambda x, w, aux: _layer_fwd(x, w, aux), _layer_core_bwd)


def _rope_angles(pos, dim):
    inv = ROPE_THETA ** (-jnp.arange(0, dim, 2, dtype=F32) / dim)
    return pos.astype(F32)[:, None] * inv[None, :]


def _rope_tables(s):
    pos = jnp.arange(s, dtype=jnp.int32)
    rows = s // GRID_W
    row_idx = jnp.repeat(jnp.arange(rows, dtype=jnp.int32), GRID_W)
    col_idx = jnp.tile(jnp.arange(GRID_W, dtype=jnp.int32), rows)
    a1 = _rope_angles(pos, MLA_ROPE)
    ar = _rope_angles(row_idx, HEAD_DIM // 2)
    ac = _rope_angles(col_idx, HEAD_DIM // 2)
    ones, zeros = jnp.ones((s, MLA_NOPE), F32), jnp.zeros((s, MLA_NOPE), F32)
    mla_cos = jnp.tile(jnp.concatenate([ones, jnp.cos(a1), jnp.cos(a1)], axis=1), (1, MLA_HEADS))
    mla_sin = jnp.tile(jnp.concatenate([zeros, -jnp.sin(a1), jnp.sin(a1)], axis=1), (1, MLA_HEADS))
    gqa_cos = jnp.tile(jnp.concatenate([jnp.cos(ar), jnp.cos(ar), jnp.cos(ac), jnp.cos(ac)], axis=1), (1, GQA_HEADS))
    gqa_sin = jnp.tile(jnp.concatenate([-jnp.sin(ar), jnp.sin(ar), -jnp.sin(ac), jnp.sin(ac)], axis=1), (1, GQA_HEADS))
    return mla_cos, mla_sin, gqa_cos, gqa_sin


def _t5_bucket(rel):
    nb = T5_BUCKETS // 2
    max_exact = nb // 2
    n = jnp.abs(rel)
    nf = jnp.maximum(n, 1).astype(F32)
    large = max_exact + (jnp.log(nf / max_exact) / math.log(T5_MAX_DIST / max_exact) * (nb - max_exact)).astype(jnp.int32)
    large = jnp.minimum(large, nb - 1)
    return jnp.where(rel > 0, nb, 0) + jnp.where(n < max_exact, n, large)


def _band_bias(table, stride, head_lo, heads, half_window):
    b = BAND_BLOCK
    offs = jnp.arange(3 * b)[None, :] - b - jnp.arange(b)[:, None]
    one_hot = (_t5_bucket(offs * stride)[..., None] == jnp.arange(T5_BUCKETS)).astype(F32)
    bias = jnp.dot(one_hot.reshape(b * 3 * b, T5_BUCKETS), table[:, head_lo:head_lo + heads],
                   precision=lax.Precision.HIGHEST)
    bias = bias.T.reshape(heads, b, 3 * b)
    return jnp.where((jnp.abs(offs) <= half_window)[None], bias, NEG_INF)


def _w_in_layout(w_in):
    d = w_in.shape[0]
    at, pieces = 0, {}
    for name, width in (("a_q", 256), ("a_kv", 128), ("a_kr", 32), ("b_q", 256), ("b_k", 128), ("b_v", 128),
                        ("c_q", 768), ("c_k", 768), ("c_v", 768), ("d_q", 256), ("d_k", 128), ("d_v", 128),
                        ("gate", N_BRANCH * BRANCH_W), ("merge", N_BRANCH * d)):
        pieces[name] = w_in[:, at:at + width]
        at += width

    def zeros(n):
        return jnp.zeros((d, n), w_in.dtype)

    front = [pieces["gate"], pieces["merge"]]
    mla = [pieces["a_q"], pieces["a_kv"], pieces["a_kr"], zeros(SMALL_W - 416)]
    gqa = [pieces["b_q"], pieces["b_k"], pieces["b_v"]]
    win = [pieces["d_q"], pieces["d_k"], pieces["d_v"], zeros(BAND_W - 512)]
    dil = [jnp.concatenate([pieces[n][:, g * QW:(g + 1) * QW] for n in ("c_q", "c_k", "c_v")], axis=1)
           for g in range(len(DIL_PATTERNS))]
    main = jnp.concatenate(front + mla + gqa + win + dil[:1], axis=1)
    assert GM_W == MLA_BLK * SMALL_W and GM_W + 2 * SMALL_W == WIN_BLK * BAND_W and main.shape[1] == P_TOT
    return [main] + dil[1:]


def _layer(x, w, l, aux, biases):
    w_kv = w["w_mla_kv_up"][l].reshape(MLA_KV_LORA, MLA_HEADS, MLA_NOPE + MLA_V)
    w_k = jnp.concatenate([w_kv[:, :, :MLA_NOPE], jnp.zeros((MLA_KV_LORA, MLA_HEADS, MLA_ROPE), w_kv.dtype)], axis=2)
    dil_bias, win_bias = biases
    prm = dict(
        g_q=w["mla_q_norm_g"][l][None, :], g_kv=w["mla_kv_norm_g"][l][None, :], w_q=w["w_mla_q_up"][l],
        w_k=w_k.reshape(MLA_KV_LORA, MLA_HEADS * MLA_QK),
        w_v=w_kv[:, :, MLA_NOPE:].reshape(MLA_KV_LORA, MLA_HEADS * MLA_V),
        gq=jnp.tile(w["gqa_q_norm_g"][l], GQA_HEADS)[None, :], gk=jnp.tile(w["gqa_k_norm_g"][l], GQA_KV_HEADS)[None, :],
        bias_dil=list(dil_bias), bias_win=win_bias, sink=w["win_sink"][l].reshape(WIN_HEADS, 1, 1),
        no_sink=jnp.full((DIL_HEADS, 1, 1), NEG_INF, F32), w_branch=w["w_branch"][l])
    layer_w = dict(norm_g=w["norm_g"][l][None, :], w_in=_w_in_layout(w["w_in"][l]), mixer=prm, w_out=w["w_out"][l])
    return x + _layer_core(x, layer_w, aux)


def _local_loss(w, x, target):
    s, d_model = x.shape
    assert d_model == D_MODEL, "the projection's window layout is laid out for d_model 1024"
    place = np.zeros((MLA_ROPE, MLA_HEADS * MLA_QK), np.float32)
    for h in range(MLA_HEADS):
        for i in range(MLA_ROPE):
            place[i, h * MLA_QK + MLA_NOPE + i] = 1.0

    def head_mean(nh):
        m = np.kron(np.eye(nh, dtype=np.float32), np.full((HEAD_DIM, HEAD_DIM), 1.0 / HEAD_DIM, np.float32))
        return jnp.asarray(m)

    aux = _rope_tables(s) + (jnp.asarray(place), head_mean(GQA_HEADS), head_mean(GQA_KV_HEADS))
    table = w["t5_table"]
    dil_bias = [_band_bias(table, dil, gi * DIL_HEADS, DIL_HEADS, window // (2 * dil))
                for gi, (window, dil) in enumerate(DIL_PATTERNS)]
    win_bias = _band_bias(table, 1, len(DIL_PATTERNS) * DIL_HEADS, WIN_HEADS, WIN_HALF)
    for l in range(w["norm_g"].shape[0]):
        x = _layer(x, w, l, aux, (dil_bias, win_bias))
    return _loss_op(x, target, w["final_norm_g"][None, :])


_ANY = pl.BlockSpec(memory_space=pl.ANY)
_MESH = pl.DeviceIdType.MESH


def _all_gather(block, name):
    def body(x_ref, out_ref, send_sems, recv_sems, local_sem):
        x, y, c = lax.axis_index("x"), lax.axis_index("y"), lax.axis_index("c")
        me, sibling = (x, y, c), (x, y, 1 - c)
        chips = [(1 - x, y), (x, 1 - y), (1 - x, 1 - y)]

        def slot(px, py, pc):
            return out_ref.at[4 * px + 2 * py + pc]

        def copy(k, blk, to, src=None):
            return pltpu.make_async_remote_copy(
                src_ref=slot(*blk) if src is None else src, dst_ref=slot(*blk),
                send_sem=send_sems.at[k], recv_sem=recv_sems.at[k], device_id=to, device_id_type=_MESH)

        mine = pltpu.make_async_copy(x_ref, slot(*me), local_sem)
        mine.start()
        first = [copy(0, me, sibling, src=x_ref)]
        first += [copy(1 + j, me, (*chip, c), src=x_ref) for j, chip in enumerate(chips)]
        for cp in first:
            cp.start()
        passed = [copy(4 + j, (*chip, c), sibling) for j, chip in enumerate(chips)]
        for j, chip in enumerate(chips):
            copy(1 + j, (*chip, c), me).wait_recv()
            passed[j].start()
        copy(0, sibling, me).wait_recv()
        for j, chip in enumerate(chips):
            copy(4 + j, (*chip, 1 - c), me).wait_recv()
        for cp in first + passed:
            cp.wait_send()
        mine.wait()

    return pl.pallas_call(
        body,
        out_shape=jax.ShapeDtypeStruct((N_DEV,) + block.shape, block.dtype),
        in_specs=[_ANY],
        out_specs=_ANY,
        scratch_shapes=[pltpu.SemaphoreType.DMA((7,)), pltpu.SemaphoreType.DMA((7,)), pltpu.SemaphoreType.DMA],
        name=name,
    )(block)


def _all_to_all(blocks, name):
    def body(x_ref, out_ref, send_sems, recv_sems, local_sem):
        x, y, c = lax.axis_index("x"), lax.axis_index("y"), lax.axis_index("c")
        me = 4 * x + 2 * y + c
        mine = pltpu.make_async_copy(x_ref.at[me], out_ref.at[me], local_sem)
        mine.start()
        copies, landed = [], []
        for k in range(1, N_DEV):
            px = 1 - x if k & 4 else x
            py = 1 - y if k & 2 else y
            pc = 1 - c if k & 1 else c
            peer = 4 * px + 2 * py + pc
            copies.append(pltpu.make_async_remote_copy(
                src_ref=x_ref.at[peer], dst_ref=out_ref.at[me], send_sem=send_sems.at[k - 1],
                recv_sem=recv_sems.at[k - 1], device_id=(px, py, pc), device_id_type=_MESH))
            landed.append(pltpu.make_async_remote_copy(
                src_ref=x_ref.at[peer], dst_ref=out_ref.at[peer], send_sem=send_sems.at[k - 1],
                recv_sem=recv_sems.at[k - 1], device_id=(px, py, pc), device_id_type=_MESH))
        for cp in copies:
            cp.start()
        for cp in landed:
            cp.wait_recv()
        for cp in copies:
            cp.wait_send()
        mine.wait()

    return pl.pallas_call(
        body,
        out_shape=jax.ShapeDtypeStruct(blocks.shape, blocks.dtype),
        in_specs=[_ANY],
        out_specs=_ANY,
        scratch_shapes=[pltpu.SemaphoreType.DMA((7,)), pltpu.SemaphoreType.DMA((7,)), pltpu.SemaphoreType.DMA],
        name=name,
    )(blocks)


def _sum_slots(parts, name):
    _, rows, w = parts.shape
    tr = _pick(rows, (1024, 512, 256, 128, 64, 32, 16, 8))

    def body(p_ref, o_ref):
        acc = p_ref[0].astype(F32)
        for j in range(1, N_DEV):
            acc = acc + p_ref[j].astype(F32)
        o_ref[...] = acc

    return pl.pallas_call(
        body,
        grid=(rows // tr,),
        in_specs=[pl.BlockSpec((N_DEV, tr, w), lambda i: (0, i, 0))],
        out_specs=pl.BlockSpec((tr, w), lambda i: (i, 0)),
        out_shape=jax.ShapeDtypeStruct((rows, w), F32),
        name=name,
        compiler_params=_params("parallel"),
    )(parts)


def _adamw(w, g, m, v, name):
    rows, width = w.shape
    tr = _pick(rows, (1024, 512, 256, 128, 64, 32, 16, 8))

    def body(w_ref, g_ref, m_ref, v_ref, d_ref, nm_ref, nv_ref):
        g_ = g_ref[...]
        m_ = ADAM_B1 * m_ref[...] + (1.0 - ADAM_B1) * g_
        v_ = ADAM_B2 * v_ref[...] + (1.0 - ADAM_B2) * jnp.square(g_)
        m_hat = m_ / (1.0 - ADAM_B1 ** ADAM_STEP)
        v_hat = v_ / (1.0 - ADAM_B2 ** ADAM_STEP)
        d_ref[...] = -ADAM_LR * (m_hat / (jnp.sqrt(v_hat) + ADAM_EPS) + ADAM_WD * w_ref[...])
        nm_ref[...] = m_
        nv_ref[...] = v_

    spec = pl.BlockSpec((tr, width), lambda i: (i, 0))
    return pl.pallas_call(
        body,
        grid=(rows // tr,),
        in_specs=[spec] * 4,
        out_specs=[spec] * 3,
        out_shape=[jax.ShapeDtypeStruct((rows, width), F32)] * 3,
        name=name,
        compiler_params=_params("parallel"),
    )(w, g, m, v)


_SHARDED = (("w_in", 2), ("w_mla_q_up", 2), ("w_mla_kv_up", 2), ("w_branch", 3), ("w_out", 1))
_REPLICATED = ("norm_g", "mla_q_norm_g", "mla_kv_norm_g", "gqa_q_norm_g", "gqa_k_norm_g", "win_sink", "t5_table",
               "final_norm_g")


def _pack(arrays, row_multiple):
    flat = jnp.concatenate([a.reshape(-1) for a in arrays])
    rows = -(-flat.shape[0] // (LANES * row_multiple)) * row_multiple
    return jnp.pad(flat, (0, rows * LANES - flat.shape[0])).reshape(rows, LANES)


def _unpack(packed, shapes):
    flat, out, at = packed.reshape(-1), [], 0
    for shp in shapes:
        n = int(np.prod(shp))
        out.append(flat[at:at + n].reshape(shp))
        at += n
    return out


def _split_shards(full, axis):
    shp = full.shape
    t = full.reshape(shp[:axis] + (N_DEV, shp[axis] // N_DEV) + shp[axis + 1:])
    return jnp.moveaxis(t, axis, 0)


def _join_shards(stacked, axis):
    t = jnp.moveaxis(stacked, 0, axis)
    shp = t.shape
    return t.reshape(shp[:axis] + (shp[axis] * shp[axis + 1],) + shp[axis + 2:])


def kernel(x, norm_g, w_in, mla_q_norm_g, mla_kv_norm_g, w_mla_q_up, w_mla_kv_up, gqa_q_norm_g, gqa_k_norm_g, win_sink, t5_table, w_branch, w_out, final_norm_g, loss_target, m_norm_g, m_w_in, m_mla_q_norm_g, m_mla_kv_norm_g, m_w_mla_q_up, m_w_mla_kv_up, m_gqa_q_norm_g, m_gqa_k_norm_g, m_win_sink, m_t5_table, m_w_branch, m_w_out, m_final_norm_g, v_norm_g, v_w_in, v_mla_q_norm_g, v_mla_kv_norm_g, v_w_mla_q_up, v_w_mla_kv_up, v_gqa_q_norm_g, v_gqa_k_norm_g, v_win_sink, v_t5_table, v_w_branch, v_w_out, v_final_norm_g):
    given = dict(locals())
    names = ("norm_g", "w_in", "mla_q_norm_g", "mla_kv_norm_g", "w_mla_q_up", "w_mla_kv_up", "gqa_q_norm_g",
             "gqa_k_norm_g", "win_sink", "t5_table", "w_branch", "w_out", "final_norm_g")
    shard_names = [n for n, _ in _SHARDED]
    shard_shapes = [given[n].shape for n in shard_names]

    mine = _pack([given[n] for n in shard_names], 16)
    gathered = _all_gather(mine.astype(BF16), "gather_weights")
    per_dev = [_unpack(gathered[j], shard_shapes) for j in range(N_DEV)]
    weights = {n: given[n] for n in _REPLICATED}
    for wi, (n, axis) in enumerate(_SHARDED):
        weights[n] = _join_shards(jnp.stack([per_dev[j][wi] for j in range(N_DEV)]), axis)

    loss, (gw, gx) = jax.value_and_grad(_local_loss, argnums=(0, 1))(weights, x[0], loss_target[0])
    loss = lax.psum(loss, ("x", "y", "c"))

    blocks = [_split_shards(gw[n], axis) for n, axis in _SHARDED]
    send = jnp.stack([_pack([b[j] for b in blocks], 16) for j in range(N_DEV)])
    g_shard = _unpack(_sum_slots(_all_to_all(send, "scatter_grads"), "sum_grads"), shard_shapes)
    rep_shapes = [given[n].shape for n in _REPLICATED]
    g_rep = _unpack(_sum_slots(_all_gather(_pack([gw[n] for n in _REPLICATED], 8), "gather_small_grads"),
                               "sum_small_grads"), rep_shapes)
    grads = dict(zip(shard_names, g_shard))
    grads.update(zip(_REPLICATED, g_rep))

    def update(group, shapes, row_multiple, name):
        outs = _adamw(*[_pack([src[n] for n in group], row_multiple) for src in (
            given, grads, {n: given["m_" + n] for n in group}, {n: given["v_" + n] for n in group})], name)
        return [dict(zip(group, _unpack(o, shapes))) for o in outs]

    big = update(shard_names, shard_shapes, 16, "adamw_shards")
    small = update(list(_REPLICATED), rep_shapes, 8, "adamw_replicated")
    delta, new_m, new_v = [{**b, **s_} for b, s_ in zip(big, small)]
    return (loss, gx[None], *[grads[n] for n in names], *[delta[n] for n in names],
            *[new_m[n] for n in names], *[new_v[n] for n in names])
```

```python
import functools
import math

import jax
import jax.numpy as jnp
import numpy as np
from jax import lax
from jax.experimental import pallas as pl
from jax.experimental.pallas import tpu as pltpu

F32 = jnp.float32
BF16 = jnp.bfloat16
N_DEV = 8
LANES = 128
HALF = LANES // 2
V7X_VMEM_LIMIT = 56 * 1024 * 1024

EPS = 1e-6
NEG_INF = -1e30
LOG2E = 1.4426950408889634
ROPE_THETA = 10000.0
GRID_W = 64
HEAD_DIM = 64
N_BRANCH = 4
BRANCH_W = 256
MLA_HEADS, MLA_Q_LORA, MLA_KV_LORA, MLA_NOPE, MLA_ROPE, MLA_V = 4, 256, 128, 64, 32, 64
MLA_QK = MLA_NOPE + MLA_ROPE
GQA_HEADS, GQA_KV_HEADS = 4, 2
DIL_PATTERNS = ((128, 1), (512, 4), (2048, 16))
DIL_HEADS = 4
WIN_HEADS, WIN_KV_HEADS, WIN_HALF = 4, 2, 128
T5_BUCKETS, T5_MAX_DIST = 32, 1024
BAND_BLOCK = 128
ADAM_LR, ADAM_B1, ADAM_B2, ADAM_EPS, ADAM_WD, ADAM_STEP = 0.001, 0.9, 0.999, 1e-08, 0.01, 10

D_MODEL = 1024
GM_W, SMALL_W, BAND_W = 5120, 512, 768
MLA_BLK, GQA_BLK, WIN_BLK, DIL_BLK = 10, 11, 8, 9
P_TOT = 7680
QW = 256


def _params(*sem):
    return pltpu.CompilerParams(dimension_semantics=sem, vmem_limit_bytes=V7X_VMEM_LIMIT)


def _pick(n, cands):
    for c in cands:
        if n % c == 0:
            return c
    return n


def _dot(a, b, ca, cb):
    return lax.dot_general(a.astype(BF16), b.astype(BF16), (((ca,), (cb,)), ((), ())), preferred_element_type=F32)


def _bmm(a, b, ca, cb):
    return lax.dot_general(a, b, (((ca,), (cb,)), ((0,), (0,))), preferred_element_type=F32)


@jax.custom_vjp
def _bdot(a, b):
    return _dot(a, b, 1, 0)


def _bdot_fwd(a, b):
    return _dot(a, b, 1, 0), (a, b)


def _bdot_bwd(res, g):
    a, b = res
    return _dot(g, b, 1, 1), _dot(a, g, 0, 0)


_bdot.defvjp(_bdot_fwd, _bdot_bwd)


def _hdot(a, c):
    return lax.dot_general(a, c, (((1,), (0,)), ((), ())), precision=lax.Precision.HIGHEST, preferred_element_type=F32)


@functools.partial(jax.custom_vjp, nondiff_argnums=(1,))
def _lane_roll(x, shift):
    return pltpu.roll(x, shift, 1)


def _lane_roll_fwd(x, shift):
    return pltpu.roll(x, shift, 1), None


def _lane_roll_bwd(shift, _, g):
    return (pltpu.roll(g, g.shape[1] - shift, 1),)


_lane_roll.defvjp(_lane_roll_fwd, _lane_roll_bwd)


@functools.partial(jax.custom_vjp, nondiff_argnums=(1,))
def _lane_ranges(x, cut):
    bounds, _ = cut
    return tuple(x[:, lo:hi] for lo, hi in zip(bounds[:-1], bounds[1:]))


def _lane_ranges_fwd(x, cut):
    return _lane_ranges(x, cut), None


def _lane_ranges_bwd(cut, _, cts):
    bounds, width = cut
    parts = list(cts)
    if bounds[-1] < width:
        parts.append(jnp.zeros((cts[0].shape[0], width - bounds[-1]), cts[0].dtype))
    return (jnp.concatenate(parts, axis=1),)


_lane_ranges.defvjp(_lane_ranges_fwd, _lane_ranges_bwd)


def _lanes(x, bounds):
    return _lane_ranges(x, (tuple(bounds), x.shape[1]))


@jax.custom_vjp
def _unstack(x):
    return tuple(x[i] for i in range(x.shape[0]))


def _unstack_fwd(x):
    return _unstack(x), None


def _unstack_bwd(_, cts):
    return (jnp.stack(cts, axis=0),)


_unstack.defvjp(_unstack_fwd, _unstack_bwd)


@functools.partial(jax.custom_vjp, nondiff_argnums=(1,))
def _split_heads(x, h):
    d = x.shape[1] // h
    return jnp.stack([x[:, i * d:(i + 1) * d] for i in range(h)], axis=0)


def _split_heads_fwd(x, h):
    return _split_heads(x, h), None


def _split_heads_bwd(h, _, ct):
    return (jnp.concatenate([ct[i] for i in range(h)], axis=1),)


_split_heads.defvjp(_split_heads_fwd, _split_heads_bwd)


def _join_heads(x):
    return jnp.concatenate(_unstack(x), axis=1)


def _rope(x, cos_t, sin_t, half):
    w = x.shape[1]
    lane = lax.broadcasted_iota(jnp.int32, (1, w), 1)
    first = (lane % (2 * half)) < half
    partner = jnp.where(first, _lane_roll(x, w - half), _lane_roll(x, half))
    return x * cos_t + partner * sin_t


def _rms(x, g):
    return x * lax.rsqrt(jnp.mean(x * x, axis=-1, keepdims=True) + EPS) * g


def _rows(tr, w, col=0):
    return pl.BlockSpec((tr, w), lambda i: (i, col))


def _head_rows(h, tr, d):
    return pl.BlockSpec((h, tr, d), lambda i: (0, i, 0))


def _whole(shape):
    nd = len(shape)
    return pl.BlockSpec(tuple(shape), lambda i: (0,) * nd)


def _fwd_call(name, fn, steps, rows, params, aux, outs):
    nr, npar, na = len(rows), len(params), len(aux)

    def body(*refs):
        vals = [x[...].astype(F32) for x in refs[:nr + npar + na]]
        res = fn(vals[:nr], vals[nr:nr + npar], vals[nr + npar:])
        for o_ref, o in zip(refs[nr + npar + na:], res):
            o_ref[...] = o.astype(o_ref.dtype)

    return pl.pallas_call(
        body,
        grid=(steps,),
        in_specs=[s for _, s in rows] + [_whole(p.shape) for p in params] + [s for _, s in aux],
        out_specs=[e[1] for e in outs],
        out_shape=[jax.ShapeDtypeStruct(e[0], e[2] if len(e) > 2 else F32) for e in outs],
        name=name + "_fwd",
        compiler_params=_params("parallel"),
    )(*[a for a, _ in rows], *params, *[a for a, _ in aux])


def _vjp_call(name, fn, steps, rows, params, aux, cts, row_grads, into=None):
    nr, npar, na, nc = len(rows), len(params), len(aux), len(cts)
    n_in = nr + npar + na + nc
    lead = 0 if into is None else 1

    def body(*refs):
        refs = refs[lead:]
        vals = [x[...].astype(F32) for x in refs[:n_in]]
        r, p, a, d = vals[:nr], vals[nr:nr + npar], vals[nr + npar:nr + npar + na], vals[nr + npar + na:]
        out_refs = refs[n_in:]
        _, vjp = jax.vjp(lambda r_, p_: tuple(fn(r_, p_, a)), r, p)
        dr, dp = vjp(tuple(d))
        for o_ref, o in zip(out_refs[:nr], dr):
            o_ref[...] = o.astype(o_ref.dtype)

        @pl.when(pl.program_id(0) == 0)
        def _():
            for o_ref in out_refs[nr:]:
                o_ref[...] = jnp.zeros_like(o_ref)

        for o_ref, o in zip(out_refs[nr:], dp):
            o_ref[...] += o

    outs = pl.pallas_call(
        body,
        grid=(steps,),
        in_specs=([] if into is None else [pl.BlockSpec(memory_space=pl.ANY)])
        + [s for _, s in rows] + [_whole(p.shape) for p in params] + [s for _, s in aux] + [s for _, s in cts],
        out_specs=[e[1] for e in row_grads] + [_whole(p.shape) for p in params],
        out_shape=[jax.ShapeDtypeStruct(e[0], e[2] if len(e) > 2 else F32) for e in row_grads]
        + [jax.ShapeDtypeStruct(p.shape, F32) for p in params],
        input_output_aliases={} if into is None else {0: 0},
        name=name + "_bwd",
        compiler_params=_params("arbitrary"),
    )(*([] if into is None else [into]), *[a for a, _ in rows], *params, *[a for a, _ in aux], *[a for a, _ in cts])
    return list(outs[:nr]), list(outs[nr:])


def _norm_tile(r, p, a):
    return (_rms(r[0], p[0]),)


def _mm(a, b, mode, name, out_dtype=F32):
    if mode == "nn":
        (m, k), n = a.shape, b.shape[1]
    elif mode == "nt":
        (m, k), n = a.shape, b.shape[0]
    else:
        (k, m), n = a.shape, b.shape[1]
    tm = _pick(m, (1024, 512, 256, 128))
    tn = _pick(n, (1024, 768, 512, 384, 256, 128))
    tk = _pick(k, (1024, 768, 512, 384, 256, 128))
    if mode == "tn":
        tk = _pick(k, (512, 256, 128))
    nk = k // tk

    def body(a_ref, b_ref, o_ref, acc_ref):
        kk = pl.program_id(2)
        if mode == "nn":
            part = _dot(a_ref[...], b_ref[...], 1, 0)
        elif mode == "nt":
            part = _dot(a_ref[...], b_ref[...], 1, 1)
        else:
            part = _dot(a_ref[...], b_ref[...], 0, 0)
        if nk == 1:
            o_ref[...] = part.astype(o_ref.dtype)
        else:
            @pl.when(kk == 0)
            def _():
                acc_ref[...] = part

            @pl.when(kk > 0)
            def _():
                acc_ref[...] += part

            @pl.when(kk == nk - 1)
            def _():
                o_ref[...] = acc_ref[...].astype(o_ref.dtype)

    if mode == "nn":
        a_spec = pl.BlockSpec((tm, tk), lambda i, j, kk: (i, kk))
        b_spec = pl.BlockSpec((tk, tn), lambda i, j, kk: (kk, j))
    elif mode == "nt":
        a_spec = pl.BlockSpec((tm, tk), lambda i, j, kk: (i, kk))
        b_spec = pl.BlockSpec((tn, tk), lambda i, j, kk: (j, kk))
    else:
        a_spec = pl.BlockSpec((tk, tm), lambda i, j, kk: (kk, i))
        b_spec = pl.BlockSpec((tk, tn), lambda i, j, kk: (kk, j))
    return pl.pallas_call(
        body,
        grid=(m // tm, n // tn, nk),
        in_specs=[a_spec, b_spec],
        out_specs=pl.BlockSpec((tm, tn), lambda i, j, kk: (i, j)),
        out_shape=jax.ShapeDtypeStruct((m, n), out_dtype),
        scratch_shapes=[pltpu.VMEM((tm, tn), F32)],
        name=name,
        compiler_params=_params("parallel", "parallel", "arbitrary"),
    )(a, b)


def _dense_fwd_call(q, k, v, scale, name):
    n, sq, d = q.shape
    sk, dv = k.shape[1], v.shape[2]
    tq = _pick(sq, (256, 128))
    c = scale * LOG2E

    def body(q_ref, k_ref, v_ref, o_ref, lse_ref, k_s, vext_s):
        @pl.when(pl.program_id(1) == 0)
        def _():
            k_s[...] = k_ref[0].astype(BF16)
            vext_s[...] = jnp.ones_like(vext_s)
            vext_s[:, :dv] = v_ref[0].astype(BF16)

        s = _dot(q_ref[0], k_s[...], 1, 1)
        m = jnp.max(s, axis=1, keepdims=True)
        p = jnp.exp2(s * c - m * c)
        acc = _dot(p, vext_s[...], 1, 0)
        l = acc[:, dv:dv + 1]
        o_ref[0] = acc[:, :dv] / l
        lse_ref[0] = m * scale + jnp.log(l)

    return pl.pallas_call(
        body,
        grid=(n, sq // tq),
        in_specs=[
            pl.BlockSpec((1, tq, d), lambda h, i: (h, i, 0)),
            pl.BlockSpec((1, sk, d), lambda h, i: (h, 0, 0)),
            pl.BlockSpec((1, sk, dv), lambda h, i: (h, 0, 0)),
        ],
        out_specs=[
            pl.BlockSpec((1, tq, dv), lambda h, i: (h, i, 0)),
            pl.BlockSpec((1, tq, 1), lambda h, i: (h, i, 0)),
        ],
        out_shape=[jax.ShapeDtypeStruct((n, sq, dv), F32), jax.ShapeDtypeStruct((n, sq, 1), F32)],
        scratch_shapes=[pltpu.VMEM((sk, d), BF16), pltpu.VMEM((sk, 2 * dv), BF16)],
        name=name + "_fwd",
        compiler_params=_params("arbitrary", "arbitrary"),
    )(q, k, v)


def _dense_bwd_call(q, k, v, o, lse, do, scale, name):
    n, sq, d = q.shape
    sk, dv = k.shape[1], v.shape[2]
    tq, tk = _pick(sq, (512, 256, 128)), _pick(sk, (2048, 1024, 512, 256, 128))
    c = scale * LOG2E

    def body(q_ref, k_ref, v_ref, o_ref, lse_ref, do_ref, dq_ref, dk_ref, dv_ref):
        j, i = pl.program_id(1), pl.program_id(2)
        qb, kb, vb = q_ref[0].astype(BF16), k_ref[0].astype(BF16), v_ref[0].astype(BF16)
        do_f = do_ref[0]
        dob = do_f.astype(BF16)
        p = jnp.exp2(_dot(qb, kb, 1, 1) * c - lse_ref[0] * LOG2E)
        delta = jnp.sum(do_f * o_ref[0], axis=1, keepdims=True)
        ds = (p * (_dot(dob, vb, 1, 1) - delta)).astype(BF16)
        dv_part = _dot(p, dob, 0, 0)
        dk_part = _dot(ds, qb, 0, 0) * scale
        dq_part = _dot(ds, kb, 1, 0) * scale
        rows = pl.ds(pl.multiple_of(i * tq, tq), tq)

        @pl.when(i == 0)
        def _():
            dk_ref[0] = dk_part
            dv_ref[0] = dv_part

        @pl.when(i > 0)
        def _():
            dk_ref[0] += dk_part
            dv_ref[0] += dv_part

        @pl.when(j == 0)
        def _():
            dq_ref[0, rows, :] = dq_part

        @pl.when(j > 0)
        def _():
            dq_ref[0, rows, :] += dq_part

    return pl.pallas_call(
        body,
        grid=(n, sk // tk, sq // tq),
        in_specs=[
            pl.BlockSpec((1, tq, d), lambda h, j, i: (h, i, 0)),
            pl.BlockSpec((1, tk, d), lambda h, j, i: (h, j, 0)),
            pl.BlockSpec((1, tk, dv), lambda h, j, i: (h, j, 0)),
            pl.BlockSpec((1, tq, dv), lambda h, j, i: (h, i, 0)),
            pl.BlockSpec((1, tq, 1), lambda h, j, i: (h, i, 0)),
            pl.BlockSpec((1, tq, dv), lambda h, j, i: (h, i, 0)),
        ],
        out_specs=[
            pl.BlockSpec((1, sq, d), lambda h, j, i: (h, 0, 0)),
            pl.BlockSpec((1, tk, d), lambda h, j, i: (h, j, 0)),
            pl.BlockSpec((1, tk, dv), lambda h, j, i: (h, j, 0)),
        ],
        out_shape=[
            jax.ShapeDtypeStruct((n, sq, d), F32),
            jax.ShapeDtypeStruct((n, sk, d), F32),
            jax.ShapeDtypeStruct((n, sk, dv), F32),
        ],
        name=name + "_bwd",
        compiler_params=_params("arbitrary", "arbitrary", "arbitrary"),
    )(q, k, v, o, lse, do)


def _head_geometry(h, group):
    pair, a = divmod(h, 2)
    kv_pair, b = divmod(h // group, 2)
    return pair, a, kv_pair, b


def _lane_half():
    return lax.broadcasted_iota(jnp.int32, (1, LANES), 1) // HALF


def _align(x, a, b):
    if a != b:
        x = pltpu.roll(x, HALF, 1)
    return jnp.where(_lane_half() == b, x, 0.0)


def _unalign(x, a, b):
    x = jnp.where(_lane_half() == b, x, 0.0)
    return pltpu.roll(x, HALF, 1) if a != b else x


def _bands(w, pw, nw, lo, kvw, nb):
    b = BAND_BLOCK
    cat = jnp.concatenate([pw[:, lo:lo + kvw], w[:, lo:lo + kvw], nw[:, lo:lo + kvw]], axis=0).astype(BF16)
    out = []
    for g in range(kvw // LANES):
        c3 = cat[:, g * LANES:(g + 1) * LANES].reshape(nb + 2, b, LANES)
        out.append(jnp.concatenate([c3[0:nb], c3[1:nb + 1], c3[2:nb + 2]], axis=1))
    return out


def _edge_mask(first_block, nb, n_blocks):
    b = BAND_BLOCK
    blk = first_block + lax.broadcasted_iota(jnp.int32, (nb, 1, 3 * b), 0)
    col = lax.broadcasted_iota(jnp.int32, (nb, 1, 3 * b), 2)
    outside = ((col < b) & (blk == 0)) | ((col >= 2 * b) & (blk == n_blocks - 1))
    return jnp.where(outside, NEG_INF, 0.0)


def _band_geometry(proj, dil):
    seq = proj.shape[0] // dil
    tl = _pick(seq, (1024, 512, 256, 128))
    return seq, tl, tl // BAND_BLOCK, seq // tl


def _band_in_specs(tl, nb, n_chunks, n_blocks, col, last_step_idle):
    def chunk(i):
        return jnp.minimum(i, n_chunks - 1) if last_step_idle else i

    main = pl.BlockSpec((tl, BAND_W), lambda j, i: (j * n_chunks + chunk(i), col))
    prev = pl.BlockSpec((BAND_BLOCK, BAND_W),
                        lambda j, i: (j * n_blocks + jnp.maximum(chunk(i) * nb - 1, 0), col))
    nxt = pl.BlockSpec((BAND_BLOCK, BAND_W),
                       lambda j, i: (j * n_blocks + jnp.minimum((chunk(i) + 1) * nb, n_blocks - 1), col))
    rows = pl.BlockSpec((tl, QW), lambda j, i: (j * n_chunks + chunk(i), 0))
    return main, prev, nxt, rows


def _band_fwd_call(proj, col, bias, sink, dil, group, kvw, scale, name):
    s_tok = proj.shape[0]
    seq, tl, nb, n_chunks = _band_geometry(proj, dil)
    n_blocks = seq // BAND_BLOCK
    heads = bias.shape[0]

    def body(w_ref, pw_ref, nw_ref, bias_ref, sink_ref, o_ref, lse_ref):
        i = pl.program_id(1)
        w, pw, nw = w_ref[...], pw_ref[...], nw_ref[...]
        kb = _bands(w, pw, nw, QW, kvw, nb)
        vb = _bands(w, pw, nw, QW + kvw, kvw, nb)
        edge = _edge_mask(i * nb, nb, n_blocks)
        o_acc = [jnp.zeros((tl, LANES), F32) for _ in range(heads // 2)]
        lse_acc = [jnp.zeros((tl, LANES), F32) for _ in range(heads // 2)]
        for h in range(heads):
            pair, a, kvp, b = _head_geometry(h, group)
            q_al = _align(w[:, pair * LANES:(pair + 1) * LANES], a, b).astype(BF16).reshape(nb, BAND_BLOCK, LANES)
            logits = _bmm(q_al, kb[kvp], 2, 2) * scale + bias_ref[h][None] + edge
            sk = sink_ref[h].reshape(1, 1, 1)
            m = jnp.maximum(jnp.max(logits, axis=2, keepdims=True), sk)
            e = jnp.exp(logits - m)
            ssum = jnp.sum(e, axis=2, keepdims=True) + jnp.exp(sk - m)
            out = _bmm(e.astype(BF16), vb[kvp], 2, 1) / ssum
            o_acc[pair] = o_acc[pair] + _unalign(out.reshape(tl, LANES), a, b)
            lse = (m + jnp.log(ssum)).reshape(tl, 1)
            lse_acc[pair] = lse_acc[pair] + jnp.where(_lane_half() == a, lse, 0.0)
        o_ref[...] = jnp.concatenate(o_acc, axis=1)
        lse_ref[...] = jnp.concatenate(lse_acc, axis=1)

    main, prev, nxt, rows = _band_in_specs(tl, nb, n_chunks, n_blocks, col, False)
    return pl.pallas_call(
        body,
        grid=(dil, n_chunks),
        in_specs=[main, prev, nxt, pl.BlockSpec(bias.shape, lambda j, i: (0, 0, 0)),
                  pl.BlockSpec(sink.shape, lambda j, i: (0, 0, 0))],
        out_specs=[rows, rows],
        out_shape=[jax.ShapeDtypeStruct((s_tok, QW), F32)] * 2,
        name=name + "_fwd",
        compiler_params=_params("parallel", "parallel"),
    )(proj, proj, proj, bias, sink)


def _band_bwd_call(proj, o, do, lse, dlse, bias, sink, dproj, col, dil, group, kvw, scale, name):
    seq, tl, nb, n_chunks = _band_geometry(proj, dil)
    lead = 0 if dproj is None else 1
    n_blocks = seq // BAND_BLOCK
    heads = bias.shape[0]
    b_ = BAND_BLOCK
    have_dlse = dlse is not None

    def body(*refs):
        (w_ref, pw_ref, nw_ref, o_ref, do_ref, lse_ref), refs = refs[lead:lead + 6], refs[lead + 6:]
        if have_dlse:
            dlse_ref, refs = refs[0], refs[1:]
        bias_ref, sink_ref, dwin_ref, dbias_ref, dsink_ref, dq_s, dk_s, dv_s = refs
        j, i = pl.program_id(0), pl.program_id(1)

        @pl.when((j == 0) & (i == 0))
        def _():
            dbias_ref[...] = jnp.zeros_like(dbias_ref)
            dsink_ref[...] = jnp.zeros_like(dsink_ref)

        @pl.when(i == 0)
        def _():
            dk_s[...] = jnp.zeros_like(dk_s)
            dv_s[...] = jnp.zeros_like(dv_s)

        @pl.when(i < n_chunks)
        def _():
            w, pw, nw = w_ref[...], pw_ref[...], nw_ref[...]
            kb = _bands(w, pw, nw, QW, kvw, nb)
            vb = _bands(w, pw, nw, QW + kvw, kvw, nb)
            edge = _edge_mask(i * nb, nb, n_blocks)
            dq_acc = [jnp.zeros((tl, LANES), F32) for _ in range(heads // 2)]
            for h in range(heads):
                pair, a, kvp, b = _head_geometry(h, group)
                lanes = slice(pair * LANES, (pair + 1) * LANES)
                mine = _lane_half() == a
                q_al = _align(w[:, lanes], a, b).astype(BF16).reshape(nb, b_, LANES)
                do_al = _align(do_ref[:, lanes], a, b).astype(BF16).reshape(nb, b_, LANES)
                lse_h = jnp.max(jnp.where(mine, lse_ref[:, lanes], NEG_INF), axis=1, keepdims=True)
                shift = -jnp.sum(jnp.where(mine, do_ref[:, lanes] * o_ref[:, lanes], 0.0), axis=1, keepdims=True)
                if have_dlse:
                    shift = shift + jnp.sum(jnp.where(mine, dlse_ref[:, lanes], 0.0), axis=1, keepdims=True)
                logits = _bmm(q_al, kb[kvp], 2, 2) * scale + bias_ref[h][None] + edge
                p = jnp.exp(logits - lse_h.reshape(nb, b_, 1))
                dlogits = p * (_bmm(do_al, vb[kvp], 2, 2) + shift.reshape(nb, b_, 1))
                dbias_ref[h] += jnp.sum(dlogits, axis=0)
                dsink_ref[h] += jnp.sum(jnp.exp(sink_ref[h] - lse_h) * shift, axis=0, keepdims=True)
                ds = (dlogits * scale).astype(BF16)
                dq_acc[pair] = dq_acc[pair] + _unalign(_bmm(ds, kb[kvp], 2, 1).reshape(tl, LANES), a, b)
                dk_band = _bmm(ds, q_al, 1, 1)
                dv_band = _bmm(p.astype(BF16), do_al, 1, 1)
                kv_lanes = slice(kvp * LANES, (kvp + 1) * LANES)
                for t in range(3):
                    at = pl.ds(pl.multiple_of(i * tl + t * b_, b_), tl)
                    dk_s[at, kv_lanes] += dk_band[:, t * b_:(t + 1) * b_, :].reshape(tl, LANES)
                    dv_s[at, kv_lanes] += dv_band[:, t * b_:(t + 1) * b_, :].reshape(tl, LANES)
            dq_s[lax.rem(i, 2)] = jnp.concatenate(dq_acc, axis=1)

        @pl.when(i >= 1)
        def _():
            at = pl.ds(pl.multiple_of((i - 1) * tl + b_, b_), tl)
            parts = [dq_s[lax.rem(i + 1, 2)], dk_s[at, :], dv_s[at, :]]
            if QW + 2 * kvw < BAND_W:
                parts.append(jnp.zeros((tl, BAND_W - QW - 2 * kvw), F32))
            dwin_ref[...] = jnp.concatenate(parts, axis=1).astype(dwin_ref.dtype)

    main, prev, nxt, rows = _band_in_specs(tl, nb, n_chunks, n_blocks, col, True)
    row_args = [o, do, lse] + ([dlse] if have_dlse else [])
    small = [pl.BlockSpec(bias.shape, lambda j, i: (0, 0, 0)), pl.BlockSpec(sink.shape, lambda j, i: (0, 0, 0))]
    return pl.pallas_call(
        body,
        grid=(dil, n_chunks + 1),
        in_specs=[pl.BlockSpec(memory_space=pl.ANY)] * lead + [main, prev, nxt] + [rows] * len(row_args) + small,
        out_specs=[pl.BlockSpec((tl, BAND_W), lambda j, i: (j * n_chunks + jnp.maximum(i - 1, 0), col))] + small,
        out_shape=[jax.ShapeDtypeStruct(proj.shape, BF16), jax.ShapeDtypeStruct(bias.shape, F32),
                   jax.ShapeDtypeStruct(sink.shape, F32)],
        scratch_shapes=[pltpu.VMEM((2, tl, QW), F32), pltpu.VMEM((seq + 2 * b_, kvw), F32),
                        pltpu.VMEM((seq + 2 * b_, kvw), F32)],
        input_output_aliases={0: 0} if lead else {},
        name=name + "_bwd",
        compiler_params=_params("arbitrary", "arbitrary"),
    )(*([dproj] if lead else []), proj, proj, proj, *row_args, bias, sink)


def _loss_call(x, target, g):
    s, d = x.shape
    tr = _pick(s, (256, 128, 64, 32, 16, 8))

    def tile_loss(xt, gt, tt):
        err = jnp.square(_rms(xt, gt) - tt)
        return 0.5 * jnp.sum(jnp.mean(err, axis=-1, keepdims=True), axis=0, keepdims=True)

    def body(x_ref, t_ref, g_ref, loss_ref, dx_ref, dg_ref):
        tt = t_ref[...]
        val, vjp = jax.vjp(lambda xt, gt: tile_loss(xt, gt, tt), x_ref[...], g_ref[...])
        dx, dg = vjp(jnp.ones_like(val))
        dx_ref[...] = dx

        @pl.when(pl.program_id(0) == 0)
        def _():
            loss_ref[...] = jnp.zeros_like(loss_ref)
            dg_ref[...] = jnp.zeros_like(dg_ref)

        loss_ref[...] += val
        dg_ref[...] += dg

    return pl.pallas_call(
        body,
        grid=(s // tr,),
        in_specs=[_rows(tr, d), _rows(tr, d), _whole((1, d))],
        out_specs=[_whole((1, 1)), _rows(tr, d), _whole((1, d))],
        out_shape=[jax.ShapeDtypeStruct((1, 1), F32), jax.ShapeDtypeStruct((s, d), F32),
                   jax.ShapeDtypeStruct((1, d), F32)],
        name="final_norm_loss",
        compiler_params=_params("arbitrary"),
    )(x, target, g)


@jax.custom_vjp
def _loss_op(x, target, g):
    return _loss_call(x, target, g)[0][0, 0]


def _loss_op_fwd(x, target, g):
    loss, dx, dg = _loss_call(x, target, g)
    return loss[0, 0], (dx, dg, target)


def _loss_op_bwd(res, ct):
    dx, dg, target = res
    return ct * dx, jnp.zeros_like(target), ct * dg


_loss_op.defvjp(_loss_op_fwd, _loss_op_bwd)


def _mla_tile(r, p, a):
    g_q, g_kv, w_q, w_k, w_v = p
    cos_t, sin_t, place_kr = a
    a_q, a_kv, a_kr = _lanes(r[0], (0, MLA_Q_LORA, MLA_Q_LORA + MLA_KV_LORA, MLA_Q_LORA + MLA_KV_LORA + MLA_ROPE))
    q = _rope(_bdot(_rms(a_q, g_q), w_q), cos_t, sin_t, MLA_ROPE // 2)
    ckv = _rms(a_kv, g_kv)
    k = _rope(_bdot(ckv, w_k) + _hdot(a_kr, place_kr), cos_t, sin_t, MLA_ROPE // 2)
    return _split_heads(q, MLA_HEADS), _split_heads(k, MLA_HEADS), _split_heads(_bdot(ckv, w_v), MLA_HEADS)


def _head_rms(x, g, head_mean):
    return x * lax.rsqrt(_hdot(x * x, head_mean) + EPS) * g


def _gqa_tile(r, p, a):
    g_q, g_k = p
    cos_t, sin_t, mean_q, mean_k = a
    wq, wk = GQA_HEADS * HEAD_DIM, GQA_KV_HEADS * HEAD_DIM
    b_q, b_k, b_v = _lanes(r[0], (0, wq, wq + wk, wq + 2 * wk))
    q = _rope(_head_rms(b_q, g_q, mean_q), cos_t, sin_t, HEAD_DIM // 4)
    k = _rope(_head_rms(b_k, g_k, mean_k), cos_t[:, :wk], sin_t[:, :wk], HEAD_DIM // 4)
    return _split_heads(q, GQA_HEADS), _split_heads(k, GQA_KV_HEADS), _split_heads(b_v, GQA_KV_HEADS)


def _merge_tile(r, p, a):
    gm, o_a, o_b, oc0, oc1, oc2, l0, l1, l2, o_d = r
    (w_branch,) = p
    d = w_branch.shape[2]
    gate_path, merge_logits = _lanes(gm, (0, N_BRANCH * BRANCH_W, N_BRANCH * BRANCH_W + N_BRANCH * d))
    m = jnp.maximum(jnp.maximum(l0, l1), l2)
    e0, e1, e2 = jnp.exp(l0 - m), jnp.exp(l1 - m), jnp.exp(l2 - m)
    y_c = (e0 * oc0 + e1 * oc1 + e2 * oc2) / (e0 + e1 + e2)
    y = jnp.concatenate([_join_heads(o_a), _join_heads(o_b), y_c, o_d], axis=1)
    u = y * (gate_path * jax.nn.sigmoid(gate_path))
    gates = _lanes(merge_logits, tuple(range(0, N_BRANCH * d + 1, d)))
    us = _lanes(u, tuple(range(0, N_BRANCH * BRANCH_W + 1, BRANCH_W)))
    branch_w = _unstack(w_branch)
    out = None
    for nb in range(N_BRANCH):
        term = jax.nn.sigmoid(gates[nb]) * _bdot(us[nb], branch_w[nb])
        out = term if out is None else out + term
    return (out,)


def _mixer_calls(proj, prm, aux):
    s = proj.shape[0]
    tr, tm = _pick(s, (256, 128)), _pick(s, (128,))
    mla_cos, mla_sin, gqa_cos, gqa_sin, place_kr, mean_q, mean_k = aux
    wq = MLA_HEADS * MLA_QK
    mla = dict(
        steps=s // tr, rows=[(proj, _rows(tr, SMALL_W, MLA_BLK))],
        params=[prm["g_q"], prm["g_kv"], prm["w_q"], prm["w_k"], prm["w_v"]],
        aux=[(mla_cos, _rows(tr, wq)), (mla_sin, _rows(tr, wq)), (place_kr, _whole(place_kr.shape))],
        outs=[((MLA_HEADS, s, MLA_QK), _head_rows(MLA_HEADS, tr, MLA_QK))] * 2
        + [((MLA_HEADS, s, MLA_V), _head_rows(MLA_HEADS, tr, MLA_V))],
        window=((s, P_TOT), _rows(tr, SMALL_W, MLA_BLK), BF16))
    wg = GQA_HEADS * HEAD_DIM
    gqa = dict(
        steps=s // tr, rows=[(proj, _rows(tr, SMALL_W, GQA_BLK))], params=[prm["gq"], prm["gk"]],
        aux=[(gqa_cos, _rows(tr, wg)), (gqa_sin, _rows(tr, wg)), (mean_q, _whole(mean_q.shape)),
             (mean_k, _whole(mean_k.shape))],
        outs=[((GQA_HEADS, s, HEAD_DIM), _head_rows(GQA_HEADS, tr, HEAD_DIM))]
        + [((GQA_KV_HEADS, s, HEAD_DIM), _head_rows(GQA_KV_HEADS, tr, HEAD_DIM))] * 2,
        window=((s, P_TOT), _rows(tr, SMALL_W, GQA_BLK), BF16))
    merge = dict(steps=s // tm, tm=tm, window=((s, P_TOT), _rows(tm, GM_W, 0), BF16))
    return mla, gqa, merge


def _merge_rows(proj, o_a, o_b, ocs, lses, o_d, tm):
    h4 = _head_rows(4, tm, HEAD_DIM)
    return ([(proj, _rows(tm, GM_W, 0)), (o_a, h4), (o_b, h4)] + [(t, _rows(tm, QW)) for t in ocs + lses]
            + [(o_d, _rows(tm, QW))])


def _to_residues(t, dil):
    s, w = t.shape
    return t if dil == 1 else t.reshape(s // dil, dil, w).transpose(1, 0, 2).reshape(s, w)


def _from_residues(t, dil):
    s, w = t.shape
    return t if dil == 1 else t.reshape(dil, s // dil, w).transpose(1, 0, 2).reshape(s, w)


def _mixer_fwd(projs, prm, aux):
    proj = projs[0]
    s = proj.shape[0]
    mla, gqa, merge = _mixer_calls(proj, prm, aux)
    q_a, k_a, v_a = _fwd_call("prep_mla", _mla_tile, mla["steps"], mla["rows"], mla["params"], mla["aux"], mla["outs"])
    o_a, lse_a = _dense_fwd_call(q_a, k_a, v_a, MLA_QK ** -0.5, "mla")
    q_b, k_b, v_b = _fwd_call("prep_gqa", _gqa_tile, gqa["steps"], gqa["rows"], gqa["params"], gqa["aux"], gqa["outs"])
    grp = GQA_HEADS // GQA_KV_HEADS
    o_b, lse_b = _dense_fwd_call(q_b.reshape(GQA_KV_HEADS, grp * s, HEAD_DIM), k_b, v_b, HEAD_DIM ** -0.5, "gqa")
    scale = HEAD_DIM ** -0.5
    ocs, lses = [], []
    for gi, (_, dil) in enumerate(DIL_PATTERNS):
        o, lse = _band_fwd_call(projs[gi], DIL_BLK if gi == 0 else 0, prm["bias_dil"][gi], prm["no_sink"], dil, 1,
                                QW, scale, "dil%d" % gi)
        ocs.append(o)
        lses.append(lse)
    o_d, lse_d = _band_fwd_call(proj, WIN_BLK, prm["bias_win"], prm["sink"], 1, WIN_HEADS // WIN_KV_HEADS,
                                WIN_KV_HEADS * HEAD_DIM, scale, "win")
    dils = [dil for _, dil in DIL_PATTERNS]
    rows = _merge_rows(proj, o_a, o_b.reshape(GQA_HEADS, s, HEAD_DIM), [_from_residues(t, r) for t, r in zip(ocs, dils)],
                       [_from_residues(t, r) for t, r in zip(lses, dils)], o_d, merge["tm"])
    mix = _fwd_call("merge", _merge_tile, merge["steps"], rows, [prm["w_branch"]], [],
                    [((s, prm["w_branch"].shape[2]), _rows(merge["tm"], prm["w_branch"].shape[2]), BF16)])[0]
    return mix, (q_a, k_a, v_a, o_a, lse_a, q_b, k_b, v_b, o_b, lse_b, ocs, lses, o_d, lse_d)


def _mixer_bwd(projs, prm, aux, saved, dmix):
    proj = projs[0]
    s = proj.shape[0]
    q_a, k_a, v_a, o_a, lse_a, q_b, k_b, v_b, o_b, lse_b, ocs, lses, o_d, lse_d = saved
    dils = [dil for _, dil in DIL_PATTERNS]
    mla, gqa, merge = _mixer_calls(proj, prm, aux)
    tm, d_model = merge["tm"], prm["w_branch"].shape[2]
    grp = GQA_HEADS // GQA_KV_HEADS
    scale = HEAD_DIM ** -0.5

    rows = _merge_rows(proj, o_a, o_b.reshape(GQA_HEADS, s, HEAD_DIM), [_from_residues(t, r) for t, r in zip(ocs, dils)],
                       [_from_residues(t, r) for t, r in zip(lses, dils)], o_d, tm)
    grads, (dw_branch,) = _vjp_call(
        "merge", _merge_tile, merge["steps"], rows, [prm["w_branch"]], [], [(dmix, _rows(tm, d_model))],
        [merge["window"]] + [(a.shape, spec) for a, spec in rows[1:]])
    dproj, do_a, do_b, docs, dlses, do_d = grads[0], grads[1], grads[2], grads[3:6], grads[6:9], grads[9]

    dq_a, dk_a, dv_a = _dense_bwd_call(q_a, k_a, v_a, o_a, lse_a, do_a, MLA_QK ** -0.5, "mla")
    (dproj,), dmla = _vjp_call("prep_mla", _mla_tile, mla["steps"], mla["rows"], mla["params"], mla["aux"],
                               [(t, spec) for t, (_, spec) in zip((dq_a, dk_a, dv_a), mla["outs"])],
                               [mla["window"]], into=dproj)
    dq_b, dk_b, dv_b = _dense_bwd_call(q_b.reshape(GQA_KV_HEADS, grp * s, HEAD_DIM), k_b, v_b, o_b, lse_b,
                                       do_b.reshape(GQA_KV_HEADS, grp * s, HEAD_DIM), scale, "gqa")
    (dproj,), dgqa = _vjp_call("prep_gqa", _gqa_tile, gqa["steps"], gqa["rows"], gqa["params"], gqa["aux"],
                               [(t, spec) for t, (_, spec) in zip((dq_b.reshape(GQA_HEADS, s, HEAD_DIM), dk_b, dv_b),
                                                                  gqa["outs"])],
                               [gqa["window"]], into=dproj)
    dproj, dbias_win, dsink = _band_bwd_call(proj, o_d, do_d, lse_d, None, prm["bias_win"], prm["sink"], dproj,
                                             WIN_BLK, 1, WIN_HEADS // WIN_KV_HEADS, WIN_KV_HEADS * HEAD_DIM, scale, "win")
    dbias_dil, dprojs = [], []
    for gi, dil in enumerate(dils):
        dside, dbias, _ = _band_bwd_call(
            projs[gi], ocs[gi], _to_residues(docs[gi], dil), lses[gi], _to_residues(dlses[gi], dil),
            prm["bias_dil"][gi], prm["no_sink"], dproj if gi == 0 else None, DIL_BLK if gi == 0 else 0, dil, 1, QW,
            scale, "dil%d" % gi)
        if gi == 0:
            dproj = dside
        else:
            dprojs.append(dside)
        dbias_dil.append(dbias)
    dprm = dict(g_q=dmla[0], g_kv=dmla[1], w_q=dmla[2], w_k=dmla[3], w_v=dmla[4], gq=dgqa[0], gk=dgqa[1],
                bias_dil=dbias_dil, bias_win=dbias_win, sink=dsink, no_sink=jnp.zeros_like(prm["no_sink"]),
                w_branch=dw_branch)
    return [dproj] + dprojs, {k: jax.tree.map(lambda g, p: g.astype(p.dtype), v, prm[k]) for k, v in dprm.items()}


def _layer_projs(xn, w):
    dils = [dil for _, dil in DIL_PATTERNS]
    xns = [xn] + [_to_residues(xn, r) for r in dils[1:]]
    return xns, [_mm(a, b, "nt", "proj%d_fwd" % i) for i, (a, b) in enumerate(zip(xns, w["w_in_t"]))]


def _layer_fwd(x, w, aux):
    s, d = x.shape
    tr = _pick(s, (256, 128, 64, 32, 16, 8))
    xn = _fwd_call("norm", _norm_tile, s // tr, [(x, _rows(tr, d))], [w["norm_g"]], [], [((s, d), _rows(tr, d), BF16)])[0]
    xns, projs = _layer_projs(xn, w)
    mix, saved = _mixer_fwd(projs, w["mixer"], aux)
    return _mm(mix, w["w_out"], "nn", "out_proj_nn"), (x, w, aux, xns, projs, mix, saved)


@jax.custom_vjp
def _layer_core(x, w, aux):
    return _layer_fwd(x, w, aux)[0]


def _layer_core_bwd(res, dout):
    x, w, aux, xns, projs, mix, saved = res
    s, d = x.shape
    tr = _pick(s, (256, 128, 64, 32, 16, 8))
    dils = [dil for _, dil in DIL_PATTERNS]
    dmix = _mm(dout, w["w_out"], "nt", "out_proj_nt")
    dw_out = _mm(mix, dout, "tn", "out_proj_tn", w["w_out"].dtype)
    dprojs, dmixer = _mixer_bwd(projs, w["mixer"], aux, saved, dmix)
    dxn = None
    for i, (dp, wi, r) in enumerate(zip(dprojs, w["w_in_t"], dils)):
        part = _from_residues(_mm(dp, wi, "nn", "proj%d_dx" % i), r)
        dxn = part if dxn is None else dxn + part
    dw_in_t = [_mm(dp, a, "tn", "proj%d_dw" % i, wi.dtype) for i, (a, dp, wi) in enumerate(zip(xns, dprojs, w["w_in_t"]))]
    (dx,), (dg,) = _vjp_call("norm", _norm_tile, s // tr, [(x, _rows(tr, d))], [w["norm_g"]], [],
                             [(dxn, _rows(tr, d))], [((s, d), _rows(tr, d))])
    dw = dict(norm_g=dg, w_in_t=dw_in_t, mixer=dmixer, w_out=dw_out)
    return dx, dw, tuple(jnp.zeros_like(t) for t in aux)


_layer_core.defvjp(lambda x, w, aux: _layer_fwd(x, w, aux), _layer_core_bwd)


def _rope_angles(pos, dim):
    inv = ROPE_THETA ** (-jnp.arange(0, dim, 2, dtype=F32) / dim)
    return pos.astype(F32)[:, None] * inv[None, :]


def _rope_tables(s):
    pos = jnp.arange(s, dtype=jnp.int32)
    rows = s // GRID_W
    row_idx = jnp.repeat(jnp.arange(rows, dtype=jnp.int32), GRID_W)
    col_idx = jnp.tile(jnp.arange(GRID_W, dtype=jnp.int32), rows)
    a1 = _rope_angles(pos, MLA_ROPE)
    ar = _rope_angles(row_idx, HEAD_DIM // 2)
    ac = _rope_angles(col_idx, HEAD_DIM // 2)
    ones, zeros = jnp.ones((s, MLA_NOPE), F32), jnp.zeros((s, MLA_NOPE), F32)
    mla_cos = jnp.tile(jnp.concatenate([ones, jnp.cos(a1), jnp.cos(a1)], axis=1), (1, MLA_HEADS))
    mla_sin = jnp.tile(jnp.concatenate([zeros, -jnp.sin(a1), jnp.sin(a1)], axis=1), (1, MLA_HEADS))
    gqa_cos = jnp.tile(jnp.concatenate([jnp.cos(ar), jnp.cos(ar), jnp.cos(ac), jnp.cos(ac)], axis=1), (1, GQA_HEADS))
    gqa_sin = jnp.tile(jnp.concatenate([-jnp.sin(ar), jnp.sin(ar), -jnp.sin(ac), jnp.sin(ac)], axis=1), (1, GQA_HEADS))
    return mla_cos, mla_sin, gqa_cos, gqa_sin


def _t5_bucket(rel):
    nb = T5_BUCKETS // 2
    max_exact = nb // 2
    n = jnp.abs(rel)
    nf = jnp.maximum(n, 1).astype(F32)
    large = max_exact + (jnp.log(nf / max_exact) / math.log(T5_MAX_DIST / max_exact) * (nb - max_exact)).astype(jnp.int32)
    large = jnp.minimum(large, nb - 1)
    return jnp.where(rel > 0, nb, 0) + jnp.where(n < max_exact, n, large)


def _band_bias(table, stride, head_lo, heads, half_window):
    b = BAND_BLOCK
    offs = jnp.arange(3 * b)[None, :] - b - jnp.arange(b)[:, None]
    one_hot = (_t5_bucket(offs * stride)[..., None] == jnp.arange(T5_BUCKETS)).astype(F32)
    bias = jnp.dot(one_hot.reshape(b * 3 * b, T5_BUCKETS), table[:, head_lo:head_lo + heads],
                   precision=lax.Precision.HIGHEST)
    bias = bias.T.reshape(heads, b, 3 * b)
    return jnp.where((jnp.abs(offs) <= half_window)[None], bias, NEG_INF)


def _w_in_rows(d):
    mla, gqa, win, dil0 = MLA_BLK * SMALL_W, GQA_BLK * SMALL_W, WIN_BLK * BAND_W, DIL_BLK * BAND_W
    plan, at = [], 0
    for width, target, row in ((256, 0, mla), (128, 0, mla + 256), (32, 0, mla + 384),
                               (256, 0, gqa), (128, 0, gqa + 256), (128, 0, gqa + 384)):
        plan.append((at, width, target, row))
        at += width
    for part in range(3):
        for g in range(len(DIL_PATTERNS)):
            plan.append((at, QW, g, (dil0 if g == 0 else 0) + part * QW))
            at += QW
    for width, row in ((256, win), (128, win + 256), (128, win + 384), (N_BRANCH * BRANCH_W, 0),
                       (N_BRANCH * d, N_BRANCH * BRANCH_W)):
        plan.append((at, width, 0, row))
        at += width
    return plan


@jax.custom_vjp
def _w_in_layout(w_in_t):
    d = w_in_t.shape[1]
    outs = []
    for target, rows in enumerate((P_TOT, BAND_W, BAND_W)):
        parts, at = [], 0
        for start, width, _, row in sorted((p for p in _w_in_rows(d) if p[2] == target), key=lambda p: p[3]):
            if row > at:
                parts.append(jnp.zeros((row - at, d), w_in_t.dtype))
            parts.append(w_in_t[start:start + width])
            at = row + width
        if at < rows:
            parts.append(jnp.zeros((rows - at, d), w_in_t.dtype))
        outs.append(jnp.concatenate(parts, axis=0))
    return outs


def _w_in_layout_fwd(w_in_t):
    return _w_in_layout(w_in_t), None


def _w_in_layout_bwd(_, cts):
    d = cts[0].shape[1]
    return (jnp.concatenate([cts[target][row:row + width] for _, width, target, row in _w_in_rows(d)], axis=0),)


_w_in_layout.defvjp(_w_in_layout_fwd, _w_in_layout_bwd)


def _layer(x, w, l, aux, biases):
    w_kv = w["w_kv_t"][l].T.reshape(MLA_KV_LORA, MLA_HEADS, MLA_NOPE + MLA_V)
    w_k = jnp.concatenate([w_kv[:, :, :MLA_NOPE], jnp.zeros((MLA_KV_LORA, MLA_HEADS, MLA_ROPE), w_kv.dtype)], axis=2)
    dil_bias, win_bias = biases
    prm = dict(
        g_q=w["mla_q_norm_g"][l][None, :], g_kv=w["mla_kv_norm_g"][l][None, :], w_q=w["w_q_t"][l].T,
        w_k=w_k.reshape(MLA_KV_LORA, MLA_HEADS * MLA_QK),
        w_v=w_kv[:, :, MLA_NOPE:].reshape(MLA_KV_LORA, MLA_HEADS * MLA_V),
        gq=jnp.tile(w["gqa_q_norm_g"][l], GQA_HEADS)[None, :], gk=jnp.tile(w["gqa_k_norm_g"][l], GQA_KV_HEADS)[None, :],
        bias_dil=list(dil_bias), bias_win=win_bias, sink=w["win_sink"][l].reshape(WIN_HEADS, 1, 1),
        no_sink=jnp.full((DIL_HEADS, 1, 1), NEG_INF, F32), w_branch=jnp.transpose(w["w_branch_t"][l].reshape(-1, N_BRANCH, BRANCH_W), (1, 2, 0)))
    layer_w = dict(norm_g=w["norm_g"][l][None, :], w_in_t=_w_in_layout(w["w_in_t"][l]), mixer=prm, w_out=w["w_out"][l])
    return x + _layer_core(x, layer_w, aux)


def _local_loss(w, x, target):
    s, d_model = x.shape
    assert d_model == D_MODEL, "the projection's window layout is laid out for d_model 1024"
    place = np.zeros((MLA_ROPE, MLA_HEADS * MLA_QK), np.float32)
    for h in range(MLA_HEADS):
        for i in range(MLA_ROPE):
            place[i, h * MLA_QK + MLA_NOPE + i] = 1.0

    def head_mean(nh):
        m = np.kron(np.eye(nh, dtype=np.float32), np.full((HEAD_DIM, HEAD_DIM), 1.0 / HEAD_DIM, np.float32))
        return jnp.asarray(m)

    aux = _rope_tables(s) + (jnp.asarray(place), head_mean(GQA_HEADS), head_mean(GQA_KV_HEADS))
    table = w["t5_table"]
    dil_bias = [_band_bias(table, dil, gi * DIL_HEADS, DIL_HEADS, window // (2 * dil))
                for gi, (window, dil) in enumerate(DIL_PATTERNS)]
    win_bias = _band_bias(table, 1, len(DIL_PATTERNS) * DIL_HEADS, WIN_HEADS, WIN_HALF)
    for l in range(w["norm_g"].shape[0]):
        x = _layer(x, w, l, aux, (dil_bias, win_bias))
    return _loss_op(x, target, w["final_norm_g"][None, :])


_ANY = pl.BlockSpec(memory_space=pl.ANY)
_MESH = pl.DeviceIdType.MESH


def _all_gather(block, name):
    def body(x_ref, out_ref, send_sems, recv_sems, local_sem):
        x, y, c = lax.axis_index("x"), lax.axis_index("y"), lax.axis_index("c")
        me, sibling = (x, y, c), (x, y, 1 - c)
        chips = [(1 - x, y), (x, 1 - y), (1 - x, 1 - y)]

        def slot(px, py, pc):
            return out_ref.at[4 * px + 2 * py + pc]

        def copy(k, blk, to, src=None):
            return pltpu.make_async_remote_copy(
                src_ref=slot(*blk) if src is None else src, dst_ref=slot(*blk),
                send_sem=send_sems.at[k], recv_sem=recv_sems.at[k], device_id=to, device_id_type=_MESH)

        mine = pltpu.make_async_copy(x_ref, slot(*me), local_sem)
        mine.start()
        first = [copy(0, me, sibling, src=x_ref)]
        first += [copy(1 + j, me, (*chip, c), src=x_ref) for j, chip in enumerate(chips)]
        for cp in first:
            cp.start()
        passed = [copy(4 + j, (*chip, c), sibling) for j, chip in enumerate(chips)]
        for j, chip in enumerate(chips):
            copy(1 + j, (*chip, c), me).wait_recv()
            passed[j].start()
        copy(0, sibling, me).wait_recv()
        for j, chip in enumerate(chips):
            copy(4 + j, (*chip, 1 - c), me).wait_recv()
        for cp in first + passed:
            cp.wait_send()
        mine.wait()

    return pl.pallas_call(
        body,
        out_shape=jax.ShapeDtypeStruct((N_DEV,) + block.shape, block.dtype),
        in_specs=[_ANY],
        out_specs=_ANY,
        scratch_shapes=[pltpu.SemaphoreType.DMA((7,)), pltpu.SemaphoreType.DMA((7,)), pltpu.SemaphoreType.DMA],
        name=name,
    )(block)


def _all_to_all(blocks, name):
    def body(x_ref, out_ref, send_sems, recv_sems, local_sem):
        x, y, c = lax.axis_index("x"), lax.axis_index("y"), lax.axis_index("c")
        me = 4 * x + 2 * y + c
        mine = pltpu.make_async_copy(x_ref.at[me], out_ref.at[me], local_sem)
        mine.start()
        copies, landed = [], []
        for k in range(1, N_DEV):
            px = 1 - x if k & 4 else x
            py = 1 - y if k & 2 else y
            pc = 1 - c if k & 1 else c
            peer = 4 * px + 2 * py + pc
            copies.append(pltpu.make_async_remote_copy(
                src_ref=x_ref.at[peer], dst_ref=out_ref.at[me], send_sem=send_sems.at[k - 1],
                recv_sem=recv_sems.at[k - 1], device_id=(px, py, pc), device_id_type=_MESH))
            landed.append(pltpu.make_async_remote_copy(
                src_ref=x_ref.at[peer], dst_ref=out_ref.at[peer], send_sem=send_sems.at[k - 1],
                recv_sem=recv_sems.at[k - 1], device_id=(px, py, pc), device_id_type=_MESH))
        for cp in copies:
            cp.start()
        for cp in landed:
            cp.wait_recv()
        for cp in copies:
            cp.wait_send()
        mine.wait()

    return pl.pallas_call(
        body,
        out_shape=jax.ShapeDtypeStruct(blocks.shape, blocks.dtype),
        in_specs=[_ANY],
        out_specs=_ANY,
        scratch_shapes=[pltpu.SemaphoreType.DMA((7,)), pltpu.SemaphoreType.DMA((7,)), pltpu.SemaphoreType.DMA],
        name=name,
    )(blocks)


def _sum_slots(parts, name):
    _, rows, w = parts.shape
    tr = _pick(rows, (1024, 512, 256, 128, 64, 32, 16, 8))

    def body(p_ref, o_ref):
        acc = p_ref[0].astype(F32)
        for j in range(1, N_DEV):
            acc = acc + p_ref[j].astype(F32)
        o_ref[...] = acc

    return pl.pallas_call(
        body,
        grid=(rows // tr,),
        in_specs=[pl.BlockSpec((N_DEV, tr, w), lambda i: (0, i, 0))],
        out_specs=pl.BlockSpec((tr, w), lambda i: (i, 0)),
        out_shape=jax.ShapeDtypeStruct((rows, w), F32),
        name=name,
        compiler_params=_params("parallel"),
    )(parts)


def _adamw(w, g, m, v, name):
    rows, width = w.shape
    tr = _pick(rows, (1024, 512, 256, 128, 64, 32, 16, 8))

    def body(w_ref, g_ref, m_ref, v_ref, d_ref, nm_ref, nv_ref):
        g_ = g_ref[...]
        m_ = ADAM_B1 * m_ref[...] + (1.0 - ADAM_B1) * g_
        v_ = ADAM_B2 * v_ref[...] + (1.0 - ADAM_B2) * jnp.square(g_)
        m_hat = m_ / (1.0 - ADAM_B1 ** ADAM_STEP)
        v_hat = v_ / (1.0 - ADAM_B2 ** ADAM_STEP)
        d_ref[...] = -ADAM_LR * (m_hat / (jnp.sqrt(v_hat) + ADAM_EPS) + ADAM_WD * w_ref[...])
        nm_ref[...] = m_
        nv_ref[...] = v_

    spec = pl.BlockSpec((tr, width), lambda i: (i, 0))
    return pl.pallas_call(
        body,
        grid=(rows // tr,),
        in_specs=[spec] * 4,
        out_specs=[spec] * 3,
        out_shape=[jax.ShapeDtypeStruct((rows, width), F32)] * 3,
        name=name,
        compiler_params=_params("parallel"),
    )(w, g, m, v)


_SHARDED = (("w_in", 2), ("w_mla_q_up", 2), ("w_mla_kv_up", 2), ("w_branch", 3), ("w_out", 1))
_REPLICATED = ("norm_g", "mla_q_norm_g", "mla_kv_norm_g", "gqa_q_norm_g", "gqa_k_norm_g", "win_sink", "t5_table",
               "final_norm_g")


def _pack(arrays, row_multiple):
    flat = jnp.concatenate([a.reshape(-1) for a in arrays])
    rows = -(-flat.shape[0] // (LANES * row_multiple)) * row_multiple
    return jnp.pad(flat, (0, rows * LANES - flat.shape[0])).reshape(rows, LANES)


def _unpack(packed, shapes):
    flat, out, at = packed.reshape(-1), [], 0
    for shp in shapes:
        n = int(np.prod(shp))
        out.append(flat[at:at + n].reshape(shp))
        at += n
    return out


_TO_WIRE = {
    "w_in": lambda t: jnp.swapaxes(t, 1, 2), "w_mla_q_up": lambda t: jnp.swapaxes(t, 1, 2),
    "w_mla_kv_up": lambda t: jnp.swapaxes(t, 1, 2),
    "w_branch": lambda t: jnp.transpose(t, (0, 3, 1, 2)).reshape(t.shape[0], t.shape[3], -1), "w_out": lambda t: t}
_FROM_WIRE = {
    "w_in": lambda t, shp: jnp.swapaxes(t, 1, 2), "w_mla_q_up": lambda t, shp: jnp.swapaxes(t, 1, 2),
    "w_mla_kv_up": lambda t, shp: jnp.swapaxes(t, 1, 2),
    "w_branch": lambda t, shp: jnp.transpose(t.reshape(shp[0], shp[3], shp[1], shp[2]), (0, 2, 3, 1)),
    "w_out": lambda t, shp: t}
_WIRE_NAME = {"w_in": "w_in_t", "w_mla_q_up": "w_q_t", "w_mla_kv_up": "w_kv_t", "w_branch": "w_branch_t",
              "w_out": "w_out"}


def _join_shards(gathered, wire_shapes):
    out, at = [], 0
    for depth, cut, rest in wire_shapes:
        n = depth * cut * rest // LANES
        blk = gathered[:, at:at + n].reshape(N_DEV, depth, cut, rest)
        out.append(jnp.moveaxis(blk, 0, 1).reshape(depth, N_DEV * cut, rest))
        at += n
    return out


def _split_shards(fulls, wire_shapes):
    parts = []
    for full, (depth, cut, rest) in zip(fulls, wire_shapes):
        blk = jnp.moveaxis(full.reshape(depth, N_DEV, cut, rest), 1, 0)
        parts.append(blk.reshape(N_DEV, depth * cut * rest // LANES, LANES))
    return jnp.concatenate(parts, axis=1)


def kernel(x, norm_g, w_in, mla_q_norm_g, mla_kv_norm_g, w_mla_q_up, w_mla_kv_up, gqa_q_norm_g, gqa_k_norm_g, win_sink, t5_table, w_branch, w_out, final_norm_g, loss_target, m_norm_g, m_w_in, m_mla_q_norm_g, m_mla_kv_norm_g, m_w_mla_q_up, m_w_mla_kv_up, m_gqa_q_norm_g, m_gqa_k_norm_g, m_win_sink, m_t5_table, m_w_branch, m_w_out, m_final_norm_g, v_norm_g, v_w_in, v_mla_q_norm_g, v_mla_kv_norm_g, v_w_mla_q_up, v_w_mla_kv_up, v_gqa_q_norm_g, v_gqa_k_norm_g, v_win_sink, v_t5_table, v_w_branch, v_w_out, v_final_norm_g):
    given = dict(locals())
    names = ("norm_g", "w_in", "mla_q_norm_g", "mla_kv_norm_g", "w_mla_q_up", "w_mla_kv_up", "gqa_q_norm_g",
             "gqa_k_norm_g", "win_sink", "t5_table", "w_branch", "w_out", "final_norm_g")
    shard_names = [n for n, _ in _SHARDED]
    shard_shapes = [given[n].shape for n in shard_names]

    wire = [_TO_WIRE[n](given[n]).astype(BF16) for n in shard_names]
    wire_shapes = [t.shape for t in wire]
    gathered = _all_gather(jnp.concatenate([t.reshape(-1, LANES) for t in wire]), "gather_weights")
    weights = {n: given[n] for n in _REPLICATED}
    weights.update(zip([_WIRE_NAME[n] for n in shard_names], _join_shards(gathered, wire_shapes)))

    loss, (gw, gx) = jax.value_and_grad(_local_loss, argnums=(0, 1))(weights, x[0], loss_target[0])
    loss = lax.psum(loss, ("x", "y", "c"))

    send = _split_shards([gw[_WIRE_NAME[n]] for n in shard_names], wire_shapes)
    g_wire = _unpack(_sum_slots(_all_to_all(send, "scatter_grads"), "sum_grads"), wire_shapes)
    g_shard = [_FROM_WIRE[n](t, shp) for n, t, shp in zip(shard_names, g_wire, shard_shapes)]
    rep_shapes = [given[n].shape for n in _REPLICATED]
    g_rep = _unpack(_sum_slots(_all_gather(_pack([gw[n] for n in _REPLICATED], 8), "gather_small_grads"),
                               "sum_small_grads"), rep_shapes)
    grads = dict(zip(shard_names, g_shard))
    grads.update(zip(_REPLICATED, g_rep))

    def update(group, shapes, row_multiple, name):
        outs = _adamw(*[_pack([src[n] for n in group], row_multiple) for src in (
            given, grads, {n: given["m_" + n] for n in group}, {n: given["v_" + n] for n in group})], name)
        return [dict(zip(group, _unpack(o, shapes))) for o in outs]

    big = update(shard_names, shard_shapes, 16, "adamw_shards")
    small = update(list(_REPLICATED), rep_shapes, 8, "adamw_replicated")
    delta, new_m, new_v = [{**b, **s_} for b, s_ in zip(big, small)]
    return (loss, gx[None], *[grads[n] for n in names], *[delta[n] for n in names],
            *[new_m[n] for n in names], *[new_v[n] for n in names])
```

```python
import functools
import math

import jax
import jax.numpy as jnp
import numpy as np
from jax import lax
from jax.experimental import pallas as pl
from jax.experimental.pallas import tpu as pltpu

F32 = jnp.float32
BF16 = jnp.bfloat16
N_DEV = 8
LANES = 128
HALF = LANES // 2
V7X_VMEM_LIMIT = 56 * 1024 * 1024

EPS = 1e-6
NEG_INF = -1e30
LOG2E = 1.4426950408889634
ROPE_THETA = 10000.0
GRID_W = 64
HEAD_DIM = 64
N_BRANCH = 4
BRANCH_W = 256
MLA_HEADS, MLA_Q_LORA, MLA_KV_LORA, MLA_NOPE, MLA_ROPE, MLA_V = 4, 256, 128, 64, 32, 64
MLA_QK = MLA_NOPE + MLA_ROPE
GQA_HEADS, GQA_KV_HEADS = 4, 2
DIL_PATTERNS = ((128, 1), (512, 4), (2048, 16))
DIL_HEADS = 4
WIN_HEADS, WIN_KV_HEADS, WIN_HALF = 4, 2, 128
T5_BUCKETS, T5_MAX_DIST = 32, 1024
BAND_BLOCK = 128
ADAM_LR, ADAM_B1, ADAM_B2, ADAM_EPS, ADAM_WD, ADAM_STEP = 0.001, 0.9, 0.999, 1e-08, 0.01, 10

D_MODEL = 1024
GM_W, SMALL_W, BAND_W = 5120, 512, 768
MLA_BLK, GQA_BLK, WIN_BLK, DIL_BLK = 10, 11, 8, 9
P_TOT = 7680
QW = 256


def _params(*sem):
    return pltpu.CompilerParams(dimension_semantics=sem, vmem_limit_bytes=V7X_VMEM_LIMIT)


def _pick(n, cands):
    for c in cands:
        if n % c == 0:
            return c
    return n


def _dot(a, b, ca, cb):
    return lax.dot_general(a.astype(BF16), b.astype(BF16), (((ca,), (cb,)), ((), ())), preferred_element_type=F32)


def _bmm(a, b, ca, cb):
    return lax.dot_general(a, b, (((ca,), (cb,)), ((0,), (0,))), preferred_element_type=F32)


@jax.custom_vjp
def _bdot(a, b):
    return _dot(a, b, 1, 0)


def _bdot_fwd(a, b):
    return _dot(a, b, 1, 0), (a, b)


def _bdot_bwd(res, g):
    a, b = res
    return _dot(g, b, 1, 1), _dot(a, g, 0, 0)


_bdot.defvjp(_bdot_fwd, _bdot_bwd)


def _hdot(a, c):
    return lax.dot_general(a, c, (((1,), (0,)), ((), ())), precision=lax.Precision.HIGHEST, preferred_element_type=F32)


@functools.partial(jax.custom_vjp, nondiff_argnums=(1,))
def _lane_roll(x, shift):
    return pltpu.roll(x, shift, 1)


def _lane_roll_fwd(x, shift):
    return pltpu.roll(x, shift, 1), None


def _lane_roll_bwd(shift, _, g):
    return (pltpu.roll(g, g.shape[1] - shift, 1),)


_lane_roll.defvjp(_lane_roll_fwd, _lane_roll_bwd)


@functools.partial(jax.custom_vjp, nondiff_argnums=(1,))
def _lane_ranges(x, cut):
    bounds, _ = cut
    return tuple(x[:, lo:hi] for lo, hi in zip(bounds[:-1], bounds[1:]))


def _lane_ranges_fwd(x, cut):
    return _lane_ranges(x, cut), None


def _lane_ranges_bwd(cut, _, cts):
    bounds, width = cut
    parts = list(cts)
    if bounds[-1] < width:
        parts.append(jnp.zeros((cts[0].shape[0], width - bounds[-1]), cts[0].dtype))
    return (jnp.concatenate(parts, axis=1),)


_lane_ranges.defvjp(_lane_ranges_fwd, _lane_ranges_bwd)


def _lanes(x, bounds):
    return _lane_ranges(x, (tuple(bounds), x.shape[1]))


@jax.custom_vjp
def _unstack(x):
    return tuple(x[i] for i in range(x.shape[0]))


def _unstack_fwd(x):
    return _unstack(x), None


def _unstack_bwd(_, cts):
    return (jnp.stack(cts, axis=0),)


_unstack.defvjp(_unstack_fwd, _unstack_bwd)


@functools.partial(jax.custom_vjp, nondiff_argnums=(1,))
def _split_heads(x, h):
    d = x.shape[1] // h
    return jnp.stack([x[:, i * d:(i + 1) * d] for i in range(h)], axis=0)


def _split_heads_fwd(x, h):
    return _split_heads(x, h), None


def _split_heads_bwd(h, _, ct):
    return (jnp.concatenate([ct[i] for i in range(h)], axis=1),)


_split_heads.defvjp(_split_heads_fwd, _split_heads_bwd)


def _join_heads(x):
    return jnp.concatenate(_unstack(x), axis=1)


def _rope(x, cos_t, sin_t, half):
    w = x.shape[1]
    lane = lax.broadcasted_iota(jnp.int32, (1, w), 1)
    first = (lane % (2 * half)) < half
    partner = jnp.where(first, _lane_roll(x, w - half), _lane_roll(x, half))
    return x * cos_t + partner * sin_t


def _rms(x, g):
    return x * lax.rsqrt(jnp.mean(x * x, axis=-1, keepdims=True) + EPS) * g


def _rows(tr, w, col=0):
    return pl.BlockSpec((tr, w), lambda i: (i, col))


def _head_rows(h, tr, d):
    return pl.BlockSpec((h, tr, d), lambda i: (0, i, 0))


def _whole(shape):
    nd = len(shape)
    return pl.BlockSpec(tuple(shape), lambda i: (0,) * nd)


def _fwd_call(name, fn, steps, rows, params, aux, outs):
    nr, npar, na = len(rows), len(params), len(aux)

    def body(*refs):
        vals = [x[...].astype(F32) for x in refs[:nr + npar + na]]
        res = fn(vals[:nr], vals[nr:nr + npar], vals[nr + npar:])
        for o_ref, o in zip(refs[nr + npar + na:], res):
            o_ref[...] = o.astype(o_ref.dtype)

    return pl.pallas_call(
        body,
        grid=(steps,),
        in_specs=[s for _, s in rows] + [_whole(p.shape) for p in params] + [s for _, s in aux],
        out_specs=[e[1] for e in outs],
        out_shape=[jax.ShapeDtypeStruct(e[0], e[2] if len(e) > 2 else F32) for e in outs],
        name=name + "_fwd",
        compiler_params=_params("parallel"),
    )(*[a for a, _ in rows], *params, *[a for a, _ in aux])


def _vjp_call(name, fn, steps, rows, params, aux, cts, row_grads, into=None):
    nr, npar, na, nc = len(rows), len(params), len(aux), len(cts)
    n_in = nr + npar + na + nc
    lead = 0 if into is None else 1

    def body(*refs):
        refs = refs[lead:]
        vals = [x[...].astype(F32) for x in refs[:n_in]]
        r, p, a, d = vals[:nr], vals[nr:nr + npar], vals[nr + npar:nr + npar + na], vals[nr + npar + na:]
        out_refs = refs[n_in:]
        _, vjp = jax.vjp(lambda r_, p_: tuple(fn(r_, p_, a)), r, p)
        dr, dp = vjp(tuple(d))
        for o_ref, o in zip(out_refs[:nr], dr):
            o_ref[...] = o.astype(o_ref.dtype)

        @pl.when(pl.program_id(0) == 0)
        def _():
            for o_ref in out_refs[nr:]:
                o_ref[...] = jnp.zeros_like(o_ref)

        for o_ref, o in zip(out_refs[nr:], dp):
            o_ref[...] += o

    outs = pl.pallas_call(
        body,
        grid=(steps,),
        in_specs=([] if into is None else [pl.BlockSpec(memory_space=pl.ANY)])
        + [s for _, s in rows] + [_whole(p.shape) for p in params] + [s for _, s in aux] + [s for _, s in cts],
        out_specs=[e[1] for e in row_grads] + [_whole(p.shape) for p in params],
        out_shape=[jax.ShapeDtypeStruct(e[0], e[2] if len(e) > 2 else F32) for e in row_grads]
        + [jax.ShapeDtypeStruct(p.shape, F32) for p in params],
        input_output_aliases={} if into is None else {0: 0},
        name=name + "_bwd",
        compiler_params=_params("arbitrary"),
    )(*([] if into is None else [into]), *[a for a, _ in rows], *params, *[a for a, _ in aux], *[a for a, _ in cts])
    return list(outs[:nr]), list(outs[nr:])


def _norm_tile(r, p, a):
    return (_rms(r[0], p[0]),)


def _mm(a, b, mode, name, out_dtype=F32):
    if mode == "nn":
        (m, k), n = a.shape, b.shape[1]
    elif mode == "nt":
        (m, k), n = a.shape, b.shape[0]
    else:
        (k, m), n = a.shape, b.shape[1]
    tm = _pick(m, (1024, 512, 256, 128))
    tn = _pick(n, (1024, 768, 512, 384, 256, 128))
    tk = _pick(k, (1024, 768, 512, 384, 256, 128))
    if mode == "tn":
        tk = _pick(k, (512, 256, 128))
    nk = k // tk

    def body(a_ref, b_ref, o_ref, acc_ref):
        kk = pl.program_id(2)
        if mode == "nn":
            part = _dot(a_ref[...], b_ref[...], 1, 0)
        elif mode == "nt":
            part = _dot(a_ref[...], b_ref[...], 1, 1)
        else:
            part = _dot(a_ref[...], b_ref[...], 0, 0)
        if nk == 1:
            o_ref[...] = part.astype(o_ref.dtype)
        else:
            @pl.when(kk == 0)
            def _():
                acc_ref[...] = part

            @pl.when(kk > 0)
            def _():
                acc_ref[...] += part

            @pl.when(kk == nk - 1)
            def _():
                o_ref[...] = acc_ref[...].astype(o_ref.dtype)

    if mode == "nn":
        a_spec = pl.BlockSpec((tm, tk), lambda i, j, kk: (i, kk))
        b_spec = pl.BlockSpec((tk, tn), lambda i, j, kk: (kk, j))
    elif mode == "nt":
        a_spec = pl.BlockSpec((tm, tk), lambda i, j, kk: (i, kk))
        b_spec = pl.BlockSpec((tn, tk), lambda i, j, kk: (j, kk))
    else:
        a_spec = pl.BlockSpec((tk, tm), lambda i, j, kk: (kk, i))
        b_spec = pl.BlockSpec((tk, tn), lambda i, j, kk: (kk, j))
    return pl.pallas_call(
        body,
        grid=(m // tm, n // tn, nk),
        in_specs=[a_spec, b_spec],
        out_specs=pl.BlockSpec((tm, tn), lambda i, j, kk: (i, j)),
        out_shape=jax.ShapeDtypeStruct((m, n), out_dtype),
        scratch_shapes=[pltpu.VMEM((tm, tn), F32)],
        name=name,
        compiler_params=_params("parallel", "parallel", "arbitrary"),
    )(a, b)


def _dense_fwd_call(q, k, v, scale, name):
    n, sq, d = q.shape
    sk, dv = k.shape[1], v.shape[2]
    tq = _pick(sq, (256, 128))
    c = scale * LOG2E

    def body(q_ref, k_ref, v_ref, o_ref, lse_ref, k_s, vext_s):
        @pl.when(pl.program_id(1) == 0)
        def _():
            k_s[...] = k_ref[0].astype(BF16)
            vext_s[...] = jnp.ones_like(vext_s)
            vext_s[:, :dv] = v_ref[0].astype(BF16)

        s = _dot(q_ref[0], k_s[...], 1, 1)
        m = jnp.max(s, axis=1, keepdims=True)
        p = jnp.exp2(s * c - m * c)
        acc = _dot(p, vext_s[...], 1, 0)
        l = acc[:, dv:dv + 1]
        o_ref[0] = acc[:, :dv] / l
        lse_ref[0] = m * scale + jnp.log(l)

    return pl.pallas_call(
        body,
        grid=(n, sq // tq),
        in_specs=[
            pl.BlockSpec((1, tq, d), lambda h, i: (h, i, 0)),
            pl.BlockSpec((1, sk, d), lambda h, i: (h, 0, 0)),
            pl.BlockSpec((1, sk, dv), lambda h, i: (h, 0, 0)),
        ],
        out_specs=[
            pl.BlockSpec((1, tq, dv), lambda h, i: (h, i, 0)),
            pl.BlockSpec((1, tq, 1), lambda h, i: (h, i, 0)),
        ],
        out_shape=[jax.ShapeDtypeStruct((n, sq, dv), F32), jax.ShapeDtypeStruct((n, sq, 1), F32)],
        scratch_shapes=[pltpu.VMEM((sk, d), BF16), pltpu.VMEM((sk, 2 * dv), BF16)],
        name=name + "_fwd",
        compiler_params=_params("arbitrary", "arbitrary"),
    )(q, k, v)


def _dense_bwd_call(q, k, v, o, lse, do, scale, name):
    n, sq, d = q.shape
    sk, dv = k.shape[1], v.shape[2]
    tq, tk = _pick(sq, (512, 256, 128)), _pick(sk, (2048, 1024, 512, 256, 128))
    c = scale * LOG2E

    def body(q_ref, k_ref, v_ref, o_ref, lse_ref, do_ref, dq_ref, dk_ref, dv_ref):
        j, i = pl.program_id(1), pl.program_id(2)
        qb, kb, vb = q_ref[0].astype(BF16), k_ref[0].astype(BF16), v_ref[0].astype(BF16)
        do_f = do_ref[0]
        dob = do_f.astype(BF16)
        p = jnp.exp2(_dot(qb, kb, 1, 1) * c - lse_ref[0] * LOG2E)
        delta = jnp.sum(do_f * o_ref[0], axis=1, keepdims=True)
        ds = (p * (_dot(dob, vb, 1, 1) - delta)).astype(BF16)
        dv_part = _dot(p, dob, 0, 0)
        dk_part = _dot(ds, qb, 0, 0) * scale
        dq_part = _dot(ds, kb, 1, 0) * scale
        rows = pl.ds(pl.multiple_of(i * tq, tq), tq)

        @pl.when(i == 0)
        def _():
            dk_ref[0] = dk_part
            dv_ref[0] = dv_part

        @pl.when(i > 0)
        def _():
            dk_ref[0] += dk_part
            dv_ref[0] += dv_part

        @pl.when(j == 0)
        def _():
            dq_ref[0, rows, :] = dq_part

        @pl.when(j > 0)
        def _():
            dq_ref[0, rows, :] += dq_part

    return pl.pallas_call(
        body,
        grid=(n, sk // tk, sq // tq),
        in_specs=[
            pl.BlockSpec((1, tq, d), lambda h, j, i: (h, i, 0)),
            pl.BlockSpec((1, tk, d), lambda h, j, i: (h, j, 0)),
            pl.BlockSpec((1, tk, dv), lambda h, j, i: (h, j, 0)),
            pl.BlockSpec((1, tq, dv), lambda h, j, i: (h, i, 0)),
            pl.BlockSpec((1, tq, 1), lambda h, j, i: (h, i, 0)),
            pl.BlockSpec((1, tq, dv), lambda h, j, i: (h, i, 0)),
        ],
        out_specs=[
            pl.BlockSpec((1, sq, d), lambda h, j, i: (h, 0, 0)),
            pl.BlockSpec((1, tk, d), lambda h, j, i: (h, j, 0)),
            pl.BlockSpec((1, tk, dv), lambda h, j, i: (h, j, 0)),
        ],
        out_shape=[
            jax.ShapeDtypeStruct((n, sq, d), F32),
            jax.ShapeDtypeStruct((n, sk, d), F32),
            jax.ShapeDtypeStruct((n, sk, dv), F32),
        ],
        name=name + "_bwd",
        compiler_params=_params("arbitrary", "arbitrary", "arbitrary"),
    )(q, k, v, o, lse, do)


def _head_geometry(h, group):
    pair, a = divmod(h, 2)
    kv_pair, b = divmod(h // group, 2)
    return pair, a, kv_pair, b


def _lane_half():
    return lax.broadcasted_iota(jnp.int32, (1, LANES), 1) // HALF


def _align(x, a, b):
    if a != b:
        x = pltpu.roll(x, HALF, 1)
    return jnp.where(_lane_half() == b, x, 0.0)


def _unalign(x, a, b):
    x = jnp.where(_lane_half() == b, x, 0.0)
    return pltpu.roll(x, HALF, 1) if a != b else x


def _bands(w, pw, nw, lo, kvw, nb):
    b = BAND_BLOCK
    cat = jnp.concatenate([pw[:, lo:lo + kvw], w[:, lo:lo + kvw], nw[:, lo:lo + kvw]], axis=0).astype(BF16)
    out = []
    for g in range(kvw // LANES):
        c3 = cat[:, g * LANES:(g + 1) * LANES].reshape(nb + 2, b, LANES)
        out.append(jnp.concatenate([c3[0:nb], c3[1:nb + 1], c3[2:nb + 2]], axis=1))
    return out


def _edge_mask(first_block, nb, period):
    b = BAND_BLOCK
    blk = (first_block + lax.broadcasted_iota(jnp.int32, (nb, 1, 3 * b), 0)) % period
    col = lax.broadcasted_iota(jnp.int32, (nb, 1, 3 * b), 2)
    outside = ((col < b) & (blk == 0)) | ((col >= 2 * b) & (blk == period - 1))
    return jnp.where(outside, NEG_INF, 0.0)


def _band_geometry(proj, dil):
    rows = proj.shape[0]
    tl = _pick(rows, (1024, 512, 256, 128))
    return rows, tl, tl // BAND_BLOCK, rows // tl, rows // dil // BAND_BLOCK


def _band_in_specs(tl, nb, n_chunks, n_blocks, col, last_step_idle):
    def chunk(i):
        return jnp.minimum(i, n_chunks - 1) if last_step_idle else i

    main = pl.BlockSpec((tl, BAND_W), lambda j, i: (j * n_chunks + chunk(i), col))
    prev = pl.BlockSpec((BAND_BLOCK, BAND_W),
                        lambda j, i: (j * n_blocks + jnp.maximum(chunk(i) * nb - 1, 0), col))
    nxt = pl.BlockSpec((BAND_BLOCK, BAND_W),
                       lambda j, i: (j * n_blocks + jnp.minimum((chunk(i) + 1) * nb, n_blocks - 1), col))
    rows = pl.BlockSpec((tl, QW), lambda j, i: (j * n_chunks + chunk(i), 0))
    return main, prev, nxt, rows


def _band_fwd_call(proj, col, bias, sink, dil, group, kvw, scale, name):
    s_tok = proj.shape[0]
    seq, tl, nb, n_chunks, period = _band_geometry(proj, dil)
    n_blocks = seq // BAND_BLOCK
    heads = bias.shape[0]

    def body(w_ref, pw_ref, nw_ref, bias_ref, sink_ref, o_ref, lse_ref):
        i = pl.program_id(1)
        w, pw, nw = w_ref[...], pw_ref[...], nw_ref[...]
        kb = _bands(w, pw, nw, QW, kvw, nb)
        vb = _bands(w, pw, nw, QW + kvw, kvw, nb)
        edge = _edge_mask(i * nb, nb, period)
        o_acc = [jnp.zeros((tl, LANES), F32) for _ in range(heads // 2)]
        lse_acc = [jnp.zeros((tl, LANES), F32) for _ in range(heads // 2)]
        for h in range(heads):
            pair, a, kvp, b = _head_geometry(h, group)
            q_al = _align(w[:, pair * LANES:(pair + 1) * LANES], a, b).astype(BF16).reshape(nb, BAND_BLOCK, LANES)
            logits = _bmm(q_al, kb[kvp], 2, 2) * scale + bias_ref[h][None] + edge
            sk = sink_ref[h].reshape(1, 1, 1)
            m = jnp.maximum(jnp.max(logits, axis=2, keepdims=True), sk)
            e = jnp.exp(logits - m)
            ssum = jnp.sum(e, axis=2, keepdims=True) + jnp.exp(sk - m)
            out = _bmm(e.astype(BF16), vb[kvp], 2, 1) / ssum
            o_acc[pair] = o_acc[pair] + _unalign(out.reshape(tl, LANES), a, b)
            lse = (m + jnp.log(ssum)).reshape(tl, 1)
            lse_acc[pair] = lse_acc[pair] + jnp.where(_lane_half() == a, lse, 0.0)
        o_ref[...] = jnp.concatenate(o_acc, axis=1)
        lse_ref[...] = jnp.concatenate(lse_acc, axis=1)

    main, prev, nxt, rows = _band_in_specs(tl, nb, n_chunks, n_blocks, col, False)
    return pl.pallas_call(
        body,
        grid=(1, n_chunks),
        in_specs=[main, prev, nxt, pl.BlockSpec(bias.shape, lambda j, i: (0, 0, 0)),
                  pl.BlockSpec(sink.shape, lambda j, i: (0, 0, 0))],
        out_specs=[rows, rows],
        out_shape=[jax.ShapeDtypeStruct((s_tok, QW), F32)] * 2,
        name=name + "_fwd",
        compiler_params=_params("parallel", "parallel"),
    )(proj, proj, proj, bias, sink)


def _band_bwd_call(proj, o, do, lse, dlse, bias, sink, dproj, col, dil, group, kvw, scale, name):
    seq, tl, nb, n_chunks, period = _band_geometry(proj, dil)
    lead = 0 if dproj is None else 1
    n_blocks = seq // BAND_BLOCK
    heads = bias.shape[0]
    b_ = BAND_BLOCK
    have_dlse = dlse is not None

    def body(*refs):
        (w_ref, pw_ref, nw_ref, o_ref, do_ref, lse_ref), refs = refs[lead:lead + 6], refs[lead + 6:]
        if have_dlse:
            dlse_ref, refs = refs[0], refs[1:]
        bias_ref, sink_ref, dwin_ref, dbias_ref, dsink_ref, dq_s, dk_s, dv_s = refs
        j, i = pl.program_id(0), pl.program_id(1)

        @pl.when((j == 0) & (i == 0))
        def _():
            dbias_ref[...] = jnp.zeros_like(dbias_ref)
            dsink_ref[...] = jnp.zeros_like(dsink_ref)

        @pl.when(i == 0)
        def _():
            dk_s[...] = jnp.zeros_like(dk_s)
            dv_s[...] = jnp.zeros_like(dv_s)

        @pl.when(i < n_chunks)
        def _():
            w, pw, nw = w_ref[...], pw_ref[...], nw_ref[...]
            kb = _bands(w, pw, nw, QW, kvw, nb)
            vb = _bands(w, pw, nw, QW + kvw, kvw, nb)
            edge = _edge_mask(i * nb, nb, period)
            dq_acc = [jnp.zeros((tl, LANES), F32) for _ in range(heads // 2)]
            for h in range(heads):
                pair, a, kvp, b = _head_geometry(h, group)
                lanes = slice(pair * LANES, (pair + 1) * LANES)
                mine = _lane_half() == a
                q_al = _align(w[:, lanes], a, b).astype(BF16).reshape(nb, b_, LANES)
                do_al = _align(do_ref[:, lanes], a, b).astype(BF16).reshape(nb, b_, LANES)
                lse_h = jnp.max(jnp.where(mine, lse_ref[:, lanes], NEG_INF), axis=1, keepdims=True)
                shift = -jnp.sum(jnp.where(mine, do_ref[:, lanes] * o_ref[:, lanes], 0.0), axis=1, keepdims=True)
                if have_dlse:
                    shift = shift + jnp.sum(jnp.where(mine, dlse_ref[:, lanes], 0.0), axis=1, keepdims=True)
                logits = _bmm(q_al, kb[kvp], 2, 2) * scale + bias_ref[h][None] + edge
                p = jnp.exp(logits - lse_h.reshape(nb, b_, 1))
                dlogits = p * (_bmm(do_al, vb[kvp], 2, 2) + shift.reshape(nb, b_, 1))
                dbias_ref[h] += jnp.sum(dlogits, axis=0)
                dsink_ref[h] += jnp.sum(jnp.exp(sink_ref[h] - lse_h) * shift, axis=0, keepdims=True)
                ds = (dlogits * scale).astype(BF16)
                dq_acc[pair] = dq_acc[pair] + _unalign(_bmm(ds, kb[kvp], 2, 1).reshape(tl, LANES), a, b)
                dk_band = _bmm(ds, q_al, 1, 1)
                dv_band = _bmm(p.astype(BF16), do_al, 1, 1)
                kv_lanes = slice(kvp * LANES, (kvp + 1) * LANES)
                for t in range(3):
                    at = pl.ds(pl.multiple_of(i * tl + t * b_, b_), tl)
                    dk_s[at, kv_lanes] += dk_band[:, t * b_:(t + 1) * b_, :].reshape(tl, LANES)
                    dv_s[at, kv_lanes] += dv_band[:, t * b_:(t + 1) * b_, :].reshape(tl, LANES)
            dq_s[lax.rem(i, 2)] = jnp.concatenate(dq_acc, axis=1)

        @pl.when(i >= 1)
        def _():
            at = pl.ds(pl.multiple_of((i - 1) * tl + b_, b_), tl)
            parts = [dq_s[lax.rem(i + 1, 2)], dk_s[at, :], dv_s[at, :]]
            if QW + 2 * kvw < BAND_W:
                parts.append(jnp.zeros((tl, BAND_W - QW - 2 * kvw), F32))
            dwin_ref[...] = jnp.concatenate(parts, axis=1).astype(dwin_ref.dtype)

    main, prev, nxt, rows = _band_in_specs(tl, nb, n_chunks, n_blocks, col, True)
    row_args = [o, do, lse] + ([dlse] if have_dlse else [])
    small = [pl.BlockSpec(bias.shape, lambda j, i: (0, 0, 0)), pl.BlockSpec(sink.shape, lambda j, i: (0, 0, 0))]
    return pl.pallas_call(
        body,
        grid=(1, n_chunks + 1),
        in_specs=[pl.BlockSpec(memory_space=pl.ANY)] * lead + [main, prev, nxt] + [rows] * len(row_args) + small,
        out_specs=[pl.BlockSpec((tl, BAND_W), lambda j, i: (j * n_chunks + jnp.maximum(i - 1, 0), col))] + small,
        out_shape=[jax.ShapeDtypeStruct(proj.shape, BF16), jax.ShapeDtypeStruct(bias.shape, F32),
                   jax.ShapeDtypeStruct(sink.shape, F32)],
        scratch_shapes=[pltpu.VMEM((2, tl, QW), F32), pltpu.VMEM((seq + 2 * b_, kvw), F32),
                        pltpu.VMEM((seq + 2 * b_, kvw), F32)],
        input_output_aliases={0: 0} if lead else {},
        name=name + "_bwd",
        compiler_params=_params("arbitrary", "arbitrary"),
    )(*([dproj] if lead else []), proj, proj, proj, *row_args, bias, sink)


def _loss_call(x, target, g):
    s, d = x.shape
    tr = _pick(s, (256, 128, 64, 32, 16, 8))

    def tile_loss(xt, gt, tt):
        err = jnp.square(_rms(xt, gt) - tt)
        return 0.5 * jnp.sum(jnp.mean(err, axis=-1, keepdims=True), axis=0, keepdims=True)

    def body(x_ref, t_ref, g_ref, loss_ref, dx_ref, dg_ref):
        tt = t_ref[...]
        val, vjp = jax.vjp(lambda xt, gt: tile_loss(xt, gt, tt), x_ref[...], g_ref[...])
        dx, dg = vjp(jnp.ones_like(val))
        dx_ref[...] = dx

        @pl.when(pl.program_id(0) == 0)
        def _():
            loss_ref[...] = jnp.zeros_like(loss_ref)
            dg_ref[...] = jnp.zeros_like(dg_ref)

        loss_ref[...] += val
        dg_ref[...] += dg

    return pl.pallas_call(
        body,
        grid=(s // tr,),
        in_specs=[_rows(tr, d), _rows(tr, d), _whole((1, d))],
        out_specs=[_whole((1, 1)), _rows(tr, d), _whole((1, d))],
        out_shape=[jax.ShapeDtypeStruct((1, 1), F32), jax.ShapeDtypeStruct((s, d), F32),
                   jax.ShapeDtypeStruct((1, d), F32)],
        name="final_norm_loss",
        compiler_params=_params("arbitrary"),
    )(x, target, g)


@jax.custom_vjp
def _loss_op(x, target, g):
    return _loss_call(x, target, g)[0][0, 0]


def _loss_op_fwd(x, target, g):
    loss, dx, dg = _loss_call(x, target, g)
    return loss[0, 0], (dx, dg, target)


def _loss_op_bwd(res, ct):
    dx, dg, target = res
    return ct * dx, jnp.zeros_like(target), ct * dg


_loss_op.defvjp(_loss_op_fwd, _loss_op_bwd)


def _mla_tile(r, p, a):
    g_q, g_kv, w_q, w_k, w_v = p
    cos_t, sin_t, place_kr = a
    a_q, a_kv, a_kr = _lanes(r[0], (0, MLA_Q_LORA, MLA_Q_LORA + MLA_KV_LORA, MLA_Q_LORA + MLA_KV_LORA + MLA_ROPE))
    q = _rope(_bdot(_rms(a_q, g_q), w_q), cos_t, sin_t, MLA_ROPE // 2)
    ckv = _rms(a_kv, g_kv)
    k = _rope(_bdot(ckv, w_k) + _hdot(a_kr, place_kr), cos_t, sin_t, MLA_ROPE // 2)
    return _split_heads(q, MLA_HEADS), _split_heads(k, MLA_HEADS), _split_heads(_bdot(ckv, w_v), MLA_HEADS)


def _head_rms(x, g, head_mean):
    return x * lax.rsqrt(_hdot(x * x, head_mean) + EPS) * g


def _gqa_tile(r, p, a):
    g_q, g_k = p
    cos_t, sin_t, mean_q, mean_k = a
    wq, wk = GQA_HEADS * HEAD_DIM, GQA_KV_HEADS * HEAD_DIM
    b_q, b_k, b_v = _lanes(r[0], (0, wq, wq + wk, wq + 2 * wk))
    q = _rope(_head_rms(b_q, g_q, mean_q), cos_t, sin_t, HEAD_DIM // 4)
    k = _rope(_head_rms(b_k, g_k, mean_k), cos_t[:, :wk], sin_t[:, :wk], HEAD_DIM // 4)
    return _split_heads(q, GQA_HEADS), _split_heads(k, GQA_KV_HEADS), _split_heads(b_v, GQA_KV_HEADS)


def _merge_tile(r, p, a):
    gm, o_a, o_b, oc0, oc1, oc2, l0, l1, l2, o_d = r
    (w_branch,) = p
    d = w_branch.shape[2]
    gate_path, merge_logits = _lanes(gm, (0, N_BRANCH * BRANCH_W, N_BRANCH * BRANCH_W + N_BRANCH * d))
    m = jnp.maximum(jnp.maximum(l0, l1), l2)
    e0, e1, e2 = jnp.exp(l0 - m), jnp.exp(l1 - m), jnp.exp(l2 - m)
    y_c = (e0 * oc0 + e1 * oc1 + e2 * oc2) / (e0 + e1 + e2)
    y = jnp.concatenate([_join_heads(o_a), _join_heads(o_b), y_c, o_d], axis=1)
    u = y * (gate_path * jax.nn.sigmoid(gate_path))
    gates = _lanes(merge_logits, tuple(range(0, N_BRANCH * d + 1, d)))
    us = _lanes(u, tuple(range(0, N_BRANCH * BRANCH_W + 1, BRANCH_W)))
    branch_w = _unstack(w_branch)
    out = None
    for nb in range(N_BRANCH):
        term = jax.nn.sigmoid(gates[nb]) * _bdot(us[nb], branch_w[nb])
        out = term if out is None else out + term
    return (out,)


def _mixer_calls(proj, prm, aux):
    s = proj.shape[0]
    tr, tm = _pick(s, (256, 128)), _pick(s, (128,))
    mla_cos, mla_sin, gqa_cos, gqa_sin, place_kr, mean_q, mean_k = aux
    wq = MLA_HEADS * MLA_QK
    mla = dict(
        steps=s // tr, rows=[(proj, _rows(tr, SMALL_W, MLA_BLK))],
        params=[prm["g_q"], prm["g_kv"], prm["w_q"], prm["w_k"], prm["w_v"]],
        aux=[(mla_cos, _rows(tr, wq)), (mla_sin, _rows(tr, wq)), (place_kr, _whole(place_kr.shape))],
        outs=[((MLA_HEADS, s, MLA_QK), _head_rows(MLA_HEADS, tr, MLA_QK))] * 2
        + [((MLA_HEADS, s, MLA_V), _head_rows(MLA_HEADS, tr, MLA_V))],
        window=((s, P_TOT), _rows(tr, SMALL_W, MLA_BLK), BF16))
    wg = GQA_HEADS * HEAD_DIM
    gqa = dict(
        steps=s // tr, rows=[(proj, _rows(tr, SMALL_W, GQA_BLK))], params=[prm["gq"], prm["gk"]],
        aux=[(gqa_cos, _rows(tr, wg)), (gqa_sin, _rows(tr, wg)), (mean_q, _whole(mean_q.shape)),
             (mean_k, _whole(mean_k.shape))],
        outs=[((GQA_HEADS, s, HEAD_DIM), _head_rows(GQA_HEADS, tr, HEAD_DIM))]
        + [((GQA_KV_HEADS, s, HEAD_DIM), _head_rows(GQA_KV_HEADS, tr, HEAD_DIM))] * 2,
        window=((s, P_TOT), _rows(tr, SMALL_W, GQA_BLK), BF16))
    merge = dict(steps=s // tm, tm=tm, window=((s, P_TOT), _rows(tm, GM_W, 0), BF16))
    return mla, gqa, merge


def _merge_rows(proj, o_a, o_b, ocs, lses, o_d, tm):
    h4 = _head_rows(4, tm, HEAD_DIM)
    return ([(proj, _rows(tm, GM_W, 0)), (o_a, h4), (o_b, h4)] + [(t, _rows(tm, QW)) for t in ocs + lses]
            + [(o_d, _rows(tm, QW))])


def _to_residues(t, dil):
    s, w = t.shape
    return t if dil == 1 else t.reshape(s // dil, dil, w).transpose(1, 0, 2).reshape(s, w)


def _from_residues(t, dil):
    s, w = t.shape
    return t if dil == 1 else t.reshape(dil, s // dil, w).transpose(1, 0, 2).reshape(s, w)


def _mixer_fwd(projs, prm, aux):
    proj = projs[0]
    s = proj.shape[0]
    mla, gqa, merge = _mixer_calls(proj, prm, aux)
    q_a, k_a, v_a = _fwd_call("prep_mla", _mla_tile, mla["steps"], mla["rows"], mla["params"], mla["aux"], mla["outs"])
    o_a, lse_a = _dense_fwd_call(q_a, k_a, v_a, MLA_QK ** -0.5, "mla")
    q_b, k_b, v_b = _fwd_call("prep_gqa", _gqa_tile, gqa["steps"], gqa["rows"], gqa["params"], gqa["aux"], gqa["outs"])
    grp = GQA_HEADS // GQA_KV_HEADS
    o_b, lse_b = _dense_fwd_call(q_b.reshape(GQA_KV_HEADS, grp * s, HEAD_DIM), k_b, v_b, HEAD_DIM ** -0.5, "gqa")
    scale = HEAD_DIM ** -0.5
    ocs, lses = [], []
    for gi, (_, dil) in enumerate(DIL_PATTERNS):
        o, lse = _band_fwd_call(projs[gi], DIL_BLK if gi == 0 else 0, prm["bias_dil"][gi], prm["no_sink"], dil, 1,
                                QW, scale, "dil%d" % gi)
        ocs.append(o)
        lses.append(lse)
    o_d, lse_d = _band_fwd_call(proj, WIN_BLK, prm["bias_win"], prm["sink"], 1, WIN_HEADS // WIN_KV_HEADS,
                                WIN_KV_HEADS * HEAD_DIM, scale, "win")
    dils = [dil for _, dil in DIL_PATTERNS]
    rows = _merge_rows(proj, o_a, o_b.reshape(GQA_HEADS, s, HEAD_DIM), [_from_residues(t, r) for t, r in zip(ocs, dils)],
                       [_from_residues(t, r) for t, r in zip(lses, dils)], o_d, merge["tm"])
    mix = _fwd_call("merge", _merge_tile, merge["steps"], rows, [prm["w_branch"]], [],
                    [((s, prm["w_branch"].shape[2]), _rows(merge["tm"], prm["w_branch"].shape[2]), BF16)])[0]
    return mix, (q_a, k_a, v_a, o_a, lse_a, q_b, k_b, v_b, o_b, lse_b, ocs, lses, o_d, lse_d)


def _mixer_bwd(projs, prm, aux, saved, dmix):
    proj = projs[0]
    s = proj.shape[0]
    q_a, k_a, v_a, o_a, lse_a, q_b, k_b, v_b, o_b, lse_b, ocs, lses, o_d, lse_d = saved
    dils = [dil for _, dil in DIL_PATTERNS]
    mla, gqa, merge = _mixer_calls(proj, prm, aux)
    tm, d_model = merge["tm"], prm["w_branch"].shape[2]
    grp = GQA_HEADS // GQA_KV_HEADS
    scale = HEAD_DIM ** -0.5

    rows = _merge_rows(proj, o_a, o_b.reshape(GQA_HEADS, s, HEAD_DIM), [_from_residues(t, r) for t, r in zip(ocs, dils)],
                       [_from_residues(t, r) for t, r in zip(lses, dils)], o_d, tm)
    grads, (dw_branch,) = _vjp_call(
        "merge", _merge_tile, merge["steps"], rows, [prm["w_branch"]], [], [(dmix, _rows(tm, d_model))],
        [merge["window"]] + [(a.shape, spec) for a, spec in rows[1:]])
    dproj, do_a, do_b, docs, dlses, do_d = grads[0], grads[1], grads[2], grads[3:6], grads[6:9], grads[9]

    dq_a, dk_a, dv_a = _dense_bwd_call(q_a, k_a, v_a, o_a, lse_a, do_a, MLA_QK ** -0.5, "mla")
    (dproj,), dmla = _vjp_call("prep_mla", _mla_tile, mla["steps"], mla["rows"], mla["params"], mla["aux"],
                               [(t, spec) for t, (_, spec) in zip((dq_a, dk_a, dv_a), mla["outs"])],
                               [mla["window"]], into=dproj)
    dq_b, dk_b, dv_b = _dense_bwd_call(q_b.reshape(GQA_KV_HEADS, grp * s, HEAD_DIM), k_b, v_b, o_b, lse_b,
                                       do_b.reshape(GQA_KV_HEADS, grp * s, HEAD_DIM), scale, "gqa")
    (dproj,), dgqa = _vjp_call("prep_gqa", _gqa_tile, gqa["steps"], gqa["rows"], gqa["params"], gqa["aux"],
                               [(t, spec) for t, (_, spec) in zip((dq_b.reshape(GQA_HEADS, s, HEAD_DIM), dk_b, dv_b),
                                                                  gqa["outs"])],
                               [gqa["window"]], into=dproj)
    dproj, dbias_win, dsink = _band_bwd_call(proj, o_d, do_d, lse_d, None, prm["bias_win"], prm["sink"], dproj,
                                             WIN_BLK, 1, WIN_HEADS // WIN_KV_HEADS, WIN_KV_HEADS * HEAD_DIM, scale, "win")
    dbias_dil, dprojs = [], []
    for gi, dil in enumerate(dils):
        dside, dbias, _ = _band_bwd_call(
            projs[gi], ocs[gi], _to_residues(docs[gi], dil), lses[gi], _to_residues(dlses[gi], dil),
            prm["bias_dil"][gi], prm["no_sink"], dproj if gi == 0 else None, DIL_BLK if gi == 0 else 0, dil, 1, QW,
            scale, "dil%d" % gi)
        if gi == 0:
            dproj = dside
        else:
            dprojs.append(dside)
        dbias_dil.append(dbias)
    dprm = dict(g_q=dmla[0], g_kv=dmla[1], w_q=dmla[2], w_k=dmla[3], w_v=dmla[4], gq=dgqa[0], gk=dgqa[1],
                bias_dil=dbias_dil, bias_win=dbias_win, sink=dsink, no_sink=jnp.zeros_like(prm["no_sink"]),
                w_branch=dw_branch)
    return [dproj] + dprojs, {k: jax.tree.map(lambda g, p: g.astype(p.dtype), v, prm[k]) for k, v in dprm.items()}


def _layer_projs(xn, w):
    dils = [dil for _, dil in DIL_PATTERNS]
    xns = [xn] + [_to_residues(xn, r) for r in dils[1:]]
    return xns, [_mm(a, b, "nt", "proj%d_fwd" % i) for i, (a, b) in enumerate(zip(xns, w["w_in_t"]))]


def _layer_fwd(x, w, aux):
    s, d = x.shape
    tr = _pick(s, (256, 128, 64, 32, 16, 8))
    def with_transpose(r, p, a):
        y = _rms(r[0], p[0])
        return y, y.T

    xn, xn_t = _fwd_call("norm", with_transpose, s // tr, [(x, _rows(tr, d))], [w["norm_g"]], [],
                         [((s, d), _rows(tr, d), BF16), ((d, s), pl.BlockSpec((d, tr), lambda i: (0, i)), BF16)])
    xns, projs = _layer_projs(xn, w)
    mix, saved = _mixer_fwd(projs, w["mixer"], aux)
    return _mm(mix, w["w_out"], "nn", "out_proj_nn"), (x, w, aux, xns, xn_t, projs, mix, saved)


@jax.custom_vjp
def _layer_core(x, w, aux):
    return _layer_fwd(x, w, aux)[0]


def _layer_core_bwd(res, dout):
    x, w, aux, xns, xn_t, projs, mix, saved = res
    s, d = x.shape
    tr = _pick(s, (256, 128, 64, 32, 16, 8))
    dils = [dil for _, dil in DIL_PATTERNS]
    dmix = _mm(dout, w["w_out"], "nt", "out_proj_nt")
    dw_out = _mm(mix, dout, "tn", "out_proj_tn", w["w_out"].dtype)
    dprojs, dmixer = _mixer_bwd(projs, w["mixer"], aux, saved, dmix)
    dxn = None
    for i, (dp, wi, r) in enumerate(zip(dprojs, w["w_in_t"], dils)):
        part = _mm(_from_residues(dp, r), wi, "nn", "proj%d_dx" % i)
        dxn = part if dxn is None else dxn + part
    dw_in_t = [_mm(xn_t, dprojs[0], "nn", "proj0_dw", w["w_in_t"][0].dtype).T]
    dw_in_t += [_mm(dp, a, "tn", "proj%d_dw" % i, wi.dtype)
                for i, (a, dp, wi) in list(enumerate(zip(xns, dprojs, w["w_in_t"])))[1:]]
    (dx,), (dg,) = _vjp_call("norm", _norm_tile, s // tr, [(x, _rows(tr, d))], [w["norm_g"]], [],
                             [(dxn, _rows(tr, d))], [((s, d), _rows(tr, d))])
    dw = dict(norm_g=dg, w_in_t=dw_in_t, mixer=dmixer, w_out=dw_out)
    return dx, dw, tuple(jnp.zeros_like(t) for t in aux)


_layer_core.defvjp(lambda x, w, aux: _layer_fwd(x, w, aux), _layer_core_bwd)


def _rope_angles(pos, dim):
    inv = ROPE_THETA ** (-jnp.arange(0, dim, 2, dtype=F32) / dim)
    return pos.astype(F32)[:, None] * inv[None, :]


def _rope_tables(s):
    pos = jnp.arange(s, dtype=jnp.int32)
    rows = s // GRID_W
    row_idx = jnp.repeat(jnp.arange(rows, dtype=jnp.int32), GRID_W)
    col_idx = jnp.tile(jnp.arange(GRID_W, dtype=jnp.int32), rows)
    a1 = _rope_angles(pos, MLA_ROPE)
    ar = _rope_angles(row_idx, HEAD_DIM // 2)
    ac = _rope_angles(col_idx, HEAD_DIM // 2)
    ones, zeros = jnp.ones((s, MLA_NOPE), F32), jnp.zeros((s, MLA_NOPE), F32)
    mla_cos = jnp.tile(jnp.concatenate([ones, jnp.cos(a1), jnp.cos(a1)], axis=1), (1, MLA_HEADS))
    mla_sin = jnp.tile(jnp.concatenate([zeros, -jnp.sin(a1), jnp.sin(a1)], axis=1), (1, MLA_HEADS))
    gqa_cos = jnp.tile(jnp.concatenate([jnp.cos(ar), jnp.cos(ar), jnp.cos(ac), jnp.cos(ac)], axis=1), (1, GQA_HEADS))
    gqa_sin = jnp.tile(jnp.concatenate([-jnp.sin(ar), jnp.sin(ar), -jnp.sin(ac), jnp.sin(ac)], axis=1), (1, GQA_HEADS))
    return mla_cos, mla_sin, gqa_cos, gqa_sin


def _t5_bucket(rel):
    nb = T5_BUCKETS // 2
    max_exact = nb // 2
    n = jnp.abs(rel)
    nf = jnp.maximum(n, 1).astype(F32)
    large = max_exact + (jnp.log(nf / max_exact) / math.log(T5_MAX_DIST / max_exact) * (nb - max_exact)).astype(jnp.int32)
    large = jnp.minimum(large, nb - 1)
    return jnp.where(rel > 0, nb, 0) + jnp.where(n < max_exact, n, large)


def _band_bias(table, stride, head_lo, heads, half_window):
    b = BAND_BLOCK
    offs = jnp.arange(3 * b)[None, :] - b - jnp.arange(b)[:, None]
    one_hot = (_t5_bucket(offs * stride)[..., None] == jnp.arange(T5_BUCKETS)).astype(F32)
    bias = jnp.dot(one_hot.reshape(b * 3 * b, T5_BUCKETS), table[:, head_lo:head_lo + heads],
                   precision=lax.Precision.HIGHEST)
    bias = bias.T.reshape(heads, b, 3 * b)
    return jnp.where((jnp.abs(offs) <= half_window)[None], bias, NEG_INF)


def _w_in_rows(d):
    mla, gqa, win, dil0 = MLA_BLK * SMALL_W, GQA_BLK * SMALL_W, WIN_BLK * BAND_W, DIL_BLK * BAND_W
    plan, at = [], 0
    for width, target, row in ((256, 0, mla), (128, 0, mla + 256), (32, 0, mla + 384),
                               (256, 0, gqa), (128, 0, gqa + 256), (128, 0, gqa + 384)):
        plan.append((at, width, target, row))
        at += width
    for part in range(3):
        for g in range(len(DIL_PATTERNS)):
            plan.append((at, QW, g, (dil0 if g == 0 else 0) + part * QW))
            at += QW
    for width, row in ((256, win), (128, win + 256), (128, win + 384), (N_BRANCH * BRANCH_W, 0),
                       (N_BRANCH * d, N_BRANCH * BRANCH_W)):
        plan.append((at, width, 0, row))
        at += width
    return plan


@jax.custom_vjp
def _w_in_layout(w_in_t):
    d = w_in_t.shape[1]
    outs = []
    for target, rows in enumerate((P_TOT, BAND_W, BAND_W)):
        parts, at = [], 0
        for start, width, _, row in sorted((p for p in _w_in_rows(d) if p[2] == target), key=lambda p: p[3]):
            if row > at:
                parts.append(jnp.zeros((row - at, d), w_in_t.dtype))
            parts.append(w_in_t[start:start + width])
            at = row + width
        if at < rows:
            parts.append(jnp.zeros((rows - at, d), w_in_t.dtype))
        outs.append(jnp.concatenate(parts, axis=0))
    return outs


def _w_in_layout_fwd(w_in_t):
    return _w_in_layout(w_in_t), None


def _w_in_layout_bwd(_, cts):
    d = cts[0].shape[1]
    return (jnp.concatenate([cts[target][row:row + width] for _, width, target, row in _w_in_rows(d)], axis=0),)


_w_in_layout.defvjp(_w_in_layout_fwd, _w_in_layout_bwd)


def _layer(x, w, l, aux, biases):
    w_kv = w["w_kv_t"][l].T.reshape(MLA_KV_LORA, MLA_HEADS, MLA_NOPE + MLA_V)
    w_k = jnp.concatenate([w_kv[:, :, :MLA_NOPE], jnp.zeros((MLA_KV_LORA, MLA_HEADS, MLA_ROPE), w_kv.dtype)], axis=2)
    dil_bias, win_bias = biases
    prm = dict(
        g_q=w["mla_q_norm_g"][l][None, :], g_kv=w["mla_kv_norm_g"][l][None, :], w_q=w["w_q_t"][l].T,
        w_k=w_k.reshape(MLA_KV_LORA, MLA_HEADS * MLA_QK),
        w_v=w_kv[:, :, MLA_NOPE:].reshape(MLA_KV_LORA, MLA_HEADS * MLA_V),
        gq=jnp.tile(w["gqa_q_norm_g"][l], GQA_HEADS)[None, :], gk=jnp.tile(w["gqa_k_norm_g"][l], GQA_KV_HEADS)[None, :],
        bias_dil=list(dil_bias), bias_win=win_bias, sink=w["win_sink"][l].reshape(WIN_HEADS, 1, 1),
        no_sink=jnp.full((DIL_HEADS, 1, 1), NEG_INF, F32), w_branch=jnp.transpose(w["w_branch_t"][l].reshape(-1, N_BRANCH, BRANCH_W), (1, 2, 0)))
    layer_w = dict(norm_g=w["norm_g"][l][None, :], w_in_t=_w_in_layout(w["w_in_t"][l]), mixer=prm, w_out=w["w_out"][l])
    return x + _layer_core(x, layer_w, aux)


def _local_loss(w, x, target):
    s, d_model = x.shape
    assert d_model == D_MODEL, "the projection's window layout is laid out for d_model 1024"
    place = np.zeros((MLA_ROPE, MLA_HEADS * MLA_QK), np.float32)
    for h in range(MLA_HEADS):
        for i in range(MLA_ROPE):
            place[i, h * MLA_QK + MLA_NOPE + i] = 1.0

    def head_mean(nh):
        m = np.kron(np.eye(nh, dtype=np.float32), np.full((HEAD_DIM, HEAD_DIM), 1.0 / HEAD_DIM, np.float32))
        return jnp.asarray(m)

    aux = _rope_tables(s) + (jnp.asarray(place), head_mean(GQA_HEADS), head_mean(GQA_KV_HEADS))
    table = w["t5_table"]
    dil_bias = [_band_bias(table, dil, gi * DIL_HEADS, DIL_HEADS, window // (2 * dil))
                for gi, (window, dil) in enumerate(DIL_PATTERNS)]
    win_bias = _band_bias(table, 1, len(DIL_PATTERNS) * DIL_HEADS, WIN_HEADS, WIN_HALF)
    for l in range(w["norm_g"].shape[0]):
        x = _layer(x, w, l, aux, (dil_bias, win_bias))
    return _loss_op(x, target, w["final_norm_g"][None, :])


_ANY = pl.BlockSpec(memory_space=pl.ANY)
_MESH = pl.DeviceIdType.MESH


def _all_gather(block, name):
    def body(x_ref, out_ref, send_sems, recv_sems, local_sem):
        x, y, c = lax.axis_index("x"), lax.axis_index("y"), lax.axis_index("c")
        me, sibling = (x, y, c), (x, y, 1 - c)
        chips = [(1 - x, y), (x, 1 - y), (1 - x, 1 - y)]

        def slot(px, py, pc):
            return out_ref.at[4 * px + 2 * py + pc]

        def copy(k, blk, to, src=None):
            return pltpu.make_async_remote_copy(
                src_ref=slot(*blk) if src is None else src, dst_ref=slot(*blk),
                send_sem=send_sems.at[k], recv_sem=recv_sems.at[k], device_id=to, device_id_type=_MESH)

        mine = pltpu.make_async_copy(x_ref, slot(*me), local_sem)
        mine.start()
        first = [copy(0, me, sibling, src=x_ref)]
        first += [copy(1 + j, me, (*chip, c), src=x_ref) for j, chip in enumerate(chips)]
        for cp in first:
            cp.start()
        passed = [copy(4 + j, (*chip, c), sibling) for j, chip in enumerate(chips)]
        for j, chip in enumerate(chips):
            copy(1 + j, (*chip, c), me).wait_recv()
            passed[j].start()
        copy(0, sibling, me).wait_recv()
        for j, chip in enumerate(chips):
            copy(4 + j, (*chip, 1 - c), me).wait_recv()
        for cp in first + passed:
            cp.wait_send()
        mine.wait()

    return pl.pallas_call(
        body,
        out_shape=jax.ShapeDtypeStruct((N_DEV,) + block.shape, block.dtype),
        in_specs=[_ANY],
        out_specs=_ANY,
        scratch_shapes=[pltpu.SemaphoreType.DMA((7,)), pltpu.SemaphoreType.DMA((7,)), pltpu.SemaphoreType.DMA],
        name=name,
    )(block)


def _swap_with_sibling(blocks, name):
    def body(x_ref, out_ref, send_sem, recv_sem):
        x, y, c = lax.axis_index("x"), lax.axis_index("y"), lax.axis_index("c")
        copy = pltpu.make_async_remote_copy(src_ref=x_ref.at[1 - c], dst_ref=out_ref, send_sem=send_sem,
                                            recv_sem=recv_sem, device_id=(x, y, 1 - c), device_id_type=_MESH)
        copy.start()
        copy.wait()

    return pl.pallas_call(
        body,
        out_shape=jax.ShapeDtypeStruct(blocks.shape[1:], blocks.dtype),
        in_specs=[_ANY],
        out_specs=_ANY,
        scratch_shapes=[pltpu.SemaphoreType.DMA, pltpu.SemaphoreType.DMA],
        name=name,
    )(blocks)


def _add_sibling(blocks, theirs, name):
    _, chips, rows, w = blocks.shape
    tr = _pick(rows, (1024, 512, 256, 128, 64, 32, 16))

    def body(b_ref, t_ref, o_ref):
        mine = b_ref[lax.axis_index("c"), 0]
        o_ref[0] = (mine.astype(F32) + t_ref[0].astype(F32)).astype(o_ref.dtype)

    return pl.pallas_call(
        body,
        grid=(chips, rows // tr),
        in_specs=[pl.BlockSpec((2, 1, tr, w), lambda k, i: (0, k, i, 0)), pl.BlockSpec((1, tr, w), lambda k, i: (k, i, 0))],
        out_specs=pl.BlockSpec((1, tr, w), lambda k, i: (k, i, 0)),
        out_shape=jax.ShapeDtypeStruct(theirs.shape, theirs.dtype),
        name=name,
        compiler_params=_params("parallel", "parallel"),
    )(blocks, theirs)


def _exchange_chips(partials, name):
    n_chips = partials.shape[0]

    def body(x_ref, out_ref, send_sems, recv_sems, local_sem):
        x, y, c = lax.axis_index("x"), lax.axis_index("y"), lax.axis_index("c")
        me = 2 * x + y
        mine = pltpu.make_async_copy(x_ref.at[me], out_ref.at[me], local_sem)
        mine.start()
        copies, landed = [], []
        for k in range(1, n_chips):
            px = 1 - x if k & 2 else x
            py = 1 - y if k & 1 else y
            peer = 2 * px + py
            copies.append(pltpu.make_async_remote_copy(
                src_ref=x_ref.at[peer], dst_ref=out_ref.at[me], send_sem=send_sems.at[k - 1],
                recv_sem=recv_sems.at[k - 1], device_id=(px, py, c), device_id_type=_MESH))
            landed.append(pltpu.make_async_remote_copy(
                src_ref=x_ref.at[peer], dst_ref=out_ref.at[peer], send_sem=send_sems.at[k - 1],
                recv_sem=recv_sems.at[k - 1], device_id=(px, py, c), device_id_type=_MESH))
        for cp in copies:
            cp.start()
        for cp in landed:
            cp.wait_recv()
        for cp in copies:
            cp.wait_send()
        mine.wait()

    return pl.pallas_call(
        body,
        out_shape=jax.ShapeDtypeStruct(partials.shape, partials.dtype),
        in_specs=[_ANY],
        out_specs=_ANY,
        scratch_shapes=[pltpu.SemaphoreType.DMA((n_chips - 1,)), pltpu.SemaphoreType.DMA((n_chips - 1,)),
                        pltpu.SemaphoreType.DMA],
        name=name,
    )(partials)


def _sum_slots(parts, name):
    slots, rows, w = parts.shape
    tr = _pick(rows, (1024, 512, 256, 128, 64, 32, 16, 8))

    def body(p_ref, o_ref):
        acc = p_ref[0].astype(F32)
        for j in range(1, slots):
            acc = acc + p_ref[j].astype(F32)
        o_ref[...] = acc

    return pl.pallas_call(
        body,
        grid=(rows // tr,),
        in_specs=[pl.BlockSpec((slots, tr, w), lambda i: (0, i, 0))],
        out_specs=pl.BlockSpec((tr, w), lambda i: (i, 0)),
        out_shape=jax.ShapeDtypeStruct((rows, w), F32),
        name=name,
        compiler_params=_params("parallel"),
    )(parts)


def _adamw(w, g, m, v, name):
    rows, width = w.shape
    tr = _pick(rows, (1024, 512, 256, 128, 64, 32, 16, 8))

    def body(w_ref, g_ref, m_ref, v_ref, d_ref, nm_ref, nv_ref):
        g_ = g_ref[...]
        m_ = ADAM_B1 * m_ref[...] + (1.0 - ADAM_B1) * g_
        v_ = ADAM_B2 * v_ref[...] + (1.0 - ADAM_B2) * jnp.square(g_)
        m_hat = m_ / (1.0 - ADAM_B1 ** ADAM_STEP)
        v_hat = v_ / (1.0 - ADAM_B2 ** ADAM_STEP)
        d_ref[...] = -ADAM_LR * (m_hat / (jnp.sqrt(v_hat) + ADAM_EPS) + ADAM_WD * w_ref[...])
        nm_ref[...] = m_
        nv_ref[...] = v_

    spec = pl.BlockSpec((tr, width), lambda i: (i, 0))
    return pl.pallas_call(
        body,
        grid=(rows // tr,),
        in_specs=[spec] * 4,
        out_specs=[spec] * 3,
        out_shape=[jax.ShapeDtypeStruct((rows, width), F32)] * 3,
        name=name,
        compiler_params=_params("parallel"),
    )(w, g, m, v)


_SHARDED = (("w_in", 2), ("w_mla_q_up", 2), ("w_mla_kv_up", 2), ("w_branch", 3), ("w_out", 1))
_REPLICATED = ("norm_g", "mla_q_norm_g", "mla_kv_norm_g", "gqa_q_norm_g", "gqa_k_norm_g", "win_sink", "t5_table",
               "final_norm_g")


def _pack(arrays, row_multiple):
    flat = jnp.concatenate([a.reshape(-1) for a in arrays])
    rows = -(-flat.shape[0] // (LANES * row_multiple)) * row_multiple
    return jnp.pad(flat, (0, rows * LANES - flat.shape[0])).reshape(rows, LANES)


def _unpack(packed, shapes):
    flat, out, at = packed.reshape(-1), [], 0
    for shp in shapes:
        n = int(np.prod(shp))
        out.append(flat[at:at + n].reshape(shp))
        at += n
    return out


_TO_WIRE = {
    "w_in": lambda t: jnp.swapaxes(t, 1, 2), "w_mla_q_up": lambda t: jnp.swapaxes(t, 1, 2),
    "w_mla_kv_up": lambda t: jnp.swapaxes(t, 1, 2),
    "w_branch": lambda t: jnp.transpose(t, (0, 3, 1, 2)).reshape(t.shape[0], t.shape[3], -1), "w_out": lambda t: t}
_FROM_WIRE = {
    "w_in": lambda t, shp: jnp.swapaxes(t, 1, 2), "w_mla_q_up": lambda t, shp: jnp.swapaxes(t, 1, 2),
    "w_mla_kv_up": lambda t, shp: jnp.swapaxes(t, 1, 2),
    "w_branch": lambda t, shp: jnp.transpose(t.reshape(shp[0], shp[3], shp[1], shp[2]), (0, 2, 3, 1)),
    "w_out": lambda t, shp: t}
_WIRE_NAME = {"w_in": "w_in_t", "w_mla_q_up": "w_q_t", "w_mla_kv_up": "w_kv_t", "w_branch": "w_branch_t",
              "w_out": "w_out"}


def _join_shards(gathered, wire_shapes):
    out, at = [], 0
    for depth, cut, rest in wire_shapes:
        n = depth * cut * rest // LANES
        blk = gathered[:, at:at + n].reshape(N_DEV, depth, cut, rest)
        out.append(jnp.moveaxis(blk, 0, 1).reshape(depth, N_DEV * cut, rest))
        at += n
    return out


def _split_shards(fulls, wire_shapes):
    parts = []
    for full, (depth, cut, rest) in zip(fulls, wire_shapes):
        blk = jnp.transpose(full.reshape(depth, N_DEV // 2, 2, cut, rest), (2, 1, 0, 3, 4))
        parts.append(blk.reshape(2, N_DEV // 2, depth * cut * rest // LANES, LANES))
    return jnp.concatenate(parts, axis=2)


def kernel(x, norm_g, w_in, mla_q_norm_g, mla_kv_norm_g, w_mla_q_up, w_mla_kv_up, gqa_q_norm_g, gqa_k_norm_g, win_sink, t5_table, w_branch, w_out, final_norm_g, loss_target, m_norm_g, m_w_in, m_mla_q_norm_g, m_mla_kv_norm_g, m_w_mla_q_up, m_w_mla_kv_up, m_gqa_q_norm_g, m_gqa_k_norm_g, m_win_sink, m_t5_table, m_w_branch, m_w_out, m_final_norm_g, v_norm_g, v_w_in, v_mla_q_norm_g, v_mla_kv_norm_g, v_w_mla_q_up, v_w_mla_kv_up, v_gqa_q_norm_g, v_gqa_k_norm_g, v_win_sink, v_t5_table, v_w_branch, v_w_out, v_final_norm_g):
    given = dict(locals())
    names = ("norm_g", "w_in", "mla_q_norm_g", "mla_kv_norm_g", "w_mla_q_up", "w_mla_kv_up", "gqa_q_norm_g",
             "gqa_k_norm_g", "win_sink", "t5_table", "w_branch", "w_out", "final_norm_g")
    shard_names = [n for n, _ in _SHARDED]
    shard_shapes = [given[n].shape for n in shard_names]

    wire = [_TO_WIRE[n](given[n]).astype(BF16) for n in shard_names]
    wire_shapes = [t.shape for t in wire]
    gathered = _all_gather(jnp.concatenate([t.reshape(-1, LANES) for t in wire]), "gather_weights")
    weights = {n: given[n] for n in _REPLICATED}
    weights.update(zip([_WIRE_NAME[n] for n in shard_names], _join_shards(gathered, wire_shapes)))

    loss, (gw, gx) = jax.value_and_grad(_local_loss, argnums=(0, 1))(weights, x[0], loss_target[0])
    loss = lax.psum(loss, ("x", "y", "c"))

    send = _split_shards([gw[_WIRE_NAME[n]] for n in shard_names], wire_shapes)
    partials = _add_sibling(send, _swap_with_sibling(send, "swap_grads"), "add_sibling_grads")
    g_wire = _unpack(_sum_slots(_exchange_chips(partials, "scatter_grads"), "sum_grads"), wire_shapes)
    g_shard = [_FROM_WIRE[n](t, shp) for n, t, shp in zip(shard_names, g_wire, shard_shapes)]
    rep_shapes = [given[n].shape for n in _REPLICATED]
    g_rep = _unpack(_sum_slots(_all_gather(_pack([gw[n] for n in _REPLICATED], 8), "gather_small_grads"),
                               "sum_small_grads"), rep_shapes)
    grads = dict(zip(shard_names, g_shard))
    grads.update(zip(_REPLICATED, g_rep))

    def update(group, shapes, row_multiple, name):
        outs = _adamw(*[_pack([src[n] for n in group], row_multiple) for src in (
            given, grads, {n: given["m_" + n] for n in group}, {n: given["v_" + n] for n in group})], name)
        return [dict(zip(group, _unpack(o, shapes))) for o in outs]

    big = update(shard_names, shard_shapes, 16, "adamw_shards")
    small = update(list(_REPLICATED), rep_shapes, 8, "adamw_replicated")
    delta, new_m, new_v = [{**b, **s_} for b, s_ in zip(big, small)]
    return (loss, gx[None], *[grads[n] for n in names], *[delta[n] for n in names],
            *[new_m[n] for n in names], *[new_v[n] for n in names])
```

```python
import functools
import math

import jax
import jax.numpy as jnp
import numpy as np
from jax import lax
from jax.experimental import pallas as pl
from jax.experimental.pallas import tpu as pltpu

F32 = jnp.float32
BF16 = jnp.bfloat16
N_DEV = 8
LANES = 128
HALF = LANES // 2
V7X_VMEM_LIMIT = 56 * 1024 * 1024

EPS = 1e-6
NEG_INF = -1e30
LOG2E = 1.4426950408889634
ROPE_THETA = 10000.0
GRID_W = 64
HEAD_DIM = 64
N_BRANCH = 4
BRANCH_W = 256
MLA_HEADS, MLA_Q_LORA, MLA_KV_LORA, MLA_NOPE, MLA_ROPE, MLA_V = 4, 256, 128, 64, 32, 64
MLA_QK = MLA_NOPE + MLA_ROPE
GQA_HEADS, GQA_KV_HEADS = 4, 2
DIL_PATTERNS = ((128, 1), (512, 4), (2048, 16))
DIL_HEADS = 4
WIN_HEADS, WIN_KV_HEADS, WIN_HALF = 4, 2, 128
T5_BUCKETS, T5_MAX_DIST = 32, 1024
BAND_BLOCK = 128
ADAM_LR, ADAM_B1, ADAM_B2, ADAM_EPS, ADAM_WD, ADAM_STEP = 0.001, 0.9, 0.999, 1e-08, 0.01, 10

D_MODEL = 1024
GM_W, SMALL_W, BAND_W = 5120, 512, 768
MLA_BLK, GQA_BLK, WIN_BLK, DIL_BLK = 10, 11, 8, 9
P_TOT = 7680
QW = 256


def _params(*sem):
    return pltpu.CompilerParams(dimension_semantics=sem, vmem_limit_bytes=V7X_VMEM_LIMIT)


def _pick(n, cands):
    for c in cands:
        if n % c == 0:
            return c
    return n


def _row_tile(rows, unit, cap):
    best = unit
    for t in range(unit, min(rows, cap) + 1, unit):
        if rows % t == 0:
            best = t
    assert rows % best == 0
    return best


def _dot(a, b, ca, cb):
    return lax.dot_general(a.astype(BF16), b.astype(BF16), (((ca,), (cb,)), ((), ())), preferred_element_type=F32)


def _bmm(a, b, ca, cb):
    return lax.dot_general(a, b, (((ca,), (cb,)), ((0,), (0,))), preferred_element_type=F32)


@jax.custom_vjp
def _bdot(a, b):
    return _dot(a, b, 1, 0)


def _bdot_fwd(a, b):
    return _dot(a, b, 1, 0), (a, b)


def _bdot_bwd(res, g):
    a, b = res
    return _dot(g, b, 1, 1), _dot(a, g, 0, 0)


_bdot.defvjp(_bdot_fwd, _bdot_bwd)


def _hdot(a, c):
    return lax.dot_general(a, c, (((1,), (0,)), ((), ())), precision=lax.Precision.HIGHEST, preferred_element_type=F32)


@functools.partial(jax.custom_vjp, nondiff_argnums=(1,))
def _lane_roll(x, shift):
    return pltpu.roll(x, shift, 1)


def _lane_roll_fwd(x, shift):
    return pltpu.roll(x, shift, 1), None


def _lane_roll_bwd(shift, _, g):
    return (pltpu.roll(g, g.shape[1] - shift, 1),)


_lane_roll.defvjp(_lane_roll_fwd, _lane_roll_bwd)


@functools.partial(jax.custom_vjp, nondiff_argnums=(1,))
def _lane_ranges(x, cut):
    bounds, _ = cut
    return tuple(x[:, lo:hi] for lo, hi in zip(bounds[:-1], bounds[1:]))


def _lane_ranges_fwd(x, cut):
    return _lane_ranges(x, cut), None


def _lane_ranges_bwd(cut, _, cts):
    bounds, width = cut
    parts = list(cts)
    if bounds[-1] < width:
        parts.append(jnp.zeros((cts[0].shape[0], width - bounds[-1]), cts[0].dtype))
    return (jnp.concatenate(parts, axis=1),)


_lane_ranges.defvjp(_lane_ranges_fwd, _lane_ranges_bwd)


def _lanes(x, bounds):
    return _lane_ranges(x, (tuple(bounds), x.shape[1]))


@jax.custom_vjp
def _unstack(x):
    return tuple(x[i] for i in range(x.shape[0]))


def _unstack_fwd(x):
    return _unstack(x), None


def _unstack_bwd(_, cts):
    return (jnp.stack(cts, axis=0),)


_unstack.defvjp(_unstack_fwd, _unstack_bwd)


@functools.partial(jax.custom_vjp, nondiff_argnums=(1,))
def _split_heads(x, h):
    d = x.shape[1] // h
    return jnp.stack([x[:, i * d:(i + 1) * d] for i in range(h)], axis=0)


def _split_heads_fwd(x, h):
    return _split_heads(x, h), None


def _split_heads_bwd(h, _, ct):
    return (jnp.concatenate([ct[i] for i in range(h)], axis=1),)


_split_heads.defvjp(_split_heads_fwd, _split_heads_bwd)


def _join_heads(x):
    return jnp.concatenate(_unstack(x), axis=1)


def _rope(x, cos_t, sin_t, half):
    w = x.shape[1]
    lane = lax.broadcasted_iota(jnp.int32, (1, w), 1)
    first = (lane % (2 * half)) < half
    partner = jnp.where(first, _lane_roll(x, w - half), _lane_roll(x, half))
    return x * cos_t + partner * sin_t


def _rms(x, g):
    return x * lax.rsqrt(jnp.mean(x * x, axis=-1, keepdims=True) + EPS) * g


def _rows(tr, w, col=0):
    return pl.BlockSpec((tr, w), lambda i: (i, col))


def _head_rows(h, tr, d):
    return pl.BlockSpec((h, tr, d), lambda i: (0, i, 0))


def _whole(shape):
    nd = len(shape)
    return pl.BlockSpec(tuple(shape), lambda i: (0,) * nd)


def _fwd_call(name, fn, steps, rows, params, aux, outs):
    nr, npar, na = len(rows), len(params), len(aux)

    def body(*refs):
        vals = [x[...].astype(F32) for x in refs[:nr + npar + na]]
        res = fn(vals[:nr], vals[nr:nr + npar], vals[nr + npar:])
        for o_ref, o in zip(refs[nr + npar + na:], res):
            o_ref[...] = o.astype(o_ref.dtype)

    return pl.pallas_call(
        body,
        grid=(steps,),
        in_specs=[s for _, s in rows] + [_whole(p.shape) for p in params] + [s for _, s in aux],
        out_specs=[e[1] for e in outs],
        out_shape=[jax.ShapeDtypeStruct(e[0], e[2] if len(e) > 2 else F32) for e in outs],
        name=name + "_fwd",
        compiler_params=_params("parallel"),
    )(*[a for a, _ in rows], *params, *[a for a, _ in aux])


def _vjp_call(name, fn, steps, rows, params, aux, cts, row_grads, into=None):
    nr, npar, na, nc = len(rows), len(params), len(aux), len(cts)
    n_in = nr + npar + na + nc
    lead = 0 if into is None else 1

    def body(*refs):
        refs = refs[lead:]
        vals = [x[...].astype(F32) for x in refs[:n_in]]
        r, p, a, d = vals[:nr], vals[nr:nr + npar], vals[nr + npar:nr + npar + na], vals[nr + npar + na:]
        out_refs = refs[n_in:]
        _, vjp = jax.vjp(lambda r_, p_: tuple(fn(r_, p_, a)), r, p)
        dr, dp = vjp(tuple(d))
        for o_ref, o in zip(out_refs[:nr], dr):
            o_ref[...] = o.astype(o_ref.dtype)

        @pl.when(pl.program_id(0) == 0)
        def _():
            for o_ref in out_refs[nr:]:
                o_ref[...] = jnp.zeros_like(o_ref)

        for o_ref, o in zip(out_refs[nr:], dp):
            o_ref[...] += o

    outs = pl.pallas_call(
        body,
        grid=(steps,),
        in_specs=([] if into is None else [pl.BlockSpec(memory_space=pl.ANY)])
        + [s for _, s in rows] + [_whole(p.shape) for p in params] + [s for _, s in aux] + [s for _, s in cts],
        out_specs=[e[1] for e in row_grads] + [_whole(p.shape) for p in params],
        out_shape=[jax.ShapeDtypeStruct(e[0], e[2] if len(e) > 2 else F32) for e in row_grads]
        + [jax.ShapeDtypeStruct(p.shape, F32) for p in params],
        input_output_aliases={} if into is None else {0: 0},
        name=name + "_bwd",
        compiler_params=_params("arbitrary"),
    )(*([] if into is None else [into]), *[a for a, _ in rows], *params, *[a for a, _ in aux], *[a for a, _ in cts])
    return list(outs[:nr]), list(outs[nr:])


def _norm_tile(r, p, a):
    return (_rms(r[0], p[0]),)


def _mm(a, b, mode, name, out_dtype=F32):
    if mode == "nn":
        (m, k), n = a.shape, b.shape[1]
    elif mode == "nt":
        (m, k), n = a.shape, b.shape[0]
    else:
        (k, m), n = a.shape, b.shape[1]
    tm = _pick(m, (1024, 512, 256, 128))
    tn = _pick(n, (1024, 768, 512, 384, 256, 128))
    tk = _pick(k, (1024, 768, 512, 384, 256, 128))
    if mode == "tn":
        tk = _pick(k, (512, 256, 128))
    nk = k // tk

    def body(a_ref, b_ref, o_ref, acc_ref):
        kk = pl.program_id(2)
        if mode == "nn":
            part = _dot(a_ref[...], b_ref[...], 1, 0)
        elif mode == "nt":
            part = _dot(a_ref[...], b_ref[...], 1, 1)
        else:
            part = _dot(a_ref[...], b_ref[...], 0, 0)
        if nk == 1:
            o_ref[...] = part.astype(o_ref.dtype)
        else:
            @pl.when(kk == 0)
            def _():
                acc_ref[...] = part

            @pl.when(kk > 0)
            def _():
                acc_ref[...] += part

            @pl.when(kk == nk - 1)
            def _():
                o_ref[...] = acc_ref[...].astype(o_ref.dtype)

    if mode == "nn":
        a_spec = pl.BlockSpec((tm, tk), lambda i, j, kk: (i, kk))
        b_spec = pl.BlockSpec((tk, tn), lambda i, j, kk: (kk, j))
    elif mode == "nt":
        a_spec = pl.BlockSpec((tm, tk), lambda i, j, kk: (i, kk))
        b_spec = pl.BlockSpec((tn, tk), lambda i, j, kk: (j, kk))
    else:
        a_spec = pl.BlockSpec((tk, tm), lambda i, j, kk: (kk, i))
        b_spec = pl.BlockSpec((tk, tn), lambda i, j, kk: (kk, j))
    return pl.pallas_call(
        body,
        grid=(m // tm, n // tn, nk),
        in_specs=[a_spec, b_spec],
        out_specs=pl.BlockSpec((tm, tn), lambda i, j, kk: (i, j)),
        out_shape=jax.ShapeDtypeStruct((m, n), out_dtype),
        scratch_shapes=[pltpu.VMEM((tm, tn), F32)],
        name=name,
        compiler_params=_params("parallel", "parallel", "arbitrary"),
    )(a, b)


def _dense_fwd_call(q, k, v, scale, name):
    n, sq, d = q.shape
    sk, dv = k.shape[1], v.shape[2]
    tq = _pick(sq, (256, 128))
    c = scale * LOG2E

    def body(q_ref, k_ref, v_ref, o_ref, lse_ref, k_s, vext_s):
        @pl.when(pl.program_id(1) == 0)
        def _():
            k_s[...] = k_ref[0].astype(BF16)
            vext_s[...] = jnp.ones_like(vext_s)
            vext_s[:, :dv] = v_ref[0].astype(BF16)

        s = _dot(q_ref[0], k_s[...], 1, 1)
        m = jnp.max(s, axis=1, keepdims=True)
        p = jnp.exp2(s * c - m * c)
        acc = _dot(p, vext_s[...], 1, 0)
        l = acc[:, dv:dv + 1]
        o_ref[0] = acc[:, :dv] / l
        lse_ref[0] = m * scale + jnp.log(l)

    return pl.pallas_call(
        body,
        grid=(n, sq // tq),
        in_specs=[
            pl.BlockSpec((1, tq, d), lambda h, i: (h, i, 0)),
            pl.BlockSpec((1, sk, d), lambda h, i: (h, 0, 0)),
            pl.BlockSpec((1, sk, dv), lambda h, i: (h, 0, 0)),
        ],
        out_specs=[
            pl.BlockSpec((1, tq, dv), lambda h, i: (h, i, 0)),
            pl.BlockSpec((1, tq, 1), lambda h, i: (h, i, 0)),
        ],
        out_shape=[jax.ShapeDtypeStruct((n, sq, dv), F32), jax.ShapeDtypeStruct((n, sq, 1), F32)],
        scratch_shapes=[pltpu.VMEM((sk, d), BF16), pltpu.VMEM((sk, 2 * dv), BF16)],
        name=name + "_fwd",
        compiler_params=_params("arbitrary", "arbitrary"),
    )(q, k, v)


def _dense_bwd_call(q, k, v, o, lse, do, scale, name):
    n, sq, d = q.shape
    sk, dv = k.shape[1], v.shape[2]
    tq, tk = _pick(sq, (512, 256, 128)), _pick(sk, (2048, 1024, 512, 256, 128))
    c = scale * LOG2E

    def body(q_ref, k_ref, v_ref, o_ref, lse_ref, do_ref, dq_ref, dk_ref, dv_ref):
        j, i = pl.program_id(1), pl.program_id(2)
        qb, kb, vb = q_ref[0].astype(BF16), k_ref[0].astype(BF16), v_ref[0].astype(BF16)
        do_f = do_ref[0]
        dob = do_f.astype(BF16)
        p = jnp.exp2(_dot(qb, kb, 1, 1) * c - lse_ref[0] * LOG2E)
        delta = jnp.sum(do_f * o_ref[0], axis=1, keepdims=True)
        ds = (p * (_dot(dob, vb, 1, 1) - delta)).astype(BF16)
        dv_part = _dot(p, dob, 0, 0)
        dk_part = _dot(ds, qb, 0, 0) * scale
        dq_part = _dot(ds, kb, 1, 0) * scale
        rows = pl.ds(pl.multiple_of(i * tq, tq), tq)

        @pl.when(i == 0)
        def _():
            dk_ref[0] = dk_part
            dv_ref[0] = dv_part

        @pl.when(i > 0)
        def _():
            dk_ref[0] += dk_part
            dv_ref[0] += dv_part

        @pl.when(j == 0)
        def _():
            dq_ref[0, rows, :] = dq_part

        @pl.when(j > 0)
        def _():
            dq_ref[0, rows, :] += dq_part

    return pl.pallas_call(
        body,
        grid=(n, sk // tk, sq // tq),
        in_specs=[
            pl.BlockSpec((1, tq, d), lambda h, j, i: (h, i, 0)),
            pl.BlockSpec((1, tk, d), lambda h, j, i: (h, j, 0)),
            pl.BlockSpec((1, tk, dv), lambda h, j, i: (h, j, 0)),
            pl.BlockSpec((1, tq, dv), lambda h, j, i: (h, i, 0)),
            pl.BlockSpec((1, tq, 1), lambda h, j, i: (h, i, 0)),
            pl.BlockSpec((1, tq, dv), lambda h, j, i: (h, i, 0)),
        ],
        out_specs=[
            pl.BlockSpec((1, sq, d), lambda h, j, i: (h, 0, 0)),
            pl.BlockSpec((1, tk, d), lambda h, j, i: (h, j, 0)),
            pl.BlockSpec((1, tk, dv), lambda h, j, i: (h, j, 0)),
        ],
        out_shape=[
            jax.ShapeDtypeStruct((n, sq, d), F32),
            jax.ShapeDtypeStruct((n, sk, d), F32),
            jax.ShapeDtypeStruct((n, sk, dv), F32),
        ],
        name=name + "_bwd",
        compiler_params=_params("arbitrary", "arbitrary", "arbitrary"),
    )(q, k, v, o, lse, do)


def _head_geometry(h, group):
    pair, a = divmod(h, 2)
    kv_pair, b = divmod(h // group, 2)
    return pair, a, kv_pair, b


def _lane_half():
    return lax.broadcasted_iota(jnp.int32, (1, LANES), 1) // HALF


def _align(x, a, b):
    if a != b:
        x = pltpu.roll(x, HALF, 1)
    return jnp.where(_lane_half() == b, x, 0.0)


def _unalign(x, a, b):
    x = jnp.where(_lane_half() == b, x, 0.0)
    return pltpu.roll(x, HALF, 1) if a != b else x


def _bands(w, pw, nw, lo, kvw, nb):
    b = BAND_BLOCK
    cat = jnp.concatenate([pw[:, lo:lo + kvw], w[:, lo:lo + kvw], nw[:, lo:lo + kvw]], axis=0).astype(BF16)
    out = []
    for g in range(kvw // LANES):
        c3 = cat[:, g * LANES:(g + 1) * LANES].reshape(nb + 2, b, LANES)
        out.append(jnp.concatenate([c3[0:nb], c3[1:nb + 1], c3[2:nb + 2]], axis=1))
    return out


def _edge_mask(first_block, nb, period):
    b = BAND_BLOCK
    blk = (first_block + lax.broadcasted_iota(jnp.int32, (nb, 1, 3 * b), 0)) % period
    col = lax.broadcasted_iota(jnp.int32, (nb, 1, 3 * b), 2)
    outside = ((col < b) & (blk == 0)) | ((col >= 2 * b) & (blk == period - 1))
    return jnp.where(outside, NEG_INF, 0.0)


def _band_geometry(proj, dil):
    rows = proj.shape[0]
    tl = _pick(rows, (1024, 512, 256, 128))
    return rows, tl, tl // BAND_BLOCK, rows // tl, rows // dil // BAND_BLOCK


def _band_in_specs(tl, nb, n_chunks, n_blocks, col, last_step_idle):
    def chunk(i):
        return jnp.minimum(i, n_chunks - 1) if last_step_idle else i

    main = pl.BlockSpec((tl, BAND_W), lambda j, i: (j * n_chunks + chunk(i), col))
    prev = pl.BlockSpec((BAND_BLOCK, BAND_W),
                        lambda j, i: (j * n_blocks + jnp.maximum(chunk(i) * nb - 1, 0), col))
    nxt = pl.BlockSpec((BAND_BLOCK, BAND_W),
                       lambda j, i: (j * n_blocks + jnp.minimum((chunk(i) + 1) * nb, n_blocks - 1), col))
    rows = pl.BlockSpec((tl, QW), lambda j, i: (j * n_chunks + chunk(i), 0))
    return main, prev, nxt, rows


def _band_fwd_call(proj, col, bias, sink, dil, group, kvw, scale, name):
    s_tok = proj.shape[0]
    seq, tl, nb, n_chunks, period = _band_geometry(proj, dil)
    n_blocks = seq // BAND_BLOCK
    heads = bias.shape[0]

    def body(w_ref, pw_ref, nw_ref, bias_ref, sink_ref, o_ref, lse_ref):
        i = pl.program_id(1)
        w, pw, nw = w_ref[...].astype(F32), pw_ref[...].astype(F32), nw_ref[...].astype(F32)
        kb = _bands(w, pw, nw, QW, kvw, nb)
        vb = _bands(w, pw, nw, QW + kvw, kvw, nb)
        edge = _edge_mask(i * nb, nb, period)
        o_acc = [jnp.zeros((tl, LANES), F32) for _ in range(heads // 2)]
        lse_acc = [jnp.zeros((tl, LANES), F32) for _ in range(heads // 2)]
        for h in range(heads):
            pair, a, kvp, b = _head_geometry(h, group)
            q_al = _align(w[:, pair * LANES:(pair + 1) * LANES], a, b).astype(BF16).reshape(nb, BAND_BLOCK, LANES)
            logits = _bmm(q_al, kb[kvp], 2, 2) * scale + bias_ref[h][None] + edge
            sk = sink_ref[h].reshape(1, 1, 1)
            m = jnp.maximum(jnp.max(logits, axis=2, keepdims=True), sk)
            e = jnp.exp(logits - m)
            ssum = jnp.sum(e, axis=2, keepdims=True) + jnp.exp(sk - m)
            out = _bmm(e.astype(BF16), vb[kvp], 2, 1) / ssum
            o_acc[pair] = o_acc[pair] + _unalign(out.reshape(tl, LANES), a, b)
            lse = (m + jnp.log(ssum)).reshape(tl, 1)
            lse_acc[pair] = lse_acc[pair] + jnp.where(_lane_half() == a, lse, 0.0)
        o_ref[...] = jnp.concatenate(o_acc, axis=1)
        lse_ref[...] = jnp.concatenate(lse_acc, axis=1)

    main, prev, nxt, rows = _band_in_specs(tl, nb, n_chunks, n_blocks, col, False)
    return pl.pallas_call(
        body,
        grid=(1, n_chunks),
        in_specs=[main, prev, nxt, pl.BlockSpec(bias.shape, lambda j, i: (0, 0, 0)),
                  pl.BlockSpec(sink.shape, lambda j, i: (0, 0, 0))],
        out_specs=[rows, rows],
        out_shape=[jax.ShapeDtypeStruct((s_tok, QW), F32)] * 2,
        name=name + "_fwd",
        compiler_params=_params("parallel", "parallel"),
    )(proj, proj, proj, bias, sink)


def _band_bwd_call(proj, o, do, lse, dlse, bias, sink, dproj, col, dil, group, kvw, scale, name):
    seq, tl, nb, n_chunks, period = _band_geometry(proj, dil)
    lead = 0 if dproj is None else 1
    n_blocks = seq // BAND_BLOCK
    heads = bias.shape[0]
    b_ = BAND_BLOCK
    have_dlse = dlse is not None

    def body(*refs):
        (w_ref, pw_ref, nw_ref, o_ref, do_ref, lse_ref), refs = refs[lead:lead + 6], refs[lead + 6:]
        if have_dlse:
            dlse_ref, refs = refs[0], refs[1:]
        bias_ref, sink_ref, dwin_ref, dbias_ref, dsink_ref, dq_s, dk_s, dv_s = refs
        j, i = pl.program_id(0), pl.program_id(1)

        @pl.when((j == 0) & (i == 0))
        def _():
            dbias_ref[...] = jnp.zeros_like(dbias_ref)
            dsink_ref[...] = jnp.zeros_like(dsink_ref)

        @pl.when(i == 0)
        def _():
            dk_s[...] = jnp.zeros_like(dk_s)
            dv_s[...] = jnp.zeros_like(dv_s)

        @pl.when(i < n_chunks)
        def _():
            w, pw, nw = w_ref[...].astype(F32), pw_ref[...].astype(F32), nw_ref[...].astype(F32)
            kb = _bands(w, pw, nw, QW, kvw, nb)
            vb = _bands(w, pw, nw, QW + kvw, kvw, nb)
            edge = _edge_mask(i * nb, nb, period)
            dq_acc = [jnp.zeros((tl, LANES), F32) for _ in range(heads // 2)]
            for h in range(heads):
                pair, a, kvp, b = _head_geometry(h, group)
                lanes = slice(pair * LANES, (pair + 1) * LANES)
                mine = _lane_half() == a
                q_al = _align(w[:, lanes], a, b).astype(BF16).reshape(nb, b_, LANES)
                do_al = _align(do_ref[:, lanes], a, b).astype(BF16).reshape(nb, b_, LANES)
                lse_h = jnp.max(jnp.where(mine, lse_ref[:, lanes], NEG_INF), axis=1, keepdims=True)
                shift = -jnp.sum(jnp.where(mine, do_ref[:, lanes] * o_ref[:, lanes], 0.0), axis=1, keepdims=True)
                if have_dlse:
                    shift = shift + jnp.sum(jnp.where(mine, dlse_ref[:, lanes], 0.0), axis=1, keepdims=True)
                logits = _bmm(q_al, kb[kvp], 2, 2) * scale + bias_ref[h][None] + edge
                p = jnp.exp(logits - lse_h.reshape(nb, b_, 1))
                dlogits = p * (_bmm(do_al, vb[kvp], 2, 2) + shift.reshape(nb, b_, 1))
                dbias_ref[h] += jnp.sum(dlogits, axis=0)
                dsink_ref[h] += jnp.sum(jnp.exp(sink_ref[h] - lse_h) * shift, axis=0, keepdims=True)
                ds = (dlogits * scale).astype(BF16)
                dq_acc[pair] = dq_acc[pair] + _unalign(_bmm(ds, kb[kvp], 2, 1).reshape(tl, LANES), a, b)
                dk_band = _bmm(ds, q_al, 1, 1)
                dv_band = _bmm(p.astype(BF16), do_al, 1, 1)
                kv_lanes = slice(kvp * LANES, (kvp + 1) * LANES)
                for t in range(3):
                    at = pl.ds(pl.multiple_of(i * tl + t * b_, b_), tl)
                    dk_s[at, kv_lanes] += dk_band[:, t * b_:(t + 1) * b_, :].reshape(tl, LANES)
                    dv_s[at, kv_lanes] += dv_band[:, t * b_:(t + 1) * b_, :].reshape(tl, LANES)
            dq_s[lax.rem(i, 2)] = jnp.concatenate(dq_acc, axis=1)

        @pl.when(i >= 1)
        def _():
            at = pl.ds(pl.multiple_of((i - 1) * tl + b_, b_), tl)
            parts = [dq_s[lax.rem(i + 1, 2)], dk_s[at, :], dv_s[at, :]]
            if QW + 2 * kvw < BAND_W:
                parts.append(jnp.zeros((tl, BAND_W - QW - 2 * kvw), F32))
            dwin_ref[...] = jnp.concatenate(parts, axis=1).astype(dwin_ref.dtype)

    main, prev, nxt, rows = _band_in_specs(tl, nb, n_chunks, n_blocks, col, True)
    row_args = [o, do, lse] + ([dlse] if have_dlse else [])
    small = [pl.BlockSpec(bias.shape, lambda j, i: (0, 0, 0)), pl.BlockSpec(sink.shape, lambda j, i: (0, 0, 0))]
    return pl.pallas_call(
        body,
        grid=(1, n_chunks + 1),
        in_specs=[pl.BlockSpec(memory_space=pl.ANY)] * lead + [main, prev, nxt] + [rows] * len(row_args) + small,
        out_specs=[pl.BlockSpec((tl, BAND_W), lambda j, i: (j * n_chunks + jnp.maximum(i - 1, 0), col))] + small,
        out_shape=[jax.ShapeDtypeStruct(proj.shape, BF16), jax.ShapeDtypeStruct(bias.shape, F32),
                   jax.ShapeDtypeStruct(sink.shape, F32)],
        scratch_shapes=[pltpu.VMEM((2, tl, QW), F32), pltpu.VMEM((seq + 2 * b_, kvw), F32),
                        pltpu.VMEM((seq + 2 * b_, kvw), F32)],
        input_output_aliases={0: 0} if lead else {},
        name=name + "_bwd",
        compiler_params=_params("arbitrary", "arbitrary"),
    )(*([dproj] if lead else []), proj, proj, proj, *row_args, bias, sink)


def _loss_call(x, target, g):
    s, d = x.shape
    tr = _pick(s, (256, 128, 64, 32, 16, 8))

    def tile_loss(xt, gt, tt):
        err = jnp.square(_rms(xt, gt) - tt)
        return 0.5 * jnp.sum(jnp.mean(err, axis=-1, keepdims=True), axis=0, keepdims=True)

    def body(x_ref, t_ref, g_ref, loss_ref, dx_ref, dg_ref):
        tt = t_ref[...]
        val, vjp = jax.vjp(lambda xt, gt: tile_loss(xt, gt, tt), x_ref[...], g_ref[...])
        dx, dg = vjp(jnp.ones_like(val))
        dx_ref[...] = dx

        @pl.when(pl.program_id(0) == 0)
        def _():
            loss_ref[...] = jnp.zeros_like(loss_ref)
            dg_ref[...] = jnp.zeros_like(dg_ref)

        loss_ref[...] += val
        dg_ref[...] += dg

    return pl.pallas_call(
        body,
        grid=(s // tr,),
        in_specs=[_rows(tr, d), _rows(tr, d), _whole((1, d))],
        out_specs=[_whole((1, 1)), _rows(tr, d), _whole((1, d))],
        out_shape=[jax.ShapeDtypeStruct((1, 1), F32), jax.ShapeDtypeStruct((s, d), F32),
                   jax.ShapeDtypeStruct((1, d), F32)],
        name="final_norm_loss",
        compiler_params=_params("arbitrary"),
    )(x, target, g)


@jax.custom_vjp
def _loss_op(x, target, g):
    return _loss_call(x, target, g)[0][0, 0]


def _loss_op_fwd(x, target, g):
    loss, dx, dg = _loss_call(x, target, g)
    return loss[0, 0], (dx, dg, target)


def _loss_op_bwd(res, ct):
    dx, dg, target = res
    return ct * dx, jnp.zeros_like(target), ct * dg


_loss_op.defvjp(_loss_op_fwd, _loss_op_bwd)


def _mla_tile(r, p, a):
    g_q, g_kv, w_q, w_k, w_v = p
    cos_t, sin_t, place_kr = a
    a_q, a_kv, a_kr = _lanes(r[0], (0, MLA_Q_LORA, MLA_Q_LORA + MLA_KV_LORA, MLA_Q_LORA + MLA_KV_LORA + MLA_ROPE))
    q = _rope(_bdot(_rms(a_q, g_q), w_q), cos_t, sin_t, MLA_ROPE // 2)
    ckv = _rms(a_kv, g_kv)
    k = _rope(_bdot(ckv, w_k) + _hdot(a_kr, place_kr), cos_t, sin_t, MLA_ROPE // 2)
    return _split_heads(q, MLA_HEADS), _split_heads(k, MLA_HEADS), _split_heads(_bdot(ckv, w_v), MLA_HEADS)


def _head_rms(x, g, head_mean):
    return x * lax.rsqrt(_hdot(x * x, head_mean) + EPS) * g


def _gqa_tile(r, p, a):
    g_q, g_k = p
    cos_t, sin_t, mean_q, mean_k = a
    wq, wk = GQA_HEADS * HEAD_DIM, GQA_KV_HEADS * HEAD_DIM
    b_q, b_k, b_v = _lanes(r[0], (0, wq, wq + wk, wq + 2 * wk))
    q = _rope(_head_rms(b_q, g_q, mean_q), cos_t, sin_t, HEAD_DIM // 4)
    k = _rope(_head_rms(b_k, g_k, mean_k), cos_t[:, :wk], sin_t[:, :wk], HEAD_DIM // 4)
    return _split_heads(q, GQA_HEADS), _split_heads(k, GQA_KV_HEADS), _split_heads(b_v, GQA_KV_HEADS)


def _merge_tile(r, p, a):
    gm, o_a, o_b, oc0, oc1, oc2, l0, l1, l2, o_d = r
    (w_branch,) = p
    d = w_branch.shape[2]
    gate_path, merge_logits = _lanes(gm, (0, N_BRANCH * BRANCH_W, N_BRANCH * BRANCH_W + N_BRANCH * d))
    m = jnp.maximum(jnp.maximum(l0, l1), l2)
    e0, e1, e2 = jnp.exp(l0 - m), jnp.exp(l1 - m), jnp.exp(l2 - m)
    y_c = (e0 * oc0 + e1 * oc1 + e2 * oc2) / (e0 + e1 + e2)
    y = jnp.concatenate([_join_heads(o_a), _join_heads(o_b), y_c, o_d], axis=1)
    u = y * (gate_path * jax.nn.sigmoid(gate_path))
    gates = _lanes(merge_logits, tuple(range(0, N_BRANCH * d + 1, d)))
    us = _lanes(u, tuple(range(0, N_BRANCH * BRANCH_W + 1, BRANCH_W)))
    branch_w = _unstack(w_branch)
    out = None
    for nb in range(N_BRANCH):
        term = jax.nn.sigmoid(gates[nb]) * _bdot(us[nb], branch_w[nb])
        out = term if out is None else out + term
    return (out,)


def _mixer_calls(proj, prm, aux):
    s = proj.shape[0]
    tr, tm = _pick(s, (256, 128)), _pick(s, (128,))
    mla_cos, mla_sin, gqa_cos, gqa_sin, place_kr, mean_q, mean_k = aux
    wq = MLA_HEADS * MLA_QK
    mla = dict(
        steps=s // tr, rows=[(proj, _rows(tr, SMALL_W, MLA_BLK))],
        params=[prm["g_q"], prm["g_kv"], prm["w_q"], prm["w_k"], prm["w_v"]],
        aux=[(mla_cos, _rows(tr, wq)), (mla_sin, _rows(tr, wq)), (place_kr, _whole(place_kr.shape))],
        outs=[((MLA_HEADS, s, MLA_QK), _head_rows(MLA_HEADS, tr, MLA_QK))] * 2
        + [((MLA_HEADS, s, MLA_V), _head_rows(MLA_HEADS, tr, MLA_V))],
        window=((s, P_TOT), _rows(tr, SMALL_W, MLA_BLK), BF16))
    wg = GQA_HEADS * HEAD_DIM
    gqa = dict(
        steps=s // tr, rows=[(proj, _rows(tr, SMALL_W, GQA_BLK))], params=[prm["gq"], prm["gk"]],
        aux=[(gqa_cos, _rows(tr, wg)), (gqa_sin, _rows(tr, wg)), (mean_q, _whole(mean_q.shape)),
             (mean_k, _whole(mean_k.shape))],
        outs=[((GQA_HEADS, s, HEAD_DIM), _head_rows(GQA_HEADS, tr, HEAD_DIM))]
        + [((GQA_KV_HEADS, s, HEAD_DIM), _head_rows(GQA_KV_HEADS, tr, HEAD_DIM))] * 2,
        window=((s, P_TOT), _rows(tr, SMALL_W, GQA_BLK), BF16))
    merge = dict(steps=s // tm, tm=tm, window=((s, P_TOT), _rows(tm, GM_W, 0), BF16))
    return mla, gqa, merge


def _merge_rows(proj, o_a, o_b, ocs, lses, o_d, tm):
    h4 = _head_rows(4, tm, HEAD_DIM)
    return ([(proj, _rows(tm, GM_W, 0)), (o_a, h4), (o_b, h4)] + [(t, _rows(tm, QW)) for t in ocs + lses]
            + [(o_d, _rows(tm, QW))])


def _to_residues(t, dil):
    s, w = t.shape
    return t if dil == 1 else t.reshape(s // dil, dil, w).transpose(1, 0, 2).reshape(s, w)


def _from_residues(t, dil):
    s, w = t.shape
    return t if dil == 1 else t.reshape(dil, s // dil, w).transpose(1, 0, 2).reshape(s, w)


def _mixer_fwd(projs, prm, aux):
    proj = projs[0]
    s = proj.shape[0]
    mla, gqa, merge = _mixer_calls(proj, prm, aux)
    q_a, k_a, v_a = _fwd_call("prep_mla", _mla_tile, mla["steps"], mla["rows"], mla["params"], mla["aux"], mla["outs"])
    o_a, lse_a = _dense_fwd_call(q_a, k_a, v_a, MLA_QK ** -0.5, "mla")
    q_b, k_b, v_b = _fwd_call("prep_gqa", _gqa_tile, gqa["steps"], gqa["rows"], gqa["params"], gqa["aux"], gqa["outs"])
    grp = GQA_HEADS // GQA_KV_HEADS
    o_b, lse_b = _dense_fwd_call(q_b.reshape(GQA_KV_HEADS, grp * s, HEAD_DIM), k_b, v_b, HEAD_DIM ** -0.5, "gqa")
    scale = HEAD_DIM ** -0.5
    ocs, lses = [], []
    for gi, (_, dil) in enumerate(DIL_PATTERNS):
        o, lse = _band_fwd_call(projs[gi], DIL_BLK if gi == 0 else 0, prm["bias_dil"][gi], prm["no_sink"], dil, 1,
                                QW, scale, "dil%d" % gi)
        ocs.append(o)
        lses.append(lse)
    o_d, lse_d = _band_fwd_call(proj, WIN_BLK, prm["bias_win"], prm["sink"], 1, WIN_HEADS // WIN_KV_HEADS,
                                WIN_KV_HEADS * HEAD_DIM, scale, "win")
    dils = [dil for _, dil in DIL_PATTERNS]
    rows = _merge_rows(proj, o_a, o_b.reshape(GQA_HEADS, s, HEAD_DIM), [_from_residues(t, r) for t, r in zip(ocs, dils)],
                       [_from_residues(t, r) for t, r in zip(lses, dils)], o_d, merge["tm"])
    mix = _fwd_call("merge", _merge_tile, merge["steps"], rows, [prm["w_branch"]], [],
                    [((s, prm["w_branch"].shape[2]), _rows(merge["tm"], prm["w_branch"].shape[2]), BF16)])[0]
    return mix, (q_a, k_a, v_a, o_a, lse_a, q_b, k_b, v_b, o_b, lse_b, ocs, lses, o_d, lse_d)


def _mixer_bwd(projs, prm, aux, saved, dmix):
    proj = projs[0]
    s = proj.shape[0]
    q_a, k_a, v_a, o_a, lse_a, q_b, k_b, v_b, o_b, lse_b, ocs, lses, o_d, lse_d = saved
    dils = [dil for _, dil in DIL_PATTERNS]
    mla, gqa, merge = _mixer_calls(proj, prm, aux)
    tm, d_model = merge["tm"], prm["w_branch"].shape[2]
    grp = GQA_HEADS // GQA_KV_HEADS
    scale = HEAD_DIM ** -0.5

    rows = _merge_rows(proj, o_a, o_b.reshape(GQA_HEADS, s, HEAD_DIM), [_from_residues(t, r) for t, r in zip(ocs, dils)],
                       [_from_residues(t, r) for t, r in zip(lses, dils)], o_d, tm)
    grads, (dw_branch,) = _vjp_call(
        "merge", _merge_tile, merge["steps"], rows, [prm["w_branch"]], [], [(dmix, _rows(tm, d_model))],
        [merge["window"]] + [(a.shape, spec) for a, spec in rows[1:]])
    dproj, do_a, do_b, docs, dlses, do_d = grads[0], grads[1], grads[2], grads[3:6], grads[6:9], grads[9]

    dq_a, dk_a, dv_a = _dense_bwd_call(q_a, k_a, v_a, o_a, lse_a, do_a, MLA_QK ** -0.5, "mla")
    (dproj,), dmla = _vjp_call("prep_mla", _mla_tile, mla["steps"], mla["rows"], mla["params"], mla["aux"],
                               [(t, spec) for t, (_, spec) in zip((dq_a, dk_a, dv_a), mla["outs"])],
                               [mla["window"]], into=dproj)
    dq_b, dk_b, dv_b = _dense_bwd_call(q_b.reshape(GQA_KV_HEADS, grp * s, HEAD_DIM), k_b, v_b, o_b, lse_b,
                                       do_b.reshape(GQA_KV_HEADS, grp * s, HEAD_DIM), scale, "gqa")
    (dproj,), dgqa = _vjp_call("prep_gqa", _gqa_tile, gqa["steps"], gqa["rows"], gqa["params"], gqa["aux"],
                               [(t, spec) for t, (_, spec) in zip((dq_b.reshape(GQA_HEADS, s, HEAD_DIM), dk_b, dv_b),
                                                                  gqa["outs"])],
                               [gqa["window"]], into=dproj)
    dproj, dbias_win, dsink = _band_bwd_call(proj, o_d, do_d, lse_d, None, prm["bias_win"], prm["sink"], dproj,
                                             WIN_BLK, 1, WIN_HEADS // WIN_KV_HEADS, WIN_KV_HEADS * HEAD_DIM, scale, "win")
    dbias_dil, dprojs = [], []
    for gi, dil in enumerate(dils):
        dside, dbias, _ = _band_bwd_call(
            projs[gi], ocs[gi], _to_residues(docs[gi], dil), lses[gi], _to_residues(dlses[gi], dil),
            prm["bias_dil"][gi], prm["no_sink"], dproj if gi == 0 else None, DIL_BLK if gi == 0 else 0, dil, 1, QW,
            scale, "dil%d" % gi)
        if gi == 0:
            dproj = dside
        else:
            dprojs.append(dside)
        dbias_dil.append(dbias)
    dprm = dict(g_q=dmla[0], g_kv=dmla[1], w_q=dmla[2], w_k=dmla[3], w_v=dmla[4], gq=dgqa[0], gk=dgqa[1],
                bias_dil=dbias_dil, bias_win=dbias_win, sink=dsink, no_sink=jnp.zeros_like(prm["no_sink"]),
                w_branch=dw_branch)
    return [dproj] + dprojs, {k: jax.tree.map(lambda g, p: g.astype(p.dtype), v, prm[k]) for k, v in dprm.items()}


def _layer_projs(xn, w):
    dils = [dil for _, dil in DIL_PATTERNS]
    xns = [xn] + [_to_residues(xn, r) for r in dils[1:]]
    return xns, [_mm(a, b, "nt", "proj%d_fwd" % i, BF16) for i, (a, b) in enumerate(zip(xns, w["w_in_t"]))]


def _layer_fwd(x, w, aux):
    s, d = x.shape
    tr = _pick(s, (256, 128, 64, 32, 16, 8))
    def with_transpose(r, p, a):
        y = _rms(r[0], p[0])
        return y, y.T

    xn, xn_t = _fwd_call("norm", with_transpose, s // tr, [(x, _rows(tr, d))], [w["norm_g"]], [],
                         [((s, d), _rows(tr, d), BF16), ((d, s), pl.BlockSpec((d, tr), lambda i: (0, i)), BF16)])
    xns, projs = _layer_projs(xn, w)
    mix, saved = _mixer_fwd(projs, w["mixer"], aux)
    return _mm(mix, w["w_out"], "nn", "out_proj_nn"), (x, w, aux, xns, xn_t, projs, mix, saved)


@jax.custom_vjp
def _layer_core(x, w, aux):
    return _layer_fwd(x, w, aux)[0]


def _layer_core_bwd(res, dout):
    x, w, aux, xns, xn_t, projs, mix, saved = res
    s, d = x.shape
    tr = _pick(s, (256, 128, 64, 32, 16, 8))
    dils = [dil for _, dil in DIL_PATTERNS]
    dmix = _mm(dout, w["w_out"], "nt", "out_proj_nt")
    dw_out = _mm(mix, dout, "tn", "out_proj_tn", w["w_out"].dtype)
    dprojs, dmixer = _mixer_bwd(projs, w["mixer"], aux, saved, dmix)
    dxn = None
    for i, (dp, wi, r) in enumerate(zip(dprojs, w["w_in_t"], dils)):
        part = _mm(_from_residues(dp, r), wi, "nn", "proj%d_dx" % i)
        dxn = part if dxn is None else dxn + part
    dw_in_t = [_mm(xn_t, dprojs[0], "nn", "proj0_dw", w["w_in_t"][0].dtype).T]
    dw_in_t += [_mm(dp, a, "tn", "proj%d_dw" % i, wi.dtype)
                for i, (a, dp, wi) in list(enumerate(zip(xns, dprojs, w["w_in_t"])))[1:]]
    (dx,), (dg,) = _vjp_call("norm", _norm_tile, s // tr, [(x, _rows(tr, d))], [w["norm_g"]], [],
                             [(dxn, _rows(tr, d))], [((s, d), _rows(tr, d))])
    dw = dict(norm_g=dg, w_in_t=dw_in_t, mixer=dmixer, w_out=dw_out)
    return dx, dw, tuple(jnp.zeros_like(t) for t in aux)


_layer_core.defvjp(lambda x, w, aux: _layer_fwd(x, w, aux), _layer_core_bwd)


def _rope_angles(pos, dim):
    inv = ROPE_THETA ** (-jnp.arange(0, dim, 2, dtype=F32) / dim)
    return pos.astype(F32)[:, None] * inv[None, :]


def _rope_tables(s):
    pos = jnp.arange(s, dtype=jnp.int32)
    rows = s // GRID_W
    row_idx = jnp.repeat(jnp.arange(rows, dtype=jnp.int32), GRID_W)
    col_idx = jnp.tile(jnp.arange(GRID_W, dtype=jnp.int32), rows)
    a1 = _rope_angles(pos, MLA_ROPE)
    ar = _rope_angles(row_idx, HEAD_DIM // 2)
    ac = _rope_angles(col_idx, HEAD_DIM // 2)
    ones, zeros = jnp.ones((s, MLA_NOPE), F32), jnp.zeros((s, MLA_NOPE), F32)
    mla_cos = jnp.tile(jnp.concatenate([ones, jnp.cos(a1), jnp.cos(a1)], axis=1), (1, MLA_HEADS))
    mla_sin = jnp.tile(jnp.concatenate([zeros, -jnp.sin(a1), jnp.sin(a1)], axis=1), (1, MLA_HEADS))
    gqa_cos = jnp.tile(jnp.concatenate([jnp.cos(ar), jnp.cos(ar), jnp.cos(ac), jnp.cos(ac)], axis=1), (1, GQA_HEADS))
    gqa_sin = jnp.tile(jnp.concatenate([-jnp.sin(ar), jnp.sin(ar), -jnp.sin(ac), jnp.sin(ac)], axis=1), (1, GQA_HEADS))
    return mla_cos, mla_sin, gqa_cos, gqa_sin


def _t5_bucket(rel):
    nb = T5_BUCKETS // 2
    max_exact = nb // 2
    n = jnp.abs(rel)
    nf = jnp.maximum(n, 1).astype(F32)
    large = max_exact + (jnp.log(nf / max_exact) / math.log(T5_MAX_DIST / max_exact) * (nb - max_exact)).astype(jnp.int32)
    large = jnp.minimum(large, nb - 1)
    return jnp.where(rel > 0, nb, 0) + jnp.where(n < max_exact, n, large)


def _band_bias(table, stride, head_lo, heads, half_window):
    b = BAND_BLOCK
    offs = jnp.arange(3 * b)[None, :] - b - jnp.arange(b)[:, None]
    one_hot = (_t5_bucket(offs * stride)[..., None] == jnp.arange(T5_BUCKETS)).astype(F32)
    bias = jnp.dot(one_hot.reshape(b * 3 * b, T5_BUCKETS), table[:, head_lo:head_lo + heads],
                   precision=lax.Precision.HIGHEST)
    bias = bias.T.reshape(heads, b, 3 * b)
    return jnp.where((jnp.abs(offs) <= half_window)[None], bias, NEG_INF)


def _w_in_rows(d):
    mla, gqa, win, dil0 = MLA_BLK * SMALL_W, GQA_BLK * SMALL_W, WIN_BLK * BAND_W, DIL_BLK * BAND_W
    plan, at = [], 0
    for width, target, row in ((256, 0, mla), (128, 0, mla + 256), (32, 0, mla + 384),
                               (256, 0, gqa), (128, 0, gqa + 256), (128, 0, gqa + 384)):
        plan.append((at, width, target, row))
        at += width
    for part in range(3):
        for g in range(len(DIL_PATTERNS)):
            plan.append((at, QW, g, (dil0 if g == 0 else 0) + part * QW))
            at += QW
    for width, row in ((256, win), (128, win + 256), (128, win + 384), (N_BRANCH * BRANCH_W, 0),
                       (N_BRANCH * d, N_BRANCH * BRANCH_W)):
        plan.append((at, width, 0, row))
        at += width
    return plan


@jax.custom_vjp
def _w_in_layout(w_in_t):
    d = w_in_t.shape[1]
    outs = []
    for target, rows in enumerate((P_TOT, BAND_W, BAND_W)):
        parts, at = [], 0
        for start, width, _, row in sorted((p for p in _w_in_rows(d) if p[2] == target), key=lambda p: p[3]):
            if row > at:
                parts.append(jnp.zeros((row - at, d), w_in_t.dtype))
            parts.append(w_in_t[start:start + width])
            at = row + width
        if at < rows:
            parts.append(jnp.zeros((rows - at, d), w_in_t.dtype))
        outs.append(jnp.concatenate(parts, axis=0))
    return outs


def _w_in_layout_fwd(w_in_t):
    return _w_in_layout(w_in_t), None


def _w_in_layout_bwd(_, cts):
    d = cts[0].shape[1]
    return (jnp.concatenate([cts[target][row:row + width] for _, width, target, row in _w_in_rows(d)], axis=0),)


_w_in_layout.defvjp(_w_in_layout_fwd, _w_in_layout_bwd)


def _layer(x, w, l, aux, biases):
    w_kv = w["w_kv_t"][l].T.reshape(MLA_KV_LORA, MLA_HEADS, MLA_NOPE + MLA_V)
    w_k = jnp.concatenate([w_kv[:, :, :MLA_NOPE], jnp.zeros((MLA_KV_LORA, MLA_HEADS, MLA_ROPE), w_kv.dtype)], axis=2)
    dil_bias, win_bias = biases
    prm = dict(
        g_q=w["mla_q_norm_g"][l][None, :], g_kv=w["mla_kv_norm_g"][l][None, :], w_q=w["w_q_t"][l].T,
        w_k=w_k.reshape(MLA_KV_LORA, MLA_HEADS * MLA_QK),
        w_v=w_kv[:, :, MLA_NOPE:].reshape(MLA_KV_LORA, MLA_HEADS * MLA_V),
        gq=jnp.tile(w["gqa_q_norm_g"][l], GQA_HEADS)[None, :], gk=jnp.tile(w["gqa_k_norm_g"][l], GQA_KV_HEADS)[None, :],
        bias_dil=list(dil_bias), bias_win=win_bias, sink=w["win_sink"][l].reshape(WIN_HEADS, 1, 1),
        no_sink=jnp.full((DIL_HEADS, 1, 1), NEG_INF, F32), w_branch=jnp.transpose(w["w_branch_t"][l].reshape(-1, N_BRANCH, BRANCH_W), (1, 2, 0)))
    layer_w = dict(norm_g=w["norm_g"][l][None, :], w_in_t=_w_in_layout(w["w_in_t"][l]), mixer=prm, w_out=w["w_out"][l])
    return x + _layer_core(x, layer_w, aux)


def _local_loss(w, x, target):
    s, d_model = x.shape
    assert d_model == D_MODEL, "the projection's window layout is laid out for d_model 1024"
    place = np.zeros((MLA_ROPE, MLA_HEADS * MLA_QK), np.float32)
    for h in range(MLA_HEADS):
        for i in range(MLA_ROPE):
            place[i, h * MLA_QK + MLA_NOPE + i] = 1.0

    def head_mean(nh):
        m = np.kron(np.eye(nh, dtype=np.float32), np.full((HEAD_DIM, HEAD_DIM), 1.0 / HEAD_DIM, np.float32))
        return jnp.asarray(m)

    aux = _rope_tables(s) + (jnp.asarray(place), head_mean(GQA_HEADS), head_mean(GQA_KV_HEADS))
    table = w["t5_table"]
    dil_bias = [_band_bias(table, dil, gi * DIL_HEADS, DIL_HEADS, window // (2 * dil))
                for gi, (window, dil) in enumerate(DIL_PATTERNS)]
    win_bias = _band_bias(table, 1, len(DIL_PATTERNS) * DIL_HEADS, WIN_HEADS, WIN_HALF)
    for l in range(w["norm_g"].shape[0]):
        x = _layer(x, w, l, aux, (dil_bias, win_bias))
    return _loss_op(x, target, w["final_norm_g"][None, :])


_ANY = pl.BlockSpec(memory_space=pl.ANY)
_MESH = pl.DeviceIdType.MESH


def _all_gather(block, name):
    def body(x_ref, out_ref, send_sems, recv_sems, local_sem):
        x, y, c = lax.axis_index("x"), lax.axis_index("y"), lax.axis_index("c")
        me, sibling = (x, y, c), (x, y, 1 - c)
        chips = [(1 - x, y), (x, 1 - y), (1 - x, 1 - y)]

        def slot(px, py, pc):
            return out_ref.at[4 * px + 2 * py + pc]

        def copy(k, blk, to, src=None):
            return pltpu.make_async_remote_copy(
                src_ref=slot(*blk) if src is None else src, dst_ref=slot(*blk),
                send_sem=send_sems.at[k], recv_sem=recv_sems.at[k], device_id=to, device_id_type=_MESH)

        mine = pltpu.make_async_copy(x_ref, slot(*me), local_sem)
        mine.start()
        first = [copy(0, me, sibling, src=x_ref)]
        first += [copy(1 + j, me, (*chip, c), src=x_ref) for j, chip in enumerate(chips)]
        for cp in first:
            cp.start()
        passed = [copy(4 + j, (*chip, c), sibling) for j, chip in enumerate(chips)]
        for j, chip in enumerate(chips):
            copy(1 + j, (*chip, c), me).wait_recv()
            passed[j].start()
        copy(0, sibling, me).wait_recv()
        for j, chip in enumerate(chips):
            copy(4 + j, (*chip, 1 - c), me).wait_recv()
        for cp in first + passed:
            cp.wait_send()
        mine.wait()

    return pl.pallas_call(
        body,
        out_shape=jax.ShapeDtypeStruct((N_DEV,) + block.shape, block.dtype),
        in_specs=[_ANY],
        out_specs=_ANY,
        scratch_shapes=[pltpu.SemaphoreType.DMA((7,)), pltpu.SemaphoreType.DMA((7,)), pltpu.SemaphoreType.DMA],
        name=name,
    )(block)


def _swap_with_sibling(blocks, name):
    chips = blocks.shape[0]

    def body(x_ref, out_ref, send_sems, recv_sems):
        x, y, c = lax.axis_index("x"), lax.axis_index("y"), lax.axis_index("c")
        copies = [pltpu.make_async_remote_copy(
            src_ref=x_ref.at[k, 1 - c], dst_ref=out_ref.at[k], send_sem=send_sems.at[k], recv_sem=recv_sems.at[k],
            device_id=(x, y, 1 - c), device_id_type=_MESH) for k in range(chips)]
        for cp in copies:
            cp.start()
        for cp in copies:
            cp.wait()

    return pl.pallas_call(
        body,
        out_shape=jax.ShapeDtypeStruct((chips,) + blocks.shape[2:], blocks.dtype),
        in_specs=[_ANY],
        out_specs=_ANY,
        scratch_shapes=[pltpu.SemaphoreType.DMA((chips,)), pltpu.SemaphoreType.DMA((chips,))],
        name=name,
    )(blocks)


def _add_sibling(blocks, theirs, name):
    chips, _, rows, w = blocks.shape
    tr = _row_tile(rows, 16, 4096)

    def body(b_ref, t_ref, o_ref):
        mine = b_ref[0, lax.axis_index("c")]
        o_ref[0] = (mine.astype(F32) + t_ref[0].astype(F32)).astype(o_ref.dtype)

    return pl.pallas_call(
        body,
        grid=(chips, rows // tr),
        in_specs=[pl.BlockSpec((1, 2, tr, w), lambda k, i: (k, 0, i, 0)), pl.BlockSpec((1, tr, w), lambda k, i: (k, i, 0))],
        out_specs=pl.BlockSpec((1, tr, w), lambda k, i: (k, i, 0)),
        out_shape=jax.ShapeDtypeStruct(theirs.shape, theirs.dtype),
        name=name,
        compiler_params=_params("parallel", "parallel"),
    )(blocks, theirs)


def _exchange_chips(partials, name):
    n_chips = partials.shape[0]

    def body(x_ref, out_ref, send_sems, recv_sems, local_sem):
        x, y, c = lax.axis_index("x"), lax.axis_index("y"), lax.axis_index("c")
        me = 2 * x + y
        mine = pltpu.make_async_copy(x_ref.at[me], out_ref.at[me], local_sem)
        mine.start()
        copies, landed = [], []
        for k in range(1, n_chips):
            px = 1 - x if k & 2 else x
            py = 1 - y if k & 1 else y
            peer = 2 * px + py
            copies.append(pltpu.make_async_remote_copy(
                src_ref=x_ref.at[peer], dst_ref=out_ref.at[me], send_sem=send_sems.at[k - 1],
                recv_sem=recv_sems.at[k - 1], device_id=(px, py, c), device_id_type=_MESH))
            landed.append(pltpu.make_async_remote_copy(
                src_ref=x_ref.at[peer], dst_ref=out_ref.at[peer], send_sem=send_sems.at[k - 1],
                recv_sem=recv_sems.at[k - 1], device_id=(px, py, c), device_id_type=_MESH))
        for cp in copies:
            cp.start()
        for cp in landed:
            cp.wait_recv()
        for cp in copies:
            cp.wait_send()
        mine.wait()

    return pl.pallas_call(
        body,
        out_shape=jax.ShapeDtypeStruct(partials.shape, partials.dtype),
        in_specs=[_ANY],
        out_specs=_ANY,
        scratch_shapes=[pltpu.SemaphoreType.DMA((n_chips - 1,)), pltpu.SemaphoreType.DMA((n_chips - 1,)),
                        pltpu.SemaphoreType.DMA],
        name=name,
    )(partials)


def _sum_slots(parts, name):
    slots, rows, w = parts.shape
    tr = _row_tile(rows, 16 if parts.dtype == BF16 else 8, 4096)

    def body(p_ref, o_ref):
        acc = p_ref[0].astype(F32)
        for j in range(1, slots):
            acc = acc + p_ref[j].astype(F32)
        o_ref[...] = acc

    return pl.pallas_call(
        body,
        grid=(rows // tr,),
        in_specs=[pl.BlockSpec((slots, tr, w), lambda i: (0, i, 0))],
        out_specs=pl.BlockSpec((tr, w), lambda i: (i, 0)),
        out_shape=jax.ShapeDtypeStruct((rows, w), F32),
        name=name,
        compiler_params=_params("parallel"),
    )(parts)


def _adamw(w, g, m, v, name):
    rows, width = w.shape
    tr = _row_tile(rows, 8, 2048)

    def body(w_ref, g_ref, m_ref, v_ref, d_ref, nm_ref, nv_ref):
        g_ = g_ref[...]
        m_ = ADAM_B1 * m_ref[...] + (1.0 - ADAM_B1) * g_
        v_ = ADAM_B2 * v_ref[...] + (1.0 - ADAM_B2) * jnp.square(g_)
        m_hat = m_ / (1.0 - ADAM_B1 ** ADAM_STEP)
        v_hat = v_ / (1.0 - ADAM_B2 ** ADAM_STEP)
        d_ref[...] = -ADAM_LR * (m_hat / (jnp.sqrt(v_hat) + ADAM_EPS) + ADAM_WD * w_ref[...])
        nm_ref[...] = m_
        nv_ref[...] = v_

    spec = pl.BlockSpec((tr, width), lambda i: (i, 0))
    return pl.pallas_call(
        body,
        grid=(rows // tr,),
        in_specs=[spec] * 4,
        out_specs=[spec] * 3,
        out_shape=[jax.ShapeDtypeStruct((rows, width), F32)] * 3,
        name=name,
        compiler_params=_params("parallel"),
    )(w, g, m, v)


_SHARDED = (("w_in", 2), ("w_mla_q_up", 2), ("w_mla_kv_up", 2), ("w_branch", 3), ("w_out", 1))
_REPLICATED = ("norm_g", "mla_q_norm_g", "mla_kv_norm_g", "gqa_q_norm_g", "gqa_k_norm_g", "win_sink", "t5_table",
               "final_norm_g")


def _pack(arrays, row_multiple):
    flat = jnp.concatenate([a.reshape(-1) for a in arrays])
    rows = -(-flat.shape[0] // (LANES * row_multiple)) * row_multiple
    return jnp.pad(flat, (0, rows * LANES - flat.shape[0])).reshape(rows, LANES)


def _unpack(packed, shapes):
    flat, out, at = packed.reshape(-1), [], 0
    for shp in shapes:
        n = int(np.prod(shp))
        out.append(flat[at:at + n].reshape(shp))
        at += n
    return out


_TO_WIRE = {
    "w_in": lambda t: jnp.swapaxes(t, 1, 2), "w_mla_q_up": lambda t: jnp.swapaxes(t, 1, 2),
    "w_mla_kv_up": lambda t: jnp.swapaxes(t, 1, 2),
    "w_branch": lambda t: jnp.transpose(t, (0, 3, 1, 2)).reshape(t.shape[0], t.shape[3], -1), "w_out": lambda t: t}
_FROM_WIRE = {
    "w_in": lambda t, shp: jnp.swapaxes(t, 1, 2), "w_mla_q_up": lambda t, shp: jnp.swapaxes(t, 1, 2),
    "w_mla_kv_up": lambda t, shp: jnp.swapaxes(t, 1, 2),
    "w_branch": lambda t, shp: jnp.transpose(t.reshape(shp[0], shp[3], shp[1], shp[2]), (0, 2, 3, 1)),
    "w_out": lambda t, shp: t}
_WIRE_NAME = {"w_in": "w_in_t", "w_mla_q_up": "w_q_t", "w_mla_kv_up": "w_kv_t", "w_branch": "w_branch_t",
              "w_out": "w_out"}


def _join_shards(gathered, wire_shapes):
    out, at = [], 0
    for depth, cut, rest in wire_shapes:
        n = depth * cut * rest // LANES
        blk = gathered[:, at:at + n].reshape(N_DEV, depth, cut, rest)
        out.append(jnp.moveaxis(blk, 0, 1).reshape(depth, N_DEV * cut, rest))
        at += n
    return out


def _split_shards(fulls, wire_shapes):
    parts = []
    for full, (depth, cut, rest) in zip(fulls, wire_shapes):
        blk = jnp.moveaxis(full.reshape(depth, N_DEV, cut, rest), 1, 0)
        parts.append(blk.reshape(N_DEV, depth * cut * rest // LANES, LANES))
    packed = jnp.concatenate(parts, axis=1)
    return packed.reshape((N_DEV // 2, 2) + packed.shape[1:])


def kernel(x, norm_g, w_in, mla_q_norm_g, mla_kv_norm_g, w_mla_q_up, w_mla_kv_up, gqa_q_norm_g, gqa_k_norm_g, win_sink, t5_table, w_branch, w_out, final_norm_g, loss_target, m_norm_g, m_w_in, m_mla_q_norm_g, m_mla_kv_norm_g, m_w_mla_q_up, m_w_mla_kv_up, m_gqa_q_norm_g, m_gqa_k_norm_g, m_win_sink, m_t5_table, m_w_branch, m_w_out, m_final_norm_g, v_norm_g, v_w_in, v_mla_q_norm_g, v_mla_kv_norm_g, v_w_mla_q_up, v_w_mla_kv_up, v_gqa_q_norm_g, v_gqa_k_norm_g, v_win_sink, v_t5_table, v_w_branch, v_w_out, v_final_norm_g):
    given = dict(locals())
    names = ("norm_g", "w_in", "mla_q_norm_g", "mla_kv_norm_g", "w_mla_q_up", "w_mla_kv_up", "gqa_q_norm_g",
             "gqa_k_norm_g", "win_sink", "t5_table", "w_branch", "w_out", "final_norm_g")
    shard_names = [n for n, _ in _SHARDED]
    shard_shapes = [given[n].shape for n in shard_names]

    wire = [_TO_WIRE[n](given[n]).astype(BF16) for n in shard_names]
    wire_shapes = [t.shape for t in wire]
    gathered = _all_gather(jnp.concatenate([t.reshape(-1, LANES) for t in wire]), "gather_weights")
    weights = {n: given[n] for n in _REPLICATED}
    weights.update(zip([_WIRE_NAME[n] for n in shard_names], _join_shards(gathered, wire_shapes)))

    loss, (gw, gx) = jax.value_and_grad(_local_loss, argnums=(0, 1))(weights, x[0], loss_target[0])
    loss = lax.psum(loss, ("x", "y", "c"))

    send = _split_shards([gw[_WIRE_NAME[n]] for n in shard_names], wire_shapes)
    partials = _add_sibling(send, _swap_with_sibling(send, "swap_grads"), "add_sibling_grads")
    g_wire = _unpack(_sum_slots(_exchange_chips(partials, "scatter_grads"), "sum_grads"), wire_shapes)
    g_shard = [_FROM_WIRE[n](t, shp) for n, t, shp in zip(shard_names, g_wire, shard_shapes)]
    rep_shapes = [given[n].shape for n in _REPLICATED]
    g_rep = _unpack(_sum_slots(_all_gather(_pack([gw[n] for n in _REPLICATED], 8), "gather_small_grads"),
                               "sum_small_grads"), rep_shapes)
    grads = dict(zip(shard_names, g_shard))
    grads.update(zip(_REPLICATED, g_rep))

    def update(group, shapes, row_multiple, name):
        outs = _adamw(*[_pack([src[n] for n in group], row_multiple) for src in (
            given, grads, {n: given["m_" + n] for n in group}, {n: given["v_" + n] for n in group})], name)
        return [dict(zip(group, _unpack(o, shapes))) for o in outs]

    big = update(shard_names, shard_shapes, 16, "adamw_shards")
    small = update(list(_REPLICATED), rep_shapes, 8, "adamw_replicated")
    delta, new_m, new_v = [{**b, **s_} for b, s_ in zip(big, small)]
    return (loss, gx[None], *[grads[n] for n in names], *[delta[n] for n in names],
            *[new_m[n] for n in names], *[new_v[n] for n in names])
```

```python
import functools
import math

import jax
import jax.numpy as jnp
import numpy as np
from jax import lax
from jax.experimental import pallas as pl
from jax.experimental.pallas import tpu as pltpu

F32 = jnp.float32
BF16 = jnp.bfloat16
N_DEV = 8
LANES = 128
HALF = LANES // 2
V7X_VMEM_LIMIT = 56 * 1024 * 1024

EPS = 1e-6
NEG_INF = -1e30
LOG2E = 1.4426950408889634
ROPE_THETA = 10000.0
GRID_W = 64
HEAD_DIM = 64
N_BRANCH = 4
BRANCH_W = 256
MLA_HEADS, MLA_Q_LORA, MLA_KV_LORA, MLA_NOPE, MLA_ROPE, MLA_V = 4, 256, 128, 64, 32, 64
MLA_QK = MLA_NOPE + MLA_ROPE
GQA_HEADS, GQA_KV_HEADS = 4, 2
DIL_PATTERNS = ((128, 1), (512, 4), (2048, 16))
DIL_HEADS = 4
WIN_HEADS, WIN_KV_HEADS, WIN_HALF = 4, 2, 128
T5_BUCKETS, T5_MAX_DIST = 32, 1024
BAND_BLOCK = 128
ADAM_LR, ADAM_B1, ADAM_B2, ADAM_EPS, ADAM_WD, ADAM_STEP = 0.001, 0.9, 0.999, 1e-08, 0.01, 10

D_MODEL = 1024
GM_W, SMALL_W, BAND_W = 5120, 512, 768
MLA_BLK, GQA_BLK, WIN_BLK, DIL_BLK = 10, 11, 8, 9
P_TOT = 7680
QW = 256


def _params(*sem):
    return pltpu.CompilerParams(dimension_semantics=sem, vmem_limit_bytes=V7X_VMEM_LIMIT)


def _pick(n, cands):
    for c in cands:
        if n % c == 0:
            return c
    return n


def _row_tile(rows, unit, cap):
    best = unit
    for t in range(unit, min(rows, cap) + 1, unit):
        if rows % t == 0:
            best = t
    assert rows % best == 0
    return best


def _dot(a, b, ca, cb):
    return lax.dot_general(a.astype(BF16), b.astype(BF16), (((ca,), (cb,)), ((), ())), preferred_element_type=F32)


def _bmm(a, b, ca, cb):
    return lax.dot_general(a, b, (((ca,), (cb,)), ((0,), (0,))), preferred_element_type=F32)


@jax.custom_vjp
def _bdot(a, b):
    return _dot(a, b, 1, 0)


def _bdot_fwd(a, b):
    return _dot(a, b, 1, 0), (a, b)


def _bdot_bwd(res, g):
    a, b = res
    return _dot(g, b, 1, 1), _dot(a, g, 0, 0)


_bdot.defvjp(_bdot_fwd, _bdot_bwd)


def _hdot(a, c):
    return lax.dot_general(a, c, (((1,), (0,)), ((), ())), precision=lax.Precision.HIGHEST, preferred_element_type=F32)


@functools.partial(jax.custom_vjp, nondiff_argnums=(1,))
def _lane_roll(x, shift):
    return pltpu.roll(x, shift, 1)


def _lane_roll_fwd(x, shift):
    return pltpu.roll(x, shift, 1), None


def _lane_roll_bwd(shift, _, g):
    return (pltpu.roll(g, g.shape[1] - shift, 1),)


_lane_roll.defvjp(_lane_roll_fwd, _lane_roll_bwd)


@functools.partial(jax.custom_vjp, nondiff_argnums=(1,))
def _lane_ranges(x, cut):
    bounds, _ = cut
    return tuple(x[:, lo:hi] for lo, hi in zip(bounds[:-1], bounds[1:]))


def _lane_ranges_fwd(x, cut):
    return _lane_ranges(x, cut), None


def _lane_ranges_bwd(cut, _, cts):
    bounds, width = cut
    parts = list(cts)
    if bounds[-1] < width:
        parts.append(jnp.zeros((cts[0].shape[0], width - bounds[-1]), cts[0].dtype))
    return (jnp.concatenate(parts, axis=1),)


_lane_ranges.defvjp(_lane_ranges_fwd, _lane_ranges_bwd)


def _lanes(x, bounds):
    return _lane_ranges(x, (tuple(bounds), x.shape[1]))


@jax.custom_vjp
def _unstack(x):
    return tuple(x[i] for i in range(x.shape[0]))


def _unstack_fwd(x):
    return _unstack(x), None


def _unstack_bwd(_, cts):
    return (jnp.stack(cts, axis=0),)


_unstack.defvjp(_unstack_fwd, _unstack_bwd)


@functools.partial(jax.custom_vjp, nondiff_argnums=(1,))
def _split_heads(x, h):
    d = x.shape[1] // h
    return jnp.stack([x[:, i * d:(i + 1) * d] for i in range(h)], axis=0)


def _split_heads_fwd(x, h):
    return _split_heads(x, h), None


def _split_heads_bwd(h, _, ct):
    return (jnp.concatenate([ct[i] for i in range(h)], axis=1),)


_split_heads.defvjp(_split_heads_fwd, _split_heads_bwd)


def _join_heads(x):
    return jnp.concatenate(_unstack(x), axis=1)


def _rope(x, cos_t, sin_t, half):
    w = x.shape[1]
    lane = lax.broadcasted_iota(jnp.int32, (1, w), 1)
    first = (lane % (2 * half)) < half
    partner = jnp.where(first, _lane_roll(x, w - half), _lane_roll(x, half))
    return x * cos_t + partner * sin_t


def _rms(x, g):
    return x * lax.rsqrt(jnp.mean(x * x, axis=-1, keepdims=True) + EPS) * g


def _rows(tr, w, col=0):
    return pl.BlockSpec((tr, w), lambda i: (i, col))


def _head_rows(h, tr, d):
    return pl.BlockSpec((h, tr, d), lambda i: (0, i, 0))


def _whole(shape):
    nd = len(shape)
    return pl.BlockSpec(tuple(shape), lambda i: (0,) * nd)


def _fwd_call(name, fn, steps, rows, params, aux, outs):
    nr, npar, na = len(rows), len(params), len(aux)

    def body(*refs):
        vals = [x[...].astype(F32) for x in refs[:nr + npar + na]]
        res = fn(vals[:nr], vals[nr:nr + npar], vals[nr + npar:])
        for o_ref, o in zip(refs[nr + npar + na:], res):
            o_ref[...] = o.astype(o_ref.dtype)

    return pl.pallas_call(
        body,
        grid=(steps,),
        in_specs=[s for _, s in rows] + [_whole(p.shape) for p in params] + [s for _, s in aux],
        out_specs=[e[1] for e in outs],
        out_shape=[jax.ShapeDtypeStruct(e[0], e[2] if len(e) > 2 else F32) for e in outs],
        name=name + "_fwd",
        compiler_params=_params("parallel"),
    )(*[a for a, _ in rows], *params, *[a for a, _ in aux])


def _vjp_call(name, fn, steps, rows, params, aux, cts, row_grads, into=None):
    nr, npar, na, nc = len(rows), len(params), len(aux), len(cts)
    n_in = nr + npar + na + nc
    lead = 0 if into is None else 1

    def body(*refs):
        refs = refs[lead:]
        vals = [x[...].astype(F32) for x in refs[:n_in]]
        r, p, a, d = vals[:nr], vals[nr:nr + npar], vals[nr + npar:nr + npar + na], vals[nr + npar + na:]
        out_refs = refs[n_in:]
        _, vjp = jax.vjp(lambda r_, p_: tuple(fn(r_, p_, a)), r, p)
        dr, dp = vjp(tuple(d))
        for o_ref, o in zip(out_refs[:nr], dr):
            o_ref[...] = o.astype(o_ref.dtype)

        @pl.when(pl.program_id(0) == 0)
        def _():
            for o_ref in out_refs[nr:]:
                o_ref[...] = jnp.zeros_like(o_ref)

        for o_ref, o in zip(out_refs[nr:], dp):
            o_ref[...] += o

    outs = pl.pallas_call(
        body,
        grid=(steps,),
        in_specs=([] if into is None else [pl.BlockSpec(memory_space=pl.ANY)])
        + [s for _, s in rows] + [_whole(p.shape) for p in params] + [s for _, s in aux] + [s for _, s in cts],
        out_specs=[e[1] for e in row_grads] + [_whole(p.shape) for p in params],
        out_shape=[jax.ShapeDtypeStruct(e[0], e[2] if len(e) > 2 else F32) for e in row_grads]
        + [jax.ShapeDtypeStruct(p.shape, F32) for p in params],
        input_output_aliases={} if into is None else {0: 0},
        name=name + "_bwd",
        compiler_params=_params("arbitrary"),
    )(*([] if into is None else [into]), *[a for a, _ in rows], *params, *[a for a, _ in aux], *[a for a, _ in cts])
    return list(outs[:nr]), list(outs[nr:])


def _norm_tile(r, p, a):
    return (_rms(r[0], p[0]),)


def _mm(a, b, mode, name, out_dtype=F32):
    if mode == "nn":
        (m, k), n = a.shape, b.shape[1]
    elif mode == "nt":
        (m, k), n = a.shape, b.shape[0]
    else:
        (k, m), n = a.shape, b.shape[1]
    tm = _pick(m, (1024, 512, 256, 128))
    tn = _pick(n, (1024, 768, 512, 384, 256, 128))
    budget = V7X_VMEM_LIMIT * 3 // 4
    for tk in (512, 256, 128) if mode == "tn" else (4096, 1024, 768, 512, 384, 256, 128):
        need = 2 * tk * (tm * a.dtype.itemsize + tn * b.dtype.itemsize) + tm * tn * (4 + 2 * np.dtype(out_dtype).itemsize)
        if k % tk == 0 and need <= budget:
            break
    nk = k // tk

    def body(a_ref, b_ref, o_ref, acc_ref):
        kk = pl.program_id(2)
        if mode == "nn":
            part = _dot(a_ref[...], b_ref[...], 1, 0)
        elif mode == "nt":
            part = _dot(a_ref[...], b_ref[...], 1, 1)
        else:
            part = _dot(a_ref[...], b_ref[...], 0, 0)
        if nk == 1:
            o_ref[...] = part.astype(o_ref.dtype)
        else:
            @pl.when(kk == 0)
            def _():
                acc_ref[...] = part

            @pl.when(kk > 0)
            def _():
                acc_ref[...] += part

            @pl.when(kk == nk - 1)
            def _():
                o_ref[...] = acc_ref[...].astype(o_ref.dtype)

    if mode == "nn":
        a_spec = pl.BlockSpec((tm, tk), lambda i, j, kk: (i, kk))
        b_spec = pl.BlockSpec((tk, tn), lambda i, j, kk: (kk, j))
    elif mode == "nt":
        a_spec = pl.BlockSpec((tm, tk), lambda i, j, kk: (i, kk))
        b_spec = pl.BlockSpec((tn, tk), lambda i, j, kk: (j, kk))
    else:
        a_spec = pl.BlockSpec((tk, tm), lambda i, j, kk: (kk, i))
        b_spec = pl.BlockSpec((tk, tn), lambda i, j, kk: (kk, j))
    return pl.pallas_call(
        body,
        grid=(m // tm, n // tn, nk),
        in_specs=[a_spec, b_spec],
        out_specs=pl.BlockSpec((tm, tn), lambda i, j, kk: (i, j)),
        out_shape=jax.ShapeDtypeStruct((m, n), out_dtype),
        scratch_shapes=[pltpu.VMEM((tm, tn), F32)],
        name=name,
        compiler_params=_params("parallel", "parallel", "arbitrary"),
    )(a, b)


def _dense_fwd_call(q, k, v, scale, name):
    n, sq, d = q.shape
    sk, dv = k.shape[1], v.shape[2]
    tq = _pick(sq, (256, 128))
    c = scale * LOG2E

    def body(q_ref, k_ref, v_ref, o_ref, lse_ref, k_s, vext_s):
        @pl.when(pl.program_id(1) == 0)
        def _():
            k_s[...] = k_ref[0].astype(BF16)
            vext_s[...] = jnp.ones_like(vext_s)
            vext_s[:, :dv] = v_ref[0].astype(BF16)

        s = _dot(q_ref[0], k_s[...], 1, 1)
        m = jnp.max(s, axis=1, keepdims=True)
        p = jnp.exp2(s * c - m * c)
        acc = _dot(p, vext_s[...], 1, 0)
        l = acc[:, dv:dv + 1]
        o_ref[0] = acc[:, :dv] / l
        lse_ref[0] = m * scale + jnp.log(l)

    return pl.pallas_call(
        body,
        grid=(n, sq // tq),
        in_specs=[
            pl.BlockSpec((1, tq, d), lambda h, i: (h, i, 0)),
            pl.BlockSpec((1, sk, d), lambda h, i: (h, 0, 0)),
            pl.BlockSpec((1, sk, dv), lambda h, i: (h, 0, 0)),
        ],
        out_specs=[
            pl.BlockSpec((1, tq, dv), lambda h, i: (h, i, 0)),
            pl.BlockSpec((1, tq, 1), lambda h, i: (h, i, 0)),
        ],
        out_shape=[jax.ShapeDtypeStruct((n, sq, dv), F32), jax.ShapeDtypeStruct((n, sq, 1), F32)],
        scratch_shapes=[pltpu.VMEM((sk, d), BF16), pltpu.VMEM((sk, 2 * dv), BF16)],
        name=name + "_fwd",
        compiler_params=_params("arbitrary", "arbitrary"),
    )(q, k, v)


def _dense_bwd_call(q, k, v, o, lse, do, scale, name):
    n, sq, d = q.shape
    sk, dv = k.shape[1], v.shape[2]
    tq, tk = _pick(sq, (512, 256, 128)), _pick(sk, (2048, 1024, 512, 256, 128))
    c = scale * LOG2E

    def body(q_ref, k_ref, v_ref, o_ref, lse_ref, do_ref, dq_ref, dk_ref, dv_ref):
        j, i = pl.program_id(1), pl.program_id(2)
        qb, kb, vb = q_ref[0].astype(BF16), k_ref[0].astype(BF16), v_ref[0].astype(BF16)
        do_f = do_ref[0]
        dob = do_f.astype(BF16)
        p = jnp.exp2(_dot(qb, kb, 1, 1) * c - lse_ref[0] * LOG2E)
        delta = jnp.sum(do_f * o_ref[0], axis=1, keepdims=True)
        ds = (p * (_dot(dob, vb, 1, 1) - delta)).astype(BF16)
        dv_part = _dot(p, dob, 0, 0)
        dk_part = _dot(ds, qb, 0, 0) * scale
        dq_part = _dot(ds, kb, 1, 0) * scale
        rows = pl.ds(pl.multiple_of(i * tq, tq), tq)

        @pl.when(i == 0)
        def _():
            dk_ref[0] = dk_part
            dv_ref[0] = dv_part

        @pl.when(i > 0)
        def _():
            dk_ref[0] += dk_part
            dv_ref[0] += dv_part

        @pl.when(j == 0)
        def _():
            dq_ref[0, rows, :] = dq_part

        @pl.when(j > 0)
        def _():
            dq_ref[0, rows, :] += dq_part

    return pl.pallas_call(
        body,
        grid=(n, sk // tk, sq // tq),
        in_specs=[
            pl.BlockSpec((1, tq, d), lambda h, j, i: (h, i, 0)),
            pl.BlockSpec((1, tk, d), lambda h, j, i: (h, j, 0)),
            pl.BlockSpec((1, tk, dv), lambda h, j, i: (h, j, 0)),
            pl.BlockSpec((1, tq, dv), lambda h, j, i: (h, i, 0)),
            pl.BlockSpec((1, tq, 1), lambda h, j, i: (h, i, 0)),
            pl.BlockSpec((1, tq, dv), lambda h, j, i: (h, i, 0)),
        ],
        out_specs=[
            pl.BlockSpec((1, sq, d), lambda h, j, i: (h, 0, 0)),
            pl.BlockSpec((1, tk, d), lambda h, j, i: (h, j, 0)),
            pl.BlockSpec((1, tk, dv), lambda h, j, i: (h, j, 0)),
        ],
        out_shape=[
            jax.ShapeDtypeStruct((n, sq, d), F32),
            jax.ShapeDtypeStruct((n, sk, d), F32),
            jax.ShapeDtypeStruct((n, sk, dv), F32),
        ],
        name=name + "_bwd",
        compiler_params=_params("arbitrary", "arbitrary", "arbitrary"),
    )(q, k, v, o, lse, do)


def _head_geometry(h, group):
    pair, a = divmod(h, 2)
    kv_pair, b = divmod(h // group, 2)
    return pair, a, kv_pair, b


def _lane_half():
    return lax.broadcasted_iota(jnp.int32, (1, LANES), 1) // HALF


def _align(x, a, b):
    if a != b:
        x = pltpu.roll(x, HALF, 1)
    return jnp.where(_lane_half() == b, x, 0.0)


def _unalign(x, a, b):
    x = jnp.where(_lane_half() == b, x, 0.0)
    return pltpu.roll(x, HALF, 1) if a != b else x


def _bands(w, pw, nw, lo, kvw, nb):
    b = BAND_BLOCK
    cat = jnp.concatenate([pw[:, lo:lo + kvw], w[:, lo:lo + kvw], nw[:, lo:lo + kvw]], axis=0).astype(BF16)
    out = []
    for g in range(kvw // LANES):
        c3 = cat[:, g * LANES:(g + 1) * LANES].reshape(nb + 2, b, LANES)
        out.append(jnp.concatenate([c3[0:nb], c3[1:nb + 1], c3[2:nb + 2]], axis=1))
    return out


def _edge_mask(first_block, nb, period):
    b = BAND_BLOCK
    blk = (first_block + lax.broadcasted_iota(jnp.int32, (nb, 1, 3 * b), 0)) % period
    col = lax.broadcasted_iota(jnp.int32, (nb, 1, 3 * b), 2)
    outside = ((col < b) & (blk == 0)) | ((col >= 2 * b) & (blk == period - 1))
    return jnp.where(outside, NEG_INF, 0.0)


def _band_geometry(proj, dil):
    rows = proj.shape[0]
    tl = _pick(rows, (1024, 512, 256, 128))
    return rows, tl, tl // BAND_BLOCK, rows // tl, rows // dil // BAND_BLOCK


def _band_in_specs(tl, nb, n_chunks, n_blocks, col, last_step_idle):
    def chunk(i):
        return jnp.minimum(i, n_chunks - 1) if last_step_idle else i

    main = pl.BlockSpec((tl, BAND_W), lambda j, i: (j * n_chunks + chunk(i), col))
    prev = pl.BlockSpec((BAND_BLOCK, BAND_W),
                        lambda j, i: (j * n_blocks + jnp.maximum(chunk(i) * nb - 1, 0), col))
    nxt = pl.BlockSpec((BAND_BLOCK, BAND_W),
                       lambda j, i: (j * n_blocks + jnp.minimum((chunk(i) + 1) * nb, n_blocks - 1), col))
    rows = pl.BlockSpec((tl, QW), lambda j, i: (j * n_chunks + chunk(i), 0))
    return main, prev, nxt, rows


def _band_fwd_call(proj, col, bias, sink, dil, group, kvw, scale, name):
    s_tok = proj.shape[0]
    seq, tl, nb, n_chunks, period = _band_geometry(proj, dil)
    n_blocks = seq // BAND_BLOCK
    heads = bias.shape[0]

    def body(w_ref, pw_ref, nw_ref, bias_ref, sink_ref, o_ref, lse_ref):
        i = pl.program_id(1)
        w, pw, nw = w_ref[...].astype(F32), pw_ref[...].astype(F32), nw_ref[...].astype(F32)
        kb = _bands(w, pw, nw, QW, kvw, nb)
        vb = _bands(w, pw, nw, QW + kvw, kvw, nb)
        edge = _edge_mask(i * nb, nb, period)
        o_acc = [jnp.zeros((tl, LANES), F32) for _ in range(heads // 2)]
        lse_acc = [jnp.zeros((tl, LANES), F32) for _ in range(heads // 2)]
        for h in range(heads):
            pair, a, kvp, b = _head_geometry(h, group)
            q_al = _align(w[:, pair * LANES:(pair + 1) * LANES], a, b).astype(BF16).reshape(nb, BAND_BLOCK, LANES)
            logits = _bmm(q_al, kb[kvp], 2, 2) * scale + bias_ref[h][None] + edge
            sk = sink_ref[h].reshape(1, 1, 1)
            m = jnp.maximum(jnp.max(logits, axis=2, keepdims=True), sk)
            e = jnp.exp(logits - m)
            ssum = jnp.sum(e, axis=2, keepdims=True) + jnp.exp(sk - m)
            out = _bmm(e.astype(BF16), vb[kvp], 2, 1) / ssum
            o_acc[pair] = o_acc[pair] + _unalign(out.reshape(tl, LANES), a, b)
            lse = (m + jnp.log(ssum)).reshape(tl, 1)
            lse_acc[pair] = lse_acc[pair] + jnp.where(_lane_half() == a, lse, 0.0)
        o_ref[...] = jnp.concatenate(o_acc, axis=1)
        lse_ref[...] = jnp.concatenate(lse_acc, axis=1)

    main, prev, nxt, rows = _band_in_specs(tl, nb, n_chunks, n_blocks, col, False)
    return pl.pallas_call(
        body,
        grid=(1, n_chunks),
        in_specs=[main, prev, nxt, pl.BlockSpec(bias.shape, lambda j, i: (0, 0, 0)),
                  pl.BlockSpec(sink.shape, lambda j, i: (0, 0, 0))],
        out_specs=[rows, rows],
        out_shape=[jax.ShapeDtypeStruct((s_tok, QW), F32)] * 2,
        name=name + "_fwd",
        compiler_params=_params("parallel", "parallel"),
    )(proj, proj, proj, bias, sink)


def _band_bwd_call(proj, o, do, lse, dlse, bias, sink, dproj, col, dil, group, kvw, scale, name):
    seq, tl, nb, n_chunks, period = _band_geometry(proj, dil)
    lead = 0 if dproj is None else 1
    n_blocks = seq // BAND_BLOCK
    heads = bias.shape[0]
    b_ = BAND_BLOCK
    have_dlse = dlse is not None

    def body(*refs):
        (w_ref, pw_ref, nw_ref, o_ref, do_ref, lse_ref), refs = refs[lead:lead + 6], refs[lead + 6:]
        if have_dlse:
            dlse_ref, refs = refs[0], refs[1:]
        bias_ref, sink_ref, dwin_ref, dbias_ref, dsink_ref, dq_s, dk_s, dv_s = refs
        j, i = pl.program_id(0), pl.program_id(1)

        @pl.when((j == 0) & (i == 0))
        def _():
            dbias_ref[...] = jnp.zeros_like(dbias_ref)
            dsink_ref[...] = jnp.zeros_like(dsink_ref)

        @pl.when(i == 0)
        def _():
            dk_s[...] = jnp.zeros_like(dk_s)
            dv_s[...] = jnp.zeros_like(dv_s)

        @pl.when(i < n_chunks)
        def _():
            w, pw, nw = w_ref[...].astype(F32), pw_ref[...].astype(F32), nw_ref[...].astype(F32)
            kb = _bands(w, pw, nw, QW, kvw, nb)
            vb = _bands(w, pw, nw, QW + kvw, kvw, nb)
            edge = _edge_mask(i * nb, nb, period)
            dq_acc = [jnp.zeros((tl, LANES), F32) for _ in range(heads // 2)]
            for h in range(heads):
                pair, a, kvp, b = _head_geometry(h, group)
                lanes = slice(pair * LANES, (pair + 1) * LANES)
                mine = _lane_half() == a
                q_al = _align(w[:, lanes], a, b).astype(BF16).reshape(nb, b_, LANES)
                do_al = _align(do_ref[:, lanes], a, b).astype(BF16).reshape(nb, b_, LANES)
                lse_h = jnp.max(jnp.where(mine, lse_ref[:, lanes], NEG_INF), axis=1, keepdims=True)
                shift = -jnp.sum(jnp.where(mine, do_ref[:, lanes] * o_ref[:, lanes], 0.0), axis=1, keepdims=True)
                if have_dlse:
                    shift = shift + jnp.sum(jnp.where(mine, dlse_ref[:, lanes], 0.0), axis=1, keepdims=True)
                logits = _bmm(q_al, kb[kvp], 2, 2) * scale + bias_ref[h][None] + edge
                p = jnp.exp(logits - lse_h.reshape(nb, b_, 1))
                dlogits = p * (_bmm(do_al, vb[kvp], 2, 2) + shift.reshape(nb, b_, 1))
                dbias_ref[h] += jnp.sum(dlogits, axis=0)
                dsink_ref[h] += jnp.sum(jnp.exp(sink_ref[h] - lse_h) * shift, axis=0, keepdims=True)
                ds = (dlogits * scale).astype(BF16)
                dq_acc[pair] = dq_acc[pair] + _unalign(_bmm(ds, kb[kvp], 2, 1).reshape(tl, LANES), a, b)
                dk_band = _bmm(ds, q_al, 1, 1)
                dv_band = _bmm(p.astype(BF16), do_al, 1, 1)
                kv_lanes = slice(kvp * LANES, (kvp + 1) * LANES)
                for t in range(3):
                    at = pl.ds(pl.multiple_of(i * tl + t * b_, b_), tl)
                    dk_s[at, kv_lanes] += dk_band[:, t * b_:(t + 1) * b_, :].reshape(tl, LANES)
                    dv_s[at, kv_lanes] += dv_band[:, t * b_:(t + 1) * b_, :].reshape(tl, LANES)
            dq_s[lax.rem(i, 2)] = jnp.concatenate(dq_acc, axis=1)

        @pl.when(i >= 1)
        def _():
            at = pl.ds(pl.multiple_of((i - 1) * tl + b_, b_), tl)
            parts = [dq_s[lax.rem(i + 1, 2)], dk_s[at, :], dv_s[at, :]]
            if QW + 2 * kvw < BAND_W:
                parts.append(jnp.zeros((tl, BAND_W - QW - 2 * kvw), F32))
            dwin_ref[...] = jnp.concatenate(parts, axis=1).astype(dwin_ref.dtype)

    main, prev, nxt, rows = _band_in_specs(tl, nb, n_chunks, n_blocks, col, True)
    row_args = [o, do, lse] + ([dlse] if have_dlse else [])
    small = [pl.BlockSpec(bias.shape, lambda j, i: (0, 0, 0)), pl.BlockSpec(sink.shape, lambda j, i: (0, 0, 0))]
    return pl.pallas_call(
        body,
        grid=(1, n_chunks + 1),
        in_specs=[pl.BlockSpec(memory_space=pl.ANY)] * lead + [main, prev, nxt] + [rows] * len(row_args) + small,
        out_specs=[pl.BlockSpec((tl, BAND_W), lambda j, i: (j * n_chunks + jnp.maximum(i - 1, 0), col))] + small,
        out_shape=[jax.ShapeDtypeStruct(proj.shape, BF16), jax.ShapeDtypeStruct(bias.shape, F32),
                   jax.ShapeDtypeStruct(sink.shape, F32)],
        scratch_shapes=[pltpu.VMEM((2, tl, QW), F32), pltpu.VMEM((seq + 2 * b_, kvw), F32),
                        pltpu.VMEM((seq + 2 * b_, kvw), F32)],
        input_output_aliases={0: 0} if lead else {},
        name=name + "_bwd",
        compiler_params=_params("arbitrary", "arbitrary"),
    )(*([dproj] if lead else []), proj, proj, proj, *row_args, bias, sink)


def _loss_call(x, target, g):
    s, d = x.shape
    tr = _pick(s, (256, 128, 64, 32, 16, 8))

    def tile_loss(xt, gt, tt):
        err = jnp.square(_rms(xt, gt) - tt)
        return 0.5 * jnp.sum(jnp.mean(err, axis=-1, keepdims=True), axis=0, keepdims=True)

    def body(x_ref, t_ref, g_ref, loss_ref, dx_ref, dg_ref):
        tt = t_ref[...]
        val, vjp = jax.vjp(lambda xt, gt: tile_loss(xt, gt, tt), x_ref[...], g_ref[...])
        dx, dg = vjp(jnp.ones_like(val))
        dx_ref[...] = dx

        @pl.when(pl.program_id(0) == 0)
        def _():
            loss_ref[...] = jnp.zeros_like(loss_ref)
            dg_ref[...] = jnp.zeros_like(dg_ref)

        loss_ref[...] += val
        dg_ref[...] += dg

    return pl.pallas_call(
        body,
        grid=(s // tr,),
        in_specs=[_rows(tr, d), _rows(tr, d), _whole((1, d))],
        out_specs=[_whole((1, 1)), _rows(tr, d), _whole((1, d))],
        out_shape=[jax.ShapeDtypeStruct((1, 1), F32), jax.ShapeDtypeStruct((s, d), F32),
                   jax.ShapeDtypeStruct((1, d), F32)],
        name="final_norm_loss",
        compiler_params=_params("arbitrary"),
    )(x, target, g)


@jax.custom_vjp
def _loss_op(x, target, g):
    return _loss_call(x, target, g)[0][0, 0]


def _loss_op_fwd(x, target, g):
    loss, dx, dg = _loss_call(x, target, g)
    return loss[0, 0], (dx, dg, target)


def _loss_op_bwd(res, ct):
    dx, dg, target = res
    return ct * dx, jnp.zeros_like(target), ct * dg


_loss_op.defvjp(_loss_op_fwd, _loss_op_bwd)


def _mla_tile(r, p, a):
    g_q, g_kv, w_q, w_k, w_v = p
    cos_t, sin_t, place_kr = a
    a_q, a_kv, a_kr = _lanes(r[0], (0, MLA_Q_LORA, MLA_Q_LORA + MLA_KV_LORA, MLA_Q_LORA + MLA_KV_LORA + MLA_ROPE))
    q = _rope(_bdot(_rms(a_q, g_q), w_q), cos_t, sin_t, MLA_ROPE // 2)
    ckv = _rms(a_kv, g_kv)
    k = _rope(_bdot(ckv, w_k) + _hdot(a_kr, place_kr), cos_t, sin_t, MLA_ROPE // 2)
    return _split_heads(q, MLA_HEADS), _split_heads(k, MLA_HEADS), _split_heads(_bdot(ckv, w_v), MLA_HEADS)


def _head_rms(x, g, head_mean):
    return x * lax.rsqrt(_hdot(x * x, head_mean) + EPS) * g


def _gqa_tile(r, p, a):
    g_q, g_k = p
    cos_t, sin_t, mean_q, mean_k = a
    wq, wk = GQA_HEADS * HEAD_DIM, GQA_KV_HEADS * HEAD_DIM
    b_q, b_k, b_v = _lanes(r[0], (0, wq, wq + wk, wq + 2 * wk))
    q = _rope(_head_rms(b_q, g_q, mean_q), cos_t, sin_t, HEAD_DIM // 4)
    k = _rope(_head_rms(b_k, g_k, mean_k), cos_t[:, :wk], sin_t[:, :wk], HEAD_DIM // 4)
    return _split_heads(q, GQA_HEADS), _split_heads(k, GQA_KV_HEADS), _split_heads(b_v, GQA_KV_HEADS)


def _permute_rows(p, x, cp):
    pb = p.astype(BF16)
    hi = x.astype(BF16)
    rest = x - hi.astype(F32)
    mid = rest.astype(BF16)
    low = (rest - mid.astype(F32)).astype(BF16)
    dims = (((cp,), (0,)), ((), ()))
    return (lax.dot_general(pb, hi, dims, preferred_element_type=F32)
            + lax.dot_general(pb, mid, dims, preferred_element_type=F32)
            + lax.dot_general(pb, low, dims, preferred_element_type=F32))


@jax.custom_vjp
def _permuted(p, x):
    return _permute_rows(p, x, 1)


def _permuted_fwd(p, x):
    return _permute_rows(p, x, 1), p


def _permuted_bwd(p, ct):
    return jnp.zeros_like(p), _permute_rows(p, ct, 0)


_permuted.defvjp(_permuted_fwd, _permuted_bwd)


def _interleave(p, x):
    return _permuted(p, x.reshape(x.shape[0] * x.shape[1], x.shape[2]))


def _interleave_matrix(rows, dil):
    p = np.zeros((rows, rows), np.float32)
    for t in range(rows):
        p[t, (t % dil) * (rows // dil) + t // dil] = 1.0
    return p


def _merge_tile(r, p, a):
    gm, o_a, o_b, oc0, oc1, oc2, l0, l1, l2, o_d = r
    (w_branch,) = p
    perm1, perm2 = a
    oc1, l1, oc2, l2 = _interleave(perm1, oc1), _interleave(perm1, l1), _interleave(perm2, oc2), _interleave(perm2, l2)
    d = w_branch.shape[2]
    gate_path, merge_logits = _lanes(gm, (0, N_BRANCH * BRANCH_W, N_BRANCH * BRANCH_W + N_BRANCH * d))
    m = jnp.maximum(jnp.maximum(l0, l1), l2)
    e0, e1, e2 = jnp.exp(l0 - m), jnp.exp(l1 - m), jnp.exp(l2 - m)
    y_c = (e0 * oc0 + e1 * oc1 + e2 * oc2) / (e0 + e1 + e2)
    y = jnp.concatenate([_join_heads(o_a), _join_heads(o_b), y_c, o_d], axis=1)
    u = y * (gate_path * jax.nn.sigmoid(gate_path))
    gates = _lanes(merge_logits, tuple(range(0, N_BRANCH * d + 1, d)))
    us = _lanes(u, tuple(range(0, N_BRANCH * BRANCH_W + 1, BRANCH_W)))
    branch_w = _unstack(w_branch)
    out = None
    for nb in range(N_BRANCH):
        term = jax.nn.sigmoid(gates[nb]) * _bdot(us[nb], branch_w[nb])
        out = term if out is None else out + term
    return (out,)


def _mixer_calls(proj, prm, aux):
    s = proj.shape[0]
    tr, tm = _pick(s, (256, 128)), _pick(s, (128,))
    mla_cos, mla_sin, gqa_cos, gqa_sin, place_kr, mean_q, mean_k = aux[:7]
    wq = MLA_HEADS * MLA_QK
    mla = dict(
        steps=s // tr, rows=[(proj, _rows(tr, SMALL_W, MLA_BLK))],
        params=[prm["g_q"], prm["g_kv"], prm["w_q"], prm["w_k"], prm["w_v"]],
        aux=[(mla_cos, _rows(tr, wq)), (mla_sin, _rows(tr, wq)), (place_kr, _whole(place_kr.shape))],
        outs=[((MLA_HEADS, s, MLA_QK), _head_rows(MLA_HEADS, tr, MLA_QK))] * 2
        + [((MLA_HEADS, s, MLA_V), _head_rows(MLA_HEADS, tr, MLA_V))],
        window=((s, P_TOT), _rows(tr, SMALL_W, MLA_BLK), BF16))
    wg = GQA_HEADS * HEAD_DIM
    gqa = dict(
        steps=s // tr, rows=[(proj, _rows(tr, SMALL_W, GQA_BLK))], params=[prm["gq"], prm["gk"]],
        aux=[(gqa_cos, _rows(tr, wg)), (gqa_sin, _rows(tr, wg)), (mean_q, _whole(mean_q.shape)),
             (mean_k, _whole(mean_k.shape))],
        outs=[((GQA_HEADS, s, HEAD_DIM), _head_rows(GQA_HEADS, tr, HEAD_DIM))]
        + [((GQA_KV_HEADS, s, HEAD_DIM), _head_rows(GQA_KV_HEADS, tr, HEAD_DIM))] * 2,
        window=((s, P_TOT), _rows(tr, SMALL_W, GQA_BLK), BF16))
    merge = dict(steps=s // tm, tm=tm, window=((s, P_TOT), _rows(tm, GM_W, 0), BF16))
    return mla, gqa, merge


def _merge_rows(proj, o_a, o_b, ocs, lses, o_d, tm):
    h4 = _head_rows(4, tm, HEAD_DIM)
    s = proj.shape[0]

    def by_residue(t, dil):
        if dil == 1:
            return t, _rows(tm, QW)
        return t.reshape(dil, s // dil, QW), pl.BlockSpec((dil, tm // dil, QW), lambda i: (0, i, 0))

    dils = [dil for _, dil in DIL_PATTERNS]
    return ([(proj, _rows(tm, GM_W, 0)), (o_a, h4), (o_b, h4)] + [by_residue(t, r) for t, r in zip(ocs, dils)]
            + [by_residue(t, r) for t, r in zip(lses, dils)] + [(o_d, _rows(tm, QW))])


def _merge_aux(aux):
    return [(t, _whole(t.shape)) for t in aux[7:9]]


def _to_residues(t, dil):
    s, w = t.shape
    return t if dil == 1 else t.reshape(s // dil, dil, w).transpose(1, 0, 2).reshape(s, w)


def _from_residues(t, dil):
    s, w = t.shape
    return t if dil == 1 else t.reshape(dil, s // dil, w).transpose(1, 0, 2).reshape(s, w)


def _mixer_fwd(projs, prm, aux):
    proj = projs[0]
    s = proj.shape[0]
    mla, gqa, merge = _mixer_calls(proj, prm, aux)
    q_a, k_a, v_a = _fwd_call("prep_mla", _mla_tile, mla["steps"], mla["rows"], mla["params"], mla["aux"], mla["outs"])
    o_a, lse_a = _dense_fwd_call(q_a, k_a, v_a, MLA_QK ** -0.5, "mla")
    q_b, k_b, v_b = _fwd_call("prep_gqa", _gqa_tile, gqa["steps"], gqa["rows"], gqa["params"], gqa["aux"], gqa["outs"])
    grp = GQA_HEADS // GQA_KV_HEADS
    o_b, lse_b = _dense_fwd_call(q_b.reshape(GQA_KV_HEADS, grp * s, HEAD_DIM), k_b, v_b, HEAD_DIM ** -0.5, "gqa")
    scale = HEAD_DIM ** -0.5
    ocs, lses = [], []
    for gi, (_, dil) in enumerate(DIL_PATTERNS):
        o, lse = _band_fwd_call(projs[gi], DIL_BLK if gi == 0 else 0, prm["bias_dil"][gi], prm["no_sink"], dil, 1,
                                QW, scale, "dil%d" % gi)
        ocs.append(o)
        lses.append(lse)
    o_d, lse_d = _band_fwd_call(proj, WIN_BLK, prm["bias_win"], prm["sink"], 1, WIN_HEADS // WIN_KV_HEADS,
                                WIN_KV_HEADS * HEAD_DIM, scale, "win")
    rows = _merge_rows(proj, o_a, o_b.reshape(GQA_HEADS, s, HEAD_DIM), ocs, lses, o_d, merge["tm"])
    mix = _fwd_call("merge", _merge_tile, merge["steps"], rows, [prm["w_branch"]], _merge_aux(aux),
                    [((s, prm["w_branch"].shape[2]), _rows(merge["tm"], prm["w_branch"].shape[2]), BF16)])[0]
    return mix, (q_a, k_a, v_a, o_a, lse_a, q_b, k_b, v_b, o_b, lse_b, ocs, lses, o_d, lse_d)


def _mixer_bwd(projs, prm, aux, saved, dmix):
    proj = projs[0]
    s = proj.shape[0]
    q_a, k_a, v_a, o_a, lse_a, q_b, k_b, v_b, o_b, lse_b, ocs, lses, o_d, lse_d = saved
    dils = [dil for _, dil in DIL_PATTERNS]
    mla, gqa, merge = _mixer_calls(proj, prm, aux)
    tm, d_model = merge["tm"], prm["w_branch"].shape[2]
    grp = GQA_HEADS // GQA_KV_HEADS
    scale = HEAD_DIM ** -0.5

    rows = _merge_rows(proj, o_a, o_b.reshape(GQA_HEADS, s, HEAD_DIM), ocs, lses, o_d, tm)
    grads, (dw_branch,) = _vjp_call(
        "merge", _merge_tile, merge["steps"], rows, [prm["w_branch"]], _merge_aux(aux), [(dmix, _rows(tm, d_model))],
        [merge["window"]] + [(a.shape, spec) for a, spec in rows[1:]])
    dproj, do_a, do_b, docs, dlses, do_d = grads[0], grads[1], grads[2], grads[3:6], grads[6:9], grads[9]

    dq_a, dk_a, dv_a = _dense_bwd_call(q_a, k_a, v_a, o_a, lse_a, do_a, MLA_QK ** -0.5, "mla")
    (dproj,), dmla = _vjp_call("prep_mla", _mla_tile, mla["steps"], mla["rows"], mla["params"], mla["aux"],
                               [(t, spec) for t, (_, spec) in zip((dq_a, dk_a, dv_a), mla["outs"])],
                               [mla["window"]], into=dproj)
    dq_b, dk_b, dv_b = _dense_bwd_call(q_b.reshape(GQA_KV_HEADS, grp * s, HEAD_DIM), k_b, v_b, o_b, lse_b,
                                       do_b.reshape(GQA_KV_HEADS, grp * s, HEAD_DIM), scale, "gqa")
    (dproj,), dgqa = _vjp_call("prep_gqa", _gqa_tile, gqa["steps"], gqa["rows"], gqa["params"], gqa["aux"],
                               [(t, spec) for t, (_, spec) in zip((dq_b.reshape(GQA_HEADS, s, HEAD_DIM), dk_b, dv_b),
                                                                  gqa["outs"])],
                               [gqa["window"]], into=dproj)
    dproj, dbias_win, dsink = _band_bwd_call(proj, o_d, do_d, lse_d, None, prm["bias_win"], prm["sink"], dproj,
                                             WIN_BLK, 1, WIN_HEADS // WIN_KV_HEADS, WIN_KV_HEADS * HEAD_DIM, scale, "win")
    dbias_dil, dprojs = [], []
    for gi, dil in enumerate(dils):
        dside, dbias, _ = _band_bwd_call(
            projs[gi], ocs[gi], docs[gi].reshape(s, QW), lses[gi], dlses[gi].reshape(s, QW),
            prm["bias_dil"][gi], prm["no_sink"], dproj if gi == 0 else None, DIL_BLK if gi == 0 else 0, dil, 1, QW,
            scale, "dil%d" % gi)
        if gi == 0:
            dproj = dside
        else:
            dprojs.append(dside)
        dbias_dil.append(dbias)
    dprm = dict(g_q=dmla[0], g_kv=dmla[1], w_q=dmla[2], w_k=dmla[3], w_v=dmla[4], gq=dgqa[0], gk=dgqa[1],
                bias_dil=dbias_dil, bias_win=dbias_win, sink=dsink, no_sink=jnp.zeros_like(prm["no_sink"]),
                w_branch=dw_branch)
    return [dproj] + dprojs, {k: jax.tree.map(lambda g, p: g.astype(p.dtype), v, prm[k]) for k, v in dprm.items()}


def _layer_fwd(x, w, aux):
    s, d = x.shape
    tr = _pick(s, (256,))
    dils = [dil for _, dil in DIL_PATTERNS]

    def norm_forms(r, p, a):
        y = _rms(r[0], p[0])
        return [y, y.T] + [_dot(q, y, 1, 0).reshape(dil, tr // dil, d) for q, dil in zip(a, dils[1:])]

    forms = _fwd_call(
        "norm", norm_forms, s // tr, [(x, _rows(tr, d))], [w["norm_g"]], [(q, _whole(q.shape)) for q in aux[9:11]],
        [((s, d), _rows(tr, d), BF16), ((d, s), pl.BlockSpec((d, tr), lambda i: (0, i)), BF16)]
        + [((dil, s // dil, d), pl.BlockSpec((dil, tr // dil, d), lambda i: (0, i, 0)), BF16) for dil in dils[1:]])
    xn_t, xns = forms[1], [forms[0]] + [t.reshape(s, d) for t in forms[2:]]
    projs = [_mm(a, b, "nt", "proj%d_fwd" % i, BF16) for i, (a, b) in enumerate(zip(xns, w["w_in_t"]))]
    mix, saved = _mixer_fwd(projs, w["mixer"], aux)
    return _mm(mix, w["w_out"], "nn", "out_proj_nn"), (x, w, aux, xns, xn_t, projs, mix, saved)


@jax.custom_vjp
def _layer_core(x, w, aux):
    return _layer_fwd(x, w, aux)[0]


def _layer_core_bwd(res, dout):
    x, w, aux, xns, xn_t, projs, mix, saved = res
    s, d = x.shape
    tr = _pick(s, (256, 128, 64, 32, 16, 8))
    dils = [dil for _, dil in DIL_PATTERNS]
    dmix = _mm(dout, w["w_out"], "nt", "out_proj_nt")
    dw_out = _mm(mix, dout, "tn", "out_proj_tn", w["w_out"].dtype)
    dprojs, dmixer = _mixer_bwd(projs, w["mixer"], aux, saved, dmix)
    dxn = None
    for i, (dp, wi, r) in enumerate(zip(dprojs, w["w_in_t"], dils)):
        part = _mm(_from_residues(dp, r), wi, "nn", "proj%d_dx" % i)
        dxn = part if dxn is None else dxn + part
    dw_in_t = [_mm(xn_t, dprojs[0], "nn", "proj0_dw", w["w_in_t"][0].dtype).T]
    dw_in_t += [_mm(dp, a, "tn", "proj%d_dw" % i, wi.dtype)
                for i, (a, dp, wi) in list(enumerate(zip(xns, dprojs, w["w_in_t"])))[1:]]
    (dx,), (dg,) = _vjp_call("norm", _norm_tile, s // tr, [(x, _rows(tr, d))], [w["norm_g"]], [],
                             [(dxn, _rows(tr, d))], [((s, d), _rows(tr, d))])
    dw = dict(norm_g=dg, w_in_t=dw_in_t, mixer=dmixer, w_out=dw_out)
    return dx, dw, tuple(jnp.zeros_like(t) for t in aux)


_layer_core.defvjp(lambda x, w, aux: _layer_fwd(x, w, aux), _layer_core_bwd)


def _rope_angles(pos, dim):
    inv = ROPE_THETA ** (-jnp.arange(0, dim, 2, dtype=F32) / dim)
    return pos.astype(F32)[:, None] * inv[None, :]


def _rope_tables(s):
    pos = jnp.arange(s, dtype=jnp.int32)
    rows = s // GRID_W
    row_idx = jnp.repeat(jnp.arange(rows, dtype=jnp.int32), GRID_W)
    col_idx = jnp.tile(jnp.arange(GRID_W, dtype=jnp.int32), rows)
    a1 = _rope_angles(pos, MLA_ROPE)
    ar = _rope_angles(row_idx, HEAD_DIM // 2)
    ac = _rope_angles(col_idx, HEAD_DIM // 2)
    ones, zeros = jnp.ones((s, MLA_NOPE), F32), jnp.zeros((s, MLA_NOPE), F32)
    mla_cos = jnp.tile(jnp.concatenate([ones, jnp.cos(a1), jnp.cos(a1)], axis=1), (1, MLA_HEADS))
    mla_sin = jnp.tile(jnp.concatenate([zeros, -jnp.sin(a1), jnp.sin(a1)], axis=1), (1, MLA_HEADS))
    gqa_cos = jnp.tile(jnp.concatenate([jnp.cos(ar), jnp.cos(ar), jnp.cos(ac), jnp.cos(ac)], axis=1), (1, GQA_HEADS))
    gqa_sin = jnp.tile(jnp.concatenate([-jnp.sin(ar), jnp.sin(ar), -jnp.sin(ac), jnp.sin(ac)], axis=1), (1, GQA_HEADS))
    return mla_cos, mla_sin, gqa_cos, gqa_sin


def _t5_bucket(rel):
    nb = T5_BUCKETS // 2
    max_exact = nb // 2
    n = jnp.abs(rel)
    nf = jnp.maximum(n, 1).astype(F32)
    large = max_exact + (jnp.log(nf / max_exact) / math.log(T5_MAX_DIST / max_exact) * (nb - max_exact)).astype(jnp.int32)
    large = jnp.minimum(large, nb - 1)
    return jnp.where(rel > 0, nb, 0) + jnp.where(n < max_exact, n, large)


def _band_bias(table, stride, head_lo, heads, half_window):
    b = BAND_BLOCK
    offs = jnp.arange(3 * b)[None, :] - b - jnp.arange(b)[:, None]
    one_hot = (_t5_bucket(offs * stride)[..., None] == jnp.arange(T5_BUCKETS)).astype(F32)
    bias = jnp.dot(one_hot.reshape(b * 3 * b, T5_BUCKETS), table[:, head_lo:head_lo + heads],
                   precision=lax.Precision.HIGHEST)
    bias = bias.T.reshape(heads, b, 3 * b)
    return jnp.where((jnp.abs(offs) <= half_window)[None], bias, NEG_INF)


def _w_in_rows(d):
    mla, gqa, win, dil0 = MLA_BLK * SMALL_W, GQA_BLK * SMALL_W, WIN_BLK * BAND_W, DIL_BLK * BAND_W
    plan, at = [], 0
    for width, target, row in ((256, 0, mla), (128, 0, mla + 256), (32, 0, mla + 384),
                               (256, 0, gqa), (128, 0, gqa + 256), (128, 0, gqa + 384)):
        plan.append((at, width, target, row))
        at += width
    for part in range(3):
        for g in range(len(DIL_PATTERNS)):
            plan.append((at, QW, g, (dil0 if g == 0 else 0) + part * QW))
            at += QW
    for width, row in ((256, win), (128, win + 256), (128, win + 384), (N_BRANCH * BRANCH_W, 0),
                       (N_BRANCH * d, N_BRANCH * BRANCH_W)):
        plan.append((at, width, 0, row))
        at += width
    return plan


@jax.custom_vjp
def _w_in_layout(w_in_t):
    d = w_in_t.shape[1]
    outs = []
    for target, rows in enumerate((P_TOT, BAND_W, BAND_W)):
        parts, at = [], 0
        for start, width, _, row in sorted((p for p in _w_in_rows(d) if p[2] == target), key=lambda p: p[3]):
            if row > at:
                parts.append(jnp.zeros((row - at, d), w_in_t.dtype))
            parts.append(w_in_t[start:start + width])
            at = row + width
        if at < rows:
            parts.append(jnp.zeros((rows - at, d), w_in_t.dtype))
        outs.append(jnp.concatenate(parts, axis=0))
    return outs


def _w_in_layout_fwd(w_in_t):
    return _w_in_layout(w_in_t), None


def _w_in_layout_bwd(_, cts):
    d = cts[0].shape[1]
    return (jnp.concatenate([cts[target][row:row + width] for _, width, target, row in _w_in_rows(d)], axis=0),)


_w_in_layout.defvjp(_w_in_layout_fwd, _w_in_layout_bwd)


def _layer(x, w, l, aux, biases):
    w_kv = w["w_kv_t"][l].T.reshape(MLA_KV_LORA, MLA_HEADS, MLA_NOPE + MLA_V)
    w_k = jnp.concatenate([w_kv[:, :, :MLA_NOPE], jnp.zeros((MLA_KV_LORA, MLA_HEADS, MLA_ROPE), w_kv.dtype)], axis=2)
    dil_bias, win_bias = biases
    prm = dict(
        g_q=w["mla_q_norm_g"][l][None, :], g_kv=w["mla_kv_norm_g"][l][None, :], w_q=w["w_q_t"][l].T,
        w_k=w_k.reshape(MLA_KV_LORA, MLA_HEADS * MLA_QK),
        w_v=w_kv[:, :, MLA_NOPE:].reshape(MLA_KV_LORA, MLA_HEADS * MLA_V),
        gq=jnp.tile(w["gqa_q_norm_g"][l], GQA_HEADS)[None, :], gk=jnp.tile(w["gqa_k_norm_g"][l], GQA_KV_HEADS)[None, :],
        bias_dil=list(dil_bias), bias_win=win_bias, sink=w["win_sink"][l].reshape(WIN_HEADS, 1, 1),
        no_sink=jnp.full((DIL_HEADS, 1, 1), NEG_INF, F32), w_branch=jnp.transpose(w["w_branch_t"][l].reshape(-1, N_BRANCH, BRANCH_W), (1, 2, 0)))
    layer_w = dict(norm_g=w["norm_g"][l][None, :], w_in_t=_w_in_layout(w["w_in_t"][l]), mixer=prm, w_out=w["w_out"][l])
    return x + _layer_core(x, layer_w, aux)


def _local_loss(w, x, target):
    s, d_model = x.shape
    assert d_model == D_MODEL, "the projection's window layout is laid out for d_model 1024"
    place = np.zeros((MLA_ROPE, MLA_HEADS * MLA_QK), np.float32)
    for h in range(MLA_HEADS):
        for i in range(MLA_ROPE):
            place[i, h * MLA_QK + MLA_NOPE + i] = 1.0

    def head_mean(nh):
        m = np.kron(np.eye(nh, dtype=np.float32), np.full((HEAD_DIM, HEAD_DIM), 1.0 / HEAD_DIM, np.float32))
        return jnp.asarray(m)

    merge_tile = _pick(s, (128,))
    norm_tile = _pick(s, (256,))
    aux = _rope_tables(s) + (jnp.asarray(place), head_mean(GQA_HEADS), head_mean(GQA_KV_HEADS)) + tuple(
        jnp.asarray(_interleave_matrix(merge_tile, dil)) for _, dil in DIL_PATTERNS[1:]) + tuple(
        jnp.asarray(_interleave_matrix(norm_tile, dil).T) for _, dil in DIL_PATTERNS[1:])
    table = w["t5_table"]
    dil_bias = [_band_bias(table, dil, gi * DIL_HEADS, DIL_HEADS, window // (2 * dil))
                for gi, (window, dil) in enumerate(DIL_PATTERNS)]
    win_bias = _band_bias(table, 1, len(DIL_PATTERNS) * DIL_HEADS, WIN_HEADS, WIN_HALF)
    for l in range(w["norm_g"].shape[0]):
        x = _layer(x, w, l, aux, (dil_bias, win_bias))
    return _loss_op(x, target, w["final_norm_g"][None, :])


_ANY = pl.BlockSpec(memory_space=pl.ANY)
_MESH = pl.DeviceIdType.MESH


def _all_gather(block, name):
    def body(x_ref, out_ref, send_sems, recv_sems, local_sem):
        x, y, c = lax.axis_index("x"), lax.axis_index("y"), lax.axis_index("c")
        me, sibling = (x, y, c), (x, y, 1 - c)
        chips = [(1 - x, y), (x, 1 - y), (1 - x, 1 - y)]

        def slot(px, py, pc):
            return out_ref.at[4 * px + 2 * py + pc]

        def copy(k, blk, to, src=None):
            return pltpu.make_async_remote_copy(
                src_ref=slot(*blk) if src is None else src, dst_ref=slot(*blk),
                send_sem=send_sems.at[k], recv_sem=recv_sems.at[k], device_id=to, device_id_type=_MESH)

        mine = pltpu.make_async_copy(x_ref, slot(*me), local_sem)
        mine.start()
        first = [copy(0, me, sibling, src=x_ref)]
        first += [copy(1 + j, me, (*chip, c), src=x_ref) for j, chip in enumerate(chips)]
        for cp in first:
            cp.start()
        passed = [copy(4 + j, (*chip, c), sibling) for j, chip in enumerate(chips)]
        for j, chip in enumerate(chips):
            copy(1 + j, (*chip, c), me).wait_recv()
            passed[j].start()
        copy(0, sibling, me).wait_recv()
        for j, chip in enumerate(chips):
            copy(4 + j, (*chip, 1 - c), me).wait_recv()
        for cp in first + passed:
            cp.wait_send()
        mine.wait()

    return pl.pallas_call(
        body,
        out_shape=jax.ShapeDtypeStruct((N_DEV,) + block.shape, block.dtype),
        in_specs=[_ANY],
        out_specs=_ANY,
        scratch_shapes=[pltpu.SemaphoreType.DMA((7,)), pltpu.SemaphoreType.DMA((7,)), pltpu.SemaphoreType.DMA],
        name=name,
    )(block)


def _swap_with_sibling(blocks, name):
    chips = blocks.shape[0]

    def body(x_ref, out_ref, send_sems, recv_sems):
        x, y, c = lax.axis_index("x"), lax.axis_index("y"), lax.axis_index("c")
        copies = [pltpu.make_async_remote_copy(
            src_ref=x_ref.at[k, 1 - c], dst_ref=out_ref.at[k], send_sem=send_sems.at[k], recv_sem=recv_sems.at[k],
            device_id=(x, y, 1 - c), device_id_type=_MESH) for k in range(chips)]
        for cp in copies:
            cp.start()
        for cp in copies:
            cp.wait()

    return pl.pallas_call(
        body,
        out_shape=jax.ShapeDtypeStruct((chips,) + blocks.shape[2:], blocks.dtype),
        in_specs=[_ANY],
        out_specs=_ANY,
        scratch_shapes=[pltpu.SemaphoreType.DMA((chips,)), pltpu.SemaphoreType.DMA((chips,))],
        name=name,
    )(blocks)


def _add_sibling(blocks, theirs, name):
    chips, _, rows, w = blocks.shape
    tr = _row_tile(rows, 16, 4096)

    def body(b_ref, t_ref, o_ref):
        mine = b_ref[0, lax.axis_index("c")]
        o_ref[0] = (mine.astype(F32) + t_ref[0].astype(F32)).astype(o_ref.dtype)

    return pl.pallas_call(
        body,
        grid=(chips, rows // tr),
        in_specs=[pl.BlockSpec((1, 2, tr, w), lambda k, i: (k, 0, i, 0)), pl.BlockSpec((1, tr, w), lambda k, i: (k, i, 0))],
        out_specs=pl.BlockSpec((1, tr, w), lambda k, i: (k, i, 0)),
        out_shape=jax.ShapeDtypeStruct(theirs.shape, theirs.dtype),
        name=name,
        compiler_params=_params("parallel", "parallel"),
    )(blocks, theirs)


def _exchange_chips(partials, name):
    n_chips = partials.shape[0]

    def body(x_ref, out_ref, send_sems, recv_sems, local_sem):
        x, y, c = lax.axis_index("x"), lax.axis_index("y"), lax.axis_index("c")
        me = 2 * x + y
        mine = pltpu.make_async_copy(x_ref.at[me], out_ref.at[me], local_sem)
        mine.start()
        copies, landed = [], []
        for k in range(1, n_chips):
            px = 1 - x if k & 2 else x
            py = 1 - y if k & 1 else y
            peer = 2 * px + py
            copies.append(pltpu.make_async_remote_copy(
                src_ref=x_ref.at[peer], dst_ref=out_ref.at[me], send_sem=send_sems.at[k - 1],
                recv_sem=recv_sems.at[k - 1], device_id=(px, py, c), device_id_type=_MESH))
            landed.append(pltpu.make_async_remote_copy(
                src_ref=x_ref.at[peer], dst_ref=out_ref.at[peer], send_sem=send_sems.at[k - 1],
                recv_sem=recv_sems.at[k - 1], device_id=(px, py, c), device_id_type=_MESH))
        for cp in copies:
            cp.start()
        for cp in landed:
            cp.wait_recv()
        for cp in copies:
            cp.wait_send()
        mine.wait()

    return pl.pallas_call(
        body,
        out_shape=jax.ShapeDtypeStruct(partials.shape, partials.dtype),
        in_specs=[_ANY],
        out_specs=_ANY,
        scratch_shapes=[pltpu.SemaphoreType.DMA((n_chips - 1,)), pltpu.SemaphoreType.DMA((n_chips - 1,)),
                        pltpu.SemaphoreType.DMA],
        name=name,
    )(partials)


def _sum_slots(parts, name):
    slots, rows, w = parts.shape
    tr = _row_tile(rows, 16 if parts.dtype == BF16 else 8, 4096)

    def body(p_ref, o_ref):
        acc = p_ref[0].astype(F32)
        for j in range(1, slots):
            acc = acc + p_ref[j].astype(F32)
        o_ref[...] = acc

    return pl.pallas_call(
        body,
        grid=(rows // tr,),
        in_specs=[pl.BlockSpec((slots, tr, w), lambda i: (0, i, 0))],
        out_specs=pl.BlockSpec((tr, w), lambda i: (i, 0)),
        out_shape=jax.ShapeDtypeStruct((rows, w), F32),
        name=name,
        compiler_params=_params("parallel"),
    )(parts)


def _adamw(w, g, m, v, name):
    rows, width = w.shape
    tr = _row_tile(rows, 8, 2048)

    def body(w_ref, g_ref, m_ref, v_ref, d_ref, nm_ref, nv_ref):
        g_ = g_ref[...]
        m_ = ADAM_B1 * m_ref[...] + (1.0 - ADAM_B1) * g_
        v_ = ADAM_B2 * v_ref[...] + (1.0 - ADAM_B2) * jnp.square(g_)
        m_hat = m_ / (1.0 - ADAM_B1 ** ADAM_STEP)
        v_hat = v_ / (1.0 - ADAM_B2 ** ADAM_STEP)
        d_ref[...] = -ADAM_LR * (m_hat / (jnp.sqrt(v_hat) + ADAM_EPS) + ADAM_WD * w_ref[...])
        nm_ref[...] = m_
        nv_ref[...] = v_

    spec = pl.BlockSpec((tr, width), lambda i: (i, 0))
    return pl.pallas_call(
        body,
        grid=(rows // tr,),
        in_specs=[spec] * 4,
        out_specs=[spec] * 3,
        out_shape=[jax.ShapeDtypeStruct((rows, width), F32)] * 3,
        name=name,
        compiler_params=_params("parallel"),
    )(w, g, m, v)


_SHARDED = (("w_in", 2), ("w_mla_q_up", 2), ("w_mla_kv_up", 2), ("w_branch", 3), ("w_out", 1))
_REPLICATED = ("norm_g", "mla_q_norm_g", "mla_kv_norm_g", "gqa_q_norm_g", "gqa_k_norm_g", "win_sink", "t5_table",
               "final_norm_g")


def _pack(arrays, row_multiple):
    flat = jnp.concatenate([a.reshape(-1) for a in arrays])
    rows = -(-flat.shape[0] // (LANES * row_multiple)) * row_multiple
    return jnp.pad(flat, (0, rows * LANES - flat.shape[0])).reshape(rows, LANES)


def _unpack(packed, shapes):
    flat, out, at = packed.reshape(-1), [], 0
    for shp in shapes:
        n = int(np.prod(shp))
        out.append(flat[at:at + n].reshape(shp))
        at += n
    return out


_TO_WIRE = {
    "w_in": lambda t: jnp.swapaxes(t, 1, 2), "w_mla_q_up": lambda t: jnp.swapaxes(t, 1, 2),
    "w_mla_kv_up": lambda t: jnp.swapaxes(t, 1, 2),
    "w_branch": lambda t: jnp.transpose(t, (0, 3, 1, 2)).reshape(t.shape[0], t.shape[3], -1), "w_out": lambda t: t}
_FROM_WIRE = {
    "w_in": lambda t, shp: jnp.swapaxes(t, 1, 2), "w_mla_q_up": lambda t, shp: jnp.swapaxes(t, 1, 2),
    "w_mla_kv_up": lambda t, shp: jnp.swapaxes(t, 1, 2),
    "w_branch": lambda t, shp: jnp.transpose(t.reshape(shp[0], shp[3], shp[1], shp[2]), (0, 2, 3, 1)),
    "w_out": lambda t, shp: t}
_WIRE_NAME = {"w_in": "w_in_t", "w_mla_q_up": "w_q_t", "w_mla_kv_up": "w_kv_t", "w_branch": "w_branch_t",
              "w_out": "w_out"}


def _join_shards(gathered, wire_shapes):
    out, at = [], 0
    for depth, cut, rest in wire_shapes:
        n = depth * cut * rest // LANES
        blk = gathered[:, at:at + n].reshape(N_DEV, depth, cut, rest)
        out.append(jnp.moveaxis(blk, 0, 1).reshape(depth, N_DEV * cut, rest))
        at += n
    return out


def _split_shards(fulls, wire_shapes):
    parts = []
    for full, (depth, cut, rest) in zip(fulls, wire_shapes):
        blk = jnp.moveaxis(full.reshape(depth, N_DEV, cut, rest), 1, 0)
        parts.append(blk.reshape(N_DEV, depth * cut * rest // LANES, LANES))
    packed = jnp.concatenate(parts, axis=1)
    return packed.reshape((N_DEV // 2, 2) + packed.shape[1:])


def kernel(x, norm_g, w_in, mla_q_norm_g, mla_kv_norm_g, w_mla_q_up, w_mla_kv_up, gqa_q_norm_g, gqa_k_norm_g, win_sink, t5_table, w_branch, w_out, final_norm_g, loss_target, m_norm_g, m_w_in, m_mla_q_norm_g, m_mla_kv_norm_g, m_w_mla_q_up, m_w_mla_kv_up, m_gqa_q_norm_g, m_gqa_k_norm_g, m_win_sink, m_t5_table, m_w_branch, m_w_out, m_final_norm_g, v_norm_g, v_w_in, v_mla_q_norm_g, v_mla_kv_norm_g, v_w_mla_q_up, v_w_mla_kv_up, v_gqa_q_norm_g, v_gqa_k_norm_g, v_win_sink, v_t5_table, v_w_branch, v_w_out, v_final_norm_g):
    given = dict(locals())
    names = ("norm_g", "w_in", "mla_q_norm_g", "mla_kv_norm_g", "w_mla_q_up", "w_mla_kv_up", "gqa_q_norm_g",
             "gqa_k_norm_g", "win_sink", "t5_table", "w_branch", "w_out", "final_norm_g")
    shard_names = [n for n, _ in _SHARDED]
    shard_shapes = [given[n].shape for n in shard_names]

    wire = [_TO_WIRE[n](given[n]).astype(BF16) for n in shard_names]
    wire_shapes = [t.shape for t in wire]
    gathered = _all_gather(jnp.concatenate([t.reshape(-1, LANES) for t in wire]), "gather_weights")
    weights = {n: given[n] for n in _REPLICATED}
    weights.update(zip([_WIRE_NAME[n] for n in shard_names], _join_shards(gathered, wire_shapes)))

    loss, (gw, gx) = jax.value_and_grad(_local_loss, argnums=(0, 1))(weights, x[0], loss_target[0])
    loss = lax.psum(loss, ("x", "y", "c"))

    send = _split_shards([gw[_WIRE_NAME[n]] for n in shard_names], wire_shapes)
    partials = _add_sibling(send, _swap_with_sibling(send, "swap_grads"), "add_sibling_grads")
    g_wire = _unpack(_sum_slots(_exchange_chips(partials, "scatter_grads"), "sum_grads"), wire_shapes)
    g_shard = [_FROM_WIRE[n](t, shp) for n, t, shp in zip(shard_names, g_wire, shard_shapes)]
    rep_shapes = [given[n].shape for n in _REPLICATED]
    g_rep = _unpack(_sum_slots(_all_gather(_pack([gw[n] for n in _REPLICATED], 8), "gather_small_grads"),
                               "sum_small_grads"), rep_shapes)
    grads = dict(zip(shard_names, g_shard))
    grads.update(zip(_REPLICATED, g_rep))

    def update(group, shapes, row_multiple, name):
        outs = _adamw(*[_pack([src[n] for n in group], row_multiple) for src in (
            given, grads, {n: given["m_" + n] for n in group}, {n: given["v_" + n] for n in group})], name)
        return [dict(zip(group, _unpack(o, shapes))) for o in outs]

    big = update(shard_names, shard_shapes, 16, "adamw_shards")
    small = update(list(_REPLICATED), rep_shapes, 8, "adamw_replicated")
    delta, new_m, new_v = [{**b, **s_} for b, s_ in zip(big, small)]
    return (loss, gx[None], *[grads[n] for n in names], *[delta[n] for n in names],
            *[new_m[n] for n in names], *[new_v[n] for n in names])
```

```python
import functools
import math

import jax
import jax.numpy as jnp
import numpy as np
from jax import lax
from jax.experimental import pallas as pl
from jax.experimental.pallas import tpu as pltpu

F32 = jnp.float32
BF16 = jnp.bfloat16
N_DEV = 8
LANES = 128
HALF = LANES // 2
V7X_VMEM_LIMIT = 56 * 1024 * 1024

EPS = 1e-6
NEG_INF = -1e30
LOG2E = 1.4426950408889634
ROPE_THETA = 10000.0
GRID_W = 64
HEAD_DIM = 64
N_BRANCH = 4
BRANCH_W = 256
MLA_HEADS, MLA_Q_LORA, MLA_KV_LORA, MLA_NOPE, MLA_ROPE, MLA_V = 4, 256, 128, 64, 32, 64
MLA_QK = MLA_NOPE + MLA_ROPE
GQA_HEADS, GQA_KV_HEADS = 4, 2
DIL_PATTERNS = ((128, 1), (512, 4), (2048, 16))
DIL_HEADS = 4
WIN_HEADS, WIN_KV_HEADS, WIN_HALF = 4, 2, 128
T5_BUCKETS, T5_MAX_DIST = 32, 1024
BAND_BLOCK = 128
ADAM_LR, ADAM_B1, ADAM_B2, ADAM_EPS, ADAM_WD, ADAM_STEP = 0.001, 0.9, 0.999, 1e-08, 0.01, 10

D_MODEL = 1024
GM_W, SMALL_W, BAND_W = 5120, 512, 768
MLA_BLK, GQA_BLK, WIN_BLK, DIL_BLK = 10, 11, 8, 9
P_TOT = 7680
QW = 256


def _params(*sem):
    return pltpu.CompilerParams(dimension_semantics=sem, vmem_limit_bytes=V7X_VMEM_LIMIT)


def _pick(n, cands):
    for c in cands:
        if n % c == 0:
            return c
    return n


def _row_tile(rows, unit, cap):
    best = unit
    for t in range(unit, min(rows, cap) + 1, unit):
        if rows % t == 0:
            best = t
    assert rows % best == 0
    return best


def _dot(a, b, ca, cb):
    return lax.dot_general(a.astype(BF16), b.astype(BF16), (((ca,), (cb,)), ((), ())), preferred_element_type=F32)


def _bmm(a, b, ca, cb):
    return lax.dot_general(a, b, (((ca,), (cb,)), ((0,), (0,))), preferred_element_type=F32)


@jax.custom_vjp
def _bdot(a, b):
    return _dot(a, b, 1, 0)


def _bdot_fwd(a, b):
    return _dot(a, b, 1, 0), (a, b)


def _bdot_bwd(res, g):
    a, b = res
    return _dot(g, b, 1, 1), _dot(a, g, 0, 0)


_bdot.defvjp(_bdot_fwd, _bdot_bwd)


def _hdot(a, c):
    return lax.dot_general(a, c, (((1,), (0,)), ((), ())), precision=lax.Precision.HIGHEST, preferred_element_type=F32)


@functools.partial(jax.custom_vjp, nondiff_argnums=(1,))
def _lane_roll(x, shift):
    return pltpu.roll(x, shift, 1)


def _lane_roll_fwd(x, shift):
    return pltpu.roll(x, shift, 1), None


def _lane_roll_bwd(shift, _, g):
    return (pltpu.roll(g, g.shape[1] - shift, 1),)


_lane_roll.defvjp(_lane_roll_fwd, _lane_roll_bwd)


@functools.partial(jax.custom_vjp, nondiff_argnums=(1,))
def _lane_ranges(x, cut):
    bounds, _ = cut
    return tuple(x[:, lo:hi] for lo, hi in zip(bounds[:-1], bounds[1:]))


def _lane_ranges_fwd(x, cut):
    return _lane_ranges(x, cut), None


def _lane_ranges_bwd(cut, _, cts):
    bounds, width = cut
    parts = list(cts)
    if bounds[-1] < width:
        parts.append(jnp.zeros((cts[0].shape[0], width - bounds[-1]), cts[0].dtype))
    return (jnp.concatenate(parts, axis=1),)


_lane_ranges.defvjp(_lane_ranges_fwd, _lane_ranges_bwd)


def _lanes(x, bounds):
    return _lane_ranges(x, (tuple(bounds), x.shape[1]))


@jax.custom_vjp
def _unstack(x):
    return tuple(x[i] for i in range(x.shape[0]))


def _unstack_fwd(x):
    return _unstack(x), None


def _unstack_bwd(_, cts):
    return (jnp.stack(cts, axis=0),)


_unstack.defvjp(_unstack_fwd, _unstack_bwd)


@functools.partial(jax.custom_vjp, nondiff_argnums=(1,))
def _split_heads(x, h):
    d = x.shape[1] // h
    return jnp.stack([x[:, i * d:(i + 1) * d] for i in range(h)], axis=0)


def _split_heads_fwd(x, h):
    return _split_heads(x, h), None


def _split_heads_bwd(h, _, ct):
    return (jnp.concatenate([ct[i] for i in range(h)], axis=1),)


_split_heads.defvjp(_split_heads_fwd, _split_heads_bwd)


def _join_heads(x):
    return jnp.concatenate(_unstack(x), axis=1)


def _rope(x, cos_t, sin_t, half):
    w = x.shape[1]
    lane = lax.broadcasted_iota(jnp.int32, (1, w), 1)
    first = (lane % (2 * half)) < half
    partner = jnp.where(first, _lane_roll(x, w - half), _lane_roll(x, half))
    return x * cos_t + partner * sin_t


def _rms(x, g):
    return x * lax.rsqrt(jnp.mean(x * x, axis=-1, keepdims=True) + EPS) * g


def _rows(tr, w, col=0):
    return pl.BlockSpec((tr, w), lambda i: (i, col))


def _head_rows(h, tr, d):
    return pl.BlockSpec((h, tr, d), lambda i: (0, i, 0))


def _whole(shape):
    nd = len(shape)
    return pl.BlockSpec(tuple(shape), lambda i: (0,) * nd)


def _fwd_call(name, fn, steps, rows, params, aux, outs):
    nr, npar, na = len(rows), len(params), len(aux)

    def body(*refs):
        vals = [x[...].astype(F32) for x in refs[:nr + npar + na]]
        res = fn(vals[:nr], vals[nr:nr + npar], vals[nr + npar:])
        for o_ref, o in zip(refs[nr + npar + na:], res):
            o_ref[...] = o.astype(o_ref.dtype)

    return pl.pallas_call(
        body,
        grid=(steps,),
        in_specs=[s for _, s in rows] + [_whole(p.shape) for p in params] + [s for _, s in aux],
        out_specs=[e[1] for e in outs],
        out_shape=[jax.ShapeDtypeStruct(e[0], e[2] if len(e) > 2 else F32) for e in outs],
        name=name + "_fwd",
        compiler_params=_params("parallel"),
    )(*[a for a, _ in rows], *params, *[a for a, _ in aux])


def _vjp_call(name, fn, steps, rows, params, aux, cts, row_grads, into=None):
    nr, npar, na, nc = len(rows), len(params), len(aux), len(cts)
    n_in = nr + npar + na + nc
    lead = 0 if into is None else 1

    def body(*refs):
        refs = refs[lead:]
        vals = [x[...].astype(F32) for x in refs[:n_in]]
        r, p, a, d = vals[:nr], vals[nr:nr + npar], vals[nr + npar:nr + npar + na], vals[nr + npar + na:]
        out_refs = refs[n_in:]
        _, vjp = jax.vjp(lambda r_, p_: tuple(fn(r_, p_, a)), r, p)
        dr, dp = vjp(tuple(d))
        for o_ref, o in zip(out_refs[:nr], dr):
            o_ref[...] = o.astype(o_ref.dtype)

        @pl.when(pl.program_id(0) == 0)
        def _():
            for o_ref in out_refs[nr:]:
                o_ref[...] = jnp.zeros_like(o_ref)

        for o_ref, o in zip(out_refs[nr:], dp):
            o_ref[...] += o

    outs = pl.pallas_call(
        body,
        grid=(steps,),
        in_specs=([] if into is None else [pl.BlockSpec(memory_space=pl.ANY)])
        + [s for _, s in rows] + [_whole(p.shape) for p in params] + [s for _, s in aux] + [s for _, s in cts],
        out_specs=[e[1] for e in row_grads] + [_whole(p.shape) for p in params],
        out_shape=[jax.ShapeDtypeStruct(e[0], e[2] if len(e) > 2 else F32) for e in row_grads]
        + [jax.ShapeDtypeStruct(p.shape, F32) for p in params],
        input_output_aliases={} if into is None else {0: 0},
        name=name + "_bwd",
        compiler_params=_params("arbitrary"),
    )(*([] if into is None else [into]), *[a for a, _ in rows], *params, *[a for a, _ in aux], *[a for a, _ in cts])
    return list(outs[:nr]), list(outs[nr:])


def _norm_tile(r, p, a):
    return (_rms(r[0], p[0]),)


def _mm(a, b, mode, name, out_dtype=F32):
    if mode == "nn":
        (m, k), n = a.shape, b.shape[1]
    elif mode == "nt":
        (m, k), n = a.shape, b.shape[0]
    else:
        (k, m), n = a.shape, b.shape[1]
    tm = _pick(m, (1024, 512, 256, 128))
    tn = _pick(n, (1024, 768, 512, 384, 256, 128))
    budget = V7X_VMEM_LIMIT * 3 // 4
    for tk in (512, 256, 128) if mode == "tn" else (4096, 1024, 768, 512, 384, 256, 128):
        need = 2 * tk * (tm * a.dtype.itemsize + tn * b.dtype.itemsize) + tm * tn * (4 + 2 * np.dtype(out_dtype).itemsize)
        if k % tk == 0 and need <= budget:
            break
    nk = k // tk

    def body(a_ref, b_ref, o_ref, acc_ref):
        kk = pl.program_id(2)
        if mode == "nn":
            part = _dot(a_ref[...], b_ref[...], 1, 0)
        elif mode == "nt":
            part = _dot(a_ref[...], b_ref[...], 1, 1)
        else:
            part = _dot(a_ref[...], b_ref[...], 0, 0)
        if nk == 1:
            o_ref[...] = part.astype(o_ref.dtype)
        else:
            @pl.when(kk == 0)
            def _():
                acc_ref[...] = part

            @pl.when(kk > 0)
            def _():
                acc_ref[...] += part

            @pl.when(kk == nk - 1)
            def _():
                o_ref[...] = acc_ref[...].astype(o_ref.dtype)

    if mode == "nn":
        a_spec = pl.BlockSpec((tm, tk), lambda i, j, kk: (i, kk))
        b_spec = pl.BlockSpec((tk, tn), lambda i, j, kk: (kk, j))
    elif mode == "nt":
        a_spec = pl.BlockSpec((tm, tk), lambda i, j, kk: (i, kk))
        b_spec = pl.BlockSpec((tn, tk), lambda i, j, kk: (j, kk))
    else:
        a_spec = pl.BlockSpec((tk, tm), lambda i, j, kk: (kk, i))
        b_spec = pl.BlockSpec((tk, tn), lambda i, j, kk: (kk, j))
    return pl.pallas_call(
        body,
        grid=(m // tm, n // tn, nk),
        in_specs=[a_spec, b_spec],
        out_specs=pl.BlockSpec((tm, tn), lambda i, j, kk: (i, j)),
        out_shape=jax.ShapeDtypeStruct((m, n), out_dtype),
        scratch_shapes=[pltpu.VMEM((tm, tn), F32)],
        name=name,
        compiler_params=_params("parallel", "parallel", "arbitrary"),
    )(a, b)


def _dense_fwd_call(q, k, v, scale, name):
    n, sq, d = q.shape
    sk, dv = k.shape[1], v.shape[2]
    tq = _pick(sq, (512, 256, 128))
    c = scale * LOG2E

    nkb = 1

    def body(q_ref, k_ref, v_ref, o_ref, lse_ref, m_s, acc_s, vext_s):
        j = pl.program_id(2)

        @pl.when(j == 0)
        def _():
            m_s[...] = jnp.full_like(m_s, NEG_INF)
            acc_s[...] = jnp.zeros_like(acc_s)
            vext_s[...] = jnp.ones_like(vext_s)

        vext_s[:, :dv] = v_ref[0].astype(BF16)
        m_old = m_s[...]
        s = _dot(q_ref[0], k_ref[0], 1, 1)
        m_new = jnp.maximum(m_old, jnp.max(s, axis=1, keepdims=True))
        p = jnp.exp2(s * c - m_new * c)
        acc = jnp.exp2((m_old - m_new) * c) * acc_s[...] + _dot(p, vext_s[...], 1, 0)
        m_s[...] = m_new
        acc_s[...] = acc

        @pl.when(j == nkb - 1)
        def _():
            l = acc[:, dv:dv + 1]
            o_ref[0] = acc[:, :dv] / l
            lse_ref[0] = m_new * scale + jnp.log(l)

    return pl.pallas_call(
        body,
        grid=(n, sq // tq, nkb),
        in_specs=[
            pl.BlockSpec((1, tq, d), lambda h, i, j: (h, i, 0)),
            pl.BlockSpec((1, sk // nkb, d), lambda h, i, j: (h, j, 0)),
            pl.BlockSpec((1, sk // nkb, dv), lambda h, i, j: (h, j, 0)),
        ],
        out_specs=[
            pl.BlockSpec((1, tq, dv), lambda h, i, j: (h, i, 0)),
            pl.BlockSpec((1, tq, 1), lambda h, i, j: (h, i, 0)),
        ],
        out_shape=[jax.ShapeDtypeStruct((n, sq, dv), F32), jax.ShapeDtypeStruct((n, sq, 1), F32)],
        scratch_shapes=[pltpu.VMEM((tq, 1), F32), pltpu.VMEM((tq, 2 * dv), F32), pltpu.VMEM((sk // nkb, 2 * dv), BF16)],
        name=name + "_fwd",
        compiler_params=_params("parallel", "parallel", "arbitrary"),
    )(q, k, v)


def _dense_bwd_call(q, k, v, o, lse, do, scale, name):
    n, sq, d = q.shape
    sk, dv = k.shape[1], v.shape[2]
    tq, tk = _pick(sq, (1024, 512, 256, 128)), _pick(sk, (2048, 1024, 512, 256, 128))
    c = scale * LOG2E

    def body(q_ref, k_ref, v_ref, o_ref, lse_ref, do_ref, dq_ref, dk_ref, dv_ref):
        j, i = pl.program_id(1), pl.program_id(2)
        qb, kb, vb = q_ref[0].astype(BF16), k_ref[0].astype(BF16), v_ref[0].astype(BF16)
        do_f = do_ref[0]
        dob = do_f.astype(BF16)
        p = jnp.exp2(_dot(qb, kb, 1, 1) * c - lse_ref[0] * LOG2E)
        delta = jnp.sum(do_f * o_ref[0], axis=1, keepdims=True)
        ds = (p * (_dot(dob, vb, 1, 1) - delta)).astype(BF16)
        dv_part = _dot(p, dob, 0, 0)
        dk_part = _dot(ds, qb, 0, 0) * scale
        dq_part = _dot(ds, kb, 1, 0) * scale
        rows = pl.ds(pl.multiple_of(i * tq, tq), tq)

        @pl.when(i == 0)
        def _():
            dk_ref[0] = dk_part
            dv_ref[0] = dv_part

        @pl.when(i > 0)
        def _():
            dk_ref[0] += dk_part
            dv_ref[0] += dv_part

        @pl.when(j == 0)
        def _():
            dq_ref[0, rows, :] = dq_part

        @pl.when(j > 0)
        def _():
            dq_ref[0, rows, :] += dq_part

    return pl.pallas_call(
        body,
        grid=(n, sk // tk, sq // tq),
        in_specs=[
            pl.BlockSpec((1, tq, d), lambda h, j, i: (h, i, 0)),
            pl.BlockSpec((1, tk, d), lambda h, j, i: (h, j, 0)),
            pl.BlockSpec((1, tk, dv), lambda h, j, i: (h, j, 0)),
            pl.BlockSpec((1, tq, dv), lambda h, j, i: (h, i, 0)),
            pl.BlockSpec((1, tq, 1), lambda h, j, i: (h, i, 0)),
            pl.BlockSpec((1, tq, dv), lambda h, j, i: (h, i, 0)),
        ],
        out_specs=[
            pl.BlockSpec((1, sq, d), lambda h, j, i: (h, 0, 0)),
            pl.BlockSpec((1, tk, d), lambda h, j, i: (h, j, 0)),
            pl.BlockSpec((1, tk, dv), lambda h, j, i: (h, j, 0)),
        ],
        out_shape=[
            jax.ShapeDtypeStruct((n, sq, d), F32),
            jax.ShapeDtypeStruct((n, sk, d), F32),
            jax.ShapeDtypeStruct((n, sk, dv), F32),
        ],
        name=name + "_bwd",
        compiler_params=_params("arbitrary", "arbitrary", "arbitrary"),
    )(q, k, v, o, lse, do)


def _head_geometry(h, group):
    pair, a = divmod(h, 2)
    kv_pair, b = divmod(h // group, 2)
    return pair, a, kv_pair, b


def _lane_half():
    return lax.broadcasted_iota(jnp.int32, (1, LANES), 1) // HALF


def _align(x, a, b):
    if a != b:
        x = pltpu.roll(x, HALF, 1)
    return jnp.where(_lane_half() == b, x, 0.0)


def _unalign(x, a, b):
    x = jnp.where(_lane_half() == b, x, 0.0)
    return pltpu.roll(x, HALF, 1) if a != b else x


def _bands(w, pw, nw, lo, kvw, nb):
    b = BAND_BLOCK
    cat = jnp.concatenate([pw[:, lo:lo + kvw], w[:, lo:lo + kvw], nw[:, lo:lo + kvw]], axis=0).astype(BF16)
    out = []
    for g in range(kvw // LANES):
        c3 = cat[:, g * LANES:(g + 1) * LANES].reshape(nb + 2, b, LANES)
        out.append(jnp.concatenate([c3[0:nb], c3[1:nb + 1], c3[2:nb + 2]], axis=1))
    return out


def _edge_mask(first_block, nb, period):
    b = BAND_BLOCK
    blk = (first_block + lax.broadcasted_iota(jnp.int32, (nb, 1, 3 * b), 0)) % period
    col = lax.broadcasted_iota(jnp.int32, (nb, 1, 3 * b), 2)
    outside = ((col < b) & (blk == 0)) | ((col >= 2 * b) & (blk == period - 1))
    return jnp.where(outside, NEG_INF, 0.0)


def _band_geometry(proj, dil):
    rows = proj.shape[0]
    tl = _pick(rows, (1024, 512, 256, 128))
    return rows, tl, tl // BAND_BLOCK, rows // tl, rows // dil // BAND_BLOCK


def _band_in_specs(tl, nb, n_chunks, n_blocks, col, last_step_idle):
    def chunk(i):
        return jnp.minimum(i, n_chunks - 1) if last_step_idle else i

    main = pl.BlockSpec((tl, BAND_W), lambda j, i: (j * n_chunks + chunk(i), col))
    prev = pl.BlockSpec((BAND_BLOCK, BAND_W),
                        lambda j, i: (j * n_blocks + jnp.maximum(chunk(i) * nb - 1, 0), col))
    nxt = pl.BlockSpec((BAND_BLOCK, BAND_W),
                       lambda j, i: (j * n_blocks + jnp.minimum((chunk(i) + 1) * nb, n_blocks - 1), col))
    rows = pl.BlockSpec((tl, QW), lambda j, i: (j * n_chunks + chunk(i), 0))
    return main, prev, nxt, rows


def _band_fwd_call(proj, col, bias, sink, dil, group, kvw, scale, name):
    s_tok = proj.shape[0]
    seq, tl, nb, n_chunks, period = _band_geometry(proj, dil)
    n_blocks = seq // BAND_BLOCK
    heads = bias.shape[0]

    def body(w_ref, pw_ref, nw_ref, bias_ref, sink_ref, o_ref, lse_ref):
        i = pl.program_id(1)
        w, pw, nw = w_ref[...].astype(F32), pw_ref[...].astype(F32), nw_ref[...].astype(F32)
        kb = _bands(w, pw, nw, QW, kvw, nb)
        vb = _bands(w, pw, nw, QW + kvw, kvw, nb)
        edge = _edge_mask(i * nb, nb, period)
        o_acc = [jnp.zeros((tl, LANES), F32) for _ in range(heads // 2)]
        lse_acc = [jnp.zeros((tl, LANES), F32) for _ in range(heads // 2)]
        for h in range(heads):
            pair, a, kvp, b = _head_geometry(h, group)
            q_al = _align(w[:, pair * LANES:(pair + 1) * LANES], a, b).astype(BF16).reshape(nb, BAND_BLOCK, LANES)
            logits = _bmm(q_al, kb[kvp], 2, 2) * scale + bias_ref[h][None] + edge
            sk = sink_ref[h].reshape(1, 1, 1)
            m = jnp.maximum(jnp.max(logits, axis=2, keepdims=True), sk)
            e = jnp.exp(logits - m)
            ssum = jnp.sum(e, axis=2, keepdims=True) + jnp.exp(sk - m)
            out = _bmm(e.astype(BF16), vb[kvp], 2, 1) / ssum
            o_acc[pair] = o_acc[pair] + _unalign(out.reshape(tl, LANES), a, b)
            lse = (m + jnp.log(ssum)).reshape(tl, 1)
            lse_acc[pair] = lse_acc[pair] + jnp.where(_lane_half() == a, lse, 0.0)
        o_ref[...] = jnp.concatenate(o_acc, axis=1)
        lse_ref[...] = jnp.concatenate(lse_acc, axis=1)

    main, prev, nxt, rows = _band_in_specs(tl, nb, n_chunks, n_blocks, col, False)
    return pl.pallas_call(
        body,
        grid=(1, n_chunks),
        in_specs=[main, prev, nxt, pl.BlockSpec(bias.shape, lambda j, i: (0, 0, 0)),
                  pl.BlockSpec(sink.shape, lambda j, i: (0, 0, 0))],
        out_specs=[rows, rows],
        out_shape=[jax.ShapeDtypeStruct((s_tok, QW), F32)] * 2,
        name=name + "_fwd",
        compiler_params=_params("parallel", "parallel"),
    )(proj, proj, proj, bias, sink)


def _band_bwd_call(proj, o, do, lse, dlse, bias, sink, dproj, col, dil, group, kvw, scale, name):
    seq, tl, nb, n_chunks, period = _band_geometry(proj, dil)
    lead = 0 if dproj is None else 1
    n_blocks = seq // BAND_BLOCK
    heads = bias.shape[0]
    b_ = BAND_BLOCK
    have_dlse = dlse is not None

    def body(*refs):
        (w_ref, pw_ref, nw_ref, o_ref, do_ref, lse_ref), refs = refs[lead:lead + 6], refs[lead + 6:]
        if have_dlse:
            dlse_ref, refs = refs[0], refs[1:]
        bias_ref, sink_ref, dwin_ref, dbias_ref, dsink_ref, dq_s, dk_s, dv_s = refs
        j, i = pl.program_id(0), pl.program_id(1)

        @pl.when((j == 0) & (i == 0))
        def _():
            dbias_ref[...] = jnp.zeros_like(dbias_ref)
            dsink_ref[...] = jnp.zeros_like(dsink_ref)

        @pl.when(i == 0)
        def _():
            dk_s[...] = jnp.zeros_like(dk_s)
            dv_s[...] = jnp.zeros_like(dv_s)

        @pl.when(i < n_chunks)
        def _():
            w, pw, nw = w_ref[...].astype(F32), pw_ref[...].astype(F32), nw_ref[...].astype(F32)
            kb = _bands(w, pw, nw, QW, kvw, nb)
            vb = _bands(w, pw, nw, QW + kvw, kvw, nb)
            edge = _edge_mask(i * nb, nb, period)
            dq_acc = [jnp.zeros((tl, LANES), F32) for _ in range(heads // 2)]
            for h in range(heads):
                pair, a, kvp, b = _head_geometry(h, group)
                lanes = slice(pair * LANES, (pair + 1) * LANES)
                mine = _lane_half() == a
                q_al = _align(w[:, lanes], a, b).astype(BF16).reshape(nb, b_, LANES)
                do_al = _align(do_ref[:, lanes], a, b).astype(BF16).reshape(nb, b_, LANES)
                lse_h = jnp.max(jnp.where(mine, lse_ref[:, lanes], NEG_INF), axis=1, keepdims=True)
                shift = -jnp.sum(jnp.where(mine, do_ref[:, lanes] * o_ref[:, lanes], 0.0), axis=1, keepdims=True)
                if have_dlse:
                    shift = shift + jnp.sum(jnp.where(mine, dlse_ref[:, lanes], 0.0), axis=1, keepdims=True)
                logits = _bmm(q_al, kb[kvp], 2, 2) * scale + bias_ref[h][None] + edge
                p = jnp.exp(logits - lse_h.reshape(nb, b_, 1))
                dlogits = p * (_bmm(do_al, vb[kvp], 2, 2) + shift.reshape(nb, b_, 1))
                dbias_ref[h] += jnp.sum(dlogits, axis=0)
                dsink_ref[h] += jnp.sum(jnp.exp(sink_ref[h] - lse_h) * shift, axis=0, keepdims=True)
                ds = (dlogits * scale).astype(BF16)
                dq_acc[pair] = dq_acc[pair] + _unalign(_bmm(ds, kb[kvp], 2, 1).reshape(tl, LANES), a, b)
                dk_band = _bmm(ds, q_al, 1, 1)
                dv_band = _bmm(p.astype(BF16), do_al, 1, 1)
                kv_lanes = slice(kvp * LANES, (kvp + 1) * LANES)
                for t in range(3):
                    at = pl.ds(pl.multiple_of(i * tl + t * b_, b_), tl)
                    dk_s[at, kv_lanes] += dk_band[:, t * b_:(t + 1) * b_, :].reshape(tl, LANES)
                    dv_s[at, kv_lanes] += dv_band[:, t * b_:(t + 1) * b_, :].reshape(tl, LANES)
            dq_s[lax.rem(i, 2)] = jnp.concatenate(dq_acc, axis=1)

        @pl.when(i >= 1)
        def _():
            at = pl.ds(pl.multiple_of((i - 1) * tl + b_, b_), tl)
            parts = [dq_s[lax.rem(i + 1, 2)], dk_s[at, :], dv_s[at, :]]
            if QW + 2 * kvw < BAND_W:
                parts.append(jnp.zeros((tl, BAND_W - QW - 2 * kvw), F32))
            dwin_ref[...] = jnp.concatenate(parts, axis=1).astype(dwin_ref.dtype)

    main, prev, nxt, rows = _band_in_specs(tl, nb, n_chunks, n_blocks, col, True)
    row_args = [o, do, lse] + ([dlse] if have_dlse else [])
    small = [pl.BlockSpec(bias.shape, lambda j, i: (0, 0, 0)), pl.BlockSpec(sink.shape, lambda j, i: (0, 0, 0))]
    return pl.pallas_call(
        body,
        grid=(1, n_chunks + 1),
        in_specs=[pl.BlockSpec(memory_space=pl.ANY)] * lead + [main, prev, nxt] + [rows] * len(row_args) + small,
        out_specs=[pl.BlockSpec((tl, BAND_W), lambda j, i: (j * n_chunks + jnp.maximum(i - 1, 0), col))] + small,
        out_shape=[jax.ShapeDtypeStruct(proj.shape, BF16), jax.ShapeDtypeStruct(bias.shape, F32),
                   jax.ShapeDtypeStruct(sink.shape, F32)],
        scratch_shapes=[pltpu.VMEM((2, tl, QW), F32), pltpu.VMEM((seq + 2 * b_, kvw), F32),
                        pltpu.VMEM((seq + 2 * b_, kvw), F32)],
        input_output_aliases={0: 0} if lead else {},
        name=name + "_bwd",
        compiler_params=_params("arbitrary", "arbitrary"),
    )(*([dproj] if lead else []), proj, proj, proj, *row_args, bias, sink)


def _loss_call(x, target, g):
    s, d = x.shape
    tr = _pick(s, (256, 128, 64, 32, 16, 8))

    def tile_loss(xt, gt, tt):
        err = jnp.square(_rms(xt, gt) - tt)
        return 0.5 * jnp.sum(jnp.mean(err, axis=-1, keepdims=True), axis=0, keepdims=True)

    def body(x_ref, t_ref, g_ref, loss_ref, dx_ref, dg_ref):
        tt = t_ref[...]
        val, vjp = jax.vjp(lambda xt, gt: tile_loss(xt, gt, tt), x_ref[...], g_ref[...])
        dx, dg = vjp(jnp.ones_like(val))
        dx_ref[...] = dx

        @pl.when(pl.program_id(0) == 0)
        def _():
            loss_ref[...] = jnp.zeros_like(loss_ref)
            dg_ref[...] = jnp.zeros_like(dg_ref)

        loss_ref[...] += val
        dg_ref[...] += dg

    return pl.pallas_call(
        body,
        grid=(s // tr,),
        in_specs=[_rows(tr, d), _rows(tr, d), _whole((1, d))],
        out_specs=[_whole((1, 1)), _rows(tr, d), _whole((1, d))],
        out_shape=[jax.ShapeDtypeStruct((1, 1), F32), jax.ShapeDtypeStruct((s, d), F32),
                   jax.ShapeDtypeStruct((1, d), F32)],
        name="final_norm_loss",
        compiler_params=_params("arbitrary"),
    )(x, target, g)


@jax.custom_vjp
def _loss_op(x, target, g):
    return _loss_call(x, target, g)[0][0, 0]


def _loss_op_fwd(x, target, g):
    loss, dx, dg = _loss_call(x, target, g)
    return loss[0, 0], (dx, dg, target)


def _loss_op_bwd(res, ct):
    dx, dg, target = res
    return ct * dx, jnp.zeros_like(target), ct * dg


_loss_op.defvjp(_loss_op_fwd, _loss_op_bwd)


def _mla_tile(r, p, a):
    g_q, g_kv, w_q, w_k, w_v = p
    cos_t, sin_t, place_kr = a
    a_q, a_kv, a_kr = _lanes(r[0], (0, MLA_Q_LORA, MLA_Q_LORA + MLA_KV_LORA, MLA_Q_LORA + MLA_KV_LORA + MLA_ROPE))
    q = _rope(_bdot(_rms(a_q, g_q), w_q), cos_t, sin_t, MLA_ROPE // 2)
    ckv = _rms(a_kv, g_kv)
    k = _rope(_bdot(ckv, w_k) + _hdot(a_kr, place_kr), cos_t, sin_t, MLA_ROPE // 2)
    return _split_heads(q, MLA_HEADS), _split_heads(k, MLA_HEADS), _split_heads(_bdot(ckv, w_v), MLA_HEADS)


def _head_rms(x, g, head_mean):
    return x * lax.rsqrt(_hdot(x * x, head_mean) + EPS) * g


def _gqa_tile(r, p, a):
    g_q, g_k = p
    cos_t, sin_t, mean_q, mean_k = a
    wq, wk = GQA_HEADS * HEAD_DIM, GQA_KV_HEADS * HEAD_DIM
    b_q, b_k, b_v = _lanes(r[0], (0, wq, wq + wk, wq + 2 * wk))
    q = _rope(_head_rms(b_q, g_q, mean_q), cos_t, sin_t, HEAD_DIM // 4)
    k = _rope(_head_rms(b_k, g_k, mean_k), cos_t[:, :wk], sin_t[:, :wk], HEAD_DIM // 4)
    return _split_heads(q, GQA_HEADS), _split_heads(k, GQA_KV_HEADS), _split_heads(b_v, GQA_KV_HEADS)


def _permute_rows(p, x, cp):
    pb = p.astype(BF16)
    hi = x.astype(BF16)
    rest = x - hi.astype(F32)
    mid = rest.astype(BF16)
    low = (rest - mid.astype(F32)).astype(BF16)
    dims = (((cp,), (0,)), ((), ()))
    return (lax.dot_general(pb, hi, dims, preferred_element_type=F32)
            + lax.dot_general(pb, mid, dims, preferred_element_type=F32)
            + lax.dot_general(pb, low, dims, preferred_element_type=F32))


@jax.custom_vjp
def _permuted(p, x):
    return _permute_rows(p, x, 1)


def _permuted_fwd(p, x):
    return _permute_rows(p, x, 1), p


def _permuted_bwd(p, ct):
    return jnp.zeros_like(p), _permute_rows(p, ct, 0)


_permuted.defvjp(_permuted_fwd, _permuted_bwd)


def _interleave(p, x):
    return _permuted(p, x.reshape(x.shape[0] * x.shape[1], x.shape[2]))


def _interleave_matrix(rows, dil):
    p = np.zeros((rows, rows), np.float32)
    for t in range(rows):
        p[t, (t % dil) * (rows // dil) + t // dil] = 1.0
    return p


def _merge_tile(r, p, a):
    gm, o_a, o_b, oc0, oc1, oc2, l0, l1, l2, o_d = r
    (w_branch,) = p
    perm1, perm2 = a
    oc1, l1, oc2, l2 = _interleave(perm1, oc1), _interleave(perm1, l1), _interleave(perm2, oc2), _interleave(perm2, l2)
    d = w_branch.shape[2]
    gate_path, merge_logits = _lanes(gm, (0, N_BRANCH * BRANCH_W, N_BRANCH * BRANCH_W + N_BRANCH * d))
    m = jnp.maximum(jnp.maximum(l0, l1), l2)
    e0, e1, e2 = jnp.exp(l0 - m), jnp.exp(l1 - m), jnp.exp(l2 - m)
    y_c = (e0 * oc0 + e1 * oc1 + e2 * oc2) / (e0 + e1 + e2)
    y = jnp.concatenate([_join_heads(o_a), _join_heads(o_b), y_c, o_d], axis=1)
    u = y * (gate_path * jax.nn.sigmoid(gate_path))
    gates = _lanes(merge_logits, tuple(range(0, N_BRANCH * d + 1, d)))
    us = _lanes(u, tuple(range(0, N_BRANCH * BRANCH_W + 1, BRANCH_W)))
    branch_w = _unstack(w_branch)
    out = None
    for nb in range(N_BRANCH):
        term = jax.nn.sigmoid(gates[nb]) * _bdot(us[nb], branch_w[nb])
        out = term if out is None else out + term
    return (out,)


def _mixer_calls(proj, prm, aux):
    s = proj.shape[0]
    tr, tm = _pick(s, (256, 128)), _pick(s, (128,))
    mla_cos, mla_sin, gqa_cos, gqa_sin, place_kr, mean_q, mean_k = aux[:7]
    wq = MLA_HEADS * MLA_QK
    mla = dict(
        steps=s // tr, rows=[(proj, _rows(tr, SMALL_W, MLA_BLK))],
        params=[prm["g_q"], prm["g_kv"], prm["w_q"], prm["w_k"], prm["w_v"]],
        aux=[(mla_cos, _rows(tr, wq)), (mla_sin, _rows(tr, wq)), (place_kr, _whole(place_kr.shape))],
        outs=[((MLA_HEADS, s, MLA_QK), _head_rows(MLA_HEADS, tr, MLA_QK))] * 2
        + [((MLA_HEADS, s, MLA_V), _head_rows(MLA_HEADS, tr, MLA_V))],
        window=((s, P_TOT), _rows(tr, SMALL_W, MLA_BLK), BF16))
    wg = GQA_HEADS * HEAD_DIM
    gqa = dict(
        steps=s // tr, rows=[(proj, _rows(tr, SMALL_W, GQA_BLK))], params=[prm["gq"], prm["gk"]],
        aux=[(gqa_cos, _rows(tr, wg)), (gqa_sin, _rows(tr, wg)), (mean_q, _whole(mean_q.shape)),
             (mean_k, _whole(mean_k.shape))],
        outs=[((GQA_HEADS, s, HEAD_DIM), _head_rows(GQA_HEADS, tr, HEAD_DIM))]
        + [((GQA_KV_HEADS, s, HEAD_DIM), _head_rows(GQA_KV_HEADS, tr, HEAD_DIM))] * 2,
        window=((s, P_TOT), _rows(tr, SMALL_W, GQA_BLK), BF16))
    merge = dict(steps=s // tm, tm=tm, window=((s, P_TOT), _rows(tm, GM_W, 0), BF16))
    return mla, gqa, merge


def _merge_rows(proj, o_a, o_b, ocs, lses, o_d, tm):
    h4 = _head_rows(4, tm, HEAD_DIM)
    s = proj.shape[0]

    def by_residue(t, dil):
        if dil == 1:
            return t, _rows(tm, QW)
        return t.reshape(dil, s // dil, QW), pl.BlockSpec((dil, tm // dil, QW), lambda i: (0, i, 0))

    dils = [dil for _, dil in DIL_PATTERNS]
    return ([(proj, _rows(tm, GM_W, 0)), (o_a, h4), (o_b, h4)] + [by_residue(t, r) for t, r in zip(ocs, dils)]
            + [by_residue(t, r) for t, r in zip(lses, dils)] + [(o_d, _rows(tm, QW))])


def _merge_aux(aux):
    return [(t, _whole(t.shape)) for t in aux[7:9]]


def _to_residues(t, dil):
    s, w = t.shape
    return t if dil == 1 else t.reshape(s // dil, dil, w).transpose(1, 0, 2).reshape(s, w)


def _from_residues(t, dil):
    s, w = t.shape
    return t if dil == 1 else t.reshape(dil, s // dil, w).transpose(1, 0, 2).reshape(s, w)


def _mixer_fwd(projs, prm, aux):
    proj = projs[0]
    s = proj.shape[0]
    mla, gqa, merge = _mixer_calls(proj, prm, aux)
    q_a, k_a, v_a = _fwd_call("prep_mla", _mla_tile, mla["steps"], mla["rows"], mla["params"], mla["aux"], mla["outs"])
    o_a, lse_a = _dense_fwd_call(q_a, k_a, v_a, MLA_QK ** -0.5, "mla")
    q_b, k_b, v_b = _fwd_call("prep_gqa", _gqa_tile, gqa["steps"], gqa["rows"], gqa["params"], gqa["aux"], gqa["outs"])
    grp = GQA_HEADS // GQA_KV_HEADS
    o_b, lse_b = _dense_fwd_call(q_b.reshape(GQA_KV_HEADS, grp * s, HEAD_DIM), k_b, v_b, HEAD_DIM ** -0.5, "gqa")
    scale = HEAD_DIM ** -0.5
    ocs, lses = [], []
    for gi, (_, dil) in enumerate(DIL_PATTERNS):
        o, lse = _band_fwd_call(projs[gi], DIL_BLK if gi == 0 else 0, prm["bias_dil"][gi], prm["no_sink"], dil, 1,
                                QW, scale, "dil%d" % gi)
        ocs.append(o)
        lses.append(lse)
    o_d, lse_d = _band_fwd_call(proj, WIN_BLK, prm["bias_win"], prm["sink"], 1, WIN_HEADS // WIN_KV_HEADS,
                                WIN_KV_HEADS * HEAD_DIM, scale, "win")
    rows = _merge_rows(proj, o_a, o_b.reshape(GQA_HEADS, s, HEAD_DIM), ocs, lses, o_d, merge["tm"])
    mix = _fwd_call("merge", _merge_tile, merge["steps"], rows, [prm["w_branch"]], _merge_aux(aux),
                    [((s, prm["w_branch"].shape[2]), _rows(merge["tm"], prm["w_branch"].shape[2]), BF16)])[0]
    return mix, (q_a, k_a, v_a, o_a, lse_a, q_b, k_b, v_b, o_b, lse_b, ocs, lses, o_d, lse_d)


def _mixer_bwd(projs, prm, aux, saved, dmix):
    proj = projs[0]
    s = proj.shape[0]
    q_a, k_a, v_a, o_a, lse_a, q_b, k_b, v_b, o_b, lse_b, ocs, lses, o_d, lse_d = saved
    dils = [dil for _, dil in DIL_PATTERNS]
    mla, gqa, merge = _mixer_calls(proj, prm, aux)
    tm, d_model = merge["tm"], prm["w_branch"].shape[2]
    grp = GQA_HEADS // GQA_KV_HEADS
    scale = HEAD_DIM ** -0.5

    rows = _merge_rows(proj, o_a, o_b.reshape(GQA_HEADS, s, HEAD_DIM), ocs, lses, o_d, tm)
    grads, (dw_branch,) = _vjp_call(
        "merge", _merge_tile, merge["steps"], rows, [prm["w_branch"]], _merge_aux(aux), [(dmix, _rows(tm, d_model))],
        [merge["window"]] + [(a.shape, spec) for a, spec in rows[1:]])
    dproj, do_a, do_b, docs, dlses, do_d = grads[0], grads[1], grads[2], grads[3:6], grads[6:9], grads[9]

    dq_a, dk_a, dv_a = _dense_bwd_call(q_a, k_a, v_a, o_a, lse_a, do_a, MLA_QK ** -0.5, "mla")
    (dproj,), dmla = _vjp_call("prep_mla", _mla_tile, mla["steps"], mla["rows"], mla["params"], mla["aux"],
                               [(t, spec) for t, (_, spec) in zip((dq_a, dk_a, dv_a), mla["outs"])],
                               [mla["window"]], into=dproj)
    dq_b, dk_b, dv_b = _dense_bwd_call(q_b.reshape(GQA_KV_HEADS, grp * s, HEAD_DIM), k_b, v_b, o_b, lse_b,
                                       do_b.reshape(GQA_KV_HEADS, grp * s, HEAD_DIM), scale, "gqa")
    (dproj,), dgqa = _vjp_call("prep_gqa", _gqa_tile, gqa["steps"], gqa["rows"], gqa["params"], gqa["aux"],
                               [(t, spec) for t, (_, spec) in zip((dq_b.reshape(GQA_HEADS, s, HEAD_DIM), dk_b, dv_b),
                                                                  gqa["outs"])],
                               [gqa["window"]], into=dproj)
    dproj, dbias_win, dsink = _band_bwd_call(proj, o_d, do_d, lse_d, None, prm["bias_win"], prm["sink"], dproj,
                                             WIN_BLK, 1, WIN_HEADS // WIN_KV_HEADS, WIN_KV_HEADS * HEAD_DIM, scale, "win")
    dbias_dil, dprojs = [], []
    for gi, dil in enumerate(dils):
        dside, dbias, _ = _band_bwd_call(
            projs[gi], ocs[gi], docs[gi].reshape(s, QW), lses[gi], dlses[gi].reshape(s, QW),
            prm["bias_dil"][gi], prm["no_sink"], dproj if gi == 0 else None, DIL_BLK if gi == 0 else 0, dil, 1, QW,
            scale, "dil%d" % gi)
        if gi == 0:
            dproj = dside
        else:
            dprojs.append(dside)
        dbias_dil.append(dbias)
    dprm = dict(g_q=dmla[0], g_kv=dmla[1], w_q=dmla[2], w_k=dmla[3], w_v=dmla[4], gq=dgqa[0], gk=dgqa[1],
                bias_dil=dbias_dil, bias_win=dbias_win, sink=dsink, no_sink=jnp.zeros_like(prm["no_sink"]),
                w_branch=dw_branch)
    return [dproj] + dprojs, {k: jax.tree.map(lambda g, p: g.astype(p.dtype), v, prm[k]) for k, v in dprm.items()}


def _layer_fwd(x, w, aux):
    s, d = x.shape
    tr = _pick(s, (256,))
    dils = [dil for _, dil in DIL_PATTERNS]

    def norm_forms(r, p, a):
        y = _rms(r[0], p[0])
        return [y, y.T] + [_dot(q, y, 1, 0).reshape(dil, tr // dil, d) for q, dil in zip(a, dils[1:])]

    forms = _fwd_call(
        "norm", norm_forms, s // tr, [(x, _rows(tr, d))], [w["norm_g"]], [(q, _whole(q.shape)) for q in aux[9:11]],
        [((s, d), _rows(tr, d), BF16), ((d, s), pl.BlockSpec((d, tr), lambda i: (0, i)), BF16)]
        + [((dil, s // dil, d), pl.BlockSpec((dil, tr // dil, d), lambda i: (0, i, 0)), BF16) for dil in dils[1:]])
    xn_t, xns = forms[1], [forms[0]] + [t.reshape(s, d) for t in forms[2:]]
    projs = [_mm(a, b, "nt", "proj%d_fwd" % i, BF16) for i, (a, b) in enumerate(zip(xns, w["w_in_t"]))]
    mix, saved = _mixer_fwd(projs, w["mixer"], aux)
    return _mm(mix, w["w_out"], "nn", "out_proj_nn"), (x, w, aux, xns, xn_t, projs, mix, saved)


@jax.custom_vjp
def _layer_core(x, w, aux):
    return _layer_fwd(x, w, aux)[0]


def _layer_core_bwd(res, dout):
    x, w, aux, xns, xn_t, projs, mix, saved = res
    s, d = x.shape
    tr = _pick(s, (256, 128, 64, 32, 16, 8))
    dils = [dil for _, dil in DIL_PATTERNS]
    dmix = _mm(dout, w["w_out"], "nt", "out_proj_nt")
    dw_out = _mm(mix, dout, "tn", "out_proj_tn", w["w_out"].dtype)
    dprojs, dmixer = _mixer_bwd(projs, w["mixer"], aux, saved, dmix)
    dxn = None
    for i, (dp, wi, r) in enumerate(zip(dprojs, w["w_in_t"], dils)):
        part = _mm(_from_residues(dp, r), wi, "nn", "proj%d_dx" % i)
        dxn = part if dxn is None else dxn + part
    dw_in_t = [_mm(xn_t, dprojs[0], "nn", "proj0_dw", w["w_in_t"][0].dtype).T]
    dw_in_t += [_mm(dp, a, "tn", "proj%d_dw" % i, wi.dtype)
                for i, (a, dp, wi) in list(enumerate(zip(xns, dprojs, w["w_in_t"])))[1:]]
    (dx,), (dg,) = _vjp_call("norm", _norm_tile, s // tr, [(x, _rows(tr, d))], [w["norm_g"]], [],
                             [(dxn, _rows(tr, d))], [((s, d), _rows(tr, d))])
    dw = dict(norm_g=dg, w_in_t=dw_in_t, mixer=dmixer, w_out=dw_out)
    return dx, dw, tuple(jnp.zeros_like(t) for t in aux)


_layer_core.defvjp(lambda x, w, aux: _layer_fwd(x, w, aux), _layer_core_bwd)


def _rope_angles(pos, dim):
    inv = ROPE_THETA ** (-jnp.arange(0, dim, 2, dtype=F32) / dim)
    return pos.astype(F32)[:, None] * inv[None, :]


def _rope_tables(s):
    pos = jnp.arange(s, dtype=jnp.int32)
    rows = s // GRID_W
    row_idx = jnp.repeat(jnp.arange(rows, dtype=jnp.int32), GRID_W)
    col_idx = jnp.tile(jnp.arange(GRID_W, dtype=jnp.int32), rows)
    a1 = _rope_angles(pos, MLA_ROPE)
    ar = _rope_angles(row_idx, HEAD_DIM // 2)
    ac = _rope_angles(col_idx, HEAD_DIM // 2)
    ones, zeros = jnp.ones((s, MLA_NOPE), F32), jnp.zeros((s, MLA_NOPE), F32)
    mla_cos = jnp.tile(jnp.concatenate([ones, jnp.cos(a1), jnp.cos(a1)], axis=1), (1, MLA_HEADS))
    mla_sin = jnp.tile(jnp.concatenate([zeros, -jnp.sin(a1), jnp.sin(a1)], axis=1), (1, MLA_HEADS))
    gqa_cos = jnp.tile(jnp.concatenate([jnp.cos(ar), jnp.cos(ar), jnp.cos(ac), jnp.cos(ac)], axis=1), (1, GQA_HEADS))
    gqa_sin = jnp.tile(jnp.concatenate([-jnp.sin(ar), jnp.sin(ar), -jnp.sin(ac), jnp.sin(ac)], axis=1), (1, GQA_HEADS))
    return mla_cos, mla_sin, gqa_cos, gqa_sin


def _t5_bucket(rel):
    nb = T5_BUCKETS // 2
    max_exact = nb // 2
    n = jnp.abs(rel)
    nf = jnp.maximum(n, 1).astype(F32)
    large = max_exact + (jnp.log(nf / max_exact) / math.log(T5_MAX_DIST / max_exact) * (nb - max_exact)).astype(jnp.int32)
    large = jnp.minimum(large, nb - 1)
    return jnp.where(rel > 0, nb, 0) + jnp.where(n < max_exact, n, large)


def _band_bias(table, stride, head_lo, heads, half_window):
    b = BAND_BLOCK
    offs = jnp.arange(3 * b)[None, :] - b - jnp.arange(b)[:, None]
    one_hot = (_t5_bucket(offs * stride)[..., None] == jnp.arange(T5_BUCKETS)).astype(F32)
    bias = jnp.dot(one_hot.reshape(b * 3 * b, T5_BUCKETS), table[:, head_lo:head_lo + heads],
                   precision=lax.Precision.HIGHEST)
    bias = bias.T.reshape(heads, b, 3 * b)
    return jnp.where((jnp.abs(offs) <= half_window)[None], bias, NEG_INF)


def _w_in_rows(d):
    mla, gqa, win, dil0 = MLA_BLK * SMALL_W, GQA_BLK * SMALL_W, WIN_BLK * BAND_W, DIL_BLK * BAND_W
    plan, at = [], 0
    for width, target, row in ((256, 0, mla), (128, 0, mla + 256), (32, 0, mla + 384),
                               (256, 0, gqa), (128, 0, gqa + 256), (128, 0, gqa + 384)):
        plan.append((at, width, target, row))
        at += width
    for part in range(3):
        for g in range(len(DIL_PATTERNS)):
            plan.append((at, QW, g, (dil0 if g == 0 else 0) + part * QW))
            at += QW
    for width, row in ((256, win), (128, win + 256), (128, win + 384), (N_BRANCH * BRANCH_W, 0),
                       (N_BRANCH * d, N_BRANCH * BRANCH_W)):
        plan.append((at, width, 0, row))
        at += width
    return plan


@jax.custom_vjp
def _w_in_layout(w_in_t):
    d = w_in_t.shape[1]
    outs = []
    for target, rows in enumerate((P_TOT, BAND_W, BAND_W)):
        parts, at = [], 0
        for start, width, _, row in sorted((p for p in _w_in_rows(d) if p[2] == target), key=lambda p: p[3]):
            if row > at:
                parts.append(jnp.zeros((row - at, d), w_in_t.dtype))
            parts.append(w_in_t[start:start + width])
            at = row + width
        if at < rows:
            parts.append(jnp.zeros((rows - at, d), w_in_t.dtype))
        outs.append(jnp.concatenate(parts, axis=0))
    return outs


def _w_in_layout_fwd(w_in_t):
    return _w_in_layout(w_in_t), None


def _w_in_layout_bwd(_, cts):
    d = cts[0].shape[1]
    return (jnp.concatenate([cts[target][row:row + width] for _, width, target, row in _w_in_rows(d)], axis=0),)


_w_in_layout.defvjp(_w_in_layout_fwd, _w_in_layout_bwd)


def _layer(x, w, l, aux, biases):
    w_kv = w["w_kv_t"][l].T.reshape(MLA_KV_LORA, MLA_HEADS, MLA_NOPE + MLA_V)
    w_k = jnp.concatenate([w_kv[:, :, :MLA_NOPE], jnp.zeros((MLA_KV_LORA, MLA_HEADS, MLA_ROPE), w_kv.dtype)], axis=2)
    dil_bias, win_bias = biases
    prm = dict(
        g_q=w["mla_q_norm_g"][l][None, :], g_kv=w["mla_kv_norm_g"][l][None, :], w_q=w["w_q_t"][l].T,
        w_k=w_k.reshape(MLA_KV_LORA, MLA_HEADS * MLA_QK),
        w_v=w_kv[:, :, MLA_NOPE:].reshape(MLA_KV_LORA, MLA_HEADS * MLA_V),
        gq=jnp.tile(w["gqa_q_norm_g"][l], GQA_HEADS)[None, :], gk=jnp.tile(w["gqa_k_norm_g"][l], GQA_KV_HEADS)[None, :],
        bias_dil=list(dil_bias), bias_win=win_bias, sink=w["win_sink"][l].reshape(WIN_HEADS, 1, 1),
        no_sink=jnp.full((DIL_HEADS, 1, 1), NEG_INF, F32), w_branch=jnp.transpose(w["w_branch_t"][l].reshape(-1, N_BRANCH, BRANCH_W), (1, 2, 0)))
    layer_w = dict(norm_g=w["norm_g"][l][None, :], w_in_t=_w_in_layout(w["w_in_t"][l]), mixer=prm, w_out=w["w_out"][l])
    return x + _layer_core(x, layer_w, aux)


def _local_loss(w, x, target):
    s, d_model = x.shape
    assert d_model == D_MODEL, "the projection's window layout is laid out for d_model 1024"
    place = np.zeros((MLA_ROPE, MLA_HEADS * MLA_QK), np.float32)
    for h in range(MLA_HEADS):
        for i in range(MLA_ROPE):
            place[i, h * MLA_QK + MLA_NOPE + i] = 1.0

    def head_mean(nh):
        m = np.kron(np.eye(nh, dtype=np.float32), np.full((HEAD_DIM, HEAD_DIM), 1.0 / HEAD_DIM, np.float32))
        return jnp.asarray(m)

    merge_tile = _pick(s, (128,))
    norm_tile = _pick(s, (256,))
    aux = _rope_tables(s) + (jnp.asarray(place), head_mean(GQA_HEADS), head_mean(GQA_KV_HEADS)) + tuple(
        jnp.asarray(_interleave_matrix(merge_tile, dil)) for _, dil in DIL_PATTERNS[1:]) + tuple(
        jnp.asarray(_interleave_matrix(norm_tile, dil).T) for _, dil in DIL_PATTERNS[1:])
    table = w["t5_table"]
    dil_bias = [_band_bias(table, dil, gi * DIL_HEADS, DIL_HEADS, window // (2 * dil))
                for gi, (window, dil) in enumerate(DIL_PATTERNS)]
    win_bias = _band_bias(table, 1, len(DIL_PATTERNS) * DIL_HEADS, WIN_HEADS, WIN_HALF)
    for l in range(w["norm_g"].shape[0]):
        x = _layer(x, w, l, aux, (dil_bias, win_bias))
    return _loss_op(x, target, w["final_norm_g"][None, :])


_ANY = pl.BlockSpec(memory_space=pl.ANY)
_MESH = pl.DeviceIdType.MESH


def _all_gather(block, name):
    def body(x_ref, out_ref, send_sems, recv_sems, local_sem):
        x, y, c = lax.axis_index("x"), lax.axis_index("y"), lax.axis_index("c")
        me, sibling = (x, y, c), (x, y, 1 - c)
        chips = [(1 - x, y), (x, 1 - y), (1 - x, 1 - y)]

        def slot(px, py, pc):
            return out_ref.at[4 * px + 2 * py + pc]

        def copy(k, blk, to, src=None):
            return pltpu.make_async_remote_copy(
                src_ref=slot(*blk) if src is None else src, dst_ref=slot(*blk),
                send_sem=send_sems.at[k], recv_sem=recv_sems.at[k], device_id=to, device_id_type=_MESH)

        mine = pltpu.make_async_copy(x_ref, slot(*me), local_sem)
        mine.start()
        first = [copy(0, me, sibling, src=x_ref)]
        first += [copy(1 + j, me, (*chip, c), src=x_ref) for j, chip in enumerate(chips)]
        for cp in first:
            cp.start()
        passed = [copy(4 + j, (*chip, c), sibling) for j, chip in enumerate(chips)]
        for j, chip in enumerate(chips):
            copy(1 + j, (*chip, c), me).wait_recv()
            passed[j].start()
        copy(0, sibling, me).wait_recv()
        for j, chip in enumerate(chips):
            copy(4 + j, (*chip, 1 - c), me).wait_recv()
        for cp in first + passed:
            cp.wait_send()
        mine.wait()

    return pl.pallas_call(
        body,
        out_shape=jax.ShapeDtypeStruct((N_DEV,) + block.shape, block.dtype),
        in_specs=[_ANY],
        out_specs=_ANY,
        scratch_shapes=[pltpu.SemaphoreType.DMA((7,)), pltpu.SemaphoreType.DMA((7,)), pltpu.SemaphoreType.DMA],
        name=name,
    )(block)


def _swap_with_sibling(blocks, name):
    chips = blocks.shape[0]

    def body(x_ref, out_ref, send_sems, recv_sems):
        x, y, c = lax.axis_index("x"), lax.axis_index("y"), lax.axis_index("c")
        copies = [pltpu.make_async_remote_copy(
            src_ref=x_ref.at[k, 1 - c], dst_ref=out_ref.at[k], send_sem=send_sems.at[k], recv_sem=recv_sems.at[k],
            device_id=(x, y, 1 - c), device_id_type=_MESH) for k in range(chips)]
        for cp in copies:
            cp.start()
        for cp in copies:
            cp.wait()

    return pl.pallas_call(
        body,
        out_shape=jax.ShapeDtypeStruct((chips,) + blocks.shape[2:], blocks.dtype),
        in_specs=[_ANY],
        out_specs=_ANY,
        scratch_shapes=[pltpu.SemaphoreType.DMA((chips,)), pltpu.SemaphoreType.DMA((chips,))],
        name=name,
    )(blocks)


def _add_sibling(blocks, theirs, name):
    chips, _, rows, w = blocks.shape
    tr = _row_tile(rows, 16, 4096)

    def body(b_ref, t_ref, o_ref):
        mine = b_ref[0, lax.axis_index("c")]
        o_ref[0] = (mine.astype(F32) + t_ref[0].astype(F32)).astype(o_ref.dtype)

    return pl.pallas_call(
        body,
        grid=(chips, rows // tr),
        in_specs=[pl.BlockSpec((1, 2, tr, w), lambda k, i: (k, 0, i, 0)), pl.BlockSpec((1, tr, w), lambda k, i: (k, i, 0))],
        out_specs=pl.BlockSpec((1, tr, w), lambda k, i: (k, i, 0)),
        out_shape=jax.ShapeDtypeStruct(theirs.shape, theirs.dtype),
        name=name,
        compiler_params=_params("parallel", "parallel"),
    )(blocks, theirs)


def _exchange_chips(partials, name):
    n_chips = partials.shape[0]

    def body(x_ref, out_ref, send_sems, recv_sems, local_sem):
        x, y, c = lax.axis_index("x"), lax.axis_index("y"), lax.axis_index("c")
        me = 2 * x + y
        mine = pltpu.make_async_copy(x_ref.at[me], out_ref.at[me], local_sem)
        mine.start()
        copies, landed = [], []
        for k in range(1, n_chips):
            px = 1 - x if k & 2 else x
            py = 1 - y if k & 1 else y
            peer = 2 * px + py
            copies.append(pltpu.make_async_remote_copy(
                src_ref=x_ref.at[peer], dst_ref=out_ref.at[me], send_sem=send_sems.at[k - 1],
                recv_sem=recv_sems.at[k - 1], device_id=(px, py, c), device_id_type=_MESH))
            landed.append(pltpu.make_async_remote_copy(
                src_ref=x_ref.at[peer], dst_ref=out_ref.at[peer], send_sem=send_sems.at[k - 1],
                recv_sem=recv_sems.at[k - 1], device_id=(px, py, c), device_id_type=_MESH))
        for cp in copies:
            cp.start()
        for cp in landed:
            cp.wait_recv()
        for cp in copies:
            cp.wait_send()
        mine.wait()

    return pl.pallas_call(
        body,
        out_shape=jax.ShapeDtypeStruct(partials.shape, partials.dtype),
        in_specs=[_ANY],
        out_specs=_ANY,
        scratch_shapes=[pltpu.SemaphoreType.DMA((n_chips - 1,)), pltpu.SemaphoreType.DMA((n_chips - 1,)),
                        pltpu.SemaphoreType.DMA],
        name=name,
    )(partials)


def _sum_slots(parts, name):
    slots, rows, w = parts.shape
    tr = _row_tile(rows, 16 if parts.dtype == BF16 else 8, 4096)

    def body(p_ref, o_ref):
        acc = p_ref[0].astype(F32)
        for j in range(1, slots):
            acc = acc + p_ref[j].astype(F32)
        o_ref[...] = acc

    return pl.pallas_call(
        body,
        grid=(rows // tr,),
        in_specs=[pl.BlockSpec((slots, tr, w), lambda i: (0, i, 0))],
        out_specs=pl.BlockSpec((tr, w), lambda i: (i, 0)),
        out_shape=jax.ShapeDtypeStruct((rows, w), F32),
        name=name,
        compiler_params=_params("parallel"),
    )(parts)


def _adamw(w, g, m, v, name):
    rows, width = w.shape
    tr = _row_tile(rows, 8, 2048)

    def body(w_ref, g_ref, m_ref, v_ref, d_ref, nm_ref, nv_ref):
        g_ = g_ref[...]
        m_ = ADAM_B1 * m_ref[...] + (1.0 - ADAM_B1) * g_
        v_ = ADAM_B2 * v_ref[...] + (1.0 - ADAM_B2) * jnp.square(g_)
        m_hat = m_ / (1.0 - ADAM_B1 ** ADAM_STEP)
        v_hat = v_ / (1.0 - ADAM_B2 ** ADAM_STEP)
        d_ref[...] = -ADAM_LR * (m_hat / (jnp.sqrt(v_hat) + ADAM_EPS) + ADAM_WD * w_ref[...])
        nm_ref[...] = m_
        nv_ref[...] = v_

    spec = pl.BlockSpec((tr, width), lambda i: (i, 0))
    return pl.pallas_call(
        body,
        grid=(rows // tr,),
        in_specs=[spec] * 4,
        out_specs=[spec] * 3,
        out_shape=[jax.ShapeDtypeStruct((rows, width), F32)] * 3,
        name=name,
        compiler_params=_params("parallel"),
    )(w, g, m, v)


_SHARDED = (("w_in", 2), ("w_mla_q_up", 2), ("w_mla_kv_up", 2), ("w_branch", 3), ("w_out", 1))
_REPLICATED = ("norm_g", "mla_q_norm_g", "mla_kv_norm_g", "gqa_q_norm_g", "gqa_k_norm_g", "win_sink", "t5_table",
               "final_norm_g")


def _pack(arrays, row_multiple):
    flat = jnp.concatenate([a.reshape(-1) for a in arrays])
    rows = -(-flat.shape[0] // (LANES * row_multiple)) * row_multiple
    return jnp.pad(flat, (0, rows * LANES - flat.shape[0])).reshape(rows, LANES)


def _unpack(packed, shapes):
    flat, out, at = packed.reshape(-1), [], 0
    for shp in shapes:
        n = int(np.prod(shp))
        out.append(flat[at:at + n].reshape(shp))
        at += n
    return out


_TO_WIRE = {
    "w_in": lambda t: jnp.swapaxes(t, 1, 2), "w_mla_q_up": lambda t: jnp.swapaxes(t, 1, 2),
    "w_mla_kv_up": lambda t: jnp.swapaxes(t, 1, 2),
    "w_branch": lambda t: jnp.transpose(t, (0, 3, 1, 2)).reshape(t.shape[0], t.shape[3], -1), "w_out": lambda t: t}
_FROM_WIRE = {
    "w_in": lambda t, shp: jnp.swapaxes(t, 1, 2), "w_mla_q_up": lambda t, shp: jnp.swapaxes(t, 1, 2),
    "w_mla_kv_up": lambda t, shp: jnp.swapaxes(t, 1, 2),
    "w_branch": lambda t, shp: jnp.transpose(t.reshape(shp[0], shp[3], shp[1], shp[2]), (0, 2, 3, 1)),
    "w_out": lambda t, shp: t}
_WIRE_NAME = {"w_in": "w_in_t", "w_mla_q_up": "w_q_t", "w_mla_kv_up": "w_kv_t", "w_branch": "w_branch_t",
              "w_out": "w_out"}


def _transpose_blocks(t, dtype, name):
    depth, a, b = t.shape

    def body(x_ref, o_ref):
        o_ref[0] = x_ref[0].T.astype(o_ref.dtype)

    return pl.pallas_call(
        body,
        grid=(depth,),
        in_specs=[pl.BlockSpec((1, a, b), lambda i: (i, 0, 0))],
        out_specs=pl.BlockSpec((1, b, a), lambda i: (i, 0, 0)),
        out_shape=jax.ShapeDtypeStruct((depth, b, a), dtype),
        name=name,
        compiler_params=_params("parallel"),
    )(t)


def _join_shards(gathered, wire_shapes):
    out, at = [], 0
    for depth, cut, rest in wire_shapes:
        n = depth * cut * rest // LANES
        blk = gathered[:, at:at + n].reshape(N_DEV, depth, cut, rest)
        out.append(jnp.moveaxis(blk, 0, 1).reshape(depth, N_DEV * cut, rest))
        at += n
    return out


def _split_shards(fulls, wire_shapes):
    parts = []
    for full, (depth, cut, rest) in zip(fulls, wire_shapes):
        blk = jnp.moveaxis(full.reshape(depth, N_DEV, cut, rest), 1, 0)
        parts.append(blk.reshape(N_DEV, depth * cut * rest // LANES, LANES))
    packed = jnp.concatenate(parts, axis=1)
    return packed.reshape((N_DEV // 2, 2) + packed.shape[1:])


def kernel(x, norm_g, w_in, mla_q_norm_g, mla_kv_norm_g, w_mla_q_up, w_mla_kv_up, gqa_q_norm_g, gqa_k_norm_g, win_sink, t5_table, w_branch, w_out, final_norm_g, loss_target, m_norm_g, m_w_in, m_mla_q_norm_g, m_mla_kv_norm_g, m_w_mla_q_up, m_w_mla_kv_up, m_gqa_q_norm_g, m_gqa_k_norm_g, m_win_sink, m_t5_table, m_w_branch, m_w_out, m_final_norm_g, v_norm_g, v_w_in, v_mla_q_norm_g, v_mla_kv_norm_g, v_w_mla_q_up, v_w_mla_kv_up, v_gqa_q_norm_g, v_gqa_k_norm_g, v_win_sink, v_t5_table, v_w_branch, v_w_out, v_final_norm_g):
    given = dict(locals())
    names = ("norm_g", "w_in", "mla_q_norm_g", "mla_kv_norm_g", "w_mla_q_up", "w_mla_kv_up", "gqa_q_norm_g",
             "gqa_k_norm_g", "win_sink", "t5_table", "w_branch", "w_out", "final_norm_g")
    shard_names = [n for n, _ in _SHARDED]
    shard_shapes = [given[n].shape for n in shard_names]

    wire = [_transpose_blocks(given[n], BF16, "w_in_to_wire") if n == "w_in" else _TO_WIRE[n](given[n]).astype(BF16)
            for n in shard_names]
    wire_shapes = [t.shape for t in wire]
    gathered = _all_gather(jnp.concatenate([t.reshape(-1, LANES) for t in wire]), "gather_weights")
    weights = {n: given[n] for n in _REPLICATED}
    weights.update(zip([_WIRE_NAME[n] for n in shard_names], _join_shards(gathered, wire_shapes)))

    loss, (gw, gx) = jax.value_and_grad(_local_loss, argnums=(0, 1))(weights, x[0], loss_target[0])
    loss = lax.psum(loss, ("x", "y", "c"))

    send = _split_shards([gw[_WIRE_NAME[n]] for n in shard_names], wire_shapes)
    partials = _add_sibling(send, _swap_with_sibling(send, "swap_grads"), "add_sibling_grads")
    g_wire = _unpack(_sum_slots(_exchange_chips(partials, "scatter_grads"), "sum_grads"), wire_shapes)
    g_shard = [_transpose_blocks(t, F32, "w_in_from_wire") if n == "w_in" else _FROM_WIRE[n](t, shp)
               for n, t, shp in zip(shard_names, g_wire, shard_shapes)]
    rep_shapes = [given[n].shape for n in _REPLICATED]
    g_rep = _unpack(_sum_slots(_all_gather(_pack([gw[n] for n in _REPLICATED], 8), "gather_small_grads"),
                               "sum_small_grads"), rep_shapes)
    grads = dict(zip(shard_names, g_shard))
    grads.update(zip(_REPLICATED, g_rep))

    def update(group, shapes, row_multiple, name):
        outs = _adamw(*[_pack([src[n] for n in group], row_multiple) for src in (
            given, grads, {n: given["m_" + n] for n in group}, {n: given["v_" + n] for n in group})], name)
        return [dict(zip(group, _unpack(o, shapes))) for o in outs]

    big = update(shard_names, shard_shapes, 16, "adamw_shards")
    small = update(list(_REPLICATED), rep_shapes, 8, "adamw_replicated")
    delta, new_m, new_v = [{**b, **s_} for b, s_ in zip(big, small)]
    return (loss, gx[None], *[grads[n] for n in names], *[delta[n] for n in names],
            *[new_m[n] for n in names], *[new_v[n] for n in names])
```

```python
import functools
import math

import jax
import jax.numpy as jnp
import numpy as np
from jax import lax
from jax.experimental import pallas as pl
from jax.experimental.pallas import tpu as pltpu

F32 = jnp.float32
BF16 = jnp.bfloat16
N_DEV = 8
LANES = 128
HALF = LANES // 2
V7X_VMEM_LIMIT = 56 * 1024 * 1024

EPS = 1e-6
NEG_INF = -1e30
LOG2E = 1.4426950408889634
ROPE_THETA = 10000.0
GRID_W = 64
HEAD_DIM = 64
N_BRANCH = 4
BRANCH_W = 256
MLA_HEADS, MLA_Q_LORA, MLA_KV_LORA, MLA_NOPE, MLA_ROPE, MLA_V = 4, 256, 128, 64, 32, 64
MLA_QK = MLA_NOPE + MLA_ROPE
GQA_HEADS, GQA_KV_HEADS = 4, 2
DIL_PATTERNS = ((128, 1), (512, 4), (2048, 16))
DIL_HEADS = 4
WIN_HEADS, WIN_KV_HEADS, WIN_HALF = 4, 2, 128
T5_BUCKETS, T5_MAX_DIST = 32, 1024
BAND_BLOCK = 128
ADAM_LR, ADAM_B1, ADAM_B2, ADAM_EPS, ADAM_WD, ADAM_STEP = 0.001, 0.9, 0.999, 1e-08, 0.01, 10

D_MODEL = 1024
GM_W, SMALL_W, BAND_W = 5120, 512, 768
MLA_BLK, GQA_BLK, WIN_BLK, DIL_BLK = 10, 11, 8, 9
P_TOT = 7680
QW = 256


def _params(*sem):
    return pltpu.CompilerParams(dimension_semantics=sem, vmem_limit_bytes=V7X_VMEM_LIMIT)


def _pick(n, cands):
    for c in cands:
        if n % c == 0:
            return c
    return n


def _row_tile(rows, unit, cap):
    best = unit
    for t in range(unit, min(rows, cap) + 1, unit):
        if rows % t == 0:
            best = t
    assert rows % best == 0
    return best


def _dot(a, b, ca, cb):
    return lax.dot_general(a.astype(BF16), b.astype(BF16), (((ca,), (cb,)), ((), ())), preferred_element_type=F32)


def _bmm(a, b, ca, cb):
    return lax.dot_general(a, b, (((ca,), (cb,)), ((0,), (0,))), preferred_element_type=F32)


@jax.custom_vjp
def _bdot(a, b):
    return _dot(a, b, 1, 0)


def _bdot_fwd(a, b):
    return _dot(a, b, 1, 0), (a, b)


def _bdot_bwd(res, g):
    a, b = res
    return _dot(g, b, 1, 1), _dot(a, g, 0, 0)


_bdot.defvjp(_bdot_fwd, _bdot_bwd)


def _hdot(a, c):
    return lax.dot_general(a, c, (((1,), (0,)), ((), ())), precision=lax.Precision.HIGHEST, preferred_element_type=F32)


@functools.partial(jax.custom_vjp, nondiff_argnums=(1,))
def _lane_roll(x, shift):
    return pltpu.roll(x, shift, 1)


def _lane_roll_fwd(x, shift):
    return pltpu.roll(x, shift, 1), None


def _lane_roll_bwd(shift, _, g):
    return (pltpu.roll(g, g.shape[1] - shift, 1),)


_lane_roll.defvjp(_lane_roll_fwd, _lane_roll_bwd)


@functools.partial(jax.custom_vjp, nondiff_argnums=(1,))
def _lane_ranges(x, cut):
    bounds, _ = cut
    return tuple(x[:, lo:hi] for lo, hi in zip(bounds[:-1], bounds[1:]))


def _lane_ranges_fwd(x, cut):
    return _lane_ranges(x, cut), None


def _lane_ranges_bwd(cut, _, cts):
    bounds, width = cut
    parts = list(cts)
    if bounds[-1] < width:
        parts.append(jnp.zeros((cts[0].shape[0], width - bounds[-1]), cts[0].dtype))
    return (jnp.concatenate(parts, axis=1),)


_lane_ranges.defvjp(_lane_ranges_fwd, _lane_ranges_bwd)


def _lanes(x, bounds):
    return _lane_ranges(x, (tuple(bounds), x.shape[1]))


@jax.custom_vjp
def _unstack(x):
    return tuple(x[i] for i in range(x.shape[0]))


def _unstack_fwd(x):
    return _unstack(x), None


def _unstack_bwd(_, cts):
    return (jnp.stack(cts, axis=0),)


_unstack.defvjp(_unstack_fwd, _unstack_bwd)


@functools.partial(jax.custom_vjp, nondiff_argnums=(1,))
def _split_heads(x, h):
    d = x.shape[1] // h
    return jnp.stack([x[:, i * d:(i + 1) * d] for i in range(h)], axis=0)


def _split_heads_fwd(x, h):
    return _split_heads(x, h), None


def _split_heads_bwd(h, _, ct):
    return (jnp.concatenate([ct[i] for i in range(h)], axis=1),)


_split_heads.defvjp(_split_heads_fwd, _split_heads_bwd)


def _join_heads(x):
    return jnp.concatenate(_unstack(x), axis=1)


def _rope(x, cos_t, sin_t, half):
    w = x.shape[1]
    lane = lax.broadcasted_iota(jnp.int32, (1, w), 1)
    first = (lane % (2 * half)) < half
    partner = jnp.where(first, _lane_roll(x, w - half), _lane_roll(x, half))
    return x * cos_t + partner * sin_t


def _rms(x, g):
    return x * lax.rsqrt(jnp.mean(x * x, axis=-1, keepdims=True) + EPS) * g


def _rows(tr, w, col=0):
    return pl.BlockSpec((tr, w), lambda i: (i, col))


def _head_rows(h, tr, d):
    return pl.BlockSpec((h, tr, d), lambda i: (0, i, 0))


def _whole(shape):
    nd = len(shape)
    return pl.BlockSpec(tuple(shape), lambda i: (0,) * nd)


def _fwd_call(name, fn, steps, rows, params, aux, outs):
    nr, npar, na = len(rows), len(params), len(aux)

    def body(*refs):
        vals = [x[...].astype(F32) for x in refs[:nr + npar + na]]
        res = fn(vals[:nr], vals[nr:nr + npar], vals[nr + npar:])
        for o_ref, o in zip(refs[nr + npar + na:], res):
            o_ref[...] = o.astype(o_ref.dtype)

    return pl.pallas_call(
        body,
        grid=(steps,),
        in_specs=[s for _, s in rows] + [_whole(p.shape) for p in params] + [s for _, s in aux],
        out_specs=[e[1] for e in outs],
        out_shape=[jax.ShapeDtypeStruct(e[0], e[2] if len(e) > 2 else F32) for e in outs],
        name=name + "_fwd",
        compiler_params=_params("parallel"),
    )(*[a for a, _ in rows], *params, *[a for a, _ in aux])


def _vjp_call(name, fn, steps, rows, params, aux, cts, row_grads, into=None):
    nr, npar, na, nc = len(rows), len(params), len(aux), len(cts)
    n_in = nr + npar + na + nc
    lead = 0 if into is None else 1

    def body(*refs):
        refs = refs[lead:]
        vals = [x[...].astype(F32) for x in refs[:n_in]]
        r, p, a, d = vals[:nr], vals[nr:nr + npar], vals[nr + npar:nr + npar + na], vals[nr + npar + na:]
        out_refs = refs[n_in:]
        _, vjp = jax.vjp(lambda r_, p_: tuple(fn(r_, p_, a)), r, p)
        dr, dp = vjp(tuple(d))
        for o_ref, o in zip(out_refs[:nr], dr):
            o_ref[...] = o.astype(o_ref.dtype)

        @pl.when(pl.program_id(0) == 0)
        def _():
            for o_ref in out_refs[nr:]:
                o_ref[...] = jnp.zeros_like(o_ref)

        for o_ref, o in zip(out_refs[nr:], dp):
            o_ref[...] += o

    outs = pl.pallas_call(
        body,
        grid=(steps,),
        in_specs=([] if into is None else [pl.BlockSpec(memory_space=pl.ANY)])
        + [s for _, s in rows] + [_whole(p.shape) for p in params] + [s for _, s in aux] + [s for _, s in cts],
        out_specs=[e[1] for e in row_grads] + [_whole(p.shape) for p in params],
        out_shape=[jax.ShapeDtypeStruct(e[0], e[2] if len(e) > 2 else F32) for e in row_grads]
        + [jax.ShapeDtypeStruct(p.shape, F32) for p in params],
        input_output_aliases={} if into is None else {0: 0},
        name=name + "_bwd",
        compiler_params=_params("arbitrary"),
    )(*([] if into is None else [into]), *[a for a, _ in rows], *params, *[a for a, _ in aux], *[a for a, _ in cts])
    return list(outs[:nr]), list(outs[nr:])


def _norm_tile(r, p, a):
    return (_rms(r[0], p[0]),)


def _mm(a, b, mode, name, out_dtype=F32, plus=None):
    if mode == "nn":
        (m, k), n = a.shape, b.shape[1]
    elif mode == "nt":
        (m, k), n = a.shape, b.shape[0]
    else:
        (k, m), n = a.shape, b.shape[1]
    tn = _pick(n, (1024, 768, 512, 384, 256, 128))
    budget = V7X_VMEM_LIMIT * 3 // 4
    out_bytes = 4 + 2 * np.dtype(out_dtype).itemsize + (0 if plus is None else 2 * plus.dtype.itemsize)

    def tiles():
        for tm in (2048, 1024, 512, 256, 128):
            for tk in (512, 256, 128) if mode == "tn" else (4096, 1024, 768, 512, 384, 256, 128):
                need = 2 * tk * (tm * a.dtype.itemsize + tn * b.dtype.itemsize) + tm * tn * out_bytes
                if m % tm == 0 and k % tk == 0 and need <= budget:
                    return tm, tk
        return _pick(m, (128,)), _pick(k, (128,))

    tm, tk = tiles()
    nk = k // tk

    def body(*refs):
        a_ref, b_ref = refs[:2]
        o_ref, acc_ref = refs[-2:]
        kk = pl.program_id(2)
        if mode == "nn":
            part = _dot(a_ref[...], b_ref[...], 1, 0)
        elif mode == "nt":
            part = _dot(a_ref[...], b_ref[...], 1, 1)
        else:
            part = _dot(a_ref[...], b_ref[...], 0, 0)
        def first():
            return part if plus is None else part + refs[2][...].astype(F32)

        if nk == 1:
            o_ref[...] = first().astype(o_ref.dtype)
        else:
            @pl.when(kk == 0)
            def _():
                acc_ref[...] = first()

            @pl.when(kk > 0)
            def _():
                acc_ref[...] += part

            @pl.when(kk == nk - 1)
            def _():
                o_ref[...] = acc_ref[...].astype(o_ref.dtype)

    if mode == "nn":
        a_spec = pl.BlockSpec((tm, tk), lambda i, j, kk: (i, kk))
        b_spec = pl.BlockSpec((tk, tn), lambda i, j, kk: (kk, j))
    elif mode == "nt":
        a_spec = pl.BlockSpec((tm, tk), lambda i, j, kk: (i, kk))
        b_spec = pl.BlockSpec((tn, tk), lambda i, j, kk: (j, kk))
    else:
        a_spec = pl.BlockSpec((tk, tm), lambda i, j, kk: (kk, i))
        b_spec = pl.BlockSpec((tk, tn), lambda i, j, kk: (kk, j))
    o_spec = pl.BlockSpec((tm, tn), lambda i, j, kk: (i, j))
    return pl.pallas_call(
        body,
        grid=(m // tm, n // tn, nk),
        in_specs=[a_spec, b_spec] + ([] if plus is None else [o_spec]),
        out_specs=o_spec,
        out_shape=jax.ShapeDtypeStruct((m, n), out_dtype),
        scratch_shapes=[pltpu.VMEM((tm, tn), F32)],
        input_output_aliases={} if plus is None else {2: 0},
        name=name,
        compiler_params=_params("parallel", "parallel", "arbitrary"),
    )(a, b, *([] if plus is None else [plus]))


def _dense_fwd_call(q, k, v, scale, name):
    n, sq, d = q.shape
    sk, dv = k.shape[1], v.shape[2]
    tq = _pick(sq, (512, 256, 128))
    c = scale * LOG2E

    nkb = 1

    def body(q_ref, k_ref, v_ref, o_ref, lse_ref, m_s, acc_s, vext_s):
        j = pl.program_id(2)

        @pl.when(j == 0)
        def _():
            m_s[...] = jnp.full_like(m_s, NEG_INF)
            acc_s[...] = jnp.zeros_like(acc_s)
            vext_s[...] = jnp.ones_like(vext_s)

        vext_s[:, :dv] = v_ref[0].astype(BF16)
        m_old = m_s[...]
        s = _dot(q_ref[0], k_ref[0], 1, 1)
        m_new = jnp.maximum(m_old, jnp.max(s, axis=1, keepdims=True))
        p = jnp.exp2(s * c - m_new * c)
        acc = jnp.exp2((m_old - m_new) * c) * acc_s[...] + _dot(p, vext_s[...], 1, 0)
        m_s[...] = m_new
        acc_s[...] = acc

        @pl.when(j == nkb - 1)
        def _():
            l = acc[:, dv:dv + 1]
            o_ref[0] = acc[:, :dv] / l
            lse_ref[0] = m_new * scale + jnp.log(l)

    return pl.pallas_call(
        body,
        grid=(n, sq // tq, nkb),
        in_specs=[
            pl.BlockSpec((1, tq, d), lambda h, i, j: (h, i, 0)),
            pl.BlockSpec((1, sk // nkb, d), lambda h, i, j: (h, j, 0)),
            pl.BlockSpec((1, sk // nkb, dv), lambda h, i, j: (h, j, 0)),
        ],
        out_specs=[
            pl.BlockSpec((1, tq, dv), lambda h, i, j: (h, i, 0)),
            pl.BlockSpec((1, tq, 1), lambda h, i, j: (h, i, 0)),
        ],
        out_shape=[jax.ShapeDtypeStruct((n, sq, dv), F32), jax.ShapeDtypeStruct((n, sq, 1), F32)],
        scratch_shapes=[pltpu.VMEM((tq, 1), F32), pltpu.VMEM((tq, 2 * dv), F32), pltpu.VMEM((sk // nkb, 2 * dv), BF16)],
        name=name + "_fwd",
        compiler_params=_params("parallel", "parallel", "arbitrary"),
    )(q, k, v)


def _dense_bwd_call(q, k, v, o, lse, do, scale, name):
    n, sq, d = q.shape
    sk, dv = k.shape[1], v.shape[2]
    tq, tk = _pick(sq, (1024, 512, 256, 128)), _pick(sk, (2048, 1024, 512, 256, 128))
    c = scale * LOG2E

    def body(q_ref, k_ref, v_ref, o_ref, lse_ref, do_ref, dq_ref, dk_ref, dv_ref):
        j, i = pl.program_id(1), pl.program_id(2)
        qb, kb, vb = q_ref[0].astype(BF16), k_ref[0].astype(BF16), v_ref[0].astype(BF16)
        do_f = do_ref[0]
        dob = do_f.astype(BF16)
        p = jnp.exp2(_dot(qb, kb, 1, 1) * c - lse_ref[0] * LOG2E)
        delta = jnp.sum(do_f * o_ref[0], axis=1, keepdims=True)
        ds = (p * (_dot(dob, vb, 1, 1) - delta)).astype(BF16)
        dv_part = _dot(p, dob, 0, 0)
        dk_part = _dot(ds, qb, 0, 0) * scale
        dq_part = _dot(ds, kb, 1, 0) * scale
        rows = pl.ds(pl.multiple_of(i * tq, tq), tq)

        @pl.when(i == 0)
        def _():
            dk_ref[0] = dk_part
            dv_ref[0] = dv_part

        @pl.when(i > 0)
        def _():
            dk_ref[0] += dk_part
            dv_ref[0] += dv_part

        @pl.when(j == 0)
        def _():
            dq_ref[0, rows, :] = dq_part

        @pl.when(j > 0)
        def _():
            dq_ref[0, rows, :] += dq_part

    return pl.pallas_call(
        body,
        grid=(n, sk // tk, sq // tq),
        in_specs=[
            pl.BlockSpec((1, tq, d), lambda h, j, i: (h, i, 0)),
            pl.BlockSpec((1, tk, d), lambda h, j, i: (h, j, 0)),
            pl.BlockSpec((1, tk, dv), lambda h, j, i: (h, j, 0)),
            pl.BlockSpec((1, tq, dv), lambda h, j, i: (h, i, 0)),
            pl.BlockSpec((1, tq, 1), lambda h, j, i: (h, i, 0)),
            pl.BlockSpec((1, tq, dv), lambda h, j, i: (h, i, 0)),
        ],
        out_specs=[
            pl.BlockSpec((1, sq, d), lambda h, j, i: (h, 0, 0)),
            pl.BlockSpec((1, tk, d), lambda h, j, i: (h, j, 0)),
            pl.BlockSpec((1, tk, dv), lambda h, j, i: (h, j, 0)),
        ],
        out_shape=[
            jax.ShapeDtypeStruct((n, sq, d), F32),
            jax.ShapeDtypeStruct((n, sk, d), F32),
            jax.ShapeDtypeStruct((n, sk, dv), F32),
        ],
        name=name + "_bwd",
        compiler_params=_params("arbitrary", "arbitrary", "arbitrary"),
    )(q, k, v, o, lse, do)


def _head_geometry(h, group):
    pair, a = divmod(h, 2)
    kv_pair, b = divmod(h // group, 2)
    return pair, a, kv_pair, b


def _lane_half():
    return lax.broadcasted_iota(jnp.int32, (1, LANES), 1) // HALF


def _align(x, a, b):
    if a != b:
        x = pltpu.roll(x, HALF, 1)
    return jnp.where(_lane_half() == b, x, 0.0)


def _unalign(x, a, b):
    x = jnp.where(_lane_half() == b, x, 0.0)
    return pltpu.roll(x, HALF, 1) if a != b else x


def _bands(w, pw, nw, lo, kvw, nb):
    b = BAND_BLOCK
    cat = jnp.concatenate([pw[:, lo:lo + kvw], w[:, lo:lo + kvw], nw[:, lo:lo + kvw]], axis=0).astype(BF16)
    out = []
    for g in range(kvw // LANES):
        c3 = cat[:, g * LANES:(g + 1) * LANES].reshape(nb + 2, b, LANES)
        out.append(jnp.concatenate([c3[0:nb], c3[1:nb + 1], c3[2:nb + 2]], axis=1))
    return out


def _edge_mask(first_block, nb, period):
    b = BAND_BLOCK
    blk = (first_block + lax.broadcasted_iota(jnp.int32, (nb, 1, 3 * b), 0)) % period
    col = lax.broadcasted_iota(jnp.int32, (nb, 1, 3 * b), 2)
    outside = ((col < b) & (blk == 0)) | ((col >= 2 * b) & (blk == period - 1))
    return jnp.where(outside, NEG_INF, 0.0)


def _band_geometry(proj, dil):
    rows = proj.shape[0]
    tl = _pick(rows, (1024, 512, 256, 128))
    return rows, tl, tl // BAND_BLOCK, rows // tl, rows // dil // BAND_BLOCK


def _band_in_specs(tl, nb, n_chunks, n_blocks, col, last_step_idle):
    def chunk(i):
        return jnp.minimum(i, n_chunks - 1) if last_step_idle else i

    main = pl.BlockSpec((tl, BAND_W), lambda j, i: (j * n_chunks + chunk(i), col))
    prev = pl.BlockSpec((BAND_BLOCK, BAND_W),
                        lambda j, i: (j * n_blocks + jnp.maximum(chunk(i) * nb - 1, 0), col))
    nxt = pl.BlockSpec((BAND_BLOCK, BAND_W),
                       lambda j, i: (j * n_blocks + jnp.minimum((chunk(i) + 1) * nb, n_blocks - 1), col))
    rows = pl.BlockSpec((tl, QW), lambda j, i: (j * n_chunks + chunk(i), 0))
    return main, prev, nxt, rows


def _band_fwd_call(proj, col, bias, sink, dil, group, kvw, scale, name):
    s_tok = proj.shape[0]
    seq, tl, nb, n_chunks, period = _band_geometry(proj, dil)
    n_blocks = seq // BAND_BLOCK
    heads = bias.shape[0]

    def body(w_ref, pw_ref, nw_ref, bias_ref, sink_ref, o_ref, lse_ref):
        i = pl.program_id(1)
        w, pw, nw = w_ref[...].astype(F32), pw_ref[...].astype(F32), nw_ref[...].astype(F32)
        kb = _bands(w, pw, nw, QW, kvw, nb)
        vb = _bands(w, pw, nw, QW + kvw, kvw, nb)
        edge = _edge_mask(i * nb, nb, period)
        o_acc = [jnp.zeros((tl, LANES), F32) for _ in range(heads // 2)]
        lse_acc = [jnp.zeros((tl, LANES), F32) for _ in range(heads // 2)]
        for h in range(heads):
            pair, a, kvp, b = _head_geometry(h, group)
            q_al = _align(w[:, pair * LANES:(pair + 1) * LANES], a, b).astype(BF16).reshape(nb, BAND_BLOCK, LANES)
            logits = _bmm(q_al, kb[kvp], 2, 2) * scale + bias_ref[h][None] + edge
            sk = sink_ref[h].reshape(1, 1, 1)
            m = jnp.maximum(jnp.max(logits, axis=2, keepdims=True), sk)
            e = jnp.exp(logits - m)
            ssum = jnp.sum(e, axis=2, keepdims=True) + jnp.exp(sk - m)
            out = _bmm(e.astype(BF16), vb[kvp], 2, 1) / ssum
            o_acc[pair] = o_acc[pair] + _unalign(out.reshape(tl, LANES), a, b)
            lse = (m + jnp.log(ssum)).reshape(tl, 1)
            lse_acc[pair] = lse_acc[pair] + jnp.where(_lane_half() == a, lse, 0.0)
        o_ref[...] = jnp.concatenate(o_acc, axis=1)
        lse_ref[...] = jnp.concatenate(lse_acc, axis=1)

    main, prev, nxt, rows = _band_in_specs(tl, nb, n_chunks, n_blocks, col, False)
    return pl.pallas_call(
        body,
        grid=(1, n_chunks),
        in_specs=[main, prev, nxt, pl.BlockSpec(bias.shape, lambda j, i: (0, 0, 0)),
                  pl.BlockSpec(sink.shape, lambda j, i: (0, 0, 0))],
        out_specs=[rows, rows],
        out_shape=[jax.ShapeDtypeStruct((s_tok, QW), F32)] * 2,
        name=name + "_fwd",
        compiler_params=_params("parallel", "parallel"),
    )(proj, proj, proj, bias, sink)


def _band_bwd_call(proj, o, do, lse, dlse, bias, sink, dproj, col, dil, group, kvw, scale, name):
    seq, tl, nb, n_chunks, period = _band_geometry(proj, dil)
    lead = 0 if dproj is None else 1
    n_blocks = seq // BAND_BLOCK
    heads = bias.shape[0]
    b_ = BAND_BLOCK
    have_dlse = dlse is not None

    def body(*refs):
        (w_ref, pw_ref, nw_ref, o_ref, do_ref, lse_ref), refs = refs[lead:lead + 6], refs[lead + 6:]
        if have_dlse:
            dlse_ref, refs = refs[0], refs[1:]
        bias_ref, sink_ref, dwin_ref, dbias_ref, dsink_ref, dq_s, dk_s, dv_s = refs
        j, i = pl.program_id(0), pl.program_id(1)

        @pl.when((j == 0) & (i == 0))
        def _():
            dbias_ref[...] = jnp.zeros_like(dbias_ref)
            dsink_ref[...] = jnp.zeros_like(dsink_ref)

        @pl.when(i == 0)
        def _():
            dk_s[...] = jnp.zeros_like(dk_s)
            dv_s[...] = jnp.zeros_like(dv_s)

        @pl.when(i < n_chunks)
        def _():
            w, pw, nw = w_ref[...].astype(F32), pw_ref[...].astype(F32), nw_ref[...].astype(F32)
            kb = _bands(w, pw, nw, QW, kvw, nb)
            vb = _bands(w, pw, nw, QW + kvw, kvw, nb)
            edge = _edge_mask(i * nb, nb, period)
            dq_acc = [jnp.zeros((tl, LANES), F32) for _ in range(heads // 2)]
            for h in range(heads):
                pair, a, kvp, b = _head_geometry(h, group)
                lanes = slice(pair * LANES, (pair + 1) * LANES)
                mine = _lane_half() == a
                q_al = _align(w[:, lanes], a, b).astype(BF16).reshape(nb, b_, LANES)
                do_al = _align(do_ref[:, lanes], a, b).astype(BF16).reshape(nb, b_, LANES)
                lse_h = jnp.max(jnp.where(mine, lse_ref[:, lanes], NEG_INF), axis=1, keepdims=True)
                shift = -jnp.sum(jnp.where(mine, do_ref[:, lanes] * o_ref[:, lanes], 0.0), axis=1, keepdims=True)
                if have_dlse:
                    shift = shift + jnp.sum(jnp.where(mine, dlse_ref[:, lanes], 0.0), axis=1, keepdims=True)
                logits = _bmm(q_al, kb[kvp], 2, 2) * scale + bias_ref[h][None] + edge
                p = jnp.exp(logits - lse_h.reshape(nb, b_, 1))
                dlogits = p * (_bmm(do_al, vb[kvp], 2, 2) + shift.reshape(nb, b_, 1))
                dbias_ref[h] += jnp.sum(dlogits, axis=0)
                dsink_ref[h] += jnp.sum(jnp.exp(sink_ref[h] - lse_h) * shift, axis=0, keepdims=True)
                ds = (dlogits * scale).astype(BF16)
                dq_acc[pair] = dq_acc[pair] + _unalign(_bmm(ds, kb[kvp], 2, 1).reshape(tl, LANES), a, b)
                dk_band = _bmm(ds, q_al, 1, 1)
                dv_band = _bmm(p.astype(BF16), do_al, 1, 1)
                kv_lanes = slice(kvp * LANES, (kvp + 1) * LANES)
                for t in range(3):
                    at = pl.ds(pl.multiple_of(i * tl + t * b_, b_), tl)
                    dk_s[at, kv_lanes] += dk_band[:, t * b_:(t + 1) * b_, :].reshape(tl, LANES)
                    dv_s[at, kv_lanes] += dv_band[:, t * b_:(t + 1) * b_, :].reshape(tl, LANES)
            dq_s[lax.rem(i, 2)] = jnp.concatenate(dq_acc, axis=1)

        @pl.when(i >= 1)
        def _():
            at = pl.ds(pl.multiple_of((i - 1) * tl + b_, b_), tl)
            parts = [dq_s[lax.rem(i + 1, 2)], dk_s[at, :], dv_s[at, :]]
            if QW + 2 * kvw < BAND_W:
                parts.append(jnp.zeros((tl, BAND_W - QW - 2 * kvw), F32))
            dwin_ref[...] = jnp.concatenate(parts, axis=1).astype(dwin_ref.dtype)

    main, prev, nxt, rows = _band_in_specs(tl, nb, n_chunks, n_blocks, col, True)
    row_args = [o, do, lse] + ([dlse] if have_dlse else [])
    small = [pl.BlockSpec(bias.shape, lambda j, i: (0, 0, 0)), pl.BlockSpec(sink.shape, lambda j, i: (0, 0, 0))]
    return pl.pallas_call(
        body,
        grid=(1, n_chunks + 1),
        in_specs=[pl.BlockSpec(memory_space=pl.ANY)] * lead + [main, prev, nxt] + [rows] * len(row_args) + small,
        out_specs=[pl.BlockSpec((tl, BAND_W), lambda j, i: (j * n_chunks + jnp.maximum(i - 1, 0), col))] + small,
        out_shape=[jax.ShapeDtypeStruct(proj.shape, BF16), jax.ShapeDtypeStruct(bias.shape, F32),
                   jax.ShapeDtypeStruct(sink.shape, F32)],
        scratch_shapes=[pltpu.VMEM((2, tl, QW), F32), pltpu.VMEM((seq + 2 * b_, kvw), F32),
                        pltpu.VMEM((seq + 2 * b_, kvw), F32)],
        input_output_aliases={0: 0} if lead else {},
        name=name + "_bwd",
        compiler_params=_params("arbitrary", "arbitrary"),
    )(*([dproj] if lead else []), proj, proj, proj, *row_args, bias, sink)


def _loss_call(x, target, g):
    s, d = x.shape
    tr = _pick(s, (256, 128, 64, 32, 16, 8))

    def tile_loss(xt, gt, tt):
        err = jnp.square(_rms(xt, gt) - tt)
        return 0.5 * jnp.sum(jnp.mean(err, axis=-1, keepdims=True), axis=0, keepdims=True)

    def body(x_ref, t_ref, g_ref, loss_ref, dx_ref, dg_ref):
        tt = t_ref[...]
        val, vjp = jax.vjp(lambda xt, gt: tile_loss(xt, gt, tt), x_ref[...], g_ref[...])
        dx, dg = vjp(jnp.ones_like(val))
        dx_ref[...] = dx

        @pl.when(pl.program_id(0) == 0)
        def _():
            loss_ref[...] = jnp.zeros_like(loss_ref)
            dg_ref[...] = jnp.zeros_like(dg_ref)

        loss_ref[...] += val
        dg_ref[...] += dg

    return pl.pallas_call(
        body,
        grid=(s // tr,),
        in_specs=[_rows(tr, d), _rows(tr, d), _whole((1, d))],
        out_specs=[_whole((1, 1)), _rows(tr, d), _whole((1, d))],
        out_shape=[jax.ShapeDtypeStruct((1, 1), F32), jax.ShapeDtypeStruct((s, d), F32),
                   jax.ShapeDtypeStruct((1, d), F32)],
        name="final_norm_loss",
        compiler_params=_params("arbitrary"),
    )(x, target, g)


@jax.custom_vjp
def _loss_op(x, target, g):
    return _loss_call(x, target, g)[0][0, 0]


def _loss_op_fwd(x, target, g):
    loss, dx, dg = _loss_call(x, target, g)
    return loss[0, 0], (dx, dg, target)


def _loss_op_bwd(res, ct):
    dx, dg, target = res
    return ct * dx, jnp.zeros_like(target), ct * dg


_loss_op.defvjp(_loss_op_fwd, _loss_op_bwd)


def _mla_tile(r, p, a):
    g_q, g_kv, w_q, w_k, w_v = p
    cos_t, sin_t, place_kr = a
    a_q, a_kv, a_kr = _lanes(r[0], (0, MLA_Q_LORA, MLA_Q_LORA + MLA_KV_LORA, MLA_Q_LORA + MLA_KV_LORA + MLA_ROPE))
    q = _rope(_bdot(_rms(a_q, g_q), w_q), cos_t, sin_t, MLA_ROPE // 2)
    ckv = _rms(a_kv, g_kv)
    k = _rope(_bdot(ckv, w_k) + _hdot(a_kr, place_kr), cos_t, sin_t, MLA_ROPE // 2)
    return _split_heads(q, MLA_HEADS), _split_heads(k, MLA_HEADS), _split_heads(_bdot(ckv, w_v), MLA_HEADS)


def _head_rms(x, g, head_mean):
    return x * lax.rsqrt(_hdot(x * x, head_mean) + EPS) * g


def _gqa_tile(r, p, a):
    g_q, g_k = p
    cos_t, sin_t, mean_q, mean_k = a
    wq, wk = GQA_HEADS * HEAD_DIM, GQA_KV_HEADS * HEAD_DIM
    b_q, b_k, b_v = _lanes(r[0], (0, wq, wq + wk, wq + 2 * wk))
    q = _rope(_head_rms(b_q, g_q, mean_q), cos_t, sin_t, HEAD_DIM // 4)
    k = _rope(_head_rms(b_k, g_k, mean_k), cos_t[:, :wk], sin_t[:, :wk], HEAD_DIM // 4)
    return _split_heads(q, GQA_HEADS), _split_heads(k, GQA_KV_HEADS), _split_heads(b_v, GQA_KV_HEADS)


def _permute_rows(p, x, cp):
    pb = p.astype(BF16)
    hi = x.astype(BF16)
    rest = x - hi.astype(F32)
    mid = rest.astype(BF16)
    low = (rest - mid.astype(F32)).astype(BF16)
    dims = (((cp,), (0,)), ((), ()))
    return (lax.dot_general(pb, hi, dims, preferred_element_type=F32)
            + lax.dot_general(pb, mid, dims, preferred_element_type=F32)
            + lax.dot_general(pb, low, dims, preferred_element_type=F32))


@jax.custom_vjp
def _permuted(p, x):
    return _permute_rows(p, x, 1)


def _permuted_fwd(p, x):
    return _permute_rows(p, x, 1), p


def _permuted_bwd(p, ct):
    return jnp.zeros_like(p), _permute_rows(p, ct, 0)


_permuted.defvjp(_permuted_fwd, _permuted_bwd)


def _interleave(p, x):
    return _permuted(p, x.reshape(x.shape[0] * x.shape[1], x.shape[2]))


def _interleave_matrix(rows, dil):
    p = np.zeros((rows, rows), np.float32)
    for t in range(rows):
        p[t, (t % dil) * (rows // dil) + t // dil] = 1.0
    return p


def _merge_tile(r, p, a):
    gm, o_a, o_b, oc0, oc1, oc2, l0, l1, l2, o_d = r
    (w_branch,) = p
    perm1, perm2 = a
    oc1, l1, oc2, l2 = _interleave(perm1, oc1), _interleave(perm1, l1), _interleave(perm2, oc2), _interleave(perm2, l2)
    d = w_branch.shape[2]
    gate_path, merge_logits = _lanes(gm, (0, N_BRANCH * BRANCH_W, N_BRANCH * BRANCH_W + N_BRANCH * d))
    m = jnp.maximum(jnp.maximum(l0, l1), l2)
    e0, e1, e2 = jnp.exp(l0 - m), jnp.exp(l1 - m), jnp.exp(l2 - m)
    y_c = (e0 * oc0 + e1 * oc1 + e2 * oc2) / (e0 + e1 + e2)
    y = jnp.concatenate([_join_heads(o_a), _join_heads(o_b), y_c, o_d], axis=1)
    u = y * (gate_path * jax.nn.sigmoid(gate_path))
    gates = _lanes(merge_logits, tuple(range(0, N_BRANCH * d + 1, d)))
    us = _lanes(u, tuple(range(0, N_BRANCH * BRANCH_W + 1, BRANCH_W)))
    branch_w = _unstack(w_branch)
    out = None
    for nb in range(N_BRANCH):
        term = jax.nn.sigmoid(gates[nb]) * _bdot(us[nb], branch_w[nb])
        out = term if out is None else out + term
    return (out,)


def _mixer_calls(proj, prm, aux):
    s = proj.shape[0]
    tr, tm = _pick(s, (512, 256, 128)), _pick(s, (256,))
    mla_cos, mla_sin, gqa_cos, gqa_sin, place_kr, mean_q, mean_k = aux[:7]
    wq = MLA_HEADS * MLA_QK
    mla = dict(
        steps=s // tr, rows=[(proj, _rows(tr, SMALL_W, MLA_BLK))],
        params=[prm["g_q"], prm["g_kv"], prm["w_q"], prm["w_k"], prm["w_v"]],
        aux=[(mla_cos, _rows(tr, wq)), (mla_sin, _rows(tr, wq)), (place_kr, _whole(place_kr.shape))],
        outs=[((MLA_HEADS, s, MLA_QK), _head_rows(MLA_HEADS, tr, MLA_QK))] * 2
        + [((MLA_HEADS, s, MLA_V), _head_rows(MLA_HEADS, tr, MLA_V))],
        window=((s, P_TOT), _rows(tr, SMALL_W, MLA_BLK), BF16))
    wg = GQA_HEADS * HEAD_DIM
    gqa = dict(
        steps=s // tr, rows=[(proj, _rows(tr, SMALL_W, GQA_BLK))], params=[prm["gq"], prm["gk"]],
        aux=[(gqa_cos, _rows(tr, wg)), (gqa_sin, _rows(tr, wg)), (mean_q, _whole(mean_q.shape)),
             (mean_k, _whole(mean_k.shape))],
        outs=[((GQA_HEADS, s, HEAD_DIM), _head_rows(GQA_HEADS, tr, HEAD_DIM))]
        + [((GQA_KV_HEADS, s, HEAD_DIM), _head_rows(GQA_KV_HEADS, tr, HEAD_DIM))] * 2,
        window=((s, P_TOT), _rows(tr, SMALL_W, GQA_BLK), BF16))
    merge = dict(steps=s // tm, tm=tm, window=((s, P_TOT), _rows(tm, GM_W, 0), BF16))
    return mla, gqa, merge


def _merge_rows(proj, o_a, o_b, ocs, lses, o_d, tm):
    h4 = _head_rows(4, tm, HEAD_DIM)
    s = proj.shape[0]

    def by_residue(t, dil):
        if dil == 1:
            return t, _rows(tm, QW)
        return t.reshape(dil, s // dil, QW), pl.BlockSpec((dil, tm // dil, QW), lambda i: (0, i, 0))

    dils = [dil for _, dil in DIL_PATTERNS]
    return ([(proj, _rows(tm, GM_W, 0)), (o_a, h4), (o_b, h4)] + [by_residue(t, r) for t, r in zip(ocs, dils)]
            + [by_residue(t, r) for t, r in zip(lses, dils)] + [(o_d, _rows(tm, QW))])


def _merge_aux(aux):
    return [(t, _whole(t.shape)) for t in aux[7:9]]


def _to_residues(t, dil):
    s, w = t.shape
    return t if dil == 1 else t.reshape(s // dil, dil, w).transpose(1, 0, 2).reshape(s, w)


def _from_residues(t, dil):
    s, w = t.shape
    return t if dil == 1 else t.reshape(dil, s // dil, w).transpose(1, 0, 2).reshape(s, w)


def _mixer_fwd(projs, prm, aux):
    proj = projs[0]
    s = proj.shape[0]
    mla, gqa, merge = _mixer_calls(proj, prm, aux)
    q_a, k_a, v_a = _fwd_call("prep_mla", _mla_tile, mla["steps"], mla["rows"], mla["params"], mla["aux"], mla["outs"])
    o_a, lse_a = _dense_fwd_call(q_a, k_a, v_a, MLA_QK ** -0.5, "mla")
    q_b, k_b, v_b = _fwd_call("prep_gqa", _gqa_tile, gqa["steps"], gqa["rows"], gqa["params"], gqa["aux"], gqa["outs"])
    grp = GQA_HEADS // GQA_KV_HEADS
    o_b, lse_b = _dense_fwd_call(q_b.reshape(GQA_KV_HEADS, grp * s, HEAD_DIM), k_b, v_b, HEAD_DIM ** -0.5, "gqa")
    scale = HEAD_DIM ** -0.5
    ocs, lses = [], []
    for gi, (_, dil) in enumerate(DIL_PATTERNS):
        o, lse = _band_fwd_call(projs[gi], DIL_BLK if gi == 0 else 0, prm["bias_dil"][gi], prm["no_sink"], dil, 1,
                                QW, scale, "dil%d" % gi)
        ocs.append(o)
        lses.append(lse)
    o_d, lse_d = _band_fwd_call(proj, WIN_BLK, prm["bias_win"], prm["sink"], 1, WIN_HEADS // WIN_KV_HEADS,
                                WIN_KV_HEADS * HEAD_DIM, scale, "win")
    rows = _merge_rows(proj, o_a, o_b.reshape(GQA_HEADS, s, HEAD_DIM), ocs, lses, o_d, merge["tm"])
    mix = _fwd_call("merge", _merge_tile, merge["steps"], rows, [prm["w_branch"]], _merge_aux(aux),
                    [((s, prm["w_branch"].shape[2]), _rows(merge["tm"], prm["w_branch"].shape[2]), BF16)])[0]
    return mix, (q_a, k_a, v_a, o_a, lse_a, q_b, k_b, v_b, o_b, lse_b, ocs, lses, o_d, lse_d)


def _mixer_bwd(projs, prm, aux, saved, dmix):
    proj = projs[0]
    s = proj.shape[0]
    q_a, k_a, v_a, o_a, lse_a, q_b, k_b, v_b, o_b, lse_b, ocs, lses, o_d, lse_d = saved
    dils = [dil for _, dil in DIL_PATTERNS]
    mla, gqa, merge = _mixer_calls(proj, prm, aux)
    tm, d_model = merge["tm"], prm["w_branch"].shape[2]
    grp = GQA_HEADS // GQA_KV_HEADS
    scale = HEAD_DIM ** -0.5

    rows = _merge_rows(proj, o_a, o_b.reshape(GQA_HEADS, s, HEAD_DIM), ocs, lses, o_d, tm)
    grads, (dw_branch,) = _vjp_call(
        "merge", _merge_tile, merge["steps"], rows, [prm["w_branch"]], _merge_aux(aux), [(dmix, _rows(tm, d_model))],
        [merge["window"]] + [(a.shape, spec) for a, spec in rows[1:]])
    dproj, do_a, do_b, docs, dlses, do_d = grads[0], grads[1], grads[2], grads[3:6], grads[6:9], grads[9]

    dq_a, dk_a, dv_a = _dense_bwd_call(q_a, k_a, v_a, o_a, lse_a, do_a, MLA_QK ** -0.5, "mla")
    (dproj,), dmla = _vjp_call("prep_mla", _mla_tile, mla["steps"], mla["rows"], mla["params"], mla["aux"],
                               [(t, spec) for t, (_, spec) in zip((dq_a, dk_a, dv_a), mla["outs"])],
                               [mla["window"]], into=dproj)
    dq_b, dk_b, dv_b = _dense_bwd_call(q_b.reshape(GQA_KV_HEADS, grp * s, HEAD_DIM), k_b, v_b, o_b, lse_b,
                                       do_b.reshape(GQA_KV_HEADS, grp * s, HEAD_DIM), scale, "gqa")
    (dproj,), dgqa = _vjp_call("prep_gqa", _gqa_tile, gqa["steps"], gqa["rows"], gqa["params"], gqa["aux"],
                               [(t, spec) for t, (_, spec) in zip((dq_b.reshape(GQA_HEADS, s, HEAD_DIM), dk_b, dv_b),
                                                                  gqa["outs"])],
                               [gqa["window"]], into=dproj)
    dproj, dbias_win, dsink = _band_bwd_call(proj, o_d, do_d, lse_d, None, prm["bias_win"], prm["sink"], dproj,
                                             WIN_BLK, 1, WIN_HEADS // WIN_KV_HEADS, WIN_KV_HEADS * HEAD_DIM, scale, "win")
    dbias_dil, dprojs = [], []
    for gi, dil in enumerate(dils):
        dside, dbias, _ = _band_bwd_call(
            projs[gi], ocs[gi], docs[gi].reshape(s, QW), lses[gi], dlses[gi].reshape(s, QW),
            prm["bias_dil"][gi], prm["no_sink"], dproj if gi == 0 else None, DIL_BLK if gi == 0 else 0, dil, 1, QW,
            scale, "dil%d" % gi)
        if gi == 0:
            dproj = dside
        else:
            dprojs.append(dside)
        dbias_dil.append(dbias)
    dprm = dict(g_q=dmla[0], g_kv=dmla[1], w_q=dmla[2], w_k=dmla[3], w_v=dmla[4], gq=dgqa[0], gk=dgqa[1],
                bias_dil=dbias_dil, bias_win=dbias_win, sink=dsink, no_sink=jnp.zeros_like(prm["no_sink"]),
                w_branch=dw_branch)
    return [dproj] + dprojs, {k: jax.tree.map(lambda g, p: g.astype(p.dtype), v, prm[k]) for k, v in dprm.items()}


def _layer_fwd(x, w, aux):
    s, d = x.shape
    tr = _pick(s, (256,))
    dils = [dil for _, dil in DIL_PATTERNS]

    def norm_forms(r, p, a):
        y = _rms(r[0], p[0])
        return [y, y.T] + [_dot(q, y, 1, 0).reshape(dil, tr // dil, d) for q, dil in zip(a, dils[1:])]

    forms = _fwd_call(
        "norm", norm_forms, s // tr, [(x, _rows(tr, d))], [w["norm_g"]], [(q, _whole(q.shape)) for q in aux[9:11]],
        [((s, d), _rows(tr, d), BF16), ((d, s), pl.BlockSpec((d, tr), lambda i: (0, i)), BF16)]
        + [((dil, s // dil, d), pl.BlockSpec((dil, tr // dil, d), lambda i: (0, i, 0)), BF16) for dil in dils[1:]])
    xn_t, xns = forms[1], [forms[0]] + [t.reshape(s, d) for t in forms[2:]]
    projs = [_mm(a, b, "nt", "proj%d_fwd" % i, BF16) for i, (a, b) in enumerate(zip(xns, w["w_in_t"]))]
    mix, saved = _mixer_fwd(projs, w["mixer"], aux)
    return _mm(mix, w["w_out"], "nn", "out_proj_nn"), (x, w, aux, xns, xn_t, projs, mix, saved)


@jax.custom_vjp
def _layer_core(x, w, aux):
    return _layer_fwd(x, w, aux)[0]


def _layer_core_bwd(res, dout):
    x, w, aux, xns, xn_t, projs, mix, saved = res
    s, d = x.shape
    tr = _pick(s, (256, 128, 64, 32, 16, 8))
    dils = [dil for _, dil in DIL_PATTERNS]
    dmix = _mm(dout, w["w_out"], "nt", "out_proj_nt")
    dw_out = _mm(mix, dout, "tn", "out_proj_tn", w["w_out"].dtype)
    dprojs, dmixer = _mixer_bwd(projs, w["mixer"], aux, saved, dmix)
    dxn = None
    for i, (dp, wi, r) in enumerate(zip(dprojs, w["w_in_t"], dils)):
        dxn = _mm(_from_residues(dp, r), wi, "nn", "proj%d_dx" % i, plus=dxn)
    dw_in_t = [_mm(xn_t, dprojs[0], "nn", "proj0_dw", w["w_in_t"][0].dtype).T]
    dw_in_t += [_mm(dp, a, "tn", "proj%d_dw" % i, wi.dtype)
                for i, (a, dp, wi) in list(enumerate(zip(xns, dprojs, w["w_in_t"])))[1:]]
    (dx,), (dg,) = _vjp_call("norm", _norm_tile, s // tr, [(x, _rows(tr, d))], [w["norm_g"]], [],
                             [(dxn, _rows(tr, d))], [((s, d), _rows(tr, d))])
    dw = dict(norm_g=dg, w_in_t=dw_in_t, mixer=dmixer, w_out=dw_out)
    return dx, dw, tuple(jnp.zeros_like(t) for t in aux)


_layer_core.defvjp(lambda x, w, aux: _layer_fwd(x, w, aux), _layer_core_bwd)


def _rope_angles(pos, dim):
    inv = ROPE_THETA ** (-jnp.arange(0, dim, 2, dtype=F32) / dim)
    return pos.astype(F32)[:, None] * inv[None, :]


def _rope_tables(s):
    pos = jnp.arange(s, dtype=jnp.int32)
    rows = s // GRID_W
    row_idx = jnp.repeat(jnp.arange(rows, dtype=jnp.int32), GRID_W)
    col_idx = jnp.tile(jnp.arange(GRID_W, dtype=jnp.int32), rows)
    a1 = _rope_angles(pos, MLA_ROPE)
    ar = _rope_angles(row_idx, HEAD_DIM // 2)
    ac = _rope_angles(col_idx, HEAD_DIM // 2)
    ones, zeros = jnp.ones((s, MLA_NOPE), F32), jnp.zeros((s, MLA_NOPE), F32)
    mla_cos = jnp.tile(jnp.concatenate([ones, jnp.cos(a1), jnp.cos(a1)], axis=1), (1, MLA_HEADS))
    mla_sin = jnp.tile(jnp.concatenate([zeros, -jnp.sin(a1), jnp.sin(a1)], axis=1), (1, MLA_HEADS))
    gqa_cos = jnp.tile(jnp.concatenate([jnp.cos(ar), jnp.cos(ar), jnp.cos(ac), jnp.cos(ac)], axis=1), (1, GQA_HEADS))
    gqa_sin = jnp.tile(jnp.concatenate([-jnp.sin(ar), jnp.sin(ar), -jnp.sin(ac), jnp.sin(ac)], axis=1), (1, GQA_HEADS))
    return mla_cos, mla_sin, gqa_cos, gqa_sin


def _t5_bucket(rel):
    nb = T5_BUCKETS // 2
    max_exact = nb // 2
    n = jnp.abs(rel)
    nf = jnp.maximum(n, 1).astype(F32)
    large = max_exact + (jnp.log(nf / max_exact) / math.log(T5_MAX_DIST / max_exact) * (nb - max_exact)).astype(jnp.int32)
    large = jnp.minimum(large, nb - 1)
    return jnp.where(rel > 0, nb, 0) + jnp.where(n < max_exact, n, large)


def _band_bias(table, stride, head_lo, heads, half_window):
    b = BAND_BLOCK
    offs = jnp.arange(3 * b)[None, :] - b - jnp.arange(b)[:, None]
    one_hot = (_t5_bucket(offs * stride)[..., None] == jnp.arange(T5_BUCKETS)).astype(F32)
    bias = jnp.dot(one_hot.reshape(b * 3 * b, T5_BUCKETS), table[:, head_lo:head_lo + heads],
                   precision=lax.Precision.HIGHEST)
    bias = bias.T.reshape(heads, b, 3 * b)
    return jnp.where((jnp.abs(offs) <= half_window)[None], bias, NEG_INF)


def _w_in_rows(d):
    mla, gqa, win, dil0 = MLA_BLK * SMALL_W, GQA_BLK * SMALL_W, WIN_BLK * BAND_W, DIL_BLK * BAND_W
    plan, at = [], 0
    for width, target, row in ((256, 0, mla), (128, 0, mla + 256), (32, 0, mla + 384),
                               (256, 0, gqa), (128, 0, gqa + 256), (128, 0, gqa + 384)):
        plan.append((at, width, target, row))
        at += width
    for part in range(3):
        for g in range(len(DIL_PATTERNS)):
            plan.append((at, QW, g, (dil0 if g == 0 else 0) + part * QW))
            at += QW
    for width, row in ((256, win), (128, win + 256), (128, win + 384), (N_BRANCH * BRANCH_W, 0),
                       (N_BRANCH * d, N_BRANCH * BRANCH_W)):
        plan.append((at, width, 0, row))
        at += width
    return plan


@jax.custom_vjp
def _w_in_layout(w_in_t):
    d = w_in_t.shape[1]
    outs = []
    for target, rows in enumerate((P_TOT, BAND_W, BAND_W)):
        parts, at = [], 0
        for start, width, _, row in sorted((p for p in _w_in_rows(d) if p[2] == target), key=lambda p: p[3]):
            if row > at:
                parts.append(jnp.zeros((row - at, d), w_in_t.dtype))
            parts.append(w_in_t[start:start + width])
            at = row + width
        if at < rows:
            parts.append(jnp.zeros((rows - at, d), w_in_t.dtype))
        outs.append(jnp.concatenate(parts, axis=0))
    return outs


def _w_in_layout_fwd(w_in_t):
    return _w_in_layout(w_in_t), None


def _w_in_layout_bwd(_, cts):
    d = cts[0].shape[1]
    return (jnp.concatenate([cts[target][row:row + width] for _, width, target, row in _w_in_rows(d)], axis=0),)


_w_in_layout.defvjp(_w_in_layout_fwd, _w_in_layout_bwd)


def _layer(x, w, l, aux, biases):
    w_kv = w["w_kv_t"][l].T.reshape(MLA_KV_LORA, MLA_HEADS, MLA_NOPE + MLA_V)
    w_k = jnp.concatenate([w_kv[:, :, :MLA_NOPE], jnp.zeros((MLA_KV_LORA, MLA_HEADS, MLA_ROPE), w_kv.dtype)], axis=2)
    dil_bias, win_bias = biases
    prm = dict(
        g_q=w["mla_q_norm_g"][l][None, :], g_kv=w["mla_kv_norm_g"][l][None, :], w_q=w["w_q_t"][l].T,
        w_k=w_k.reshape(MLA_KV_LORA, MLA_HEADS * MLA_QK),
        w_v=w_kv[:, :, MLA_NOPE:].reshape(MLA_KV_LORA, MLA_HEADS * MLA_V),
        gq=jnp.tile(w["gqa_q_norm_g"][l], GQA_HEADS)[None, :], gk=jnp.tile(w["gqa_k_norm_g"][l], GQA_KV_HEADS)[None, :],
        bias_dil=list(dil_bias), bias_win=win_bias, sink=w["win_sink"][l].reshape(WIN_HEADS, 1, 1),
        no_sink=jnp.full((DIL_HEADS, 1, 1), NEG_INF, F32), w_branch=jnp.transpose(w["w_branch_t"][l].reshape(-1, N_BRANCH, BRANCH_W), (1, 2, 0)))
    layer_w = dict(norm_g=w["norm_g"][l][None, :], w_in_t=_w_in_layout(w["w_in_t"][l]), mixer=prm, w_out=w["w_out"][l])
    return x + _layer_core(x, layer_w, aux)


def _local_loss(w, x, target):
    s, d_model = x.shape
    assert d_model == D_MODEL, "the projection's window layout is laid out for d_model 1024"
    place = np.zeros((MLA_ROPE, MLA_HEADS * MLA_QK), np.float32)
    for h in range(MLA_HEADS):
        for i in range(MLA_ROPE):
            place[i, h * MLA_QK + MLA_NOPE + i] = 1.0

    def head_mean(nh):
        m = np.kron(np.eye(nh, dtype=np.float32), np.full((HEAD_DIM, HEAD_DIM), 1.0 / HEAD_DIM, np.float32))
        return jnp.asarray(m)

    merge_tile = _pick(s, (256,))
    norm_tile = _pick(s, (256,))
    aux = _rope_tables(s) + (jnp.asarray(place), head_mean(GQA_HEADS), head_mean(GQA_KV_HEADS)) + tuple(
        jnp.asarray(_interleave_matrix(merge_tile, dil)) for _, dil in DIL_PATTERNS[1:]) + tuple(
        jnp.asarray(_interleave_matrix(norm_tile, dil).T) for _, dil in DIL_PATTERNS[1:])
    table = w["t5_table"]
    dil_bias = [_band_bias(table, dil, gi * DIL_HEADS, DIL_HEADS, window // (2 * dil))
                for gi, (window, dil) in enumerate(DIL_PATTERNS)]
    win_bias = _band_bias(table, 1, len(DIL_PATTERNS) * DIL_HEADS, WIN_HEADS, WIN_HALF)
    for l in range(w["norm_g"].shape[0]):
        x = _layer(x, w, l, aux, (dil_bias, win_bias))
    return _loss_op(x, target, w["final_norm_g"][None, :])


_ANY = pl.BlockSpec(memory_space=pl.ANY)
_MESH = pl.DeviceIdType.MESH


def _all_gather(block, name):
    def body(x_ref, out_ref, send_sems, recv_sems, local_sem):
        x, y, c = lax.axis_index("x"), lax.axis_index("y"), lax.axis_index("c")
        me, sibling = (x, y, c), (x, y, 1 - c)
        chips = [(1 - x, y), (x, 1 - y), (1 - x, 1 - y)]

        def slot(px, py, pc):
            return out_ref.at[4 * px + 2 * py + pc]

        def copy(k, blk, to, src=None):
            return pltpu.make_async_remote_copy(
                src_ref=slot(*blk) if src is None else src, dst_ref=slot(*blk),
                send_sem=send_sems.at[k], recv_sem=recv_sems.at[k], device_id=to, device_id_type=_MESH)

        mine = pltpu.make_async_copy(x_ref, slot(*me), local_sem)
        mine.start()
        first = [copy(0, me, sibling, src=x_ref)]
        first += [copy(1 + j, me, (*chip, c), src=x_ref) for j, chip in enumerate(chips)]
        for cp in first:
            cp.start()
        passed = [copy(4 + j, (*chip, c), sibling) for j, chip in enumerate(chips)]
        for j, chip in enumerate(chips):
            copy(1 + j, (*chip, c), me).wait_recv()
            passed[j].start()
        copy(0, sibling, me).wait_recv()
        for j, chip in enumerate(chips):
            copy(4 + j, (*chip, 1 - c), me).wait_recv()
        for cp in first + passed:
            cp.wait_send()
        mine.wait()

    return pl.pallas_call(
        body,
        out_shape=jax.ShapeDtypeStruct((N_DEV,) + block.shape, block.dtype),
        in_specs=[_ANY],
        out_specs=_ANY,
        scratch_shapes=[pltpu.SemaphoreType.DMA((7,)), pltpu.SemaphoreType.DMA((7,)), pltpu.SemaphoreType.DMA],
        name=name,
    )(block)


def _swap_with_sibling(blocks, name):
    chips = blocks.shape[0]

    def body(x_ref, out_ref, send_sems, recv_sems):
        x, y, c = lax.axis_index("x"), lax.axis_index("y"), lax.axis_index("c")
        copies = [pltpu.make_async_remote_copy(
            src_ref=x_ref.at[k, 1 - c], dst_ref=out_ref.at[k], send_sem=send_sems.at[k], recv_sem=recv_sems.at[k],
            device_id=(x, y, 1 - c), device_id_type=_MESH) for k in range(chips)]
        for cp in copies:
            cp.start()
        for cp in copies:
            cp.wait()

    return pl.pallas_call(
        body,
        out_shape=jax.ShapeDtypeStruct((chips,) + blocks.shape[2:], blocks.dtype),
        in_specs=[_ANY],
        out_specs=_ANY,
        scratch_shapes=[pltpu.SemaphoreType.DMA((chips,)), pltpu.SemaphoreType.DMA((chips,))],
        name=name,
    )(blocks)


def _add_sibling(blocks, theirs, name):
    chips, _, rows, w = blocks.shape
    tr = _row_tile(rows, 16, 4096)

    def body(b_ref, t_ref, o_ref):
        mine = b_ref[0, lax.axis_index("c")]
        o_ref[0] = (mine.astype(F32) + t_ref[0].astype(F32)).astype(o_ref.dtype)

    return pl.pallas_call(
        body,
        grid=(chips, rows // tr),
        in_specs=[pl.BlockSpec((1, 2, tr, w), lambda k, i: (k, 0, i, 0)), pl.BlockSpec((1, tr, w), lambda k, i: (k, i, 0))],
        out_specs=pl.BlockSpec((1, tr, w), lambda k, i: (k, i, 0)),
        out_shape=jax.ShapeDtypeStruct(theirs.shape, theirs.dtype),
        name=name,
        compiler_params=_params("parallel", "parallel"),
    )(blocks, theirs)


def _exchange_chips(partials, name):
    n_chips = partials.shape[0]

    def body(x_ref, out_ref, send_sems, recv_sems, local_sem):
        x, y, c = lax.axis_index("x"), lax.axis_index("y"), lax.axis_index("c")
        me = 2 * x + y
        mine = pltpu.make_async_copy(x_ref.at[me], out_ref.at[me], local_sem)
        mine.start()
        copies, landed = [], []
        for k in range(1, n_chips):
            px = 1 - x if k & 2 else x
            py = 1 - y if k & 1 else y
            peer = 2 * px + py
            copies.append(pltpu.make_async_remote_copy(
                src_ref=x_ref.at[peer], dst_ref=out_ref.at[me], send_sem=send_sems.at[k - 1],
                recv_sem=recv_sems.at[k - 1], device_id=(px, py, c), device_id_type=_MESH))
            landed.append(pltpu.make_async_remote_copy(
                src_ref=x_ref.at[peer], dst_ref=out_ref.at[peer], send_sem=send_sems.at[k - 1],
                recv_sem=recv_sems.at[k - 1], device_id=(px, py, c), device_id_type=_MESH))
        for cp in copies:
            cp.start()
        for cp in landed:
            cp.wait_recv()
        for cp in copies:
            cp.wait_send()
        mine.wait()

    return pl.pallas_call(
        body,
        out_shape=jax.ShapeDtypeStruct(partials.shape, partials.dtype),
        in_specs=[_ANY],
        out_specs=_ANY,
        scratch_shapes=[pltpu.SemaphoreType.DMA((n_chips - 1,)), pltpu.SemaphoreType.DMA((n_chips - 1,)),
                        pltpu.SemaphoreType.DMA],
        name=name,
    )(partials)


def _sum_slots(parts, name):
    slots, rows, w = parts.shape
    tr = _row_tile(rows, 16 if parts.dtype == BF16 else 8, 4096)

    def body(p_ref, o_ref):
        acc = p_ref[0].astype(F32)
        for j in range(1, slots):
            acc = acc + p_ref[j].astype(F32)
        o_ref[...] = acc

    return pl.pallas_call(
        body,
        grid=(rows // tr,),
        in_specs=[pl.BlockSpec((slots, tr, w), lambda i: (0, i, 0))],
        out_specs=pl.BlockSpec((tr, w), lambda i: (i, 0)),
        out_shape=jax.ShapeDtypeStruct((rows, w), F32),
        name=name,
        compiler_params=_params("parallel"),
    )(parts)


def _adamw(w, g, m, v, name):
    rows, width = w.shape
    tr = _row_tile(rows, 8, 2048)

    def body(w_ref, g_ref, m_ref, v_ref, d_ref, nm_ref, nv_ref):
        g_ = g_ref[...]
        m_ = ADAM_B1 * m_ref[...] + (1.0 - ADAM_B1) * g_
        v_ = ADAM_B2 * v_ref[...] + (1.0 - ADAM_B2) * jnp.square(g_)
        m_hat = m_ / (1.0 - ADAM_B1 ** ADAM_STEP)
        v_hat = v_ / (1.0 - ADAM_B2 ** ADAM_STEP)
        d_ref[...] = -ADAM_LR * (m_hat / (jnp.sqrt(v_hat) + ADAM_EPS) + ADAM_WD * w_ref[...])
        nm_ref[...] = m_
        nv_ref[...] = v_

    spec = pl.BlockSpec((tr, width), lambda i: (i, 0))
    return pl.pallas_call(
        body,
        grid=(rows // tr,),
        in_specs=[spec] * 4,
        out_specs=[spec] * 3,
        out_shape=[jax.ShapeDtypeStruct((rows, width), F32)] * 3,
        name=name,
        compiler_params=_params("parallel"),
    )(w, g, m, v)


_SHARDED = (("w_in", 2), ("w_mla_q_up", 2), ("w_mla_kv_up", 2), ("w_branch", 3), ("w_out", 1))
_REPLICATED = ("norm_g", "mla_q_norm_g", "mla_kv_norm_g", "gqa_q_norm_g", "gqa_k_norm_g", "win_sink", "t5_table",
               "final_norm_g")


def _pack(arrays, row_multiple):
    flat = jnp.concatenate([a.reshape(-1) for a in arrays])
    rows = -(-flat.shape[0] // (LANES * row_multiple)) * row_multiple
    return jnp.pad(flat, (0, rows * LANES - flat.shape[0])).reshape(rows, LANES)


def _unpack(packed, shapes):
    flat, out, at = packed.reshape(-1), [], 0
    for shp in shapes:
        n = int(np.prod(shp))
        out.append(flat[at:at + n].reshape(shp))
        at += n
    return out


_TO_WIRE = {
    "w_in": lambda t: jnp.swapaxes(t, 1, 2), "w_mla_q_up": lambda t: jnp.swapaxes(t, 1, 2),
    "w_mla_kv_up": lambda t: jnp.swapaxes(t, 1, 2),
    "w_branch": lambda t: jnp.transpose(t, (0, 3, 1, 2)).reshape(t.shape[0], t.shape[3], -1), "w_out": lambda t: t}
_FROM_WIRE = {
    "w_in": lambda t, shp: jnp.swapaxes(t, 1, 2), "w_mla_q_up": lambda t, shp: jnp.swapaxes(t, 1, 2),
    "w_mla_kv_up": lambda t, shp: jnp.swapaxes(t, 1, 2),
    "w_branch": lambda t, shp: jnp.transpose(t.reshape(shp[0], shp[3], shp[1], shp[2]), (0, 2, 3, 1)),
    "w_out": lambda t, shp: t}
_WIRE_NAME = {"w_in": "w_in_t", "w_mla_q_up": "w_q_t", "w_mla_kv_up": "w_kv_t", "w_branch": "w_branch_t",
              "w_out": "w_out"}


def _transpose_blocks(t, dtype, name):
    depth, a, b = t.shape

    def body(x_ref, o_ref):
        o_ref[0] = x_ref[0].T.astype(o_ref.dtype)

    return pl.pallas_call(
        body,
        grid=(depth,),
        in_specs=[pl.BlockSpec((1, a, b), lambda i: (i, 0, 0))],
        out_specs=pl.BlockSpec((1, b, a), lambda i: (i, 0, 0)),
        out_shape=jax.ShapeDtypeStruct((depth, b, a), dtype),
        name=name,
        compiler_params=_params("parallel"),
    )(t)


def _join_shards(gathered, wire_shapes):
    out, at = [], 0
    for depth, cut, rest in wire_shapes:
        n = depth * cut * rest // LANES
        blk = gathered[:, at:at + n].reshape(N_DEV, depth, cut, rest)
        out.append(jnp.moveaxis(blk, 0, 1).reshape(depth, N_DEV * cut, rest))
        at += n
    return out


def _split_shards(fulls, wire_shapes):
    parts = []
    for full, (depth, cut, rest) in zip(fulls, wire_shapes):
        blk = jnp.moveaxis(full.reshape(depth, N_DEV, cut, rest), 1, 0)
        parts.append(blk.reshape(N_DEV, depth * cut * rest // LANES, LANES))
    packed = jnp.concatenate(parts, axis=1)
    return packed.reshape((N_DEV // 2, 2) + packed.shape[1:])


def kernel(x, norm_g, w_in, mla_q_norm_g, mla_kv_norm_g, w_mla_q_up, w_mla_kv_up, gqa_q_norm_g, gqa_k_norm_g, win_sink, t5_table, w_branch, w_out, final_norm_g, loss_target, m_norm_g, m_w_in, m_mla_q_norm_g, m_mla_kv_norm_g, m_w_mla_q_up, m_w_mla_kv_up, m_gqa_q_norm_g, m_gqa_k_norm_g, m_win_sink, m_t5_table, m_w_branch, m_w_out, m_final_norm_g, v_norm_g, v_w_in, v_mla_q_norm_g, v_mla_kv_norm_g, v_w_mla_q_up, v_w_mla_kv_up, v_gqa_q_norm_g, v_gqa_k_norm_g, v_win_sink, v_t5_table, v_w_branch, v_w_out, v_final_norm_g):
    given = dict(locals())
    names = ("norm_g", "w_in", "mla_q_norm_g", "mla_kv_norm_g", "w_mla_q_up", "w_mla_kv_up", "gqa_q_norm_g",
             "gqa_k_norm_g", "win_sink", "t5_table", "w_branch", "w_out", "final_norm_g")
    shard_names = [n for n, _ in _SHARDED]
    shard_shapes = [given[n].shape for n in shard_names]

    wire = [_transpose_blocks(given[n], BF16, "w_in_to_wire") if n == "w_in" else _TO_WIRE[n](given[n]).astype(BF16)
            for n in shard_names]
    wire_shapes = [t.shape for t in wire]
    gathered = _all_gather(jnp.concatenate([t.reshape(-1, LANES) for t in wire]), "gather_weights")
    weights = {n: given[n] for n in _REPLICATED}
    weights.update(zip([_WIRE_NAME[n] for n in shard_names], _join_shards(gathered, wire_shapes)))

    loss, (gw, gx) = jax.value_and_grad(_local_loss, argnums=(0, 1))(weights, x[0], loss_target[0])
    loss = lax.psum(loss, ("x", "y", "c"))

    send = _split_shards([gw[_WIRE_NAME[n]] for n in shard_names], wire_shapes)
    partials = _add_sibling(send, _swap_with_sibling(send, "swap_grads"), "add_sibling_grads")
    g_wire = _unpack(_sum_slots(_exchange_chips(partials, "scatter_grads"), "sum_grads"), wire_shapes)
    g_shard = [_transpose_blocks(t, F32, "w_in_from_wire") if n == "w_in" else _FROM_WIRE[n](t, shp)
               for n, t, shp in zip(shard_names, g_wire, shard_shapes)]
    rep_shapes = [given[n].shape for n in _REPLICATED]
    g_rep = _unpack(_sum_slots(_all_gather(_pack([gw[n] for n in _REPLICATED], 8), "gather_small_grads"),
                               "sum_small_grads"), rep_shapes)
    grads = dict(zip(shard_names, g_shard))
    grads.update(zip(_REPLICATED, g_rep))

    def update(group, shapes, row_multiple, name):
        outs = _adamw(*[_pack([src[n] for n in group], row_multiple) for src in (
            given, grads, {n: given["m_" + n] for n in group}, {n: given["v_" + n] for n in group})], name)
        return [dict(zip(group, _unpack(o, shapes))) for o in outs]

    big = update(shard_names, shard_shapes, 16, "adamw_shards")
    small = update(list(_REPLICATED), rep_shapes, 8, "adamw_replicated")
    delta, new_m, new_v = [{**b, **s_} for b, s_ in zip(big, small)]
    return (loss, gx[None], *[grads[n] for n in names], *[delta[n] for n in names],
            *[new_m[n] for n in names], *[new_v[n] for n in names])
```

```python
import functools
import math

import jax
import jax.numpy as jnp
import numpy as np
from jax import lax
from jax.experimental import pallas as pl
from jax.experimental.pallas import tpu as pltpu

F32 = jnp.float32
BF16 = jnp.bfloat16
N_DEV = 8
LANES = 128
HALF = LANES // 2
V7X_VMEM_LIMIT = 56 * 1024 * 1024

EPS = 1e-6
NEG_INF = -1e30
LOG2E = 1.4426950408889634
ROPE_THETA = 10000.0
GRID_W = 64
HEAD_DIM = 64
N_BRANCH = 4
BRANCH_W = 256
MLA_HEADS, MLA_Q_LORA, MLA_KV_LORA, MLA_NOPE, MLA_ROPE, MLA_V = 4, 256, 128, 64, 32, 64
MLA_QK = MLA_NOPE + MLA_ROPE
GQA_HEADS, GQA_KV_HEADS = 4, 2
DIL_PATTERNS = ((128, 1), (512, 4), (2048, 16))
DIL_HEADS = 4
WIN_HEADS, WIN_KV_HEADS, WIN_HALF = 4, 2, 128
T5_BUCKETS, T5_MAX_DIST = 32, 1024
BAND_BLOCK = 128
ADAM_LR, ADAM_B1, ADAM_B2, ADAM_EPS, ADAM_WD, ADAM_STEP = 0.001, 0.9, 0.999, 1e-08, 0.01, 10

D_MODEL = 1024
GM_W, SMALL_W, BAND_W = 5120, 512, 768
MLA_BLK, GQA_BLK, WIN_BLK, DIL_BLK = 10, 11, 8, 9
P_TOT = 7680
QW = 256


def _params(*sem):
    return pltpu.CompilerParams(dimension_semantics=sem, vmem_limit_bytes=V7X_VMEM_LIMIT)


def _pick(n, cands):
    for c in cands:
        if n % c == 0:
            return c
    return n


def _row_tile(rows, unit, cap):
    best = unit
    for t in range(unit, min(rows, cap) + 1, unit):
        if rows % t == 0:
            best = t
    assert rows % best == 0
    return best


def _dot(a, b, ca, cb):
    return lax.dot_general(a.astype(BF16), b.astype(BF16), (((ca,), (cb,)), ((), ())), preferred_element_type=F32)


def _bmm(a, b, ca, cb):
    return lax.dot_general(a, b, (((ca,), (cb,)), ((0,), (0,))), preferred_element_type=F32)


@jax.custom_vjp
def _bdot(a, b):
    return _dot(a, b, 1, 0)


def _bdot_fwd(a, b):
    return _dot(a, b, 1, 0), (a, b)


def _bdot_bwd(res, g):
    a, b = res
    return _dot(g, b, 1, 1), _dot(a, g, 0, 0)


_bdot.defvjp(_bdot_fwd, _bdot_bwd)


def _hdot(a, c):
    return lax.dot_general(a, c, (((1,), (0,)), ((), ())), precision=lax.Precision.HIGHEST, preferred_element_type=F32)


@functools.partial(jax.custom_vjp, nondiff_argnums=(1,))
def _lane_roll(x, shift):
    return pltpu.roll(x, shift, 1)


def _lane_roll_fwd(x, shift):
    return pltpu.roll(x, shift, 1), None


def _lane_roll_bwd(shift, _, g):
    return (pltpu.roll(g, g.shape[1] - shift, 1),)


_lane_roll.defvjp(_lane_roll_fwd, _lane_roll_bwd)


@functools.partial(jax.custom_vjp, nondiff_argnums=(1,))
def _lane_ranges(x, cut):
    bounds, _ = cut
    return tuple(x[:, lo:hi] for lo, hi in zip(bounds[:-1], bounds[1:]))


def _lane_ranges_fwd(x, cut):
    return _lane_ranges(x, cut), None


def _lane_ranges_bwd(cut, _, cts):
    bounds, width = cut
    parts = list(cts)
    if bounds[-1] < width:
        parts.append(jnp.zeros((cts[0].shape[0], width - bounds[-1]), cts[0].dtype))
    return (jnp.concatenate(parts, axis=1),)


_lane_ranges.defvjp(_lane_ranges_fwd, _lane_ranges_bwd)


def _lanes(x, bounds):
    return _lane_ranges(x, (tuple(bounds), x.shape[1]))


@jax.custom_vjp
def _unstack(x):
    return tuple(x[i] for i in range(x.shape[0]))


def _unstack_fwd(x):
    return _unstack(x), None


def _unstack_bwd(_, cts):
    return (jnp.stack(cts, axis=0),)


_unstack.defvjp(_unstack_fwd, _unstack_bwd)


@functools.partial(jax.custom_vjp, nondiff_argnums=(1,))
def _split_heads(x, h):
    d = x.shape[1] // h
    return jnp.stack([x[:, i * d:(i + 1) * d] for i in range(h)], axis=0)


def _split_heads_fwd(x, h):
    return _split_heads(x, h), None


def _split_heads_bwd(h, _, ct):
    return (jnp.concatenate([ct[i] for i in range(h)], axis=1),)


_split_heads.defvjp(_split_heads_fwd, _split_heads_bwd)


def _join_heads(x):
    return jnp.concatenate(_unstack(x), axis=1)


def _rope(x, cos_t, sin_t, half):
    w = x.shape[1]
    lane = lax.broadcasted_iota(jnp.int32, (1, w), 1)
    first = (lane % (2 * half)) < half
    partner = jnp.where(first, _lane_roll(x, w - half), _lane_roll(x, half))
    return x * cos_t + partner * sin_t


def _rms(x, g):
    return x * lax.rsqrt(jnp.mean(x * x, axis=-1, keepdims=True) + EPS) * g


def _rows(tr, w, col=0):
    return pl.BlockSpec((tr, w), lambda i: (i, col))


def _head_rows(h, tr, d):
    return pl.BlockSpec((h, tr, d), lambda i: (0, i, 0))


def _whole(shape):
    nd = len(shape)
    return pl.BlockSpec(tuple(shape), lambda i: (0,) * nd)


def _fwd_call(name, fn, steps, rows, params, aux, outs):
    nr, npar, na = len(rows), len(params), len(aux)

    def body(*refs):
        vals = [x[...].astype(F32) for x in refs[:nr + npar + na]]
        res = fn(vals[:nr], vals[nr:nr + npar], vals[nr + npar:])
        for o_ref, o in zip(refs[nr + npar + na:], res):
            o_ref[...] = o.astype(o_ref.dtype)

    return pl.pallas_call(
        body,
        grid=(steps,),
        in_specs=[s for _, s in rows] + [_whole(p.shape) for p in params] + [s for _, s in aux],
        out_specs=[e[1] for e in outs],
        out_shape=[jax.ShapeDtypeStruct(e[0], e[2] if len(e) > 2 else F32) for e in outs],
        name=name + "_fwd",
        compiler_params=_params("parallel"),
    )(*[a for a, _ in rows], *params, *[a for a, _ in aux])


def _vjp_call(name, fn, steps, rows, params, aux, cts, row_grads, into=None):
    nr, npar, na, nc = len(rows), len(params), len(aux), len(cts)
    n_in = nr + npar + na + nc
    lead = 0 if into is None else 1

    def body(*refs):
        refs = refs[lead:]
        vals = [x[...].astype(F32) for x in refs[:n_in]]
        r, p, a, d = vals[:nr], vals[nr:nr + npar], vals[nr + npar:nr + npar + na], vals[nr + npar + na:]
        out_refs = refs[n_in:]
        _, vjp = jax.vjp(lambda r_, p_: tuple(fn(r_, p_, a)), r, p)
        dr, dp = vjp(tuple(d))
        for o_ref, o in zip(out_refs[:nr], dr):
            o_ref[...] = o.astype(o_ref.dtype)

        @pl.when(pl.program_id(0) == 0)
        def _():
            for o_ref in out_refs[nr:]:
                o_ref[...] = jnp.zeros_like(o_ref)

        for o_ref, o in zip(out_refs[nr:], dp):
            o_ref[...] += o

    outs = pl.pallas_call(
        body,
        grid=(steps,),
        in_specs=([] if into is None else [pl.BlockSpec(memory_space=pl.ANY)])
        + [s for _, s in rows] + [_whole(p.shape) for p in params] + [s for _, s in aux] + [s for _, s in cts],
        out_specs=[e[1] for e in row_grads] + [_whole(p.shape) for p in params],
        out_shape=[jax.ShapeDtypeStruct(e[0], e[2] if len(e) > 2 else F32) for e in row_grads]
        + [jax.ShapeDtypeStruct(p.shape, F32) for p in params],
        input_output_aliases={} if into is None else {0: 0},
        name=name + "_bwd",
        compiler_params=_params("arbitrary"),
    )(*([] if into is None else [into]), *[a for a, _ in rows], *params, *[a for a, _ in aux], *[a for a, _ in cts])
    return list(outs[:nr]), list(outs[nr:])


def _norm_tile(r, p, a):
    return (_rms(r[0], p[0]),)


def _mm(a, b, mode, name, out_dtype=F32, plus=None):
    if mode == "nn":
        (m, k), n = a.shape, b.shape[1]
    elif mode == "nt":
        (m, k), n = a.shape, b.shape[0]
    else:
        (k, m), n = a.shape, b.shape[1]
    tn = _pick(n, (1024, 768, 512, 384, 256, 128))
    budget = V7X_VMEM_LIMIT * 3 // 4
    out_bytes = 4 + 2 * np.dtype(out_dtype).itemsize + (0 if plus is None else 2 * plus.dtype.itemsize)

    def tiles():
        for tm in (2048, 1024, 512, 256, 128):
            for tk in (512, 256, 128) if mode == "tn" else (4096, 1024, 768, 512, 384, 256, 128):
                need = 2 * tk * (tm * a.dtype.itemsize + tn * b.dtype.itemsize) + tm * tn * out_bytes
                if m % tm == 0 and k % tk == 0 and need <= budget:
                    return tm, tk
        return _pick(m, (128,)), _pick(k, (128,))

    tm, tk = tiles()
    nk = k // tk

    def body(*refs):
        a_ref, b_ref = refs[:2]
        o_ref, acc_ref = refs[-2:]
        kk = pl.program_id(2)
        if mode == "nn":
            part = _dot(a_ref[...], b_ref[...], 1, 0)
        elif mode == "nt":
            part = _dot(a_ref[...], b_ref[...], 1, 1)
        else:
            part = _dot(a_ref[...], b_ref[...], 0, 0)
        def first():
            return part if plus is None else part + refs[2][...].astype(F32)

        if nk == 1:
            o_ref[...] = first().astype(o_ref.dtype)
        else:
            @pl.when(kk == 0)
            def _():
                acc_ref[...] = first()

            @pl.when(kk > 0)
            def _():
                acc_ref[...] += part

            @pl.when(kk == nk - 1)
            def _():
                o_ref[...] = acc_ref[...].astype(o_ref.dtype)

    if mode == "nn":
        a_spec = pl.BlockSpec((tm, tk), lambda i, j, kk: (i, kk))
        b_spec = pl.BlockSpec((tk, tn), lambda i, j, kk: (kk, j))
    elif mode == "nt":
        a_spec = pl.BlockSpec((tm, tk), lambda i, j, kk: (i, kk))
        b_spec = pl.BlockSpec((tn, tk), lambda i, j, kk: (j, kk))
    else:
        a_spec = pl.BlockSpec((tk, tm), lambda i, j, kk: (kk, i))
        b_spec = pl.BlockSpec((tk, tn), lambda i, j, kk: (kk, j))
    o_spec = pl.BlockSpec((tm, tn), lambda i, j, kk: (i, j))
    return pl.pallas_call(
        body,
        grid=(m // tm, n // tn, nk),
        in_specs=[a_spec, b_spec] + ([] if plus is None else [o_spec]),
        out_specs=o_spec,
        out_shape=jax.ShapeDtypeStruct((m, n), out_dtype),
        scratch_shapes=[pltpu.VMEM((tm, tn), F32)],
        input_output_aliases={} if plus is None else {2: 0},
        name=name,
        compiler_params=_params("parallel", "parallel", "arbitrary"),
    )(a, b, *([] if plus is None else [plus]))


def _dense_fwd_call(q, k, v, scale, name):
    n, sq, d = q.shape
    sk, dv = k.shape[1], v.shape[2]
    tq = _pick(sq, (512, 256, 128))
    c = scale * LOG2E

    nkb = 1

    def body(q_ref, k_ref, v_ref, o_ref, lse_ref, m_s, acc_s, vext_s):
        j = pl.program_id(2)

        @pl.when(j == 0)
        def _():
            m_s[...] = jnp.full_like(m_s, NEG_INF)
            acc_s[...] = jnp.zeros_like(acc_s)
            vext_s[...] = jnp.ones_like(vext_s)

        vext_s[:, :dv] = v_ref[0].astype(BF16)
        m_old = m_s[...]
        s = _dot(q_ref[0], k_ref[0], 1, 1)
        m_new = jnp.maximum(m_old, jnp.max(s, axis=1, keepdims=True))
        p = jnp.exp2(s * c - m_new * c)
        acc = jnp.exp2((m_old - m_new) * c) * acc_s[...] + _dot(p, vext_s[...], 1, 0)
        m_s[...] = m_new
        acc_s[...] = acc

        @pl.when(j == nkb - 1)
        def _():
            l = acc[:, dv:dv + 1]
            o_ref[0] = acc[:, :dv] / l
            lse_ref[0] = m_new * scale + jnp.log(l)

    return pl.pallas_call(
        body,
        grid=(n, sq // tq, nkb),
        in_specs=[
            pl.BlockSpec((1, tq, d), lambda h, i, j: (h, i, 0)),
            pl.BlockSpec((1, sk // nkb, d), lambda h, i, j: (h, j, 0)),
            pl.BlockSpec((1, sk // nkb, dv), lambda h, i, j: (h, j, 0)),
        ],
        out_specs=[
            pl.BlockSpec((1, tq, dv), lambda h, i, j: (h, i, 0)),
            pl.BlockSpec((1, tq, 1), lambda h, i, j: (h, i, 0)),
        ],
        out_shape=[jax.ShapeDtypeStruct((n, sq, dv), F32), jax.ShapeDtypeStruct((n, sq, 1), F32)],
        scratch_shapes=[pltpu.VMEM((tq, 1), F32), pltpu.VMEM((tq, 2 * dv), F32), pltpu.VMEM((sk // nkb, 2 * dv), BF16)],
        name=name + "_fwd",
        compiler_params=_params("parallel", "parallel", "arbitrary"),
    )(q, k, v)


def _dense_bwd_call(q, k, v, o, lse, do, scale, name):
    n, sq, d = q.shape
    sk, dv = k.shape[1], v.shape[2]
    tq, tk = _pick(sq, (1024, 512, 256, 128)), _pick(sk, (2048, 1024, 512, 256, 128))
    c = scale * LOG2E

    def body(q_ref, k_ref, v_ref, o_ref, lse_ref, do_ref, dq_ref, dk_ref, dv_ref):
        j, i = pl.program_id(1), pl.program_id(2)
        qb, kb, vb = q_ref[0].astype(BF16), k_ref[0].astype(BF16), v_ref[0].astype(BF16)
        do_f = do_ref[0]
        dob = do_f.astype(BF16)
        p = jnp.exp2(_dot(qb, kb, 1, 1) * c - lse_ref[0] * LOG2E)
        delta = jnp.sum(do_f * o_ref[0], axis=1, keepdims=True)
        ds = (p * (_dot(dob, vb, 1, 1) - delta)).astype(BF16)
        dv_part = _dot(p, dob, 0, 0)
        dk_part = _dot(ds, qb, 0, 0) * scale
        dq_part = _dot(ds, kb, 1, 0) * scale
        rows = pl.ds(pl.multiple_of(i * tq, tq), tq)

        @pl.when(i == 0)
        def _():
            dk_ref[0] = dk_part
            dv_ref[0] = dv_part

        @pl.when(i > 0)
        def _():
            dk_ref[0] += dk_part
            dv_ref[0] += dv_part

        @pl.when(j == 0)
        def _():
            dq_ref[0, rows, :] = dq_part

        @pl.when(j > 0)
        def _():
            dq_ref[0, rows, :] += dq_part

    return pl.pallas_call(
        body,
        grid=(n, sk // tk, sq // tq),
        in_specs=[
            pl.BlockSpec((1, tq, d), lambda h, j, i: (h, i, 0)),
            pl.BlockSpec((1, tk, d), lambda h, j, i: (h, j, 0)),
            pl.BlockSpec((1, tk, dv), lambda h, j, i: (h, j, 0)),
            pl.BlockSpec((1, tq, dv), lambda h, j, i: (h, i, 0)),
            pl.BlockSpec((1, tq, 1), lambda h, j, i: (h, i, 0)),
            pl.BlockSpec((1, tq, dv), lambda h, j, i: (h, i, 0)),
        ],
        out_specs=[
            pl.BlockSpec((1, sq, d), lambda h, j, i: (h, 0, 0)),
            pl.BlockSpec((1, tk, d), lambda h, j, i: (h, j, 0)),
            pl.BlockSpec((1, tk, dv), lambda h, j, i: (h, j, 0)),
        ],
        out_shape=[
            jax.ShapeDtypeStruct((n, sq, d), F32),
            jax.ShapeDtypeStruct((n, sk, d), F32),
            jax.ShapeDtypeStruct((n, sk, dv), F32),
        ],
        name=name + "_bwd",
        compiler_params=_params("arbitrary", "arbitrary", "arbitrary"),
    )(q, k, v, o, lse, do)


def _head_geometry(h, group):
    pair, a = divmod(h, 2)
    kv_pair, b = divmod(h // group, 2)
    return pair, a, kv_pair, b


def _lane_half():
    return lax.broadcasted_iota(jnp.int32, (1, LANES), 1) // HALF


def _align(x, a, b):
    if a != b:
        x = pltpu.roll(x, HALF, 1)
    return jnp.where(_lane_half() == b, x, 0.0)


def _unalign(x, a, b):
    x = jnp.where(_lane_half() == b, x, 0.0)
    return pltpu.roll(x, HALF, 1) if a != b else x


def _bands(w, pw, nw, lo, kvw, nb):
    b = BAND_BLOCK
    cat = jnp.concatenate([pw[:, lo:lo + kvw], w[:, lo:lo + kvw], nw[:, lo:lo + kvw]], axis=0).astype(BF16)
    out = []
    for g in range(kvw // LANES):
        c3 = cat[:, g * LANES:(g + 1) * LANES].reshape(nb + 2, b, LANES)
        out.append(jnp.concatenate([c3[0:nb], c3[1:nb + 1], c3[2:nb + 2]], axis=1))
    return out


def _edge_mask(first_block, nb, period):
    b = BAND_BLOCK
    blk = (first_block + lax.broadcasted_iota(jnp.int32, (nb, 1, 3 * b), 0)) % period
    col = lax.broadcasted_iota(jnp.int32, (nb, 1, 3 * b), 2)
    outside = ((col < b) & (blk == 0)) | ((col >= 2 * b) & (blk == period - 1))
    return jnp.where(outside, NEG_INF, 0.0)


def _band_geometry(proj, dil):
    rows = proj.shape[0]
    tl = _pick(rows, (1024, 512, 256, 128))
    return rows, tl, tl // BAND_BLOCK, rows // tl, rows // dil // BAND_BLOCK


def _band_in_specs(tl, nb, n_chunks, n_blocks, col, last_step_idle):
    def chunk(i):
        return jnp.minimum(i, n_chunks - 1) if last_step_idle else i

    main = pl.BlockSpec((tl, BAND_W), lambda j, i: (j * n_chunks + chunk(i), col))
    prev = pl.BlockSpec((BAND_BLOCK, BAND_W),
                        lambda j, i: (j * n_blocks + jnp.maximum(chunk(i) * nb - 1, 0), col))
    nxt = pl.BlockSpec((BAND_BLOCK, BAND_W),
                       lambda j, i: (j * n_blocks + jnp.minimum((chunk(i) + 1) * nb, n_blocks - 1), col))
    rows = pl.BlockSpec((tl, QW), lambda j, i: (j * n_chunks + chunk(i), 0))
    return main, prev, nxt, rows


def _band_fwd_call(proj, col, bias, sink, dil, group, kvw, scale, name):
    s_tok = proj.shape[0]
    seq, tl, nb, n_chunks, period = _band_geometry(proj, dil)
    n_blocks = seq // BAND_BLOCK
    heads = bias.shape[0]

    def body(w_ref, pw_ref, nw_ref, bias_ref, sink_ref, o_ref, lse_ref):
        i = pl.program_id(1)
        w, pw, nw = w_ref[...].astype(F32), pw_ref[...].astype(F32), nw_ref[...].astype(F32)
        kb = _bands(w, pw, nw, QW, kvw, nb)
        vb = _bands(w, pw, nw, QW + kvw, kvw, nb)
        edge = _edge_mask(i * nb, nb, period)
        o_acc = [jnp.zeros((tl, LANES), F32) for _ in range(heads // 2)]
        lse_acc = [jnp.zeros((tl, LANES), F32) for _ in range(heads // 2)]
        for h in range(heads):
            pair, a, kvp, b = _head_geometry(h, group)
            q_al = _align(w[:, pair * LANES:(pair + 1) * LANES], a, b).astype(BF16).reshape(nb, BAND_BLOCK, LANES)
            logits = _bmm(q_al, kb[kvp], 2, 2) * scale + bias_ref[h][None] + edge
            sk = sink_ref[h].reshape(1, 1, 1)
            m = jnp.maximum(jnp.max(logits, axis=2, keepdims=True), sk)
            e = jnp.exp(logits - m)
            ssum = jnp.sum(e, axis=2, keepdims=True) + jnp.exp(sk - m)
            out = _bmm(e.astype(BF16), vb[kvp], 2, 1) / ssum
            o_acc[pair] = o_acc[pair] + _unalign(out.reshape(tl, LANES), a, b)
            lse = (m + jnp.log(ssum)).reshape(tl, 1)
            lse_acc[pair] = lse_acc[pair] + jnp.where(_lane_half() == a, lse, 0.0)
        o_ref[...] = jnp.concatenate(o_acc, axis=1)
        lse_ref[...] = jnp.concatenate(lse_acc, axis=1)

    main, prev, nxt, rows = _band_in_specs(tl, nb, n_chunks, n_blocks, col, False)
    return pl.pallas_call(
        body,
        grid=(1, n_chunks),
        in_specs=[main, prev, nxt, pl.BlockSpec(bias.shape, lambda j, i: (0, 0, 0)),
                  pl.BlockSpec(sink.shape, lambda j, i: (0, 0, 0))],
        out_specs=[rows, rows],
        out_shape=[jax.ShapeDtypeStruct((s_tok, QW), F32)] * 2,
        name=name + "_fwd",
        compiler_params=_params("parallel", "parallel"),
    )(proj, proj, proj, bias, sink)


def _band_bwd_call(proj, o, do, lse, dlse, bias, sink, dproj, col, dil, group, kvw, scale, name):
    seq, tl, nb, n_chunks, period = _band_geometry(proj, dil)
    lead = 0 if dproj is None else 1
    n_blocks = seq // BAND_BLOCK
    heads = bias.shape[0]
    b_ = BAND_BLOCK
    have_dlse = dlse is not None

    def body(*refs):
        (w_ref, pw_ref, nw_ref, o_ref, do_ref, lse_ref), refs = refs[lead:lead + 6], refs[lead + 6:]
        if have_dlse:
            dlse_ref, refs = refs[0], refs[1:]
        bias_ref, sink_ref, dwin_ref, dbias_ref, dsink_ref, dq_s, dk_s, dv_s = refs
        j, i = pl.program_id(0), pl.program_id(1)

        @pl.when((j == 0) & (i == 0))
        def _():
            dbias_ref[...] = jnp.zeros_like(dbias_ref)
            dsink_ref[...] = jnp.zeros_like(dsink_ref)

        @pl.when(i == 0)
        def _():
            dk_s[...] = jnp.zeros_like(dk_s)
            dv_s[...] = jnp.zeros_like(dv_s)

        @pl.when(i < n_chunks)
        def _():
            w, pw, nw = w_ref[...].astype(F32), pw_ref[...].astype(F32), nw_ref[...].astype(F32)
            kb = _bands(w, pw, nw, QW, kvw, nb)
            vb = _bands(w, pw, nw, QW + kvw, kvw, nb)
            edge = _edge_mask(i * nb, nb, period)
            dq_acc = [jnp.zeros((tl, LANES), F32) for _ in range(heads // 2)]
            for h in range(heads):
                pair, a, kvp, b = _head_geometry(h, group)
                lanes = slice(pair * LANES, (pair + 1) * LANES)
                mine = _lane_half() == a
                q_al = _align(w[:, lanes], a, b).astype(BF16).reshape(nb, b_, LANES)
                do_al = _align(do_ref[:, lanes], a, b).astype(BF16).reshape(nb, b_, LANES)
                lse_h = jnp.max(jnp.where(mine, lse_ref[:, lanes], NEG_INF), axis=1, keepdims=True)
                shift = -jnp.sum(jnp.where(mine, do_ref[:, lanes] * o_ref[:, lanes], 0.0), axis=1, keepdims=True)
                if have_dlse:
                    shift = shift + jnp.sum(jnp.where(mine, dlse_ref[:, lanes], 0.0), axis=1, keepdims=True)
                logits = _bmm(q_al, kb[kvp], 2, 2) * scale + bias_ref[h][None] + edge
                p = jnp.exp(logits - lse_h.reshape(nb, b_, 1))
                dlogits = p * (_bmm(do_al, vb[kvp], 2, 2) + shift.reshape(nb, b_, 1))
                dbias_ref[h] += jnp.sum(dlogits, axis=0)
                dsink_ref[h] += jnp.sum(jnp.exp(sink_ref[h] - lse_h) * shift, axis=0, keepdims=True)
                ds = (dlogits * scale).astype(BF16)
                dq_acc[pair] = dq_acc[pair] + _unalign(_bmm(ds, kb[kvp], 2, 1).reshape(tl, LANES), a, b)
                dk_band = _bmm(ds, q_al, 1, 1)
                dv_band = _bmm(p.astype(BF16), do_al, 1, 1)
                kv_lanes = slice(kvp * LANES, (kvp + 1) * LANES)
                for t in range(3):
                    at = pl.ds(pl.multiple_of(i * tl + t * b_, b_), tl)
                    dk_s[at, kv_lanes] += dk_band[:, t * b_:(t + 1) * b_, :].reshape(tl, LANES)
                    dv_s[at, kv_lanes] += dv_band[:, t * b_:(t + 1) * b_, :].reshape(tl, LANES)
            dq_s[lax.rem(i, 2)] = jnp.concatenate(dq_acc, axis=1)

        @pl.when(i >= 1)
        def _():
            at = pl.ds(pl.multiple_of((i - 1) * tl + b_, b_), tl)
            parts = [dq_s[lax.rem(i + 1, 2)], dk_s[at, :], dv_s[at, :]]
            if QW + 2 * kvw < BAND_W:
                parts.append(jnp.zeros((tl, BAND_W - QW - 2 * kvw), F32))
            dwin_ref[...] = jnp.concatenate(parts, axis=1).astype(dwin_ref.dtype)

    main, prev, nxt, rows = _band_in_specs(tl, nb, n_chunks, n_blocks, col, True)
    row_args = [o, do, lse] + ([dlse] if have_dlse else [])
    small = [pl.BlockSpec(bias.shape, lambda j, i: (0, 0, 0)), pl.BlockSpec(sink.shape, lambda j, i: (0, 0, 0))]
    return pl.pallas_call(
        body,
        grid=(1, n_chunks + 1),
        in_specs=[pl.BlockSpec(memory_space=pl.ANY)] * lead + [main, prev, nxt] + [rows] * len(row_args) + small,
        out_specs=[pl.BlockSpec((tl, BAND_W), lambda j, i: (j * n_chunks + jnp.maximum(i - 1, 0), col))] + small,
        out_shape=[jax.ShapeDtypeStruct(proj.shape, BF16), jax.ShapeDtypeStruct(bias.shape, F32),
                   jax.ShapeDtypeStruct(sink.shape, F32)],
        scratch_shapes=[pltpu.VMEM((2, tl, QW), F32), pltpu.VMEM((seq + 2 * b_, kvw), F32),
                        pltpu.VMEM((seq + 2 * b_, kvw), F32)],
        input_output_aliases={0: 0} if lead else {},
        name=name + "_bwd",
        compiler_params=_params("arbitrary", "arbitrary"),
    )(*([dproj] if lead else []), proj, proj, proj, *row_args, bias, sink)


def _loss_call(x, target, g):
    s, d = x.shape
    tr = _pick(s, (256, 128, 64, 32, 16, 8))

    def tile_loss(xt, gt, tt):
        err = jnp.square(_rms(xt, gt) - tt)
        return 0.5 * jnp.sum(jnp.mean(err, axis=-1, keepdims=True), axis=0, keepdims=True)

    def body(x_ref, t_ref, g_ref, loss_ref, dx_ref, dg_ref):
        tt = t_ref[...]
        val, vjp = jax.vjp(lambda xt, gt: tile_loss(xt, gt, tt), x_ref[...], g_ref[...])
        dx, dg = vjp(jnp.ones_like(val))
        dx_ref[...] = dx

        @pl.when(pl.program_id(0) == 0)
        def _():
            loss_ref[...] = jnp.zeros_like(loss_ref)
            dg_ref[...] = jnp.zeros_like(dg_ref)

        loss_ref[...] += val
        dg_ref[...] += dg

    return pl.pallas_call(
        body,
        grid=(s // tr,),
        in_specs=[_rows(tr, d), _rows(tr, d), _whole((1, d))],
        out_specs=[_whole((1, 1)), _rows(tr, d), _whole((1, d))],
        out_shape=[jax.ShapeDtypeStruct((1, 1), F32), jax.ShapeDtypeStruct((s, d), F32),
                   jax.ShapeDtypeStruct((1, d), F32)],
        name="final_norm_loss",
        compiler_params=_params("arbitrary"),
    )(x, target, g)


@jax.custom_vjp
def _loss_op(x, target, g):
    return _loss_call(x, target, g)[0][0, 0]


def _loss_op_fwd(x, target, g):
    loss, dx, dg = _loss_call(x, target, g)
    return loss[0, 0], (dx, dg, target)


def _loss_op_bwd(res, ct):
    dx, dg, target = res
    return ct * dx, jnp.zeros_like(target), ct * dg


_loss_op.defvjp(_loss_op_fwd, _loss_op_bwd)


def _mla_tile(r, p, a):
    g_q, g_kv, w_q, w_k, w_v = p
    cos_t, sin_t, place_kr = a
    a_q, a_kv, a_kr = _lanes(r[0], (0, MLA_Q_LORA, MLA_Q_LORA + MLA_KV_LORA, MLA_Q_LORA + MLA_KV_LORA + MLA_ROPE))
    q = _rope(_bdot(_rms(a_q, g_q), w_q), cos_t, sin_t, MLA_ROPE // 2)
    ckv = _rms(a_kv, g_kv)
    k = _rope(_bdot(ckv, w_k) + _hdot(a_kr, place_kr), cos_t, sin_t, MLA_ROPE // 2)
    return _split_heads(q, MLA_HEADS), _split_heads(k, MLA_HEADS), _split_heads(_bdot(ckv, w_v), MLA_HEADS)


def _head_rms(x, g, head_mean):
    return x * lax.rsqrt(_hdot(x * x, head_mean) + EPS) * g


def _gqa_tile(r, p, a):
    g_q, g_k = p
    cos_t, sin_t, mean_q, mean_k = a
    wq, wk = GQA_HEADS * HEAD_DIM, GQA_KV_HEADS * HEAD_DIM
    b_q, b_k, b_v = _lanes(r[0], (0, wq, wq + wk, wq + 2 * wk))
    q = _rope(_head_rms(b_q, g_q, mean_q), cos_t, sin_t, HEAD_DIM // 4)
    k = _rope(_head_rms(b_k, g_k, mean_k), cos_t[:, :wk], sin_t[:, :wk], HEAD_DIM // 4)
    return _split_heads(q, GQA_HEADS), _split_heads(k, GQA_KV_HEADS), _split_heads(b_v, GQA_KV_HEADS)


def _permute_rows(p, x, cp):
    pb = p.astype(BF16)
    hi = x.astype(BF16)
    rest = x - hi.astype(F32)
    mid = rest.astype(BF16)
    low = (rest - mid.astype(F32)).astype(BF16)
    dims = (((cp,), (0,)), ((), ()))
    return (lax.dot_general(pb, hi, dims, preferred_element_type=F32)
            + lax.dot_general(pb, mid, dims, preferred_element_type=F32)
            + lax.dot_general(pb, low, dims, preferred_element_type=F32))


@jax.custom_vjp
def _permuted(p, x):
    return _permute_rows(p, x, 1)


def _permuted_fwd(p, x):
    return _permute_rows(p, x, 1), p


def _permuted_bwd(p, ct):
    return jnp.zeros_like(p), _permute_rows(p, ct, 0)


_permuted.defvjp(_permuted_fwd, _permuted_bwd)


def _interleave(p, x):
    return _permuted(p, x.reshape(x.shape[0] * x.shape[1], x.shape[2]))


def _interleave_matrix(rows, dil):
    p = np.zeros((rows, rows), np.float32)
    for t in range(rows):
        p[t, (t % dil) * (rows // dil) + t // dil] = 1.0
    return p


def _merge_tile(r, p, a):
    gm, o_a, o_b, oc0, oc1, oc2, l0, l1, l2, o_d = r
    (w_branch,) = p
    perm1, perm2 = a
    oc1, l1, oc2, l2 = _interleave(perm1, oc1), _interleave(perm1, l1), _interleave(perm2, oc2), _interleave(perm2, l2)
    d = w_branch.shape[2]
    gate_path, merge_logits = _lanes(gm, (0, N_BRANCH * BRANCH_W, N_BRANCH * BRANCH_W + N_BRANCH * d))
    m = jnp.maximum(jnp.maximum(l0, l1), l2)
    e0, e1, e2 = jnp.exp(l0 - m), jnp.exp(l1 - m), jnp.exp(l2 - m)
    y_c = (e0 * oc0 + e1 * oc1 + e2 * oc2) / (e0 + e1 + e2)
    y = jnp.concatenate([_join_heads(o_a), _join_heads(o_b), y_c, o_d], axis=1)
    u = y * (gate_path * jax.nn.sigmoid(gate_path))
    gates = _lanes(merge_logits, tuple(range(0, N_BRANCH * d + 1, d)))
    us = _lanes(u, tuple(range(0, N_BRANCH * BRANCH_W + 1, BRANCH_W)))
    branch_w = _unstack(w_branch)
    out = None
    for nb in range(N_BRANCH):
        term = jax.nn.sigmoid(gates[nb]) * _bdot(us[nb], branch_w[nb])
        out = term if out is None else out + term
    return (out,)


def _mixer_calls(proj, prm, aux):
    s = proj.shape[0]
    tr, tm = _pick(s, (512, 256, 128)), _pick(s, (256,))
    mla_cos, mla_sin, gqa_cos, gqa_sin, place_kr, mean_q, mean_k = aux[:7]
    wq = MLA_HEADS * MLA_QK
    mla = dict(
        steps=s // tr, rows=[(proj, _rows(tr, SMALL_W, MLA_BLK))],
        params=[prm["g_q"], prm["g_kv"], prm["w_q"], prm["w_k"], prm["w_v"]],
        aux=[(mla_cos, _rows(tr, wq)), (mla_sin, _rows(tr, wq)), (place_kr, _whole(place_kr.shape))],
        outs=[((MLA_HEADS, s, MLA_QK), _head_rows(MLA_HEADS, tr, MLA_QK))] * 2
        + [((MLA_HEADS, s, MLA_V), _head_rows(MLA_HEADS, tr, MLA_V))],
        window=((s, P_TOT), _rows(tr, SMALL_W, MLA_BLK), BF16))
    wg = GQA_HEADS * HEAD_DIM
    gqa = dict(
        steps=s // tr, rows=[(proj, _rows(tr, SMALL_W, GQA_BLK))], params=[prm["gq"], prm["gk"]],
        aux=[(gqa_cos, _rows(tr, wg)), (gqa_sin, _rows(tr, wg)), (mean_q, _whole(mean_q.shape)),
             (mean_k, _whole(mean_k.shape))],
        outs=[((GQA_HEADS, s, HEAD_DIM), _head_rows(GQA_HEADS, tr, HEAD_DIM))]
        + [((GQA_KV_HEADS, s, HEAD_DIM), _head_rows(GQA_KV_HEADS, tr, HEAD_DIM))] * 2,
        window=((s, P_TOT), _rows(tr, SMALL_W, GQA_BLK), BF16))
    merge = dict(steps=s // tm, tm=tm, window=((s, P_TOT), _rows(tm, GM_W, 0), BF16))
    return mla, gqa, merge


def _merge_rows(proj, o_a, o_b, ocs, lses, o_d, tm):
    h4 = _head_rows(4, tm, HEAD_DIM)
    s = proj.shape[0]

    def by_residue(t, dil):
        if dil == 1:
            return t, _rows(tm, QW)
        return t.reshape(dil, s // dil, QW), pl.BlockSpec((dil, tm // dil, QW), lambda i: (0, i, 0))

    dils = [dil for _, dil in DIL_PATTERNS]
    return ([(proj, _rows(tm, GM_W, 0)), (o_a, h4), (o_b, h4)] + [by_residue(t, r) for t, r in zip(ocs, dils)]
            + [by_residue(t, r) for t, r in zip(lses, dils)] + [(o_d, _rows(tm, QW))])


def _merge_aux(aux):
    return [(t, _whole(t.shape)) for t in aux[7:9]]


def _to_residues(t, dil):
    s, w = t.shape
    return t if dil == 1 else t.reshape(s // dil, dil, w).transpose(1, 0, 2).reshape(s, w)


def _from_residues(t, dil):
    s, w = t.shape
    return t if dil == 1 else t.reshape(dil, s // dil, w).transpose(1, 0, 2).reshape(s, w)


def _mixer_fwd(projs, prm, aux):
    proj = projs[0]
    s = proj.shape[0]
    mla, gqa, merge = _mixer_calls(proj, prm, aux)
    q_a, k_a, v_a = _fwd_call("prep_mla", _mla_tile, mla["steps"], mla["rows"], mla["params"], mla["aux"], mla["outs"])
    o_a, lse_a = _dense_fwd_call(q_a, k_a, v_a, MLA_QK ** -0.5, "mla")
    q_b, k_b, v_b = _fwd_call("prep_gqa", _gqa_tile, gqa["steps"], gqa["rows"], gqa["params"], gqa["aux"], gqa["outs"])
    grp = GQA_HEADS // GQA_KV_HEADS
    o_b, lse_b = _dense_fwd_call(q_b.reshape(GQA_KV_HEADS, grp * s, HEAD_DIM), k_b, v_b, HEAD_DIM ** -0.5, "gqa")
    scale = HEAD_DIM ** -0.5
    ocs, lses = [], []
    for gi, (_, dil) in enumerate(DIL_PATTERNS):
        o, lse = _band_fwd_call(projs[gi], DIL_BLK if gi == 0 else 0, prm["bias_dil"][gi], prm["no_sink"], dil, 1,
                                QW, scale, "dil%d" % gi)
        ocs.append(o)
        lses.append(lse)
    o_d, lse_d = _band_fwd_call(proj, WIN_BLK, prm["bias_win"], prm["sink"], 1, WIN_HEADS // WIN_KV_HEADS,
                                WIN_KV_HEADS * HEAD_DIM, scale, "win")
    rows = _merge_rows(proj, o_a, o_b.reshape(GQA_HEADS, s, HEAD_DIM), ocs, lses, o_d, merge["tm"])
    mix = _fwd_call("merge", _merge_tile, merge["steps"], rows, [prm["w_branch"]], _merge_aux(aux),
                    [((s, prm["w_branch"].shape[2]), _rows(merge["tm"], prm["w_branch"].shape[2]), BF16)])[0]
    return mix, (q_a, k_a, v_a, o_a, lse_a, q_b, k_b, v_b, o_b, lse_b, ocs, lses, o_d, lse_d)


def _mixer_bwd(projs, prm, aux, saved, dmix):
    proj = projs[0]
    s = proj.shape[0]
    q_a, k_a, v_a, o_a, lse_a, q_b, k_b, v_b, o_b, lse_b, ocs, lses, o_d, lse_d = saved
    dils = [dil for _, dil in DIL_PATTERNS]
    mla, gqa, merge = _mixer_calls(proj, prm, aux)
    tm, d_model = merge["tm"], prm["w_branch"].shape[2]
    grp = GQA_HEADS // GQA_KV_HEADS
    scale = HEAD_DIM ** -0.5

    rows = _merge_rows(proj, o_a, o_b.reshape(GQA_HEADS, s, HEAD_DIM), ocs, lses, o_d, tm)
    grads, (dw_branch,) = _vjp_call(
        "merge", _merge_tile, merge["steps"], rows, [prm["w_branch"]], _merge_aux(aux), [(dmix, _rows(tm, d_model))],
        [merge["window"]] + [(a.shape, spec) for a, spec in rows[1:]])
    dproj, do_a, do_b, docs, dlses, do_d = grads[0], grads[1], grads[2], grads[3:6], grads[6:9], grads[9]

    dq_a, dk_a, dv_a = _dense_bwd_call(q_a, k_a, v_a, o_a, lse_a, do_a, MLA_QK ** -0.5, "mla")
    (dproj,), dmla = _vjp_call("prep_mla", _mla_tile, mla["steps"], mla["rows"], mla["params"], mla["aux"],
                               [(t, spec) for t, (_, spec) in zip((dq_a, dk_a, dv_a), mla["outs"])],
                               [mla["window"]], into=dproj)
    dq_b, dk_b, dv_b = _dense_bwd_call(q_b.reshape(GQA_KV_HEADS, grp * s, HEAD_DIM), k_b, v_b, o_b, lse_b,
                                       do_b.reshape(GQA_KV_HEADS, grp * s, HEAD_DIM), scale, "gqa")
    (dproj,), dgqa = _vjp_call("prep_gqa", _gqa_tile, gqa["steps"], gqa["rows"], gqa["params"], gqa["aux"],
                               [(t, spec) for t, (_, spec) in zip((dq_b.reshape(GQA_HEADS, s, HEAD_DIM), dk_b, dv_b),
                                                                  gqa["outs"])],
                               [gqa["window"]], into=dproj)
    dproj, dbias_win, dsink = _band_bwd_call(proj, o_d, do_d, lse_d, None, prm["bias_win"], prm["sink"], dproj,
                                             WIN_BLK, 1, WIN_HEADS // WIN_KV_HEADS, WIN_KV_HEADS * HEAD_DIM, scale, "win")
    dbias_dil, dprojs = [], []
    for gi, dil in enumerate(dils):
        dside, dbias, _ = _band_bwd_call(
            projs[gi], ocs[gi], docs[gi].reshape(s, QW), lses[gi], dlses[gi].reshape(s, QW),
            prm["bias_dil"][gi], prm["no_sink"], dproj if gi == 0 else None, DIL_BLK if gi == 0 else 0, dil, 1, QW,
            scale, "dil%d" % gi)
        if gi == 0:
            dproj = dside
        else:
            dprojs.append(dside)
        dbias_dil.append(dbias)
    dprm = dict(g_q=dmla[0], g_kv=dmla[1], w_q=dmla[2], w_k=dmla[3], w_v=dmla[4], gq=dgqa[0], gk=dgqa[1],
                bias_dil=dbias_dil, bias_win=dbias_win, sink=dsink, no_sink=jnp.zeros_like(prm["no_sink"]),
                w_branch=dw_branch)
    return [dproj] + dprojs, {k: jax.tree.map(lambda g, p: g.astype(p.dtype), v, prm[k]) for k, v in dprm.items()}


def _layer_fwd(x, w, aux):
    s, d = x.shape
    tr = _pick(s, (256,))
    dils = [dil for _, dil in DIL_PATTERNS]

    def norm_forms(r, p, a):
        y = _rms(r[0], p[0])
        return [y, y.T] + [_dot(q, y, 1, 0).reshape(dil, tr // dil, d) for q, dil in zip(a, dils[1:])]

    forms = _fwd_call(
        "norm", norm_forms, s // tr, [(x, _rows(tr, d))], [w["norm_g"]], [(q, _whole(q.shape)) for q in aux[9:11]],
        [((s, d), _rows(tr, d), BF16), ((d, s), pl.BlockSpec((d, tr), lambda i: (0, i)), BF16)]
        + [((dil, s // dil, d), pl.BlockSpec((dil, tr // dil, d), lambda i: (0, i, 0)), BF16) for dil in dils[1:]])
    xn_t, xns = forms[1], [forms[0]] + [t.reshape(s, d) for t in forms[2:]]
    projs = [_mm(a, b, "nt", "proj%d_fwd" % i, BF16) for i, (a, b) in enumerate(zip(xns, w["w_in_t"]))]
    mix, saved = _mixer_fwd(projs, w["mixer"], aux)
    return _mm(mix, w["w_out"], "nn", "out_proj_nn"), (x, w, aux, xns, xn_t, projs, mix, saved)


@jax.custom_vjp
def _layer_core(x, w, aux):
    return _layer_fwd(x, w, aux)[0]


def _layer_core_bwd(res, dout):
    x, w, aux, xns, xn_t, projs, mix, saved = res
    s, d = x.shape
    tr = _pick(s, (256, 128, 64, 32, 16, 8))
    dils = [dil for _, dil in DIL_PATTERNS]
    dmix = _mm(dout, w["w_out"], "nt", "out_proj_nt")
    dw_out = _mm(mix, dout, "tn", "out_proj_tn", w["w_out"].dtype)
    dprojs, dmixer = _mixer_bwd(projs, w["mixer"], aux, saved, dmix)
    dxn = _mm(dprojs[0], w["w_in_t"][0], "nn", "proj0_dx")
    side = jnp.concatenate([_from_residues(dp, r) for dp, r in zip(dprojs[1:], dils[1:])], axis=1)
    dxn = _mm(side, jnp.concatenate(w["w_in_t"][1:], axis=0), "nn", "proj_side_dx", plus=dxn)
    dw_in_t = [_mm(xn_t, dprojs[0], "nn", "proj0_dw", w["w_in_t"][0].dtype).T]
    dw_in_t += [_mm(dp, a, "tn", "proj%d_dw" % i, wi.dtype)
                for i, (a, dp, wi) in list(enumerate(zip(xns, dprojs, w["w_in_t"])))[1:]]
    (dx,), (dg,) = _vjp_call("norm", _norm_tile, s // tr, [(x, _rows(tr, d))], [w["norm_g"]], [],
                             [(dxn, _rows(tr, d))], [((s, d), _rows(tr, d))])
    dw = dict(norm_g=dg, w_in_t=dw_in_t, mixer=dmixer, w_out=dw_out)
    return dx, dw, tuple(jnp.zeros_like(t) for t in aux)


_layer_core.defvjp(lambda x, w, aux: _layer_fwd(x, w, aux), _layer_core_bwd)


def _rope_angles(pos, dim):
    inv = ROPE_THETA ** (-jnp.arange(0, dim, 2, dtype=F32) / dim)
    return pos.astype(F32)[:, None] * inv[None, :]


def _rope_tables(s):
    pos = jnp.arange(s, dtype=jnp.int32)
    rows = s // GRID_W
    row_idx = jnp.repeat(jnp.arange(rows, dtype=jnp.int32), GRID_W)
    col_idx = jnp.tile(jnp.arange(GRID_W, dtype=jnp.int32), rows)
    a1 = _rope_angles(pos, MLA_ROPE)
    ar = _rope_angles(row_idx, HEAD_DIM // 2)
    ac = _rope_angles(col_idx, HEAD_DIM // 2)
    ones, zeros = jnp.ones((s, MLA_NOPE), F32), jnp.zeros((s, MLA_NOPE), F32)
    mla_cos = jnp.tile(jnp.concatenate([ones, jnp.cos(a1), jnp.cos(a1)], axis=1), (1, MLA_HEADS))
    mla_sin = jnp.tile(jnp.concatenate([zeros, -jnp.sin(a1), jnp.sin(a1)], axis=1), (1, MLA_HEADS))
    gqa_cos = jnp.tile(jnp.concatenate([jnp.cos(ar), jnp.cos(ar), jnp.cos(ac), jnp.cos(ac)], axis=1), (1, GQA_HEADS))
    gqa_sin = jnp.tile(jnp.concatenate([-jnp.sin(ar), jnp.sin(ar), -jnp.sin(ac), jnp.sin(ac)], axis=1), (1, GQA_HEADS))
    return mla_cos, mla_sin, gqa_cos, gqa_sin


def _t5_bucket(rel):
    nb = T5_BUCKETS // 2
    max_exact = nb // 2
    n = jnp.abs(rel)
    nf = jnp.maximum(n, 1).astype(F32)
    large = max_exact + (jnp.log(nf / max_exact) / math.log(T5_MAX_DIST / max_exact) * (nb - max_exact)).astype(jnp.int32)
    large = jnp.minimum(large, nb - 1)
    return jnp.where(rel > 0, nb, 0) + jnp.where(n < max_exact, n, large)


def _band_bias(table, stride, head_lo, heads, half_window):
    b = BAND_BLOCK
    offs = jnp.arange(3 * b)[None, :] - b - jnp.arange(b)[:, None]
    one_hot = (_t5_bucket(offs * stride)[..., None] == jnp.arange(T5_BUCKETS)).astype(F32)
    bias = jnp.dot(one_hot.reshape(b * 3 * b, T5_BUCKETS), table[:, head_lo:head_lo + heads],
                   precision=lax.Precision.HIGHEST)
    bias = bias.T.reshape(heads, b, 3 * b)
    return jnp.where((jnp.abs(offs) <= half_window)[None], bias, NEG_INF)


def _w_in_rows(d):
    mla, gqa, win, dil0 = MLA_BLK * SMALL_W, GQA_BLK * SMALL_W, WIN_BLK * BAND_W, DIL_BLK * BAND_W
    plan, at = [], 0
    for width, target, row in ((256, 0, mla), (128, 0, mla + 256), (32, 0, mla + 384),
                               (256, 0, gqa), (128, 0, gqa + 256), (128, 0, gqa + 384)):
        plan.append((at, width, target, row))
        at += width
    for part in range(3):
        for g in range(len(DIL_PATTERNS)):
            plan.append((at, QW, g, (dil0 if g == 0 else 0) + part * QW))
            at += QW
    for width, row in ((256, win), (128, win + 256), (128, win + 384), (N_BRANCH * BRANCH_W, 0),
                       (N_BRANCH * d, N_BRANCH * BRANCH_W)):
        plan.append((at, width, 0, row))
        at += width
    return plan


@jax.custom_vjp
def _w_in_layout(w_in_t):
    d = w_in_t.shape[1]
    outs = []
    for target, rows in enumerate((P_TOT, BAND_W, BAND_W)):
        parts, at = [], 0
        for start, width, _, row in sorted((p for p in _w_in_rows(d) if p[2] == target), key=lambda p: p[3]):
            if row > at:
                parts.append(jnp.zeros((row - at, d), w_in_t.dtype))
            parts.append(w_in_t[start:start + width])
            at = row + width
        if at < rows:
            parts.append(jnp.zeros((rows - at, d), w_in_t.dtype))
        outs.append(jnp.concatenate(parts, axis=0))
    return outs


def _w_in_layout_fwd(w_in_t):
    return _w_in_layout(w_in_t), None


def _w_in_layout_bwd(_, cts):
    d = cts[0].shape[1]
    return (jnp.concatenate([cts[target][row:row + width] for _, width, target, row in _w_in_rows(d)], axis=0),)


_w_in_layout.defvjp(_w_in_layout_fwd, _w_in_layout_bwd)


def _layer(x, w, l, aux, biases):
    w_kv = w["w_kv_t"][l].T.reshape(MLA_KV_LORA, MLA_HEADS, MLA_NOPE + MLA_V)
    w_k = jnp.concatenate([w_kv[:, :, :MLA_NOPE], jnp.zeros((MLA_KV_LORA, MLA_HEADS, MLA_ROPE), w_kv.dtype)], axis=2)
    dil_bias, win_bias = biases
    prm = dict(
        g_q=w["mla_q_norm_g"][l][None, :], g_kv=w["mla_kv_norm_g"][l][None, :], w_q=w["w_q_t"][l].T,
        w_k=w_k.reshape(MLA_KV_LORA, MLA_HEADS * MLA_QK),
        w_v=w_kv[:, :, MLA_NOPE:].reshape(MLA_KV_LORA, MLA_HEADS * MLA_V),
        gq=jnp.tile(w["gqa_q_norm_g"][l], GQA_HEADS)[None, :], gk=jnp.tile(w["gqa_k_norm_g"][l], GQA_KV_HEADS)[None, :],
        bias_dil=list(dil_bias), bias_win=win_bias, sink=w["win_sink"][l].reshape(WIN_HEADS, 1, 1),
        no_sink=jnp.full((DIL_HEADS, 1, 1), NEG_INF, F32), w_branch=jnp.transpose(w["w_branch_t"][l].reshape(-1, N_BRANCH, BRANCH_W), (1, 2, 0)))
    layer_w = dict(norm_g=w["norm_g"][l][None, :], w_in_t=_w_in_layout(w["w_in_t"][l]), mixer=prm, w_out=w["w_out"][l])
    return x + _layer_core(x, layer_w, aux)


def _local_loss(w, x, target):
    s, d_model = x.shape
    assert d_model == D_MODEL, "the projection's window layout is laid out for d_model 1024"
    place = np.zeros((MLA_ROPE, MLA_HEADS * MLA_QK), np.float32)
    for h in range(MLA_HEADS):
        for i in range(MLA_ROPE):
            place[i, h * MLA_QK + MLA_NOPE + i] = 1.0

    def head_mean(nh):
        m = np.kron(np.eye(nh, dtype=np.float32), np.full((HEAD_DIM, HEAD_DIM), 1.0 / HEAD_DIM, np.float32))
        return jnp.asarray(m)

    merge_tile = _pick(s, (256,))
    norm_tile = _pick(s, (256,))
    aux = _rope_tables(s) + (jnp.asarray(place), head_mean(GQA_HEADS), head_mean(GQA_KV_HEADS)) + tuple(
        jnp.asarray(_interleave_matrix(merge_tile, dil)) for _, dil in DIL_PATTERNS[1:]) + tuple(
        jnp.asarray(_interleave_matrix(norm_tile, dil).T) for _, dil in DIL_PATTERNS[1:])
    table = w["t5_table"]
    dil_bias = [_band_bias(table, dil, gi * DIL_HEADS, DIL_HEADS, window // (2 * dil))
                for gi, (window, dil) in enumerate(DIL_PATTERNS)]
    win_bias = _band_bias(table, 1, len(DIL_PATTERNS) * DIL_HEADS, WIN_HEADS, WIN_HALF)
    for l in range(w["norm_g"].shape[0]):
        x = _layer(x, w, l, aux, (dil_bias, win_bias))
    return _loss_op(x, target, w["final_norm_g"][None, :])


_ANY = pl.BlockSpec(memory_space=pl.ANY)
_MESH = pl.DeviceIdType.MESH


def _all_gather(block, name):
    def body(x_ref, out_ref, send_sems, recv_sems, local_sem):
        x, y, c = lax.axis_index("x"), lax.axis_index("y"), lax.axis_index("c")
        me, sibling = (x, y, c), (x, y, 1 - c)
        chips = [(1 - x, y), (x, 1 - y), (1 - x, 1 - y)]

        def slot(px, py, pc):
            return out_ref.at[4 * px + 2 * py + pc]

        def copy(k, blk, to, src=None):
            return pltpu.make_async_remote_copy(
                src_ref=slot(*blk) if src is None else src, dst_ref=slot(*blk),
                send_sem=send_sems.at[k], recv_sem=recv_sems.at[k], device_id=to, device_id_type=_MESH)

        mine = pltpu.make_async_copy(x_ref, slot(*me), local_sem)
        mine.start()
        first = [copy(0, me, sibling, src=x_ref)]
        first += [copy(1 + j, me, (*chip, c), src=x_ref) for j, chip in enumerate(chips)]
        for cp in first:
            cp.start()
        passed = [copy(4 + j, (*chip, c), sibling) for j, chip in enumerate(chips)]
        for j, chip in enumerate(chips):
            copy(1 + j, (*chip, c), me).wait_recv()
            passed[j].start()
        copy(0, sibling, me).wait_recv()
        for j, chip in enumerate(chips):
            copy(4 + j, (*chip, 1 - c), me).wait_recv()
        for cp in first + passed:
            cp.wait_send()
        mine.wait()

    return pl.pallas_call(
        body,
        out_shape=jax.ShapeDtypeStruct((N_DEV,) + block.shape, block.dtype),
        in_specs=[_ANY],
        out_specs=_ANY,
        scratch_shapes=[pltpu.SemaphoreType.DMA((7,)), pltpu.SemaphoreType.DMA((7,)), pltpu.SemaphoreType.DMA],
        name=name,
    )(block)


def _swap_with_sibling(blocks, name):
    chips = blocks.shape[0]

    def body(x_ref, out_ref, send_sems, recv_sems):
        x, y, c = lax.axis_index("x"), lax.axis_index("y"), lax.axis_index("c")
        copies = [pltpu.make_async_remote_copy(
            src_ref=x_ref.at[k, 1 - c], dst_ref=out_ref.at[k], send_sem=send_sems.at[k], recv_sem=recv_sems.at[k],
            device_id=(x, y, 1 - c), device_id_type=_MESH) for k in range(chips)]
        for cp in copies:
            cp.start()
        for cp in copies:
            cp.wait()

    return pl.pallas_call(
        body,
        out_shape=jax.ShapeDtypeStruct((chips,) + blocks.shape[2:], blocks.dtype),
        in_specs=[_ANY],
        out_specs=_ANY,
        scratch_shapes=[pltpu.SemaphoreType.DMA((chips,)), pltpu.SemaphoreType.DMA((chips,))],
        name=name,
    )(blocks)


def _add_sibling(blocks, theirs, name):
    chips, _, rows, w = blocks.shape
    tr = _row_tile(rows, 16, 4096)

    def body(b_ref, t_ref, o_ref):
        mine = b_ref[0, lax.axis_index("c")]
        o_ref[0] = (mine.astype(F32) + t_ref[0].astype(F32)).astype(o_ref.dtype)

    return pl.pallas_call(
        body,
        grid=(chips, rows // tr),
        in_specs=[pl.BlockSpec((1, 2, tr, w), lambda k, i: (k, 0, i, 0)), pl.BlockSpec((1, tr, w), lambda k, i: (k, i, 0))],
        out_specs=pl.BlockSpec((1, tr, w), lambda k, i: (k, i, 0)),
        out_shape=jax.ShapeDtypeStruct(theirs.shape, theirs.dtype),
        name=name,
        compiler_params=_params("parallel", "parallel"),
    )(blocks, theirs)


def _exchange_chips(partials, name):
    n_chips = partials.shape[0]

    def body(x_ref, out_ref, send_sems, recv_sems, local_sem):
        x, y, c = lax.axis_index("x"), lax.axis_index("y"), lax.axis_index("c")
        me = 2 * x + y
        mine = pltpu.make_async_copy(x_ref.at[me], out_ref.at[me], local_sem)
        mine.start()
        copies, landed = [], []
        for k in range(1, n_chips):
            px = 1 - x if k & 2 else x
            py = 1 - y if k & 1 else y
            peer = 2 * px + py
            copies.append(pltpu.make_async_remote_copy(
                src_ref=x_ref.at[peer], dst_ref=out_ref.at[me], send_sem=send_sems.at[k - 1],
                recv_sem=recv_sems.at[k - 1], device_id=(px, py, c), device_id_type=_MESH))
            landed.append(pltpu.make_async_remote_copy(
                src_ref=x_ref.at[peer], dst_ref=out_ref.at[peer], send_sem=send_sems.at[k - 1],
                recv_sem=recv_sems.at[k - 1], device_id=(px, py, c), device_id_type=_MESH))
        for cp in copies:
            cp.start()
        for cp in landed:
            cp.wait_recv()
        for cp in copies:
            cp.wait_send()
        mine.wait()

    return pl.pallas_call(
        body,
        out_shape=jax.ShapeDtypeStruct(partials.shape, partials.dtype),
        in_specs=[_ANY],
        out_specs=_ANY,
        scratch_shapes=[pltpu.SemaphoreType.DMA((n_chips - 1,)), pltpu.SemaphoreType.DMA((n_chips - 1,)),
                        pltpu.SemaphoreType.DMA],
        name=name,
    )(partials)


def _sum_slots(parts, name):
    slots, rows, w = parts.shape
    tr = _row_tile(rows, 16 if parts.dtype == BF16 else 8, 4096)

    def body(p_ref, o_ref):
        acc = p_ref[0].astype(F32)
        for j in range(1, slots):
            acc = acc + p_ref[j].astype(F32)
        o_ref[...] = acc

    return pl.pallas_call(
        body,
        grid=(rows // tr,),
        in_specs=[pl.BlockSpec((slots, tr, w), lambda i: (0, i, 0))],
        out_specs=pl.BlockSpec((tr, w), lambda i: (i, 0)),
        out_shape=jax.ShapeDtypeStruct((rows, w), F32),
        name=name,
        compiler_params=_params("parallel"),
    )(parts)


def _adamw(w, g, m, v, name):
    rows, width = w.shape
    tr = _row_tile(rows, 8, 2048)

    def body(w_ref, g_ref, m_ref, v_ref, d_ref, nm_ref, nv_ref):
        g_ = g_ref[...]
        m_ = ADAM_B1 * m_ref[...] + (1.0 - ADAM_B1) * g_
        v_ = ADAM_B2 * v_ref[...] + (1.0 - ADAM_B2) * jnp.square(g_)
        m_hat = m_ / (1.0 - ADAM_B1 ** ADAM_STEP)
        v_hat = v_ / (1.0 - ADAM_B2 ** ADAM_STEP)
        d_ref[...] = -ADAM_LR * (m_hat / (jnp.sqrt(v_hat) + ADAM_EPS) + ADAM_WD * w_ref[...])
        nm_ref[...] = m_
        nv_ref[...] = v_

    spec = pl.BlockSpec((tr, width), lambda i: (i, 0))
    return pl.pallas_call(
        body,
        grid=(rows // tr,),
        in_specs=[spec] * 4,
        out_specs=[spec] * 3,
        out_shape=[jax.ShapeDtypeStruct((rows, width), F32)] * 3,
        name=name,
        compiler_params=_params("parallel"),
    )(w, g, m, v)


_SHARDED = (("w_in", 2), ("w_mla_q_up", 2), ("w_mla_kv_up", 2), ("w_branch", 3), ("w_out", 1))
_REPLICATED = ("norm_g", "mla_q_norm_g", "mla_kv_norm_g", "gqa_q_norm_g", "gqa_k_norm_g", "win_sink", "t5_table",
               "final_norm_g")


def _pack(arrays, row_multiple):
    flat = jnp.concatenate([a.reshape(-1) for a in arrays])
    rows = -(-flat.shape[0] // (LANES * row_multiple)) * row_multiple
    return jnp.pad(flat, (0, rows * LANES - flat.shape[0])).reshape(rows, LANES)


def _unpack(packed, shapes):
    flat, out, at = packed.reshape(-1), [], 0
    for shp in shapes:
        n = int(np.prod(shp))
        out.append(flat[at:at + n].reshape(shp))
        at += n
    return out


_TO_WIRE = {
    "w_in": lambda t: jnp.swapaxes(t, 1, 2), "w_mla_q_up": lambda t: jnp.swapaxes(t, 1, 2),
    "w_mla_kv_up": lambda t: jnp.swapaxes(t, 1, 2),
    "w_branch": lambda t: jnp.transpose(t, (0, 3, 1, 2)).reshape(t.shape[0], t.shape[3], -1), "w_out": lambda t: t}
_FROM_WIRE = {
    "w_in": lambda t, shp: jnp.swapaxes(t, 1, 2), "w_mla_q_up": lambda t, shp: jnp.swapaxes(t, 1, 2),
    "w_mla_kv_up": lambda t, shp: jnp.swapaxes(t, 1, 2),
    "w_branch": lambda t, shp: jnp.transpose(t.reshape(shp[0], shp[3], shp[1], shp[2]), (0, 2, 3, 1)),
    "w_out": lambda t, shp: t}
_WIRE_NAME = {"w_in": "w_in_t", "w_mla_q_up": "w_q_t", "w_mla_kv_up": "w_kv_t", "w_branch": "w_branch_t",
              "w_out": "w_out"}


def _transpose_blocks(t, dtype, name):
    depth, a, b = t.shape

    def body(x_ref, o_ref):
        o_ref[0] = x_ref[0].T.astype(o_ref.dtype)

    return pl.pallas_call(
        body,
        grid=(depth,),
        in_specs=[pl.BlockSpec((1, a, b), lambda i: (i, 0, 0))],
        out_specs=pl.BlockSpec((1, b, a), lambda i: (i, 0, 0)),
        out_shape=jax.ShapeDtypeStruct((depth, b, a), dtype),
        name=name,
        compiler_params=_params("parallel"),
    )(t)


def _join_shards(gathered, wire_shapes):
    out, at = [], 0
    for depth, cut, rest in wire_shapes:
        n = depth * cut * rest // LANES
        blk = gathered[:, at:at + n].reshape(N_DEV, depth, cut, rest)
        out.append(jnp.moveaxis(blk, 0, 1).reshape(depth, N_DEV * cut, rest))
        at += n
    return out


def _split_shards(fulls, wire_shapes):
    parts = []
    for full, (depth, cut, rest) in zip(fulls, wire_shapes):
        blk = jnp.moveaxis(full.reshape(depth, N_DEV, cut, rest), 1, 0)
        parts.append(blk.reshape(N_DEV, depth * cut * rest // LANES, LANES))
    packed = jnp.concatenate(parts, axis=1)
    return packed.reshape((N_DEV // 2, 2) + packed.shape[1:])


def kernel(x, norm_g, w_in, mla_q_norm_g, mla_kv_norm_g, w_mla_q_up, w_mla_kv_up, gqa_q_norm_g, gqa_k_norm_g, win_sink, t5_table, w_branch, w_out, final_norm_g, loss_target, m_norm_g, m_w_in, m_mla_q_norm_g, m_mla_kv_norm_g, m_w_mla_q_up, m_w_mla_kv_up, m_gqa_q_norm_g, m_gqa_k_norm_g, m_win_sink, m_t5_table, m_w_branch, m_w_out, m_final_norm_g, v_norm_g, v_w_in, v_mla_q_norm_g, v_mla_kv_norm_g, v_w_mla_q_up, v_w_mla_kv_up, v_gqa_q_norm_g, v_gqa_k_norm_g, v_win_sink, v_t5_table, v_w_branch, v_w_out, v_final_norm_g):
    given = dict(locals())
    names = ("norm_g", "w_in", "mla_q_norm_g", "mla_kv_norm_g", "w_mla_q_up", "w_mla_kv_up", "gqa_q_norm_g",
             "gqa_k_norm_g", "win_sink", "t5_table", "w_branch", "w_out", "final_norm_g")
    shard_names = [n for n, _ in _SHARDED]
    shard_shapes = [given[n].shape for n in shard_names]

    wire = [_transpose_blocks(given[n], BF16, "w_in_to_wire") if n == "w_in" else _TO_WIRE[n](given[n]).astype(BF16)
            for n in shard_names]
    wire_shapes = [t.shape for t in wire]
    gathered = _all_gather(jnp.concatenate([t.reshape(-1, LANES) for t in wire]), "gather_weights")
    weights = {n: given[n] for n in _REPLICATED}
    weights.update(zip([_WIRE_NAME[n] for n in shard_names], _join_shards(gathered, wire_shapes)))

    loss, (gw, gx) = jax.value_and_grad(_local_loss, argnums=(0, 1))(weights, x[0], loss_target[0])
    loss = lax.psum(loss, ("x", "y", "c"))

    send = _split_shards([gw[_WIRE_NAME[n]] for n in shard_names], wire_shapes)
    partials = _add_sibling(send, _swap_with_sibling(send, "swap_grads"), "add_sibling_grads")
    g_wire = _unpack(_sum_slots(_exchange_chips(partials, "scatter_grads"), "sum_grads"), wire_shapes)
    g_shard = [_transpose_blocks(t, F32, "w_in_from_wire") if n == "w_in" else _FROM_WIRE[n](t, shp)
               for n, t, shp in zip(shard_names, g_wire, shard_shapes)]
    rep_shapes = [given[n].shape for n in _REPLICATED]
    g_rep = _unpack(_sum_slots(_all_gather(_pack([gw[n] for n in _REPLICATED], 8), "gather_small_grads"),
                               "sum_small_grads"), rep_shapes)
    grads = dict(zip(shard_names, g_shard))
    grads.update(zip(_REPLICATED, g_rep))

    def update(group, shapes, row_multiple, name):
        outs = _adamw(*[_pack([src[n] for n in group], row_multiple) for src in (
            given, grads, {n: given["m_" + n] for n in group}, {n: given["v_" + n] for n in group})], name)
        return [dict(zip(group, _unpack(o, shapes))) for o in outs]

    big = update(shard_names, shard_shapes, 16, "adamw_shards")
    small = update(list(_REPLICATED), rep_shapes, 8, "adamw_replicated")
    delta, new_m, new_v = [{**b, **s_} for b, s_ in zip(big, small)]
    return (loss, gx[None], *[grads[n] for n in names], *[delta[n] for n in names],
            *[new_m[n] for n in names], *[new_v[n] for n in names])
```

```python
import functools
import math

import jax
import jax.numpy as jnp
import numpy as np
from jax import lax
from jax.experimental import pallas as pl
from jax.experimental.pallas import tpu as pltpu

F32 = jnp.float32
BF16 = jnp.bfloat16
N_DEV = 8
LANES = 128
HALF = LANES // 2
V7X_VMEM_LIMIT = 56 * 1024 * 1024

EPS = 1e-6
NEG_INF = -1e30
LOG2E = 1.4426950408889634
ROPE_THETA = 10000.0
GRID_W = 64
HEAD_DIM = 64
N_BRANCH = 4
BRANCH_W = 256
MLA_HEADS, MLA_Q_LORA, MLA_KV_LORA, MLA_NOPE, MLA_ROPE, MLA_V = 4, 256, 128, 64, 32, 64
MLA_QK = MLA_NOPE + MLA_ROPE
GQA_HEADS, GQA_KV_HEADS = 4, 2
DIL_PATTERNS = ((128, 1), (512, 4), (2048, 16))
DIL_HEADS = 4
WIN_HEADS, WIN_KV_HEADS, WIN_HALF = 4, 2, 128
T5_BUCKETS, T5_MAX_DIST = 32, 1024
BAND_BLOCK = 128
ADAM_LR, ADAM_B1, ADAM_B2, ADAM_EPS, ADAM_WD, ADAM_STEP = 0.001, 0.9, 0.999, 1e-08, 0.01, 10

D_MODEL = 1024
GM_W, SMALL_W, BAND_W = 5120, 512, 768
MLA_BLK, GQA_BLK, WIN_BLK, DIL_BLK = 10, 11, 8, 9
P_TOT = 7680
QW = 256


def _params(*sem):
    return pltpu.CompilerParams(dimension_semantics=sem, vmem_limit_bytes=V7X_VMEM_LIMIT)


def _pick(n, cands):
    for c in cands:
        if n % c == 0:
            return c
    return n


def _row_tile(rows, unit, cap):
    best = unit
    for t in range(unit, min(rows, cap) + 1, unit):
        if rows % t == 0:
            best = t
    assert rows % best == 0
    return best


def _dot(a, b, ca, cb):
    return lax.dot_general(a.astype(BF16), b.astype(BF16), (((ca,), (cb,)), ((), ())), preferred_element_type=F32)


def _bmm(a, b, ca, cb):
    return lax.dot_general(a, b, (((ca,), (cb,)), ((0,), (0,))), preferred_element_type=F32)


@jax.custom_vjp
def _bdot(a, b):
    return _dot(a, b, 1, 0)


def _bdot_fwd(a, b):
    return _dot(a, b, 1, 0), (a, b)


def _bdot_bwd(res, g):
    a, b = res
    return _dot(g, b, 1, 1), _dot(a, g, 0, 0)


_bdot.defvjp(_bdot_fwd, _bdot_bwd)


def _hdot(a, c):
    return lax.dot_general(a, c, (((1,), (0,)), ((), ())), precision=lax.Precision.HIGHEST, preferred_element_type=F32)


@functools.partial(jax.custom_vjp, nondiff_argnums=(1,))
def _lane_roll(x, shift):
    return pltpu.roll(x, shift, 1)


def _lane_roll_fwd(x, shift):
    return pltpu.roll(x, shift, 1), None


def _lane_roll_bwd(shift, _, g):
    return (pltpu.roll(g, g.shape[1] - shift, 1),)


_lane_roll.defvjp(_lane_roll_fwd, _lane_roll_bwd)


@functools.partial(jax.custom_vjp, nondiff_argnums=(1,))
def _lane_ranges(x, cut):
    bounds, _ = cut
    return tuple(x[:, lo:hi] for lo, hi in zip(bounds[:-1], bounds[1:]))


def _lane_ranges_fwd(x, cut):
    return _lane_ranges(x, cut), None


def _lane_ranges_bwd(cut, _, cts):
    bounds, width = cut
    parts = list(cts)
    if bounds[-1] < width:
        parts.append(jnp.zeros((cts[0].shape[0], width - bounds[-1]), cts[0].dtype))
    return (jnp.concatenate(parts, axis=1),)


_lane_ranges.defvjp(_lane_ranges_fwd, _lane_ranges_bwd)


def _lanes(x, bounds):
    return _lane_ranges(x, (tuple(bounds), x.shape[1]))


@jax.custom_vjp
def _unstack(x):
    return tuple(x[i] for i in range(x.shape[0]))


def _unstack_fwd(x):
    return _unstack(x), None


def _unstack_bwd(_, cts):
    return (jnp.stack(cts, axis=0),)


_unstack.defvjp(_unstack_fwd, _unstack_bwd)


@functools.partial(jax.custom_vjp, nondiff_argnums=(1,))
def _split_heads(x, h):
    d = x.shape[1] // h
    return jnp.stack([x[:, i * d:(i + 1) * d] for i in range(h)], axis=0)


def _split_heads_fwd(x, h):
    return _split_heads(x, h), None


def _split_heads_bwd(h, _, ct):
    return (jnp.concatenate([ct[i] for i in range(h)], axis=1),)


_split_heads.defvjp(_split_heads_fwd, _split_heads_bwd)


def _join_heads(x):
    return jnp.concatenate(_unstack(x), axis=1)


def _rope(x, cos_t, sin_t, half):
    w = x.shape[1]
    lane = lax.broadcasted_iota(jnp.int32, (1, w), 1)
    first = (lane % (2 * half)) < half
    partner = jnp.where(first, _lane_roll(x, w - half), _lane_roll(x, half))
    return x * cos_t + partner * sin_t


def _rms(x, g):
    return x * lax.rsqrt(jnp.mean(x * x, axis=-1, keepdims=True) + EPS) * g


def _rows(tr, w, col=0):
    return pl.BlockSpec((tr, w), lambda i: (i, col))


def _head_rows(h, tr, d):
    return pl.BlockSpec((h, tr, d), lambda i: (0, i, 0))


def _whole(shape):
    nd = len(shape)
    return pl.BlockSpec(tuple(shape), lambda i: (0,) * nd)


def _fwd_call(name, fn, steps, rows, params, aux, outs):
    nr, npar, na = len(rows), len(params), len(aux)

    def body(*refs):
        vals = [x[...].astype(F32) for x in refs[:nr + npar + na]]
        res = fn(vals[:nr], vals[nr:nr + npar], vals[nr + npar:])
        for o_ref, o in zip(refs[nr + npar + na:], res):
            o_ref[...] = o.astype(o_ref.dtype)

    return pl.pallas_call(
        body,
        grid=(steps,),
        in_specs=[s for _, s in rows] + [_whole(p.shape) for p in params] + [s for _, s in aux],
        out_specs=[e[1] for e in outs],
        out_shape=[jax.ShapeDtypeStruct(e[0], e[2] if len(e) > 2 else F32) for e in outs],
        name=name + "_fwd",
        compiler_params=_params("parallel"),
    )(*[a for a, _ in rows], *params, *[a for a, _ in aux])


def _vjp_call(name, fn, steps, rows, params, aux, cts, row_grads, into=None):
    nr, npar, na, nc = len(rows), len(params), len(aux), len(cts)
    n_in = nr + npar + na + nc
    lead = 0 if into is None else 1

    def body(*refs):
        refs = refs[lead:]
        vals = [x[...].astype(F32) for x in refs[:n_in]]
        r, p, a, d = vals[:nr], vals[nr:nr + npar], vals[nr + npar:nr + npar + na], vals[nr + npar + na:]
        out_refs = refs[n_in:]
        _, vjp = jax.vjp(lambda r_, p_: tuple(fn(r_, p_, a)), r, p)
        dr, dp = vjp(tuple(d))
        for o_ref, o in zip(out_refs[:nr], dr):
            o_ref[...] = o.astype(o_ref.dtype)

        @pl.when(pl.program_id(0) == 0)
        def _():
            for o_ref in out_refs[nr:]:
                o_ref[...] = jnp.zeros_like(o_ref)

        for o_ref, o in zip(out_refs[nr:], dp):
            o_ref[...] += o

    outs = pl.pallas_call(
        body,
        grid=(steps,),
        in_specs=([] if into is None else [pl.BlockSpec(memory_space=pl.ANY)])
        + [s for _, s in rows] + [_whole(p.shape) for p in params] + [s for _, s in aux] + [s for _, s in cts],
        out_specs=[e[1] for e in row_grads] + [_whole(p.shape) for p in params],
        out_shape=[jax.ShapeDtypeStruct(e[0], e[2] if len(e) > 2 else F32) for e in row_grads]
        + [jax.ShapeDtypeStruct(p.shape, F32) for p in params],
        input_output_aliases={} if into is None else {0: 0},
        name=name + "_bwd",
        compiler_params=_params("arbitrary"),
    )(*([] if into is None else [into]), *[a for a, _ in rows], *params, *[a for a, _ in aux], *[a for a, _ in cts])
    return list(outs[:nr]), list(outs[nr:])


def _norm_tile(r, p, a):
    return (_rms(r[0], p[0]),)


def _mm(a, b, mode, name, out_dtype=F32):
    if mode == "nn":
        (m, k), n = a.shape, b.shape[1]
    elif mode == "nt":
        (m, k), n = a.shape, b.shape[0]
    else:
        (k, m), n = a.shape, b.shape[1]
    tn = _pick(n, (1024, 768, 512, 384, 256, 128))
    budget = V7X_VMEM_LIMIT * 3 // 4
    out_bytes = 4 + 2 * np.dtype(out_dtype).itemsize

    def tiles():
        for tm in (2048, 1024, 512, 256, 128):
            for tk in (512, 256, 128) if mode == "tn" else (4096, 1024, 768, 512, 384, 256, 128):
                need = 2 * tk * (tm * a.dtype.itemsize + tn * b.dtype.itemsize) + tm * tn * out_bytes
                if m % tm == 0 and k % tk == 0 and need <= budget:
                    return tm, tk
        return _pick(m, (128,)), _pick(k, (128,))

    tm, tk = tiles()
    nk = k // tk

    def body(*refs):
        a_ref, b_ref, o_ref, acc_ref = refs
        kk = pl.program_id(2)
        if mode == "nn":
            part = _dot(a_ref[...], b_ref[...], 1, 0)
        elif mode == "nt":
            part = _dot(a_ref[...], b_ref[...], 1, 1)
        else:
            part = _dot(a_ref[...], b_ref[...], 0, 0)
        if nk == 1:
            o_ref[...] = part.astype(o_ref.dtype)
        else:
            @pl.when(kk == 0)
            def _():
                acc_ref[...] = part

            @pl.when(kk > 0)
            def _():
                acc_ref[...] += part

            @pl.when(kk == nk - 1)
            def _():
                o_ref[...] = acc_ref[...].astype(o_ref.dtype)

    if mode == "nn":
        a_spec = pl.BlockSpec((tm, tk), lambda i, j, kk: (i, kk))
        b_spec = pl.BlockSpec((tk, tn), lambda i, j, kk: (kk, j))
    elif mode == "nt":
        a_spec = pl.BlockSpec((tm, tk), lambda i, j, kk: (i, kk))
        b_spec = pl.BlockSpec((tn, tk), lambda i, j, kk: (j, kk))
    else:
        a_spec = pl.BlockSpec((tk, tm), lambda i, j, kk: (kk, i))
        b_spec = pl.BlockSpec((tk, tn), lambda i, j, kk: (kk, j))
    o_spec = pl.BlockSpec((tm, tn), lambda i, j, kk: (i, j))
    return pl.pallas_call(
        body,
        grid=(m // tm, n // tn, nk),
        in_specs=[a_spec, b_spec],
        out_specs=o_spec,
        out_shape=jax.ShapeDtypeStruct((m, n), out_dtype),
        scratch_shapes=[pltpu.VMEM((tm, tn), F32)],
        name=name,
        compiler_params=_params("parallel", "parallel", "arbitrary"),
    )(a, b)


def _dense_fwd_call(q, k, v, scale, name):
    n, sq, d = q.shape
    sk, dv = k.shape[1], v.shape[2]
    tq = _pick(sq, (512, 256, 128))
    c = scale * LOG2E

    nkb = 1

    def body(q_ref, k_ref, v_ref, o_ref, lse_ref, m_s, acc_s, vext_s):
        j = pl.program_id(2)

        @pl.when(j == 0)
        def _():
            m_s[...] = jnp.full_like(m_s, NEG_INF)
            acc_s[...] = jnp.zeros_like(acc_s)
            vext_s[...] = jnp.ones_like(vext_s)

        vext_s[:, :dv] = v_ref[0].astype(BF16)
        m_old = m_s[...]
        s = _dot(q_ref[0], k_ref[0], 1, 1)
        m_new = jnp.maximum(m_old, jnp.max(s, axis=1, keepdims=True))
        p = jnp.exp2(s * c - m_new * c)
        acc = jnp.exp2((m_old - m_new) * c) * acc_s[...] + _dot(p, vext_s[...], 1, 0)
        m_s[...] = m_new
        acc_s[...] = acc

        @pl.when(j == nkb - 1)
        def _():
            l = acc[:, dv:dv + 1]
            o_ref[0] = acc[:, :dv] / l
            lse_ref[0] = m_new * scale + jnp.log(l)

    return pl.pallas_call(
        body,
        grid=(n, sq // tq, nkb),
        in_specs=[
            pl.BlockSpec((1, tq, d), lambda h, i, j: (h, i, 0)),
            pl.BlockSpec((1, sk // nkb, d), lambda h, i, j: (h, j, 0)),
            pl.BlockSpec((1, sk // nkb, dv), lambda h, i, j: (h, j, 0)),
        ],
        out_specs=[
            pl.BlockSpec((1, tq, dv), lambda h, i, j: (h, i, 0)),
            pl.BlockSpec((1, tq, 1), lambda h, i, j: (h, i, 0)),
        ],
        out_shape=[jax.ShapeDtypeStruct((n, sq, dv), F32), jax.ShapeDtypeStruct((n, sq, 1), F32)],
        scratch_shapes=[pltpu.VMEM((tq, 1), F32), pltpu.VMEM((tq, 2 * dv), F32), pltpu.VMEM((sk // nkb, 2 * dv), BF16)],
        name=name + "_fwd",
        compiler_params=_params("parallel", "parallel", "arbitrary"),
    )(q, k, v)


def _dense_bwd_call(q, k, v, o, lse, do, scale, name):
    n, sq, d = q.shape
    sk, dv = k.shape[1], v.shape[2]
    tq, tk = _pick(sq, (1024, 512, 256, 128)), _pick(sk, (2048, 1024, 512, 256, 128))
    c = scale * LOG2E

    def body(q_ref, k_ref, v_ref, o_ref, lse_ref, do_ref, dq_ref, dk_ref, dv_ref):
        j, i = pl.program_id(1), pl.program_id(2)
        qb, kb, vb = q_ref[0].astype(BF16), k_ref[0].astype(BF16), v_ref[0].astype(BF16)
        do_f = do_ref[0]
        dob = do_f.astype(BF16)
        p = jnp.exp2(_dot(qb, kb, 1, 1) * c - lse_ref[0] * LOG2E)
        delta = jnp.sum(do_f * o_ref[0], axis=1, keepdims=True)
        ds = (p * (_dot(dob, vb, 1, 1) - delta)).astype(BF16)
        dv_part = _dot(p, dob, 0, 0)
        dk_part = _dot(ds, qb, 0, 0) * scale
        dq_part = _dot(ds, kb, 1, 0) * scale
        rows = pl.ds(pl.multiple_of(i * tq, tq), tq)

        @pl.when(i == 0)
        def _():
            dk_ref[0] = dk_part
            dv_ref[0] = dv_part

        @pl.when(i > 0)
        def _():
            dk_ref[0] += dk_part
            dv_ref[0] += dv_part

        @pl.when(j == 0)
        def _():
            dq_ref[0, rows, :] = dq_part

        @pl.when(j > 0)
        def _():
            dq_ref[0, rows, :] += dq_part

    return pl.pallas_call(
        body,
        grid=(n, sk // tk, sq // tq),
        in_specs=[
            pl.BlockSpec((1, tq, d), lambda h, j, i: (h, i, 0)),
            pl.BlockSpec((1, tk, d), lambda h, j, i: (h, j, 0)),
            pl.BlockSpec((1, tk, dv), lambda h, j, i: (h, j, 0)),
            pl.BlockSpec((1, tq, dv), lambda h, j, i: (h, i, 0)),
            pl.BlockSpec((1, tq, 1), lambda h, j, i: (h, i, 0)),
            pl.BlockSpec((1, tq, dv), lambda h, j, i: (h, i, 0)),
        ],
        out_specs=[
            pl.BlockSpec((1, sq, d), lambda h, j, i: (h, 0, 0)),
            pl.BlockSpec((1, tk, d), lambda h, j, i: (h, j, 0)),
            pl.BlockSpec((1, tk, dv), lambda h, j, i: (h, j, 0)),
        ],
        out_shape=[
            jax.ShapeDtypeStruct((n, sq, d), F32),
            jax.ShapeDtypeStruct((n, sk, d), F32),
            jax.ShapeDtypeStruct((n, sk, dv), F32),
        ],
        name=name + "_bwd",
        compiler_params=_params("arbitrary", "arbitrary", "arbitrary"),
    )(q, k, v, o, lse, do)


def _head_geometry(h, group):
    pair, a = divmod(h, 2)
    kv_pair, b = divmod(h // group, 2)
    return pair, a, kv_pair, b


def _lane_half():
    return lax.broadcasted_iota(jnp.int32, (1, LANES), 1) // HALF


def _align(x, a, b):
    if a != b:
        x = pltpu.roll(x, HALF, 1)
    return jnp.where(_lane_half() == b, x, 0.0)


def _unalign(x, a, b):
    x = jnp.where(_lane_half() == b, x, 0.0)
    return pltpu.roll(x, HALF, 1) if a != b else x


def _bands(w, pw, nw, lo, kvw, nb):
    b = BAND_BLOCK
    cat = jnp.concatenate([pw[:, lo:lo + kvw], w[:, lo:lo + kvw], nw[:, lo:lo + kvw]], axis=0).astype(BF16)
    out = []
    for g in range(kvw // LANES):
        c3 = cat[:, g * LANES:(g + 1) * LANES].reshape(nb + 2, b, LANES)
        out.append(jnp.concatenate([c3[0:nb], c3[1:nb + 1], c3[2:nb + 2]], axis=1))
    return out


def _edge_mask(first_block, nb, period):
    b = BAND_BLOCK
    blk = (first_block + lax.broadcasted_iota(jnp.int32, (nb, 1, 3 * b), 0)) % period
    col = lax.broadcasted_iota(jnp.int32, (nb, 1, 3 * b), 2)
    outside = ((col < b) & (blk == 0)) | ((col >= 2 * b) & (blk == period - 1))
    return jnp.where(outside, NEG_INF, 0.0)


def _band_geometry(proj, dil):
    rows = proj.shape[0]
    tl = _pick(rows, (1024, 512, 256, 128))
    return rows, tl, tl // BAND_BLOCK, rows // tl, rows // dil // BAND_BLOCK


def _band_in_specs(tl, nb, n_chunks, n_blocks, col, last_step_idle):
    def chunk(i):
        return jnp.minimum(i, n_chunks - 1) if last_step_idle else i

    main = pl.BlockSpec((tl, BAND_W), lambda j, i: (j * n_chunks + chunk(i), col))
    prev = pl.BlockSpec((BAND_BLOCK, BAND_W),
                        lambda j, i: (j * n_blocks + jnp.maximum(chunk(i) * nb - 1, 0), col))
    nxt = pl.BlockSpec((BAND_BLOCK, BAND_W),
                       lambda j, i: (j * n_blocks + jnp.minimum((chunk(i) + 1) * nb, n_blocks - 1), col))
    rows = pl.BlockSpec((tl, QW), lambda j, i: (j * n_chunks + chunk(i), 0))
    return main, prev, nxt, rows


def _band_fwd_call(proj, col, bias, sink, dil, group, kvw, scale, name):
    s_tok = proj.shape[0]
    seq, tl, nb, n_chunks, period = _band_geometry(proj, dil)
    n_blocks = seq // BAND_BLOCK
    heads = bias.shape[0]

    def body(w_ref, pw_ref, nw_ref, bias_ref, sink_ref, o_ref, lse_ref):
        i = pl.program_id(1)
        w, pw, nw = w_ref[...].astype(F32), pw_ref[...].astype(F32), nw_ref[...].astype(F32)
        kb = _bands(w, pw, nw, QW, kvw, nb)
        vb = _bands(w, pw, nw, QW + kvw, kvw, nb)
        edge = _edge_mask(i * nb, nb, period)
        o_acc = [jnp.zeros((tl, LANES), F32) for _ in range(heads // 2)]
        lse_acc = [jnp.zeros((tl, LANES), F32) for _ in range(heads // 2)]
        geom = [_head_geometry(h, group) for h in range(heads)]
        logits = []
        for h, (pair, a, kvp, b) in enumerate(geom):
            q_al = _align(w[:, pair * LANES:(pair + 1) * LANES], a, b).astype(BF16).reshape(nb, BAND_BLOCK, LANES)
            logits.append(_bmm(q_al, kb[kvp], 2, 2) * scale + bias_ref[h][None] + edge)
        es, ssums, ms = [], [], []
        for h in range(heads):
            sk = sink_ref[h].reshape(1, 1, 1)
            m = jnp.maximum(jnp.max(logits[h], axis=2, keepdims=True), sk)
            e = jnp.exp(logits[h] - m)
            es.append(e.astype(BF16))
            ssums.append(jnp.sum(e, axis=2, keepdims=True) + jnp.exp(sk - m))
            ms.append(m)
        for h, (pair, a, kvp, b) in enumerate(geom):
            out = _bmm(es[h], vb[kvp], 2, 1) / ssums[h]
            o_acc[pair] = o_acc[pair] + _unalign(out.reshape(tl, LANES), a, b)
            lse = (ms[h] + jnp.log(ssums[h])).reshape(tl, 1)
            lse_acc[pair] = lse_acc[pair] + jnp.where(_lane_half() == a, lse, 0.0)
        o_ref[...] = jnp.concatenate(o_acc, axis=1)
        lse_ref[...] = jnp.concatenate(lse_acc, axis=1)

    main, prev, nxt, rows = _band_in_specs(tl, nb, n_chunks, n_blocks, col, False)
    return pl.pallas_call(
        body,
        grid=(1, n_chunks),
        in_specs=[main, prev, nxt, pl.BlockSpec(bias.shape, lambda j, i: (0, 0, 0)),
                  pl.BlockSpec(sink.shape, lambda j, i: (0, 0, 0))],
        out_specs=[rows, rows],
        out_shape=[jax.ShapeDtypeStruct((s_tok, QW), F32)] * 2,
        name=name + "_fwd",
        compiler_params=_params("parallel", "parallel"),
    )(proj, proj, proj, bias, sink)


def _band_bwd_call(proj, o, do, lse, dlse, bias, sink, dproj, col, dil, group, kvw, scale, name):
    seq, tl, nb, n_chunks, period = _band_geometry(proj, dil)
    lead = 0 if dproj is None else 1
    n_blocks = seq // BAND_BLOCK
    heads = bias.shape[0]
    b_ = BAND_BLOCK
    have_dlse = dlse is not None

    def body(*refs):
        (w_ref, pw_ref, nw_ref, o_ref, do_ref, lse_ref), refs = refs[lead:lead + 6], refs[lead + 6:]
        if have_dlse:
            dlse_ref, refs = refs[0], refs[1:]
        bias_ref, sink_ref, dwin_ref, dbias_ref, dsink_ref, dq_s, dk_s, dv_s = refs
        j, i = pl.program_id(0), pl.program_id(1)

        @pl.when((j == 0) & (i == 0))
        def _():
            dbias_ref[...] = jnp.zeros_like(dbias_ref)
            dsink_ref[...] = jnp.zeros_like(dsink_ref)

        @pl.when(i == 0)
        def _():
            dk_s[...] = jnp.zeros_like(dk_s)
            dv_s[...] = jnp.zeros_like(dv_s)

        @pl.when(i < n_chunks)
        def _():
            w, pw, nw = w_ref[...].astype(F32), pw_ref[...].astype(F32), nw_ref[...].astype(F32)
            kb = _bands(w, pw, nw, QW, kvw, nb)
            vb = _bands(w, pw, nw, QW + kvw, kvw, nb)
            edge = _edge_mask(i * nb, nb, period)
            dq_acc = [jnp.zeros((tl, LANES), F32) for _ in range(heads // 2)]
            for h in range(heads):
                pair, a, kvp, b = _head_geometry(h, group)
                lanes = slice(pair * LANES, (pair + 1) * LANES)
                mine = _lane_half() == a
                q_al = _align(w[:, lanes], a, b).astype(BF16).reshape(nb, b_, LANES)
                do_al = _align(do_ref[:, lanes], a, b).astype(BF16).reshape(nb, b_, LANES)
                lse_h = jnp.max(jnp.where(mine, lse_ref[:, lanes], NEG_INF), axis=1, keepdims=True)
                shift = -jnp.sum(jnp.where(mine, do_ref[:, lanes] * o_ref[:, lanes], 0.0), axis=1, keepdims=True)
                if have_dlse:
                    shift = shift + jnp.sum(jnp.where(mine, dlse_ref[:, lanes], 0.0), axis=1, keepdims=True)
                logits = _bmm(q_al, kb[kvp], 2, 2) * scale + bias_ref[h][None] + edge
                p = jnp.exp(logits - lse_h.reshape(nb, b_, 1))
                dlogits = p * (_bmm(do_al, vb[kvp], 2, 2) + shift.reshape(nb, b_, 1))
                dbias_ref[h] += jnp.sum(dlogits, axis=0)
                dsink_ref[h] += jnp.sum(jnp.exp(sink_ref[h] - lse_h) * shift, axis=0, keepdims=True)
                ds = (dlogits * scale).astype(BF16)
                dq_acc[pair] = dq_acc[pair] + _unalign(_bmm(ds, kb[kvp], 2, 1).reshape(tl, LANES), a, b)
                dk_band = _bmm(ds, q_al, 1, 1)
                dv_band = _bmm(p.astype(BF16), do_al, 1, 1)
                kv_lanes = slice(kvp * LANES, (kvp + 1) * LANES)
                for t in range(3):
                    at = pl.ds(pl.multiple_of(i * tl + t * b_, b_), tl)
                    dk_s[at, kv_lanes] += dk_band[:, t * b_:(t + 1) * b_, :].reshape(tl, LANES)
                    dv_s[at, kv_lanes] += dv_band[:, t * b_:(t + 1) * b_, :].reshape(tl, LANES)
            dq_s[lax.rem(i, 2)] = jnp.concatenate(dq_acc, axis=1)

        @pl.when(i >= 1)
        def _():
            at = pl.ds(pl.multiple_of((i - 1) * tl + b_, b_), tl)
            parts = [dq_s[lax.rem(i + 1, 2)], dk_s[at, :], dv_s[at, :]]
            if QW + 2 * kvw < BAND_W:
                parts.append(jnp.zeros((tl, BAND_W - QW - 2 * kvw), F32))
            dwin_ref[...] = jnp.concatenate(parts, axis=1).astype(dwin_ref.dtype)

    main, prev, nxt, rows = _band_in_specs(tl, nb, n_chunks, n_blocks, col, True)
    row_args = [o, do, lse] + ([dlse] if have_dlse else [])
    small = [pl.BlockSpec(bias.shape, lambda j, i: (0, 0, 0)), pl.BlockSpec(sink.shape, lambda j, i: (0, 0, 0))]
    return pl.pallas_call(
        body,
        grid=(1, n_chunks + 1),
        in_specs=[pl.BlockSpec(memory_space=pl.ANY)] * lead + [main, prev, nxt] + [rows] * len(row_args) + small,
        out_specs=[pl.BlockSpec((tl, BAND_W), lambda j, i: (j * n_chunks + jnp.maximum(i - 1, 0), col))] + small,
        out_shape=[jax.ShapeDtypeStruct(proj.shape, BF16), jax.ShapeDtypeStruct(bias.shape, F32),
                   jax.ShapeDtypeStruct(sink.shape, F32)],
        scratch_shapes=[pltpu.VMEM((2, tl, QW), F32), pltpu.VMEM((seq + 2 * b_, kvw), F32),
                        pltpu.VMEM((seq + 2 * b_, kvw), F32)],
        input_output_aliases={0: 0} if lead else {},
        name=name + "_bwd",
        compiler_params=_params("arbitrary", "arbitrary"),
    )(*([dproj] if lead else []), proj, proj, proj, *row_args, bias, sink)


def _loss_call(x, target, g):
    s, d = x.shape
    tr = _pick(s, (256, 128, 64, 32, 16, 8))

    def tile_loss(xt, gt, tt):
        err = jnp.square(_rms(xt, gt) - tt)
        return 0.5 * jnp.sum(jnp.mean(err, axis=-1, keepdims=True), axis=0, keepdims=True)

    def body(x_ref, t_ref, g_ref, loss_ref, dx_ref, dg_ref):
        tt = t_ref[...]
        val, vjp = jax.vjp(lambda xt, gt: tile_loss(xt, gt, tt), x_ref[...], g_ref[...])
        dx, dg = vjp(jnp.ones_like(val))
        dx_ref[...] = dx

        @pl.when(pl.program_id(0) == 0)
        def _():
            loss_ref[...] = jnp.zeros_like(loss_ref)
            dg_ref[...] = jnp.zeros_like(dg_ref)

        loss_ref[...] += val
        dg_ref[...] += dg

    return pl.pallas_call(
        body,
        grid=(s // tr,),
        in_specs=[_rows(tr, d), _rows(tr, d), _whole((1, d))],
        out_specs=[_whole((1, 1)), _rows(tr, d), _whole((1, d))],
        out_shape=[jax.ShapeDtypeStruct((1, 1), F32), jax.ShapeDtypeStruct((s, d), F32),
                   jax.ShapeDtypeStruct((1, d), F32)],
        name="final_norm_loss",
        compiler_params=_params("arbitrary"),
    )(x, target, g)


@jax.custom_vjp
def _loss_op(x, target, g):
    return _loss_call(x, target, g)[0][0, 0]


def _loss_op_fwd(x, target, g):
    loss, dx, dg = _loss_call(x, target, g)
    return loss[0, 0], (dx, dg, target)


def _loss_op_bwd(res, ct):
    dx, dg, target = res
    return ct * dx, jnp.zeros_like(target), ct * dg


_loss_op.defvjp(_loss_op_fwd, _loss_op_bwd)


def _mla_tile(r, p, a):
    g_q, g_kv, w_q, w_k, w_v = p
    cos_t, sin_t, place_kr = a
    a_q, a_kv, a_kr = _lanes(r[0], (0, MLA_Q_LORA, MLA_Q_LORA + MLA_KV_LORA, MLA_Q_LORA + MLA_KV_LORA + MLA_ROPE))
    q = _rope(_bdot(_rms(a_q, g_q), w_q), cos_t, sin_t, MLA_ROPE // 2)
    ckv = _rms(a_kv, g_kv)
    k = _rope(_bdot(ckv, w_k) + _hdot(a_kr, place_kr), cos_t, sin_t, MLA_ROPE // 2)
    return _split_heads(q, MLA_HEADS), _split_heads(k, MLA_HEADS), _split_heads(_bdot(ckv, w_v), MLA_HEADS)


def _head_rms(x, g, head_mean):
    return x * lax.rsqrt(_hdot(x * x, head_mean) + EPS) * g


def _gqa_tile(r, p, a):
    g_q, g_k = p
    cos_t, sin_t, mean_q, mean_k = a
    wq, wk = GQA_HEADS * HEAD_DIM, GQA_KV_HEADS * HEAD_DIM
    b_q, b_k, b_v = _lanes(r[0], (0, wq, wq + wk, wq + 2 * wk))
    q = _rope(_head_rms(b_q, g_q, mean_q), cos_t, sin_t, HEAD_DIM // 4)
    k = _rope(_head_rms(b_k, g_k, mean_k), cos_t[:, :wk], sin_t[:, :wk], HEAD_DIM // 4)
    return _split_heads(q, GQA_HEADS), _split_heads(k, GQA_KV_HEADS), _split_heads(b_v, GQA_KV_HEADS)


def _permute_rows(p, x, cp):
    pb = p.astype(BF16)
    hi = x.astype(BF16)
    rest = x - hi.astype(F32)
    mid = rest.astype(BF16)
    low = (rest - mid.astype(F32)).astype(BF16)
    dims = (((cp,), (0,)), ((), ()))
    return (lax.dot_general(pb, hi, dims, preferred_element_type=F32)
            + lax.dot_general(pb, mid, dims, preferred_element_type=F32)
            + lax.dot_general(pb, low, dims, preferred_element_type=F32))


@jax.custom_vjp
def _permuted(p, x):
    return _permute_rows(p, x, 1)


def _permuted_fwd(p, x):
    return _permute_rows(p, x, 1), p


def _permuted_bwd(p, ct):
    return jnp.zeros_like(p), _permute_rows(p, ct, 0)


_permuted.defvjp(_permuted_fwd, _permuted_bwd)


def _interleave(p, x):
    return _permuted(p, x.reshape(x.shape[0] * x.shape[1], x.shape[2]))


def _interleave_matrix(rows, dil):
    p = np.zeros((rows, rows), np.float32)
    for t in range(rows):
        p[t, (t % dil) * (rows // dil) + t // dil] = 1.0
    return p


def _merge_tile(r, p, a):
    gm, o_a, o_b, oc0, oc1, oc2, l0, l1, l2, o_d = r
    (w_branch,) = p
    perm1, perm2 = a
    oc1, l1, oc2, l2 = _interleave(perm1, oc1), _interleave(perm1, l1), _interleave(perm2, oc2), _interleave(perm2, l2)
    d = w_branch.shape[2]
    gate_path, merge_logits = _lanes(gm, (0, N_BRANCH * BRANCH_W, N_BRANCH * BRANCH_W + N_BRANCH * d))
    m = jnp.maximum(jnp.maximum(l0, l1), l2)
    e0, e1, e2 = jnp.exp(l0 - m), jnp.exp(l1 - m), jnp.exp(l2 - m)
    y_c = (e0 * oc0 + e1 * oc1 + e2 * oc2) / (e0 + e1 + e2)
    y = jnp.concatenate([_join_heads(o_a), _join_heads(o_b), y_c, o_d], axis=1)
    u = y * (gate_path * jax.nn.sigmoid(gate_path))
    gates = _lanes(merge_logits, tuple(range(0, N_BRANCH * d + 1, d)))
    us = _lanes(u, tuple(range(0, N_BRANCH * BRANCH_W + 1, BRANCH_W)))
    branch_w = _unstack(w_branch)
    out = None
    for nb in range(N_BRANCH):
        term = jax.nn.sigmoid(gates[nb]) * _bdot(us[nb], branch_w[nb])
        out = term if out is None else out + term
    return (out,)


def _mixer_calls(proj, prm, aux):
    s = proj.shape[0]
    tr, tm = _pick(s, (512, 256, 128)), _pick(s, (256,))
    mla_cos, mla_sin, gqa_cos, gqa_sin, place_kr, mean_q, mean_k = aux[:7]
    wq = MLA_HEADS * MLA_QK
    mla = dict(
        steps=s // tr, rows=[(proj, _rows(tr, SMALL_W, MLA_BLK))],
        params=[prm["g_q"], prm["g_kv"], prm["w_q"], prm["w_k"], prm["w_v"]],
        aux=[(mla_cos, _rows(tr, wq)), (mla_sin, _rows(tr, wq)), (place_kr, _whole(place_kr.shape))],
        outs=[((MLA_HEADS, s, MLA_QK), _head_rows(MLA_HEADS, tr, MLA_QK))] * 2
        + [((MLA_HEADS, s, MLA_V), _head_rows(MLA_HEADS, tr, MLA_V))],
        window=((s, P_TOT), _rows(tr, SMALL_W, MLA_BLK), BF16))
    wg = GQA_HEADS * HEAD_DIM
    gqa = dict(
        steps=s // tr, rows=[(proj, _rows(tr, SMALL_W, GQA_BLK))], params=[prm["gq"], prm["gk"]],
        aux=[(gqa_cos, _rows(tr, wg)), (gqa_sin, _rows(tr, wg)), (mean_q, _whole(mean_q.shape)),
             (mean_k, _whole(mean_k.shape))],
        outs=[((GQA_HEADS, s, HEAD_DIM), _head_rows(GQA_HEADS, tr, HEAD_DIM))]
        + [((GQA_KV_HEADS, s, HEAD_DIM), _head_rows(GQA_KV_HEADS, tr, HEAD_DIM))] * 2,
        window=((s, P_TOT), _rows(tr, SMALL_W, GQA_BLK), BF16))
    merge = dict(steps=s // tm, tm=tm, window=((s, P_TOT), _rows(tm, GM_W, 0), BF16))
    return mla, gqa, merge


def _merge_rows(proj, o_a, o_b, ocs, lses, o_d, tm):
    h4 = _head_rows(4, tm, HEAD_DIM)
    s = proj.shape[0]

    def by_residue(t, dil):
        if dil == 1:
            return t, _rows(tm, QW)
        return t.reshape(dil, s // dil, QW), pl.BlockSpec((dil, tm // dil, QW), lambda i: (0, i, 0))

    dils = [dil for _, dil in DIL_PATTERNS]
    return ([(proj, _rows(tm, GM_W, 0)), (o_a, h4), (o_b, h4)] + [by_residue(t, r) for t, r in zip(ocs, dils)]
            + [by_residue(t, r) for t, r in zip(lses, dils)] + [(o_d, _rows(tm, QW))])


def _merge_aux(aux):
    return [(t, _whole(t.shape)) for t in aux[7:9]]


def _to_residues(t, dil):
    s, w = t.shape
    return t if dil == 1 else t.reshape(s // dil, dil, w).transpose(1, 0, 2).reshape(s, w)


def _from_residues(t, dil):
    s, w = t.shape
    return t if dil == 1 else t.reshape(dil, s // dil, w).transpose(1, 0, 2).reshape(s, w)


def _mixer_fwd(projs, prm, aux):
    proj = projs[0]
    s = proj.shape[0]
    mla, gqa, merge = _mixer_calls(proj, prm, aux)
    q_a, k_a, v_a = _fwd_call("prep_mla", _mla_tile, mla["steps"], mla["rows"], mla["params"], mla["aux"], mla["outs"])
    o_a, lse_a = _dense_fwd_call(q_a, k_a, v_a, MLA_QK ** -0.5, "mla")
    q_b, k_b, v_b = _fwd_call("prep_gqa", _gqa_tile, gqa["steps"], gqa["rows"], gqa["params"], gqa["aux"], gqa["outs"])
    grp = GQA_HEADS // GQA_KV_HEADS
    o_b, lse_b = _dense_fwd_call(q_b.reshape(GQA_KV_HEADS, grp * s, HEAD_DIM), k_b, v_b, HEAD_DIM ** -0.5, "gqa")
    scale = HEAD_DIM ** -0.5
    ocs, lses = [], []
    for gi, (_, dil) in enumerate(DIL_PATTERNS):
        o, lse = _band_fwd_call(projs[gi], DIL_BLK if gi == 0 else 0, prm["bias_dil"][gi], prm["no_sink"], dil, 1,
                                QW, scale, "dil%d" % gi)
        ocs.append(o)
        lses.append(lse)
    o_d, lse_d = _band_fwd_call(proj, WIN_BLK, prm["bias_win"], prm["sink"], 1, WIN_HEADS // WIN_KV_HEADS,
                                WIN_KV_HEADS * HEAD_DIM, scale, "win")
    rows = _merge_rows(proj, o_a, o_b.reshape(GQA_HEADS, s, HEAD_DIM), ocs, lses, o_d, merge["tm"])
    mix = _fwd_call("merge", _merge_tile, merge["steps"], rows, [prm["w_branch"]], _merge_aux(aux),
                    [((s, prm["w_branch"].shape[2]), _rows(merge["tm"], prm["w_branch"].shape[2]), BF16)])[0]
    return mix, (q_a, k_a, v_a, o_a, lse_a, q_b, k_b, v_b, o_b, lse_b, ocs, lses, o_d, lse_d)


def _mixer_bwd(projs, prm, aux, saved, dmix):
    proj = projs[0]
    s = proj.shape[0]
    q_a, k_a, v_a, o_a, lse_a, q_b, k_b, v_b, o_b, lse_b, ocs, lses, o_d, lse_d = saved
    dils = [dil for _, dil in DIL_PATTERNS]
    mla, gqa, merge = _mixer_calls(proj, prm, aux)
    tm, d_model = merge["tm"], prm["w_branch"].shape[2]
    grp = GQA_HEADS // GQA_KV_HEADS
    scale = HEAD_DIM ** -0.5

    rows = _merge_rows(proj, o_a, o_b.reshape(GQA_HEADS, s, HEAD_DIM), ocs, lses, o_d, tm)
    grads, (dw_branch,) = _vjp_call(
        "merge", _merge_tile, merge["steps"], rows, [prm["w_branch"]], _merge_aux(aux), [(dmix, _rows(tm, d_model))],
        [merge["window"]] + [(a.shape, spec) for a, spec in rows[1:]])
    dproj, do_a, do_b, docs, dlses, do_d = grads[0], grads[1], grads[2], grads[3:6], grads[6:9], grads[9]

    dq_a, dk_a, dv_a = _dense_bwd_call(q_a, k_a, v_a, o_a, lse_a, do_a, MLA_QK ** -0.5, "mla")
    (dproj,), dmla = _vjp_call("prep_mla", _mla_tile, mla["steps"], mla["rows"], mla["params"], mla["aux"],
                               [(t, spec) for t, (_, spec) in zip((dq_a, dk_a, dv_a), mla["outs"])],
                               [mla["window"]], into=dproj)
    dq_b, dk_b, dv_b = _dense_bwd_call(q_b.reshape(GQA_KV_HEADS, grp * s, HEAD_DIM), k_b, v_b, o_b, lse_b,
                                       do_b.reshape(GQA_KV_HEADS, grp * s, HEAD_DIM), scale, "gqa")
    (dproj,), dgqa = _vjp_call("prep_gqa", _gqa_tile, gqa["steps"], gqa["rows"], gqa["params"], gqa["aux"],
                               [(t, spec) for t, (_, spec) in zip((dq_b.reshape(GQA_HEADS, s, HEAD_DIM), dk_b, dv_b),
                                                                  gqa["outs"])],
                               [gqa["window"]], into=dproj)
    dproj, dbias_win, dsink = _band_bwd_call(proj, o_d, do_d, lse_d, None, prm["bias_win"], prm["sink"], dproj,
                                             WIN_BLK, 1, WIN_HEADS // WIN_KV_HEADS, WIN_KV_HEADS * HEAD_DIM, scale, "win")
    dbias_dil, dprojs = [], []
    for gi, dil in enumerate(dils):
        dside, dbias, _ = _band_bwd_call(
            projs[gi], ocs[gi], docs[gi].reshape(s, QW), lses[gi], dlses[gi].reshape(s, QW),
            prm["bias_dil"][gi], prm["no_sink"], dproj if gi == 0 else None, DIL_BLK if gi == 0 else 0, dil, 1, QW,
            scale, "dil%d" % gi)
        if gi == 0:
            dproj = dside
        else:
            dprojs.append(dside)
        dbias_dil.append(dbias)
    dprm = dict(g_q=dmla[0], g_kv=dmla[1], w_q=dmla[2], w_k=dmla[3], w_v=dmla[4], gq=dgqa[0], gk=dgqa[1],
                bias_dil=dbias_dil, bias_win=dbias_win, sink=dsink, no_sink=jnp.zeros_like(prm["no_sink"]),
                w_branch=dw_branch)
    return [dproj] + dprojs, {k: jax.tree.map(lambda g, p: g.astype(p.dtype), v, prm[k]) for k, v in dprm.items()}


def _layer_fwd(x, w, aux):
    s, d = x.shape
    tr = _pick(s, (256,))
    dils = [dil for _, dil in DIL_PATTERNS]

    def norm_forms(r, p, a):
        y = _rms(r[0], p[0])
        return [y, y.T] + [_dot(q, y, 1, 0).reshape(dil, tr // dil, d) for q, dil in zip(a, dils[1:])]

    forms = _fwd_call(
        "norm", norm_forms, s // tr, [(x, _rows(tr, d))], [w["norm_g"]], [(q, _whole(q.shape)) for q in aux[9:11]],
        [((s, d), _rows(tr, d), BF16), ((d, s), pl.BlockSpec((d, tr), lambda i: (0, i)), BF16)]
        + [((dil, s // dil, d), pl.BlockSpec((dil, tr // dil, d), lambda i: (0, i, 0)), BF16) for dil in dils[1:]])
    xn_t, xns = forms[1], [forms[0]] + [t.reshape(s, d) for t in forms[2:]]
    projs = [_mm(a, b, "nt", "proj%d_fwd" % i, BF16) for i, (a, b) in enumerate(zip(xns, w["w_in_t"]))]
    mix, saved = _mixer_fwd(projs, w["mixer"], aux)
    return _mm(mix, w["w_out"], "nn", "out_proj_nn"), (x, w, aux, xns, xn_t, projs, mix, saved)


@jax.custom_vjp
def _layer_core(x, w, aux):
    return _layer_fwd(x, w, aux)[0]


def _layer_core_bwd(res, dout):
    x, w, aux, xns, xn_t, projs, mix, saved = res
    s, d = x.shape
    tr = _pick(s, (256, 128, 64, 32, 16, 8))
    dils = [dil for _, dil in DIL_PATTERNS]
    dmix = _mm(dout, w["w_out"], "nt", "out_proj_nt")
    dw_out = _mm(mix, dout, "tn", "out_proj_tn", w["w_out"].dtype)
    dprojs, dmixer = _mixer_bwd(projs, w["mixer"], aux, saved, dmix)
    side = jnp.concatenate([_from_residues(dp, r) for dp, r in zip(dprojs[1:], dils[1:])], axis=1)
    dxn_terms = [_mm(dprojs[0], w["w_in_t"][0], "nn", "proj0_dx"),
                 _mm(side, jnp.concatenate(w["w_in_t"][1:], axis=0), "nn", "proj_side_dx")]
    dw_in_t = [_mm(xn_t, dprojs[0], "nn", "proj0_dw", w["w_in_t"][0].dtype).T]
    dw_in_t += [_mm(dp, a, "tn", "proj%d_dw" % i, wi.dtype)
                for i, (a, dp, wi) in list(enumerate(zip(xns, dprojs, w["w_in_t"])))[1:]]
    (dx,), (dg,) = _vjp_call("norm", lambda r, p, a: _norm_tile(r, p, a) * len(dxn_terms), s // tr, [(x, _rows(tr, d))],
                             [w["norm_g"]], [], [(t, _rows(tr, d)) for t in dxn_terms], [((s, d), _rows(tr, d))])
    dw = dict(norm_g=dg, w_in_t=dw_in_t, mixer=dmixer, w_out=dw_out)
    return dx, dw, tuple(jnp.zeros_like(t) for t in aux)


_layer_core.defvjp(lambda x, w, aux: _layer_fwd(x, w, aux), _layer_core_bwd)


def _rope_angles(pos, dim):
    inv = ROPE_THETA ** (-jnp.arange(0, dim, 2, dtype=F32) / dim)
    return pos.astype(F32)[:, None] * inv[None, :]


def _rope_tables(s):
    pos = jnp.arange(s, dtype=jnp.int32)
    rows = s // GRID_W
    row_idx = jnp.repeat(jnp.arange(rows, dtype=jnp.int32), GRID_W)
    col_idx = jnp.tile(jnp.arange(GRID_W, dtype=jnp.int32), rows)
    a1 = _rope_angles(pos, MLA_ROPE)
    ar = _rope_angles(row_idx, HEAD_DIM // 2)
    ac = _rope_angles(col_idx, HEAD_DIM // 2)
    ones, zeros = jnp.ones((s, MLA_NOPE), F32), jnp.zeros((s, MLA_NOPE), F32)
    mla_cos = jnp.tile(jnp.concatenate([ones, jnp.cos(a1), jnp.cos(a1)], axis=1), (1, MLA_HEADS))
    mla_sin = jnp.tile(jnp.concatenate([zeros, -jnp.sin(a1), jnp.sin(a1)], axis=1), (1, MLA_HEADS))
    gqa_cos = jnp.tile(jnp.concatenate([jnp.cos(ar), jnp.cos(ar), jnp.cos(ac), jnp.cos(ac)], axis=1), (1, GQA_HEADS))
    gqa_sin = jnp.tile(jnp.concatenate([-jnp.sin(ar), jnp.sin(ar), -jnp.sin(ac), jnp.sin(ac)], axis=1), (1, GQA_HEADS))
    return mla_cos, mla_sin, gqa_cos, gqa_sin


def _t5_bucket(rel):
    nb = T5_BUCKETS // 2
    max_exact = nb // 2
    n = jnp.abs(rel)
    nf = jnp.maximum(n, 1).astype(F32)
    large = max_exact + (jnp.log(nf / max_exact) / math.log(T5_MAX_DIST / max_exact) * (nb - max_exact)).astype(jnp.int32)
    large = jnp.minimum(large, nb - 1)
    return jnp.where(rel > 0, nb, 0) + jnp.where(n < max_exact, n, large)


def _band_bias(table, stride, head_lo, heads, half_window):
    b = BAND_BLOCK
    offs = jnp.arange(3 * b)[None, :] - b - jnp.arange(b)[:, None]
    one_hot = (_t5_bucket(offs * stride)[..., None] == jnp.arange(T5_BUCKETS)).astype(F32)
    bias = jnp.dot(one_hot.reshape(b * 3 * b, T5_BUCKETS), table[:, head_lo:head_lo + heads],
                   precision=lax.Precision.HIGHEST)
    bias = bias.T.reshape(heads, b, 3 * b)
    return jnp.where((jnp.abs(offs) <= half_window)[None], bias, NEG_INF)


def _w_in_rows(d):
    mla, gqa, win, dil0 = MLA_BLK * SMALL_W, GQA_BLK * SMALL_W, WIN_BLK * BAND_W, DIL_BLK * BAND_W
    plan, at = [], 0
    for width, target, row in ((256, 0, mla), (128, 0, mla + 256), (32, 0, mla + 384),
                               (256, 0, gqa), (128, 0, gqa + 256), (128, 0, gqa + 384)):
        plan.append((at, width, target, row))
        at += width
    for part in range(3):
        for g in range(len(DIL_PATTERNS)):
            plan.append((at, QW, g, (dil0 if g == 0 else 0) + part * QW))
            at += QW
    for width, row in ((256, win), (128, win + 256), (128, win + 384), (N_BRANCH * BRANCH_W, 0),
                       (N_BRANCH * d, N_BRANCH * BRANCH_W)):
        plan.append((at, width, 0, row))
        at += width
    return plan


@jax.custom_vjp
def _w_in_layout(w_in_t):
    d = w_in_t.shape[1]
    outs = []
    for target, rows in enumerate((P_TOT, BAND_W, BAND_W)):
        parts, at = [], 0
        for start, width, _, row in sorted((p for p in _w_in_rows(d) if p[2] == target), key=lambda p: p[3]):
            if row > at:
                parts.append(jnp.zeros((row - at, d), w_in_t.dtype))
            parts.append(w_in_t[start:start + width])
            at = row + width
        if at < rows:
            parts.append(jnp.zeros((rows - at, d), w_in_t.dtype))
        outs.append(jnp.concatenate(parts, axis=0))
    return outs


def _w_in_layout_fwd(w_in_t):
    return _w_in_layout(w_in_t), None


def _w_in_layout_bwd(_, cts):
    d = cts[0].shape[1]
    return (jnp.concatenate([cts[target][row:row + width] for _, width, target, row in _w_in_rows(d)], axis=0),)


_w_in_layout.defvjp(_w_in_layout_fwd, _w_in_layout_bwd)


def _layer(x, w, l, aux, biases):
    w_kv = w["w_kv_t"][l].T.reshape(MLA_KV_LORA, MLA_HEADS, MLA_NOPE + MLA_V)
    w_k = jnp.concatenate([w_kv[:, :, :MLA_NOPE], jnp.zeros((MLA_KV_LORA, MLA_HEADS, MLA_ROPE), w_kv.dtype)], axis=2)
    dil_bias, win_bias = biases
    prm = dict(
        g_q=w["mla_q_norm_g"][l][None, :], g_kv=w["mla_kv_norm_g"][l][None, :], w_q=w["w_q_t"][l].T,
        w_k=w_k.reshape(MLA_KV_LORA, MLA_HEADS * MLA_QK),
        w_v=w_kv[:, :, MLA_NOPE:].reshape(MLA_KV_LORA, MLA_HEADS * MLA_V),
        gq=jnp.tile(w["gqa_q_norm_g"][l], GQA_HEADS)[None, :], gk=jnp.tile(w["gqa_k_norm_g"][l], GQA_KV_HEADS)[None, :],
        bias_dil=list(dil_bias), bias_win=win_bias, sink=w["win_sink"][l].reshape(WIN_HEADS, 1, 1),
        no_sink=jnp.full((DIL_HEADS, 1, 1), NEG_INF, F32), w_branch=jnp.transpose(w["w_branch_t"][l].reshape(-1, N_BRANCH, BRANCH_W), (1, 2, 0)))
    layer_w = dict(norm_g=w["norm_g"][l][None, :], w_in_t=_w_in_layout(w["w_in_t"][l]), mixer=prm, w_out=w["w_out"][l])
    return x + _layer_core(x, layer_w, aux)


def _local_loss(w, x, target):
    s, d_model = x.shape
    assert d_model == D_MODEL, "the projection's window layout is laid out for d_model 1024"
    place = np.zeros((MLA_ROPE, MLA_HEADS * MLA_QK), np.float32)
    for h in range(MLA_HEADS):
        for i in range(MLA_ROPE):
            place[i, h * MLA_QK + MLA_NOPE + i] = 1.0

    def head_mean(nh):
        m = np.kron(np.eye(nh, dtype=np.float32), np.full((HEAD_DIM, HEAD_DIM), 1.0 / HEAD_DIM, np.float32))
        return jnp.asarray(m)

    merge_tile = _pick(s, (256,))
    norm_tile = _pick(s, (256,))
    aux = _rope_tables(s) + (jnp.asarray(place), head_mean(GQA_HEADS), head_mean(GQA_KV_HEADS)) + tuple(
        jnp.asarray(_interleave_matrix(merge_tile, dil)) for _, dil in DIL_PATTERNS[1:]) + tuple(
        jnp.asarray(_interleave_matrix(norm_tile, dil).T) for _, dil in DIL_PATTERNS[1:])
    table = w["t5_table"]
    dil_bias = [_band_bias(table, dil, gi * DIL_HEADS, DIL_HEADS, window // (2 * dil))
                for gi, (window, dil) in enumerate(DIL_PATTERNS)]
    win_bias = _band_bias(table, 1, len(DIL_PATTERNS) * DIL_HEADS, WIN_HEADS, WIN_HALF)
    for l in range(w["norm_g"].shape[0]):
        x = _layer(x, w, l, aux, (dil_bias, win_bias))
    return _loss_op(x, target, w["final_norm_g"][None, :])


_ANY = pl.BlockSpec(memory_space=pl.ANY)
_MESH = pl.DeviceIdType.MESH


def _all_gather(block, name):
    def body(x_ref, out_ref, send_sems, recv_sems, local_sem):
        x, y, c = lax.axis_index("x"), lax.axis_index("y"), lax.axis_index("c")
        me, sibling = (x, y, c), (x, y, 1 - c)
        chips = [(1 - x, y), (x, 1 - y), (1 - x, 1 - y)]

        def slot(px, py, pc):
            return out_ref.at[4 * px + 2 * py + pc]

        def copy(k, blk, to, src=None):
            return pltpu.make_async_remote_copy(
                src_ref=slot(*blk) if src is None else src, dst_ref=slot(*blk),
                send_sem=send_sems.at[k], recv_sem=recv_sems.at[k], device_id=to, device_id_type=_MESH)

        mine = pltpu.make_async_copy(x_ref, slot(*me), local_sem)
        mine.start()
        first = [copy(0, me, sibling, src=x_ref)]
        first += [copy(1 + j, me, (*chip, c), src=x_ref) for j, chip in enumerate(chips)]
        for cp in first:
            cp.start()
        passed = [copy(4 + j, (*chip, c), sibling) for j, chip in enumerate(chips)]
        for j, chip in enumerate(chips):
            copy(1 + j, (*chip, c), me).wait_recv()
            passed[j].start()
        copy(0, sibling, me).wait_recv()
        for j, chip in enumerate(chips):
            copy(4 + j, (*chip, 1 - c), me).wait_recv()
        for cp in first + passed:
            cp.wait_send()
        mine.wait()

    return pl.pallas_call(
        body,
        out_shape=jax.ShapeDtypeStruct((N_DEV,) + block.shape, block.dtype),
        in_specs=[_ANY],
        out_specs=_ANY,
        scratch_shapes=[pltpu.SemaphoreType.DMA((7,)), pltpu.SemaphoreType.DMA((7,)), pltpu.SemaphoreType.DMA],
        name=name,
    )(block)


def _swap_with_sibling(blocks, name):
    chips = blocks.shape[0]

    def body(x_ref, out_ref, send_sems, recv_sems):
        x, y, c = lax.axis_index("x"), lax.axis_index("y"), lax.axis_index("c")
        copies = [pltpu.make_async_remote_copy(
            src_ref=x_ref.at[k, 1 - c], dst_ref=out_ref.at[k], send_sem=send_sems.at[k], recv_sem=recv_sems.at[k],
            device_id=(x, y, 1 - c), device_id_type=_MESH) for k in range(chips)]
        for cp in copies:
            cp.start()
        for cp in copies:
            cp.wait()

    return pl.pallas_call(
        body,
        out_shape=jax.ShapeDtypeStruct((chips,) + blocks.shape[2:], blocks.dtype),
        in_specs=[_ANY],
        out_specs=_ANY,
        scratch_shapes=[pltpu.SemaphoreType.DMA((chips,)), pltpu.SemaphoreType.DMA((chips,))],
        name=name,
    )(blocks)


def _add_sibling(blocks, theirs, name):
    chips, _, rows, w = blocks.shape
    tr = _row_tile(rows, 16, 4096)

    def body(b_ref, t_ref, o_ref):
        mine = b_ref[0, lax.axis_index("c")]
        o_ref[0] = (mine.astype(F32) + t_ref[0].astype(F32)).astype(o_ref.dtype)

    return pl.pallas_call(
        body,
        grid=(chips, rows // tr),
        in_specs=[pl.BlockSpec((1, 2, tr, w), lambda k, i: (k, 0, i, 0)), pl.BlockSpec((1, tr, w), lambda k, i: (k, i, 0))],
        out_specs=pl.BlockSpec((1, tr, w), lambda k, i: (k, i, 0)),
        out_shape=jax.ShapeDtypeStruct(theirs.shape, theirs.dtype),
        name=name,
        compiler_params=_params("parallel", "parallel"),
    )(blocks, theirs)


def _exchange_chips(partials, name):
    n_chips = partials.shape[0]

    def body(x_ref, out_ref, send_sems, recv_sems, local_sem):
        x, y, c = lax.axis_index("x"), lax.axis_index("y"), lax.axis_index("c")
        me = 2 * x + y
        mine = pltpu.make_async_copy(x_ref.at[me], out_ref.at[me], local_sem)
        mine.start()
        copies, landed = [], []
        for k in range(1, n_chips):
            px = 1 - x if k & 2 else x
            py = 1 - y if k & 1 else y
            peer = 2 * px + py
            copies.append(pltpu.make_async_remote_copy(
                src_ref=x_ref.at[peer], dst_ref=out_ref.at[me], send_sem=send_sems.at[k - 1],
                recv_sem=recv_sems.at[k - 1], device_id=(px, py, c), device_id_type=_MESH))
            landed.append(pltpu.make_async_remote_copy(
                src_ref=x_ref.at[peer], dst_ref=out_ref.at[peer], send_sem=send_sems.at[k - 1],
                recv_sem=recv_sems.at[k - 1], device_id=(px, py, c), device_id_type=_MESH))
        for cp in copies:
            cp.start()
        for cp in landed:
            cp.wait_recv()
        for cp in copies:
            cp.wait_send()
        mine.wait()

    return pl.pallas_call(
        body,
        out_shape=jax.ShapeDtypeStruct(partials.shape, partials.dtype),
        in_specs=[_ANY],
        out_specs=_ANY,
        scratch_shapes=[pltpu.SemaphoreType.DMA((n_chips - 1,)), pltpu.SemaphoreType.DMA((n_chips - 1,)),
                        pltpu.SemaphoreType.DMA],
        name=name,
    )(partials)


def _sum_slots(parts, name):
    slots, rows, w = parts.shape
    tr = _row_tile(rows, 16 if parts.dtype == BF16 else 8, 4096)

    def body(p_ref, o_ref):
        acc = p_ref[0].astype(F32)
        for j in range(1, slots):
            acc = acc + p_ref[j].astype(F32)
        o_ref[...] = acc

    return pl.pallas_call(
        body,
        grid=(rows // tr,),
        in_specs=[pl.BlockSpec((slots, tr, w), lambda i: (0, i, 0))],
        out_specs=pl.BlockSpec((tr, w), lambda i: (i, 0)),
        out_shape=jax.ShapeDtypeStruct((rows, w), F32),
        name=name,
        compiler_params=_params("parallel"),
    )(parts)


def _adamw(w, g, m, v, name):
    rows, width = w.shape
    tr = _row_tile(rows, 8, 2048)

    def body(w_ref, g_ref, m_ref, v_ref, d_ref, nm_ref, nv_ref):
        g_ = g_ref[...]
        m_ = ADAM_B1 * m_ref[...] + (1.0 - ADAM_B1) * g_
        v_ = ADAM_B2 * v_ref[...] + (1.0 - ADAM_B2) * jnp.square(g_)
        m_hat = m_ / (1.0 - ADAM_B1 ** ADAM_STEP)
        v_hat = v_ / (1.0 - ADAM_B2 ** ADAM_STEP)
        d_ref[...] = -ADAM_LR * (m_hat / (jnp.sqrt(v_hat) + ADAM_EPS) + ADAM_WD * w_ref[...])
        nm_ref[...] = m_
        nv_ref[...] = v_

    spec = pl.BlockSpec((tr, width), lambda i: (i, 0))
    return pl.pallas_call(
        body,
        grid=(rows // tr,),
        in_specs=[spec] * 4,
        out_specs=[spec] * 3,
        out_shape=[jax.ShapeDtypeStruct((rows, width), F32)] * 3,
        name=name,
        compiler_params=_params("parallel"),
    )(w, g, m, v)


_SHARDED = (("w_in", 2), ("w_mla_q_up", 2), ("w_mla_kv_up", 2), ("w_branch", 3), ("w_out", 1))
_REPLICATED = ("norm_g", "mla_q_norm_g", "mla_kv_norm_g", "gqa_q_norm_g", "gqa_k_norm_g", "win_sink", "t5_table",
               "final_norm_g")


def _pack(arrays, row_multiple):
    flat = jnp.concatenate([a.reshape(-1) for a in arrays])
    rows = -(-flat.shape[0] // (LANES * row_multiple)) * row_multiple
    return jnp.pad(flat, (0, rows * LANES - flat.shape[0])).reshape(rows, LANES)


def _unpack(packed, shapes):
    flat, out, at = packed.reshape(-1), [], 0
    for shp in shapes:
        n = int(np.prod(shp))
        out.append(flat[at:at + n].reshape(shp))
        at += n
    return out


_TO_WIRE = {
    "w_in": lambda t: jnp.swapaxes(t, 1, 2), "w_mla_q_up": lambda t: jnp.swapaxes(t, 1, 2),
    "w_mla_kv_up": lambda t: jnp.swapaxes(t, 1, 2),
    "w_branch": lambda t: jnp.transpose(t, (0, 3, 1, 2)).reshape(t.shape[0], t.shape[3], -1), "w_out": lambda t: t}
_FROM_WIRE = {
    "w_in": lambda t, shp: jnp.swapaxes(t, 1, 2), "w_mla_q_up": lambda t, shp: jnp.swapaxes(t, 1, 2),
    "w_mla_kv_up": lambda t, shp: jnp.swapaxes(t, 1, 2),
    "w_branch": lambda t, shp: jnp.transpose(t.reshape(shp[0], shp[3], shp[1], shp[2]), (0, 2, 3, 1)),
    "w_out": lambda t, shp: t}
_WIRE_NAME = {"w_in": "w_in_t", "w_mla_q_up": "w_q_t", "w_mla_kv_up": "w_kv_t", "w_branch": "w_branch_t",
              "w_out": "w_out"}


def _transpose_blocks(t, dtype, name):
    depth, a, b = t.shape

    def body(x_ref, o_ref):
        o_ref[0] = x_ref[0].T.astype(o_ref.dtype)

    return pl.pallas_call(
        body,
        grid=(depth,),
        in_specs=[pl.BlockSpec((1, a, b), lambda i: (i, 0, 0))],
        out_specs=pl.BlockSpec((1, b, a), lambda i: (i, 0, 0)),
        out_shape=jax.ShapeDtypeStruct((depth, b, a), dtype),
        name=name,
        compiler_params=_params("parallel"),
    )(t)


def _join_shards(gathered, wire_shapes):
    out, at = [], 0
    for depth, cut, rest in wire_shapes:
        n = depth * cut * rest // LANES
        blk = gathered[:, at:at + n].reshape(N_DEV, depth, cut, rest)
        out.append(jnp.moveaxis(blk, 0, 1).reshape(depth, N_DEV * cut, rest))
        at += n
    return out


def _split_shards(fulls, wire_shapes):
    parts = []
    for full, (depth, cut, rest) in zip(fulls, wire_shapes):
        blk = jnp.moveaxis(full.reshape(depth, N_DEV, cut, rest), 1, 0)
        parts.append(blk.reshape(N_DEV, depth * cut * rest // LANES, LANES))
    packed = jnp.concatenate(parts, axis=1)
    return packed.reshape((N_DEV // 2, 2) + packed.shape[1:])


def kernel(x, norm_g, w_in, mla_q_norm_g, mla_kv_norm_g, w_mla_q_up, w_mla_kv_up, gqa_q_norm_g, gqa_k_norm_g, win_sink, t5_table, w_branch, w_out, final_norm_g, loss_target, m_norm_g, m_w_in, m_mla_q_norm_g, m_mla_kv_norm_g, m_w_mla_q_up, m_w_mla_kv_up, m_gqa_q_norm_g, m_gqa_k_norm_g, m_win_sink, m_t5_table, m_w_branch, m_w_out, m_final_norm_g, v_norm_g, v_w_in, v_mla_q_norm_g, v_mla_kv_norm_g, v_w_mla_q_up, v_w_mla_kv_up, v_gqa_q_norm_g, v_gqa_k_norm_g, v_win_sink, v_t5_table, v_w_branch, v_w_out, v_final_norm_g):
    given = dict(locals())
    names = ("norm_g", "w_in", "mla_q_norm_g", "mla_kv_norm_g", "w_mla_q_up", "w_mla_kv_up", "gqa_q_norm_g",
             "gqa_k_norm_g", "win_sink", "t5_table", "w_branch", "w_out", "final_norm_g")
    shard_names = [n for n, _ in _SHARDED]
    shard_shapes = [given[n].shape for n in shard_names]

    wire = [_transpose_blocks(given[n], BF16, "w_in_to_wire") if n == "w_in" else _TO_WIRE[n](given[n]).astype(BF16)
            for n in shard_names]
    wire_shapes = [t.shape for t in wire]
    gathered = _all_gather(jnp.concatenate([t.reshape(-1, LANES) for t in wire]), "gather_weights")
    weights = {n: given[n] for n in _REPLICATED}
    weights.update(zip([_WIRE_NAME[n] for n in shard_names], _join_shards(gathered, wire_shapes)))

    loss, (gw, gx) = jax.value_and_grad(_local_loss, argnums=(0, 1))(weights, x[0], loss_target[0])
    loss = lax.psum(loss, ("x", "y", "c"))

    send = _split_shards([gw[_WIRE_NAME[n]] for n in shard_names], wire_shapes)
    partials = _add_sibling(send, _swap_with_sibling(send, "swap_grads"), "add_sibling_grads")
    g_wire = _unpack(_sum_slots(_exchange_chips(partials, "scatter_grads"), "sum_grads"), wire_shapes)
    g_shard = [_transpose_blocks(t, F32, "w_in_from_wire") if n == "w_in" else _FROM_WIRE[n](t, shp)
               for n, t, shp in zip(shard_names, g_wire, shard_shapes)]
    rep_shapes = [given[n].shape for n in _REPLICATED]
    g_rep = _unpack(_sum_slots(_all_gather(_pack([gw[n] for n in _REPLICATED], 8), "gather_small_grads"),
                               "sum_small_grads"), rep_shapes)
    grads = dict(zip(shard_names, g_shard))
    grads.update(zip(_REPLICATED, g_rep))

    def update(group, shapes, row_multiple, name):
        outs = _adamw(*[_pack([src[n] for n in group], row_multiple) for src in (
            given, grads, {n: given["m_" + n] for n in group}, {n: given["v_" + n] for n in group})], name)
        return [dict(zip(group, _unpack(o, shapes))) for o in outs]

    big = update(shard_names, shard_shapes, 16, "adamw_shards")
    small = update(list(_REPLICATED), rep_shapes, 8, "adamw_replicated")
    delta, new_m, new_v = [{**b, **s_} for b, s_ in zip(big, small)]
    return (loss, gx[None], *[grads[n] for n in names], *[delta[n] for n in names],
            *[new_m[n] for n in names], *[new_v[n] for n in names])
```

```python
import functools
import math

import jax
import jax.numpy as jnp
import numpy as np
from jax import lax
from jax.experimental import pallas as pl
from jax.experimental.pallas import tpu as pltpu

F32 = jnp.float32
BF16 = jnp.bfloat16
N_DEV = 8
LANES = 128
HALF = LANES // 2
V7X_VMEM_LIMIT = 56 * 1024 * 1024

EPS = 1e-6
NEG_INF = -1e30
LOG2E = 1.4426950408889634
ROPE_THETA = 10000.0
GRID_W = 64
HEAD_DIM = 64
N_BRANCH = 4
BRANCH_W = 256
MLA_HEADS, MLA_Q_LORA, MLA_KV_LORA, MLA_NOPE, MLA_ROPE, MLA_V = 4, 256, 128, 64, 32, 64
MLA_QK = MLA_NOPE + MLA_ROPE
GQA_HEADS, GQA_KV_HEADS = 4, 2
DIL_PATTERNS = ((128, 1), (512, 4), (2048, 16))
DIL_HEADS = 4
WIN_HEADS, WIN_KV_HEADS, WIN_HALF = 4, 2, 128
T5_BUCKETS, T5_MAX_DIST = 32, 1024
BAND_BLOCK = 128
ADAM_LR, ADAM_B1, ADAM_B2, ADAM_EPS, ADAM_WD, ADAM_STEP = 0.001, 0.9, 0.999, 1e-08, 0.01, 10

D_MODEL = 1024
GM_W, SMALL_W, BAND_W = 5120, 512, 768
MLA_BLK, GQA_BLK, WIN_BLK, DIL_BLK = 10, 11, 8, 9
P_TOT = 7680
QW = 256


def _params(*sem):
    return pltpu.CompilerParams(dimension_semantics=sem, vmem_limit_bytes=V7X_VMEM_LIMIT)


def _pick(n, cands):
    for c in cands:
        if n % c == 0:
            return c
    return n


def _row_tile(rows, unit, cap):
    best = unit
    for t in range(unit, min(rows, cap) + 1, unit):
        if rows % t == 0:
            best = t
    assert rows % best == 0
    return best


def _dot(a, b, ca, cb):
    return lax.dot_general(a.astype(BF16), b.astype(BF16), (((ca,), (cb,)), ((), ())), preferred_element_type=F32)


def _bmm(a, b, ca, cb):
    return lax.dot_general(a, b, (((ca,), (cb,)), ((0,), (0,))), preferred_element_type=F32)


@jax.custom_vjp
def _bdot(a, b):
    return _dot(a, b, 1, 0)


def _bdot_fwd(a, b):
    return _dot(a, b, 1, 0), (a, b)


def _bdot_bwd(res, g):
    a, b = res
    return _dot(g, b, 1, 1), _dot(a, g, 0, 0)


_bdot.defvjp(_bdot_fwd, _bdot_bwd)


def _hdot(a, c):
    return lax.dot_general(a, c, (((1,), (0,)), ((), ())), precision=lax.Precision.HIGHEST, preferred_element_type=F32)


@functools.partial(jax.custom_vjp, nondiff_argnums=(1,))
def _lane_roll(x, shift):
    return pltpu.roll(x, shift, 1)


def _lane_roll_fwd(x, shift):
    return pltpu.roll(x, shift, 1), None


def _lane_roll_bwd(shift, _, g):
    return (pltpu.roll(g, g.shape[1] - shift, 1),)


_lane_roll.defvjp(_lane_roll_fwd, _lane_roll_bwd)


@functools.partial(jax.custom_vjp, nondiff_argnums=(1,))
def _lane_ranges(x, cut):
    bounds, _ = cut
    return tuple(x[:, lo:hi] for lo, hi in zip(bounds[:-1], bounds[1:]))


def _lane_ranges_fwd(x, cut):
    return _lane_ranges(x, cut), None


def _lane_ranges_bwd(cut, _, cts):
    bounds, width = cut
    parts = list(cts)
    if bounds[-1] < width:
        parts.append(jnp.zeros((cts[0].shape[0], width - bounds[-1]), cts[0].dtype))
    return (jnp.concatenate(parts, axis=1),)


_lane_ranges.defvjp(_lane_ranges_fwd, _lane_ranges_bwd)


def _lanes(x, bounds):
    return _lane_ranges(x, (tuple(bounds), x.shape[1]))


@jax.custom_vjp
def _unstack(x):
    return tuple(x[i] for i in range(x.shape[0]))


def _unstack_fwd(x):
    return _unstack(x), None


def _unstack_bwd(_, cts):
    return (jnp.stack(cts, axis=0),)


_unstack.defvjp(_unstack_fwd, _unstack_bwd)


@functools.partial(jax.custom_vjp, nondiff_argnums=(1,))
def _split_heads(x, h):
    d = x.shape[1] // h
    return jnp.stack([x[:, i * d:(i + 1) * d] for i in range(h)], axis=0)


def _split_heads_fwd(x, h):
    return _split_heads(x, h), None


def _split_heads_bwd(h, _, ct):
    return (jnp.concatenate([ct[i] for i in range(h)], axis=1),)


_split_heads.defvjp(_split_heads_fwd, _split_heads_bwd)


def _join_heads(x):
    return jnp.concatenate(_unstack(x), axis=1)


def _rope(x, cos_t, sin_t, half):
    w = x.shape[1]
    lane = lax.broadcasted_iota(jnp.int32, (1, w), 1)
    first = (lane % (2 * half)) < half
    partner = jnp.where(first, _lane_roll(x, w - half), _lane_roll(x, half))
    return x * cos_t + partner * sin_t


def _rms(x, g):
    return x * lax.rsqrt(jnp.mean(x * x, axis=-1, keepdims=True) + EPS) * g


def _rows(tr, w, col=0):
    return pl.BlockSpec((tr, w), lambda i: (i, col))


def _head_rows(h, tr, d):
    return pl.BlockSpec((h, tr, d), lambda i: (0, i, 0))


def _whole(shape):
    nd = len(shape)
    return pl.BlockSpec(tuple(shape), lambda i: (0,) * nd)


def _fwd_call(name, fn, steps, rows, params, aux, outs):
    nr, npar, na = len(rows), len(params), len(aux)

    def body(*refs):
        vals = [x[...].astype(F32) for x in refs[:nr + npar + na]]
        res = fn(vals[:nr], vals[nr:nr + npar], vals[nr + npar:])
        for o_ref, o in zip(refs[nr + npar + na:], res):
            o_ref[...] = o.astype(o_ref.dtype)

    return pl.pallas_call(
        body,
        grid=(steps,),
        in_specs=[s for _, s in rows] + [_whole(p.shape) for p in params] + [s for _, s in aux],
        out_specs=[e[1] for e in outs],
        out_shape=[jax.ShapeDtypeStruct(e[0], e[2] if len(e) > 2 else F32) for e in outs],
        name=name + "_fwd",
        compiler_params=_params("parallel"),
    )(*[a for a, _ in rows], *params, *[a for a, _ in aux])


def _vjp_call(name, fn, steps, rows, params, aux, cts, row_grads, into=None):
    nr, npar, na, nc = len(rows), len(params), len(aux), len(cts)
    n_in = nr + npar + na + nc
    lead = 0 if into is None else 1

    def body(*refs):
        refs = refs[lead:]
        vals = [x[...].astype(F32) for x in refs[:n_in]]
        r, p, a, d = vals[:nr], vals[nr:nr + npar], vals[nr + npar:nr + npar + na], vals[nr + npar + na:]
        out_refs = refs[n_in:]
        _, vjp = jax.vjp(lambda r_, p_: tuple(fn(r_, p_, a)), r, p)
        dr, dp = vjp(tuple(d))
        for o_ref, o in zip(out_refs[:nr], dr):
            o_ref[...] = o.astype(o_ref.dtype)

        @pl.when(pl.program_id(0) == 0)
        def _():
            for o_ref in out_refs[nr:]:
                o_ref[...] = jnp.zeros_like(o_ref)

        for o_ref, o in zip(out_refs[nr:], dp):
            o_ref[...] += o

    outs = pl.pallas_call(
        body,
        grid=(steps,),
        in_specs=([] if into is None else [pl.BlockSpec(memory_space=pl.ANY)])
        + [s for _, s in rows] + [_whole(p.shape) for p in params] + [s for _, s in aux] + [s for _, s in cts],
        out_specs=[e[1] for e in row_grads] + [_whole(p.shape) for p in params],
        out_shape=[jax.ShapeDtypeStruct(e[0], e[2] if len(e) > 2 else F32) for e in row_grads]
        + [jax.ShapeDtypeStruct(p.shape, F32) for p in params],
        input_output_aliases={} if into is None else {0: 0},
        name=name + "_bwd",
        compiler_params=_params("arbitrary"),
    )(*([] if into is None else [into]), *[a for a, _ in rows], *params, *[a for a, _ in aux], *[a for a, _ in cts])
    return list(outs[:nr]), list(outs[nr:])


def _norm_tile(r, p, a):
    return (_rms(r[0], p[0]),)


def _mm(a, b, mode, name, out_dtype=F32):
    if mode == "nn":
        (m, k), n = a.shape, b.shape[1]
    elif mode == "nt":
        (m, k), n = a.shape, b.shape[0]
    else:
        (k, m), n = a.shape, b.shape[1]
    tn = _pick(n, (1024, 768, 512, 384, 256, 128))
    budget = V7X_VMEM_LIMIT * 3 // 4
    out_bytes = 4 + 2 * np.dtype(out_dtype).itemsize

    def tiles():
        for tm in (2048, 1024, 512, 256, 128):
            for tk in (512, 256, 128) if mode == "tn" else (4096, 1024, 768, 512, 384, 256, 128):
                need = 2 * tk * (tm * a.dtype.itemsize + tn * b.dtype.itemsize) + tm * tn * out_bytes
                if m % tm == 0 and k % tk == 0 and need <= budget:
                    return tm, tk
        return _pick(m, (128,)), _pick(k, (128,))

    tm, tk = tiles()
    nk = k // tk

    def body(*refs):
        a_ref, b_ref, o_ref, acc_ref = refs
        kk = pl.program_id(2)
        if mode == "nn":
            part = _dot(a_ref[...], b_ref[...], 1, 0)
        elif mode == "nt":
            part = _dot(a_ref[...], b_ref[...], 1, 1)
        else:
            part = _dot(a_ref[...], b_ref[...], 0, 0)
        if nk == 1:
            o_ref[...] = part.astype(o_ref.dtype)
        else:
            @pl.when(kk == 0)
            def _():
                acc_ref[...] = part

            @pl.when(kk > 0)
            def _():
                acc_ref[...] += part

            @pl.when(kk == nk - 1)
            def _():
                o_ref[...] = acc_ref[...].astype(o_ref.dtype)

    if mode == "nn":
        a_spec = pl.BlockSpec((tm, tk), lambda i, j, kk: (i, kk))
        b_spec = pl.BlockSpec((tk, tn), lambda i, j, kk: (kk, j))
    elif mode == "nt":
        a_spec = pl.BlockSpec((tm, tk), lambda i, j, kk: (i, kk))
        b_spec = pl.BlockSpec((tn, tk), lambda i, j, kk: (j, kk))
    else:
        a_spec = pl.BlockSpec((tk, tm), lambda i, j, kk: (kk, i))
        b_spec = pl.BlockSpec((tk, tn), lambda i, j, kk: (kk, j))
    o_spec = pl.BlockSpec((tm, tn), lambda i, j, kk: (i, j))
    return pl.pallas_call(
        body,
        grid=(m // tm, n // tn, nk),
        in_specs=[a_spec, b_spec],
        out_specs=o_spec,
        out_shape=jax.ShapeDtypeStruct((m, n), out_dtype),
        scratch_shapes=[pltpu.VMEM((tm, tn), F32)],
        name=name,
        compiler_params=_params("parallel", "parallel", "arbitrary"),
    )(a, b)


def _dense_fwd_call(q, k, v, scale, name):
    n, sq, d = q.shape
    sk, dv = k.shape[1], v.shape[2]
    tq = _pick(sq, (512, 256, 128))
    c = scale * LOG2E

    nkb = 1

    def body(q_ref, k_ref, v_ref, o_ref, lse_ref, m_s, acc_s, vext_s):
        j = pl.program_id(2)

        @pl.when(j == 0)
        def _():
            m_s[...] = jnp.full_like(m_s, NEG_INF)
            acc_s[...] = jnp.zeros_like(acc_s)
            vext_s[...] = jnp.ones_like(vext_s)

        vext_s[:, :dv] = v_ref[0].astype(BF16)
        m_old = m_s[...]
        s = _dot(q_ref[0], k_ref[0], 1, 1)
        m_new = jnp.maximum(m_old, jnp.max(s, axis=1, keepdims=True))
        p = jnp.exp2(s * c - m_new * c)
        acc = jnp.exp2((m_old - m_new) * c) * acc_s[...] + _dot(p, vext_s[...], 1, 0)
        m_s[...] = m_new
        acc_s[...] = acc

        @pl.when(j == nkb - 1)
        def _():
            l = acc[:, dv:dv + 1]
            o_ref[0] = acc[:, :dv] / l
            lse_ref[0] = m_new * scale + jnp.log(l)

    return pl.pallas_call(
        body,
        grid=(n, sq // tq, nkb),
        in_specs=[
            pl.BlockSpec((1, tq, d), lambda h, i, j: (h, i, 0)),
            pl.BlockSpec((1, sk // nkb, d), lambda h, i, j: (h, j, 0)),
            pl.BlockSpec((1, sk // nkb, dv), lambda h, i, j: (h, j, 0)),
        ],
        out_specs=[
            pl.BlockSpec((1, tq, dv), lambda h, i, j: (h, i, 0)),
            pl.BlockSpec((1, tq, 1), lambda h, i, j: (h, i, 0)),
        ],
        out_shape=[jax.ShapeDtypeStruct((n, sq, dv), F32), jax.ShapeDtypeStruct((n, sq, 1), F32)],
        scratch_shapes=[pltpu.VMEM((tq, 1), F32), pltpu.VMEM((tq, 2 * dv), F32), pltpu.VMEM((sk // nkb, 2 * dv), BF16)],
        name=name + "_fwd",
        compiler_params=_params("parallel", "parallel", "arbitrary"),
    )(q, k, v)


def _dense_bwd_call(q, k, v, o, lse, do, scale, name):
    n, sq, d = q.shape
    sk, dv = k.shape[1], v.shape[2]
    tq, tk = _pick(sq, (1024, 512, 256, 128)), _pick(sk, (2048, 1024, 512, 256, 128))
    c = scale * LOG2E

    def body(q_ref, k_ref, v_ref, o_ref, lse_ref, do_ref, dq_ref, dk_ref, dv_ref):
        j, i = pl.program_id(1), pl.program_id(2)
        qb, kb, vb = q_ref[0].astype(BF16), k_ref[0].astype(BF16), v_ref[0].astype(BF16)
        do_f = do_ref[0]
        dob = do_f.astype(BF16)
        p = jnp.exp2(_dot(qb, kb, 1, 1) * c - lse_ref[0] * LOG2E)
        delta = jnp.sum(do_f * o_ref[0], axis=1, keepdims=True)
        ds = (p * (_dot(dob, vb, 1, 1) - delta)).astype(BF16)
        dv_part = _dot(p, dob, 0, 0)
        dk_part = _dot(ds, qb, 0, 0) * scale
        dq_part = _dot(ds, kb, 1, 0) * scale
        rows = pl.ds(pl.multiple_of(i * tq, tq), tq)

        @pl.when(i == 0)
        def _():
            dk_ref[0] = dk_part
            dv_ref[0] = dv_part

        @pl.when(i > 0)
        def _():
            dk_ref[0] += dk_part
            dv_ref[0] += dv_part

        @pl.when(j == 0)
        def _():
            dq_ref[0, rows, :] = dq_part

        @pl.when(j > 0)
        def _():
            dq_ref[0, rows, :] += dq_part

    return pl.pallas_call(
        body,
        grid=(n, sk // tk, sq // tq),
        in_specs=[
            pl.BlockSpec((1, tq, d), lambda h, j, i: (h, i, 0)),
            pl.BlockSpec((1, tk, d), lambda h, j, i: (h, j, 0)),
            pl.BlockSpec((1, tk, dv), lambda h, j, i: (h, j, 0)),
            pl.BlockSpec((1, tq, dv), lambda h, j, i: (h, i, 0)),
            pl.BlockSpec((1, tq, 1), lambda h, j, i: (h, i, 0)),
            pl.BlockSpec((1, tq, dv), lambda h, j, i: (h, i, 0)),
        ],
        out_specs=[
            pl.BlockSpec((1, sq, d), lambda h, j, i: (h, 0, 0)),
            pl.BlockSpec((1, tk, d), lambda h, j, i: (h, j, 0)),
            pl.BlockSpec((1, tk, dv), lambda h, j, i: (h, j, 0)),
        ],
        out_shape=[
            jax.ShapeDtypeStruct((n, sq, d), F32),
            jax.ShapeDtypeStruct((n, sk, d), F32),
            jax.ShapeDtypeStruct((n, sk, dv), F32),
        ],
        name=name + "_bwd",
        compiler_params=_params("arbitrary", "arbitrary", "arbitrary"),
    )(q, k, v, o, lse, do)


def _head_geometry(h, group):
    pair, a = divmod(h, 2)
    kv_pair, b = divmod(h // group, 2)
    return pair, a, kv_pair, b


def _lane_half():
    return lax.broadcasted_iota(jnp.int32, (1, LANES), 1) // HALF


def _align(x, a, b):
    if a != b:
        x = pltpu.roll(x, HALF, 1)
    return jnp.where(_lane_half() == b, x, 0.0)


def _unalign(x, a, b):
    x = jnp.where(_lane_half() == b, x, 0.0)
    return pltpu.roll(x, HALF, 1) if a != b else x


def _bands(w, pw, nw, lo, kvw, nb):
    b = BAND_BLOCK
    cat = jnp.concatenate([pw[:, lo:lo + kvw], w[:, lo:lo + kvw], nw[:, lo:lo + kvw]], axis=0).astype(BF16)
    out = []
    for g in range(kvw // LANES):
        c3 = cat[:, g * LANES:(g + 1) * LANES].reshape(nb + 2, b, LANES)
        out.append(jnp.concatenate([c3[0:nb], c3[1:nb + 1], c3[2:nb + 2]], axis=1))
    return out


def _edge_mask(first_block, nb, period):
    b = BAND_BLOCK
    blk = (first_block + lax.broadcasted_iota(jnp.int32, (nb, 1, 3 * b), 0)) % period
    col = lax.broadcasted_iota(jnp.int32, (nb, 1, 3 * b), 2)
    outside = ((col < b) & (blk == 0)) | ((col >= 2 * b) & (blk == period - 1))
    return jnp.where(outside, NEG_INF, 0.0)


def _band_geometry(proj, dil):
    rows = proj.shape[0]
    tl = _pick(rows, (1024, 512, 256, 128))
    return rows, tl, tl // BAND_BLOCK, rows // tl, rows // dil // BAND_BLOCK


def _band_in_specs(tl, nb, n_chunks, n_blocks, col, last_step_idle):
    def chunk(i):
        return jnp.minimum(i, n_chunks - 1) if last_step_idle else i

    main = pl.BlockSpec((tl, BAND_W), lambda j, i: (j * n_chunks + chunk(i), col))
    prev = pl.BlockSpec((BAND_BLOCK, BAND_W),
                        lambda j, i: (j * n_blocks + jnp.maximum(chunk(i) * nb - 1, 0), col))
    nxt = pl.BlockSpec((BAND_BLOCK, BAND_W),
                       lambda j, i: (j * n_blocks + jnp.minimum((chunk(i) + 1) * nb, n_blocks - 1), col))
    rows = pl.BlockSpec((tl, QW), lambda j, i: (j * n_chunks + chunk(i), 0))
    return main, prev, nxt, rows


def _band_fwd_call(proj, col, bias, sink, dil, group, kvw, scale, name):
    s_tok = proj.shape[0]
    seq, tl, nb, n_chunks, period = _band_geometry(proj, dil)
    n_blocks = seq // BAND_BLOCK
    heads = bias.shape[0]

    def body(w_ref, pw_ref, nw_ref, bias_ref, sink_ref, o_ref, lse_ref):
        i = pl.program_id(1)
        w, pw, nw = w_ref[...].astype(F32), pw_ref[...].astype(F32), nw_ref[...].astype(F32)
        kb = _bands(w, pw, nw, QW, kvw, nb)
        vb = _bands(w, pw, nw, QW + kvw, kvw, nb)
        edge = _edge_mask(i * nb, nb, period)
        o_acc = [jnp.zeros((tl, LANES), F32) for _ in range(heads // 2)]
        lse_acc = [jnp.zeros((tl, LANES), F32) for _ in range(heads // 2)]
        geom = [_head_geometry(h, group) for h in range(heads)]
        logits = []
        for h, (pair, a, kvp, b) in enumerate(geom):
            q_al = _align(w[:, pair * LANES:(pair + 1) * LANES], a, b).astype(BF16).reshape(nb, BAND_BLOCK, LANES)
            logits.append(_bmm(q_al, kb[kvp], 2, 2) * scale + bias_ref[h][None] + edge)
        es, ssums, ms = [], [], []
        for h in range(heads):
            sk = sink_ref[h].reshape(1, 1, 1)
            m = jnp.maximum(jnp.max(logits[h], axis=2, keepdims=True), sk)
            e = jnp.exp(logits[h] - m)
            es.append(e.astype(BF16))
            ssums.append(jnp.sum(e, axis=2, keepdims=True) + jnp.exp(sk - m))
            ms.append(m)
        for h, (pair, a, kvp, b) in enumerate(geom):
            out = _bmm(es[h], vb[kvp], 2, 1) / ssums[h]
            o_acc[pair] = o_acc[pair] + _unalign(out.reshape(tl, LANES), a, b)
            lse = (ms[h] + jnp.log(ssums[h])).reshape(tl, 1)
            lse_acc[pair] = lse_acc[pair] + jnp.where(_lane_half() == a, lse, 0.0)
        o_ref[...] = jnp.concatenate(o_acc, axis=1)
        lse_ref[...] = jnp.concatenate(lse_acc, axis=1)

    main, prev, nxt, rows = _band_in_specs(tl, nb, n_chunks, n_blocks, col, False)
    return pl.pallas_call(
        body,
        grid=(1, n_chunks),
        in_specs=[main, prev, nxt, pl.BlockSpec(bias.shape, lambda j, i: (0, 0, 0)),
                  pl.BlockSpec(sink.shape, lambda j, i: (0, 0, 0))],
        out_specs=[rows, rows],
        out_shape=[jax.ShapeDtypeStruct((s_tok, QW), F32)] * 2,
        name=name + "_fwd",
        compiler_params=_params("parallel", "parallel"),
    )(proj, proj, proj, bias, sink)


def _band_bwd_call(proj, o, do, lse, dlse, bias, sink, dproj, col, dil, group, kvw, scale, name):
    seq, tl, nb, n_chunks, period = _band_geometry(proj, dil)
    lead = 0 if dproj is None else 1
    n_blocks = seq // BAND_BLOCK
    heads = bias.shape[0]
    b_ = BAND_BLOCK
    have_dlse = dlse is not None

    def body(*refs):
        (w_ref, pw_ref, nw_ref, o_ref, do_ref, lse_ref), refs = refs[lead:lead + 6], refs[lead + 6:]
        if have_dlse:
            dlse_ref, refs = refs[0], refs[1:]
        bias_ref, sink_ref, dwin_ref, dbias_ref, dsink_ref, dq_s, dk_s, dv_s = refs
        j, i = pl.program_id(0), pl.program_id(1)

        @pl.when((j == 0) & (i == 0))
        def _():
            dbias_ref[...] = jnp.zeros_like(dbias_ref)
            dsink_ref[...] = jnp.zeros_like(dsink_ref)

        @pl.when(i == 0)
        def _():
            dk_s[...] = jnp.zeros_like(dk_s)
            dv_s[...] = jnp.zeros_like(dv_s)

        @pl.when(i < n_chunks)
        def _():
            w, pw, nw = w_ref[...].astype(F32), pw_ref[...].astype(F32), nw_ref[...].astype(F32)
            kb = _bands(w, pw, nw, QW, kvw, nb)
            vb = _bands(w, pw, nw, QW + kvw, kvw, nb)
            edge = _edge_mask(i * nb, nb, period)
            dq_acc = [jnp.zeros((tl, LANES), F32) for _ in range(heads // 2)]
            for h in range(heads):
                pair, a, kvp, b = _head_geometry(h, group)
                lanes = slice(pair * LANES, (pair + 1) * LANES)
                mine = _lane_half() == a
                q_al = _align(w[:, lanes], a, b).astype(BF16).reshape(nb, b_, LANES)
                do_al = _align(do_ref[:, lanes], a, b).astype(BF16).reshape(nb, b_, LANES)
                lse_h = jnp.max(jnp.where(mine, lse_ref[:, lanes], NEG_INF), axis=1, keepdims=True)
                shift = -jnp.sum(jnp.where(mine, do_ref[:, lanes] * o_ref[:, lanes], 0.0), axis=1, keepdims=True)
                if have_dlse:
                    shift = shift + jnp.sum(jnp.where(mine, dlse_ref[:, lanes], 0.0), axis=1, keepdims=True)
                logits = _bmm(q_al, kb[kvp], 2, 2) * scale + bias_ref[h][None] + edge
                p = jnp.exp(logits - lse_h.reshape(nb, b_, 1))
                dlogits = p * (_bmm(do_al, vb[kvp], 2, 2) + shift.reshape(nb, b_, 1))
                dbias_ref[h] += jnp.sum(dlogits, axis=0)
                dsink_ref[h] += jnp.sum(jnp.exp(sink_ref[h] - lse_h) * shift, axis=0, keepdims=True)
                ds = (dlogits * scale).astype(BF16)
                dq_acc[pair] = dq_acc[pair] + _unalign(_bmm(ds, kb[kvp], 2, 1).reshape(tl, LANES), a, b)
                dk_band = _bmm(ds, q_al, 1, 1)
                dv_band = _bmm(p.astype(BF16), do_al, 1, 1)
                kv_lanes = slice(kvp * LANES, (kvp + 1) * LANES)
                for t in range(3):
                    at = pl.ds(pl.multiple_of(i * tl + t * b_, b_), tl)
                    dk_s[at, kv_lanes] += dk_band[:, t * b_:(t + 1) * b_, :].reshape(tl, LANES)
                    dv_s[at, kv_lanes] += dv_band[:, t * b_:(t + 1) * b_, :].reshape(tl, LANES)
            dq_s[lax.rem(i, 2)] = jnp.concatenate(dq_acc, axis=1)

        @pl.when(i >= 1)
        def _():
            at = pl.ds(pl.multiple_of((i - 1) * tl + b_, b_), tl)
            parts = [dq_s[lax.rem(i + 1, 2)], dk_s[at, :], dv_s[at, :]]
            if QW + 2 * kvw < BAND_W:
                parts.append(jnp.zeros((tl, BAND_W - QW - 2 * kvw), F32))
            dwin_ref[...] = jnp.concatenate(parts, axis=1).astype(dwin_ref.dtype)

    main, prev, nxt, rows = _band_in_specs(tl, nb, n_chunks, n_blocks, col, True)
    row_args = [o, do, lse] + ([dlse] if have_dlse else [])
    small = [pl.BlockSpec(bias.shape, lambda j, i: (0, 0, 0)), pl.BlockSpec(sink.shape, lambda j, i: (0, 0, 0))]
    return pl.pallas_call(
        body,
        grid=(1, n_chunks + 1),
        in_specs=[pl.BlockSpec(memory_space=pl.ANY)] * lead + [main, prev, nxt] + [rows] * len(row_args) + small,
        out_specs=[pl.BlockSpec((tl, BAND_W), lambda j, i: (j * n_chunks + jnp.maximum(i - 1, 0), col))] + small,
        out_shape=[jax.ShapeDtypeStruct(proj.shape, BF16), jax.ShapeDtypeStruct(bias.shape, F32),
                   jax.ShapeDtypeStruct(sink.shape, F32)],
        scratch_shapes=[pltpu.VMEM((2, tl, QW), F32), pltpu.VMEM((seq + 2 * b_, kvw), F32),
                        pltpu.VMEM((seq + 2 * b_, kvw), F32)],
        input_output_aliases={0: 0} if lead else {},
        name=name + "_bwd",
        compiler_params=_params("arbitrary", "arbitrary"),
    )(*([dproj] if lead else []), proj, proj, proj, *row_args, bias, sink)


def _loss_call(x, target, g):
    s, d = x.shape
    tr = _pick(s, (256, 128, 64, 32, 16, 8))

    def tile_loss(xt, gt, tt):
        err = jnp.square(_rms(xt, gt) - tt)
        return 0.5 * jnp.sum(jnp.mean(err, axis=-1, keepdims=True), axis=0, keepdims=True)

    def body(x_ref, t_ref, g_ref, loss_ref, dx_ref, dg_ref):
        tt = t_ref[...]
        val, vjp = jax.vjp(lambda xt, gt: tile_loss(xt, gt, tt), x_ref[...], g_ref[...])
        dx, dg = vjp(jnp.ones_like(val))
        dx_ref[...] = dx

        @pl.when(pl.program_id(0) == 0)
        def _():
            loss_ref[...] = jnp.zeros_like(loss_ref)
            dg_ref[...] = jnp.zeros_like(dg_ref)

        loss_ref[...] += val
        dg_ref[...] += dg

    return pl.pallas_call(
        body,
        grid=(s // tr,),
        in_specs=[_rows(tr, d), _rows(tr, d), _whole((1, d))],
        out_specs=[_whole((1, 1)), _rows(tr, d), _whole((1, d))],
        out_shape=[jax.ShapeDtypeStruct((1, 1), F32), jax.ShapeDtypeStruct((s, d), F32),
                   jax.ShapeDtypeStruct((1, d), F32)],
        name="final_norm_loss",
        compiler_params=_params("arbitrary"),
    )(x, target, g)


@jax.custom_vjp
def _loss_op(x, target, g):
    return _loss_call(x, target, g)[0][0, 0]


def _loss_op_fwd(x, target, g):
    loss, dx, dg = _loss_call(x, target, g)
    return loss[0, 0], (dx, dg, target)


def _loss_op_bwd(res, ct):
    dx, dg, target = res
    return ct * dx, jnp.zeros_like(target), ct * dg


_loss_op.defvjp(_loss_op_fwd, _loss_op_bwd)


def _mla_tile(r, p, a):
    g_q, g_kv, w_q, w_k, w_v = p
    cos_t, sin_t, place_kr = a
    a_q, a_kv, a_kr = _lanes(r[0], (0, MLA_Q_LORA, MLA_Q_LORA + MLA_KV_LORA, MLA_Q_LORA + MLA_KV_LORA + MLA_ROPE))
    q = _rope(_bdot(_rms(a_q, g_q), w_q), cos_t, sin_t, MLA_ROPE // 2)
    ckv = _rms(a_kv, g_kv)
    k = _rope(_bdot(ckv, w_k) + _hdot(a_kr, place_kr), cos_t, sin_t, MLA_ROPE // 2)
    return _split_heads(q, MLA_HEADS), _split_heads(k, MLA_HEADS), _split_heads(_bdot(ckv, w_v), MLA_HEADS)


def _head_rms(x, g, head_mean):
    return x * lax.rsqrt(_hdot(x * x, head_mean) + EPS) * g


def _gqa_tile(r, p, a):
    g_q, g_k = p
    cos_t, sin_t, mean_q, mean_k = a
    wq, wk = GQA_HEADS * HEAD_DIM, GQA_KV_HEADS * HEAD_DIM
    b_q, b_k, b_v = _lanes(r[0], (0, wq, wq + wk, wq + 2 * wk))
    q = _rope(_head_rms(b_q, g_q, mean_q), cos_t, sin_t, HEAD_DIM // 4)
    k = _rope(_head_rms(b_k, g_k, mean_k), cos_t[:, :wk], sin_t[:, :wk], HEAD_DIM // 4)
    return _split_heads(q, GQA_HEADS), _split_heads(k, GQA_KV_HEADS), _split_heads(b_v, GQA_KV_HEADS)


def _permute_rows(p, x, cp):
    pb = p.astype(BF16)
    hi = x.astype(BF16)
    rest = x - hi.astype(F32)
    mid = rest.astype(BF16)
    low = (rest - mid.astype(F32)).astype(BF16)
    dims = (((cp,), (0,)), ((), ()))
    return (lax.dot_general(pb, hi, dims, preferred_element_type=F32)
            + lax.dot_general(pb, mid, dims, preferred_element_type=F32)
            + lax.dot_general(pb, low, dims, preferred_element_type=F32))


@jax.custom_vjp
def _permuted(p, x):
    return _permute_rows(p, x, 1)


def _permuted_fwd(p, x):
    return _permute_rows(p, x, 1), p


def _permuted_bwd(p, ct):
    return jnp.zeros_like(p), _permute_rows(p, ct, 0)


_permuted.defvjp(_permuted_fwd, _permuted_bwd)


def _interleave(p, x):
    return _permuted(p, x.reshape(x.shape[0] * x.shape[1], x.shape[2]))


def _interleave_matrix(rows, dil):
    p = np.zeros((rows, rows), np.float32)
    for t in range(rows):
        p[t, (t % dil) * (rows // dil) + t // dil] = 1.0
    return p


def _merge_tile(r, p, a):
    gm, o_a, o_b, oc0, oc1, oc2, l0, l1, l2, o_d = r
    (w_branch,) = p
    perm1, perm2 = a
    oc1, l1, oc2, l2 = _interleave(perm1, oc1), _interleave(perm1, l1), _interleave(perm2, oc2), _interleave(perm2, l2)
    d = w_branch.shape[2]
    gate_path, merge_logits = _lanes(gm, (0, N_BRANCH * BRANCH_W, N_BRANCH * BRANCH_W + N_BRANCH * d))
    m = jnp.maximum(jnp.maximum(l0, l1), l2)
    e0, e1, e2 = jnp.exp(l0 - m), jnp.exp(l1 - m), jnp.exp(l2 - m)
    y_c = (e0 * oc0 + e1 * oc1 + e2 * oc2) / (e0 + e1 + e2)
    y = jnp.concatenate([_join_heads(o_a), _join_heads(o_b), y_c, o_d], axis=1)
    u = y * (gate_path * jax.nn.sigmoid(gate_path))
    gates = _lanes(merge_logits, tuple(range(0, N_BRANCH * d + 1, d)))
    us = _lanes(u, tuple(range(0, N_BRANCH * BRANCH_W + 1, BRANCH_W)))
    branch_w = _unstack(w_branch)
    out = None
    for nb in range(N_BRANCH):
        term = jax.nn.sigmoid(gates[nb]) * _bdot(us[nb], branch_w[nb])
        out = term if out is None else out + term
    return (out,)


def _mixer_calls(proj, prm, aux):
    s = proj.shape[0]
    tr, tm = _pick(s, (512, 256, 128)), _pick(s, (256,))
    mla_cos, mla_sin, gqa_cos, gqa_sin, place_kr, mean_q, mean_k = aux[:7]
    wq = MLA_HEADS * MLA_QK
    mla = dict(
        steps=s // tr, rows=[(proj, _rows(tr, SMALL_W, MLA_BLK))],
        params=[prm["g_q"], prm["g_kv"], prm["w_q"], prm["w_k"], prm["w_v"]],
        aux=[(mla_cos, _rows(tr, wq)), (mla_sin, _rows(tr, wq)), (place_kr, _whole(place_kr.shape))],
        outs=[((MLA_HEADS, s, MLA_QK), _head_rows(MLA_HEADS, tr, MLA_QK))] * 2
        + [((MLA_HEADS, s, MLA_V), _head_rows(MLA_HEADS, tr, MLA_V))],
        window=((s, P_TOT), _rows(tr, SMALL_W, MLA_BLK), BF16))
    wg = GQA_HEADS * HEAD_DIM
    gqa = dict(
        steps=s // tr, rows=[(proj, _rows(tr, SMALL_W, GQA_BLK))], params=[prm["gq"], prm["gk"]],
        aux=[(gqa_cos, _rows(tr, wg)), (gqa_sin, _rows(tr, wg)), (mean_q, _whole(mean_q.shape)),
             (mean_k, _whole(mean_k.shape))],
        outs=[((GQA_HEADS, s, HEAD_DIM), _head_rows(GQA_HEADS, tr, HEAD_DIM))]
        + [((GQA_KV_HEADS, s, HEAD_DIM), _head_rows(GQA_KV_HEADS, tr, HEAD_DIM))] * 2,
        window=((s, P_TOT), _rows(tr, SMALL_W, GQA_BLK), BF16))
    merge = dict(steps=s // tm, tm=tm, window=((s, P_TOT), _rows(tm, GM_W, 0), BF16))
    return mla, gqa, merge


def _merge_rows(proj, o_a, o_b, ocs, lses, o_d, tm):
    h4 = _head_rows(4, tm, HEAD_DIM)
    s = proj.shape[0]

    def by_residue(t, dil):
        if dil == 1:
            return t, _rows(tm, QW)
        return t.reshape(dil, s // dil, QW), pl.BlockSpec((dil, tm // dil, QW), lambda i: (0, i, 0))

    dils = [dil for _, dil in DIL_PATTERNS]
    return ([(proj, _rows(tm, GM_W, 0)), (o_a, h4), (o_b, h4)] + [by_residue(t, r) for t, r in zip(ocs, dils)]
            + [by_residue(t, r) for t, r in zip(lses, dils)] + [(o_d, _rows(tm, QW))])


def _merge_aux(aux):
    return [(t, _whole(t.shape)) for t in aux[7:9]]


def _to_residues(t, dil):
    s, w = t.shape
    return t if dil == 1 else t.reshape(s // dil, dil, w).transpose(1, 0, 2).reshape(s, w)


def _from_residues(t, dil):
    s, w = t.shape
    return t if dil == 1 else t.reshape(dil, s // dil, w).transpose(1, 0, 2).reshape(s, w)


def _mixer_fwd(projs, prm, aux):
    proj = projs[0]
    s = proj.shape[0]
    mla, gqa, merge = _mixer_calls(proj, prm, aux)
    q_a, k_a, v_a = _fwd_call("prep_mla", _mla_tile, mla["steps"], mla["rows"], mla["params"], mla["aux"], mla["outs"])
    o_a, lse_a = _dense_fwd_call(q_a, k_a, v_a, MLA_QK ** -0.5, "mla")
    q_b, k_b, v_b = _fwd_call("prep_gqa", _gqa_tile, gqa["steps"], gqa["rows"], gqa["params"], gqa["aux"], gqa["outs"])
    grp = GQA_HEADS // GQA_KV_HEADS
    o_b, lse_b = _dense_fwd_call(q_b.reshape(GQA_KV_HEADS, grp * s, HEAD_DIM), k_b, v_b, HEAD_DIM ** -0.5, "gqa")
    scale = HEAD_DIM ** -0.5
    ocs, lses = [], []
    for gi, (_, dil) in enumerate(DIL_PATTERNS):
        o, lse = _band_fwd_call(projs[gi], DIL_BLK if gi == 0 else 0, prm["bias_dil"][gi], prm["no_sink"], dil, 1,
                                QW, scale, "dil%d" % gi)
        ocs.append(o)
        lses.append(lse)
    o_d, lse_d = _band_fwd_call(proj, WIN_BLK, prm["bias_win"], prm["sink"], 1, WIN_HEADS // WIN_KV_HEADS,
                                WIN_KV_HEADS * HEAD_DIM, scale, "win")
    rows = _merge_rows(proj, o_a, o_b.reshape(GQA_HEADS, s, HEAD_DIM), ocs, lses, o_d, merge["tm"])
    mix = _fwd_call("merge", _merge_tile, merge["steps"], rows, [prm["w_branch"]], _merge_aux(aux),
                    [((s, prm["w_branch"].shape[2]), _rows(merge["tm"], prm["w_branch"].shape[2]), BF16)])[0]
    return mix, (q_a, k_a, v_a, o_a, lse_a, q_b, k_b, v_b, o_b, lse_b, ocs, lses, o_d, lse_d)


def _mixer_bwd(projs, prm, aux, saved, dmix):
    proj = projs[0]
    s = proj.shape[0]
    q_a, k_a, v_a, o_a, lse_a, q_b, k_b, v_b, o_b, lse_b, ocs, lses, o_d, lse_d = saved
    dils = [dil for _, dil in DIL_PATTERNS]
    mla, gqa, merge = _mixer_calls(proj, prm, aux)
    tm, d_model = merge["tm"], prm["w_branch"].shape[2]
    grp = GQA_HEADS // GQA_KV_HEADS
    scale = HEAD_DIM ** -0.5

    rows = _merge_rows(proj, o_a, o_b.reshape(GQA_HEADS, s, HEAD_DIM), ocs, lses, o_d, tm)
    grads, (dw_branch,) = _vjp_call(
        "merge", _merge_tile, merge["steps"], rows, [prm["w_branch"]], _merge_aux(aux), [(dmix, _rows(tm, d_model))],
        [merge["window"]] + [(a.shape, spec) for a, spec in rows[1:]])
    dproj, do_a, do_b, docs, dlses, do_d = grads[0], grads[1], grads[2], grads[3:6], grads[6:9], grads[9]

    dq_a, dk_a, dv_a = _dense_bwd_call(q_a, k_a, v_a, o_a, lse_a, do_a, MLA_QK ** -0.5, "mla")
    (dproj,), dmla = _vjp_call("prep_mla", _mla_tile, mla["steps"], mla["rows"], mla["params"], mla["aux"],
                               [(t, spec) for t, (_, spec) in zip((dq_a, dk_a, dv_a), mla["outs"])],
                               [mla["window"]], into=dproj)
    dq_b, dk_b, dv_b = _dense_bwd_call(q_b.reshape(GQA_KV_HEADS, grp * s, HEAD_DIM), k_b, v_b, o_b, lse_b,
                                       do_b.reshape(GQA_KV_HEADS, grp * s, HEAD_DIM), scale, "gqa")
    (dproj,), dgqa = _vjp_call("prep_gqa", _gqa_tile, gqa["steps"], gqa["rows"], gqa["params"], gqa["aux"],
                               [(t, spec) for t, (_, spec) in zip((dq_b.reshape(GQA_HEADS, s, HEAD_DIM), dk_b, dv_b),
                                                                  gqa["outs"])],
                               [gqa["window"]], into=dproj)
    dproj, dbias_win, dsink = _band_bwd_call(proj, o_d, do_d, lse_d, None, prm["bias_win"], prm["sink"], dproj,
                                             WIN_BLK, 1, WIN_HEADS // WIN_KV_HEADS, WIN_KV_HEADS * HEAD_DIM, scale, "win")
    dbias_dil, dprojs = [], []
    for gi, dil in enumerate(dils):
        dside, dbias, _ = _band_bwd_call(
            projs[gi], ocs[gi], docs[gi].reshape(s, QW), lses[gi], dlses[gi].reshape(s, QW),
            prm["bias_dil"][gi], prm["no_sink"], dproj if gi == 0 else None, DIL_BLK if gi == 0 else 0, dil, 1, QW,
            scale, "dil%d" % gi)
        if gi == 0:
            dproj = dside
        else:
            dprojs.append(dside)
        dbias_dil.append(dbias)
    dprm = dict(g_q=dmla[0], g_kv=dmla[1], w_q=dmla[2], w_k=dmla[3], w_v=dmla[4], gq=dgqa[0], gk=dgqa[1],
                bias_dil=dbias_dil, bias_win=dbias_win, sink=dsink, no_sink=jnp.zeros_like(prm["no_sink"]),
                w_branch=dw_branch)
    return [dproj] + dprojs, {k: jax.tree.map(lambda g, p: g.astype(p.dtype), v, prm[k]) for k, v in dprm.items()}


def _layer_fwd(x, w, aux):
    s, d = x.shape
    tr = _pick(s, (256,))
    dils = [dil for _, dil in DIL_PATTERNS]

    def norm_forms(r, p, a):
        y = _rms(r[0], p[0])
        return [y, y.T] + [_dot(q, y, 1, 0).reshape(dil, tr // dil, d) for q, dil in zip(a, dils[1:])]

    forms = _fwd_call(
        "norm", norm_forms, s // tr, [(x, _rows(tr, d))], [w["norm_g"]], [(q, _whole(q.shape)) for q in aux[9:11]],
        [((s, d), _rows(tr, d), BF16), ((d, s), pl.BlockSpec((d, tr), lambda i: (0, i)), BF16)]
        + [((dil, s // dil, d), pl.BlockSpec((dil, tr // dil, d), lambda i: (0, i, 0)), BF16) for dil in dils[1:]])
    xn_t, xns = forms[1], [forms[0]] + [t.reshape(s, d) for t in forms[2:]]
    projs = [_mm(a, b, "nt", "proj%d_fwd" % i, BF16) for i, (a, b) in enumerate(zip(xns, w["w_in_t"]))]
    mix, saved = _mixer_fwd(projs, w["mixer"], aux)
    return _mm(mix, w["w_out"], "nn", "out_proj_nn"), (x, w, aux, xns, xn_t, projs, mix, saved)


@jax.custom_vjp
def _layer_core(x, w, aux):
    return _layer_fwd(x, w, aux)[0]


def _layer_core_bwd(res, dout):
    x, w, aux, xns, xn_t, projs, mix, saved = res
    s, d = x.shape
    tr = _pick(s, (256, 128, 64, 32, 16, 8))
    dils = [dil for _, dil in DIL_PATTERNS]
    dmix = _mm(dout, w["w_out"], "nt", "out_proj_nt")
    dw_out = _mm(mix.T, dout, "nn", "out_proj_dw", w["w_out"].dtype)
    dprojs, dmixer = _mixer_bwd(projs, w["mixer"], aux, saved, dmix)
    side = jnp.concatenate([_from_residues(dp, r) for dp, r in zip(dprojs[1:], dils[1:])], axis=1)
    dxn_terms = [_mm(dprojs[0], w["w_in_t"][0], "nn", "proj0_dx"),
                 _mm(side, jnp.concatenate(w["w_in_t"][1:], axis=0), "nn", "proj_side_dx")]
    dw_in_t = [_mm(a_t, dp, "nn", "proj%d_dw" % i, wi.dtype).T
               for i, (a_t, dp, wi) in enumerate(zip([xn_t] + [a.T for a in xns[1:]], dprojs, w["w_in_t"]))]
    (dx,), (dg,) = _vjp_call("norm", lambda r, p, a: _norm_tile(r, p, a) * len(dxn_terms), s // tr, [(x, _rows(tr, d))],
                             [w["norm_g"]], [], [(t, _rows(tr, d)) for t in dxn_terms], [((s, d), _rows(tr, d))])
    dw = dict(norm_g=dg, w_in_t=dw_in_t, mixer=dmixer, w_out=dw_out)
    return dx, dw, tuple(jnp.zeros_like(t) for t in aux)


_layer_core.defvjp(lambda x, w, aux: _layer_fwd(x, w, aux), _layer_core_bwd)


def _rope_angles(pos, dim):
    inv = ROPE_THETA ** (-jnp.arange(0, dim, 2, dtype=F32) / dim)
    return pos.astype(F32)[:, None] * inv[None, :]


def _rope_tables(s):
    pos = jnp.arange(s, dtype=jnp.int32)
    rows = s // GRID_W
    row_idx = jnp.repeat(jnp.arange(rows, dtype=jnp.int32), GRID_W)
    col_idx = jnp.tile(jnp.arange(GRID_W, dtype=jnp.int32), rows)
    a1 = _rope_angles(pos, MLA_ROPE)
    ar = _rope_angles(row_idx, HEAD_DIM // 2)
    ac = _rope_angles(col_idx, HEAD_DIM // 2)
    ones, zeros = jnp.ones((s, MLA_NOPE), F32), jnp.zeros((s, MLA_NOPE), F32)
    mla_cos = jnp.tile(jnp.concatenate([ones, jnp.cos(a1), jnp.cos(a1)], axis=1), (1, MLA_HEADS))
    mla_sin = jnp.tile(jnp.concatenate([zeros, -jnp.sin(a1), jnp.sin(a1)], axis=1), (1, MLA_HEADS))
    gqa_cos = jnp.tile(jnp.concatenate([jnp.cos(ar), jnp.cos(ar), jnp.cos(ac), jnp.cos(ac)], axis=1), (1, GQA_HEADS))
    gqa_sin = jnp.tile(jnp.concatenate([-jnp.sin(ar), jnp.sin(ar), -jnp.sin(ac), jnp.sin(ac)], axis=1), (1, GQA_HEADS))
    return mla_cos, mla_sin, gqa_cos, gqa_sin


def _t5_bucket(rel):
    nb = T5_BUCKETS // 2
    max_exact = nb // 2
    n = jnp.abs(rel)
    nf = jnp.maximum(n, 1).astype(F32)
    large = max_exact + (jnp.log(nf / max_exact) / math.log(T5_MAX_DIST / max_exact) * (nb - max_exact)).astype(jnp.int32)
    large = jnp.minimum(large, nb - 1)
    return jnp.where(rel > 0, nb, 0) + jnp.where(n < max_exact, n, large)


def _band_bias(table, stride, head_lo, heads, half_window):
    b = BAND_BLOCK
    offs = jnp.arange(3 * b)[None, :] - b - jnp.arange(b)[:, None]
    one_hot = (_t5_bucket(offs * stride)[..., None] == jnp.arange(T5_BUCKETS)).astype(F32)
    bias = jnp.dot(one_hot.reshape(b * 3 * b, T5_BUCKETS), table[:, head_lo:head_lo + heads],
                   precision=lax.Precision.HIGHEST)
    bias = bias.T.reshape(heads, b, 3 * b)
    return jnp.where((jnp.abs(offs) <= half_window)[None], bias, NEG_INF)


def _w_in_rows(d):
    mla, gqa, win, dil0 = MLA_BLK * SMALL_W, GQA_BLK * SMALL_W, WIN_BLK * BAND_W, DIL_BLK * BAND_W
    plan, at = [], 0
    for width, target, row in ((256, 0, mla), (128, 0, mla + 256), (32, 0, mla + 384),
                               (256, 0, gqa), (128, 0, gqa + 256), (128, 0, gqa + 384)):
        plan.append((at, width, target, row))
        at += width
    for part in range(3):
        for g in range(len(DIL_PATTERNS)):
            plan.append((at, QW, g, (dil0 if g == 0 else 0) + part * QW))
            at += QW
    for width, row in ((256, win), (128, win + 256), (128, win + 384), (N_BRANCH * BRANCH_W, 0),
                       (N_BRANCH * d, N_BRANCH * BRANCH_W)):
        plan.append((at, width, 0, row))
        at += width
    return plan


@jax.custom_vjp
def _w_in_layout(w_in_t):
    d = w_in_t.shape[1]
    outs = []
    for target, rows in enumerate((P_TOT, BAND_W, BAND_W)):
        parts, at = [], 0
        for start, width, _, row in sorted((p for p in _w_in_rows(d) if p[2] == target), key=lambda p: p[3]):
            if row > at:
                parts.append(jnp.zeros((row - at, d), w_in_t.dtype))
            parts.append(w_in_t[start:start + width])
            at = row + width
        if at < rows:
            parts.append(jnp.zeros((rows - at, d), w_in_t.dtype))
        outs.append(jnp.concatenate(parts, axis=0))
    return outs


def _w_in_layout_fwd(w_in_t):
    return _w_in_layout(w_in_t), None


def _w_in_layout_bwd(_, cts):
    d = cts[0].shape[1]
    return (jnp.concatenate([cts[target][row:row + width] for _, width, target, row in _w_in_rows(d)], axis=0),)


_w_in_layout.defvjp(_w_in_layout_fwd, _w_in_layout_bwd)


def _layer(x, w, l, aux, biases):
    w_kv = w["w_kv_t"][l].T.reshape(MLA_KV_LORA, MLA_HEADS, MLA_NOPE + MLA_V)
    w_k = jnp.concatenate([w_kv[:, :, :MLA_NOPE], jnp.zeros((MLA_KV_LORA, MLA_HEADS, MLA_ROPE), w_kv.dtype)], axis=2)
    dil_bias, win_bias = biases
    prm = dict(
        g_q=w["mla_q_norm_g"][l][None, :], g_kv=w["mla_kv_norm_g"][l][None, :], w_q=w["w_q_t"][l].T,
        w_k=w_k.reshape(MLA_KV_LORA, MLA_HEADS * MLA_QK),
        w_v=w_kv[:, :, MLA_NOPE:].reshape(MLA_KV_LORA, MLA_HEADS * MLA_V),
        gq=jnp.tile(w["gqa_q_norm_g"][l], GQA_HEADS)[None, :], gk=jnp.tile(w["gqa_k_norm_g"][l], GQA_KV_HEADS)[None, :],
        bias_dil=list(dil_bias), bias_win=win_bias, sink=w["win_sink"][l].reshape(WIN_HEADS, 1, 1),
        no_sink=jnp.full((DIL_HEADS, 1, 1), NEG_INF, F32), w_branch=jnp.transpose(w["w_branch_t"][l].reshape(-1, N_BRANCH, BRANCH_W), (1, 2, 0)))
    layer_w = dict(norm_g=w["norm_g"][l][None, :], w_in_t=_w_in_layout(w["w_in_t"][l]), mixer=prm, w_out=w["w_out"][l])
    return x + _layer_core(x, layer_w, aux)


def _local_loss(w, x, target):
    s, d_model = x.shape
    assert d_model == D_MODEL, "the projection's window layout is laid out for d_model 1024"
    place = np.zeros((MLA_ROPE, MLA_HEADS * MLA_QK), np.float32)
    for h in range(MLA_HEADS):
        for i in range(MLA_ROPE):
            place[i, h * MLA_QK + MLA_NOPE + i] = 1.0

    def head_mean(nh):
        m = np.kron(np.eye(nh, dtype=np.float32), np.full((HEAD_DIM, HEAD_DIM), 1.0 / HEAD_DIM, np.float32))
        return jnp.asarray(m)

    merge_tile = _pick(s, (256,))
    norm_tile = _pick(s, (256,))
    aux = _rope_tables(s) + (jnp.asarray(place), head_mean(GQA_HEADS), head_mean(GQA_KV_HEADS)) + tuple(
        jnp.asarray(_interleave_matrix(merge_tile, dil)) for _, dil in DIL_PATTERNS[1:]) + tuple(
        jnp.asarray(_interleave_matrix(norm_tile, dil).T) for _, dil in DIL_PATTERNS[1:])
    table = w["t5_table"]
    dil_bias = [_band_bias(table, dil, gi * DIL_HEADS, DIL_HEADS, window // (2 * dil))
                for gi, (window, dil) in enumerate(DIL_PATTERNS)]
    win_bias = _band_bias(table, 1, len(DIL_PATTERNS) * DIL_HEADS, WIN_HEADS, WIN_HALF)
    for l in range(w["norm_g"].shape[0]):
        x = _layer(x, w, l, aux, (dil_bias, win_bias))
    return _loss_op(x, target, w["final_norm_g"][None, :])


_ANY = pl.BlockSpec(memory_space=pl.ANY)
_MESH = pl.DeviceIdType.MESH


def _all_gather(block, name):
    def body(x_ref, out_ref, send_sems, recv_sems, local_sem):
        x, y, c = lax.axis_index("x"), lax.axis_index("y"), lax.axis_index("c")
        me, sibling = (x, y, c), (x, y, 1 - c)
        chips = [(1 - x, y), (x, 1 - y), (1 - x, 1 - y)]

        def slot(px, py, pc):
            return out_ref.at[4 * px + 2 * py + pc]

        def copy(k, blk, to, src=None):
            return pltpu.make_async_remote_copy(
                src_ref=slot(*blk) if src is None else src, dst_ref=slot(*blk),
                send_sem=send_sems.at[k], recv_sem=recv_sems.at[k], device_id=to, device_id_type=_MESH)

        mine = pltpu.make_async_copy(x_ref, slot(*me), local_sem)
        mine.start()
        first = [copy(0, me, sibling, src=x_ref)]
        first += [copy(1 + j, me, (*chip, c), src=x_ref) for j, chip in enumerate(chips)]
        for cp in first:
            cp.start()
        passed = [copy(4 + j, (*chip, c), sibling) for j, chip in enumerate(chips)]
        for j, chip in enumerate(chips):
            copy(1 + j, (*chip, c), me).wait_recv()
            passed[j].start()
        copy(0, sibling, me).wait_recv()
        for j, chip in enumerate(chips):
            copy(4 + j, (*chip, 1 - c), me).wait_recv()
        for cp in first + passed:
            cp.wait_send()
        mine.wait()

    return pl.pallas_call(
        body,
        out_shape=jax.ShapeDtypeStruct((N_DEV,) + block.shape, block.dtype),
        in_specs=[_ANY],
        out_specs=_ANY,
        scratch_shapes=[pltpu.SemaphoreType.DMA((7,)), pltpu.SemaphoreType.DMA((7,)), pltpu.SemaphoreType.DMA],
        name=name,
    )(block)


def _swap_with_sibling(blocks, name):
    chips = blocks.shape[0]

    def body(x_ref, out_ref, send_sems, recv_sems):
        x, y, c = lax.axis_index("x"), lax.axis_index("y"), lax.axis_index("c")
        copies = [pltpu.make_async_remote_copy(
            src_ref=x_ref.at[k, 1 - c], dst_ref=out_ref.at[k], send_sem=send_sems.at[k], recv_sem=recv_sems.at[k],
            device_id=(x, y, 1 - c), device_id_type=_MESH) for k in range(chips)]
        for cp in copies:
            cp.start()
        for cp in copies:
            cp.wait()

    return pl.pallas_call(
        body,
        out_shape=jax.ShapeDtypeStruct((chips,) + blocks.shape[2:], blocks.dtype),
        in_specs=[_ANY],
        out_specs=_ANY,
        scratch_shapes=[pltpu.SemaphoreType.DMA((chips,)), pltpu.SemaphoreType.DMA((chips,))],
        name=name,
    )(blocks)


def _add_sibling(blocks, theirs, name):
    chips, _, rows, w = blocks.shape
    tr = _row_tile(rows, 16, 4096)

    def body(b_ref, t_ref, o_ref):
        mine = b_ref[0, lax.axis_index("c")]
        o_ref[0] = (mine.astype(F32) + t_ref[0].astype(F32)).astype(o_ref.dtype)

    return pl.pallas_call(
        body,
        grid=(chips, rows // tr),
        in_specs=[pl.BlockSpec((1, 2, tr, w), lambda k, i: (k, 0, i, 0)), pl.BlockSpec((1, tr, w), lambda k, i: (k, i, 0))],
        out_specs=pl.BlockSpec((1, tr, w), lambda k, i: (k, i, 0)),
        out_shape=jax.ShapeDtypeStruct(theirs.shape, theirs.dtype),
        name=name,
        compiler_params=_params("parallel", "parallel"),
    )(blocks, theirs)


def _exchange_chips(partials, name):
    n_chips = partials.shape[0]

    def body(x_ref, out_ref, send_sems, recv_sems, local_sem):
        x, y, c = lax.axis_index("x"), lax.axis_index("y"), lax.axis_index("c")
        me = 2 * x + y
        mine = pltpu.make_async_copy(x_ref.at[me], out_ref.at[me], local_sem)
        mine.start()
        copies, landed = [], []
        for k in range(1, n_chips):
            px = 1 - x if k & 2 else x
            py = 1 - y if k & 1 else y
            peer = 2 * px + py
            copies.append(pltpu.make_async_remote_copy(
                src_ref=x_ref.at[peer], dst_ref=out_ref.at[me], send_sem=send_sems.at[k - 1],
                recv_sem=recv_sems.at[k - 1], device_id=(px, py, c), device_id_type=_MESH))
            landed.append(pltpu.make_async_remote_copy(
                src_ref=x_ref.at[peer], dst_ref=out_ref.at[peer], send_sem=send_sems.at[k - 1],
                recv_sem=recv_sems.at[k - 1], device_id=(px, py, c), device_id_type=_MESH))
        for cp in copies:
            cp.start()
        for cp in landed:
            cp.wait_recv()
        for cp in copies:
            cp.wait_send()
        mine.wait()

    return pl.pallas_call(
        body,
        out_shape=jax.ShapeDtypeStruct(partials.shape, partials.dtype),
        in_specs=[_ANY],
        out_specs=_ANY,
        scratch_shapes=[pltpu.SemaphoreType.DMA((n_chips - 1,)), pltpu.SemaphoreType.DMA((n_chips - 1,)),
                        pltpu.SemaphoreType.DMA],
        name=name,
    )(partials)


def _sum_slots(parts, name):
    slots, rows, w = parts.shape
    tr = _row_tile(rows, 16 if parts.dtype == BF16 else 8, 4096)

    def body(p_ref, o_ref):
        acc = p_ref[0].astype(F32)
        for j in range(1, slots):
            acc = acc + p_ref[j].astype(F32)
        o_ref[...] = acc

    return pl.pallas_call(
        body,
        grid=(rows // tr,),
        in_specs=[pl.BlockSpec((slots, tr, w), lambda i: (0, i, 0))],
        out_specs=pl.BlockSpec((tr, w), lambda i: (i, 0)),
        out_shape=jax.ShapeDtypeStruct((rows, w), F32),
        name=name,
        compiler_params=_params("parallel"),
    )(parts)


def _adamw(w, g, m, v, name):
    rows, width = w.shape
    tr = _row_tile(rows, 8, 2048)

    def body(w_ref, g_ref, m_ref, v_ref, d_ref, nm_ref, nv_ref):
        g_ = g_ref[...]
        m_ = ADAM_B1 * m_ref[...] + (1.0 - ADAM_B1) * g_
        v_ = ADAM_B2 * v_ref[...] + (1.0 - ADAM_B2) * jnp.square(g_)
        m_hat = m_ / (1.0 - ADAM_B1 ** ADAM_STEP)
        v_hat = v_ / (1.0 - ADAM_B2 ** ADAM_STEP)
        d_ref[...] = -ADAM_LR * (m_hat / (jnp.sqrt(v_hat) + ADAM_EPS) + ADAM_WD * w_ref[...])
        nm_ref[...] = m_
        nv_ref[...] = v_

    spec = pl.BlockSpec((tr, width), lambda i: (i, 0))
    return pl.pallas_call(
        body,
        grid=(rows // tr,),
        in_specs=[spec] * 4,
        out_specs=[spec] * 3,
        out_shape=[jax.ShapeDtypeStruct((rows, width), F32)] * 3,
        name=name,
        compiler_params=_params("parallel"),
    )(w, g, m, v)


_SHARDED = (("w_in", 2), ("w_mla_q_up", 2), ("w_mla_kv_up", 2), ("w_branch", 3), ("w_out", 1))
_REPLICATED = ("norm_g", "mla_q_norm_g", "mla_kv_norm_g", "gqa_q_norm_g", "gqa_k_norm_g", "win_sink", "t5_table",
               "final_norm_g")


def _pack(arrays, row_multiple):
    flat = jnp.concatenate([a.reshape(-1) for a in arrays])
    rows = -(-flat.shape[0] // (LANES * row_multiple)) * row_multiple
    return jnp.pad(flat, (0, rows * LANES - flat.shape[0])).reshape(rows, LANES)


def _unpack(packed, shapes):
    flat, out, at = packed.reshape(-1), [], 0
    for shp in shapes:
        n = int(np.prod(shp))
        out.append(flat[at:at + n].reshape(shp))
        at += n
    return out


_TO_WIRE = {
    "w_in": lambda t: jnp.swapaxes(t, 1, 2), "w_mla_q_up": lambda t: jnp.swapaxes(t, 1, 2),
    "w_mla_kv_up": lambda t: jnp.swapaxes(t, 1, 2),
    "w_branch": lambda t: jnp.transpose(t, (0, 3, 1, 2)).reshape(t.shape[0], t.shape[3], -1), "w_out": lambda t: t}
_FROM_WIRE = {
    "w_in": lambda t, shp: jnp.swapaxes(t, 1, 2), "w_mla_q_up": lambda t, shp: jnp.swapaxes(t, 1, 2),
    "w_mla_kv_up": lambda t, shp: jnp.swapaxes(t, 1, 2),
    "w_branch": lambda t, shp: jnp.transpose(t.reshape(shp[0], shp[3], shp[1], shp[2]), (0, 2, 3, 1)),
    "w_out": lambda t, shp: t}
_WIRE_NAME = {"w_in": "w_in_t", "w_mla_q_up": "w_q_t", "w_mla_kv_up": "w_kv_t", "w_branch": "w_branch_t",
              "w_out": "w_out"}


def _transpose_blocks(t, dtype, name):
    depth, a, b = t.shape

    def body(x_ref, o_ref):
        o_ref[0] = x_ref[0].T.astype(o_ref.dtype)

    return pl.pallas_call(
        body,
        grid=(depth,),
        in_specs=[pl.BlockSpec((1, a, b), lambda i: (i, 0, 0))],
        out_specs=pl.BlockSpec((1, b, a), lambda i: (i, 0, 0)),
        out_shape=jax.ShapeDtypeStruct((depth, b, a), dtype),
        name=name,
        compiler_params=_params("parallel"),
    )(t)


def _join_shards(gathered, wire_shapes):
    out, at = [], 0
    for depth, cut, rest in wire_shapes:
        n = depth * cut * rest // LANES
        blk = gathered[:, at:at + n].reshape(N_DEV, depth, cut, rest)
        out.append(jnp.moveaxis(blk, 0, 1).reshape(depth, N_DEV * cut, rest))
        at += n
    return out


def _split_shards(fulls, wire_shapes):
    parts = []
    for full, (depth, cut, rest) in zip(fulls, wire_shapes):
        blk = jnp.moveaxis(full.reshape(depth, N_DEV, cut, rest), 1, 0)
        parts.append(blk.reshape(N_DEV, depth * cut * rest // LANES, LANES))
    packed = jnp.concatenate(parts, axis=1)
    return packed.reshape((N_DEV // 2, 2) + packed.shape[1:])


def kernel(x, norm_g, w_in, mla_q_norm_g, mla_kv_norm_g, w_mla_q_up, w_mla_kv_up, gqa_q_norm_g, gqa_k_norm_g, win_sink, t5_table, w_branch, w_out, final_norm_g, loss_target, m_norm_g, m_w_in, m_mla_q_norm_g, m_mla_kv_norm_g, m_w_mla_q_up, m_w_mla_kv_up, m_gqa_q_norm_g, m_gqa_k_norm_g, m_win_sink, m_t5_table, m_w_branch, m_w_out, m_final_norm_g, v_norm_g, v_w_in, v_mla_q_norm_g, v_mla_kv_norm_g, v_w_mla_q_up, v_w_mla_kv_up, v_gqa_q_norm_g, v_gqa_k_norm_g, v_win_sink, v_t5_table, v_w_branch, v_w_out, v_final_norm_g):
    given = dict(locals())
    names = ("norm_g", "w_in", "mla_q_norm_g", "mla_kv_norm_g", "w_mla_q_up", "w_mla_kv_up", "gqa_q_norm_g",
             "gqa_k_norm_g", "win_sink", "t5_table", "w_branch", "w_out", "final_norm_g")
    shard_names = [n for n, _ in _SHARDED]
    shard_shapes = [given[n].shape for n in shard_names]

    wire = [_transpose_blocks(given[n], BF16, "w_in_to_wire") if n == "w_in" else _TO_WIRE[n](given[n]).astype(BF16)
            for n in shard_names]
    wire_shapes = [t.shape for t in wire]
    gathered = _all_gather(jnp.concatenate([t.reshape(-1, LANES) for t in wire]), "gather_weights")
    weights = {n: given[n] for n in _REPLICATED}
    weights.update(zip([_WIRE_NAME[n] for n in shard_names], _join_shards(gathered, wire_shapes)))

    loss, (gw, gx) = jax.value_and_grad(_local_loss, argnums=(0, 1))(weights, x[0], loss_target[0])
    loss = lax.psum(loss, ("x", "y", "c"))

    send = _split_shards([gw[_WIRE_NAME[n]] for n in shard_names], wire_shapes)
    partials = _add_sibling(send, _swap_with_sibling(send, "swap_grads"), "add_sibling_grads")
    g_wire = _unpack(_sum_slots(_exchange_chips(partials, "scatter_grads"), "sum_grads"), wire_shapes)
    g_shard = [_transpose_blocks(t, F32, "w_in_from_wire") if n == "w_in" else _FROM_WIRE[n](t, shp)
               for n, t, shp in zip(shard_names, g_wire, shard_shapes)]
    rep_shapes = [given[n].shape for n in _REPLICATED]
    g_rep = _unpack(_sum_slots(_all_gather(_pack([gw[n] for n in _REPLICATED], 8), "gather_small_grads"),
                               "sum_small_grads"), rep_shapes)
    grads = dict(zip(shard_names, g_shard))
    grads.update(zip(_REPLICATED, g_rep))

    def update(group, shapes, row_multiple, name):
        outs = _adamw(*[_pack([src[n] for n in group], row_multiple) for src in (
            given, grads, {n: given["m_" + n] for n in group}, {n: given["v_" + n] for n in group})], name)
        return [dict(zip(group, _unpack(o, shapes))) for o in outs]

    big = update(shard_names, shard_shapes, 16, "adamw_shards")
    small = update(list(_REPLICATED), rep_shapes, 8, "adamw_replicated")
    delta, new_m, new_v = [{**b, **s_} for b, s_ in zip(big, small)]
    return (loss, gx[None], *[grads[n] for n in names], *[delta[n] for n in names],
            *[new_m[n] for n in names], *[new_v[n] for n in names])
```

```python
import functools
import math

import jax
import jax.numpy as jnp
import numpy as np
from jax import lax
from jax.experimental import pallas as pl
from jax.experimental.pallas import tpu as pltpu

F32 = jnp.float32
BF16 = jnp.bfloat16
N_DEV = 8
LANES = 128
HALF = LANES // 2
V7X_VMEM_LIMIT = 56 * 1024 * 1024

EPS = 1e-6
NEG_INF = -1e30
LOG2E = 1.4426950408889634
ROPE_THETA = 10000.0
GRID_W = 64
HEAD_DIM = 64
N_BRANCH = 4
BRANCH_W = 256
MLA_HEADS, MLA_Q_LORA, MLA_KV_LORA, MLA_NOPE, MLA_ROPE, MLA_V = 4, 256, 128, 64, 32, 64
MLA_QK = MLA_NOPE + MLA_ROPE
GQA_HEADS, GQA_KV_HEADS = 4, 2
DIL_PATTERNS = ((128, 1), (512, 4), (2048, 16))
DIL_HEADS = 4
WIN_HEADS, WIN_KV_HEADS, WIN_HALF = 4, 2, 128
T5_BUCKETS, T5_MAX_DIST = 32, 1024
BAND_BLOCK = 128
ADAM_LR, ADAM_B1, ADAM_B2, ADAM_EPS, ADAM_WD, ADAM_STEP = 0.001, 0.9, 0.999, 1e-08, 0.01, 10

D_MODEL = 1024
GM_W, SMALL_W, BAND_W = 5120, 512, 768
MLA_BLK, GQA_BLK, WIN_BLK, DIL_BLK = 10, 11, 8, 9
P_TOT = 7680
QW = 256


def _params(*sem):
    return pltpu.CompilerParams(dimension_semantics=sem, vmem_limit_bytes=V7X_VMEM_LIMIT)


def _pick(n, cands):
    for c in cands:
        if n % c == 0:
            return c
    return n


def _row_tile(rows, unit, cap):
    best = unit
    for t in range(unit, min(rows, cap) + 1, unit):
        if rows % t == 0:
            best = t
    assert rows % best == 0
    return best


def _dot(a, b, ca, cb):
    return lax.dot_general(a.astype(BF16), b.astype(BF16), (((ca,), (cb,)), ((), ())), preferred_element_type=F32)


def _bmm(a, b, ca, cb):
    return lax.dot_general(a, b, (((ca,), (cb,)), ((0,), (0,))), preferred_element_type=F32)


@jax.custom_vjp
def _bdot(a, b):
    return _dot(a, b, 1, 0)


def _bdot_fwd(a, b):
    return _dot(a, b, 1, 0), (a, b)


def _bdot_bwd(res, g):
    a, b = res
    return _dot(g, b, 1, 1), _dot(a, g, 0, 0)


_bdot.defvjp(_bdot_fwd, _bdot_bwd)


def _hdot(a, c):
    return lax.dot_general(a, c, (((1,), (0,)), ((), ())), precision=lax.Precision.HIGHEST, preferred_element_type=F32)


@functools.partial(jax.custom_vjp, nondiff_argnums=(1,))
def _lane_roll(x, shift):
    return pltpu.roll(x, shift, 1)


def _lane_roll_fwd(x, shift):
    return pltpu.roll(x, shift, 1), None


def _lane_roll_bwd(shift, _, g):
    return (pltpu.roll(g, g.shape[1] - shift, 1),)


_lane_roll.defvjp(_lane_roll_fwd, _lane_roll_bwd)


@functools.partial(jax.custom_vjp, nondiff_argnums=(1,))
def _lane_ranges(x, cut):
    bounds, _ = cut
    return tuple(x[:, lo:hi] for lo, hi in zip(bounds[:-1], bounds[1:]))


def _lane_ranges_fwd(x, cut):
    return _lane_ranges(x, cut), None


def _lane_ranges_bwd(cut, _, cts):
    bounds, width = cut
    parts = list(cts)
    if bounds[-1] < width:
        parts.append(jnp.zeros((cts[0].shape[0], width - bounds[-1]), cts[0].dtype))
    return (jnp.concatenate(parts, axis=1),)


_lane_ranges.defvjp(_lane_ranges_fwd, _lane_ranges_bwd)


def _lanes(x, bounds):
    return _lane_ranges(x, (tuple(bounds), x.shape[1]))


@jax.custom_vjp
def _unstack(x):
    return tuple(x[i] for i in range(x.shape[0]))


def _unstack_fwd(x):
    return _unstack(x), None


def _unstack_bwd(_, cts):
    return (jnp.stack(cts, axis=0),)


_unstack.defvjp(_unstack_fwd, _unstack_bwd)


@functools.partial(jax.custom_vjp, nondiff_argnums=(1,))
def _split_heads(x, h):
    d = x.shape[1] // h
    return jnp.stack([x[:, i * d:(i + 1) * d] for i in range(h)], axis=0)


def _split_heads_fwd(x, h):
    return _split_heads(x, h), None


def _split_heads_bwd(h, _, ct):
    return (jnp.concatenate([ct[i] for i in range(h)], axis=1),)


_split_heads.defvjp(_split_heads_fwd, _split_heads_bwd)


def _join_heads(x):
    return jnp.concatenate(_unstack(x), axis=1)


def _rope(x, cos_t, sin_t, half):
    w = x.shape[1]
    lane = lax.broadcasted_iota(jnp.int32, (1, w), 1)
    first = (lane % (2 * half)) < half
    partner = jnp.where(first, _lane_roll(x, w - half), _lane_roll(x, half))
    return x * cos_t + partner * sin_t


def _rms(x, g):
    return x * lax.rsqrt(jnp.mean(x * x, axis=-1, keepdims=True) + EPS) * g


def _rows(tr, w, col=0):
    return pl.BlockSpec((tr, w), lambda i: (i, col))


def _head_rows(h, tr, d):
    return pl.BlockSpec((h, tr, d), lambda i: (0, i, 0))


def _whole(shape):
    nd = len(shape)
    return pl.BlockSpec(tuple(shape), lambda i: (0,) * nd)


def _fwd_call(name, fn, steps, rows, params, aux, outs):
    nr, npar, na = len(rows), len(params), len(aux)

    def body(*refs):
        vals = [x[...].astype(F32) for x in refs[:nr + npar + na]]
        res = fn(vals[:nr], vals[nr:nr + npar], vals[nr + npar:])
        for o_ref, o in zip(refs[nr + npar + na:], res):
            o_ref[...] = o.astype(o_ref.dtype)

    return pl.pallas_call(
        body,
        grid=(steps,),
        in_specs=[s for _, s in rows] + [_whole(p.shape) for p in params] + [s for _, s in aux],
        out_specs=[e[1] for e in outs],
        out_shape=[jax.ShapeDtypeStruct(e[0], e[2] if len(e) > 2 else F32) for e in outs],
        name=name + "_fwd",
        compiler_params=_params("parallel"),
    )(*[a for a, _ in rows], *params, *[a for a, _ in aux])


def _vjp_call(name, fn, steps, rows, params, aux, cts, row_grads, into=None):
    nr, npar, na, nc = len(rows), len(params), len(aux), len(cts)
    n_in = nr + npar + na + nc
    lead = 0 if into is None else 1

    def body(*refs):
        refs = refs[lead:]
        vals = [x[...].astype(F32) for x in refs[:n_in]]
        r, p, a, d = vals[:nr], vals[nr:nr + npar], vals[nr + npar:nr + npar + na], vals[nr + npar + na:]
        out_refs = refs[n_in:]
        _, vjp = jax.vjp(lambda r_, p_: tuple(fn(r_, p_, a)), r, p)
        dr, dp = vjp(tuple(d))
        for o_ref, o in zip(out_refs[:nr], dr):
            o_ref[...] = o.astype(o_ref.dtype)

        @pl.when(pl.program_id(0) == 0)
        def _():
            for o_ref in out_refs[nr:]:
                o_ref[...] = jnp.zeros_like(o_ref)

        for o_ref, o in zip(out_refs[nr:], dp):
            o_ref[...] += o

    outs = pl.pallas_call(
        body,
        grid=(steps,),
        in_specs=([] if into is None else [pl.BlockSpec(memory_space=pl.ANY)])
        + [s for _, s in rows] + [_whole(p.shape) for p in params] + [s for _, s in aux] + [s for _, s in cts],
        out_specs=[e[1] for e in row_grads] + [_whole(p.shape) for p in params],
        out_shape=[jax.ShapeDtypeStruct(e[0], e[2] if len(e) > 2 else F32) for e in row_grads]
        + [jax.ShapeDtypeStruct(p.shape, F32) for p in params],
        input_output_aliases={} if into is None else {0: 0},
        name=name + "_bwd",
        compiler_params=_params("arbitrary"),
    )(*([] if into is None else [into]), *[a for a, _ in rows], *params, *[a for a, _ in aux], *[a for a, _ in cts])
    return list(outs[:nr]), list(outs[nr:])


def _norm_tile(r, p, a):
    return (_rms(r[0], p[0]),)


def _mm(a, b, mode, name, out_dtype=F32):
    if mode == "nn":
        (m, k), n = a.shape, b.shape[1]
    elif mode == "nt":
        (m, k), n = a.shape, b.shape[0]
    else:
        (k, m), n = a.shape, b.shape[1]
    tn = _pick(n, (1024, 768, 512, 384, 256, 128))
    budget = V7X_VMEM_LIMIT * 3 // 4
    out_bytes = 4 + 2 * np.dtype(out_dtype).itemsize

    def tiles():
        for tm in (2048, 1024, 512, 256, 128):
            for tk in (512, 256, 128) if mode == "tn" else (4096, 1024, 768, 512, 384, 256, 128):
                need = 2 * tk * (tm * a.dtype.itemsize + tn * b.dtype.itemsize) + tm * tn * out_bytes
                if m % tm == 0 and k % tk == 0 and need <= budget:
                    return tm, tk
        return _pick(m, (128,)), _pick(k, (128,))

    tm, tk = tiles()
    nk = k // tk

    def body(*refs):
        a_ref, b_ref, o_ref, acc_ref = refs
        kk = pl.program_id(2)
        if mode == "nn":
            part = _dot(a_ref[...], b_ref[...], 1, 0)
        elif mode == "nt":
            part = _dot(a_ref[...], b_ref[...], 1, 1)
        else:
            part = _dot(a_ref[...], b_ref[...], 0, 0)
        if nk == 1:
            o_ref[...] = part.astype(o_ref.dtype)
        else:
            @pl.when(kk == 0)
            def _():
                acc_ref[...] = part

            @pl.when(kk > 0)
            def _():
                acc_ref[...] += part

            @pl.when(kk == nk - 1)
            def _():
                o_ref[...] = acc_ref[...].astype(o_ref.dtype)

    if mode == "nn":
        a_spec = pl.BlockSpec((tm, tk), lambda i, j, kk: (i, kk))
        b_spec = pl.BlockSpec((tk, tn), lambda i, j, kk: (kk, j))
    elif mode == "nt":
        a_spec = pl.BlockSpec((tm, tk), lambda i, j, kk: (i, kk))
        b_spec = pl.BlockSpec((tn, tk), lambda i, j, kk: (j, kk))
    else:
        a_spec = pl.BlockSpec((tk, tm), lambda i, j, kk: (kk, i))
        b_spec = pl.BlockSpec((tk, tn), lambda i, j, kk: (kk, j))
    o_spec = pl.BlockSpec((tm, tn), lambda i, j, kk: (i, j))
    return pl.pallas_call(
        body,
        grid=(m // tm, n // tn, nk),
        in_specs=[a_spec, b_spec],
        out_specs=o_spec,
        out_shape=jax.ShapeDtypeStruct((m, n), out_dtype),
        scratch_shapes=[pltpu.VMEM((tm, tn), F32)],
        name=name,
        compiler_params=_params("parallel", "parallel", "arbitrary"),
    )(a, b)


def _dense_fwd_call(q, k, v, scale, name):
    n, sq, d = q.shape
    sk, dv = k.shape[1], v.shape[2]
    tq = _pick(sq, (512, 256, 128))
    c = scale * LOG2E

    nkb = 1

    def body(q_ref, k_ref, v_ref, o_ref, lse_ref, m_s, acc_s, vext_s):
        j = pl.program_id(2)

        @pl.when(j == 0)
        def _():
            m_s[...] = jnp.full_like(m_s, NEG_INF)
            acc_s[...] = jnp.zeros_like(acc_s)
            vext_s[...] = jnp.ones_like(vext_s)

        vext_s[:, :dv] = v_ref[0].astype(BF16)
        m_old = m_s[...]
        s = _dot(q_ref[0], k_ref[0], 1, 1)
        m_new = jnp.maximum(m_old, jnp.max(s, axis=1, keepdims=True))
        p = jnp.exp2(s * c - m_new * c)
        acc = jnp.exp2((m_old - m_new) * c) * acc_s[...] + _dot(p, vext_s[...], 1, 0)
        m_s[...] = m_new
        acc_s[...] = acc

        @pl.when(j == nkb - 1)
        def _():
            l = acc[:, dv:dv + 1]
            o_ref[0] = acc[:, :dv] / l
            lse_ref[0] = m_new * scale + jnp.log(l)

    return pl.pallas_call(
        body,
        grid=(n, sq // tq, nkb),
        in_specs=[
            pl.BlockSpec((1, tq, d), lambda h, i, j: (h, i, 0)),
            pl.BlockSpec((1, sk // nkb, d), lambda h, i, j: (h, j, 0)),
            pl.BlockSpec((1, sk // nkb, dv), lambda h, i, j: (h, j, 0)),
        ],
        out_specs=[
            pl.BlockSpec((1, tq, dv), lambda h, i, j: (h, i, 0)),
            pl.BlockSpec((1, tq, 1), lambda h, i, j: (h, i, 0)),
        ],
        out_shape=[jax.ShapeDtypeStruct((n, sq, dv), F32), jax.ShapeDtypeStruct((n, sq, 1), F32)],
        scratch_shapes=[pltpu.VMEM((tq, 1), F32), pltpu.VMEM((tq, 2 * dv), F32), pltpu.VMEM((sk // nkb, 2 * dv), BF16)],
        name=name + "_fwd",
        compiler_params=_params("parallel", "parallel", "arbitrary"),
    )(q, k, v)


def _dense_bwd_call(q, k, v, o, lse, do, scale, name):
    n, sq, d = q.shape
    sk, dv = k.shape[1], v.shape[2]
    tq, tk = _pick(sq, (1024, 512, 256, 128)), _pick(sk, (2048, 1024, 512, 256, 128))
    c = scale * LOG2E

    def body(q_ref, k_ref, v_ref, o_ref, lse_ref, do_ref, dq_ref, dk_ref, dv_ref):
        j, i = pl.program_id(1), pl.program_id(2)
        qb, kb, vb = q_ref[0].astype(BF16), k_ref[0].astype(BF16), v_ref[0].astype(BF16)
        do_f = do_ref[0]
        dob = do_f.astype(BF16)
        p = jnp.exp2(_dot(qb, kb, 1, 1) * c - lse_ref[0] * LOG2E)
        delta = jnp.sum(do_f * o_ref[0], axis=1, keepdims=True)
        ds = (p * (_dot(dob, vb, 1, 1) - delta)).astype(BF16)
        dv_part = _dot(p, dob, 0, 0)
        dk_part = _dot(ds, qb, 0, 0) * scale
        dq_part = _dot(ds, kb, 1, 0) * scale
        rows = pl.ds(pl.multiple_of(i * tq, tq), tq)

        @pl.when(i == 0)
        def _():
            dk_ref[0] = dk_part
            dv_ref[0] = dv_part

        @pl.when(i > 0)
        def _():
            dk_ref[0] += dk_part
            dv_ref[0] += dv_part

        @pl.when(j == 0)
        def _():
            dq_ref[0, rows, :] = dq_part

        @pl.when(j > 0)
        def _():
            dq_ref[0, rows, :] += dq_part

    return pl.pallas_call(
        body,
        grid=(n, sk // tk, sq // tq),
        in_specs=[
            pl.BlockSpec((1, tq, d), lambda h, j, i: (h, i, 0)),
            pl.BlockSpec((1, tk, d), lambda h, j, i: (h, j, 0)),
            pl.BlockSpec((1, tk, dv), lambda h, j, i: (h, j, 0)),
            pl.BlockSpec((1, tq, dv), lambda h, j, i: (h, i, 0)),
            pl.BlockSpec((1, tq, 1), lambda h, j, i: (h, i, 0)),
            pl.BlockSpec((1, tq, dv), lambda h, j, i: (h, i, 0)),
        ],
        out_specs=[
            pl.BlockSpec((1, sq, d), lambda h, j, i: (h, 0, 0)),
            pl.BlockSpec((1, tk, d), lambda h, j, i: (h, j, 0)),
            pl.BlockSpec((1, tk, dv), lambda h, j, i: (h, j, 0)),
        ],
        out_shape=[
            jax.ShapeDtypeStruct((n, sq, d), F32),
            jax.ShapeDtypeStruct((n, sk, d), F32),
            jax.ShapeDtypeStruct((n, sk, dv), F32),
        ],
        name=name + "_bwd",
        compiler_params=_params("arbitrary", "arbitrary", "arbitrary"),
    )(q, k, v, o, lse, do)


def _head_geometry(h, group):
    pair, a = divmod(h, 2)
    kv_pair, b = divmod(h // group, 2)
    return pair, a, kv_pair, b


def _lane_half():
    return lax.broadcasted_iota(jnp.int32, (1, LANES), 1) // HALF


def _align(x, a, b):
    if a != b:
        x = pltpu.roll(x, HALF, 1)
    return jnp.where(_lane_half() == b, x, 0.0)


def _unalign(x, a, b):
    x = jnp.where(_lane_half() == b, x, 0.0)
    return pltpu.roll(x, HALF, 1) if a != b else x


def _bands(w, pw, nw, lo, kvw, nb):
    b = BAND_BLOCK
    cat = jnp.concatenate([pw[:, lo:lo + kvw], w[:, lo:lo + kvw], nw[:, lo:lo + kvw]], axis=0).astype(BF16)
    out = []
    for g in range(kvw // LANES):
        c3 = cat[:, g * LANES:(g + 1) * LANES].reshape(nb + 2, b, LANES)
        out.append(jnp.concatenate([c3[0:nb], c3[1:nb + 1], c3[2:nb + 2]], axis=1))
    return out


def _edge_mask(first_block, nb, period):
    b = BAND_BLOCK
    blk = (first_block + lax.broadcasted_iota(jnp.int32, (nb, 1, 3 * b), 0)) % period
    col = lax.broadcasted_iota(jnp.int32, (nb, 1, 3 * b), 2)
    outside = ((col < b) & (blk == 0)) | ((col >= 2 * b) & (blk == period - 1))
    return jnp.where(outside, NEG_INF, 0.0)


def _band_geometry(proj, dil):
    rows = proj.shape[0]
    tl = _pick(rows, (1024, 512, 256, 128))
    return rows, tl, tl // BAND_BLOCK, rows // tl, rows // dil // BAND_BLOCK


def _band_in_specs(tl, nb, n_chunks, n_blocks, col, last_step_idle):
    def chunk(i):
        return jnp.minimum(i, n_chunks - 1) if last_step_idle else i

    main = pl.BlockSpec((tl, BAND_W), lambda j, i: (j * n_chunks + chunk(i), col))
    prev = pl.BlockSpec((BAND_BLOCK, BAND_W),
                        lambda j, i: (j * n_blocks + jnp.maximum(chunk(i) * nb - 1, 0), col))
    nxt = pl.BlockSpec((BAND_BLOCK, BAND_W),
                       lambda j, i: (j * n_blocks + jnp.minimum((chunk(i) + 1) * nb, n_blocks - 1), col))
    rows = pl.BlockSpec((tl, QW), lambda j, i: (j * n_chunks + chunk(i), 0))
    return main, prev, nxt, rows


def _band_fwd_call(proj, col, bias, sink, dil, group, kvw, scale, name):
    s_tok = proj.shape[0]
    seq, tl, nb, n_chunks, period = _band_geometry(proj, dil)
    n_blocks = seq // BAND_BLOCK
    heads = bias.shape[0]

    def body(w_ref, pw_ref, nw_ref, bias_ref, sink_ref, o_ref, lse_ref):
        i = pl.program_id(1)
        w, pw, nw = w_ref[...].astype(F32), pw_ref[...].astype(F32), nw_ref[...].astype(F32)
        kb = _bands(w, pw, nw, QW, kvw, nb)
        vb = _bands(w, pw, nw, QW + kvw, kvw, nb)
        edge = _edge_mask(i * nb, nb, period)
        o_acc = [jnp.zeros((tl, LANES), F32) for _ in range(heads // 2)]
        lse_acc = [jnp.zeros((tl, LANES), F32) for _ in range(heads // 2)]
        geom = [_head_geometry(h, group) for h in range(heads)]
        logits = []
        for h, (pair, a, kvp, b) in enumerate(geom):
            q_al = _align(w[:, pair * LANES:(pair + 1) * LANES], a, b).astype(BF16).reshape(nb, BAND_BLOCK, LANES)
            logits.append(_bmm(q_al, kb[kvp], 2, 2) * scale + bias_ref[h][None] + edge)
        es, ssums, ms = [], [], []
        for h in range(heads):
            sk = sink_ref[h].reshape(1, 1, 1)
            m = jnp.maximum(jnp.max(logits[h], axis=2, keepdims=True), sk)
            e = jnp.exp(logits[h] - m)
            es.append(e.astype(BF16))
            ssums.append(jnp.sum(e, axis=2, keepdims=True) + jnp.exp(sk - m))
            ms.append(m)
        for h, (pair, a, kvp, b) in enumerate(geom):
            out = _bmm(es[h], vb[kvp], 2, 1) / ssums[h]
            o_acc[pair] = o_acc[pair] + _unalign(out.reshape(tl, LANES), a, b)
            lse = (ms[h] + jnp.log(ssums[h])).reshape(tl, 1)
            lse_acc[pair] = lse_acc[pair] + jnp.where(_lane_half() == a, lse, 0.0)
        o_ref[...] = jnp.concatenate(o_acc, axis=1)
        lse_ref[...] = jnp.concatenate(lse_acc, axis=1)

    main, prev, nxt, rows = _band_in_specs(tl, nb, n_chunks, n_blocks, col, False)
    return pl.pallas_call(
        body,
        grid=(1, n_chunks),
        in_specs=[main, prev, nxt, pl.BlockSpec(bias.shape, lambda j, i: (0, 0, 0)),
                  pl.BlockSpec(sink.shape, lambda j, i: (0, 0, 0))],
        out_specs=[rows, rows],
        out_shape=[jax.ShapeDtypeStruct((s_tok, QW), F32)] * 2,
        name=name + "_fwd",
        compiler_params=_params("parallel", "parallel"),
    )(proj, proj, proj, bias, sink)


def _band_bwd_call(proj, o, do, lse, dlse, bias, sink, dproj, col, dil, group, kvw, scale, name):
    seq, tl, nb, n_chunks, period = _band_geometry(proj, dil)
    lead = 0 if dproj is None else 1
    n_blocks = seq // BAND_BLOCK
    heads = bias.shape[0]
    b_ = BAND_BLOCK
    have_dlse = dlse is not None

    def body(*refs):
        (w_ref, pw_ref, nw_ref, o_ref, do_ref, lse_ref), refs = refs[lead:lead + 6], refs[lead + 6:]
        if have_dlse:
            dlse_ref, refs = refs[0], refs[1:]
        bias_ref, sink_ref, dwin_ref, dbias_ref, dsink_ref, dq_s, dk_s, dv_s = refs
        j, i = pl.program_id(0), pl.program_id(1)

        @pl.when((j == 0) & (i == 0))
        def _():
            dbias_ref[...] = jnp.zeros_like(dbias_ref)
            dsink_ref[...] = jnp.zeros_like(dsink_ref)

        @pl.when(i == 0)
        def _():
            dk_s[...] = jnp.zeros_like(dk_s)
            dv_s[...] = jnp.zeros_like(dv_s)

        @pl.when(i < n_chunks)
        def _():
            w, pw, nw = w_ref[...].astype(F32), pw_ref[...].astype(F32), nw_ref[...].astype(F32)
            kb = _bands(w, pw, nw, QW, kvw, nb)
            vb = _bands(w, pw, nw, QW + kvw, kvw, nb)
            edge = _edge_mask(i * nb, nb, period)
            dq_acc = [jnp.zeros((tl, LANES), F32) for _ in range(heads // 2)]
            for h in range(heads):
                pair, a, kvp, b = _head_geometry(h, group)
                lanes = slice(pair * LANES, (pair + 1) * LANES)
                mine = _lane_half() == a
                q_al = _align(w[:, lanes], a, b).astype(BF16).reshape(nb, b_, LANES)
                do_al = _align(do_ref[:, lanes], a, b).astype(BF16).reshape(nb, b_, LANES)
                lse_h = jnp.max(jnp.where(mine, lse_ref[:, lanes], NEG_INF), axis=1, keepdims=True)
                shift = -jnp.sum(jnp.where(mine, do_ref[:, lanes] * o_ref[:, lanes], 0.0), axis=1, keepdims=True)
                if have_dlse:
                    shift = shift + jnp.sum(jnp.where(mine, dlse_ref[:, lanes], 0.0), axis=1, keepdims=True)
                logits = _bmm(q_al, kb[kvp], 2, 2) * scale + bias_ref[h][None] + edge
                p = jnp.exp(logits - lse_h.reshape(nb, b_, 1))
                dlogits = p * (_bmm(do_al, vb[kvp], 2, 2) + shift.reshape(nb, b_, 1))
                dbias_ref[h] += jnp.sum(dlogits, axis=0)
                dsink_ref[h] += jnp.sum(jnp.exp(sink_ref[h] - lse_h) * shift, axis=0, keepdims=True)
                ds = (dlogits * scale).astype(BF16)
                dq_acc[pair] = dq_acc[pair] + _unalign(_bmm(ds, kb[kvp], 2, 1).reshape(tl, LANES), a, b)
                dk_band = _bmm(ds, q_al, 1, 1)
                dv_band = _bmm(p.astype(BF16), do_al, 1, 1)
                kv_lanes = slice(kvp * LANES, (kvp + 1) * LANES)
                for t in range(3):
                    at = pl.ds(pl.multiple_of(i * tl + t * b_, b_), tl)
                    dk_s[at, kv_lanes] += dk_band[:, t * b_:(t + 1) * b_, :].reshape(tl, LANES)
                    dv_s[at, kv_lanes] += dv_band[:, t * b_:(t + 1) * b_, :].reshape(tl, LANES)
            dq_s[lax.rem(i, 2)] = jnp.concatenate(dq_acc, axis=1)

        @pl.when(i >= 1)
        def _():
            at = pl.ds(pl.multiple_of((i - 1) * tl + b_, b_), tl)
            parts = [dq_s[lax.rem(i + 1, 2)], dk_s[at, :], dv_s[at, :]]
            if QW + 2 * kvw < BAND_W:
                parts.append(jnp.zeros((tl, BAND_W - QW - 2 * kvw), F32))
            dwin_ref[...] = jnp.concatenate(parts, axis=1).astype(dwin_ref.dtype)

    main, prev, nxt, rows = _band_in_specs(tl, nb, n_chunks, n_blocks, col, True)
    row_args = [o, do, lse] + ([dlse] if have_dlse else [])
    small = [pl.BlockSpec(bias.shape, lambda j, i: (0, 0, 0)), pl.BlockSpec(sink.shape, lambda j, i: (0, 0, 0))]
    return pl.pallas_call(
        body,
        grid=(1, n_chunks + 1),
        in_specs=[pl.BlockSpec(memory_space=pl.ANY)] * lead + [main, prev, nxt] + [rows] * len(row_args) + small,
        out_specs=[pl.BlockSpec((tl, BAND_W), lambda j, i: (j * n_chunks + jnp.maximum(i - 1, 0), col))] + small,
        out_shape=[jax.ShapeDtypeStruct(proj.shape, BF16), jax.ShapeDtypeStruct(bias.shape, F32),
                   jax.ShapeDtypeStruct(sink.shape, F32)],
        scratch_shapes=[pltpu.VMEM((2, tl, QW), F32), pltpu.VMEM((seq + 2 * b_, kvw), F32),
                        pltpu.VMEM((seq + 2 * b_, kvw), F32)],
        input_output_aliases={0: 0} if lead else {},
        name=name + "_bwd",
        compiler_params=_params("arbitrary", "arbitrary"),
    )(*([dproj] if lead else []), proj, proj, proj, *row_args, bias, sink)


def _loss_call(x, target, g):
    s, d = x.shape
    tr = _pick(s, (256, 128, 64, 32, 16, 8))

    def tile_loss(xt, gt, tt):
        err = jnp.square(_rms(xt, gt) - tt)
        return 0.5 * jnp.sum(jnp.mean(err, axis=-1, keepdims=True), axis=0, keepdims=True)

    def body(x_ref, t_ref, g_ref, loss_ref, dx_ref, dg_ref):
        tt = t_ref[...]
        val, vjp = jax.vjp(lambda xt, gt: tile_loss(xt, gt, tt), x_ref[...], g_ref[...])
        dx, dg = vjp(jnp.ones_like(val))
        dx_ref[...] = dx

        @pl.when(pl.program_id(0) == 0)
        def _():
            loss_ref[...] = jnp.zeros_like(loss_ref)
            dg_ref[...] = jnp.zeros_like(dg_ref)

        loss_ref[...] += val
        dg_ref[...] += dg

    return pl.pallas_call(
        body,
        grid=(s // tr,),
        in_specs=[_rows(tr, d), _rows(tr, d), _whole((1, d))],
        out_specs=[_whole((1, 1)), _rows(tr, d), _whole((1, d))],
        out_shape=[jax.ShapeDtypeStruct((1, 1), F32), jax.ShapeDtypeStruct((s, d), F32),
                   jax.ShapeDtypeStruct((1, d), F32)],
        name="final_norm_loss",
        compiler_params=_params("arbitrary"),
    )(x, target, g)


@jax.custom_vjp
def _loss_op(x, target, g):
    return _loss_call(x, target, g)[0][0, 0]


def _loss_op_fwd(x, target, g):
    loss, dx, dg = _loss_call(x, target, g)
    return loss[0, 0], (dx, dg, target)


def _loss_op_bwd(res, ct):
    dx, dg, target = res
    return ct * dx, jnp.zeros_like(target), ct * dg


_loss_op.defvjp(_loss_op_fwd, _loss_op_bwd)


def _mla_tile(r, p, a):
    g_q, g_kv, w_q, w_k, w_v = p
    cos_t, sin_t, place_kr = a
    a_q, a_kv, a_kr = _lanes(r[0], (0, MLA_Q_LORA, MLA_Q_LORA + MLA_KV_LORA, MLA_Q_LORA + MLA_KV_LORA + MLA_ROPE))
    q = _rope(_bdot(_rms(a_q, g_q), w_q), cos_t, sin_t, MLA_ROPE // 2)
    ckv = _rms(a_kv, g_kv)
    k = _rope(_bdot(ckv, w_k) + _hdot(a_kr, place_kr), cos_t, sin_t, MLA_ROPE // 2)
    return _split_heads(q, MLA_HEADS), _split_heads(k, MLA_HEADS), _split_heads(_bdot(ckv, w_v), MLA_HEADS)


def _head_rms(x, g, head_mean):
    return x * lax.rsqrt(_hdot(x * x, head_mean) + EPS) * g


def _gqa_tile(r, p, a):
    g_q, g_k = p
    cos_t, sin_t, mean_q, mean_k = a
    wq, wk = GQA_HEADS * HEAD_DIM, GQA_KV_HEADS * HEAD_DIM
    b_q, b_k, b_v = _lanes(r[0], (0, wq, wq + wk, wq + 2 * wk))
    q = _rope(_head_rms(b_q, g_q, mean_q), cos_t, sin_t, HEAD_DIM // 4)
    k = _rope(_head_rms(b_k, g_k, mean_k), cos_t[:, :wk], sin_t[:, :wk], HEAD_DIM // 4)
    return _split_heads(q, GQA_HEADS), _split_heads(k, GQA_KV_HEADS), _split_heads(b_v, GQA_KV_HEADS)


def _permute_rows(p, x, cp):
    pb = p.astype(BF16)
    hi = x.astype(BF16)
    rest = x - hi.astype(F32)
    mid = rest.astype(BF16)
    low = (rest - mid.astype(F32)).astype(BF16)
    dims = (((cp,), (0,)), ((), ()))
    return (lax.dot_general(pb, hi, dims, preferred_element_type=F32)
            + lax.dot_general(pb, mid, dims, preferred_element_type=F32)
            + lax.dot_general(pb, low, dims, preferred_element_type=F32))


@jax.custom_vjp
def _permuted(p, x):
    return _permute_rows(p, x, 1)


def _permuted_fwd(p, x):
    return _permute_rows(p, x, 1), p


def _permuted_bwd(p, ct):
    return jnp.zeros_like(p), _permute_rows(p, ct, 0)


_permuted.defvjp(_permuted_fwd, _permuted_bwd)


def _interleave(p, x):
    return _permuted(p, x.reshape(x.shape[0] * x.shape[1], x.shape[2]))


def _interleave_matrix(rows, dil):
    p = np.zeros((rows, rows), np.float32)
    for t in range(rows):
        p[t, (t % dil) * (rows // dil) + t // dil] = 1.0
    return p


def _merge_tile(r, p, a):
    gm, o_a, o_b, oc0, oc1, oc2, l0, l1, l2, o_d = r
    (w_branch,) = p
    perm1, perm2 = a
    oc1, l1, oc2, l2 = _interleave(perm1, oc1), _interleave(perm1, l1), _interleave(perm2, oc2), _interleave(perm2, l2)
    d = w_branch.shape[2]
    gate_path, merge_logits = _lanes(gm, (0, N_BRANCH * BRANCH_W, N_BRANCH * BRANCH_W + N_BRANCH * d))
    m = jnp.maximum(jnp.maximum(l0, l1), l2)
    e0, e1, e2 = jnp.exp(l0 - m), jnp.exp(l1 - m), jnp.exp(l2 - m)
    y_c = (e0 * oc0 + e1 * oc1 + e2 * oc2) / (e0 + e1 + e2)
    y = jnp.concatenate([_join_heads(o_a), _join_heads(o_b), y_c, o_d], axis=1)
    u = y * (gate_path * jax.nn.sigmoid(gate_path))
    gates = _lanes(merge_logits, tuple(range(0, N_BRANCH * d + 1, d)))
    us = _lanes(u, tuple(range(0, N_BRANCH * BRANCH_W + 1, BRANCH_W)))
    branch_w = _unstack(w_branch)
    out = None
    for nb in range(N_BRANCH):
        term = jax.nn.sigmoid(gates[nb]) * _bdot(us[nb], branch_w[nb])
        out = term if out is None else out + term
    return (out,)


def _mixer_calls(proj, prm, aux):
    s = proj.shape[0]
    tr, tm = _pick(s, (512, 256, 128)), _pick(s, (256,))
    mla_cos, mla_sin, gqa_cos, gqa_sin, place_kr, mean_q, mean_k = aux[:7]
    wq = MLA_HEADS * MLA_QK
    mla = dict(
        steps=s // tr, rows=[(proj, _rows(tr, SMALL_W, MLA_BLK))],
        params=[prm["g_q"], prm["g_kv"], prm["w_q"], prm["w_k"], prm["w_v"]],
        aux=[(mla_cos, _rows(tr, wq)), (mla_sin, _rows(tr, wq)), (place_kr, _whole(place_kr.shape))],
        outs=[((MLA_HEADS, s, MLA_QK), _head_rows(MLA_HEADS, tr, MLA_QK), BF16)] * 2
        + [((MLA_HEADS, s, MLA_V), _head_rows(MLA_HEADS, tr, MLA_V), BF16)],
        window=((s, P_TOT), _rows(tr, SMALL_W, MLA_BLK), BF16))
    wg = GQA_HEADS * HEAD_DIM
    gqa = dict(
        steps=s // tr, rows=[(proj, _rows(tr, SMALL_W, GQA_BLK))], params=[prm["gq"], prm["gk"]],
        aux=[(gqa_cos, _rows(tr, wg)), (gqa_sin, _rows(tr, wg)), (mean_q, _whole(mean_q.shape)),
             (mean_k, _whole(mean_k.shape))],
        outs=[((GQA_HEADS, s, HEAD_DIM), _head_rows(GQA_HEADS, tr, HEAD_DIM), BF16)]
        + [((GQA_KV_HEADS, s, HEAD_DIM), _head_rows(GQA_KV_HEADS, tr, HEAD_DIM), BF16)] * 2,
        window=((s, P_TOT), _rows(tr, SMALL_W, GQA_BLK), BF16))
    merge = dict(steps=s // tm, tm=tm, window=((s, P_TOT), _rows(tm, GM_W, 0), BF16))
    return mla, gqa, merge


def _merge_rows(proj, o_a, o_b, ocs, lses, o_d, tm):
    h4 = _head_rows(4, tm, HEAD_DIM)
    s = proj.shape[0]

    def by_residue(t, dil):
        if dil == 1:
            return t, _rows(tm, QW)
        return t.reshape(dil, s // dil, QW), pl.BlockSpec((dil, tm // dil, QW), lambda i: (0, i, 0))

    dils = [dil for _, dil in DIL_PATTERNS]
    return ([(proj, _rows(tm, GM_W, 0)), (o_a, h4), (o_b, h4)] + [by_residue(t, r) for t, r in zip(ocs, dils)]
            + [by_residue(t, r) for t, r in zip(lses, dils)] + [(o_d, _rows(tm, QW))])


def _merge_aux(aux):
    return [(t, _whole(t.shape)) for t in aux[7:9]]


def _to_residues(t, dil):
    s, w = t.shape
    return t if dil == 1 else t.reshape(s // dil, dil, w).transpose(1, 0, 2).reshape(s, w)


def _from_residues(t, dil):
    s, w = t.shape
    return t if dil == 1 else t.reshape(dil, s // dil, w).transpose(1, 0, 2).reshape(s, w)


def _mixer_fwd(projs, prm, aux):
    proj = projs[0]
    s = proj.shape[0]
    mla, gqa, merge = _mixer_calls(proj, prm, aux)
    q_a, k_a, v_a = _fwd_call("prep_mla", _mla_tile, mla["steps"], mla["rows"], mla["params"], mla["aux"], mla["outs"])
    o_a, lse_a = _dense_fwd_call(q_a, k_a, v_a, MLA_QK ** -0.5, "mla")
    q_b, k_b, v_b = _fwd_call("prep_gqa", _gqa_tile, gqa["steps"], gqa["rows"], gqa["params"], gqa["aux"], gqa["outs"])
    grp = GQA_HEADS // GQA_KV_HEADS
    o_b, lse_b = _dense_fwd_call(q_b.reshape(GQA_KV_HEADS, grp * s, HEAD_DIM), k_b, v_b, HEAD_DIM ** -0.5, "gqa")
    scale = HEAD_DIM ** -0.5
    ocs, lses = [], []
    for gi, (_, dil) in enumerate(DIL_PATTERNS):
        o, lse = _band_fwd_call(projs[gi], DIL_BLK if gi == 0 else 0, prm["bias_dil"][gi], prm["no_sink"], dil, 1,
                                QW, scale, "dil%d" % gi)
        ocs.append(o)
        lses.append(lse)
    o_d, lse_d = _band_fwd_call(proj, WIN_BLK, prm["bias_win"], prm["sink"], 1, WIN_HEADS // WIN_KV_HEADS,
                                WIN_KV_HEADS * HEAD_DIM, scale, "win")
    rows = _merge_rows(proj, o_a, o_b.reshape(GQA_HEADS, s, HEAD_DIM), ocs, lses, o_d, merge["tm"])
    mix = _fwd_call("merge", _merge_tile, merge["steps"], rows, [prm["w_branch"]], _merge_aux(aux),
                    [((s, prm["w_branch"].shape[2]), _rows(merge["tm"], prm["w_branch"].shape[2]), BF16)])[0]
    return mix, (q_a, k_a, v_a, o_a, lse_a, q_b, k_b, v_b, o_b, lse_b, ocs, lses, o_d, lse_d)


def _mixer_bwd(projs, prm, aux, saved, dmix):
    proj = projs[0]
    s = proj.shape[0]
    q_a, k_a, v_a, o_a, lse_a, q_b, k_b, v_b, o_b, lse_b, ocs, lses, o_d, lse_d = saved
    dils = [dil for _, dil in DIL_PATTERNS]
    mla, gqa, merge = _mixer_calls(proj, prm, aux)
    tm, d_model = merge["tm"], prm["w_branch"].shape[2]
    grp = GQA_HEADS // GQA_KV_HEADS
    scale = HEAD_DIM ** -0.5

    rows = _merge_rows(proj, o_a, o_b.reshape(GQA_HEADS, s, HEAD_DIM), ocs, lses, o_d, tm)
    grads, (dw_branch,) = _vjp_call(
        "merge", _merge_tile, merge["steps"], rows, [prm["w_branch"]], _merge_aux(aux), [(dmix, _rows(tm, d_model))],
        [merge["window"]] + [(a.shape, spec) for a, spec in rows[1:]])
    dproj, do_a, do_b, docs, dlses, do_d = grads[0], grads[1], grads[2], grads[3:6], grads[6:9], grads[9]

    dq_a, dk_a, dv_a = _dense_bwd_call(q_a, k_a, v_a, o_a, lse_a, do_a, MLA_QK ** -0.5, "mla")
    (dproj,), dmla = _vjp_call("prep_mla", _mla_tile, mla["steps"], mla["rows"], mla["params"], mla["aux"],
                               [(t, e[1]) for t, e in zip((dq_a, dk_a, dv_a), mla["outs"])],
                               [mla["window"]], into=dproj)
    dq_b, dk_b, dv_b = _dense_bwd_call(q_b.reshape(GQA_KV_HEADS, grp * s, HEAD_DIM), k_b, v_b, o_b, lse_b,
                                       do_b.reshape(GQA_KV_HEADS, grp * s, HEAD_DIM), scale, "gqa")
    (dproj,), dgqa = _vjp_call("prep_gqa", _gqa_tile, gqa["steps"], gqa["rows"], gqa["params"], gqa["aux"],
                               [(t, e[1]) for t, e in zip((dq_b.reshape(GQA_HEADS, s, HEAD_DIM), dk_b, dv_b), gqa["outs"])],
                               [gqa["window"]], into=dproj)
    dproj, dbias_win, dsink = _band_bwd_call(proj, o_d, do_d, lse_d, None, prm["bias_win"], prm["sink"], dproj,
                                             WIN_BLK, 1, WIN_HEADS // WIN_KV_HEADS, WIN_KV_HEADS * HEAD_DIM, scale, "win")
    dbias_dil, dprojs = [], []
    for gi, dil in enumerate(dils):
        dside, dbias, _ = _band_bwd_call(
            projs[gi], ocs[gi], docs[gi].reshape(s, QW), lses[gi], dlses[gi].reshape(s, QW),
            prm["bias_dil"][gi], prm["no_sink"], dproj if gi == 0 else None, DIL_BLK if gi == 0 else 0, dil, 1, QW,
            scale, "dil%d" % gi)
        if gi == 0:
            dproj = dside
        else:
            dprojs.append(dside)
        dbias_dil.append(dbias)
    dprm = dict(g_q=dmla[0], g_kv=dmla[1], w_q=dmla[2], w_k=dmla[3], w_v=dmla[4], gq=dgqa[0], gk=dgqa[1],
                bias_dil=dbias_dil, bias_win=dbias_win, sink=dsink, no_sink=jnp.zeros_like(prm["no_sink"]),
                w_branch=dw_branch)
    return [dproj] + dprojs, {k: jax.tree.map(lambda g, p: g.astype(p.dtype), v, prm[k]) for k, v in dprm.items()}


def _layer_fwd(x, w, aux):
    s, d = x.shape
    tr = _pick(s, (256,))
    dils = [dil for _, dil in DIL_PATTERNS]

    def norm_forms(r, p, a):
        y = _rms(r[0], p[0])
        return [y, y.T] + [_dot(q, y, 1, 0).reshape(dil, tr // dil, d) for q, dil in zip(a, dils[1:])]

    forms = _fwd_call(
        "norm", norm_forms, s // tr, [(x, _rows(tr, d))], [w["norm_g"]], [(q, _whole(q.shape)) for q in aux[9:11]],
        [((s, d), _rows(tr, d), BF16), ((d, s), pl.BlockSpec((d, tr), lambda i: (0, i)), BF16)]
        + [((dil, s // dil, d), pl.BlockSpec((dil, tr // dil, d), lambda i: (0, i, 0)), BF16) for dil in dils[1:]])
    xn_t, xns = forms[1], [forms[0]] + [t.reshape(s, d) for t in forms[2:]]
    projs = [_mm(a, b, "nt", "proj%d_fwd" % i, BF16) for i, (a, b) in enumerate(zip(xns, w["w_in_t"]))]
    mix, saved = _mixer_fwd(projs, w["mixer"], aux)
    return _mm(mix, w["w_out"], "nn", "out_proj_nn"), (x, w, aux, xns, xn_t, projs, mix, saved)


@jax.custom_vjp
def _layer_core(x, w, aux):
    return _layer_fwd(x, w, aux)[0]


def _layer_core_bwd(res, dout):
    x, w, aux, xns, xn_t, projs, mix, saved = res
    s, d = x.shape
    tr = _pick(s, (256, 128, 64, 32, 16, 8))
    dils = [dil for _, dil in DIL_PATTERNS]
    dmix = _mm(dout, w["w_out"], "nt", "out_proj_nt")
    dw_out = _mm(mix.T, dout, "nn", "out_proj_dw", w["w_out"].dtype)
    dprojs, dmixer = _mixer_bwd(projs, w["mixer"], aux, saved, dmix)
    side = jnp.concatenate([_from_residues(dp, r) for dp, r in zip(dprojs[1:], dils[1:])], axis=1)
    dxn_terms = [_mm(dprojs[0], w["w_in_t"][0], "nn", "proj0_dx"),
                 _mm(side, jnp.concatenate(w["w_in_t"][1:], axis=0), "nn", "proj_side_dx")]
    dw_in_t = [_mm(a_t, dp, "nn", "proj%d_dw" % i, wi.dtype).T
               for i, (a_t, dp, wi) in enumerate(zip([xn_t] + [a.T for a in xns[1:]], dprojs, w["w_in_t"]))]
    (dx,), (dg,) = _vjp_call("norm", lambda r, p, a: _norm_tile(r, p, a) * len(dxn_terms), s // tr, [(x, _rows(tr, d))],
                             [w["norm_g"]], [], [(t, _rows(tr, d)) for t in dxn_terms], [((s, d), _rows(tr, d))])
    dw = dict(norm_g=dg, w_in_t=dw_in_t, mixer=dmixer, w_out=dw_out)
    return dx, dw, tuple(jnp.zeros_like(t) for t in aux)


_layer_core.defvjp(lambda x, w, aux: _layer_fwd(x, w, aux), _layer_core_bwd)


def _rope_angles(pos, dim):
    inv = ROPE_THETA ** (-jnp.arange(0, dim, 2, dtype=F32) / dim)
    return pos.astype(F32)[:, None] * inv[None, :]


def _rope_tables(s):
    pos = jnp.arange(s, dtype=jnp.int32)
    rows = s // GRID_W
    row_idx = jnp.repeat(jnp.arange(rows, dtype=jnp.int32), GRID_W)
    col_idx = jnp.tile(jnp.arange(GRID_W, dtype=jnp.int32), rows)
    a1 = _rope_angles(pos, MLA_ROPE)
    ar = _rope_angles(row_idx, HEAD_DIM // 2)
    ac = _rope_angles(col_idx, HEAD_DIM // 2)
    ones, zeros = jnp.ones((s, MLA_NOPE), F32), jnp.zeros((s, MLA_NOPE), F32)
    mla_cos = jnp.tile(jnp.concatenate([ones, jnp.cos(a1), jnp.cos(a1)], axis=1), (1, MLA_HEADS))
    mla_sin = jnp.tile(jnp.concatenate([zeros, -jnp.sin(a1), jnp.sin(a1)], axis=1), (1, MLA_HEADS))
    gqa_cos = jnp.tile(jnp.concatenate([jnp.cos(ar), jnp.cos(ar), jnp.cos(ac), jnp.cos(ac)], axis=1), (1, GQA_HEADS))
    gqa_sin = jnp.tile(jnp.concatenate([-jnp.sin(ar), jnp.sin(ar), -jnp.sin(ac), jnp.sin(ac)], axis=1), (1, GQA_HEADS))
    return mla_cos, mla_sin, gqa_cos, gqa_sin


def _t5_bucket(rel):
    nb = T5_BUCKETS // 2
    max_exact = nb // 2
    n = jnp.abs(rel)
    nf = jnp.maximum(n, 1).astype(F32)
    large = max_exact + (jnp.log(nf / max_exact) / math.log(T5_MAX_DIST / max_exact) * (nb - max_exact)).astype(jnp.int32)
    large = jnp.minimum(large, nb - 1)
    return jnp.where(rel > 0, nb, 0) + jnp.where(n < max_exact, n, large)


def _band_bias(table, stride, head_lo, heads, half_window):
    b = BAND_BLOCK
    offs = jnp.arange(3 * b)[None, :] - b - jnp.arange(b)[:, None]
    one_hot = (_t5_bucket(offs * stride)[..., None] == jnp.arange(T5_BUCKETS)).astype(F32)
    bias = jnp.dot(one_hot.reshape(b * 3 * b, T5_BUCKETS), table[:, head_lo:head_lo + heads],
                   precision=lax.Precision.HIGHEST)
    bias = bias.T.reshape(heads, b, 3 * b)
    return jnp.where((jnp.abs(offs) <= half_window)[None], bias, NEG_INF)


def _w_in_rows(d):
    mla, gqa, win, dil0 = MLA_BLK * SMALL_W, GQA_BLK * SMALL_W, WIN_BLK * BAND_W, DIL_BLK * BAND_W
    plan, at = [], 0
    for width, target, row in ((256, 0, mla), (128, 0, mla + 256), (32, 0, mla + 384),
                               (256, 0, gqa), (128, 0, gqa + 256), (128, 0, gqa + 384)):
        plan.append((at, width, target, row))
        at += width
    for part in range(3):
        for g in range(len(DIL_PATTERNS)):
            plan.append((at, QW, g, (dil0 if g == 0 else 0) + part * QW))
            at += QW
    for width, row in ((256, win), (128, win + 256), (128, win + 384), (N_BRANCH * BRANCH_W, 0),
                       (N_BRANCH * d, N_BRANCH * BRANCH_W)):
        plan.append((at, width, 0, row))
        at += width
    return plan


@jax.custom_vjp
def _w_in_layout(w_in_t):
    d = w_in_t.shape[1]
    outs = []
    for target, rows in enumerate((P_TOT, BAND_W, BAND_W)):
        parts, at = [], 0
        for start, width, _, row in sorted((p for p in _w_in_rows(d) if p[2] == target), key=lambda p: p[3]):
            if row > at:
                parts.append(jnp.zeros((row - at, d), w_in_t.dtype))
            parts.append(w_in_t[start:start + width])
            at = row + width
        if at < rows:
            parts.append(jnp.zeros((rows - at, d), w_in_t.dtype))
        outs.append(jnp.concatenate(parts, axis=0))
    return outs


def _w_in_layout_fwd(w_in_t):
    return _w_in_layout(w_in_t), None


def _w_in_layout_bwd(_, cts):
    d = cts[0].shape[1]
    return (jnp.concatenate([cts[target][row:row + width] for _, width, target, row in _w_in_rows(d)], axis=0),)


_w_in_layout.defvjp(_w_in_layout_fwd, _w_in_layout_bwd)


def _layer(x, w, l, aux, biases):
    w_kv = w["w_kv_t"][l].T.reshape(MLA_KV_LORA, MLA_HEADS, MLA_NOPE + MLA_V)
    w_k = jnp.concatenate([w_kv[:, :, :MLA_NOPE], jnp.zeros((MLA_KV_LORA, MLA_HEADS, MLA_ROPE), w_kv.dtype)], axis=2)
    dil_bias, win_bias = biases
    prm = dict(
        g_q=w["mla_q_norm_g"][l][None, :], g_kv=w["mla_kv_norm_g"][l][None, :], w_q=w["w_q_t"][l].T,
        w_k=w_k.reshape(MLA_KV_LORA, MLA_HEADS * MLA_QK),
        w_v=w_kv[:, :, MLA_NOPE:].reshape(MLA_KV_LORA, MLA_HEADS * MLA_V),
        gq=jnp.tile(w["gqa_q_norm_g"][l], GQA_HEADS)[None, :], gk=jnp.tile(w["gqa_k_norm_g"][l], GQA_KV_HEADS)[None, :],
        bias_dil=list(dil_bias), bias_win=win_bias, sink=w["win_sink"][l].reshape(WIN_HEADS, 1, 1),
        no_sink=jnp.full((DIL_HEADS, 1, 1), NEG_INF, F32), w_branch=jnp.transpose(w["w_branch_t"][l].reshape(-1, N_BRANCH, BRANCH_W), (1, 2, 0)))
    layer_w = dict(norm_g=w["norm_g"][l][None, :], w_in_t=_w_in_layout(w["w_in_t"][l]), mixer=prm, w_out=w["w_out"][l])
    return x + _layer_core(x, layer_w, aux)


def _local_loss(w, x, target):
    s, d_model = x.shape
    assert d_model == D_MODEL, "the projection's window layout is laid out for d_model 1024"
    place = np.zeros((MLA_ROPE, MLA_HEADS * MLA_QK), np.float32)
    for h in range(MLA_HEADS):
        for i in range(MLA_ROPE):
            place[i, h * MLA_QK + MLA_NOPE + i] = 1.0

    def head_mean(nh):
        m = np.kron(np.eye(nh, dtype=np.float32), np.full((HEAD_DIM, HEAD_DIM), 1.0 / HEAD_DIM, np.float32))
        return jnp.asarray(m)

    merge_tile = _pick(s, (256,))
    norm_tile = _pick(s, (256,))
    aux = _rope_tables(s) + (jnp.asarray(place), head_mean(GQA_HEADS), head_mean(GQA_KV_HEADS)) + tuple(
        jnp.asarray(_interleave_matrix(merge_tile, dil)) for _, dil in DIL_PATTERNS[1:]) + tuple(
        jnp.asarray(_interleave_matrix(norm_tile, dil).T) for _, dil in DIL_PATTERNS[1:])
    table = w["t5_table"]
    dil_bias = [_band_bias(table, dil, gi * DIL_HEADS, DIL_HEADS, window // (2 * dil))
                for gi, (window, dil) in enumerate(DIL_PATTERNS)]
    win_bias = _band_bias(table, 1, len(DIL_PATTERNS) * DIL_HEADS, WIN_HEADS, WIN_HALF)
    for l in range(w["norm_g"].shape[0]):
        x = _layer(x, w, l, aux, (dil_bias, win_bias))
    return _loss_op(x, target, w["final_norm_g"][None, :])


_ANY = pl.BlockSpec(memory_space=pl.ANY)
_MESH = pl.DeviceIdType.MESH


def _all_gather(block, name):
    def body(x_ref, out_ref, send_sems, recv_sems, local_sem):
        x, y, c = lax.axis_index("x"), lax.axis_index("y"), lax.axis_index("c")
        me, sibling = (x, y, c), (x, y, 1 - c)
        chips = [(1 - x, y), (x, 1 - y), (1 - x, 1 - y)]

        def slot(px, py, pc):
            return out_ref.at[4 * px + 2 * py + pc]

        def copy(k, blk, to, src=None):
            return pltpu.make_async_remote_copy(
                src_ref=slot(*blk) if src is None else src, dst_ref=slot(*blk),
                send_sem=send_sems.at[k], recv_sem=recv_sems.at[k], device_id=to, device_id_type=_MESH)

        mine = pltpu.make_async_copy(x_ref, slot(*me), local_sem)
        mine.start()
        first = [copy(0, me, sibling, src=x_ref)]
        first += [copy(1 + j, me, (*chip, c), src=x_ref) for j, chip in enumerate(chips)]
        for cp in first:
            cp.start()
        passed = [copy(4 + j, (*chip, c), sibling) for j, chip in enumerate(chips)]
        for j, chip in enumerate(chips):
            copy(1 + j, (*chip, c), me).wait_recv()
            passed[j].start()
        copy(0, sibling, me).wait_recv()
        for j, chip in enumerate(chips):
            copy(4 + j, (*chip, 1 - c), me).wait_recv()
        for cp in first + passed:
            cp.wait_send()
        mine.wait()

    return pl.pallas_call(
        body,
        out_shape=jax.ShapeDtypeStruct((N_DEV,) + block.shape, block.dtype),
        in_specs=[_ANY],
        out_specs=_ANY,
        scratch_shapes=[pltpu.SemaphoreType.DMA((7,)), pltpu.SemaphoreType.DMA((7,)), pltpu.SemaphoreType.DMA],
        name=name,
    )(block)


def _swap_with_sibling(blocks, name):
    chips = blocks.shape[0]

    def body(x_ref, out_ref, send_sems, recv_sems):
        x, y, c = lax.axis_index("x"), lax.axis_index("y"), lax.axis_index("c")
        copies = [pltpu.make_async_remote_copy(
            src_ref=x_ref.at[k, 1 - c], dst_ref=out_ref.at[k], send_sem=send_sems.at[k], recv_sem=recv_sems.at[k],
            device_id=(x, y, 1 - c), device_id_type=_MESH) for k in range(chips)]
        for cp in copies:
            cp.start()
        for cp in copies:
            cp.wait()

    return pl.pallas_call(
        body,
        out_shape=jax.ShapeDtypeStruct((chips,) + blocks.shape[2:], blocks.dtype),
        in_specs=[_ANY],
        out_specs=_ANY,
        scratch_shapes=[pltpu.SemaphoreType.DMA((chips,)), pltpu.SemaphoreType.DMA((chips,))],
        name=name,
    )(blocks)


def _add_sibling(blocks, theirs, name):
    chips, _, rows, w = blocks.shape
    tr = _row_tile(rows, 16, 4096)

    def body(b_ref, t_ref, o_ref):
        mine = b_ref[0, lax.axis_index("c")]
        o_ref[0] = (mine.astype(F32) + t_ref[0].astype(F32)).astype(o_ref.dtype)

    return pl.pallas_call(
        body,
        grid=(chips, rows // tr),
        in_specs=[pl.BlockSpec((1, 2, tr, w), lambda k, i: (k, 0, i, 0)), pl.BlockSpec((1, tr, w), lambda k, i: (k, i, 0))],
        out_specs=pl.BlockSpec((1, tr, w), lambda k, i: (k, i, 0)),
        out_shape=jax.ShapeDtypeStruct(theirs.shape, theirs.dtype),
        name=name,
        compiler_params=_params("parallel", "parallel"),
    )(blocks, theirs)


def _exchange_chips(partials, name):
    n_chips = partials.shape[0]

    def body(x_ref, out_ref, send_sems, recv_sems, local_sem):
        x, y, c = lax.axis_index("x"), lax.axis_index("y"), lax.axis_index("c")
        me = 2 * x + y
        mine = pltpu.make_async_copy(x_ref.at[me], out_ref.at[me], local_sem)
        mine.start()
        copies, landed = [], []
        for k in range(1, n_chips):
            px = 1 - x if k & 2 else x
            py = 1 - y if k & 1 else y
            peer = 2 * px + py
            copies.append(pltpu.make_async_remote_copy(
                src_ref=x_ref.at[peer], dst_ref=out_ref.at[me], send_sem=send_sems.at[k - 1],
                recv_sem=recv_sems.at[k - 1], device_id=(px, py, c), device_id_type=_MESH))
            landed.append(pltpu.make_async_remote_copy(
                src_ref=x_ref.at[peer], dst_ref=out_ref.at[peer], send_sem=send_sems.at[k - 1],
                recv_sem=recv_sems.at[k - 1], device_id=(px, py, c), device_id_type=_MESH))
        for cp in copies:
            cp.start()
        for cp in landed:
            cp.wait_recv()
        for cp in copies:
            cp.wait_send()
        mine.wait()

    return pl.pallas_call(
        body,
        out_shape=jax.ShapeDtypeStruct(partials.shape, partials.dtype),
        in_specs=[_ANY],
        out_specs=_ANY,
        scratch_shapes=[pltpu.SemaphoreType.DMA((n_chips - 1,)), pltpu.SemaphoreType.DMA((n_chips - 1,)),
                        pltpu.SemaphoreType.DMA],
        name=name,
    )(partials)


def _sum_slots(parts, name):
    slots, rows, w = parts.shape
    tr = _row_tile(rows, 16 if parts.dtype == BF16 else 8, 4096)

    def body(p_ref, o_ref):
        acc = p_ref[0].astype(F32)
        for j in range(1, slots):
            acc = acc + p_ref[j].astype(F32)
        o_ref[...] = acc

    return pl.pallas_call(
        body,
        grid=(rows // tr,),
        in_specs=[pl.BlockSpec((slots, tr, w), lambda i: (0, i, 0))],
        out_specs=pl.BlockSpec((tr, w), lambda i: (i, 0)),
        out_shape=jax.ShapeDtypeStruct((rows, w), F32),
        name=name,
        compiler_params=_params("parallel"),
    )(parts)


def _adamw(w, g, m, v, name):
    rows, width = w.shape
    tr = _row_tile(rows, 8, 2048)

    def body(w_ref, g_ref, m_ref, v_ref, d_ref, nm_ref, nv_ref):
        g_ = g_ref[...]
        m_ = ADAM_B1 * m_ref[...] + (1.0 - ADAM_B1) * g_
        v_ = ADAM_B2 * v_ref[...] + (1.0 - ADAM_B2) * jnp.square(g_)
        m_hat = m_ / (1.0 - ADAM_B1 ** ADAM_STEP)
        v_hat = v_ / (1.0 - ADAM_B2 ** ADAM_STEP)
        d_ref[...] = -ADAM_LR * (m_hat / (jnp.sqrt(v_hat) + ADAM_EPS) + ADAM_WD * w_ref[...])
        nm_ref[...] = m_
        nv_ref[...] = v_

    spec = pl.BlockSpec((tr, width), lambda i: (i, 0))
    return pl.pallas_call(
        body,
        grid=(rows // tr,),
        in_specs=[spec] * 4,
        out_specs=[spec] * 3,
        out_shape=[jax.ShapeDtypeStruct((rows, width), F32)] * 3,
        name=name,
        compiler_params=_params("parallel"),
    )(w, g, m, v)


_SHARDED = (("w_in", 2), ("w_mla_q_up", 2), ("w_mla_kv_up", 2), ("w_branch", 3), ("w_out", 1))
_REPLICATED = ("norm_g", "mla_q_norm_g", "mla_kv_norm_g", "gqa_q_norm_g", "gqa_k_norm_g", "win_sink", "t5_table",
               "final_norm_g")


def _pack(arrays, row_multiple):
    flat = jnp.concatenate([a.reshape(-1) for a in arrays])
    rows = -(-flat.shape[0] // (LANES * row_multiple)) * row_multiple
    return jnp.pad(flat, (0, rows * LANES - flat.shape[0])).reshape(rows, LANES)


def _unpack(packed, shapes):
    flat, out, at = packed.reshape(-1), [], 0
    for shp in shapes:
        n = int(np.prod(shp))
        out.append(flat[at:at + n].reshape(shp))
        at += n
    return out


_TO_WIRE = {
    "w_in": lambda t: jnp.swapaxes(t, 1, 2), "w_mla_q_up": lambda t: jnp.swapaxes(t, 1, 2),
    "w_mla_kv_up": lambda t: jnp.swapaxes(t, 1, 2),
    "w_branch": lambda t: jnp.transpose(t, (0, 3, 1, 2)).reshape(t.shape[0], t.shape[3], -1), "w_out": lambda t: t}
_FROM_WIRE = {
    "w_in": lambda t, shp: jnp.swapaxes(t, 1, 2), "w_mla_q_up": lambda t, shp: jnp.swapaxes(t, 1, 2),
    "w_mla_kv_up": lambda t, shp: jnp.swapaxes(t, 1, 2),
    "w_branch": lambda t, shp: jnp.transpose(t.reshape(shp[0], shp[3], shp[1], shp[2]), (0, 2, 3, 1)),
    "w_out": lambda t, shp: t}
_WIRE_NAME = {"w_in": "w_in_t", "w_mla_q_up": "w_q_t", "w_mla_kv_up": "w_kv_t", "w_branch": "w_branch_t",
              "w_out": "w_out"}


def _transpose_blocks(t, dtype, name):
    depth, a, b = t.shape

    def body(x_ref, o_ref):
        o_ref[0] = x_ref[0].T.astype(o_ref.dtype)

    return pl.pallas_call(
        body,
        grid=(depth,),
        in_specs=[pl.BlockSpec((1, a, b), lambda i: (i, 0, 0))],
        out_specs=pl.BlockSpec((1, b, a), lambda i: (i, 0, 0)),
        out_shape=jax.ShapeDtypeStruct((depth, b, a), dtype),
        name=name,
        compiler_params=_params("parallel"),
    )(t)


def _join_shards(gathered, wire_shapes):
    out, at = [], 0
    for depth, cut, rest in wire_shapes:
        n = depth * cut * rest // LANES
        blk = gathered[:, at:at + n].reshape(N_DEV, depth, cut, rest)
        out.append(jnp.moveaxis(blk, 0, 1).reshape(depth, N_DEV * cut, rest))
        at += n
    return out


def _split_shards(fulls, wire_shapes):
    parts = []
    for full, (depth, cut, rest) in zip(fulls, wire_shapes):
        blk = jnp.moveaxis(full.reshape(depth, N_DEV, cut, rest), 1, 0)
        parts.append(blk.reshape(N_DEV, depth * cut * rest // LANES, LANES))
    packed = jnp.concatenate(parts, axis=1)
    return packed.reshape((N_DEV // 2, 2) + packed.shape[1:])


def kernel(x, norm_g, w_in, mla_q_norm_g, mla_kv_norm_g, w_mla_q_up, w_mla_kv_up, gqa_q_norm_g, gqa_k_norm_g, win_sink, t5_table, w_branch, w_out, final_norm_g, loss_target, m_norm_g, m_w_in, m_mla_q_norm_g, m_mla_kv_norm_g, m_w_mla_q_up, m_w_mla_kv_up, m_gqa_q_norm_g, m_gqa_k_norm_g, m_win_sink, m_t5_table, m_w_branch, m_w_out, m_final_norm_g, v_norm_g, v_w_in, v_mla_q_norm_g, v_mla_kv_norm_g, v_w_mla_q_up, v_w_mla_kv_up, v_gqa_q_norm_g, v_gqa_k_norm_g, v_win_sink, v_t5_table, v_w_branch, v_w_out, v_final_norm_g):
    given = dict(locals())
    names = ("norm_g", "w_in", "mla_q_norm_g", "mla_kv_norm_g", "w_mla_q_up", "w_mla_kv_up", "gqa_q_norm_g",
             "gqa_k_norm_g", "win_sink", "t5_table", "w_branch", "w_out", "final_norm_g")
    shard_names = [n for n, _ in _SHARDED]
    shard_shapes = [given[n].shape for n in shard_names]

    wire = [_transpose_blocks(given[n], BF16, "w_in_to_wire") if n == "w_in" else _TO_WIRE[n](given[n]).astype(BF16)
            for n in shard_names]
    wire_shapes = [t.shape for t in wire]
    gathered = _all_gather(jnp.concatenate([t.reshape(-1, LANES) for t in wire]), "gather_weights")
    weights = {n: given[n] for n in _REPLICATED}
    weights.update(zip([_WIRE_NAME[n] for n in shard_names], _join_shards(gathered, wire_shapes)))

    loss, (gw, gx) = jax.value_and_grad(_local_loss, argnums=(0, 1))(weights, x[0], loss_target[0])
    loss = lax.psum(loss, ("x", "y", "c"))

    send = _split_shards([gw[_WIRE_NAME[n]] for n in shard_names], wire_shapes)
    partials = _add_sibling(send, _swap_with_sibling(send, "swap_grads"), "add_sibling_grads")
    g_wire = _unpack(_sum_slots(_exchange_chips(partials, "scatter_grads"), "sum_grads"), wire_shapes)
    g_shard = [_transpose_blocks(t, F32, "w_in_from_wire") if n == "w_in" else _FROM_WIRE[n](t, shp)
               for n, t, shp in zip(shard_names, g_wire, shard_shapes)]
    rep_shapes = [given[n].shape for n in _REPLICATED]
    g_rep = _unpack(_sum_slots(_all_gather(_pack([gw[n] for n in _REPLICATED], 8), "gather_small_grads"),
                               "sum_small_grads"), rep_shapes)
    grads = dict(zip(shard_names, g_shard))
    grads.update(zip(_REPLICATED, g_rep))

    def update(group, shapes, row_multiple, name):
        outs = _adamw(*[_pack([src[n] for n in group], row_multiple) for src in (
            given, grads, {n: given["m_" + n] for n in group}, {n: given["v_" + n] for n in group})], name)
        return [dict(zip(group, _unpack(o, shapes))) for o in outs]

    big = update(shard_names, shard_shapes, 16, "adamw_shards")
    small = update(list(_REPLICATED), rep_shapes, 8, "adamw_replicated")
    delta, new_m, new_v = [{**b, **s_} for b, s_ in zip(big, small)]
    return (loss, gx[None], *[grads[n] for n in names], *[delta[n] for n in names],
            *[new_m[n] for n in names], *[new_v[n] for n in names])
```

```python
import functools
import math

import jax
import jax.numpy as jnp
import numpy as np
from jax import lax
from jax.experimental import pallas as pl
from jax.experimental.pallas import tpu as pltpu

F32 = jnp.float32
BF16 = jnp.bfloat16
N_DEV = 8
LANES = 128
HALF = LANES // 2
V7X_VMEM_LIMIT = 56 * 1024 * 1024

EPS = 1e-6
NEG_INF = -1e30
LOG2E = 1.4426950408889634
ROPE_THETA = 10000.0
GRID_W = 64
HEAD_DIM = 64
N_BRANCH = 4
BRANCH_W = 256
MLA_HEADS, MLA_Q_LORA, MLA_KV_LORA, MLA_NOPE, MLA_ROPE, MLA_V = 4, 256, 128, 64, 32, 64
MLA_QK = MLA_NOPE + MLA_ROPE
GQA_HEADS, GQA_KV_HEADS = 4, 2
DIL_PATTERNS = ((128, 1), (512, 4), (2048, 16))
DIL_HEADS = 4
WIN_HEADS, WIN_KV_HEADS, WIN_HALF = 4, 2, 128
T5_BUCKETS, T5_MAX_DIST = 32, 1024
BAND_BLOCK = 128
ADAM_LR, ADAM_B1, ADAM_B2, ADAM_EPS, ADAM_WD, ADAM_STEP = 0.001, 0.9, 0.999, 1e-08, 0.01, 10

D_MODEL = 1024
GM_W, SMALL_W, BAND_W = 5120, 512, 768
MLA_BLK, GQA_BLK, WIN_BLK, DIL_BLK = 10, 11, 8, 9
P_TOT = 7680
QW = 256


def _params(*sem):
    return pltpu.CompilerParams(dimension_semantics=sem, vmem_limit_bytes=V7X_VMEM_LIMIT)


def _pick(n, cands):
    for c in cands:
        if n % c == 0:
            return c
    return n


def _row_tile(rows, unit, cap):
    best = unit
    for t in range(unit, min(rows, cap) + 1, unit):
        if rows % t == 0:
            best = t
    assert rows % best == 0
    return best


def _dot(a, b, ca, cb):
    return lax.dot_general(a.astype(BF16), b.astype(BF16), (((ca,), (cb,)), ((), ())), preferred_element_type=F32)


def _bmm(a, b, ca, cb):
    return lax.dot_general(a, b, (((ca,), (cb,)), ((0,), (0,))), preferred_element_type=F32)


@jax.custom_vjp
def _bdot(a, b):
    return _dot(a, b, 1, 0)


def _bdot_fwd(a, b):
    return _dot(a, b, 1, 0), (a, b)


def _bdot_bwd(res, g):
    a, b = res
    return _dot(g, b, 1, 1), _dot(a, g, 0, 0)


_bdot.defvjp(_bdot_fwd, _bdot_bwd)


def _hdot(a, c):
    return lax.dot_general(a, c, (((1,), (0,)), ((), ())), precision=lax.Precision.HIGHEST, preferred_element_type=F32)


@functools.partial(jax.custom_vjp, nondiff_argnums=(1,))
def _lane_roll(x, shift):
    return pltpu.roll(x, shift, 1)


def _lane_roll_fwd(x, shift):
    return pltpu.roll(x, shift, 1), None


def _lane_roll_bwd(shift, _, g):
    return (pltpu.roll(g, g.shape[1] - shift, 1),)


_lane_roll.defvjp(_lane_roll_fwd, _lane_roll_bwd)


@functools.partial(jax.custom_vjp, nondiff_argnums=(1,))
def _lane_ranges(x, cut):
    bounds, _ = cut
    return tuple(x[:, lo:hi] for lo, hi in zip(bounds[:-1], bounds[1:]))


def _lane_ranges_fwd(x, cut):
    return _lane_ranges(x, cut), None


def _lane_ranges_bwd(cut, _, cts):
    bounds, width = cut
    parts = list(cts)
    if bounds[-1] < width:
        parts.append(jnp.zeros((cts[0].shape[0], width - bounds[-1]), cts[0].dtype))
    return (jnp.concatenate(parts, axis=1),)


_lane_ranges.defvjp(_lane_ranges_fwd, _lane_ranges_bwd)


def _lanes(x, bounds):
    return _lane_ranges(x, (tuple(bounds), x.shape[1]))


@jax.custom_vjp
def _unstack(x):
    return tuple(x[i] for i in range(x.shape[0]))


def _unstack_fwd(x):
    return _unstack(x), None


def _unstack_bwd(_, cts):
    return (jnp.stack(cts, axis=0),)


_unstack.defvjp(_unstack_fwd, _unstack_bwd)


@functools.partial(jax.custom_vjp, nondiff_argnums=(1,))
def _split_heads(x, h):
    d = x.shape[1] // h
    return jnp.stack([x[:, i * d:(i + 1) * d] for i in range(h)], axis=0)


def _split_heads_fwd(x, h):
    return _split_heads(x, h), None


def _split_heads_bwd(h, _, ct):
    return (jnp.concatenate([ct[i] for i in range(h)], axis=1),)


_split_heads.defvjp(_split_heads_fwd, _split_heads_bwd)


def _join_heads(x):
    return jnp.concatenate(_unstack(x), axis=1)


def _rope(x, cos_t, sin_t, half):
    w = x.shape[1]
    lane = lax.broadcasted_iota(jnp.int32, (1, w), 1)
    first = (lane % (2 * half)) < half
    partner = jnp.where(first, _lane_roll(x, w - half), _lane_roll(x, half))
    return x * cos_t + partner * sin_t


def _rms(x, g):
    return x * lax.rsqrt(jnp.mean(x * x, axis=-1, keepdims=True) + EPS) * g


def _rows(tr, w, col=0):
    return pl.BlockSpec((tr, w), lambda i: (i, col))


def _head_rows(h, tr, d):
    return pl.BlockSpec((h, tr, d), lambda i: (0, i, 0))


def _whole(shape):
    nd = len(shape)
    return pl.BlockSpec(tuple(shape), lambda i: (0,) * nd)


def _fwd_call(name, fn, steps, rows, params, aux, outs):
    nr, npar, na = len(rows), len(params), len(aux)

    def body(*refs):
        vals = [x[...].astype(F32) for x in refs[:nr + npar + na]]
        res = fn(vals[:nr], vals[nr:nr + npar], vals[nr + npar:])
        for o_ref, o in zip(refs[nr + npar + na:], res):
            o_ref[...] = o.astype(o_ref.dtype)

    return pl.pallas_call(
        body,
        grid=(steps,),
        in_specs=[s for _, s in rows] + [_whole(p.shape) for p in params] + [s for _, s in aux],
        out_specs=[e[1] for e in outs],
        out_shape=[jax.ShapeDtypeStruct(e[0], e[2] if len(e) > 2 else F32) for e in outs],
        name=name + "_fwd",
        compiler_params=_params("parallel"),
    )(*[a for a, _ in rows], *params, *[a for a, _ in aux])


def _vjp_call(name, fn, steps, rows, params, aux, cts, row_grads, into=None):
    nr, npar, na, nc = len(rows), len(params), len(aux), len(cts)
    n_in = nr + npar + na + nc
    lead = 0 if into is None else 1

    def body(*refs):
        refs = refs[lead:]
        vals = [x[...].astype(F32) for x in refs[:n_in]]
        r, p, a, d = vals[:nr], vals[nr:nr + npar], vals[nr + npar:nr + npar + na], vals[nr + npar + na:]
        out_refs = refs[n_in:]
        _, vjp = jax.vjp(lambda r_, p_: tuple(fn(r_, p_, a)), r, p)
        dr, dp = vjp(tuple(d))
        for o_ref, o in zip(out_refs[:nr], dr):
            o_ref[...] = o.astype(o_ref.dtype)

        @pl.when(pl.program_id(0) == 0)
        def _():
            for o_ref in out_refs[nr:]:
                o_ref[...] = jnp.zeros_like(o_ref)

        for o_ref, o in zip(out_refs[nr:], dp):
            o_ref[...] += o

    outs = pl.pallas_call(
        body,
        grid=(steps,),
        in_specs=([] if into is None else [pl.BlockSpec(memory_space=pl.ANY)])
        + [s for _, s in rows] + [_whole(p.shape) for p in params] + [s for _, s in aux] + [s for _, s in cts],
        out_specs=[e[1] for e in row_grads] + [_whole(p.shape) for p in params],
        out_shape=[jax.ShapeDtypeStruct(e[0], e[2] if len(e) > 2 else F32) for e in row_grads]
        + [jax.ShapeDtypeStruct(p.shape, F32) for p in params],
        input_output_aliases={} if into is None else {0: 0},
        name=name + "_bwd",
        compiler_params=_params("arbitrary"),
    )(*([] if into is None else [into]), *[a for a, _ in rows], *params, *[a for a, _ in aux], *[a for a, _ in cts])
    return list(outs[:nr]), list(outs[nr:])


def _norm_tile(r, p, a):
    return (_rms(r[0], p[0]),)


def _mm(a, b, mode, name, out_dtype=F32):
    if mode == "nn":
        (m, k), n = a.shape, b.shape[1]
    elif mode == "nt":
        (m, k), n = a.shape, b.shape[0]
    else:
        (k, m), n = a.shape, b.shape[1]
    tn = _pick(n, (1024, 768, 512, 384, 256, 128))
    budget = V7X_VMEM_LIMIT * 3 // 4
    out_bytes = 4 + 2 * np.dtype(out_dtype).itemsize

    def tiles():
        for tm in (2048, 1024, 512, 256, 128):
            for tk in (512, 256, 128) if mode == "tn" else (4096, 1024, 768, 512, 384, 256, 128):
                need = 2 * tk * (tm * a.dtype.itemsize + tn * b.dtype.itemsize) + tm * tn * out_bytes
                if m % tm == 0 and k % tk == 0 and need <= budget:
                    return tm, tk
        return _pick(m, (128,)), _pick(k, (128,))

    tm, tk = tiles()
    nk = k // tk

    def body(*refs):
        a_ref, b_ref, o_ref, acc_ref = refs
        kk = pl.program_id(2)
        if mode == "nn":
            part = _dot(a_ref[...], b_ref[...], 1, 0)
        elif mode == "nt":
            part = _dot(a_ref[...], b_ref[...], 1, 1)
        else:
            part = _dot(a_ref[...], b_ref[...], 0, 0)
        if nk == 1:
            o_ref[...] = part.astype(o_ref.dtype)
        else:
            @pl.when(kk == 0)
            def _():
                acc_ref[...] = part

            @pl.when(kk > 0)
            def _():
                acc_ref[...] += part

            @pl.when(kk == nk - 1)
            def _():
                o_ref[...] = acc_ref[...].astype(o_ref.dtype)

    if mode == "nn":
        a_spec = pl.BlockSpec((tm, tk), lambda i, j, kk: (i, kk))
        b_spec = pl.BlockSpec((tk, tn), lambda i, j, kk: (kk, j))
    elif mode == "nt":
        a_spec = pl.BlockSpec((tm, tk), lambda i, j, kk: (i, kk))
        b_spec = pl.BlockSpec((tn, tk), lambda i, j, kk: (j, kk))
    else:
        a_spec = pl.BlockSpec((tk, tm), lambda i, j, kk: (kk, i))
        b_spec = pl.BlockSpec((tk, tn), lambda i, j, kk: (kk, j))
    o_spec = pl.BlockSpec((tm, tn), lambda i, j, kk: (i, j))
    return pl.pallas_call(
        body,
        grid=(m // tm, n // tn, nk),
        in_specs=[a_spec, b_spec],
        out_specs=o_spec,
        out_shape=jax.ShapeDtypeStruct((m, n), out_dtype),
        scratch_shapes=[pltpu.VMEM((tm, tn), F32)],
        name=name,
        compiler_params=_params("parallel", "parallel", "arbitrary"),
    )(a, b)


def _dense_fwd_call(q, k, v, scale, name):
    n, sq, d = q.shape
    sk, dv = k.shape[1], v.shape[2]
    tq = _pick(sq, (512, 256, 128))
    c = scale * LOG2E

    nkb = 1

    def body(q_ref, k_ref, v_ref, o_ref, lse_ref, m_s, acc_s, vext_s):
        j = pl.program_id(2)

        @pl.when(j == 0)
        def _():
            m_s[...] = jnp.full_like(m_s, NEG_INF)
            acc_s[...] = jnp.zeros_like(acc_s)
            vext_s[...] = jnp.ones_like(vext_s)

        vext_s[:, :dv] = v_ref[0].astype(BF16)
        m_old = m_s[...]
        s = _dot(q_ref[0], k_ref[0], 1, 1)
        m_new = jnp.maximum(m_old, jnp.max(s, axis=1, keepdims=True))
        p = jnp.exp2(s * c - m_new * c)
        acc = jnp.exp2((m_old - m_new) * c) * acc_s[...] + _dot(p, vext_s[...], 1, 0)
        m_s[...] = m_new
        acc_s[...] = acc

        @pl.when(j == nkb - 1)
        def _():
            l = acc[:, dv:dv + 1]
            o_ref[0] = acc[:, :dv] / l
            lse_ref[0] = m_new * scale + jnp.log(l)

    return pl.pallas_call(
        body,
        grid=(n, sq // tq, nkb),
        in_specs=[
            pl.BlockSpec((1, tq, d), lambda h, i, j: (h, i, 0)),
            pl.BlockSpec((1, sk // nkb, d), lambda h, i, j: (h, j, 0)),
            pl.BlockSpec((1, sk // nkb, dv), lambda h, i, j: (h, j, 0)),
        ],
        out_specs=[
            pl.BlockSpec((1, tq, dv), lambda h, i, j: (h, i, 0)),
            pl.BlockSpec((1, tq, 1), lambda h, i, j: (h, i, 0)),
        ],
        out_shape=[jax.ShapeDtypeStruct((n, sq, dv), F32), jax.ShapeDtypeStruct((n, sq, 1), F32)],
        scratch_shapes=[pltpu.VMEM((tq, 1), F32), pltpu.VMEM((tq, 2 * dv), F32), pltpu.VMEM((sk // nkb, 2 * dv), BF16)],
        name=name + "_fwd",
        compiler_params=_params("parallel", "parallel", "arbitrary"),
    )(q, k, v)


def _dense_bwd_call(q, k, v, o, lse, do, scale, name):
    n, sq, d = q.shape
    sk, dv = k.shape[1], v.shape[2]
    tq, tk = _pick(sq, (1024, 512, 256, 128)), _pick(sk, (2048, 1024, 512, 256, 128))
    c = scale * LOG2E

    def body(q_ref, k_ref, v_ref, o_ref, lse_ref, do_ref, dq_ref, dk_ref, dv_ref):
        j, i = pl.program_id(1), pl.program_id(2)
        qb, kb, vb = q_ref[0].astype(BF16), k_ref[0].astype(BF16), v_ref[0].astype(BF16)
        do_f = do_ref[0]
        dob = do_f.astype(BF16)
        p = jnp.exp2(_dot(qb, kb, 1, 1) * c - lse_ref[0] * LOG2E)
        delta = jnp.sum(do_f * o_ref[0], axis=1, keepdims=True)
        ds = (p * (_dot(dob, vb, 1, 1) - delta)).astype(BF16)
        dv_part = _dot(p, dob, 0, 0)
        dk_part = _dot(ds, qb, 0, 0) * scale
        dq_part = _dot(ds, kb, 1, 0) * scale
        rows = pl.ds(pl.multiple_of(i * tq, tq), tq)

        @pl.when(i == 0)
        def _():
            dk_ref[0] = dk_part
            dv_ref[0] = dv_part

        @pl.when(i > 0)
        def _():
            dk_ref[0] += dk_part
            dv_ref[0] += dv_part

        @pl.when(j == 0)
        def _():
            dq_ref[0, rows, :] = dq_part

        @pl.when(j > 0)
        def _():
            dq_ref[0, rows, :] += dq_part

    return pl.pallas_call(
        body,
        grid=(n, sk // tk, sq // tq),
        in_specs=[
            pl.BlockSpec((1, tq, d), lambda h, j, i: (h, i, 0)),
            pl.BlockSpec((1, tk, d), lambda h, j, i: (h, j, 0)),
            pl.BlockSpec((1, tk, dv), lambda h, j, i: (h, j, 0)),
            pl.BlockSpec((1, tq, dv), lambda h, j, i: (h, i, 0)),
            pl.BlockSpec((1, tq, 1), lambda h, j, i: (h, i, 0)),
            pl.BlockSpec((1, tq, dv), lambda h, j, i: (h, i, 0)),
        ],
        out_specs=[
            pl.BlockSpec((1, sq, d), lambda h, j, i: (h, 0, 0)),
            pl.BlockSpec((1, tk, d), lambda h, j, i: (h, j, 0)),
            pl.BlockSpec((1, tk, dv), lambda h, j, i: (h, j, 0)),
        ],
        out_shape=[
            jax.ShapeDtypeStruct((n, sq, d), F32),
            jax.ShapeDtypeStruct((n, sk, d), F32),
            jax.ShapeDtypeStruct((n, sk, dv), F32),
        ],
        name=name + "_bwd",
        compiler_params=_params("arbitrary", "arbitrary", "arbitrary"),
    )(q, k, v, o, lse, do)


def _head_geometry(h, group):
    pair, a = divmod(h, 2)
    kv_pair, b = divmod(h // group, 2)
    return pair, a, kv_pair, b


def _lane_half():
    return lax.broadcasted_iota(jnp.int32, (1, LANES), 1) // HALF


def _align(x, a, b):
    if a != b:
        x = pltpu.roll(x, HALF, 1)
    return jnp.where(_lane_half() == b, x, 0.0)


def _unalign(x, a, b):
    x = jnp.where(_lane_half() == b, x, 0.0)
    return pltpu.roll(x, HALF, 1) if a != b else x


def _bands(w, pw, nw, lo, kvw, nb):
    b = BAND_BLOCK
    cat = jnp.concatenate([pw[:, lo:lo + kvw], w[:, lo:lo + kvw], nw[:, lo:lo + kvw]], axis=0).astype(BF16)
    out = []
    for g in range(kvw // LANES):
        c3 = cat[:, g * LANES:(g + 1) * LANES].reshape(nb + 2, b, LANES)
        out.append(jnp.concatenate([c3[0:nb], c3[1:nb + 1], c3[2:nb + 2]], axis=1))
    return out


def _edge_mask(first_block, nb, period):
    b = BAND_BLOCK
    blk = (first_block + lax.broadcasted_iota(jnp.int32, (nb, 1, 3 * b), 0)) % period
    col = lax.broadcasted_iota(jnp.int32, (nb, 1, 3 * b), 2)
    outside = ((col < b) & (blk == 0)) | ((col >= 2 * b) & (blk == period - 1))
    return jnp.where(outside, NEG_INF, 0.0)


def _band_geometry(proj, dil):
    rows = proj.shape[0]
    tl = _pick(rows, (1024, 512, 256, 128))
    return rows, tl, tl // BAND_BLOCK, rows // tl, rows // dil // BAND_BLOCK


def _band_in_specs(tl, nb, n_chunks, n_blocks, col, last_step_idle):
    def chunk(i):
        return jnp.minimum(i, n_chunks - 1) if last_step_idle else i

    main = pl.BlockSpec((tl, BAND_W), lambda j, i: (j * n_chunks + chunk(i), col))
    prev = pl.BlockSpec((BAND_BLOCK, BAND_W),
                        lambda j, i: (j * n_blocks + jnp.maximum(chunk(i) * nb - 1, 0), col))
    nxt = pl.BlockSpec((BAND_BLOCK, BAND_W),
                       lambda j, i: (j * n_blocks + jnp.minimum((chunk(i) + 1) * nb, n_blocks - 1), col))
    rows = pl.BlockSpec((tl, QW), lambda j, i: (j * n_chunks + chunk(i), 0))
    return main, prev, nxt, rows


def _band_fwd_call(proj, col, bias, sink, dil, group, kvw, scale, name):
    s_tok = proj.shape[0]
    seq, tl, nb, n_chunks, period = _band_geometry(proj, dil)
    n_blocks = seq // BAND_BLOCK
    heads = bias.shape[0]

    def body(w_ref, pw_ref, nw_ref, bias_ref, sink_ref, o_ref, lse_ref):
        i = pl.program_id(1)
        w, pw, nw = w_ref[...].astype(F32), pw_ref[...].astype(F32), nw_ref[...].astype(F32)
        kb = _bands(w, pw, nw, QW, kvw, nb)
        vb = _bands(w, pw, nw, QW + kvw, kvw, nb)
        edge = _edge_mask(i * nb, nb, period)
        o_acc = [jnp.zeros((tl, LANES), F32) for _ in range(heads // 2)]
        lse_acc = [jnp.zeros((tl, LANES), F32) for _ in range(heads // 2)]
        geom = [_head_geometry(h, group) for h in range(heads)]
        logits = []
        for h, (pair, a, kvp, b) in enumerate(geom):
            q_al = _align(w[:, pair * LANES:(pair + 1) * LANES], a, b).astype(BF16).reshape(nb, BAND_BLOCK, LANES)
            logits.append(_bmm(q_al, kb[kvp], 2, 2) * scale + bias_ref[h][None] + edge)
        es, ssums, ms = [], [], []
        for h in range(heads):
            sk = sink_ref[h].reshape(1, 1, 1)
            m = jnp.maximum(jnp.max(logits[h], axis=2, keepdims=True), sk)
            e = jnp.exp(logits[h] - m)
            es.append(e.astype(BF16))
            ssums.append(jnp.sum(e, axis=2, keepdims=True) + jnp.exp(sk - m))
            ms.append(m)
        for h, (pair, a, kvp, b) in enumerate(geom):
            out = _bmm(es[h], vb[kvp], 2, 1) / ssums[h]
            o_acc[pair] = o_acc[pair] + _unalign(out.reshape(tl, LANES), a, b)
            lse = (ms[h] + jnp.log(ssums[h])).reshape(tl, 1)
            lse_acc[pair] = lse_acc[pair] + jnp.where(_lane_half() == a, lse, 0.0)
        o_ref[...] = jnp.concatenate(o_acc, axis=1)
        lse_ref[...] = jnp.concatenate(lse_acc, axis=1)

    main, prev, nxt, rows = _band_in_specs(tl, nb, n_chunks, n_blocks, col, False)
    return pl.pallas_call(
        body,
        grid=(1, n_chunks),
        in_specs=[main, prev, nxt, pl.BlockSpec(bias.shape, lambda j, i: (0, 0, 0)),
                  pl.BlockSpec(sink.shape, lambda j, i: (0, 0, 0))],
        out_specs=[rows, rows],
        out_shape=[jax.ShapeDtypeStruct((s_tok, QW), F32)] * 2,
        name=name + "_fwd",
        compiler_params=_params("parallel", "parallel"),
    )(proj, proj, proj, bias, sink)


def _band_bwd_call(proj, o, do, lse, dlse, bias, sink, dproj, col, dil, group, kvw, scale, name):
    seq, tl, nb, n_chunks, period = _band_geometry(proj, dil)
    lead = 0 if dproj is None else 1
    n_blocks = seq // BAND_BLOCK
    heads = bias.shape[0]
    b_ = BAND_BLOCK
    have_dlse = dlse is not None

    def body(*refs):
        (w_ref, pw_ref, nw_ref, o_ref, do_ref, lse_ref), refs = refs[lead:lead + 6], refs[lead + 6:]
        if have_dlse:
            dlse_ref, refs = refs[0], refs[1:]
        bias_ref, sink_ref, dwin_ref, dbias_ref, dsink_ref, dq_s, dk_s, dv_s = refs
        j, i = pl.program_id(0), pl.program_id(1)

        @pl.when((j == 0) & (i == 0))
        def _():
            dbias_ref[...] = jnp.zeros_like(dbias_ref)
            dsink_ref[...] = jnp.zeros_like(dsink_ref)

        @pl.when(i == 0)
        def _():
            dk_s[...] = jnp.zeros_like(dk_s)
            dv_s[...] = jnp.zeros_like(dv_s)

        @pl.when(i < n_chunks)
        def _():
            w, pw, nw = w_ref[...].astype(F32), pw_ref[...].astype(F32), nw_ref[...].astype(F32)
            kb = _bands(w, pw, nw, QW, kvw, nb)
            vb = _bands(w, pw, nw, QW + kvw, kvw, nb)
            edge = _edge_mask(i * nb, nb, period)
            dq_acc = [jnp.zeros((tl, LANES), F32) for _ in range(heads // 2)]
            for h in range(heads):
                pair, a, kvp, b = _head_geometry(h, group)
                lanes = slice(pair * LANES, (pair + 1) * LANES)
                mine = _lane_half() == a
                q_al = _align(w[:, lanes], a, b).astype(BF16).reshape(nb, b_, LANES)
                do_al = _align(do_ref[:, lanes], a, b).astype(BF16).reshape(nb, b_, LANES)
                lse_h = jnp.max(jnp.where(mine, lse_ref[:, lanes], NEG_INF), axis=1, keepdims=True)
                shift = -jnp.sum(jnp.where(mine, do_ref[:, lanes] * o_ref[:, lanes], 0.0), axis=1, keepdims=True)
                if have_dlse:
                    shift = shift + jnp.sum(jnp.where(mine, dlse_ref[:, lanes], 0.0), axis=1, keepdims=True)
                logits = _bmm(q_al, kb[kvp], 2, 2) * scale + bias_ref[h][None] + edge
                p = jnp.exp(logits - lse_h.reshape(nb, b_, 1))
                dlogits = p * (_bmm(do_al, vb[kvp], 2, 2) + shift.reshape(nb, b_, 1))
                dbias_ref[h] += jnp.sum(dlogits, axis=0)
                dsink_ref[h] += jnp.sum(jnp.exp(sink_ref[h] - lse_h) * shift, axis=0, keepdims=True)
                ds = (dlogits * scale).astype(BF16)
                dq_acc[pair] = dq_acc[pair] + _unalign(_bmm(ds, kb[kvp], 2, 1).reshape(tl, LANES), a, b)
                dk_band = _bmm(ds, q_al, 1, 1)
                dv_band = _bmm(p.astype(BF16), do_al, 1, 1)
                kv_lanes = slice(kvp * LANES, (kvp + 1) * LANES)
                for t in range(3):
                    at = pl.ds(pl.multiple_of(i * tl + t * b_, b_), tl)
                    dk_s[at, kv_lanes] += dk_band[:, t * b_:(t + 1) * b_, :].reshape(tl, LANES)
                    dv_s[at, kv_lanes] += dv_band[:, t * b_:(t + 1) * b_, :].reshape(tl, LANES)
            dq_s[lax.rem(i, 2)] = jnp.concatenate(dq_acc, axis=1)

        @pl.when(i >= 1)
        def _():
            at = pl.ds(pl.multiple_of((i - 1) * tl + b_, b_), tl)
            parts = [dq_s[lax.rem(i + 1, 2)], dk_s[at, :], dv_s[at, :]]
            if QW + 2 * kvw < BAND_W:
                parts.append(jnp.zeros((tl, BAND_W - QW - 2 * kvw), F32))
            dwin_ref[...] = jnp.concatenate(parts, axis=1).astype(dwin_ref.dtype)

    main, prev, nxt, rows = _band_in_specs(tl, nb, n_chunks, n_blocks, col, True)
    row_args = [o, do, lse] + ([dlse] if have_dlse else [])
    small = [pl.BlockSpec(bias.shape, lambda j, i: (0, 0, 0)), pl.BlockSpec(sink.shape, lambda j, i: (0, 0, 0))]
    return pl.pallas_call(
        body,
        grid=(1, n_chunks + 1),
        in_specs=[pl.BlockSpec(memory_space=pl.ANY)] * lead + [main, prev, nxt] + [rows] * len(row_args) + small,
        out_specs=[pl.BlockSpec((tl, BAND_W), lambda j, i: (j * n_chunks + jnp.maximum(i - 1, 0), col))] + small,
        out_shape=[jax.ShapeDtypeStruct(proj.shape, BF16), jax.ShapeDtypeStruct(bias.shape, F32),
                   jax.ShapeDtypeStruct(sink.shape, F32)],
        scratch_shapes=[pltpu.VMEM((2, tl, QW), F32), pltpu.VMEM((seq + 2 * b_, kvw), F32),
                        pltpu.VMEM((seq + 2 * b_, kvw), F32)],
        input_output_aliases={0: 0} if lead else {},
        name=name + "_bwd",
        compiler_params=_params("arbitrary", "arbitrary"),
    )(*([dproj] if lead else []), proj, proj, proj, *row_args, bias, sink)


def _loss_call(x, target, g):
    s, d = x.shape
    tr = _pick(s, (256, 128, 64, 32, 16, 8))

    def tile_loss(xt, gt, tt):
        err = jnp.square(_rms(xt, gt) - tt)
        return 0.5 * jnp.sum(jnp.mean(err, axis=-1, keepdims=True), axis=0, keepdims=True)

    def body(x_ref, t_ref, g_ref, loss_ref, dx_ref, dg_ref):
        tt = t_ref[...]
        val, vjp = jax.vjp(lambda xt, gt: tile_loss(xt, gt, tt), x_ref[...], g_ref[...])
        dx, dg = vjp(jnp.ones_like(val))
        dx_ref[...] = dx

        @pl.when(pl.program_id(0) == 0)
        def _():
            loss_ref[...] = jnp.zeros_like(loss_ref)
            dg_ref[...] = jnp.zeros_like(dg_ref)

        loss_ref[...] += val
        dg_ref[...] += dg

    return pl.pallas_call(
        body,
        grid=(s // tr,),
        in_specs=[_rows(tr, d), _rows(tr, d), _whole((1, d))],
        out_specs=[_whole((1, 1)), _rows(tr, d), _whole((1, d))],
        out_shape=[jax.ShapeDtypeStruct((1, 1), F32), jax.ShapeDtypeStruct((s, d), F32),
                   jax.ShapeDtypeStruct((1, d), F32)],
        name="final_norm_loss",
        compiler_params=_params("arbitrary"),
    )(x, target, g)


@jax.custom_vjp
def _loss_op(x, target, g):
    return _loss_call(x, target, g)[0][0, 0]


def _loss_op_fwd(x, target, g):
    loss, dx, dg = _loss_call(x, target, g)
    return loss[0, 0], (dx, dg, target)


def _loss_op_bwd(res, ct):
    dx, dg, target = res
    return ct * dx, jnp.zeros_like(target), ct * dg


_loss_op.defvjp(_loss_op_fwd, _loss_op_bwd)


def _mla_tile(r, p, a):
    g_q, g_kv, w_q, w_k, w_v = p
    cos_t, sin_t, place_kr = a
    a_q, a_kv, a_kr = _lanes(r[0], (0, MLA_Q_LORA, MLA_Q_LORA + MLA_KV_LORA, MLA_Q_LORA + MLA_KV_LORA + MLA_ROPE))
    q = _rope(_bdot(_rms(a_q, g_q), w_q), cos_t, sin_t, MLA_ROPE // 2)
    ckv = _rms(a_kv, g_kv)
    k = _rope(_bdot(ckv, w_k) + _hdot(a_kr, place_kr), cos_t, sin_t, MLA_ROPE // 2)
    return _split_heads(q, MLA_HEADS), _split_heads(k, MLA_HEADS), _split_heads(_bdot(ckv, w_v), MLA_HEADS)


def _head_rms(x, g, head_mean):
    return x * lax.rsqrt(_hdot(x * x, head_mean) + EPS) * g


def _gqa_tile(r, p, a):
    g_q, g_k = p
    cos_t, sin_t, mean_q, mean_k = a
    wq, wk = GQA_HEADS * HEAD_DIM, GQA_KV_HEADS * HEAD_DIM
    b_q, b_k, b_v = _lanes(r[0], (0, wq, wq + wk, wq + 2 * wk))
    q = _rope(_head_rms(b_q, g_q, mean_q), cos_t, sin_t, HEAD_DIM // 4)
    k = _rope(_head_rms(b_k, g_k, mean_k), cos_t[:, :wk], sin_t[:, :wk], HEAD_DIM // 4)
    return _split_heads(q, GQA_HEADS), _split_heads(k, GQA_KV_HEADS), _split_heads(b_v, GQA_KV_HEADS)


def _permute_rows(p, x, cp):
    pb = p.astype(BF16)
    hi = x.astype(BF16)
    rest = x - hi.astype(F32)
    mid = rest.astype(BF16)
    low = (rest - mid.astype(F32)).astype(BF16)
    dims = (((cp,), (0,)), ((), ()))
    return (lax.dot_general(pb, hi, dims, preferred_element_type=F32)
            + lax.dot_general(pb, mid, dims, preferred_element_type=F32)
            + lax.dot_general(pb, low, dims, preferred_element_type=F32))


@jax.custom_vjp
def _permuted(p, x):
    return _permute_rows(p, x, 1)


def _permuted_fwd(p, x):
    return _permute_rows(p, x, 1), p


def _permuted_bwd(p, ct):
    return jnp.zeros_like(p), _permute_rows(p, ct, 0)


_permuted.defvjp(_permuted_fwd, _permuted_bwd)


def _interleave(p, x):
    return _permuted(p, x.reshape(x.shape[0] * x.shape[1], x.shape[2]))


def _interleave_matrix(rows, dil):
    p = np.zeros((rows, rows), np.float32)
    for t in range(rows):
        p[t, (t % dil) * (rows // dil) + t // dil] = 1.0
    return p


def _merge_tile(r, p, a):
    gm, o_a, o_b, oc0, oc1, oc2, l0, l1, l2, o_d = r
    (w_branch,) = p
    perm1, perm2 = a
    oc1, l1, oc2, l2 = _interleave(perm1, oc1), _interleave(perm1, l1), _interleave(perm2, oc2), _interleave(perm2, l2)
    d = w_branch.shape[2]
    gate_path, merge_logits = _lanes(gm, (0, N_BRANCH * BRANCH_W, N_BRANCH * BRANCH_W + N_BRANCH * d))
    m = jnp.maximum(jnp.maximum(l0, l1), l2)
    e0, e1, e2 = jnp.exp(l0 - m), jnp.exp(l1 - m), jnp.exp(l2 - m)
    y_c = (e0 * oc0 + e1 * oc1 + e2 * oc2) / (e0 + e1 + e2)
    y = jnp.concatenate([_join_heads(o_a), _join_heads(o_b), y_c, o_d], axis=1)
    u = y * (gate_path * jax.nn.sigmoid(gate_path))
    gates = _lanes(merge_logits, tuple(range(0, N_BRANCH * d + 1, d)))
    us = _lanes(u, tuple(range(0, N_BRANCH * BRANCH_W + 1, BRANCH_W)))
    branch_w = _unstack(w_branch)
    out = None
    for nb in range(N_BRANCH):
        term = jax.nn.sigmoid(gates[nb]) * _bdot(us[nb], branch_w[nb])
        out = term if out is None else out + term
    return (out,)


def _mixer_calls(proj, prm, aux):
    s = proj.shape[0]
    tr, tm = _pick(s, (512, 256, 128)), _pick(s, (256,))
    mla_cos, mla_sin, gqa_cos, gqa_sin, place_kr, mean_q, mean_k = aux[:7]
    wq = MLA_HEADS * MLA_QK
    mla = dict(
        steps=s // tr, rows=[(proj, _rows(tr, SMALL_W, MLA_BLK))],
        params=[prm["g_q"], prm["g_kv"], prm["w_q"], prm["w_k"], prm["w_v"]],
        aux=[(mla_cos, _rows(tr, wq)), (mla_sin, _rows(tr, wq)), (place_kr, _whole(place_kr.shape))],
        outs=[((MLA_HEADS, s, MLA_QK), _head_rows(MLA_HEADS, tr, MLA_QK), BF16)] * 2
        + [((MLA_HEADS, s, MLA_V), _head_rows(MLA_HEADS, tr, MLA_V), BF16)],
        window=((s, P_TOT), _rows(tr, SMALL_W, MLA_BLK), BF16))
    wg = GQA_HEADS * HEAD_DIM
    gqa = dict(
        steps=s // tr, rows=[(proj, _rows(tr, SMALL_W, GQA_BLK))], params=[prm["gq"], prm["gk"]],
        aux=[(gqa_cos, _rows(tr, wg)), (gqa_sin, _rows(tr, wg)), (mean_q, _whole(mean_q.shape)),
             (mean_k, _whole(mean_k.shape))],
        outs=[((GQA_HEADS, s, HEAD_DIM), _head_rows(GQA_HEADS, tr, HEAD_DIM), BF16)]
        + [((GQA_KV_HEADS, s, HEAD_DIM), _head_rows(GQA_KV_HEADS, tr, HEAD_DIM), BF16)] * 2,
        window=((s, P_TOT), _rows(tr, SMALL_W, GQA_BLK), BF16))
    merge = dict(steps=s // tm, tm=tm, window=((s, P_TOT), _rows(tm, GM_W, 0), BF16))
    return mla, gqa, merge


def _merge_rows(proj, o_a, o_b, ocs, lses, o_d, tm):
    h4 = _head_rows(4, tm, HEAD_DIM)
    s = proj.shape[0]

    def by_residue(t, dil):
        if dil == 1:
            return t, _rows(tm, QW)
        return t.reshape(dil, s // dil, QW), pl.BlockSpec((dil, tm // dil, QW), lambda i: (0, i, 0))

    dils = [dil for _, dil in DIL_PATTERNS]
    return ([(proj, _rows(tm, GM_W, 0)), (o_a, h4), (o_b, h4)] + [by_residue(t, r) for t, r in zip(ocs, dils)]
            + [by_residue(t, r) for t, r in zip(lses, dils)] + [(o_d, _rows(tm, QW))])


def _merge_aux(aux):
    return [(t, _whole(t.shape)) for t in aux[7:9]]


def _to_residues(t, dil):
    s, w = t.shape
    return t if dil == 1 else t.reshape(s // dil, dil, w).transpose(1, 0, 2).reshape(s, w)


def _from_residues(t, dil):
    s, w = t.shape
    return t if dil == 1 else t.reshape(dil, s // dil, w).transpose(1, 0, 2).reshape(s, w)


def _mixer_fwd(projs, prm, aux):
    proj = projs[0]
    s = proj.shape[0]
    mla, gqa, merge = _mixer_calls(proj, prm, aux)
    q_a, k_a, v_a = _fwd_call("prep_mla", _mla_tile, mla["steps"], mla["rows"], mla["params"], mla["aux"], mla["outs"])
    o_a, lse_a = _dense_fwd_call(q_a, k_a, v_a, MLA_QK ** -0.5, "mla")
    q_b, k_b, v_b = _fwd_call("prep_gqa", _gqa_tile, gqa["steps"], gqa["rows"], gqa["params"], gqa["aux"], gqa["outs"])
    grp = GQA_HEADS // GQA_KV_HEADS
    o_b, lse_b = _dense_fwd_call(q_b.reshape(GQA_KV_HEADS, grp * s, HEAD_DIM), k_b, v_b, HEAD_DIM ** -0.5, "gqa")
    scale = HEAD_DIM ** -0.5
    ocs, lses = [], []
    for gi, (_, dil) in enumerate(DIL_PATTERNS):
        o, lse = _band_fwd_call(projs[gi], DIL_BLK if gi == 0 else 0, prm["bias_dil"][gi], prm["no_sink"], dil, 1,
                                QW, scale, "dil%d" % gi)
        ocs.append(o)
        lses.append(lse)
    o_d, lse_d = _band_fwd_call(proj, WIN_BLK, prm["bias_win"], prm["sink"], 1, WIN_HEADS // WIN_KV_HEADS,
                                WIN_KV_HEADS * HEAD_DIM, scale, "win")
    rows = _merge_rows(proj, o_a, o_b.reshape(GQA_HEADS, s, HEAD_DIM), ocs, lses, o_d, merge["tm"])
    mix = _fwd_call("merge", _merge_tile, merge["steps"], rows, [prm["w_branch"]], _merge_aux(aux),
                    [((s, prm["w_branch"].shape[2]), _rows(merge["tm"], prm["w_branch"].shape[2]), BF16)])[0]
    return mix, (q_a, k_a, v_a, o_a, lse_a, q_b, k_b, v_b, o_b, lse_b, ocs, lses, o_d, lse_d)


def _mixer_bwd(projs, prm, aux, saved, dmix):
    proj = projs[0]
    s = proj.shape[0]
    q_a, k_a, v_a, o_a, lse_a, q_b, k_b, v_b, o_b, lse_b, ocs, lses, o_d, lse_d = saved
    dils = [dil for _, dil in DIL_PATTERNS]
    mla, gqa, merge = _mixer_calls(proj, prm, aux)
    tm, d_model = merge["tm"], prm["w_branch"].shape[2]
    grp = GQA_HEADS // GQA_KV_HEADS
    scale = HEAD_DIM ** -0.5

    rows = _merge_rows(proj, o_a, o_b.reshape(GQA_HEADS, s, HEAD_DIM), ocs, lses, o_d, tm)
    grads, (dw_branch,) = _vjp_call(
        "merge", _merge_tile, merge["steps"], rows, [prm["w_branch"]], _merge_aux(aux), [(dmix, _rows(tm, d_model))],
        [merge["window"]] + [(a.shape, spec) for a, spec in rows[1:]])
    dproj, do_a, do_b, docs, dlses, do_d = grads[0], grads[1], grads[2], grads[3:6], grads[6:9], grads[9]

    dq_a, dk_a, dv_a = _dense_bwd_call(q_a, k_a, v_a, o_a, lse_a, do_a, MLA_QK ** -0.5, "mla")
    (dproj,), dmla = _vjp_call("prep_mla", _mla_tile, mla["steps"], mla["rows"], mla["params"], mla["aux"],
                               [(t, e[1]) for t, e in zip((dq_a, dk_a, dv_a), mla["outs"])],
                               [mla["window"]], into=dproj)
    dq_b, dk_b, dv_b = _dense_bwd_call(q_b.reshape(GQA_KV_HEADS, grp * s, HEAD_DIM), k_b, v_b, o_b, lse_b,
                                       do_b.reshape(GQA_KV_HEADS, grp * s, HEAD_DIM), scale, "gqa")
    (dproj,), dgqa = _vjp_call("prep_gqa", _gqa_tile, gqa["steps"], gqa["rows"], gqa["params"], gqa["aux"],
                               [(t, e[1]) for t, e in zip((dq_b.reshape(GQA_HEADS, s, HEAD_DIM), dk_b, dv_b), gqa["outs"])],
                               [gqa["window"]], into=dproj)
    dproj, dbias_win, dsink = _band_bwd_call(proj, o_d, do_d, lse_d, None, prm["bias_win"], prm["sink"], dproj,
                                             WIN_BLK, 1, WIN_HEADS // WIN_KV_HEADS, WIN_KV_HEADS * HEAD_DIM, scale, "win")
    dbias_dil, dprojs = [], []
    for gi, dil in enumerate(dils):
        dside, dbias, _ = _band_bwd_call(
            projs[gi], ocs[gi], docs[gi].reshape(s, QW), lses[gi], dlses[gi].reshape(s, QW),
            prm["bias_dil"][gi], prm["no_sink"], dproj if gi == 0 else None, DIL_BLK if gi == 0 else 0, dil, 1, QW,
            scale, "dil%d" % gi)
        if gi == 0:
            dproj = dside
        else:
            dprojs.append(dside)
        dbias_dil.append(dbias)
    dprm = dict(g_q=dmla[0], g_kv=dmla[1], w_q=dmla[2], w_k=dmla[3], w_v=dmla[4], gq=dgqa[0], gk=dgqa[1],
                bias_dil=dbias_dil, bias_win=dbias_win, sink=dsink, no_sink=jnp.zeros_like(prm["no_sink"]),
                w_branch=dw_branch)
    return [dproj] + dprojs, {k: jax.tree.map(lambda g, p: g.astype(p.dtype), v, prm[k]) for k, v in dprm.items()}


def _layer_fwd(x, w, aux):
    s, d = x.shape
    tr = _pick(s, (256,))
    dils = [dil for _, dil in DIL_PATTERNS]

    def norm_forms(r, p, a):
        y = _rms(r[0], p[0])
        return [y, y.T] + [_dot(q, y, 1, 0).reshape(dil, tr // dil, d) for q, dil in zip(a, dils[1:])]

    forms = _fwd_call(
        "norm", norm_forms, s // tr, [(x, _rows(tr, d))], [w["norm_g"]], [(q, _whole(q.shape)) for q in aux[9:11]],
        [((s, d), _rows(tr, d), BF16), ((d, s), pl.BlockSpec((d, tr), lambda i: (0, i)), BF16)]
        + [((dil, s // dil, d), pl.BlockSpec((dil, tr // dil, d), lambda i: (0, i, 0)), BF16) for dil in dils[1:]])
    xn_t, xns = forms[1], [forms[0]] + [t.reshape(s, d) for t in forms[2:]]
    projs = [_mm(a, b, "nt", "proj%d_fwd" % i, BF16) for i, (a, b) in enumerate(zip(xns, w["w_in_t"]))]
    mix, saved = _mixer_fwd(projs, w["mixer"], aux)
    return _mm(mix, w["w_out"], "nn", "out_proj_nn"), (x, w, aux, xns, xn_t, projs, mix, saved)


@jax.custom_vjp
def _layer_core(x, w, aux):
    return _layer_fwd(x, w, aux)[0]


def _layer_core_bwd(res, dout):
    x, w, aux, xns, xn_t, projs, mix, saved = res
    s, d = x.shape
    tr = _pick(s, (256, 128, 64, 32, 16, 8))
    dils = [dil for _, dil in DIL_PATTERNS]
    dmix = _mm(dout, w["w_out"], "nt", "out_proj_nt")
    dw_out = _mm(mix.T, dout, "nn", "out_proj_dw", w["w_out"].dtype)
    dprojs, dmixer = _mixer_bwd(projs, w["mixer"], aux, saved, dmix)
    side = jnp.concatenate([_from_residues(dp, r) for dp, r in zip(dprojs[1:], dils[1:])], axis=1)
    dxn_terms = [_mm(dprojs[0], w["w_in_t"][0], "nn", "proj0_dx"),
                 _mm(side, jnp.concatenate(w["w_in_t"][1:], axis=0), "nn", "proj_side_dx")]
    dw_in_t = [_mm(a_t, dp, "nn", "proj%d_dw" % i, wi.dtype).T
               for i, (a_t, dp, wi) in enumerate(zip([xn_t] + [a.T for a in xns[1:]], dprojs, w["w_in_t"]))]
    (dx,), (dg,) = _vjp_call("norm", lambda r, p, a: _norm_tile(r, p, a) * len(dxn_terms), s // tr, [(x, _rows(tr, d))],
                             [w["norm_g"]], [], [(t, _rows(tr, d)) for t in dxn_terms], [((s, d), _rows(tr, d))])
    dw = dict(norm_g=dg, w_in_t=dw_in_t, mixer=dmixer, w_out=dw_out)
    return dx, dw, tuple(jnp.zeros_like(t) for t in aux)


_layer_core.defvjp(lambda x, w, aux: _layer_fwd(x, w, aux), _layer_core_bwd)


def _rope_angles(pos, dim):
    inv = ROPE_THETA ** (-jnp.arange(0, dim, 2, dtype=F32) / dim)
    return pos.astype(F32)[:, None] * inv[None, :]


def _rope_tables(s):
    pos = jnp.arange(s, dtype=jnp.int32)
    rows = s // GRID_W
    row_idx = jnp.repeat(jnp.arange(rows, dtype=jnp.int32), GRID_W)
    col_idx = jnp.tile(jnp.arange(GRID_W, dtype=jnp.int32), rows)
    a1 = _rope_angles(pos, MLA_ROPE)
    ar = _rope_angles(row_idx, HEAD_DIM // 2)
    ac = _rope_angles(col_idx, HEAD_DIM // 2)
    ones, zeros = jnp.ones((s, MLA_NOPE), F32), jnp.zeros((s, MLA_NOPE), F32)
    mla_cos = jnp.tile(jnp.concatenate([ones, jnp.cos(a1), jnp.cos(a1)], axis=1), (1, MLA_HEADS))
    mla_sin = jnp.tile(jnp.concatenate([zeros, -jnp.sin(a1), jnp.sin(a1)], axis=1), (1, MLA_HEADS))
    gqa_cos = jnp.tile(jnp.concatenate([jnp.cos(ar), jnp.cos(ar), jnp.cos(ac), jnp.cos(ac)], axis=1), (1, GQA_HEADS))
    gqa_sin = jnp.tile(jnp.concatenate([-jnp.sin(ar), jnp.sin(ar), -jnp.sin(ac), jnp.sin(ac)], axis=1), (1, GQA_HEADS))
    return mla_cos, mla_sin, gqa_cos, gqa_sin


def _t5_bucket(rel):
    nb = T5_BUCKETS // 2
    max_exact = nb // 2
    n = jnp.abs(rel)
    nf = jnp.maximum(n, 1).astype(F32)
    large = max_exact + (jnp.log(nf / max_exact) / math.log(T5_MAX_DIST / max_exact) * (nb - max_exact)).astype(jnp.int32)
    large = jnp.minimum(large, nb - 1)
    return jnp.where(rel > 0, nb, 0) + jnp.where(n < max_exact, n, large)


def _band_bias(table, stride, head_lo, heads, half_window):
    b = BAND_BLOCK
    offs = jnp.arange(3 * b)[None, :] - b - jnp.arange(b)[:, None]
    one_hot = (_t5_bucket(offs * stride)[..., None] == jnp.arange(T5_BUCKETS)).astype(F32)
    bias = jnp.dot(one_hot.reshape(b * 3 * b, T5_BUCKETS), table[:, head_lo:head_lo + heads],
                   precision=lax.Precision.HIGHEST)
    bias = bias.T.reshape(heads, b, 3 * b)
    return jnp.where((jnp.abs(offs) <= half_window)[None], bias, NEG_INF)


def _w_in_rows(d):
    mla, gqa, win, dil0 = MLA_BLK * SMALL_W, GQA_BLK * SMALL_W, WIN_BLK * BAND_W, DIL_BLK * BAND_W
    plan, at = [], 0
    for width, target, row in ((256, 0, mla), (128, 0, mla + 256), (32, 0, mla + 384),
                               (256, 0, gqa), (128, 0, gqa + 256), (128, 0, gqa + 384)):
        plan.append((at, width, target, row))
        at += width
    for part in range(3):
        for g in range(len(DIL_PATTERNS)):
            plan.append((at, QW, g, (dil0 if g == 0 else 0) + part * QW))
            at += QW
    for width, row in ((256, win), (128, win + 256), (128, win + 384), (N_BRANCH * BRANCH_W, 0),
                       (N_BRANCH * d, N_BRANCH * BRANCH_W)):
        plan.append((at, width, 0, row))
        at += width
    return plan


@jax.custom_vjp
def _w_in_layout(w_in_t):
    d = w_in_t.shape[1]
    outs = []
    for target, rows in enumerate((P_TOT, BAND_W, BAND_W)):
        parts, at = [], 0
        for start, width, _, row in sorted((p for p in _w_in_rows(d) if p[2] == target), key=lambda p: p[3]):
            if row > at:
                parts.append(jnp.zeros((row - at, d), w_in_t.dtype))
            parts.append(w_in_t[start:start + width])
            at = row + width
        if at < rows:
            parts.append(jnp.zeros((rows - at, d), w_in_t.dtype))
        outs.append(jnp.concatenate(parts, axis=0))
    return outs


def _w_in_layout_fwd(w_in_t):
    return _w_in_layout(w_in_t), None


def _w_in_layout_bwd(_, cts):
    d = cts[0].shape[1]
    return (jnp.concatenate([cts[target][row:row + width] for _, width, target, row in _w_in_rows(d)], axis=0),)


_w_in_layout.defvjp(_w_in_layout_fwd, _w_in_layout_bwd)


def _layer(x, w, l, aux, biases):
    w_kv = w["w_kv_t"][l].T.reshape(MLA_KV_LORA, MLA_HEADS, MLA_NOPE + MLA_V)
    w_k = jnp.concatenate([w_kv[:, :, :MLA_NOPE], jnp.zeros((MLA_KV_LORA, MLA_HEADS, MLA_ROPE), w_kv.dtype)], axis=2)
    dil_bias, win_bias = biases
    prm = dict(
        g_q=w["mla_q_norm_g"][l][None, :], g_kv=w["mla_kv_norm_g"][l][None, :], w_q=w["w_q_t"][l].T,
        w_k=w_k.reshape(MLA_KV_LORA, MLA_HEADS * MLA_QK),
        w_v=w_kv[:, :, MLA_NOPE:].reshape(MLA_KV_LORA, MLA_HEADS * MLA_V),
        gq=jnp.tile(w["gqa_q_norm_g"][l], GQA_HEADS)[None, :], gk=jnp.tile(w["gqa_k_norm_g"][l], GQA_KV_HEADS)[None, :],
        bias_dil=list(dil_bias), bias_win=win_bias, sink=w["win_sink"][l].reshape(WIN_HEADS, 1, 1),
        no_sink=jnp.full((DIL_HEADS, 1, 1), NEG_INF, F32), w_branch=jnp.transpose(w["w_branch_t"][l].reshape(-1, N_BRANCH, BRANCH_W), (1, 2, 0)))
    layer_w = dict(norm_g=w["norm_g"][l][None, :], w_in_t=_w_in_layout(w["w_in_t"][l]), mixer=prm, w_out=w["w_out"][l])
    return x + _layer_core(x, layer_w, aux)


def _local_loss(w, x, target):
    s, d_model = x.shape
    assert d_model == D_MODEL, "the projection's window layout is laid out for d_model 1024"
    place = np.zeros((MLA_ROPE, MLA_HEADS * MLA_QK), np.float32)
    for h in range(MLA_HEADS):
        for i in range(MLA_ROPE):
            place[i, h * MLA_QK + MLA_NOPE + i] = 1.0

    def head_mean(nh):
        m = np.kron(np.eye(nh, dtype=np.float32), np.full((HEAD_DIM, HEAD_DIM), 1.0 / HEAD_DIM, np.float32))
        return jnp.asarray(m)

    merge_tile = _pick(s, (256,))
    norm_tile = _pick(s, (256,))
    aux = _rope_tables(s) + (jnp.asarray(place), head_mean(GQA_HEADS), head_mean(GQA_KV_HEADS)) + tuple(
        jnp.asarray(_interleave_matrix(merge_tile, dil)) for _, dil in DIL_PATTERNS[1:]) + tuple(
        jnp.asarray(_interleave_matrix(norm_tile, dil).T) for _, dil in DIL_PATTERNS[1:])
    table = w["t5_table"]
    dil_bias = [_band_bias(table, dil, gi * DIL_HEADS, DIL_HEADS, window // (2 * dil))
                for gi, (window, dil) in enumerate(DIL_PATTERNS)]
    win_bias = _band_bias(table, 1, len(DIL_PATTERNS) * DIL_HEADS, WIN_HEADS, WIN_HALF)
    for l in range(w["norm_g"].shape[0]):
        x = _layer(x, w, l, aux, (dil_bias, win_bias))
    return _loss_op(x, target, w["final_norm_g"][None, :])


_ANY = pl.BlockSpec(memory_space=pl.ANY)
_MESH = pl.DeviceIdType.MESH


def _all_gather(block, name):
    def body(x_ref, out_ref, send_sems, recv_sems, local_sem):
        x, y, c = lax.axis_index("x"), lax.axis_index("y"), lax.axis_index("c")
        me, sibling = (x, y, c), (x, y, 1 - c)
        chips = [(1 - x, y), (x, 1 - y), (1 - x, 1 - y)]

        def slot(px, py, pc):
            return out_ref.at[4 * px + 2 * py + pc]

        def copy(k, blk, to, src=None):
            return pltpu.make_async_remote_copy(
                src_ref=slot(*blk) if src is None else src, dst_ref=slot(*blk),
                send_sem=send_sems.at[k], recv_sem=recv_sems.at[k], device_id=to, device_id_type=_MESH)

        mine = pltpu.make_async_copy(x_ref, slot(*me), local_sem)
        mine.start()
        first = [copy(0, me, sibling, src=x_ref)]
        first += [copy(1 + j, me, (*chip, c), src=x_ref) for j, chip in enumerate(chips)]
        for cp in first:
            cp.start()
        passed = [copy(4 + j, (*chip, c), sibling) for j, chip in enumerate(chips)]
        for j, chip in enumerate(chips):
            copy(1 + j, (*chip, c), me).wait_recv()
            passed[j].start()
        copy(0, sibling, me).wait_recv()
        for j, chip in enumerate(chips):
            copy(4 + j, (*chip, 1 - c), me).wait_recv()
        for cp in first + passed:
            cp.wait_send()
        mine.wait()

    return pl.pallas_call(
        body,
        out_shape=jax.ShapeDtypeStruct((N_DEV,) + block.shape, block.dtype),
        in_specs=[_ANY],
        out_specs=_ANY,
        scratch_shapes=[pltpu.SemaphoreType.DMA((7,)), pltpu.SemaphoreType.DMA((7,)), pltpu.SemaphoreType.DMA],
        name=name,
    )(block)


def _swap_with_sibling(blocks, name):
    chips = blocks.shape[0]

    def body(x_ref, out_ref, send_sems, recv_sems):
        x, y, c = lax.axis_index("x"), lax.axis_index("y"), lax.axis_index("c")
        copies = [pltpu.make_async_remote_copy(
            src_ref=x_ref.at[k, 1 - c], dst_ref=out_ref.at[k], send_sem=send_sems.at[k], recv_sem=recv_sems.at[k],
            device_id=(x, y, 1 - c), device_id_type=_MESH) for k in range(chips)]
        for cp in copies:
            cp.start()
        for cp in copies:
            cp.wait()

    return pl.pallas_call(
        body,
        out_shape=jax.ShapeDtypeStruct((chips,) + blocks.shape[2:], blocks.dtype),
        in_specs=[_ANY],
        out_specs=_ANY,
        scratch_shapes=[pltpu.SemaphoreType.DMA((chips,)), pltpu.SemaphoreType.DMA((chips,))],
        name=name,
    )(blocks)


def _add_sibling(blocks, theirs, name):
    chips, _, rows, w = blocks.shape
    tr = _row_tile(rows, 16, 4096)

    def body(b_ref, t_ref, o_ref):
        mine = b_ref[0, lax.axis_index("c")]
        o_ref[0] = (mine.astype(F32) + t_ref[0].astype(F32)).astype(o_ref.dtype)

    return pl.pallas_call(
        body,
        grid=(chips, rows // tr),
        in_specs=[pl.BlockSpec((1, 2, tr, w), lambda k, i: (k, 0, i, 0)), pl.BlockSpec((1, tr, w), lambda k, i: (k, i, 0))],
        out_specs=pl.BlockSpec((1, tr, w), lambda k, i: (k, i, 0)),
        out_shape=jax.ShapeDtypeStruct(theirs.shape, theirs.dtype),
        name=name,
        compiler_params=_params("parallel", "parallel"),
    )(blocks, theirs)


def _exchange_chips(partials, name):
    n_chips = partials.shape[0]

    def body(x_ref, out_ref, send_sems, recv_sems, local_sem):
        x, y, c = lax.axis_index("x"), lax.axis_index("y"), lax.axis_index("c")
        me = 2 * x + y
        mine = pltpu.make_async_copy(x_ref.at[me], out_ref.at[me], local_sem)
        mine.start()
        copies, landed = [], []
        for k in range(1, n_chips):
            px = 1 - x if k & 2 else x
            py = 1 - y if k & 1 else y
            peer = 2 * px + py
            copies.append(pltpu.make_async_remote_copy(
                src_ref=x_ref.at[peer], dst_ref=out_ref.at[me], send_sem=send_sems.at[k - 1],
                recv_sem=recv_sems.at[k - 1], device_id=(px, py, c), device_id_type=_MESH))
            landed.append(pltpu.make_async_remote_copy(
                src_ref=x_ref.at[peer], dst_ref=out_ref.at[peer], send_sem=send_sems.at[k - 1],
                recv_sem=recv_sems.at[k - 1], device_id=(px, py, c), device_id_type=_MESH))
        for cp in copies:
            cp.start()
        for cp in landed:
            cp.wait_recv()
        for cp in copies:
            cp.wait_send()
        mine.wait()

    return pl.pallas_call(
        body,
        out_shape=jax.ShapeDtypeStruct(partials.shape, partials.dtype),
        in_specs=[_ANY],
        out_specs=_ANY,
        scratch_shapes=[pltpu.SemaphoreType.DMA((n_chips - 1,)), pltpu.SemaphoreType.DMA((n_chips - 1,)),
                        pltpu.SemaphoreType.DMA],
        name=name,
    )(partials)


def _sum_slots(parts, name):
    slots, rows, w = parts.shape
    tr = _row_tile(rows, 16 if parts.dtype == BF16 else 8, 4096)

    def body(p_ref, o_ref):
        acc = p_ref[0].astype(F32)
        for j in range(1, slots):
            acc = acc + p_ref[j].astype(F32)
        o_ref[...] = acc

    return pl.pallas_call(
        body,
        grid=(rows // tr,),
        in_specs=[pl.BlockSpec((slots, tr, w), lambda i: (0, i, 0))],
        out_specs=pl.BlockSpec((tr, w), lambda i: (i, 0)),
        out_shape=jax.ShapeDtypeStruct((rows, w), F32),
        name=name,
        compiler_params=_params("parallel"),
    )(parts)


def _adamw(w, g, m, v, name):
    rows, width = w.shape
    lanes = -(-width // LANES) * LANES
    tr = _row_tile(rows, 8, max(8, V7X_VMEM_LIMIT // 3 // (7 * 2 * 4 * lanes) // 8 * 8))

    def body(w_ref, g_ref, m_ref, v_ref, d_ref, nm_ref, nv_ref):
        g_ = g_ref[...]
        m_ = ADAM_B1 * m_ref[...] + (1.0 - ADAM_B1) * g_
        v_ = ADAM_B2 * v_ref[...] + (1.0 - ADAM_B2) * jnp.square(g_)
        m_hat = m_ / (1.0 - ADAM_B1 ** ADAM_STEP)
        v_hat = v_ / (1.0 - ADAM_B2 ** ADAM_STEP)
        d_ref[...] = -ADAM_LR * (m_hat / (jnp.sqrt(v_hat) + ADAM_EPS) + ADAM_WD * w_ref[...])
        nm_ref[...] = m_
        nv_ref[...] = v_

    spec = pl.BlockSpec((tr, width), lambda i: (i, 0))
    return pl.pallas_call(
        body,
        grid=(rows // tr,),
        in_specs=[spec] * 4,
        out_specs=[spec] * 3,
        out_shape=[jax.ShapeDtypeStruct((rows, width), F32)] * 3,
        name=name,
        compiler_params=_params("parallel"),
    )(w, g, m, v)


_SHARDED = (("w_in", 2), ("w_mla_q_up", 2), ("w_mla_kv_up", 2), ("w_branch", 3), ("w_out", 1))
_REPLICATED = ("norm_g", "mla_q_norm_g", "mla_kv_norm_g", "gqa_q_norm_g", "gqa_k_norm_g", "win_sink", "t5_table",
               "final_norm_g")


def _pack(arrays, row_multiple):
    flat = jnp.concatenate([a.reshape(-1) for a in arrays])
    rows = -(-flat.shape[0] // (LANES * row_multiple)) * row_multiple
    return jnp.pad(flat, (0, rows * LANES - flat.shape[0])).reshape(rows, LANES)


def _unpack(packed, shapes):
    flat, out, at = packed.reshape(-1), [], 0
    for shp in shapes:
        n = int(np.prod(shp))
        out.append(flat[at:at + n].reshape(shp))
        at += n
    return out


_TO_WIRE = {
    "w_in": lambda t: jnp.swapaxes(t, 1, 2), "w_mla_q_up": lambda t: jnp.swapaxes(t, 1, 2),
    "w_mla_kv_up": lambda t: jnp.swapaxes(t, 1, 2),
    "w_branch": lambda t: jnp.transpose(t, (0, 3, 1, 2)).reshape(t.shape[0], t.shape[3], -1), "w_out": lambda t: t}
_FROM_WIRE = {
    "w_in": lambda t, shp: jnp.swapaxes(t, 1, 2), "w_mla_q_up": lambda t, shp: jnp.swapaxes(t, 1, 2),
    "w_mla_kv_up": lambda t, shp: jnp.swapaxes(t, 1, 2),
    "w_branch": lambda t, shp: jnp.transpose(t.reshape(shp[0], shp[3], shp[1], shp[2]), (0, 2, 3, 1)),
    "w_out": lambda t, shp: t}
_WIRE_NAME = {"w_in": "w_in_t", "w_mla_q_up": "w_q_t", "w_mla_kv_up": "w_kv_t", "w_branch": "w_branch_t",
              "w_out": "w_out"}


def _transpose_blocks(t, dtype, name):
    depth, a, b = t.shape

    def body(x_ref, o_ref):
        o_ref[0] = x_ref[0].T.astype(o_ref.dtype)

    return pl.pallas_call(
        body,
        grid=(depth,),
        in_specs=[pl.BlockSpec((1, a, b), lambda i: (i, 0, 0))],
        out_specs=pl.BlockSpec((1, b, a), lambda i: (i, 0, 0)),
        out_shape=jax.ShapeDtypeStruct((depth, b, a), dtype),
        name=name,
        compiler_params=_params("parallel"),
    )(t)


def _join_shards(gathered, wire_shapes):
    out, at = [], 0
    for depth, cut, rest in wire_shapes:
        n = depth * cut * rest // LANES
        blk = gathered[:, at:at + n].reshape(N_DEV, depth, cut, rest)
        out.append(jnp.moveaxis(blk, 0, 1).reshape(depth, N_DEV * cut, rest))
        at += n
    return out


def _split_shards(fulls, wire_shapes):
    parts = []
    for full, (depth, cut, rest) in zip(fulls, wire_shapes):
        blk = jnp.moveaxis(full.reshape(depth, N_DEV, cut, rest), 1, 0)
        parts.append(blk.reshape(N_DEV, depth * cut * rest // LANES, LANES))
    packed = jnp.concatenate(parts, axis=1)
    return packed.reshape((N_DEV // 2, 2) + packed.shape[1:])


def kernel(x, norm_g, w_in, mla_q_norm_g, mla_kv_norm_g, w_mla_q_up, w_mla_kv_up, gqa_q_norm_g, gqa_k_norm_g, win_sink, t5_table, w_branch, w_out, final_norm_g, loss_target, m_norm_g, m_w_in, m_mla_q_norm_g, m_mla_kv_norm_g, m_w_mla_q_up, m_w_mla_kv_up, m_gqa_q_norm_g, m_gqa_k_norm_g, m_win_sink, m_t5_table, m_w_branch, m_w_out, m_final_norm_g, v_norm_g, v_w_in, v_mla_q_norm_g, v_mla_kv_norm_g, v_w_mla_q_up, v_w_mla_kv_up, v_gqa_q_norm_g, v_gqa_k_norm_g, v_win_sink, v_t5_table, v_w_branch, v_w_out, v_final_norm_g):
    given = dict(locals())
    names = ("norm_g", "w_in", "mla_q_norm_g", "mla_kv_norm_g", "w_mla_q_up", "w_mla_kv_up", "gqa_q_norm_g",
             "gqa_k_norm_g", "win_sink", "t5_table", "w_branch", "w_out", "final_norm_g")
    shard_names = [n for n, _ in _SHARDED]
    shard_shapes = [given[n].shape for n in shard_names]

    wire = [_transpose_blocks(given[n], BF16, "w_in_to_wire") if n == "w_in" else _TO_WIRE[n](given[n]).astype(BF16)
            for n in shard_names]
    wire_shapes = [t.shape for t in wire]
    gathered = _all_gather(jnp.concatenate([t.reshape(-1, LANES) for t in wire]), "gather_weights")
    weights = {n: given[n] for n in _REPLICATED}
    weights.update(zip([_WIRE_NAME[n] for n in shard_names], _join_shards(gathered, wire_shapes)))

    loss, (gw, gx) = jax.value_and_grad(_local_loss, argnums=(0, 1))(weights, x[0], loss_target[0])
    loss = lax.psum(loss, ("x", "y", "c"))

    send = _split_shards([gw[_WIRE_NAME[n]] for n in shard_names], wire_shapes)
    partials = _add_sibling(send, _swap_with_sibling(send, "swap_grads"), "add_sibling_grads")
    g_wire = _unpack(_sum_slots(_exchange_chips(partials, "scatter_grads"), "sum_grads"), wire_shapes)
    g_shard = [_transpose_blocks(t, F32, "w_in_from_wire") if n == "w_in" else _FROM_WIRE[n](t, shp)
               for n, t, shp in zip(shard_names, g_wire, shard_shapes)]
    rep_shapes = [given[n].shape for n in _REPLICATED]
    g_rep = _unpack(_sum_slots(_all_gather(_pack([gw[n] for n in _REPLICATED], 8), "gather_small_grads"),
                               "sum_small_grads"), rep_shapes)
    grads = dict(zip(shard_names, g_shard))
    grads.update(zip(_REPLICATED, g_rep))

    def update(group, shapes, row_multiple, name):
        outs = _adamw(*[_pack([src[n] for n in group], row_multiple) for src in (
            given, grads, {n: given["m_" + n] for n in group}, {n: given["v_" + n] for n in group})], name)
        return [dict(zip(group, _unpack(o, shapes))) for o in outs]

    def update_shard(n):
        shp = given[n].shape
        outs = _adamw(*[t.reshape(-1, shp[-1]) for t in (given[n], grads[n], given["m_" + n], given["v_" + n])],
                      "adamw_" + n)
        return [o.reshape(shp) for o in outs]

    per_shard = [update_shard(n) for n in shard_names]
    big = [dict(zip(shard_names, [u[k] for u in per_shard])) for k in range(3)]
    small = update(list(_REPLICATED), rep_shapes, 8, "adamw_replicated")
    delta, new_m, new_v = [{**b, **s_} for b, s_ in zip(big, small)]
    return (loss, gx[None], *[grads[n] for n in names], *[delta[n] for n in names],
            *[new_m[n] for n in names], *[new_v[n] for n in names])
```

```python
import functools
import math

import jax
import jax.numpy as jnp
import numpy as np
from jax import lax
from jax.experimental import pallas as pl
from jax.experimental.pallas import tpu as pltpu

F32 = jnp.float32
BF16 = jnp.bfloat16
N_DEV = 8
LANES = 128
HALF = LANES // 2
V7X_VMEM_LIMIT = 56 * 1024 * 1024

EPS = 1e-6
NEG_INF = -1e30
LOG2E = 1.4426950408889634
ROPE_THETA = 10000.0
GRID_W = 64
HEAD_DIM = 64
N_BRANCH = 4
BRANCH_W = 256
MLA_HEADS, MLA_Q_LORA, MLA_KV_LORA, MLA_NOPE, MLA_ROPE, MLA_V = 4, 256, 128, 64, 32, 64
MLA_QK = MLA_NOPE + MLA_ROPE
GQA_HEADS, GQA_KV_HEADS = 4, 2
DIL_PATTERNS = ((128, 1), (512, 4), (2048, 16))
DIL_HEADS = 4
WIN_HEADS, WIN_KV_HEADS, WIN_HALF = 4, 2, 128
T5_BUCKETS, T5_MAX_DIST = 32, 1024
BAND_BLOCK = 128
ADAM_LR, ADAM_B1, ADAM_B2, ADAM_EPS, ADAM_WD, ADAM_STEP = 0.001, 0.9, 0.999, 1e-08, 0.01, 10

D_MODEL = 1024
GM_W, SMALL_W, BAND_W = 5120, 512, 768
MLA_BLK, GQA_BLK, WIN_BLK, DIL_BLK = 10, 11, 8, 9
P_TOT = 7680
QW = 256


def _params(*sem):
    return pltpu.CompilerParams(dimension_semantics=sem, vmem_limit_bytes=V7X_VMEM_LIMIT)


def _pick(n, cands):
    for c in cands:
        if n % c == 0:
            return c
    return n


def _row_tile(rows, unit, cap):
    best = unit
    for t in range(unit, min(rows, cap) + 1, unit):
        if rows % t == 0:
            best = t
    assert rows % best == 0
    return best


def _dot(a, b, ca, cb):
    return lax.dot_general(a.astype(BF16), b.astype(BF16), (((ca,), (cb,)), ((), ())), preferred_element_type=F32)


def _bmm(a, b, ca, cb):
    return lax.dot_general(a, b, (((ca,), (cb,)), ((0,), (0,))), preferred_element_type=F32)


@jax.custom_vjp
def _bdot(a, b):
    return _dot(a, b, 1, 0)


def _bdot_fwd(a, b):
    return _dot(a, b, 1, 0), (a, b)


def _bdot_bwd(res, g):
    a, b = res
    return _dot(g, b, 1, 1), _dot(a, g, 0, 0)


_bdot.defvjp(_bdot_fwd, _bdot_bwd)


def _hdot(a, c):
    return lax.dot_general(a, c, (((1,), (0,)), ((), ())), precision=lax.Precision.HIGHEST, preferred_element_type=F32)


@functools.partial(jax.custom_vjp, nondiff_argnums=(1,))
def _lane_roll(x, shift):
    return pltpu.roll(x, shift, 1)


def _lane_roll_fwd(x, shift):
    return pltpu.roll(x, shift, 1), None


def _lane_roll_bwd(shift, _, g):
    return (pltpu.roll(g, g.shape[1] - shift, 1),)


_lane_roll.defvjp(_lane_roll_fwd, _lane_roll_bwd)


@functools.partial(jax.custom_vjp, nondiff_argnums=(1,))
def _lane_ranges(x, cut):
    bounds, _ = cut
    return tuple(x[:, lo:hi] for lo, hi in zip(bounds[:-1], bounds[1:]))


def _lane_ranges_fwd(x, cut):
    return _lane_ranges(x, cut), None


def _lane_ranges_bwd(cut, _, cts):
    bounds, width = cut
    parts = list(cts)
    if bounds[-1] < width:
        parts.append(jnp.zeros((cts[0].shape[0], width - bounds[-1]), cts[0].dtype))
    return (jnp.concatenate(parts, axis=1),)


_lane_ranges.defvjp(_lane_ranges_fwd, _lane_ranges_bwd)


def _lanes(x, bounds):
    return _lane_ranges(x, (tuple(bounds), x.shape[1]))


@jax.custom_vjp
def _unstack(x):
    return tuple(x[i] for i in range(x.shape[0]))


def _unstack_fwd(x):
    return _unstack(x), None


def _unstack_bwd(_, cts):
    return (jnp.stack(cts, axis=0),)


_unstack.defvjp(_unstack_fwd, _unstack_bwd)


@functools.partial(jax.custom_vjp, nondiff_argnums=(1,))
def _split_heads(x, h):
    d = x.shape[1] // h
    return jnp.stack([x[:, i * d:(i + 1) * d] for i in range(h)], axis=0)


def _split_heads_fwd(x, h):
    return _split_heads(x, h), None


def _split_heads_bwd(h, _, ct):
    return (jnp.concatenate([ct[i] for i in range(h)], axis=1),)


_split_heads.defvjp(_split_heads_fwd, _split_heads_bwd)


def _join_heads(x):
    return jnp.concatenate(_unstack(x), axis=1)


def _rope(x, cos_t, sin_t, half):
    w = x.shape[1]
    lane = lax.broadcasted_iota(jnp.int32, (1, w), 1)
    first = (lane % (2 * half)) < half
    partner = jnp.where(first, _lane_roll(x, w - half), _lane_roll(x, half))
    return x * cos_t + partner * sin_t


def _rms(x, g):
    return x * lax.rsqrt(jnp.mean(x * x, axis=-1, keepdims=True) + EPS) * g


def _rows(tr, w, col=0):
    return pl.BlockSpec((tr, w), lambda i: (i, col))


def _head_rows(h, tr, d):
    return pl.BlockSpec((h, tr, d), lambda i: (0, i, 0))


def _whole(shape):
    nd = len(shape)
    return pl.BlockSpec(tuple(shape), lambda i: (0,) * nd)


def _fwd_call(name, fn, steps, rows, params, aux, outs):
    nr, npar, na = len(rows), len(params), len(aux)

    def body(*refs):
        vals = [x[...].astype(F32) for x in refs[:nr + npar + na]]
        res = fn(vals[:nr], vals[nr:nr + npar], vals[nr + npar:])
        for o_ref, o in zip(refs[nr + npar + na:], res):
            o_ref[...] = o.astype(o_ref.dtype)

    return pl.pallas_call(
        body,
        grid=(steps,),
        in_specs=[s for _, s in rows] + [_whole(p.shape) for p in params] + [s for _, s in aux],
        out_specs=[e[1] for e in outs],
        out_shape=[jax.ShapeDtypeStruct(e[0], e[2] if len(e) > 2 else F32) for e in outs],
        name=name + "_fwd",
        compiler_params=_params("parallel"),
    )(*[a for a, _ in rows], *params, *[a for a, _ in aux])


def _vjp_call(name, fn, steps, rows, params, aux, cts, row_grads, into=None):
    nr, npar, na, nc = len(rows), len(params), len(aux), len(cts)
    n_in = nr + npar + na + nc
    lead = 0 if into is None else 1

    def body(*refs):
        refs = refs[lead:]
        vals = [x[...].astype(F32) for x in refs[:n_in]]
        r, p, a, d = vals[:nr], vals[nr:nr + npar], vals[nr + npar:nr + npar + na], vals[nr + npar + na:]
        out_refs = refs[n_in:]
        _, vjp = jax.vjp(lambda r_, p_: tuple(fn(r_, p_, a)), r, p)
        dr, dp = vjp(tuple(d))
        for o_ref, o in zip(out_refs[:nr], dr):
            o_ref[...] = o.astype(o_ref.dtype)

        @pl.when(pl.program_id(0) == 0)
        def _():
            for o_ref in out_refs[nr:]:
                o_ref[...] = jnp.zeros_like(o_ref)

        for o_ref, o in zip(out_refs[nr:], dp):
            o_ref[...] += o

    outs = pl.pallas_call(
        body,
        grid=(steps,),
        in_specs=([] if into is None else [pl.BlockSpec(memory_space=pl.ANY)])
        + [s for _, s in rows] + [_whole(p.shape) for p in params] + [s for _, s in aux] + [s for _, s in cts],
        out_specs=[e[1] for e in row_grads] + [_whole(p.shape) for p in params],
        out_shape=[jax.ShapeDtypeStruct(e[0], e[2] if len(e) > 2 else F32) for e in row_grads]
        + [jax.ShapeDtypeStruct(p.shape, F32) for p in params],
        input_output_aliases={} if into is None else {0: 0},
        name=name + "_bwd",
        compiler_params=_params("arbitrary"),
    )(*([] if into is None else [into]), *[a for a, _ in rows], *params, *[a for a, _ in aux], *[a for a, _ in cts])
    return list(outs[:nr]), list(outs[nr:])


def _norm_tile(r, p, a):
    return (_rms(r[0], p[0]),)


def _mm(a, b, mode, name, out_dtype=F32):
    if mode == "nn":
        (m, k), n = a.shape, b.shape[1]
    elif mode == "nt":
        (m, k), n = a.shape, b.shape[0]
    else:
        (k, m), n = a.shape, b.shape[1]
    tn = _pick(n, (1024, 768, 512, 384, 256, 128))
    budget = V7X_VMEM_LIMIT * 3 // 4
    out_bytes = 4 + 2 * np.dtype(out_dtype).itemsize

    def tiles():
        for tm in (2048, 1024, 512, 256, 128):
            for tk in (512, 256, 128) if mode == "tn" else (4096, 1024, 768, 512, 384, 256, 128):
                need = 2 * tk * (tm * a.dtype.itemsize + tn * b.dtype.itemsize) + tm * tn * out_bytes
                if m % tm == 0 and k % tk == 0 and need <= budget:
                    return tm, tk
        return _pick(m, (128,)), _pick(k, (128,))

    tm, tk = tiles()
    nk = k // tk

    def body(*refs):
        a_ref, b_ref, o_ref, acc_ref = refs
        kk = pl.program_id(2)
        if mode == "nn":
            part = _dot(a_ref[...], b_ref[...], 1, 0)
        elif mode == "nt":
            part = _dot(a_ref[...], b_ref[...], 1, 1)
        else:
            part = _dot(a_ref[...], b_ref[...], 0, 0)
        if nk == 1:
            o_ref[...] = part.astype(o_ref.dtype)
        else:
            @pl.when(kk == 0)
            def _():
                acc_ref[...] = part

            @pl.when(kk > 0)
            def _():
                acc_ref[...] += part

            @pl.when(kk == nk - 1)
            def _():
                o_ref[...] = acc_ref[...].astype(o_ref.dtype)

    if mode == "nn":
        a_spec = pl.BlockSpec((tm, tk), lambda i, j, kk: (i, kk))
        b_spec = pl.BlockSpec((tk, tn), lambda i, j, kk: (kk, j))
    elif mode == "nt":
        a_spec = pl.BlockSpec((tm, tk), lambda i, j, kk: (i, kk))
        b_spec = pl.BlockSpec((tn, tk), lambda i, j, kk: (j, kk))
    else:
        a_spec = pl.BlockSpec((tk, tm), lambda i, j, kk: (kk, i))
        b_spec = pl.BlockSpec((tk, tn), lambda i, j, kk: (kk, j))
    o_spec = pl.BlockSpec((tm, tn), lambda i, j, kk: (i, j))
    return pl.pallas_call(
        body,
        grid=(m // tm, n // tn, nk),
        in_specs=[a_spec, b_spec],
        out_specs=o_spec,
        out_shape=jax.ShapeDtypeStruct((m, n), out_dtype),
        scratch_shapes=[pltpu.VMEM((tm, tn), F32)],
        name=name,
        compiler_params=_params("parallel", "parallel", "arbitrary"),
    )(a, b)


def _dense_fwd_call(q, k, v, scale, name):
    n, sq, d = q.shape
    sk, dv = k.shape[1], v.shape[2]
    tq = _pick(sq, (512, 256, 128))
    c = scale * LOG2E

    nkb = 1

    def body(q_ref, k_ref, v_ref, o_ref, lse_ref, m_s, acc_s, vext_s):
        j = pl.program_id(2)

        @pl.when(j == 0)
        def _():
            m_s[...] = jnp.full_like(m_s, NEG_INF)
            acc_s[...] = jnp.zeros_like(acc_s)
            vext_s[...] = jnp.ones_like(vext_s)

        vext_s[:, :dv] = v_ref[0].astype(BF16)
        m_old = m_s[...]
        s = _dot(q_ref[0], k_ref[0], 1, 1)
        m_new = jnp.maximum(m_old, jnp.max(s, axis=1, keepdims=True))
        p = jnp.exp2(s * c - m_new * c)
        acc = jnp.exp2((m_old - m_new) * c) * acc_s[...] + _dot(p, vext_s[...], 1, 0)
        m_s[...] = m_new
        acc_s[...] = acc

        @pl.when(j == nkb - 1)
        def _():
            l = acc[:, dv:dv + 1]
            o_ref[0] = acc[:, :dv] / l
            lse_ref[0] = m_new * scale + jnp.log(l)

    return pl.pallas_call(
        body,
        grid=(n, sq // tq, nkb),
        in_specs=[
            pl.BlockSpec((1, tq, d), lambda h, i, j: (h, i, 0)),
            pl.BlockSpec((1, sk // nkb, d), lambda h, i, j: (h, j, 0)),
            pl.BlockSpec((1, sk // nkb, dv), lambda h, i, j: (h, j, 0)),
        ],
        out_specs=[
            pl.BlockSpec((1, tq, dv), lambda h, i, j: (h, i, 0)),
            pl.BlockSpec((1, tq, 1), lambda h, i, j: (h, i, 0)),
        ],
        out_shape=[jax.ShapeDtypeStruct((n, sq, dv), F32), jax.ShapeDtypeStruct((n, sq, 1), F32)],
        scratch_shapes=[pltpu.VMEM((tq, 1), F32), pltpu.VMEM((tq, 2 * dv), F32), pltpu.VMEM((sk // nkb, 2 * dv), BF16)],
        name=name + "_fwd",
        compiler_params=_params("parallel", "parallel", "arbitrary"),
    )(q, k, v)


def _dense_bwd_call(q, k, v, o, lse, do, scale, name):
    n, sq, d = q.shape
    sk, dv = k.shape[1], v.shape[2]
    tq, tk = _pick(sq, (1024, 512, 256, 128)), _pick(sk, (2048, 1024, 512, 256, 128))
    c = scale * LOG2E

    def body(q_ref, k_ref, v_ref, o_ref, lse_ref, do_ref, dq_ref, dk_ref, dv_ref):
        j, i = pl.program_id(1), pl.program_id(2)
        qb, kb, vb = q_ref[0].astype(BF16), k_ref[0].astype(BF16), v_ref[0].astype(BF16)
        do_f = do_ref[0]
        dob = do_f.astype(BF16)
        p = jnp.exp2(_dot(qb, kb, 1, 1) * c - lse_ref[0] * LOG2E)
        delta = jnp.sum(do_f * o_ref[0], axis=1, keepdims=True)
        ds = (p * (_dot(dob, vb, 1, 1) - delta)).astype(BF16)
        dv_part = _dot(p, dob, 0, 0)
        dk_part = _dot(ds, qb, 0, 0) * scale
        dq_part = _dot(ds, kb, 1, 0) * scale
        rows = pl.ds(pl.multiple_of(i * tq, tq), tq)

        @pl.when(i == 0)
        def _():
            dk_ref[0] = dk_part
            dv_ref[0] = dv_part

        @pl.when(i > 0)
        def _():
            dk_ref[0] += dk_part
            dv_ref[0] += dv_part

        @pl.when(j == 0)
        def _():
            dq_ref[0, rows, :] = dq_part

        @pl.when(j > 0)
        def _():
            dq_ref[0, rows, :] += dq_part

    return pl.pallas_call(
        body,
        grid=(n, sk // tk, sq // tq),
        in_specs=[
            pl.BlockSpec((1, tq, d), lambda h, j, i: (h, i, 0)),
            pl.BlockSpec((1, tk, d), lambda h, j, i: (h, j, 0)),
            pl.BlockSpec((1, tk, dv), lambda h, j, i: (h, j, 0)),
            pl.BlockSpec((1, tq, dv), lambda h, j, i: (h, i, 0)),
            pl.BlockSpec((1, tq, 1), lambda h, j, i: (h, i, 0)),
            pl.BlockSpec((1, tq, dv), lambda h, j, i: (h, i, 0)),
        ],
        out_specs=[
            pl.BlockSpec((1, sq, d), lambda h, j, i: (h, 0, 0)),
            pl.BlockSpec((1, tk, d), lambda h, j, i: (h, j, 0)),
            pl.BlockSpec((1, tk, dv), lambda h, j, i: (h, j, 0)),
        ],
        out_shape=[
            jax.ShapeDtypeStruct((n, sq, d), F32),
            jax.ShapeDtypeStruct((n, sk, d), F32),
            jax.ShapeDtypeStruct((n, sk, dv), F32),
        ],
        name=name + "_bwd",
        compiler_params=_params("arbitrary", "arbitrary", "arbitrary"),
    )(q, k, v, o, lse, do)


def _head_geometry(h, group):
    pair, a = divmod(h, 2)
    kv_pair, b = divmod(h // group, 2)
    return pair, a, kv_pair, b


def _lane_half():
    return lax.broadcasted_iota(jnp.int32, (1, LANES), 1) // HALF


def _align(x, a, b):
    if a != b:
        x = pltpu.roll(x, HALF, 1)
    return jnp.where(_lane_half() == b, x, 0.0)


def _unalign(x, a, b):
    x = jnp.where(_lane_half() == b, x, 0.0)
    return pltpu.roll(x, HALF, 1) if a != b else x


def _bands(w, pw, nw, lo, kvw, nb):
    b = BAND_BLOCK
    cat = jnp.concatenate([pw[:, lo:lo + kvw], w[:, lo:lo + kvw], nw[:, lo:lo + kvw]], axis=0).astype(BF16)
    out = []
    for g in range(kvw // LANES):
        c3 = cat[:, g * LANES:(g + 1) * LANES].reshape(nb + 2, b, LANES)
        out.append(jnp.concatenate([c3[0:nb], c3[1:nb + 1], c3[2:nb + 2]], axis=1))
    return out


def _edge_mask(first_block, nb, period):
    b = BAND_BLOCK
    blk = (first_block + lax.broadcasted_iota(jnp.int32, (nb, 1, 3 * b), 0)) % period
    col = lax.broadcasted_iota(jnp.int32, (nb, 1, 3 * b), 2)
    outside = ((col < b) & (blk == 0)) | ((col >= 2 * b) & (blk == period - 1))
    return jnp.where(outside, NEG_INF, 0.0)


def _band_geometry(proj, dil):
    rows = proj.shape[0]
    tl = _pick(rows, (1024, 512, 256, 128))
    return rows, tl, tl // BAND_BLOCK, rows // tl, rows // dil // BAND_BLOCK


def _band_in_specs(tl, nb, n_chunks, n_blocks, col, last_step_idle):
    def chunk(i):
        return jnp.minimum(i, n_chunks - 1) if last_step_idle else i

    main = pl.BlockSpec((tl, BAND_W), lambda j, i: (j * n_chunks + chunk(i), col))
    prev = pl.BlockSpec((BAND_BLOCK, BAND_W),
                        lambda j, i: (j * n_blocks + jnp.maximum(chunk(i) * nb - 1, 0), col))
    nxt = pl.BlockSpec((BAND_BLOCK, BAND_W),
                       lambda j, i: (j * n_blocks + jnp.minimum((chunk(i) + 1) * nb, n_blocks - 1), col))
    rows = pl.BlockSpec((tl, QW), lambda j, i: (j * n_chunks + chunk(i), 0))
    return main, prev, nxt, rows


def _band_fwd_call(proj, col, bias, sink, dil, group, kvw, scale, name):
    s_tok = proj.shape[0]
    seq, tl, nb, n_chunks, period = _band_geometry(proj, dil)
    n_blocks = seq // BAND_BLOCK
    heads = bias.shape[0]

    def body(w_ref, pw_ref, nw_ref, bias_ref, sink_ref, o_ref, lse_ref):
        i = pl.program_id(1)
        w, pw, nw = w_ref[...].astype(F32), pw_ref[...].astype(F32), nw_ref[...].astype(F32)
        kb = _bands(w, pw, nw, QW, kvw, nb)
        vb = _bands(w, pw, nw, QW + kvw, kvw, nb)
        edge = _edge_mask(i * nb, nb, period)
        o_acc = [jnp.zeros((tl, LANES), F32) for _ in range(heads // 2)]
        lse_acc = [jnp.zeros((tl, LANES), F32) for _ in range(heads // 2)]
        geom = [_head_geometry(h, group) for h in range(heads)]
        logits = []
        for h, (pair, a, kvp, b) in enumerate(geom):
            q_al = _align(w[:, pair * LANES:(pair + 1) * LANES], a, b).astype(BF16).reshape(nb, BAND_BLOCK, LANES)
            logits.append(_bmm(q_al, kb[kvp], 2, 2) * scale + bias_ref[h][None] + edge)
        es, ssums, ms = [], [], []
        for h in range(heads):
            sk = sink_ref[h].reshape(1, 1, 1)
            m = jnp.maximum(jnp.max(logits[h], axis=2, keepdims=True), sk)
            e = jnp.exp(logits[h] - m)
            es.append(e.astype(BF16))
            ssums.append(jnp.sum(e, axis=2, keepdims=True) + jnp.exp(sk - m))
            ms.append(m)
        for h, (pair, a, kvp, b) in enumerate(geom):
            out = _bmm(es[h], vb[kvp], 2, 1) / ssums[h]
            o_acc[pair] = o_acc[pair] + _unalign(out.reshape(tl, LANES), a, b)
            lse = (ms[h] + jnp.log(ssums[h])).reshape(tl, 1)
            lse_acc[pair] = lse_acc[pair] + jnp.where(_lane_half() == a, lse, 0.0)
        o_ref[...] = jnp.concatenate(o_acc, axis=1)
        lse_ref[...] = jnp.concatenate(lse_acc, axis=1)

    main, prev, nxt, rows = _band_in_specs(tl, nb, n_chunks, n_blocks, col, False)
    return pl.pallas_call(
        body,
        grid=(1, n_chunks),
        in_specs=[main, prev, nxt, pl.BlockSpec(bias.shape, lambda j, i: (0, 0, 0)),
                  pl.BlockSpec(sink.shape, lambda j, i: (0, 0, 0))],
        out_specs=[rows, rows],
        out_shape=[jax.ShapeDtypeStruct((s_tok, QW), F32)] * 2,
        name=name + "_fwd",
        compiler_params=_params("parallel", "parallel"),
    )(proj, proj, proj, bias, sink)


def _band_bwd_call(proj, o, do, lse, dlse, bias, sink, dproj, col, dil, group, kvw, scale, name):
    seq, tl, nb, n_chunks, period = _band_geometry(proj, dil)
    lead = 0 if dproj is None else 1
    n_blocks = seq // BAND_BLOCK
    heads = bias.shape[0]
    b_ = BAND_BLOCK
    have_dlse = dlse is not None

    def body(*refs):
        (w_ref, pw_ref, nw_ref, o_ref, do_ref, lse_ref), refs = refs[lead:lead + 6], refs[lead + 6:]
        if have_dlse:
            dlse_ref, refs = refs[0], refs[1:]
        bias_ref, sink_ref, dwin_ref, dbias_ref, dsink_ref, dq_s, dk_s, dv_s = refs
        j, i = pl.program_id(0), pl.program_id(1)

        @pl.when((j == 0) & (i == 0))
        def _():
            dbias_ref[...] = jnp.zeros_like(dbias_ref)
            dsink_ref[...] = jnp.zeros_like(dsink_ref)

        @pl.when(i == 0)
        def _():
            dk_s[...] = jnp.zeros_like(dk_s)
            dv_s[...] = jnp.zeros_like(dv_s)

        @pl.when(i < n_chunks)
        def _():
            w, pw, nw = w_ref[...].astype(F32), pw_ref[...].astype(F32), nw_ref[...].astype(F32)
            kb = _bands(w, pw, nw, QW, kvw, nb)
            vb = _bands(w, pw, nw, QW + kvw, kvw, nb)
            edge = _edge_mask(i * nb, nb, period)
            dq_acc = [jnp.zeros((tl, LANES), F32) for _ in range(heads // 2)]
            for h in range(heads):
                pair, a, kvp, b = _head_geometry(h, group)
                lanes = slice(pair * LANES, (pair + 1) * LANES)
                mine = _lane_half() == a
                q_al = _align(w[:, lanes], a, b).astype(BF16).reshape(nb, b_, LANES)
                do_al = _align(do_ref[:, lanes], a, b).astype(BF16).reshape(nb, b_, LANES)
                lse_h = jnp.max(jnp.where(mine, lse_ref[:, lanes], NEG_INF), axis=1, keepdims=True)
                shift = -jnp.sum(jnp.where(mine, do_ref[:, lanes] * o_ref[:, lanes], 0.0), axis=1, keepdims=True)
                if have_dlse:
                    shift = shift + jnp.sum(jnp.where(mine, dlse_ref[:, lanes], 0.0), axis=1, keepdims=True)
                logits = _bmm(q_al, kb[kvp], 2, 2) * scale + bias_ref[h][None] + edge
                p = jnp.exp(logits - lse_h.reshape(nb, b_, 1))
                dlogits = p * (_bmm(do_al, vb[kvp], 2, 2) + shift.reshape(nb, b_, 1))
                dbias_ref[h] += jnp.sum(dlogits, axis=0)
                dsink_ref[h] += jnp.sum(jnp.exp(sink_ref[h] - lse_h) * shift, axis=0, keepdims=True)
                ds = (dlogits * scale).astype(BF16)
                dq_acc[pair] = dq_acc[pair] + _unalign(_bmm(ds, kb[kvp], 2, 1).reshape(tl, LANES), a, b)
                dk_band = _bmm(ds, q_al, 1, 1)
                dv_band = _bmm(p.astype(BF16), do_al, 1, 1)
                kv_lanes = slice(kvp * LANES, (kvp + 1) * LANES)
                for t in range(3):
                    at = pl.ds(pl.multiple_of(i * tl + t * b_, b_), tl)
                    dk_s[at, kv_lanes] += dk_band[:, t * b_:(t + 1) * b_, :].reshape(tl, LANES)
                    dv_s[at, kv_lanes] += dv_band[:, t * b_:(t + 1) * b_, :].reshape(tl, LANES)
            dq_s[lax.rem(i, 2)] = jnp.concatenate(dq_acc, axis=1)

        @pl.when(i >= 1)
        def _():
            at = pl.ds(pl.multiple_of((i - 1) * tl + b_, b_), tl)
            parts = [dq_s[lax.rem(i + 1, 2)], dk_s[at, :], dv_s[at, :]]
            if QW + 2 * kvw < BAND_W:
                parts.append(jnp.zeros((tl, BAND_W - QW - 2 * kvw), F32))
            dwin_ref[...] = jnp.concatenate(parts, axis=1).astype(dwin_ref.dtype)

    main, prev, nxt, rows = _band_in_specs(tl, nb, n_chunks, n_blocks, col, True)
    row_args = [o, do, lse] + ([dlse] if have_dlse else [])
    small = [pl.BlockSpec(bias.shape, lambda j, i: (0, 0, 0)), pl.BlockSpec(sink.shape, lambda j, i: (0, 0, 0))]
    return pl.pallas_call(
        body,
        grid=(1, n_chunks + 1),
        in_specs=[pl.BlockSpec(memory_space=pl.ANY)] * lead + [main, prev, nxt] + [rows] * len(row_args) + small,
        out_specs=[pl.BlockSpec((tl, BAND_W), lambda j, i: (j * n_chunks + jnp.maximum(i - 1, 0), col))] + small,
        out_shape=[jax.ShapeDtypeStruct(proj.shape, BF16), jax.ShapeDtypeStruct(bias.shape, F32),
                   jax.ShapeDtypeStruct(sink.shape, F32)],
        scratch_shapes=[pltpu.VMEM((2, tl, QW), F32), pltpu.VMEM((seq + 2 * b_, kvw), F32),
                        pltpu.VMEM((seq + 2 * b_, kvw), F32)],
        input_output_aliases={0: 0} if lead else {},
        name=name + "_bwd",
        compiler_params=_params("arbitrary", "arbitrary"),
    )(*([dproj] if lead else []), proj, proj, proj, *row_args, bias, sink)


def _loss_call(x, target, g):
    s, d = x.shape
    tr = _pick(s, (256, 128, 64, 32, 16, 8))

    def tile_loss(xt, gt, tt):
        err = jnp.square(_rms(xt, gt) - tt)
        return 0.5 * jnp.sum(jnp.mean(err, axis=-1, keepdims=True), axis=0, keepdims=True)

    def body(x_ref, t_ref, g_ref, loss_ref, dx_ref, dg_ref):
        tt = t_ref[...]
        val, vjp = jax.vjp(lambda xt, gt: tile_loss(xt, gt, tt), x_ref[...], g_ref[...])
        dx, dg = vjp(jnp.ones_like(val))
        dx_ref[...] = dx

        @pl.when(pl.program_id(0) == 0)
        def _():
            loss_ref[...] = jnp.zeros_like(loss_ref)
            dg_ref[...] = jnp.zeros_like(dg_ref)

        loss_ref[...] += val
        dg_ref[...] += dg

    return pl.pallas_call(
        body,
        grid=(s // tr,),
        in_specs=[_rows(tr, d), _rows(tr, d), _whole((1, d))],
        out_specs=[_whole((1, 1)), _rows(tr, d), _whole((1, d))],
        out_shape=[jax.ShapeDtypeStruct((1, 1), F32), jax.ShapeDtypeStruct((s, d), F32),
                   jax.ShapeDtypeStruct((1, d), F32)],
        name="final_norm_loss",
        compiler_params=_params("arbitrary"),
    )(x, target, g)


@jax.custom_vjp
def _loss_op(x, target, g):
    return _loss_call(x, target, g)[0][0, 0]


def _loss_op_fwd(x, target, g):
    loss, dx, dg = _loss_call(x, target, g)
    return loss[0, 0], (dx, dg, target)


def _loss_op_bwd(res, ct):
    dx, dg, target = res
    return ct * dx, jnp.zeros_like(target), ct * dg


_loss_op.defvjp(_loss_op_fwd, _loss_op_bwd)


def _expand_cols(x, e):
    hi = x.astype(BF16)
    rest = x - hi.astype(F32)
    mid = rest.astype(BF16)
    low = (rest - mid.astype(F32)).astype(BF16)
    return _dot(hi, e, 1, 0) + _dot(mid, e, 1, 0) + _dot(low, e, 1, 0)


def _mla_tile(r, p, a):
    g_q, g_kv, w_q, w_k, w_v = p
    cs, spread, place_kr = a
    lane = lax.broadcasted_iota(jnp.int32, (1, MLA_HEADS * MLA_QK), 1)
    cos_t = jnp.where(lane % MLA_QK < MLA_NOPE, 1.0, 0.0) + _expand_cols(cs, spread[0])
    sin_t = _expand_cols(cs, spread[1])
    a_q, a_kv, a_kr = _lanes(r[0], (0, MLA_Q_LORA, MLA_Q_LORA + MLA_KV_LORA, MLA_Q_LORA + MLA_KV_LORA + MLA_ROPE))
    q = _rope(_bdot(_rms(a_q, g_q), w_q), cos_t, sin_t, MLA_ROPE // 2)
    ckv = _rms(a_kv, g_kv)
    k = _rope(_bdot(ckv, w_k) + _hdot(a_kr, place_kr), cos_t, sin_t, MLA_ROPE // 2)
    return _split_heads(q, MLA_HEADS), _split_heads(k, MLA_HEADS), _split_heads(_bdot(ckv, w_v), MLA_HEADS)


def _head_rms(x, g, head_mean):
    return x * lax.rsqrt(_hdot(x * x, head_mean) + EPS) * g


def _gqa_tile(r, p, a):
    g_q, g_k = p
    cs, spread, mean_q, mean_k = a
    cos_t, sin_t = _expand_cols(cs, spread[0]), _expand_cols(cs, spread[1])
    wq, wk = GQA_HEADS * HEAD_DIM, GQA_KV_HEADS * HEAD_DIM
    b_q, b_k, b_v = _lanes(r[0], (0, wq, wq + wk, wq + 2 * wk))
    q = _rope(_head_rms(b_q, g_q, mean_q), cos_t, sin_t, HEAD_DIM // 4)
    k = _rope(_head_rms(b_k, g_k, mean_k), cos_t[:, :wk], sin_t[:, :wk], HEAD_DIM // 4)
    return _split_heads(q, GQA_HEADS), _split_heads(k, GQA_KV_HEADS), _split_heads(b_v, GQA_KV_HEADS)


def _permute_rows(p, x, cp):
    pb = p.astype(BF16)
    hi = x.astype(BF16)
    rest = x - hi.astype(F32)
    mid = rest.astype(BF16)
    low = (rest - mid.astype(F32)).astype(BF16)
    dims = (((cp,), (0,)), ((), ()))
    return (lax.dot_general(pb, hi, dims, preferred_element_type=F32)
            + lax.dot_general(pb, mid, dims, preferred_element_type=F32)
            + lax.dot_general(pb, low, dims, preferred_element_type=F32))


@jax.custom_vjp
def _permuted(p, x):
    return _permute_rows(p, x, 1)


def _permuted_fwd(p, x):
    return _permute_rows(p, x, 1), p


def _permuted_bwd(p, ct):
    return jnp.zeros_like(p), _permute_rows(p, ct, 0)


_permuted.defvjp(_permuted_fwd, _permuted_bwd)


def _interleave(p, x):
    return _permuted(p, x.reshape(x.shape[0] * x.shape[1], x.shape[2]))


def _interleave_matrix(rows, dil):
    p = np.zeros((rows, rows), np.float32)
    for t in range(rows):
        p[t, (t % dil) * (rows // dil) + t // dil] = 1.0
    return p


def _merge_tile(r, p, a):
    gm, o_a, o_b, oc0, oc1, oc2, l0, l1, l2, o_d = r
    (w_branch,) = p
    perm1, perm2 = a
    oc1, l1, oc2, l2 = _interleave(perm1, oc1), _interleave(perm1, l1), _interleave(perm2, oc2), _interleave(perm2, l2)
    d = w_branch.shape[2]
    gate_path, merge_logits = _lanes(gm, (0, N_BRANCH * BRANCH_W, N_BRANCH * BRANCH_W + N_BRANCH * d))
    m = jnp.maximum(jnp.maximum(l0, l1), l2)
    e0, e1, e2 = jnp.exp(l0 - m), jnp.exp(l1 - m), jnp.exp(l2 - m)
    y_c = (e0 * oc0 + e1 * oc1 + e2 * oc2) / (e0 + e1 + e2)
    y = jnp.concatenate([_join_heads(o_a), _join_heads(o_b), y_c, o_d], axis=1)
    u = y * (gate_path * jax.nn.sigmoid(gate_path))
    gates = _lanes(merge_logits, tuple(range(0, N_BRANCH * d + 1, d)))
    us = _lanes(u, tuple(range(0, N_BRANCH * BRANCH_W + 1, BRANCH_W)))
    branch_w = _unstack(w_branch)
    out = None
    for nb in range(N_BRANCH):
        term = jax.nn.sigmoid(gates[nb]) * _bdot(us[nb], branch_w[nb])
        out = term if out is None else out + term
    return (out,)


def _mixer_calls(proj, prm, aux):
    s = proj.shape[0]
    tr, tm = _pick(s, (512, 256, 128)), _pick(s, (256,))
    mla_cs, gqa_cs, mla_spread, gqa_spread, place_kr, mean_q, mean_k = aux[:7]
    wq = MLA_HEADS * MLA_QK
    mla = dict(
        steps=s // tr, rows=[(proj, _rows(tr, SMALL_W, MLA_BLK))],
        params=[prm["g_q"], prm["g_kv"], prm["w_q"], prm["w_k"], prm["w_v"]],
        aux=[(mla_cs, _rows(tr, mla_cs.shape[1])), (mla_spread, _whole(mla_spread.shape)),
             (place_kr, _whole(place_kr.shape))],
        outs=[((MLA_HEADS, s, MLA_QK), _head_rows(MLA_HEADS, tr, MLA_QK), BF16)] * 2
        + [((MLA_HEADS, s, MLA_V), _head_rows(MLA_HEADS, tr, MLA_V), BF16)],
        window=((s, P_TOT), _rows(tr, SMALL_W, MLA_BLK), BF16))
    wg = GQA_HEADS * HEAD_DIM
    gqa = dict(
        steps=s // tr, rows=[(proj, _rows(tr, SMALL_W, GQA_BLK))], params=[prm["gq"], prm["gk"]],
        aux=[(gqa_cs, _rows(tr, gqa_cs.shape[1])), (gqa_spread, _whole(gqa_spread.shape)), (mean_q, _whole(mean_q.shape)),
             (mean_k, _whole(mean_k.shape))],
        outs=[((GQA_HEADS, s, HEAD_DIM), _head_rows(GQA_HEADS, tr, HEAD_DIM), BF16)]
        + [((GQA_KV_HEADS, s, HEAD_DIM), _head_rows(GQA_KV_HEADS, tr, HEAD_DIM), BF16)] * 2,
        window=((s, P_TOT), _rows(tr, SMALL_W, GQA_BLK), BF16))
    merge = dict(steps=s // tm, tm=tm, window=((s, P_TOT), _rows(tm, GM_W, 0), BF16))
    return mla, gqa, merge


def _merge_rows(proj, o_a, o_b, ocs, lses, o_d, tm):
    h4 = _head_rows(4, tm, HEAD_DIM)
    s = proj.shape[0]

    def by_residue(t, dil):
        if dil == 1:
            return t, _rows(tm, QW)
        return t.reshape(dil, s // dil, QW), pl.BlockSpec((dil, tm // dil, QW), lambda i: (0, i, 0))

    dils = [dil for _, dil in DIL_PATTERNS]
    return ([(proj, _rows(tm, GM_W, 0)), (o_a, h4), (o_b, h4)] + [by_residue(t, r) for t, r in zip(ocs, dils)]
            + [by_residue(t, r) for t, r in zip(lses, dils)] + [(o_d, _rows(tm, QW))])


def _merge_aux(aux):
    return [(t, _whole(t.shape)) for t in aux[7:9]]


def _to_residues(t, dil):
    s, w = t.shape
    return t if dil == 1 else t.reshape(s // dil, dil, w).transpose(1, 0, 2).reshape(s, w)


def _from_residues(t, dil):
    s, w = t.shape
    return t if dil == 1 else t.reshape(dil, s // dil, w).transpose(1, 0, 2).reshape(s, w)


def _mixer_fwd(projs, prm, aux):
    proj = projs[0]
    s = proj.shape[0]
    mla, gqa, merge = _mixer_calls(proj, prm, aux)
    q_a, k_a, v_a = _fwd_call("prep_mla", _mla_tile, mla["steps"], mla["rows"], mla["params"], mla["aux"], mla["outs"])
    o_a, lse_a = _dense_fwd_call(q_a, k_a, v_a, MLA_QK ** -0.5, "mla")
    q_b, k_b, v_b = _fwd_call("prep_gqa", _gqa_tile, gqa["steps"], gqa["rows"], gqa["params"], gqa["aux"], gqa["outs"])
    grp = GQA_HEADS // GQA_KV_HEADS
    o_b, lse_b = _dense_fwd_call(q_b.reshape(GQA_KV_HEADS, grp * s, HEAD_DIM), k_b, v_b, HEAD_DIM ** -0.5, "gqa")
    scale = HEAD_DIM ** -0.5
    ocs, lses = [], []
    for gi, (_, dil) in enumerate(DIL_PATTERNS):
        o, lse = _band_fwd_call(projs[gi], DIL_BLK if gi == 0 else 0, prm["bias_dil"][gi], prm["no_sink"], dil, 1,
                                QW, scale, "dil%d" % gi)
        ocs.append(o)
        lses.append(lse)
    o_d, lse_d = _band_fwd_call(proj, WIN_BLK, prm["bias_win"], prm["sink"], 1, WIN_HEADS // WIN_KV_HEADS,
                                WIN_KV_HEADS * HEAD_DIM, scale, "win")
    rows = _merge_rows(proj, o_a, o_b.reshape(GQA_HEADS, s, HEAD_DIM), ocs, lses, o_d, merge["tm"])
    mix = _fwd_call("merge", _merge_tile, merge["steps"], rows, [prm["w_branch"]], _merge_aux(aux),
                    [((s, prm["w_branch"].shape[2]), _rows(merge["tm"], prm["w_branch"].shape[2]), BF16)])[0]
    return mix, (q_a, k_a, v_a, o_a, lse_a, q_b, k_b, v_b, o_b, lse_b, ocs, lses, o_d, lse_d)


def _mixer_bwd(projs, prm, aux, saved, dmix):
    proj = projs[0]
    s = proj.shape[0]
    q_a, k_a, v_a, o_a, lse_a, q_b, k_b, v_b, o_b, lse_b, ocs, lses, o_d, lse_d = saved
    dils = [dil for _, dil in DIL_PATTERNS]
    mla, gqa, merge = _mixer_calls(proj, prm, aux)
    tm, d_model = merge["tm"], prm["w_branch"].shape[2]
    grp = GQA_HEADS // GQA_KV_HEADS
    scale = HEAD_DIM ** -0.5

    rows = _merge_rows(proj, o_a, o_b.reshape(GQA_HEADS, s, HEAD_DIM), ocs, lses, o_d, tm)
    grads, (dw_branch,) = _vjp_call(
        "merge", _merge_tile, merge["steps"], rows, [prm["w_branch"]], _merge_aux(aux), [(dmix, _rows(tm, d_model))],
        [merge["window"]] + [(a.shape, spec) for a, spec in rows[1:]])
    dproj, do_a, do_b, docs, dlses, do_d = grads[0], grads[1], grads[2], grads[3:6], grads[6:9], grads[9]

    dq_a, dk_a, dv_a = _dense_bwd_call(q_a, k_a, v_a, o_a, lse_a, do_a, MLA_QK ** -0.5, "mla")
    (dproj,), dmla = _vjp_call("prep_mla", _mla_tile, mla["steps"], mla["rows"], mla["params"], mla["aux"],
                               [(t, e[1]) for t, e in zip((dq_a, dk_a, dv_a), mla["outs"])],
                               [mla["window"]], into=dproj)
    dq_b, dk_b, dv_b = _dense_bwd_call(q_b.reshape(GQA_KV_HEADS, grp * s, HEAD_DIM), k_b, v_b, o_b, lse_b,
                                       do_b.reshape(GQA_KV_HEADS, grp * s, HEAD_DIM), scale, "gqa")
    (dproj,), dgqa = _vjp_call("prep_gqa", _gqa_tile, gqa["steps"], gqa["rows"], gqa["params"], gqa["aux"],
                               [(t, e[1]) for t, e in zip((dq_b.reshape(GQA_HEADS, s, HEAD_DIM), dk_b, dv_b), gqa["outs"])],
                               [gqa["window"]], into=dproj)
    dproj, dbias_win, dsink = _band_bwd_call(proj, o_d, do_d, lse_d, None, prm["bias_win"], prm["sink"], dproj,
                                             WIN_BLK, 1, WIN_HEADS // WIN_KV_HEADS, WIN_KV_HEADS * HEAD_DIM, scale, "win")
    dbias_dil, dprojs = [], []
    for gi, dil in enumerate(dils):
        dside, dbias, _ = _band_bwd_call(
            projs[gi], ocs[gi], docs[gi].reshape(s, QW), lses[gi], dlses[gi].reshape(s, QW),
            prm["bias_dil"][gi], prm["no_sink"], dproj if gi == 0 else None, DIL_BLK if gi == 0 else 0, dil, 1, QW,
            scale, "dil%d" % gi)
        if gi == 0:
            dproj = dside
        else:
            dprojs.append(dside)
        dbias_dil.append(dbias)
    dprm = dict(g_q=dmla[0], g_kv=dmla[1], w_q=dmla[2], w_k=dmla[3], w_v=dmla[4], gq=dgqa[0], gk=dgqa[1],
                bias_dil=dbias_dil, bias_win=dbias_win, sink=dsink, no_sink=jnp.zeros_like(prm["no_sink"]),
                w_branch=dw_branch)
    return [dproj] + dprojs, {k: jax.tree.map(lambda g, p: g.astype(p.dtype), v, prm[k]) for k, v in dprm.items()}


def _layer_fwd(x, w, aux):
    s, d = x.shape
    tr = _pick(s, (256,))
    dils = [dil for _, dil in DIL_PATTERNS]

    def norm_forms(r, p, a):
        y = _rms(r[0], p[0])
        return [y, y.T] + [_dot(q, y, 1, 0).reshape(dil, tr // dil, d) for q, dil in zip(a, dils[1:])]

    forms = _fwd_call(
        "norm", norm_forms, s // tr, [(x, _rows(tr, d))], [w["norm_g"]], [(q, _whole(q.shape)) for q in aux[9:11]],
        [((s, d), _rows(tr, d), BF16), ((d, s), pl.BlockSpec((d, tr), lambda i: (0, i)), BF16)]
        + [((dil, s // dil, d), pl.BlockSpec((dil, tr // dil, d), lambda i: (0, i, 0)), BF16) for dil in dils[1:]])
    xn_t, xns = forms[1], [forms[0]] + [t.reshape(s, d) for t in forms[2:]]
    projs = [_mm(a, b, "nt", "proj%d_fwd" % i, BF16) for i, (a, b) in enumerate(zip(xns, w["w_in_t"]))]
    mix, saved = _mixer_fwd(projs, w["mixer"], aux)
    return _mm(mix, w["w_out"], "nn", "out_proj_nn"), (x, w, aux, xns, xn_t, projs, mix, saved)


@jax.custom_vjp
def _layer_core(x, w, aux):
    return _layer_fwd(x, w, aux)[0]


def _layer_core_bwd(res, dout):
    x, w, aux, xns, xn_t, projs, mix, saved = res
    s, d = x.shape
    tr = _pick(s, (256, 128, 64, 32, 16, 8))
    dils = [dil for _, dil in DIL_PATTERNS]
    dmix = _mm(dout, w["w_out"], "nt", "out_proj_nt")
    dw_out = _mm(mix.T, dout, "nn", "out_proj_dw", w["w_out"].dtype)
    dprojs, dmixer = _mixer_bwd(projs, w["mixer"], aux, saved, dmix)
    side = jnp.concatenate([_from_residues(dp, r) for dp, r in zip(dprojs[1:], dils[1:])], axis=1)
    dxn_terms = [_mm(dprojs[0], w["w_in_t"][0], "nn", "proj0_dx"),
                 _mm(side, jnp.concatenate(w["w_in_t"][1:], axis=0), "nn", "proj_side_dx")]
    dw_in_t = [_mm(a_t, dp, "nn", "proj%d_dw" % i, wi.dtype).T
               for i, (a_t, dp, wi) in enumerate(zip([xn_t] + [a.T for a in xns[1:]], dprojs, w["w_in_t"]))]
    (dx,), (dg,) = _vjp_call("norm", lambda r, p, a: _norm_tile(r, p, a) * len(dxn_terms), s // tr, [(x, _rows(tr, d))],
                             [w["norm_g"]], [], [(t, _rows(tr, d)) for t in dxn_terms], [((s, d), _rows(tr, d))])
    dw = dict(norm_g=dg, w_in_t=dw_in_t, mixer=dmixer, w_out=dw_out)
    return dx, dw, tuple(jnp.zeros_like(t) for t in aux)


_layer_core.defvjp(lambda x, w, aux: _layer_fwd(x, w, aux), _layer_core_bwd)


def _rope_angles(pos, dim):
    inv = ROPE_THETA ** (-jnp.arange(0, dim, 2, dtype=F32) / dim)
    return pos.astype(F32)[:, None] * inv[None, :]


def _rope_tables(s):
    pos = jnp.arange(s, dtype=jnp.int32)
    rows = s // GRID_W
    row_idx = jnp.repeat(jnp.arange(rows, dtype=jnp.int32), GRID_W)
    col_idx = jnp.tile(jnp.arange(GRID_W, dtype=jnp.int32), rows)
    a1 = _rope_angles(pos, MLA_ROPE)
    ar = _rope_angles(row_idx, HEAD_DIM // 2)
    ac = _rope_angles(col_idx, HEAD_DIM // 2)
    h = MLA_ROPE // 2
    mla_cs = jnp.concatenate([jnp.cos(a1), jnp.sin(a1)], axis=1)
    gqa_cs = jnp.concatenate([jnp.cos(ar), jnp.cos(ac), jnp.sin(ar), jnp.sin(ac)], axis=1)
    mla_spread = np.zeros((2, 2 * h, MLA_HEADS * MLA_QK), np.float32)
    gqa_spread = np.zeros((2, 4 * h, GQA_HEADS * HEAD_DIM), np.float32)
    for i in range(h):
        for head in range(MLA_HEADS):
            lo = head * MLA_QK + MLA_NOPE + i
            mla_spread[0, i, [lo, lo + h]] = 1.0
            mla_spread[1, h + i, lo], mla_spread[1, h + i, lo + h] = -1.0, 1.0
        for head in range(GQA_HEADS):
            for axis in range(2):
                lo = head * HEAD_DIM + axis * 2 * h + i
                gqa_spread[0, axis * h + i, [lo, lo + h]] = 1.0
                gqa_spread[1, 2 * h + axis * h + i, lo], gqa_spread[1, 2 * h + axis * h + i, lo + h] = -1.0, 1.0
    return mla_cs, gqa_cs, jnp.asarray(mla_spread), jnp.asarray(gqa_spread)


def _t5_bucket(rel):
    nb = T5_BUCKETS // 2
    max_exact = nb // 2
    n = jnp.abs(rel)
    nf = jnp.maximum(n, 1).astype(F32)
    large = max_exact + (jnp.log(nf / max_exact) / math.log(T5_MAX_DIST / max_exact) * (nb - max_exact)).astype(jnp.int32)
    large = jnp.minimum(large, nb - 1)
    return jnp.where(rel > 0, nb, 0) + jnp.where(n < max_exact, n, large)


def _band_bias(table, stride, head_lo, heads, half_window):
    b = BAND_BLOCK
    offs = jnp.arange(3 * b)[None, :] - b - jnp.arange(b)[:, None]
    one_hot = (_t5_bucket(offs * stride)[..., None] == jnp.arange(T5_BUCKETS)).astype(F32)
    bias = jnp.dot(one_hot.reshape(b * 3 * b, T5_BUCKETS), table[:, head_lo:head_lo + heads],
                   precision=lax.Precision.HIGHEST)
    bias = bias.T.reshape(heads, b, 3 * b)
    return jnp.where((jnp.abs(offs) <= half_window)[None], bias, NEG_INF)


def _w_in_rows(d):
    mla, gqa, win, dil0 = MLA_BLK * SMALL_W, GQA_BLK * SMALL_W, WIN_BLK * BAND_W, DIL_BLK * BAND_W
    plan, at = [], 0
    for width, target, row in ((256, 0, mla), (128, 0, mla + 256), (32, 0, mla + 384),
                               (256, 0, gqa), (128, 0, gqa + 256), (128, 0, gqa + 384)):
        plan.append((at, width, target, row))
        at += width
    for part in range(3):
        for g in range(len(DIL_PATTERNS)):
            plan.append((at, QW, g, (dil0 if g == 0 else 0) + part * QW))
            at += QW
    for width, row in ((256, win), (128, win + 256), (128, win + 384), (N_BRANCH * BRANCH_W, 0),
                       (N_BRANCH * d, N_BRANCH * BRANCH_W)):
        plan.append((at, width, 0, row))
        at += width
    return plan


@jax.custom_vjp
def _w_in_layout(w_in_t):
    d = w_in_t.shape[1]
    outs = []
    for target, rows in enumerate((P_TOT, BAND_W, BAND_W)):
        parts, at = [], 0
        for start, width, _, row in sorted((p for p in _w_in_rows(d) if p[2] == target), key=lambda p: p[3]):
            if row > at:
                parts.append(jnp.zeros((row - at, d), w_in_t.dtype))
            parts.append(w_in_t[start:start + width])
            at = row + width
        if at < rows:
            parts.append(jnp.zeros((rows - at, d), w_in_t.dtype))
        outs.append(jnp.concatenate(parts, axis=0))
    return outs


def _w_in_layout_fwd(w_in_t):
    return _w_in_layout(w_in_t), None


def _w_in_layout_bwd(_, cts):
    d = cts[0].shape[1]
    return (jnp.concatenate([cts[target][row:row + width] for _, width, target, row in _w_in_rows(d)], axis=0),)


_w_in_layout.defvjp(_w_in_layout_fwd, _w_in_layout_bwd)


def _layer(x, w, l, aux, biases):
    w_kv = w["w_kv_t"][l].T.reshape(MLA_KV_LORA, MLA_HEADS, MLA_NOPE + MLA_V)
    w_k = jnp.concatenate([w_kv[:, :, :MLA_NOPE], jnp.zeros((MLA_KV_LORA, MLA_HEADS, MLA_ROPE), w_kv.dtype)], axis=2)
    dil_bias, win_bias = biases
    prm = dict(
        g_q=w["mla_q_norm_g"][l][None, :], g_kv=w["mla_kv_norm_g"][l][None, :], w_q=w["w_q_t"][l].T,
        w_k=w_k.reshape(MLA_KV_LORA, MLA_HEADS * MLA_QK),
        w_v=w_kv[:, :, MLA_NOPE:].reshape(MLA_KV_LORA, MLA_HEADS * MLA_V),
        gq=jnp.tile(w["gqa_q_norm_g"][l], GQA_HEADS)[None, :], gk=jnp.tile(w["gqa_k_norm_g"][l], GQA_KV_HEADS)[None, :],
        bias_dil=list(dil_bias), bias_win=win_bias, sink=w["win_sink"][l].reshape(WIN_HEADS, 1, 1),
        no_sink=jnp.full((DIL_HEADS, 1, 1), NEG_INF, F32), w_branch=jnp.transpose(w["w_branch_t"][l].reshape(-1, N_BRANCH, BRANCH_W), (1, 2, 0)))
    layer_w = dict(norm_g=w["norm_g"][l][None, :], w_in_t=_w_in_layout(w["w_in_t"][l]), mixer=prm, w_out=w["w_out"][l])
    return x + _layer_core(x, layer_w, aux)


def _local_loss(w, x, target):
    s, d_model = x.shape
    assert d_model == D_MODEL, "the projection's window layout is laid out for d_model 1024"
    place = np.zeros((MLA_ROPE, MLA_HEADS * MLA_QK), np.float32)
    for h in range(MLA_HEADS):
        for i in range(MLA_ROPE):
            place[i, h * MLA_QK + MLA_NOPE + i] = 1.0

    def head_mean(nh):
        m = np.kron(np.eye(nh, dtype=np.float32), np.full((HEAD_DIM, HEAD_DIM), 1.0 / HEAD_DIM, np.float32))
        return jnp.asarray(m)

    merge_tile = _pick(s, (256,))
    norm_tile = _pick(s, (256,))
    aux = _rope_tables(s) + (jnp.asarray(place), head_mean(GQA_HEADS), head_mean(GQA_KV_HEADS)) + tuple(
        jnp.asarray(_interleave_matrix(merge_tile, dil)) for _, dil in DIL_PATTERNS[1:]) + tuple(
        jnp.asarray(_interleave_matrix(norm_tile, dil).T) for _, dil in DIL_PATTERNS[1:])
    table = w["t5_table"]
    dil_bias = [_band_bias(table, dil, gi * DIL_HEADS, DIL_HEADS, window // (2 * dil))
                for gi, (window, dil) in enumerate(DIL_PATTERNS)]
    win_bias = _band_bias(table, 1, len(DIL_PATTERNS) * DIL_HEADS, WIN_HEADS, WIN_HALF)
    for l in range(w["norm_g"].shape[0]):
        x = _layer(x, w, l, aux, (dil_bias, win_bias))
    return _loss_op(x, target, w["final_norm_g"][None, :])


_ANY = pl.BlockSpec(memory_space=pl.ANY)
_MESH = pl.DeviceIdType.MESH


def _all_gather(block, name):
    def body(x_ref, out_ref, send_sems, recv_sems, local_sem):
        x, y, c = lax.axis_index("x"), lax.axis_index("y"), lax.axis_index("c")
        me, sibling = (x, y, c), (x, y, 1 - c)
        chips = [(1 - x, y), (x, 1 - y), (1 - x, 1 - y)]

        def slot(px, py, pc):
            return out_ref.at[4 * px + 2 * py + pc]

        def copy(k, blk, to, src=None):
            return pltpu.make_async_remote_copy(
                src_ref=slot(*blk) if src is None else src, dst_ref=slot(*blk),
                send_sem=send_sems.at[k], recv_sem=recv_sems.at[k], device_id=to, device_id_type=_MESH)

        mine = pltpu.make_async_copy(x_ref, slot(*me), local_sem)
        mine.start()
        first = [copy(0, me, sibling, src=x_ref)]
        first += [copy(1 + j, me, (*chip, c), src=x_ref) for j, chip in enumerate(chips)]
        for cp in first:
            cp.start()
        passed = [copy(4 + j, (*chip, c), sibling) for j, chip in enumerate(chips)]
        for j, chip in enumerate(chips):
            copy(1 + j, (*chip, c), me).wait_recv()
            passed[j].start()
        copy(0, sibling, me).wait_recv()
        for j, chip in enumerate(chips):
            copy(4 + j, (*chip, 1 - c), me).wait_recv()
        for cp in first + passed:
            cp.wait_send()
        mine.wait()

    return pl.pallas_call(
        body,
        out_shape=jax.ShapeDtypeStruct((N_DEV,) + block.shape, block.dtype),
        in_specs=[_ANY],
        out_specs=_ANY,
        scratch_shapes=[pltpu.SemaphoreType.DMA((7,)), pltpu.SemaphoreType.DMA((7,)), pltpu.SemaphoreType.DMA],
        name=name,
    )(block)


def _swap_with_sibling(blocks, name):
    chips = blocks.shape[0]

    def body(x_ref, out_ref, send_sems, recv_sems):
        x, y, c = lax.axis_index("x"), lax.axis_index("y"), lax.axis_index("c")
        copies = [pltpu.make_async_remote_copy(
            src_ref=x_ref.at[k, 1 - c], dst_ref=out_ref.at[k], send_sem=send_sems.at[k], recv_sem=recv_sems.at[k],
            device_id=(x, y, 1 - c), device_id_type=_MESH) for k in range(chips)]
        for cp in copies:
            cp.start()
        for cp in copies:
            cp.wait()

    return pl.pallas_call(
        body,
        out_shape=jax.ShapeDtypeStruct((chips,) + blocks.shape[2:], blocks.dtype),
        in_specs=[_ANY],
        out_specs=_ANY,
        scratch_shapes=[pltpu.SemaphoreType.DMA((chips,)), pltpu.SemaphoreType.DMA((chips,))],
        name=name,
    )(blocks)


def _add_sibling(blocks, theirs, name):
    chips, _, rows, w = blocks.shape
    tr = _row_tile(rows, 16, 4096)

    def body(b_ref, t_ref, o_ref):
        mine = b_ref[0, lax.axis_index("c")]
        o_ref[0] = (mine.astype(F32) + t_ref[0].astype(F32)).astype(o_ref.dtype)

    return pl.pallas_call(
        body,
        grid=(chips, rows // tr),
        in_specs=[pl.BlockSpec((1, 2, tr, w), lambda k, i: (k, 0, i, 0)), pl.BlockSpec((1, tr, w), lambda k, i: (k, i, 0))],
        out_specs=pl.BlockSpec((1, tr, w), lambda k, i: (k, i, 0)),
        out_shape=jax.ShapeDtypeStruct(theirs.shape, theirs.dtype),
        name=name,
        compiler_params=_params("parallel", "parallel"),
    )(blocks, theirs)


def _exchange_chips(partials, name):
    n_chips = partials.shape[0]

    def body(x_ref, out_ref, send_sems, recv_sems, local_sem):
        x, y, c = lax.axis_index("x"), lax.axis_index("y"), lax.axis_index("c")
        me = 2 * x + y
        mine = pltpu.make_async_copy(x_ref.at[me], out_ref.at[me], local_sem)
        mine.start()
        copies, landed = [], []
        for k in range(1, n_chips):
            px = 1 - x if k & 2 else x
            py = 1 - y if k & 1 else y
            peer = 2 * px + py
            copies.append(pltpu.make_async_remote_copy(
                src_ref=x_ref.at[peer], dst_ref=out_ref.at[me], send_sem=send_sems.at[k - 1],
                recv_sem=recv_sems.at[k - 1], device_id=(px, py, c), device_id_type=_MESH))
            landed.append(pltpu.make_async_remote_copy(
                src_ref=x_ref.at[peer], dst_ref=out_ref.at[peer], send_sem=send_sems.at[k - 1],
                recv_sem=recv_sems.at[k - 1], device_id=(px, py, c), device_id_type=_MESH))
        for cp in copies:
            cp.start()
        for cp in landed:
            cp.wait_recv()
        for cp in copies:
            cp.wait_send()
        mine.wait()

    return pl.pallas_call(
        body,
        out_shape=jax.ShapeDtypeStruct(partials.shape, partials.dtype),
        in_specs=[_ANY],
        out_specs=_ANY,
        scratch_shapes=[pltpu.SemaphoreType.DMA((n_chips - 1,)), pltpu.SemaphoreType.DMA((n_chips - 1,)),
                        pltpu.SemaphoreType.DMA],
        name=name,
    )(partials)


def _sum_slots(parts, name):
    slots, rows, w = parts.shape
    tr = _row_tile(rows, 16 if parts.dtype == BF16 else 8, 4096)

    def body(p_ref, o_ref):
        acc = p_ref[0].astype(F32)
        for j in range(1, slots):
            acc = acc + p_ref[j].astype(F32)
        o_ref[...] = acc

    return pl.pallas_call(
        body,
        grid=(rows // tr,),
        in_specs=[pl.BlockSpec((slots, tr, w), lambda i: (0, i, 0))],
        out_specs=pl.BlockSpec((tr, w), lambda i: (i, 0)),
        out_shape=jax.ShapeDtypeStruct((rows, w), F32),
        name=name,
        compiler_params=_params("parallel"),
    )(parts)


def _adamw(w, g, m, v, name):
    rows, width = w.shape
    lanes = -(-width // LANES) * LANES
    tr = _row_tile(rows, 8, max(8, V7X_VMEM_LIMIT // 3 // (7 * 2 * 4 * lanes) // 8 * 8))

    def body(w_ref, g_ref, m_ref, v_ref, d_ref, nm_ref, nv_ref):
        g_ = g_ref[...]
        m_ = ADAM_B1 * m_ref[...] + (1.0 - ADAM_B1) * g_
        v_ = ADAM_B2 * v_ref[...] + (1.0 - ADAM_B2) * jnp.square(g_)
        m_hat = m_ / (1.0 - ADAM_B1 ** ADAM_STEP)
        v_hat = v_ / (1.0 - ADAM_B2 ** ADAM_STEP)
        d_ref[...] = -ADAM_LR * (m_hat / (jnp.sqrt(v_hat) + ADAM_EPS) + ADAM_WD * w_ref[...])
        nm_ref[...] = m_
        nv_ref[...] = v_

    spec = pl.BlockSpec((tr, width), lambda i: (i, 0))
    return pl.pallas_call(
        body,
        grid=(rows // tr,),
        in_specs=[spec] * 4,
        out_specs=[spec] * 3,
        out_shape=[jax.ShapeDtypeStruct((rows, width), F32)] * 3,
        name=name,
        compiler_params=_params("parallel"),
    )(w, g, m, v)


_SHARDED = (("w_in", 2), ("w_mla_q_up", 2), ("w_mla_kv_up", 2), ("w_branch", 3), ("w_out", 1))
_REPLICATED = ("norm_g", "mla_q_norm_g", "mla_kv_norm_g", "gqa_q_norm_g", "gqa_k_norm_g", "win_sink", "t5_table",
               "final_norm_g")


def _pack(arrays, row_multiple):
    flat = jnp.concatenate([a.reshape(-1) for a in arrays])
    rows = -(-flat.shape[0] // (LANES * row_multiple)) * row_multiple
    return jnp.pad(flat, (0, rows * LANES - flat.shape[0])).reshape(rows, LANES)


def _unpack(packed, shapes):
    flat, out, at = packed.reshape(-1), [], 0
    for shp in shapes:
        n = int(np.prod(shp))
        out.append(flat[at:at + n].reshape(shp))
        at += n
    return out


_TO_WIRE = {
    "w_in": lambda t: jnp.swapaxes(t, 1, 2), "w_mla_q_up": lambda t: jnp.swapaxes(t, 1, 2),
    "w_mla_kv_up": lambda t: jnp.swapaxes(t, 1, 2),
    "w_branch": lambda t: jnp.transpose(t, (0, 3, 1, 2)).reshape(t.shape[0], t.shape[3], -1), "w_out": lambda t: t}
_FROM_WIRE = {
    "w_in": lambda t, shp: jnp.swapaxes(t, 1, 2), "w_mla_q_up": lambda t, shp: jnp.swapaxes(t, 1, 2),
    "w_mla_kv_up": lambda t, shp: jnp.swapaxes(t, 1, 2),
    "w_branch": lambda t, shp: jnp.transpose(t.reshape(shp[0], shp[3], shp[1], shp[2]), (0, 2, 3, 1)),
    "w_out": lambda t, shp: t}
_WIRE_NAME = {"w_in": "w_in_t", "w_mla_q_up": "w_q_t", "w_mla_kv_up": "w_kv_t", "w_branch": "w_branch_t",
              "w_out": "w_out"}


def _transpose_blocks(t, dtype, name):
    depth, a, b = t.shape

    def body(x_ref, o_ref):
        o_ref[0] = x_ref[0].T.astype(o_ref.dtype)

    return pl.pallas_call(
        body,
        grid=(depth,),
        in_specs=[pl.BlockSpec((1, a, b), lambda i: (i, 0, 0))],
        out_specs=pl.BlockSpec((1, b, a), lambda i: (i, 0, 0)),
        out_shape=jax.ShapeDtypeStruct((depth, b, a), dtype),
        name=name,
        compiler_params=_params("parallel"),
    )(t)


def _join_shards(gathered, wire_shapes):
    out, at = [], 0
    for depth, cut, rest in wire_shapes:
        n = depth * cut * rest // LANES
        blk = gathered[:, at:at + n].reshape(N_DEV, depth, cut, rest)
        out.append(jnp.moveaxis(blk, 0, 1).reshape(depth, N_DEV * cut, rest))
        at += n
    return out


def _split_shards(fulls, wire_shapes):
    parts = []
    for full, (depth, cut, rest) in zip(fulls, wire_shapes):
        blk = jnp.moveaxis(full.reshape(depth, N_DEV, cut, rest), 1, 0)
        parts.append(blk.reshape(N_DEV, depth * cut * rest // LANES, LANES))
    packed = jnp.concatenate(parts, axis=1)
    return packed.reshape((N_DEV // 2, 2) + packed.shape[1:])


def kernel(x, norm_g, w_in, mla_q_norm_g, mla_kv_norm_g, w_mla_q_up, w_mla_kv_up, gqa_q_norm_g, gqa_k_norm_g, win_sink, t5_table, w_branch, w_out, final_norm_g, loss_target, m_norm_g, m_w_in, m_mla_q_norm_g, m_mla_kv_norm_g, m_w_mla_q_up, m_w_mla_kv_up, m_gqa_q_norm_g, m_gqa_k_norm_g, m_win_sink, m_t5_table, m_w_branch, m_w_out, m_final_norm_g, v_norm_g, v_w_in, v_mla_q_norm_g, v_mla_kv_norm_g, v_w_mla_q_up, v_w_mla_kv_up, v_gqa_q_norm_g, v_gqa_k_norm_g, v_win_sink, v_t5_table, v_w_branch, v_w_out, v_final_norm_g):
    given = dict(locals())
    names = ("norm_g", "w_in", "mla_q_norm_g", "mla_kv_norm_g", "w_mla_q_up", "w_mla_kv_up", "gqa_q_norm_g",
             "gqa_k_norm_g", "win_sink", "t5_table", "w_branch", "w_out", "final_norm_g")
    shard_names = [n for n, _ in _SHARDED]
    shard_shapes = [given[n].shape for n in shard_names]

    wire = [_transpose_blocks(given[n], BF16, "w_in_to_wire") if n == "w_in" else _TO_WIRE[n](given[n]).astype(BF16)
            for n in shard_names]
    wire_shapes = [t.shape for t in wire]
    gathered = _all_gather(jnp.concatenate([t.reshape(-1, LANES) for t in wire]), "gather_weights")
    weights = {n: given[n] for n in _REPLICATED}
    weights.update(zip([_WIRE_NAME[n] for n in shard_names], _join_shards(gathered, wire_shapes)))

    loss, (gw, gx) = jax.value_and_grad(_local_loss, argnums=(0, 1))(weights, x[0], loss_target[0])
    loss = lax.psum(loss, ("x", "y", "c"))

    send = _split_shards([gw[_WIRE_NAME[n]] for n in shard_names], wire_shapes)
    partials = _add_sibling(send, _swap_with_sibling(send, "swap_grads"), "add_sibling_grads")
    g_wire = _unpack(_sum_slots(_exchange_chips(partials, "scatter_grads"), "sum_grads"), wire_shapes)
    g_shard = [_transpose_blocks(t, F32, "w_in_from_wire") if n == "w_in" else _FROM_WIRE[n](t, shp)
               for n, t, shp in zip(shard_names, g_wire, shard_shapes)]
    rep_shapes = [given[n].shape for n in _REPLICATED]
    g_rep = _unpack(_sum_slots(_all_gather(_pack([gw[n] for n in _REPLICATED], 8), "gather_small_grads"),
                               "sum_small_grads"), rep_shapes)
    grads = dict(zip(shard_names, g_shard))
    grads.update(zip(_REPLICATED, g_rep))

    def update(group, shapes, row_multiple, name):
        outs = _adamw(*[_pack([src[n] for n in group], row_multiple) for src in (
            given, grads, {n: given["m_" + n] for n in group}, {n: given["v_" + n] for n in group})], name)
        return [dict(zip(group, _unpack(o, shapes))) for o in outs]

    def update_shard(n):
        shp = given[n].shape
        outs = _adamw(*[t.reshape(-1, shp[-1]) for t in (given[n], grads[n], given["m_" + n], given["v_" + n])],
                      "adamw_" + n)
        return [o.reshape(shp) for o in outs]

    per_shard = [update_shard(n) for n in shard_names]
    big = [dict(zip(shard_names, [u[k] for u in per_shard])) for k in range(3)]
    small = update(list(_REPLICATED), rep_shapes, 8, "adamw_replicated")
    delta, new_m, new_v = [{**b, **s_} for b, s_ in zip(big, small)]
    return (loss, gx[None], *[grads[n] for n in names], *[delta[n] for n in names],
            *[new_m[n] for n in names], *[new_v[n] for n in names])
```

```python
import functools
import math

import jax
import jax.numpy as jnp
import numpy as np
from jax import lax
from jax.experimental import pallas as pl
from jax.experimental.pallas import tpu as pltpu

F32 = jnp.float32
BF16 = jnp.bfloat16
N_DEV = 8
LANES = 128
HALF = LANES // 2
V7X_VMEM_LIMIT = 56 * 1024 * 1024

EPS = 1e-6
NEG_INF = -1e30
LOG2E = 1.4426950408889634
ROPE_THETA = 10000.0
GRID_W = 64
HEAD_DIM = 64
N_BRANCH = 4
BRANCH_W = 256
MLA_HEADS, MLA_Q_LORA, MLA_KV_LORA, MLA_NOPE, MLA_ROPE, MLA_V = 4, 256, 128, 64, 32, 64
MLA_QK = MLA_NOPE + MLA_ROPE
GQA_HEADS, GQA_KV_HEADS = 4, 2
DIL_PATTERNS = ((128, 1), (512, 4), (2048, 16))
DIL_HEADS = 4
WIN_HEADS, WIN_KV_HEADS, WIN_HALF = 4, 2, 128
T5_BUCKETS, T5_MAX_DIST = 32, 1024
BAND_BLOCK = 128
ADAM_LR, ADAM_B1, ADAM_B2, ADAM_EPS, ADAM_WD, ADAM_STEP = 0.001, 0.9, 0.999, 1e-08, 0.01, 10

D_MODEL = 1024
GM_W, SMALL_W, BAND_W = 5120, 512, 768
MLA_BLK, GQA_BLK, WIN_BLK, DIL_BLK = 10, 11, 8, 9
P_TOT = 7680
QW = 256


def _params(*sem):
    return pltpu.CompilerParams(dimension_semantics=sem, vmem_limit_bytes=V7X_VMEM_LIMIT)


def _pick(n, cands):
    for c in cands:
        if n % c == 0:
            return c
    return n


def _row_tile(rows, unit, cap):
    best = unit
    for t in range(unit, min(rows, cap) + 1, unit):
        if rows % t == 0:
            best = t
    assert rows % best == 0
    return best


def _dot(a, b, ca, cb):
    return lax.dot_general(a.astype(BF16), b.astype(BF16), (((ca,), (cb,)), ((), ())), preferred_element_type=F32)


def _bmm(a, b, ca, cb):
    return lax.dot_general(a, b, (((ca,), (cb,)), ((0,), (0,))), preferred_element_type=F32)


@jax.custom_vjp
def _bdot(a, b):
    return _dot(a, b, 1, 0)


def _bdot_fwd(a, b):
    return _dot(a, b, 1, 0), (a, b)


def _bdot_bwd(res, g):
    a, b = res
    return _dot(g, b, 1, 1), _dot(a, g, 0, 0)


_bdot.defvjp(_bdot_fwd, _bdot_bwd)


def _hdot(a, c):
    return lax.dot_general(a, c, (((1,), (0,)), ((), ())), precision=lax.Precision.HIGHEST, preferred_element_type=F32)


@functools.partial(jax.custom_vjp, nondiff_argnums=(1,))
def _lane_roll(x, shift):
    return pltpu.roll(x, shift, 1)


def _lane_roll_fwd(x, shift):
    return pltpu.roll(x, shift, 1), None


def _lane_roll_bwd(shift, _, g):
    return (pltpu.roll(g, g.shape[1] - shift, 1),)


_lane_roll.defvjp(_lane_roll_fwd, _lane_roll_bwd)


@functools.partial(jax.custom_vjp, nondiff_argnums=(1,))
def _lane_ranges(x, cut):
    bounds, _ = cut
    return tuple(x[:, lo:hi] for lo, hi in zip(bounds[:-1], bounds[1:]))


def _lane_ranges_fwd(x, cut):
    return _lane_ranges(x, cut), None


def _lane_ranges_bwd(cut, _, cts):
    bounds, width = cut
    parts = list(cts)
    if bounds[-1] < width:
        parts.append(jnp.zeros((cts[0].shape[0], width - bounds[-1]), cts[0].dtype))
    return (jnp.concatenate(parts, axis=1),)


_lane_ranges.defvjp(_lane_ranges_fwd, _lane_ranges_bwd)


def _lanes(x, bounds):
    return _lane_ranges(x, (tuple(bounds), x.shape[1]))


@jax.custom_vjp
def _unstack(x):
    return tuple(x[i] for i in range(x.shape[0]))


def _unstack_fwd(x):
    return _unstack(x), None


def _unstack_bwd(_, cts):
    return (jnp.stack(cts, axis=0),)


_unstack.defvjp(_unstack_fwd, _unstack_bwd)


@functools.partial(jax.custom_vjp, nondiff_argnums=(1,))
def _split_heads(x, h):
    d = x.shape[1] // h
    return jnp.stack([x[:, i * d:(i + 1) * d] for i in range(h)], axis=0)


def _split_heads_fwd(x, h):
    return _split_heads(x, h), None


def _split_heads_bwd(h, _, ct):
    return (jnp.concatenate([ct[i] for i in range(h)], axis=1),)


_split_heads.defvjp(_split_heads_fwd, _split_heads_bwd)


def _join_heads(x):
    return jnp.concatenate(_unstack(x), axis=1)


def _rope(x, cos_t, sin_t, half):
    w = x.shape[1]
    lane = lax.broadcasted_iota(jnp.int32, (1, w), 1)
    first = (lane % (2 * half)) < half
    partner = jnp.where(first, _lane_roll(x, w - half), _lane_roll(x, half))
    return x * cos_t + partner * sin_t


def _rms(x, g):
    return x * lax.rsqrt(jnp.mean(x * x, axis=-1, keepdims=True) + EPS) * g


def _rows(tr, w, col=0):
    return pl.BlockSpec((tr, w), lambda i: (i, col))


def _head_rows(h, tr, d):
    return pl.BlockSpec((h, tr, d), lambda i: (0, i, 0))


def _whole(shape):
    nd = len(shape)
    return pl.BlockSpec(tuple(shape), lambda i: (0,) * nd)


def _fwd_call(name, fn, steps, rows, params, aux, outs):
    nr, npar, na = len(rows), len(params), len(aux)

    def body(*refs):
        vals = [x[...].astype(F32) for x in refs[:nr + npar + na]]
        res = fn(vals[:nr], vals[nr:nr + npar], vals[nr + npar:])
        for o_ref, o in zip(refs[nr + npar + na:], res):
            o_ref[...] = o.astype(o_ref.dtype)

    return pl.pallas_call(
        body,
        grid=(steps,),
        in_specs=[s for _, s in rows] + [_whole(p.shape) for p in params] + [s for _, s in aux],
        out_specs=[e[1] for e in outs],
        out_shape=[jax.ShapeDtypeStruct(e[0], e[2] if len(e) > 2 else F32) for e in outs],
        name=name + "_fwd",
        compiler_params=_params("parallel"),
    )(*[a for a, _ in rows], *params, *[a for a, _ in aux])


def _vjp_call(name, fn, steps, rows, params, aux, cts, row_grads, into=None):
    nr, npar, na, nc = len(rows), len(params), len(aux), len(cts)
    n_in = nr + npar + na + nc
    lead = 0 if into is None else 1

    def body(*refs):
        refs = refs[lead:]
        vals = [x[...].astype(F32) for x in refs[:n_in]]
        r, p, a, d = vals[:nr], vals[nr:nr + npar], vals[nr + npar:nr + npar + na], vals[nr + npar + na:]
        out_refs = refs[n_in:]
        _, vjp = jax.vjp(lambda r_, p_: tuple(fn(r_, p_, a)), r, p)
        dr, dp = vjp(tuple(d))
        for o_ref, o in zip(out_refs[:nr], dr):
            o_ref[...] = o.astype(o_ref.dtype)

        @pl.when(pl.program_id(0) == 0)
        def _():
            for o_ref in out_refs[nr:]:
                o_ref[...] = jnp.zeros_like(o_ref)

        for o_ref, o in zip(out_refs[nr:], dp):
            o_ref[...] += o

    outs = pl.pallas_call(
        body,
        grid=(steps,),
        in_specs=([] if into is None else [pl.BlockSpec(memory_space=pl.ANY)])
        + [s for _, s in rows] + [_whole(p.shape) for p in params] + [s for _, s in aux] + [s for _, s in cts],
        out_specs=[e[1] for e in row_grads] + [_whole(p.shape) for p in params],
        out_shape=[jax.ShapeDtypeStruct(e[0], e[2] if len(e) > 2 else F32) for e in row_grads]
        + [jax.ShapeDtypeStruct(p.shape, F32) for p in params],
        input_output_aliases={} if into is None else {0: 0},
        name=name + "_bwd",
        compiler_params=_params("arbitrary"),
    )(*([] if into is None else [into]), *[a for a, _ in rows], *params, *[a for a, _ in aux], *[a for a, _ in cts])
    return list(outs[:nr]), list(outs[nr:])


def _norm_tile(r, p, a):
    return (_rms(r[0], p[0]),)


def _mm(a, b, mode, name, out_dtype=F32):
    if mode == "nn":
        (m, k), n = a.shape, b.shape[1]
    elif mode == "nt":
        (m, k), n = a.shape, b.shape[0]
    else:
        (k, m), n = a.shape, b.shape[1]
    tn = _pick(n, (1024, 768, 512, 384, 256, 128))
    budget = V7X_VMEM_LIMIT * 3 // 4
    out_bytes = 4 + 2 * np.dtype(out_dtype).itemsize

    def tiles():
        for tm in (2048, 1024, 512, 256, 128):
            for tk in (512, 256, 128) if mode == "tn" else (4096, 1024, 768, 512, 384, 256, 128):
                need = 2 * tk * (tm * a.dtype.itemsize + tn * b.dtype.itemsize) + tm * tn * out_bytes
                if m % tm == 0 and k % tk == 0 and need <= budget:
                    return tm, tk
        return _pick(m, (128,)), _pick(k, (128,))

    tm, tk = tiles()
    nk = k // tk

    def body(*refs):
        a_ref, b_ref, o_ref, acc_ref = refs
        kk = pl.program_id(2)
        if mode == "nn":
            part = _dot(a_ref[...], b_ref[...], 1, 0)
        elif mode == "nt":
            part = _dot(a_ref[...], b_ref[...], 1, 1)
        else:
            part = _dot(a_ref[...], b_ref[...], 0, 0)
        if nk == 1:
            o_ref[...] = part.astype(o_ref.dtype)
        else:
            @pl.when(kk == 0)
            def _():
                acc_ref[...] = part

            @pl.when(kk > 0)
            def _():
                acc_ref[...] += part

            @pl.when(kk == nk - 1)
            def _():
                o_ref[...] = acc_ref[...].astype(o_ref.dtype)

    if mode == "nn":
        a_spec = pl.BlockSpec((tm, tk), lambda i, j, kk: (i, kk))
        b_spec = pl.BlockSpec((tk, tn), lambda i, j, kk: (kk, j))
    elif mode == "nt":
        a_spec = pl.BlockSpec((tm, tk), lambda i, j, kk: (i, kk))
        b_spec = pl.BlockSpec((tn, tk), lambda i, j, kk: (j, kk))
    else:
        a_spec = pl.BlockSpec((tk, tm), lambda i, j, kk: (kk, i))
        b_spec = pl.BlockSpec((tk, tn), lambda i, j, kk: (kk, j))
    o_spec = pl.BlockSpec((tm, tn), lambda i, j, kk: (i, j))
    return pl.pallas_call(
        body,
        grid=(m // tm, n // tn, nk),
        in_specs=[a_spec, b_spec],
        out_specs=o_spec,
        out_shape=jax.ShapeDtypeStruct((m, n), out_dtype),
        scratch_shapes=[pltpu.VMEM((tm, tn), F32)],
        name=name,
        compiler_params=_params("parallel", "parallel", "arbitrary"),
    )(a, b)


def _dense_fwd_call(q, k, v, scale, name):
    n, sq, d = q.shape
    sk, dv = k.shape[1], v.shape[2]
    tq = _pick(sq, (512, 256, 128))
    c = scale * LOG2E

    nkb = 1

    def body(q_ref, k_ref, v_ref, o_ref, lse_ref, m_s, acc_s, vext_s):
        j = pl.program_id(2)

        @pl.when(j == 0)
        def _():
            m_s[...] = jnp.full_like(m_s, NEG_INF)
            acc_s[...] = jnp.zeros_like(acc_s)
            vext_s[...] = jnp.ones_like(vext_s)

        vext_s[:, :dv] = v_ref[0].astype(BF16)
        m_old = m_s[...]
        s = _dot(q_ref[0], k_ref[0], 1, 1)
        m_new = jnp.maximum(m_old, jnp.max(s, axis=1, keepdims=True))
        p = jnp.exp2(s * c - m_new * c)
        acc = jnp.exp2((m_old - m_new) * c) * acc_s[...] + _dot(p, vext_s[...], 1, 0)
        m_s[...] = m_new
        acc_s[...] = acc

        @pl.when(j == nkb - 1)
        def _():
            l = acc[:, dv:dv + 1]
            o_ref[0] = acc[:, :dv] / l
            lse_ref[0] = m_new * scale + jnp.log(l)

    return pl.pallas_call(
        body,
        grid=(n, sq // tq, nkb),
        in_specs=[
            pl.BlockSpec((1, tq, d), lambda h, i, j: (h, i, 0)),
            pl.BlockSpec((1, sk // nkb, d), lambda h, i, j: (h, j, 0)),
            pl.BlockSpec((1, sk // nkb, dv), lambda h, i, j: (h, j, 0)),
        ],
        out_specs=[
            pl.BlockSpec((1, tq, dv), lambda h, i, j: (h, i, 0)),
            pl.BlockSpec((1, tq, 1), lambda h, i, j: (h, i, 0)),
        ],
        out_shape=[jax.ShapeDtypeStruct((n, sq, dv), F32), jax.ShapeDtypeStruct((n, sq, 1), F32)],
        scratch_shapes=[pltpu.VMEM((tq, 1), F32), pltpu.VMEM((tq, 2 * dv), F32), pltpu.VMEM((sk // nkb, 2 * dv), BF16)],
        name=name + "_fwd",
        compiler_params=_params("parallel", "parallel", "arbitrary"),
    )(q, k, v)


def _dense_bwd_call(q, k, v, o, lse, do, scale, name):
    n, sq, d = q.shape
    sk, dv = k.shape[1], v.shape[2]
    tq, tk = _pick(sq, (1024, 512, 256, 128)), _pick(sk, (2048, 1024, 512, 256, 128))
    c = scale * LOG2E

    def body(q_ref, k_ref, v_ref, o_ref, lse_ref, do_ref, dq_ref, dk_ref, dv_ref):
        j, i = pl.program_id(1), pl.program_id(2)
        qb, kb, vb = q_ref[0].astype(BF16), k_ref[0].astype(BF16), v_ref[0].astype(BF16)
        do_f = do_ref[0]
        dob = do_f.astype(BF16)
        p = jnp.exp2(_dot(qb, kb, 1, 1) * c - lse_ref[0] * LOG2E)
        delta = jnp.sum(do_f * o_ref[0], axis=1, keepdims=True)
        ds = (p * (_dot(dob, vb, 1, 1) - delta)).astype(BF16)
        dv_part = _dot(p, dob, 0, 0)
        dk_part = _dot(ds, qb, 0, 0) * scale
        dq_part = _dot(ds, kb, 1, 0) * scale
        rows = pl.ds(pl.multiple_of(i * tq, tq), tq)

        @pl.when(i == 0)
        def _():
            dk_ref[0] = dk_part
            dv_ref[0] = dv_part

        @pl.when(i > 0)
        def _():
            dk_ref[0] += dk_part
            dv_ref[0] += dv_part

        @pl.when(j == 0)
        def _():
            dq_ref[0, rows, :] = dq_part

        @pl.when(j > 0)
        def _():
            dq_ref[0, rows, :] += dq_part

    return pl.pallas_call(
        body,
        grid=(n, sk // tk, sq // tq),
        in_specs=[
            pl.BlockSpec((1, tq, d), lambda h, j, i: (h, i, 0)),
            pl.BlockSpec((1, tk, d), lambda h, j, i: (h, j, 0)),
            pl.BlockSpec((1, tk, dv), lambda h, j, i: (h, j, 0)),
            pl.BlockSpec((1, tq, dv), lambda h, j, i: (h, i, 0)),
            pl.BlockSpec((1, tq, 1), lambda h, j, i: (h, i, 0)),
            pl.BlockSpec((1, tq, dv), lambda h, j, i: (h, i, 0)),
        ],
        out_specs=[
            pl.BlockSpec((1, sq, d), lambda h, j, i: (h, 0, 0)),
            pl.BlockSpec((1, tk, d), lambda h, j, i: (h, j, 0)),
            pl.BlockSpec((1, tk, dv), lambda h, j, i: (h, j, 0)),
        ],
        out_shape=[
            jax.ShapeDtypeStruct((n, sq, d), F32),
            jax.ShapeDtypeStruct((n, sk, d), F32),
            jax.ShapeDtypeStruct((n, sk, dv), F32),
        ],
        name=name + "_bwd",
        compiler_params=_params("arbitrary", "arbitrary", "arbitrary"),
    )(q, k, v, o, lse, do)


def _head_geometry(h, group):
    pair, a = divmod(h, 2)
    kv_pair, b = divmod(h // group, 2)
    return pair, a, kv_pair, b


def _lane_half():
    return lax.broadcasted_iota(jnp.int32, (1, LANES), 1) // HALF


def _align(x, a, b):
    if a != b:
        x = pltpu.roll(x, HALF, 1)
    return jnp.where(_lane_half() == b, x, 0.0)


def _unalign(x, a, b):
    x = jnp.where(_lane_half() == b, x, 0.0)
    return pltpu.roll(x, HALF, 1) if a != b else x


def _bands(w, pw, nw, lo, kvw, nb):
    b = BAND_BLOCK
    cat = jnp.concatenate([pw[:, lo:lo + kvw], w[:, lo:lo + kvw], nw[:, lo:lo + kvw]], axis=0).astype(BF16)
    out = []
    for g in range(kvw // LANES):
        c3 = cat[:, g * LANES:(g + 1) * LANES].reshape(nb + 2, b, LANES)
        out.append(jnp.concatenate([c3[0:nb], c3[1:nb + 1], c3[2:nb + 2]], axis=1))
    return out


def _edge_mask(first_block, nb, period):
    b = BAND_BLOCK
    blk = (first_block + lax.broadcasted_iota(jnp.int32, (nb, 1, 3 * b), 0)) % period
    col = lax.broadcasted_iota(jnp.int32, (nb, 1, 3 * b), 2)
    outside = ((col < b) & (blk == 0)) | ((col >= 2 * b) & (blk == period - 1))
    return jnp.where(outside, NEG_INF, 0.0)


def _band_geometry(proj, dil):
    rows = proj.shape[0]
    tl = _pick(rows, (1024, 512, 256, 128))
    return rows, tl, tl // BAND_BLOCK, rows // tl, rows // dil // BAND_BLOCK


def _band_in_specs(tl, nb, n_chunks, n_blocks, col, last_step_idle):
    def chunk(i):
        return jnp.minimum(i, n_chunks - 1) if last_step_idle else i

    main = pl.BlockSpec((tl, BAND_W), lambda j, i: (j * n_chunks + chunk(i), col))
    prev = pl.BlockSpec((BAND_BLOCK, BAND_W),
                        lambda j, i: (j * n_blocks + jnp.maximum(chunk(i) * nb - 1, 0), col))
    nxt = pl.BlockSpec((BAND_BLOCK, BAND_W),
                       lambda j, i: (j * n_blocks + jnp.minimum((chunk(i) + 1) * nb, n_blocks - 1), col))
    rows = pl.BlockSpec((tl, QW), lambda j, i: (j * n_chunks + chunk(i), 0))
    return main, prev, nxt, rows


def _band_fwd_call(proj, col, bias, sink, dil, group, kvw, scale, name):
    s_tok = proj.shape[0]
    seq, tl, nb, n_chunks, period = _band_geometry(proj, dil)
    n_blocks = seq // BAND_BLOCK
    heads = bias.shape[0]

    def body(w_ref, pw_ref, nw_ref, bias_ref, sink_ref, o_ref, lse_ref):
        i = pl.program_id(1)
        w, pw, nw = w_ref[...].astype(F32), pw_ref[...].astype(F32), nw_ref[...].astype(F32)
        kb = _bands(w, pw, nw, QW, kvw, nb)
        vb = _bands(w, pw, nw, QW + kvw, kvw, nb)
        edge = _edge_mask(i * nb, nb, period)
        o_acc = [jnp.zeros((tl, LANES), F32) for _ in range(heads // 2)]
        lse_acc = [jnp.zeros((tl, LANES), F32) for _ in range(heads // 2)]
        geom = [_head_geometry(h, group) for h in range(heads)]
        logits = []
        for h, (pair, a, kvp, b) in enumerate(geom):
            q_al = _align(w[:, pair * LANES:(pair + 1) * LANES], a, b).astype(BF16).reshape(nb, BAND_BLOCK, LANES)
            logits.append(_bmm(q_al, kb[kvp], 2, 2) * scale + bias_ref[h][None] + edge)
        es, ssums, ms = [], [], []
        for h in range(heads):
            sk = sink_ref[h].reshape(1, 1, 1)
            m = jnp.maximum(jnp.max(logits[h], axis=2, keepdims=True), sk)
            e = jnp.exp(logits[h] - m)
            es.append(e.astype(BF16))
            ssums.append(jnp.sum(e, axis=2, keepdims=True) + jnp.exp(sk - m))
            ms.append(m)
        for h, (pair, a, kvp, b) in enumerate(geom):
            out = _bmm(es[h], vb[kvp], 2, 1) / ssums[h]
            o_acc[pair] = o_acc[pair] + _unalign(out.reshape(tl, LANES), a, b)
            lse = (ms[h] + jnp.log(ssums[h])).reshape(tl, 1)
            lse_acc[pair] = lse_acc[pair] + jnp.where(_lane_half() == a, lse, 0.0)
        o_ref[...] = jnp.concatenate(o_acc, axis=1)
        lse_ref[...] = jnp.concatenate(lse_acc, axis=1)

    main, prev, nxt, rows = _band_in_specs(tl, nb, n_chunks, n_blocks, col, False)
    return pl.pallas_call(
        body,
        grid=(1, n_chunks),
        in_specs=[main, prev, nxt, pl.BlockSpec(bias.shape, lambda j, i: (0, 0, 0)),
                  pl.BlockSpec(sink.shape, lambda j, i: (0, 0, 0))],
        out_specs=[rows, rows],
        out_shape=[jax.ShapeDtypeStruct((s_tok, QW), F32)] * 2,
        name=name + "_fwd",
        compiler_params=_params("parallel", "parallel"),
    )(proj, proj, proj, bias, sink)


def _band_bwd_call(proj, o, do, lse, dlse, bias, sink, dproj, col, dil, group, kvw, scale, name):
    seq, tl, nb, n_chunks, period = _band_geometry(proj, dil)
    lead = 0 if dproj is None else 1
    n_blocks = seq // BAND_BLOCK
    heads = bias.shape[0]
    b_ = BAND_BLOCK
    have_dlse = dlse is not None

    def body(*refs):
        (w_ref, pw_ref, nw_ref, o_ref, do_ref, lse_ref), refs = refs[lead:lead + 6], refs[lead + 6:]
        if have_dlse:
            dlse_ref, refs = refs[0], refs[1:]
        bias_ref, sink_ref, dwin_ref, dbias_ref, dsink_ref, dq_s, dk_s, dv_s = refs
        j, i = pl.program_id(0), pl.program_id(1)

        @pl.when((j == 0) & (i == 0))
        def _():
            dbias_ref[...] = jnp.zeros_like(dbias_ref)
            dsink_ref[...] = jnp.zeros_like(dsink_ref)

        @pl.when(i == 0)
        def _():
            dk_s[...] = jnp.zeros_like(dk_s)
            dv_s[...] = jnp.zeros_like(dv_s)

        @pl.when(i < n_chunks)
        def _():
            w, pw, nw = w_ref[...].astype(F32), pw_ref[...].astype(F32), nw_ref[...].astype(F32)
            kb = _bands(w, pw, nw, QW, kvw, nb)
            vb = _bands(w, pw, nw, QW + kvw, kvw, nb)
            edge = _edge_mask(i * nb, nb, period)
            dq_acc = [jnp.zeros((tl, LANES), F32) for _ in range(heads // 2)]
            for h in range(heads):
                pair, a, kvp, b = _head_geometry(h, group)
                lanes = slice(pair * LANES, (pair + 1) * LANES)
                mine = _lane_half() == a
                q_al = _align(w[:, lanes], a, b).astype(BF16).reshape(nb, b_, LANES)
                do_al = _align(do_ref[:, lanes], a, b).astype(BF16).reshape(nb, b_, LANES)
                lse_h = jnp.max(jnp.where(mine, lse_ref[:, lanes], NEG_INF), axis=1, keepdims=True)
                shift = -jnp.sum(jnp.where(mine, do_ref[:, lanes] * o_ref[:, lanes], 0.0), axis=1, keepdims=True)
                if have_dlse:
                    shift = shift + jnp.sum(jnp.where(mine, dlse_ref[:, lanes], 0.0), axis=1, keepdims=True)
                logits = _bmm(q_al, kb[kvp], 2, 2) * scale + bias_ref[h][None] + edge
                p = jnp.exp(logits - lse_h.reshape(nb, b_, 1))
                dlogits = p * (_bmm(do_al, vb[kvp], 2, 2) + shift.reshape(nb, b_, 1))
                dbias_ref[h] += jnp.sum(dlogits, axis=0)
                dsink_ref[h] += jnp.sum(jnp.exp(sink_ref[h] - lse_h) * shift, axis=0, keepdims=True)
                ds = (dlogits * scale).astype(BF16)
                dq_acc[pair] = dq_acc[pair] + _unalign(_bmm(ds, kb[kvp], 2, 1).reshape(tl, LANES), a, b)
                dk_band = _bmm(ds, q_al, 1, 1)
                dv_band = _bmm(p.astype(BF16), do_al, 1, 1)
                kv_lanes = slice(kvp * LANES, (kvp + 1) * LANES)
                for t in range(3):
                    at = pl.ds(pl.multiple_of(i * tl + t * b_, b_), tl)
                    dk_s[at, kv_lanes] += dk_band[:, t * b_:(t + 1) * b_, :].reshape(tl, LANES)
                    dv_s[at, kv_lanes] += dv_band[:, t * b_:(t + 1) * b_, :].reshape(tl, LANES)
            dq_s[lax.rem(i, 2)] = jnp.concatenate(dq_acc, axis=1)

        @pl.when(i >= 1)
        def _():
            at = pl.ds(pl.multiple_of((i - 1) * tl + b_, b_), tl)
            parts = [dq_s[lax.rem(i + 1, 2)], dk_s[at, :], dv_s[at, :]]
            if QW + 2 * kvw < BAND_W:
                parts.append(jnp.zeros((tl, BAND_W - QW - 2 * kvw), F32))
            dwin_ref[...] = jnp.concatenate(parts, axis=1).astype(dwin_ref.dtype)

    main, prev, nxt, rows = _band_in_specs(tl, nb, n_chunks, n_blocks, col, True)
    row_args = [o, do, lse] + ([dlse] if have_dlse else [])
    small = [pl.BlockSpec(bias.shape, lambda j, i: (0, 0, 0)), pl.BlockSpec(sink.shape, lambda j, i: (0, 0, 0))]
    return pl.pallas_call(
        body,
        grid=(1, n_chunks + 1),
        in_specs=[pl.BlockSpec(memory_space=pl.ANY)] * lead + [main, prev, nxt] + [rows] * len(row_args) + small,
        out_specs=[pl.BlockSpec((tl, BAND_W), lambda j, i: (j * n_chunks + jnp.maximum(i - 1, 0), col))] + small,
        out_shape=[jax.ShapeDtypeStruct(proj.shape, BF16), jax.ShapeDtypeStruct(bias.shape, F32),
                   jax.ShapeDtypeStruct(sink.shape, F32)],
        scratch_shapes=[pltpu.VMEM((2, tl, QW), F32), pltpu.VMEM((seq + 2 * b_, kvw), F32),
                        pltpu.VMEM((seq + 2 * b_, kvw), F32)],
        input_output_aliases={0: 0} if lead else {},
        name=name + "_bwd",
        compiler_params=_params("arbitrary", "arbitrary"),
    )(*([dproj] if lead else []), proj, proj, proj, *row_args, bias, sink)


def _loss_call(x, target, g):
    s, d = x.shape
    tr = _pick(s, (256, 128, 64, 32, 16, 8))

    def tile_loss(xt, gt, tt):
        err = jnp.square(_rms(xt, gt) - tt)
        return 0.5 * jnp.sum(jnp.mean(err, axis=-1, keepdims=True), axis=0, keepdims=True)

    def body(x_ref, t_ref, g_ref, loss_ref, dx_ref, dg_ref):
        tt = t_ref[...]
        val, vjp = jax.vjp(lambda xt, gt: tile_loss(xt, gt, tt), x_ref[...], g_ref[...])
        dx, dg = vjp(jnp.ones_like(val))
        dx_ref[...] = dx

        @pl.when(pl.program_id(0) == 0)
        def _():
            loss_ref[...] = jnp.zeros_like(loss_ref)
            dg_ref[...] = jnp.zeros_like(dg_ref)

        loss_ref[...] += val
        dg_ref[...] += dg

    return pl.pallas_call(
        body,
        grid=(s // tr,),
        in_specs=[_rows(tr, d), _rows(tr, d), _whole((1, d))],
        out_specs=[_whole((1, 1)), _rows(tr, d), _whole((1, d))],
        out_shape=[jax.ShapeDtypeStruct((1, 1), F32), jax.ShapeDtypeStruct((s, d), F32),
                   jax.ShapeDtypeStruct((1, d), F32)],
        name="final_norm_loss",
        compiler_params=_params("arbitrary"),
    )(x, target, g)


@jax.custom_vjp
def _loss_op(x, target, g):
    return _loss_call(x, target, g)[0][0, 0]


def _loss_op_fwd(x, target, g):
    loss, dx, dg = _loss_call(x, target, g)
    return loss[0, 0], (dx, dg, target)


def _loss_op_bwd(res, ct):
    dx, dg, target = res
    return ct * dx, jnp.zeros_like(target), ct * dg


_loss_op.defvjp(_loss_op_fwd, _loss_op_bwd)


def _expand_cols(x, e):
    hi = x.astype(BF16)
    rest = x - hi.astype(F32)
    mid = rest.astype(BF16)
    low = (rest - mid.astype(F32)).astype(BF16)
    return _dot(hi, e, 1, 0) + _dot(mid, e, 1, 0) + _dot(low, e, 1, 0)


def _mla_tile(r, p, a):
    g_q, g_kv, w_q, w_k, w_v = p
    cs, spread, place_kr = a
    lane = lax.broadcasted_iota(jnp.int32, (1, MLA_HEADS * MLA_QK), 1)
    cos_t = jnp.where(lane % MLA_QK < MLA_NOPE, 1.0, 0.0) + _expand_cols(cs, spread[0])
    sin_t = _expand_cols(cs, spread[1])
    a_q, a_kv, a_kr = _lanes(r[0], (0, MLA_Q_LORA, MLA_Q_LORA + MLA_KV_LORA, MLA_Q_LORA + MLA_KV_LORA + MLA_ROPE))
    q = _rope(_bdot(_rms(a_q, g_q), w_q), cos_t, sin_t, MLA_ROPE // 2)
    ckv = _rms(a_kv, g_kv)
    k = _rope(_bdot(ckv, w_k) + _hdot(a_kr, place_kr), cos_t, sin_t, MLA_ROPE // 2)
    return _split_heads(q, MLA_HEADS), _split_heads(k, MLA_HEADS), _split_heads(_bdot(ckv, w_v), MLA_HEADS)


def _head_rms(x, g, head_mean):
    return x * lax.rsqrt(_hdot(x * x, head_mean) + EPS) * g


def _gqa_tile(r, p, a):
    g_q, g_k = p
    cs, spread, mean_q, mean_k = a
    cos_t, sin_t = _expand_cols(cs, spread[0]), _expand_cols(cs, spread[1])
    wq, wk = GQA_HEADS * HEAD_DIM, GQA_KV_HEADS * HEAD_DIM
    b_q, b_k, b_v = _lanes(r[0], (0, wq, wq + wk, wq + 2 * wk))
    q = _rope(_head_rms(b_q, g_q, mean_q), cos_t, sin_t, HEAD_DIM // 4)
    k = _rope(_head_rms(b_k, g_k, mean_k), cos_t[:, :wk], sin_t[:, :wk], HEAD_DIM // 4)
    return _split_heads(q, GQA_HEADS), _split_heads(k, GQA_KV_HEADS), _split_heads(b_v, GQA_KV_HEADS)


def _permute_rows(p, x, cp):
    pb = p.astype(BF16)
    hi = x.astype(BF16)
    rest = x - hi.astype(F32)
    mid = rest.astype(BF16)
    low = (rest - mid.astype(F32)).astype(BF16)
    dims = (((cp,), (0,)), ((), ()))
    return (lax.dot_general(pb, hi, dims, preferred_element_type=F32)
            + lax.dot_general(pb, mid, dims, preferred_element_type=F32)
            + lax.dot_general(pb, low, dims, preferred_element_type=F32))


@jax.custom_vjp
def _permuted(p, x):
    return _permute_rows(p, x, 1)


def _permuted_fwd(p, x):
    return _permute_rows(p, x, 1), p


def _permuted_bwd(p, ct):
    return jnp.zeros_like(p), _permute_rows(p, ct, 0)


_permuted.defvjp(_permuted_fwd, _permuted_bwd)


def _interleave(p, x):
    return _permuted(p, x.reshape(x.shape[0] * x.shape[1], x.shape[2]))


def _interleave_matrix(rows, dil):
    p = np.zeros((rows, rows), np.float32)
    for t in range(rows):
        p[t, (t % dil) * (rows // dil) + t // dil] = 1.0
    return p


def _merge_tile(r, p, a):
    gm, o_a, o_b, oc0, oc1, oc2, l0, l1, l2, o_d = r
    (w_branch,) = p
    perm1, perm2 = a
    oc1, l1, oc2, l2 = _interleave(perm1, oc1), _interleave(perm1, l1), _interleave(perm2, oc2), _interleave(perm2, l2)
    d = w_branch.shape[2]
    gate_path, merge_logits = _lanes(gm, (0, N_BRANCH * BRANCH_W, N_BRANCH * BRANCH_W + N_BRANCH * d))
    m = jnp.maximum(jnp.maximum(l0, l1), l2)
    e0, e1, e2 = jnp.exp(l0 - m), jnp.exp(l1 - m), jnp.exp(l2 - m)
    y_c = (e0 * oc0 + e1 * oc1 + e2 * oc2) / (e0 + e1 + e2)
    y = jnp.concatenate([_join_heads(o_a), _join_heads(o_b), y_c, o_d], axis=1)
    u = y * (gate_path * jax.nn.sigmoid(gate_path))
    gates = _lanes(merge_logits, tuple(range(0, N_BRANCH * d + 1, d)))
    us = _lanes(u, tuple(range(0, N_BRANCH * BRANCH_W + 1, BRANCH_W)))
    branch_w = _unstack(w_branch)
    out = None
    for nb in range(N_BRANCH):
        term = jax.nn.sigmoid(gates[nb]) * _bdot(us[nb], branch_w[nb])
        out = term if out is None else out + term
    return (out,)


def _mixer_calls(proj, prm, aux):
    s = proj.shape[0]
    tr, tm = _pick(s, (512, 256, 128)), _pick(s, (256,))
    mla_cs, gqa_cs, mla_spread, gqa_spread, place_kr, mean_q, mean_k = aux[:7]
    wq = MLA_HEADS * MLA_QK
    mla = dict(
        steps=s // tr, rows=[(proj, _rows(tr, SMALL_W, MLA_BLK))],
        params=[prm["g_q"], prm["g_kv"], prm["w_q"], prm["w_k"], prm["w_v"]],
        aux=[(mla_cs, _rows(tr, mla_cs.shape[1])), (mla_spread, _whole(mla_spread.shape)),
             (place_kr, _whole(place_kr.shape))],
        outs=[((MLA_HEADS, s, MLA_QK), _head_rows(MLA_HEADS, tr, MLA_QK), BF16)] * 2
        + [((MLA_HEADS, s, MLA_V), _head_rows(MLA_HEADS, tr, MLA_V), BF16)],
        window=((s, P_TOT), _rows(tr, SMALL_W, MLA_BLK), BF16))
    wg = GQA_HEADS * HEAD_DIM
    gqa = dict(
        steps=s // tr, rows=[(proj, _rows(tr, SMALL_W, GQA_BLK))], params=[prm["gq"], prm["gk"]],
        aux=[(gqa_cs, _rows(tr, gqa_cs.shape[1])), (gqa_spread, _whole(gqa_spread.shape)), (mean_q, _whole(mean_q.shape)),
             (mean_k, _whole(mean_k.shape))],
        outs=[((GQA_HEADS, s, HEAD_DIM), _head_rows(GQA_HEADS, tr, HEAD_DIM), BF16)]
        + [((GQA_KV_HEADS, s, HEAD_DIM), _head_rows(GQA_KV_HEADS, tr, HEAD_DIM), BF16)] * 2,
        window=((s, P_TOT), _rows(tr, SMALL_W, GQA_BLK), BF16))
    merge = dict(steps=s // tm, tm=tm, window=((s, P_TOT), _rows(tm, GM_W, 0), BF16))
    return mla, gqa, merge


def _merge_rows(proj, o_a, o_b, ocs, lses, o_d, tm):
    h4 = _head_rows(4, tm, HEAD_DIM)
    s = proj.shape[0]

    def by_residue(t, dil):
        if dil == 1:
            return t, _rows(tm, QW)
        return t.reshape(dil, s // dil, QW), pl.BlockSpec((dil, tm // dil, QW), lambda i: (0, i, 0))

    dils = [dil for _, dil in DIL_PATTERNS]
    return ([(proj, _rows(tm, GM_W, 0)), (o_a, h4), (o_b, h4)] + [by_residue(t, r) for t, r in zip(ocs, dils)]
            + [by_residue(t, r) for t, r in zip(lses, dils)] + [(o_d, _rows(tm, QW))])


def _merge_aux(aux):
    return [(t, _whole(t.shape)) for t in aux[7:9]]


def _to_residues(t, dil):
    s, w = t.shape
    return t if dil == 1 else t.reshape(s // dil, dil, w).transpose(1, 0, 2).reshape(s, w)


def _from_residues(t, dil):
    s, w = t.shape
    return t if dil == 1 else t.reshape(dil, s // dil, w).transpose(1, 0, 2).reshape(s, w)


def _mixer_fwd(projs, prm, aux):
    proj = projs[0]
    s = proj.shape[0]
    mla, gqa, merge = _mixer_calls(proj, prm, aux)
    q_a, k_a, v_a = _fwd_call("prep_mla", _mla_tile, mla["steps"], mla["rows"], mla["params"], mla["aux"], mla["outs"])
    o_a, lse_a = _dense_fwd_call(q_a, k_a, v_a, MLA_QK ** -0.5, "mla")
    q_b, k_b, v_b = _fwd_call("prep_gqa", _gqa_tile, gqa["steps"], gqa["rows"], gqa["params"], gqa["aux"], gqa["outs"])
    grp = GQA_HEADS // GQA_KV_HEADS
    o_b, lse_b = _dense_fwd_call(q_b.reshape(GQA_KV_HEADS, grp * s, HEAD_DIM), k_b, v_b, HEAD_DIM ** -0.5, "gqa")
    scale = HEAD_DIM ** -0.5
    ocs, lses = [], []
    for gi, (_, dil) in enumerate(DIL_PATTERNS):
        o, lse = _band_fwd_call(projs[gi], DIL_BLK if gi == 0 else 0, prm["bias_dil"][gi], prm["no_sink"], dil, 1,
                                QW, scale, "dil%d" % gi)
        ocs.append(o)
        lses.append(lse)
    o_d, lse_d = _band_fwd_call(proj, WIN_BLK, prm["bias_win"], prm["sink"], 1, WIN_HEADS // WIN_KV_HEADS,
                                WIN_KV_HEADS * HEAD_DIM, scale, "win")
    rows = _merge_rows(proj, o_a, o_b.reshape(GQA_HEADS, s, HEAD_DIM), ocs, lses, o_d, merge["tm"])
    d_model, tm = prm["w_branch"].shape[2], merge["tm"]

    def with_transpose(r, p, a):
        (out,) = _merge_tile(r, p, a)
        return out, out.T

    mix, mix_t = _fwd_call("merge", with_transpose, merge["steps"], rows, [prm["w_branch"]], _merge_aux(aux),
                           [((s, d_model), _rows(tm, d_model), BF16),
                            ((d_model, s), pl.BlockSpec((d_model, tm), lambda i: (0, i)), BF16)])
    return (mix, mix_t), (q_a, k_a, v_a, o_a, lse_a, q_b, k_b, v_b, o_b, lse_b, ocs, lses, o_d, lse_d)


def _mixer_bwd(projs, prm, aux, saved, dmix):
    proj = projs[0]
    s = proj.shape[0]
    q_a, k_a, v_a, o_a, lse_a, q_b, k_b, v_b, o_b, lse_b, ocs, lses, o_d, lse_d = saved
    dils = [dil for _, dil in DIL_PATTERNS]
    mla, gqa, merge = _mixer_calls(proj, prm, aux)
    tm, d_model = merge["tm"], prm["w_branch"].shape[2]
    grp = GQA_HEADS // GQA_KV_HEADS
    scale = HEAD_DIM ** -0.5

    rows = _merge_rows(proj, o_a, o_b.reshape(GQA_HEADS, s, HEAD_DIM), ocs, lses, o_d, tm)
    grads, (dw_branch,) = _vjp_call(
        "merge", _merge_tile, merge["steps"], rows, [prm["w_branch"]], _merge_aux(aux), [(dmix, _rows(tm, d_model))],
        [merge["window"]] + [(a.shape, spec) for a, spec in rows[1:]])
    dproj, do_a, do_b, docs, dlses, do_d = grads[0], grads[1], grads[2], grads[3:6], grads[6:9], grads[9]

    dq_a, dk_a, dv_a = _dense_bwd_call(q_a, k_a, v_a, o_a, lse_a, do_a, MLA_QK ** -0.5, "mla")
    (dproj,), dmla = _vjp_call("prep_mla", _mla_tile, mla["steps"], mla["rows"], mla["params"], mla["aux"],
                               [(t, e[1]) for t, e in zip((dq_a, dk_a, dv_a), mla["outs"])],
                               [mla["window"]], into=dproj)
    dq_b, dk_b, dv_b = _dense_bwd_call(q_b.reshape(GQA_KV_HEADS, grp * s, HEAD_DIM), k_b, v_b, o_b, lse_b,
                                       do_b.reshape(GQA_KV_HEADS, grp * s, HEAD_DIM), scale, "gqa")
    (dproj,), dgqa = _vjp_call("prep_gqa", _gqa_tile, gqa["steps"], gqa["rows"], gqa["params"], gqa["aux"],
                               [(t, e[1]) for t, e in zip((dq_b.reshape(GQA_HEADS, s, HEAD_DIM), dk_b, dv_b), gqa["outs"])],
                               [gqa["window"]], into=dproj)
    dproj, dbias_win, dsink = _band_bwd_call(proj, o_d, do_d, lse_d, None, prm["bias_win"], prm["sink"], dproj,
                                             WIN_BLK, 1, WIN_HEADS // WIN_KV_HEADS, WIN_KV_HEADS * HEAD_DIM, scale, "win")
    dbias_dil, dprojs = [], []
    for gi, dil in enumerate(dils):
        dside, dbias, _ = _band_bwd_call(
            projs[gi], ocs[gi], docs[gi].reshape(s, QW), lses[gi], dlses[gi].reshape(s, QW),
            prm["bias_dil"][gi], prm["no_sink"], dproj if gi == 0 else None, DIL_BLK if gi == 0 else 0, dil, 1, QW,
            scale, "dil%d" % gi)
        if gi == 0:
            dproj = dside
        else:
            dprojs.append(dside)
        dbias_dil.append(dbias)
    dprm = dict(g_q=dmla[0], g_kv=dmla[1], w_q=dmla[2], w_k=dmla[3], w_v=dmla[4], gq=dgqa[0], gk=dgqa[1],
                bias_dil=dbias_dil, bias_win=dbias_win, sink=dsink, no_sink=jnp.zeros_like(prm["no_sink"]),
                w_branch=dw_branch)
    return [dproj] + dprojs, {k: jax.tree.map(lambda g, p: g.astype(p.dtype), v, prm[k]) for k, v in dprm.items()}


def _layer_fwd(x, w, aux):
    s, d = x.shape
    tr = _pick(s, (256,))
    dils = [dil for _, dil in DIL_PATTERNS]

    def norm_forms(r, p, a):
        y = _rms(r[0], p[0])
        return [y, y.T] + [_dot(q, y, 1, 0).reshape(dil, tr // dil, d) for q, dil in zip(a, dils[1:])]

    forms = _fwd_call(
        "norm", norm_forms, s // tr, [(x, _rows(tr, d))], [w["norm_g"]], [(q, _whole(q.shape)) for q in aux[9:11]],
        [((s, d), _rows(tr, d), BF16), ((d, s), pl.BlockSpec((d, tr), lambda i: (0, i)), BF16)]
        + [((dil, s // dil, d), pl.BlockSpec((dil, tr // dil, d), lambda i: (0, i, 0)), BF16) for dil in dils[1:]])
    xn_t, xns = forms[1], [forms[0]] + [t.reshape(s, d) for t in forms[2:]]
    projs = [_mm(a, b, "nt", "proj%d_fwd" % i, BF16) for i, (a, b) in enumerate(zip(xns, w["w_in_t"]))]
    (mix, mix_t), saved = _mixer_fwd(projs, w["mixer"], aux)
    return _mm(mix, w["w_out"], "nn", "out_proj_nn"), (x, w, aux, xns, xn_t, projs, mix_t, saved)


@jax.custom_vjp
def _layer_core(x, w, aux):
    return _layer_fwd(x, w, aux)[0]


def _layer_core_bwd(res, dout):
    x, w, aux, xns, xn_t, projs, mix_t, saved = res
    s, d = x.shape
    tr = _pick(s, (256, 128, 64, 32, 16, 8))
    dils = [dil for _, dil in DIL_PATTERNS]
    dmix = _mm(dout, w["w_out"], "nt", "out_proj_nt")
    dw_out = _mm(mix_t, dout, "nn", "out_proj_dw", w["w_out"].dtype)
    dprojs, dmixer = _mixer_bwd(projs, w["mixer"], aux, saved, dmix)
    side = jnp.concatenate([_from_residues(dp, r) for dp, r in zip(dprojs[1:], dils[1:])], axis=1)
    dxn_terms = [_mm(dprojs[0], w["w_in_t"][0], "nn", "proj0_dx"),
                 _mm(side, jnp.concatenate(w["w_in_t"][1:], axis=0), "nn", "proj_side_dx")]
    dw_in_t = [_mm(a_t, dp, "nn", "proj%d_dw" % i, wi.dtype).T
               for i, (a_t, dp, wi) in enumerate(zip([xn_t] + [a.T for a in xns[1:]], dprojs, w["w_in_t"]))]
    (dx,), (dg,) = _vjp_call("norm", lambda r, p, a: _norm_tile(r, p, a) * len(dxn_terms), s // tr, [(x, _rows(tr, d))],
                             [w["norm_g"]], [], [(t, _rows(tr, d)) for t in dxn_terms], [((s, d), _rows(tr, d))])
    dw = dict(norm_g=dg, w_in_t=dw_in_t, mixer=dmixer, w_out=dw_out)
    return dx, dw, tuple(jnp.zeros_like(t) for t in aux)


_layer_core.defvjp(lambda x, w, aux: _layer_fwd(x, w, aux), _layer_core_bwd)


def _rope_angles(pos, dim):
    inv = ROPE_THETA ** (-jnp.arange(0, dim, 2, dtype=F32) / dim)
    return pos.astype(F32)[:, None] * inv[None, :]


def _rope_tables(s):
    pos = jnp.arange(s, dtype=jnp.int32)
    rows = s // GRID_W
    row_idx = jnp.repeat(jnp.arange(rows, dtype=jnp.int32), GRID_W)
    col_idx = jnp.tile(jnp.arange(GRID_W, dtype=jnp.int32), rows)
    a1 = _rope_angles(pos, MLA_ROPE)
    ar = _rope_angles(row_idx, HEAD_DIM // 2)
    ac = _rope_angles(col_idx, HEAD_DIM // 2)
    h = MLA_ROPE // 2
    mla_cs = jnp.concatenate([jnp.cos(a1), jnp.sin(a1)], axis=1)
    gqa_cs = jnp.concatenate([jnp.cos(ar), jnp.cos(ac), jnp.sin(ar), jnp.sin(ac)], axis=1)
    mla_spread = np.zeros((2, 2 * h, MLA_HEADS * MLA_QK), np.float32)
    gqa_spread = np.zeros((2, 4 * h, GQA_HEADS * HEAD_DIM), np.float32)
    for i in range(h):
        for head in range(MLA_HEADS):
            lo = head * MLA_QK + MLA_NOPE + i
            mla_spread[0, i, [lo, lo + h]] = 1.0
            mla_spread[1, h + i, lo], mla_spread[1, h + i, lo + h] = -1.0, 1.0
        for head in range(GQA_HEADS):
            for axis in range(2):
                lo = head * HEAD_DIM + axis * 2 * h + i
                gqa_spread[0, axis * h + i, [lo, lo + h]] = 1.0
                gqa_spread[1, 2 * h + axis * h + i, lo], gqa_spread[1, 2 * h + axis * h + i, lo + h] = -1.0, 1.0
    return mla_cs, gqa_cs, jnp.asarray(mla_spread), jnp.asarray(gqa_spread)


def _t5_bucket(rel):
    nb = T5_BUCKETS // 2
    max_exact = nb // 2
    n = jnp.abs(rel)
    nf = jnp.maximum(n, 1).astype(F32)
    large = max_exact + (jnp.log(nf / max_exact) / math.log(T5_MAX_DIST / max_exact) * (nb - max_exact)).astype(jnp.int32)
    large = jnp.minimum(large, nb - 1)
    return jnp.where(rel > 0, nb, 0) + jnp.where(n < max_exact, n, large)


def _band_bias(table, stride, head_lo, heads, half_window):
    b = BAND_BLOCK
    offs = jnp.arange(3 * b)[None, :] - b - jnp.arange(b)[:, None]
    one_hot = (_t5_bucket(offs * stride)[..., None] == jnp.arange(T5_BUCKETS)).astype(F32)
    bias = jnp.dot(one_hot.reshape(b * 3 * b, T5_BUCKETS), table[:, head_lo:head_lo + heads],
                   precision=lax.Precision.HIGHEST)
    bias = bias.T.reshape(heads, b, 3 * b)
    return jnp.where((jnp.abs(offs) <= half_window)[None], bias, NEG_INF)


def _w_in_rows(d):
    mla, gqa, win, dil0 = MLA_BLK * SMALL_W, GQA_BLK * SMALL_W, WIN_BLK * BAND_W, DIL_BLK * BAND_W
    plan, at = [], 0
    for width, target, row in ((256, 0, mla), (128, 0, mla + 256), (32, 0, mla + 384),
                               (256, 0, gqa), (128, 0, gqa + 256), (128, 0, gqa + 384)):
        plan.append((at, width, target, row))
        at += width
    for part in range(3):
        for g in range(len(DIL_PATTERNS)):
            plan.append((at, QW, g, (dil0 if g == 0 else 0) + part * QW))
            at += QW
    for width, row in ((256, win), (128, win + 256), (128, win + 384), (N_BRANCH * BRANCH_W, 0),
                       (N_BRANCH * d, N_BRANCH * BRANCH_W)):
        plan.append((at, width, 0, row))
        at += width
    return plan


@jax.custom_vjp
def _w_in_layout(w_in_t):
    d = w_in_t.shape[1]
    outs = []
    for target, rows in enumerate((P_TOT, BAND_W, BAND_W)):
        parts, at = [], 0
        for start, width, _, row in sorted((p for p in _w_in_rows(d) if p[2] == target), key=lambda p: p[3]):
            if row > at:
                parts.append(jnp.zeros((row - at, d), w_in_t.dtype))
            parts.append(w_in_t[start:start + width])
            at = row + width
        if at < rows:
            parts.append(jnp.zeros((rows - at, d), w_in_t.dtype))
        outs.append(jnp.concatenate(parts, axis=0))
    return outs


def _w_in_layout_fwd(w_in_t):
    return _w_in_layout(w_in_t), None


def _w_in_layout_bwd(_, cts):
    d = cts[0].shape[1]
    return (jnp.concatenate([cts[target][row:row + width] for _, width, target, row in _w_in_rows(d)], axis=0),)


_w_in_layout.defvjp(_w_in_layout_fwd, _w_in_layout_bwd)


def _layer(x, w, l, aux, biases):
    w_kv = w["w_kv_t"][l].T.reshape(MLA_KV_LORA, MLA_HEADS, MLA_NOPE + MLA_V)
    w_k = jnp.concatenate([w_kv[:, :, :MLA_NOPE], jnp.zeros((MLA_KV_LORA, MLA_HEADS, MLA_ROPE), w_kv.dtype)], axis=2)
    dil_bias, win_bias = biases
    prm = dict(
        g_q=w["mla_q_norm_g"][l][None, :], g_kv=w["mla_kv_norm_g"][l][None, :], w_q=w["w_q_t"][l].T,
        w_k=w_k.reshape(MLA_KV_LORA, MLA_HEADS * MLA_QK),
        w_v=w_kv[:, :, MLA_NOPE:].reshape(MLA_KV_LORA, MLA_HEADS * MLA_V),
        gq=jnp.tile(w["gqa_q_norm_g"][l], GQA_HEADS)[None, :], gk=jnp.tile(w["gqa_k_norm_g"][l], GQA_KV_HEADS)[None, :],
        bias_dil=list(dil_bias), bias_win=win_bias, sink=w["win_sink"][l].reshape(WIN_HEADS, 1, 1),
        no_sink=jnp.full((DIL_HEADS, 1, 1), NEG_INF, F32), w_branch=jnp.transpose(w["w_branch_t"][l].reshape(-1, N_BRANCH, BRANCH_W), (1, 2, 0)))
    layer_w = dict(norm_g=w["norm_g"][l][None, :], w_in_t=_w_in_layout(w["w_in_t"][l]), mixer=prm, w_out=w["w_out"][l])
    return x + _layer_core(x, layer_w, aux)


def _local_loss(w, x, target):
    s, d_model = x.shape
    assert d_model == D_MODEL, "the projection's window layout is laid out for d_model 1024"
    place = np.zeros((MLA_ROPE, MLA_HEADS * MLA_QK), np.float32)
    for h in range(MLA_HEADS):
        for i in range(MLA_ROPE):
            place[i, h * MLA_QK + MLA_NOPE + i] = 1.0

    def head_mean(nh):
        m = np.kron(np.eye(nh, dtype=np.float32), np.full((HEAD_DIM, HEAD_DIM), 1.0 / HEAD_DIM, np.float32))
        return jnp.asarray(m)

    merge_tile = _pick(s, (256,))
    norm_tile = _pick(s, (256,))
    aux = _rope_tables(s) + (jnp.asarray(place), head_mean(GQA_HEADS), head_mean(GQA_KV_HEADS)) + tuple(
        jnp.asarray(_interleave_matrix(merge_tile, dil)) for _, dil in DIL_PATTERNS[1:]) + tuple(
        jnp.asarray(_interleave_matrix(norm_tile, dil).T) for _, dil in DIL_PATTERNS[1:])
    table = w["t5_table"]
    dil_bias = [_band_bias(table, dil, gi * DIL_HEADS, DIL_HEADS, window // (2 * dil))
                for gi, (window, dil) in enumerate(DIL_PATTERNS)]
    win_bias = _band_bias(table, 1, len(DIL_PATTERNS) * DIL_HEADS, WIN_HEADS, WIN_HALF)
    for l in range(w["norm_g"].shape[0]):
        x = _layer(x, w, l, aux, (dil_bias, win_bias))
    return _loss_op(x, target, w["final_norm_g"][None, :])


_ANY = pl.BlockSpec(memory_space=pl.ANY)
_MESH = pl.DeviceIdType.MESH


def _all_gather(block, name):
    def body(x_ref, out_ref, send_sems, recv_sems, local_sem):
        x, y, c = lax.axis_index("x"), lax.axis_index("y"), lax.axis_index("c")
        me, sibling = (x, y, c), (x, y, 1 - c)
        chips = [(1 - x, y), (x, 1 - y), (1 - x, 1 - y)]

        def slot(px, py, pc):
            return out_ref.at[4 * px + 2 * py + pc]

        def copy(k, blk, to, src=None):
            return pltpu.make_async_remote_copy(
                src_ref=slot(*blk) if src is None else src, dst_ref=slot(*blk),
                send_sem=send_sems.at[k], recv_sem=recv_sems.at[k], device_id=to, device_id_type=_MESH)

        mine = pltpu.make_async_copy(x_ref, slot(*me), local_sem)
        mine.start()
        first = [copy(0, me, sibling, src=x_ref)]
        first += [copy(1 + j, me, (*chip, c), src=x_ref) for j, chip in enumerate(chips)]
        for cp in first:
            cp.start()
        passed = [copy(4 + j, (*chip, c), sibling) for j, chip in enumerate(chips)]
        for j, chip in enumerate(chips):
            copy(1 + j, (*chip, c), me).wait_recv()
            passed[j].start()
        copy(0, sibling, me).wait_recv()
        for j, chip in enumerate(chips):
            copy(4 + j, (*chip, 1 - c), me).wait_recv()
        for cp in first + passed:
            cp.wait_send()
        mine.wait()

    return pl.pallas_call(
        body,
        out_shape=jax.ShapeDtypeStruct((N_DEV,) + block.shape, block.dtype),
        in_specs=[_ANY],
        out_specs=_ANY,
        scratch_shapes=[pltpu.SemaphoreType.DMA((7,)), pltpu.SemaphoreType.DMA((7,)), pltpu.SemaphoreType.DMA],
        name=name,
    )(block)


def _swap_with_sibling(blocks, name):
    chips = blocks.shape[0]

    def body(x_ref, out_ref, send_sems, recv_sems):
        x, y, c = lax.axis_index("x"), lax.axis_index("y"), lax.axis_index("c")
        copies = [pltpu.make_async_remote_copy(
            src_ref=x_ref.at[k, 1 - c], dst_ref=out_ref.at[k], send_sem=send_sems.at[k], recv_sem=recv_sems.at[k],
            device_id=(x, y, 1 - c), device_id_type=_MESH) for k in range(chips)]
        for cp in copies:
            cp.start()
        for cp in copies:
            cp.wait()

    return pl.pallas_call(
        body,
        out_shape=jax.ShapeDtypeStruct((chips,) + blocks.shape[2:], blocks.dtype),
        in_specs=[_ANY],
        out_specs=_ANY,
        scratch_shapes=[pltpu.SemaphoreType.DMA((chips,)), pltpu.SemaphoreType.DMA((chips,))],
        name=name,
    )(blocks)


def _add_sibling(blocks, theirs, name):
    chips, _, rows, w = blocks.shape
    tr = _row_tile(rows, 16, 4096)

    def body(b_ref, t_ref, o_ref):
        mine = b_ref[0, lax.axis_index("c")]
        o_ref[0] = (mine.astype(F32) + t_ref[0].astype(F32)).astype(o_ref.dtype)

    return pl.pallas_call(
        body,
        grid=(chips, rows // tr),
        in_specs=[pl.BlockSpec((1, 2, tr, w), lambda k, i: (k, 0, i, 0)), pl.BlockSpec((1, tr, w), lambda k, i: (k, i, 0))],
        out_specs=pl.BlockSpec((1, tr, w), lambda k, i: (k, i, 0)),
        out_shape=jax.ShapeDtypeStruct(theirs.shape, theirs.dtype),
        name=name,
        compiler_params=_params("parallel", "parallel"),
    )(blocks, theirs)


def _exchange_chips(partials, name):
    n_chips = partials.shape[0]

    def body(x_ref, out_ref, send_sems, recv_sems, local_sem):
        x, y, c = lax.axis_index("x"), lax.axis_index("y"), lax.axis_index("c")
        me = 2 * x + y
        mine = pltpu.make_async_copy(x_ref.at[me], out_ref.at[me], local_sem)
        mine.start()
        copies, landed = [], []
        for k in range(1, n_chips):
            px = 1 - x if k & 2 else x
            py = 1 - y if k & 1 else y
            peer = 2 * px + py
            copies.append(pltpu.make_async_remote_copy(
                src_ref=x_ref.at[peer], dst_ref=out_ref.at[me], send_sem=send_sems.at[k - 1],
                recv_sem=recv_sems.at[k - 1], device_id=(px, py, c), device_id_type=_MESH))
            landed.append(pltpu.make_async_remote_copy(
                src_ref=x_ref.at[peer], dst_ref=out_ref.at[peer], send_sem=send_sems.at[k - 1],
                recv_sem=recv_sems.at[k - 1], device_id=(px, py, c), device_id_type=_MESH))
        for cp in copies:
            cp.start()
        for cp in landed:
            cp.wait_recv()
        for cp in copies:
            cp.wait_send()
        mine.wait()

    return pl.pallas_call(
        body,
        out_shape=jax.ShapeDtypeStruct(partials.shape, partials.dtype),
        in_specs=[_ANY],
        out_specs=_ANY,
        scratch_shapes=[pltpu.SemaphoreType.DMA((n_chips - 1,)), pltpu.SemaphoreType.DMA((n_chips - 1,)),
                        pltpu.SemaphoreType.DMA],
        name=name,
    )(partials)


def _sum_slots(parts, name):
    slots, rows, w = parts.shape
    tr = _row_tile(rows, 16 if parts.dtype == BF16 else 8, 4096)

    def body(p_ref, o_ref):
        acc = p_ref[0].astype(F32)
        for j in range(1, slots):
            acc = acc + p_ref[j].astype(F32)
        o_ref[...] = acc

    return pl.pallas_call(
        body,
        grid=(rows // tr,),
        in_specs=[pl.BlockSpec((slots, tr, w), lambda i: (0, i, 0))],
        out_specs=pl.BlockSpec((tr, w), lambda i: (i, 0)),
        out_shape=jax.ShapeDtypeStruct((rows, w), F32),
        name=name,
        compiler_params=_params("parallel"),
    )(parts)


def _adamw(w, g, m, v, name):
    rows, width = w.shape
    lanes = -(-width // LANES) * LANES
    tr = _row_tile(rows, 8, max(8, V7X_VMEM_LIMIT // 3 // (7 * 2 * 4 * lanes) // 8 * 8))

    def body(w_ref, g_ref, m_ref, v_ref, d_ref, nm_ref, nv_ref):
        g_ = g_ref[...]
        m_ = ADAM_B1 * m_ref[...] + (1.0 - ADAM_B1) * g_
        v_ = ADAM_B2 * v_ref[...] + (1.0 - ADAM_B2) * jnp.square(g_)
        m_hat = m_ / (1.0 - ADAM_B1 ** ADAM_STEP)
        v_hat = v_ / (1.0 - ADAM_B2 ** ADAM_STEP)
        d_ref[...] = -ADAM_LR * (m_hat / (jnp.sqrt(v_hat) + ADAM_EPS) + ADAM_WD * w_ref[...])
        nm_ref[...] = m_
        nv_ref[...] = v_

    spec = pl.BlockSpec((tr, width), lambda i: (i, 0))
    return pl.pallas_call(
        body,
        grid=(rows // tr,),
        in_specs=[spec] * 4,
        out_specs=[spec] * 3,
        out_shape=[jax.ShapeDtypeStruct((rows, width), F32)] * 3,
        name=name,
        compiler_params=_params("parallel"),
    )(w, g, m, v)


_SHARDED = (("w_in", 2), ("w_mla_q_up", 2), ("w_mla_kv_up", 2), ("w_branch", 3), ("w_out", 1))
_REPLICATED = ("norm_g", "mla_q_norm_g", "mla_kv_norm_g", "gqa_q_norm_g", "gqa_k_norm_g", "win_sink", "t5_table",
               "final_norm_g")


def _pack(arrays, row_multiple):
    flat = jnp.concatenate([a.reshape(-1) for a in arrays])
    rows = -(-flat.shape[0] // (LANES * row_multiple)) * row_multiple
    return jnp.pad(flat, (0, rows * LANES - flat.shape[0])).reshape(rows, LANES)


def _unpack(packed, shapes):
    flat, out, at = packed.reshape(-1), [], 0
    for shp in shapes:
        n = int(np.prod(shp))
        out.append(flat[at:at + n].reshape(shp))
        at += n
    return out


_TO_WIRE = {
    "w_in": lambda t: jnp.swapaxes(t, 1, 2), "w_mla_q_up": lambda t: jnp.swapaxes(t, 1, 2),
    "w_mla_kv_up": lambda t: jnp.swapaxes(t, 1, 2),
    "w_branch": lambda t: jnp.transpose(t, (0, 3, 1, 2)).reshape(t.shape[0], t.shape[3], -1), "w_out": lambda t: t}
_FROM_WIRE = {
    "w_in": lambda t, shp: jnp.swapaxes(t, 1, 2), "w_mla_q_up": lambda t, shp: jnp.swapaxes(t, 1, 2),
    "w_mla_kv_up": lambda t, shp: jnp.swapaxes(t, 1, 2),
    "w_branch": lambda t, shp: jnp.transpose(t.reshape(shp[0], shp[3], shp[1], shp[2]), (0, 2, 3, 1)),
    "w_out": lambda t, shp: t}
_WIRE_NAME = {"w_in": "w_in_t", "w_mla_q_up": "w_q_t", "w_mla_kv_up": "w_kv_t", "w_branch": "w_branch_t",
              "w_out": "w_out"}


def _transpose_blocks(t, dtype, name):
    depth, a, b = t.shape

    def body(x_ref, o_ref):
        o_ref[0] = x_ref[0].T.astype(o_ref.dtype)

    return pl.pallas_call(
        body,
        grid=(depth,),
        in_specs=[pl.BlockSpec((1, a, b), lambda i: (i, 0, 0))],
        out_specs=pl.BlockSpec((1, b, a), lambda i: (i, 0, 0)),
        out_shape=jax.ShapeDtypeStruct((depth, b, a), dtype),
        name=name,
        compiler_params=_params("parallel"),
    )(t)


def _join_shards(gathered, wire_shapes):
    out, at = [], 0
    for depth, cut, rest in wire_shapes:
        n = depth * cut * rest // LANES
        blk = gathered[:, at:at + n].reshape(N_DEV, depth, cut, rest)
        out.append(jnp.moveaxis(blk, 0, 1).reshape(depth, N_DEV * cut, rest))
        at += n
    return out


def _split_shards(fulls, wire_shapes):
    parts = []
    for full, (depth, cut, rest) in zip(fulls, wire_shapes):
        blk = jnp.moveaxis(full.reshape(depth, N_DEV, cut, rest), 1, 0)
        parts.append(blk.reshape(N_DEV, depth * cut * rest // LANES, LANES))
    packed = jnp.concatenate(parts, axis=1)
    return packed.reshape((N_DEV // 2, 2) + packed.shape[1:])


def kernel(x, norm_g, w_in, mla_q_norm_g, mla_kv_norm_g, w_mla_q_up, w_mla_kv_up, gqa_q_norm_g, gqa_k_norm_g, win_sink, t5_table, w_branch, w_out, final_norm_g, loss_target, m_norm_g, m_w_in, m_mla_q_norm_g, m_mla_kv_norm_g, m_w_mla_q_up, m_w_mla_kv_up, m_gqa_q_norm_g, m_gqa_k_norm_g, m_win_sink, m_t5_table, m_w_branch, m_w_out, m_final_norm_g, v_norm_g, v_w_in, v_mla_q_norm_g, v_mla_kv_norm_g, v_w_mla_q_up, v_w_mla_kv_up, v_gqa_q_norm_g, v_gqa_k_norm_g, v_win_sink, v_t5_table, v_w_branch, v_w_out, v_final_norm_g):
    given = dict(locals())
    names = ("norm_g", "w_in", "mla_q_norm_g", "mla_kv_norm_g", "w_mla_q_up", "w_mla_kv_up", "gqa_q_norm_g",
             "gqa_k_norm_g", "win_sink", "t5_table", "w_branch", "w_out", "final_norm_g")
    shard_names = [n for n, _ in _SHARDED]
    shard_shapes = [given[n].shape for n in shard_names]

    wire = [_transpose_blocks(given[n], BF16, "w_in_to_wire") if n == "w_in" else _TO_WIRE[n](given[n]).astype(BF16)
            for n in shard_names]
    wire_shapes = [t.shape for t in wire]
    gathered = _all_gather(jnp.concatenate([t.reshape(-1, LANES) for t in wire]), "gather_weights")
    weights = {n: given[n] for n in _REPLICATED}
    weights.update(zip([_WIRE_NAME[n] for n in shard_names], _join_shards(gathered, wire_shapes)))

    loss, (gw, gx) = jax.value_and_grad(_local_loss, argnums=(0, 1))(weights, x[0], loss_target[0])
    loss = lax.psum(loss, ("x", "y", "c"))

    send = _split_shards([gw[_WIRE_NAME[n]] for n in shard_names], wire_shapes)
    partials = _add_sibling(send, _swap_with_sibling(send, "swap_grads"), "add_sibling_grads")
    g_wire = _unpack(_sum_slots(_exchange_chips(partials, "scatter_grads"), "sum_grads"), wire_shapes)
    g_shard = [_transpose_blocks(t, F32, "w_in_from_wire") if n == "w_in" else _FROM_WIRE[n](t, shp)
               for n, t, shp in zip(shard_names, g_wire, shard_shapes)]
    rep_shapes = [given[n].shape for n in _REPLICATED]
    g_rep = _unpack(_sum_slots(_all_gather(_pack([gw[n] for n in _REPLICATED], 8), "gather_small_grads"),
                               "sum_small_grads"), rep_shapes)
    grads = dict(zip(shard_names, g_shard))
    grads.update(zip(_REPLICATED, g_rep))

    def update(group, shapes, row_multiple, name):
        outs = _adamw(*[_pack([src[n] for n in group], row_multiple) for src in (
            given, grads, {n: given["m_" + n] for n in group}, {n: given["v_" + n] for n in group})], name)
        return [dict(zip(group, _unpack(o, shapes))) for o in outs]

    def update_shard(n):
        shp = given[n].shape
        outs = _adamw(*[t.reshape(-1, shp[-1]) for t in (given[n], grads[n], given["m_" + n], given["v_" + n])],
                      "adamw_" + n)
        return [o.reshape(shp) for o in outs]

    per_shard = [update_shard(n) for n in shard_names]
    big = [dict(zip(shard_names, [u[k] for u in per_shard])) for k in range(3)]
    small = update(list(_REPLICATED), rep_shapes, 8, "adamw_replicated")
    delta, new_m, new_v = [{**b, **s_} for b, s_ in zip(big, small)]
    return (loss, gx[None], *[grads[n] for n in names], *[delta[n] for n in names],
            *[new_m[n] for n in names], *[new_v[n] for n in names])
```
